```python
import math
import jax, jax.numpy as jnp
from jax import lax
import numpy as np

D_MODEL = 1024
BATCH = 2
SEQ = 8192
DEPTH = 1
DEC_BATCH = 128
DEC_SEQ = 1
PAST_LEN = 2048
PAGE_SIZE = 128

D_MIX = D_MODEL
HEAD_DIM = 64
D_A = D_MIX // 2
D_B = D_MIX - D_A
A_GROUPS = D_A // HEAD_DIM
CHUNK = 128
N_HEADS = D_B // HEAD_DIM
N_KV = 2
GQA = N_HEADS // N_KV
KV_W = N_KV * HEAD_DIM
ROT_DIM = HEAD_DIM // 4
ROPE_THETA = 500000.0
CMP_BLOCK = 32
CMP_STRIDE = 16
SEL_BLOCK = 64
N_SELECT = 16
WINDOW = 512
Q_BLOCK = 128
NORM_EPS = 1e-6
FORCE_SCORE = 1e4
NEG = -1e30
PROJ_SIZES = (D_A, D_A, D_A, D_B, KV_W, KV_W, KV_W, KV_W, KV_W, KV_W, 3 * N_HEADS, D_B)

kernel_name = 'hybrid_gmlp_nsa_decoder_step'


def rms_norm(x, g):
    xf = x.astype(jnp.float32)
    y = xf * lax.rsqrt(jnp.mean(xf * xf, -1, keepdims=True) + NORM_EPS)
    return (y * g.astype(jnp.float32)).astype(x.dtype)


def layer_norm(x, g, b):
    xf = x.astype(jnp.float32)
    xc = xf - jnp.mean(xf, -1, keepdims=True)
    y = xc * lax.rsqrt(jnp.mean(xc * xc, -1, keepdims=True) + NORM_EPS)
    return (y * g.astype(jnp.float32) + b.astype(jnp.float32)).astype(x.dtype)


def rope(x, pos):
    half = ROT_DIM // 2
    inv = jnp.power(jnp.float32(ROPE_THETA), -jnp.arange(half, dtype=jnp.float32) / half)
    ang = pos.astype(jnp.float32)[:, None] * inv[None, :]
    cos = jnp.cos(ang)[:, None, :]
    sin = jnp.sin(ang)[:, None, :]
    xr = x[..., :ROT_DIM].astype(jnp.float32)
    x1, x2 = xr[..., :half], xr[..., half:]
    rot = jnp.concatenate([x1 * cos - x2 * sin, x2 * cos + x1 * sin], -1).astype(x.dtype)
    return jnp.concatenate([rot, x[..., ROT_DIM:]], -1)


def split_projection(hn, w_in, pos):
    Bn, T = hn.shape[0], hn.shape[1]
    z = jnp.einsum('btd,de->bte', hn, w_in)
    offs = []
    acc = 0
    for s in PROJ_SIZES[:-1]:
        acc += s
        offs.append(acc)
    u, v, z_a, q, kc, vc, ks, vs, kw, vw, g, z_b = jnp.split(z, offs, axis=-1)
    heads = lambda a, n: a.reshape(Bn, T, n, HEAD_DIM)
    q = rope(heads(q, N_HEADS), pos)
    ks = rope(heads(ks, N_KV), pos)
    kw = rope(heads(kw, N_KV), pos)
    gates = jax.nn.sigmoid(g.astype(jnp.float32)).reshape(Bn, T, N_HEADS, 3).astype(hn.dtype)
    return (u, v, z_a, q, heads(kc, N_KV), heads(vc, N_KV), ks, heads(vs, N_KV),
            kw, heads(vw, N_KV), gates, z_b)


def spatial_gate(v, w_s, b_s):
    Bn, T = v.shape[0], v.shape[1]
    lc = min(CHUNK, T)
    n_chunk = -(-T // lc)
    pad = n_chunk * lc - T
    vp = jnp.pad(v, ((0, 0), (0, pad), (0, 0), (0, 0))).reshape(Bn, n_chunk, lc, A_GROUPS, HEAD_DIM)
    w = w_s[:, :lc, :lc] * jnp.tril(jnp.ones((lc, lc), w_s.dtype))
    s = jnp.einsum('gts,bnsgc->bntgc', w, vp) + b_s[:, :lc].T[None, None, :, :, None]
    return s.reshape(Bn, n_chunk * lc, A_GROUPS, HEAD_DIM)[:, :T]


def gmlp_branch(u, v, z_a, ln_g, ln_b, w_s, b_s):
    Bn, T = u.shape[0], u.shape[1]
    vn = layer_norm(v, ln_g, ln_b)
    s = spatial_gate(vn.reshape(Bn, T, A_GROUPS, HEAD_DIM), w_s, b_s).reshape(Bn, T, D_A)
    return u * s * jax.nn.silu(z_a), vn


def compress(k_raw, w_c, pe_c):
    L = k_raw.shape[1]
    n_cmp = (L - CMP_BLOCK) // CMP_STRIDE + 1
    starts = jnp.arange(n_cmp, dtype=jnp.int32) * CMP_STRIDE
    idx = starts[:, None] + jnp.arange(CMP_BLOCK, dtype=jnp.int32)[None, :]
    blocks = k_raw[:, idx] + pe_c[None, None, :, None, :]
    return jnp.einsum('bnlhd,lde->bnhe', blocks, w_c), starts


def to_blocks(k):
    Bn, L = k.shape[0], k.shape[1]
    n_blk = -(-L // SEL_BLOCK)
    kp = jnp.pad(k, ((0, 0), (0, n_blk * SEL_BLOCK - L), (0, 0), (0, 0)))
    return kp.reshape(Bn, n_blk, SEL_BLOCK, N_KV, HEAD_DIM).transpose(0, 3, 1, 2, 4)


def nsa_core(q, q_pos, gates, kc, vc, c_end, kb, vb, kw, vw, kw_pos):
    Bn, Tq = q.shape[0], q.shape[1]
    scale = HEAD_DIM ** -0.5
    qg = q.reshape(Bn, Tq, N_KV, GQA, HEAD_DIM)
    t = q_pos[:, None]
    m_c = (c_end[None, :] <= t)[None, :, None, None, :]
    s_c = jnp.einsum('bqhgd,bnhd->bqhgn', qg, kc).astype(jnp.float32) * scale
    p_c = jax.nn.softmax(jnp.where(m_c, s_c, NEG), -1) * m_c
    o_c = jnp.einsum('bqhgn,bnhd->bqhgd', p_c.astype(vc.dtype), vc)
    n_blk = kb.shape[2]
    cs = c_end - (CMP_BLOCK - 1)
    bs = jnp.arange(n_blk, dtype=jnp.int32) * SEL_BLOCK
    overlap = ((cs[:, None] < bs[None, :] + SEL_BLOCK) & (cs[:, None] + CMP_BLOCK > bs[None, :])).astype(jnp.float32)
    imp = jnp.einsum('bqhgn,nk->bqhk', p_c, overlap)
    blk = jnp.arange(n_blk, dtype=jnp.int32)[None, :]
    cur = t // SEL_BLOCK
    forced = ((blk == 0) | (blk == cur) | (blk == cur - 1))[None, :, None, :]
    valid = (bs[None, :] <= t)[None, :, None, :]
    imp = jnp.where(valid, jnp.where(forced, imp + FORCE_SCORE, imp), -jnp.inf)
    n_top = min(N_SELECT, n_blk)
    _, sel = lax.top_k(imp, n_top)
    sel_t = sel.transpose(0, 2, 1, 3)
    gather = jax.vmap(jax.vmap(lambda kk, ii: kk[ii]))
    kg = gather(kb, sel_t)
    vg = gather(vb, sel_t)
    tok = sel_t[..., None] * SEL_BLOCK + jnp.arange(SEL_BLOCK, dtype=jnp.int32)
    m_s = (tok <= q_pos[None, None, :, None, None]).transpose(0, 2, 1, 3, 4)[:, :, :, None]
    s_s = jnp.einsum('bqhgd,bhqksd->bqhgks', qg, kg).astype(jnp.float32) * scale
    p_s = jax.nn.softmax(jnp.where(m_s, s_s, NEG), axis=(-2, -1))
    o_s = jnp.einsum('bqhgks,bhqksd->bqhgd', p_s.astype(vg.dtype), vg)
    kp = kw_pos[None, :]
    m_w = ((kp <= t) & (kp > t - WINDOW) & (kp >= 0))[None, :, None, None, :]
    s_w = jnp.einsum('bqhgd,blhd->bqhgl', qg, kw).astype(jnp.float32) * scale
    p_w = jax.nn.softmax(jnp.where(m_w, s_w, NEG), -1)
    o_w = jnp.einsum('bqhgl,blhd->bqhgd', p_w.astype(vw.dtype), vw)
    gq = gates.reshape(Bn, Tq, N_KV, GQA, 3)
    o = gq[..., 0:1] * o_c + gq[..., 1:2] * o_s + gq[..., 2:3] * o_w
    return o.reshape(Bn, Tq, N_HEADS, HEAD_DIM)


def merge_out(x, a_out, o_b, z_b, w_out):
    Bn, T = x.shape[0], x.shape[1]
    mix = jnp.concatenate([a_out, o_b.reshape(Bn, T, D_B) * jax.nn.silu(z_b)], -1)
    return x + jnp.einsum('bte,ed->btd', mix, w_out)


def prompt_layer(x, norm_g, w_in, ln_v_g, ln_v_b, w_s, b_s, w_ck, pe_ck, w_cv, pe_cv, w_out):
    Bn, T = x.shape[0], x.shape[1]
    pos = jnp.arange(T, dtype=jnp.int32)
    hn = rms_norm(x, norm_g)
    u, v, z_a, q, kc, vc, ks, vs, kw, vw, gates, z_b = split_projection(hn, w_in, pos)
    a_out, _ = gmlp_branch(u, v, z_a, ln_v_g, ln_v_b, w_s, b_s)
    kcc, starts = compress(kc, w_ck, pe_ck)
    kcc = rope(kcc, starts)
    vcc, _ = compress(vc, w_cv, pe_cv)
    c_end = starts + CMP_BLOCK - 1
    kb, vb = to_blocks(ks), to_blocks(vs)
    padw = ((0, 0), (WINDOW, 0), (0, 0), (0, 0))
    kw_pad, vw_pad = jnp.pad(kw, padw), jnp.pad(vw, padw)
    n_qb = T // Q_BLOCK
    q_blocks = jnp.moveaxis(q.reshape(Bn, n_qb, Q_BLOCK, N_HEADS, HEAD_DIM), 1, 0)
    g_blocks = jnp.moveaxis(gates.reshape(Bn, n_qb, Q_BLOCK, N_HEADS, 3), 1, 0)
    q0s = jnp.arange(n_qb, dtype=jnp.int32) * Q_BLOCK

    def block_fn(args):
        qb, gb, q0 = args
        qpos = q0 + jnp.arange(Q_BLOCK, dtype=jnp.int32)
        kwb = lax.dynamic_slice_in_dim(kw_pad, q0, WINDOW + Q_BLOCK, axis=1)
        vwb = lax.dynamic_slice_in_dim(vw_pad, q0, WINDOW + Q_BLOCK, axis=1)
        kwpos = q0 - WINDOW + jnp.arange(WINDOW + Q_BLOCK, dtype=jnp.int32)
        return nsa_core(qb, qpos, gb, kcc, vcc, c_end, kb, vb, kwb, vwb, kwpos)

    o_b = lax.map(block_fn, (q_blocks, g_blocks, q0s))
    o_b = jnp.moveaxis(o_b, 0, 1).reshape(Bn, T, N_HEADS, HEAD_DIM)
    y = merge_out(x, a_out, o_b, z_b, w_out)
    keep = min(WINDOW, T)
    return y, (kc, vc, ks, vs, kw[:, T - keep:], vw[:, T - keep:])


def sample_layer(x, ck_cmp, cv_cmp, ck_sel, cv_sel, ck_win, cv_win, page_table,
                 norm_g, w_in, ln_v_g, ln_v_b, w_s, b_s, w_ck, pe_ck, w_cv, pe_cv, w_out):
    Bn, T = x.shape[0], x.shape[1]
    past = page_table.shape[1] * PAGE_SIZE
    pos = past + jnp.arange(T, dtype=jnp.int32)
    hn = rms_norm(x, norm_g)
    u, v, z_a, q, kc, vc, ks, vs, kw, vw, gates, z_b = split_projection(hn, w_in, pos)
    a_out, vn = gmlp_branch(u, v, z_a, ln_v_g, ln_v_b, w_s, b_s)
    pages = lambda c: c[page_table].reshape(Bn, past, N_KV, HEAD_DIM)
    kc_full = jnp.concatenate([pages(ck_cmp), kc], 1)
    vc_full = jnp.concatenate([pages(cv_cmp), vc], 1)
    ks_full = jnp.concatenate([pages(ck_sel), ks], 1)
    vs_full = jnp.concatenate([pages(cv_sel), vs], 1)
    kcc, starts = compress(kc_full, w_ck, pe_ck)
    kcc = rope(kcc, starts)
    vcc, _ = compress(vc_full, w_cv, pe_cv)
    c_end = starts + CMP_BLOCK - 1
    kb, vb = to_blocks(ks_full), to_blocks(vs_full)
    keep = ck_win.shape[1]
    kw_all = jnp.concatenate([ck_win, kw], 1)
    vw_all = jnp.concatenate([cv_win, vw], 1)
    kwpos = past - keep + jnp.arange(keep + T, dtype=jnp.int32)
    o_b = nsa_core(q, pos, gates, kcc, vcc, c_end, kb, vb, kw_all, vw_all, kwpos)
    y = merge_out(x, a_out, o_b, z_b, w_out)
    return y, (kc, vc, ks, vs, kw_all[:, T:], vw_all[:, T:], vn)


def setup_inputs(seed: int = 0) -> dict:
    key = jax.random.key(seed)
    ks = jax.random.split(key, 24)
    n_pages = PAST_LEN // PAGE_SIZE
    n_used = DEC_BATCH * n_pages
    n_pool = n_used + n_used // 4
    win_keep = min(WINDOW, PAST_LEN)
    d_in = sum(PROJ_SIZES)
    nrm = lambda k, shape, s: jax.random.normal(k, shape, jnp.float32) * s
    page_table = jax.random.permutation(ks[0], n_pool)[:n_used].reshape(DEC_BATCH, n_pages).astype(jnp.int32)
    paged = (DEPTH, n_pool, PAGE_SIZE, N_KV, HEAD_DIM)
    win = (DEPTH, DEC_BATCH, win_keep, N_KV, HEAD_DIM)
    return {
        'x_prompt': nrm(ks[1], (BATCH, SEQ, D_MODEL), 1.0),
        'x_sample': nrm(ks[2], (DEC_BATCH, DEC_SEQ, D_MODEL), 1.0),
        'cache_k_cmp': nrm(ks[3], paged, 1.0),
        'cache_v_cmp': nrm(ks[4], paged, 1.0),
        'cache_k_sel': nrm(ks[5], paged, 1.0),
        'cache_v_sel': nrm(ks[6], paged, 1.0),
        'cache_k_win': nrm(ks[7], win, 1.0),
        'cache_v_win': nrm(ks[8], win, 1.0),
        'page_table': page_table,
        'norm_g': 1.0 + nrm(ks[9], (DEPTH, D_MODEL), 0.02),
        'w_in': nrm(ks[10], (DEPTH, D_MODEL, d_in), D_MODEL ** -0.5),
        'ln_v_g': 1.0 + nrm(ks[11], (DEPTH, D_A), 0.02),
        'ln_v_b': nrm(ks[12], (DEPTH, D_A), 0.02),
        'w_s': nrm(ks[13], (DEPTH, A_GROUPS, CHUNK, CHUNK), CHUNK ** -0.5),
        'b_s': 1.0 + nrm(ks[14], (DEPTH, A_GROUPS, CHUNK), 0.02),
        'w_ck': nrm(ks[15], (DEPTH, CMP_BLOCK, HEAD_DIM, HEAD_DIM), (CMP_BLOCK * HEAD_DIM) ** -0.5),
        'pe_ck': nrm(ks[16], (DEPTH, CMP_BLOCK, HEAD_DIM), 0.02),
        'w_cv': nrm(ks[17], (DEPTH, CMP_BLOCK, HEAD_DIM, HEAD_DIM), (CMP_BLOCK * HEAD_DIM) ** -0.5),
        'pe_cv': nrm(ks[18], (DEPTH, CMP_BLOCK, HEAD_DIM), 0.02),
        'w_out': nrm(ks[19], (DEPTH, D_MIX, D_MODEL), D_MIX ** -0.5),
        'final_g': 1.0 + nrm(ks[20], (D_MODEL,), 0.02),
    }


def reference(x_prompt, x_sample, cache_k_cmp, cache_v_cmp, cache_k_sel, cache_v_sel,
              cache_k_win, cache_v_win, page_table, norm_g, w_in, ln_v_g, ln_v_b, w_s, b_s,
              w_ck, pe_ck, w_cv, pe_cv, w_out, final_g):
    hp, hs = x_prompt, x_sample
    sp = [[] for _ in range(6)]
    ss = [[] for _ in range(7)]
    for l in range(DEPTH):
        w = (norm_g[l], w_in[l], ln_v_g[l], ln_v_b[l], w_s[l], b_s[l],
             w_ck[l], pe_ck[l], w_cv[l], pe_cv[l], w_out[l])
        hp, st_p = prompt_layer(hp, *w)
        hs, st_s = sample_layer(hs, cache_k_cmp[l], cache_v_cmp[l], cache_k_sel[l], cache_v_sel[l],
                                cache_k_win[l], cache_v_win[l], page_table, *w)
        for i in range(6):
            sp[i].append(st_p[i])
        for i in range(7):
            ss[i].append(st_s[i])
    y_prompt = rms_norm(hp, final_g)
    y_sample = rms_norm(hs, final_g)
    sp = [jnp.stack(a, 0) for a in sp]
    ss = [jnp.stack(a, 0) for a in ss]
    return (y_prompt, y_sample, sp[0], sp[1], sp[2], sp[3], sp[4], sp[5],
            ss[0], ss[1], ss[2], ss[3], ss[4], ss[5], ss[6])
```

```python
import functools

import jax
import jax.numpy as jnp
from jax import lax
from jax.experimental import pallas as pl
from jax.experimental.pallas import tpu as pltpu

F32 = jnp.float32
BF16 = jnp.bfloat16

D_MODEL = 1024
HEAD_DIM = 64
D_A = 512
D_B = 512
A_GROUPS = 8
CHUNK = 128
N_HEADS = 8
N_KV = 2
GQA = 4
KV_W = 128
ROT_DIM = 16
ROPE_THETA = 500000.0
CMP_BLOCK = 32
CMP_STRIDE = 16
SEL_BLOCK = 64
SEL_SHIFT = 6
N_SELECT = 16
WINDOW = 512
Q_BLOCK = 128
PAGE_SIZE = 128
NORM_EPS = 1e-6
FORCE_SCORE = 1e4
NEG = -1e30

LANES = 128
VMEM_LIMIT_BYTES = 56 * 1024 * 1024

C_U, C_V, C_ZA, C_Q = 0, 512, 1024, 1536
C_KC, C_VC, C_KS, C_VS, C_KW, C_VW = 2048, 2176, 2304, 2432, 2560, 2688
C_G, C_ZB, C_END = 2816, 2944, 3456
N_GATE_COLS = 3 * N_HEADS

SEL_TILE = 512
WIN_SPAN = WINDOW + Q_BLOCK

_NT = (((1,), (1,)), ((), ()))


def _dot(a, b):
    return jnp.dot(a, b, preferred_element_type=F32)


def _dot_nt(a, b):
    return lax.dot_general(a, b, _NT, preferred_element_type=F32)


def _lane_iota(shape):
    return lax.broadcasted_iota(jnp.int32, shape, len(shape) - 1)


def _row_iota(shape):
    return lax.broadcasted_iota(jnp.int32, shape, len(shape) - 2)


def _rope(x, tab):
    c = tab[:, 0:LANES]
    s1 = tab[:, LANES:2 * LANES]
    s2 = tab[:, 2 * LANES:3 * LANES]
    return x * c + pltpu.roll(x, LANES - ROT_DIM // 2, 1) * s1 + pltpu.roll(x, ROT_DIM // 2, 1) * s2


def _rms_rows(x, g):
    ms = jnp.mean(x * x, axis=-1, keepdims=True)
    return x * lax.rsqrt(ms + NORM_EPS) * g


def _layer_norm_rows(v, g, b):
    mu = jnp.mean(v, axis=-1, keepdims=True)
    vc = v - mu
    var = jnp.mean(vc * vc, axis=-1, keepdims=True)
    return vc * lax.rsqrt(var + NORM_EPS) * g + b


def _head_slabs(x512):
    out = []
    rows = x512.shape[0]
    lane = _lane_iota((rows, LANES))
    lo = lane < HEAD_DIM
    for j in range(4):
        slab = x512[:, j * LANES:(j + 1) * LANES]
        swapped = pltpu.roll(slab, HEAD_DIM, 1)
        if j < 2:
            out.append(jnp.where(lo, slab, 0.0))
            out.append(jnp.where(lo, swapped, 0.0))
        else:
            out.append(jnp.where(lo, 0.0, swapped))
            out.append(jnp.where(lo, 0.0, slab))
    return out


def _project(x_ref, g_ref, w_ref, z_scr):
    hn = _rms_rows(x_ref[...], g_ref[...])
    z_scr[...] = _dot(hn.astype(BF16), w_ref[...])


def _inproj_prompt_kernel(x_ref, g_ref, w_ref, lng_ref, lnb_ref, ws_ref, bias_ref, rope_ref,
                          aout_ref, qa_ref, kc_ref, vc_ref, ks_ref, vs_ref, kw_ref, vw_ref,
                          ksa_ref, vsb_ref, kwb_ref, vwb_ref, gate_ref, szb_ref,
                          z_scr, *, tm, tiles_per_batch):
    _project(x_ref, g_ref, w_ref, z_scr)
    tab = rope_ref[...]

    lane = _lane_iota((CHUNK, LANES))
    lo = lane < HEAD_DIM
    for c in range(tm // CHUNK):
        r0 = c * CHUNK
        u = z_scr[r0:r0 + CHUNK, C_U:C_U + D_A]
        v = z_scr[r0:r0 + CHUNK, C_V:C_V + D_A]
        za = z_scr[r0:r0 + CHUNK, C_ZA:C_ZA + D_A]
        vn = _layer_norm_rows(v, lng_ref[...], lnb_ref[...])
        parts = []
        for p in range(A_GROUPS // 2):
            vp = vn[:, p * LANES:(p + 1) * LANES]
            v_lo = jnp.where(lo, vp, 0.0).astype(BF16)
            v_hi = jnp.where(lo, 0.0, vp).astype(BF16)
            parts.append(_dot(ws_ref[2 * p], v_lo) + _dot(ws_ref[2 * p + 1], v_hi))
        s = jnp.concatenate(parts, axis=1) + bias_ref[...]
        aout_ref[r0:r0 + CHUNK, :] = (u * s * jax.nn.silu(za)).astype(BF16)

    q = z_scr[:, C_Q:C_Q + D_B]
    qr = jnp.concatenate(
        [_rope(q[:, j * LANES:(j + 1) * LANES], tab) for j in range(4)], axis=1) * (HEAD_DIM ** -0.5)
    for e, slab in enumerate(_head_slabs(qr)):
        qa_ref[:, e * LANES:(e + 1) * LANES] = slab.astype(BF16)
    kc_ref[...] = z_scr[:, C_KC:C_KC + KV_W]
    vc_ref[...] = z_scr[:, C_VC:C_VC + KV_W]
    ks = _rope(z_scr[:, C_KS:C_KS + KV_W], tab)
    ks_ref[...] = ks
    vs = z_scr[:, C_VS:C_VS + KV_W]
    vs_ref[...] = vs
    kw = _rope(z_scr[:, C_KW:C_KW + KV_W], tab)
    kw_ref[...] = kw
    vw = z_scr[:, C_VW:C_VW + KV_W]
    vw_ref[...] = vw
    pos = (pl.program_id(0) % tiles_per_batch) * tm + _row_iota((tm, LANES))
    onehot = jnp.where((pos >> SEL_SHIFT) == _lane_iota((tm, LANES)), 1.0, 0.0)
    ksa_ref[:, 0:LANES] = ks.astype(BF16)
    ksa_ref[:, LANES:2 * LANES] = onehot.astype(BF16)
    vsb_ref[...] = vs.astype(BF16)
    kwb_ref[...] = kw.astype(BF16)
    vwb_ref[...] = vw.astype(BF16)
    gate_ref[...] = jax.nn.sigmoid(z_scr[:, C_G:C_G + LANES])
    szb_ref[...] = jax.nn.silu(z_scr[:, C_ZB:C_ZB + D_B])


def _inproj_prompt(x2, g, w, lng, lnb, ws_bf, bias_full, rope_tab, seq, tm=256):
    n = x2.shape[0]
    tiles_per_batch = seq // tm
    row = lambda i: (i, 0)
    const2 = lambda i: (0, 0)
    out_shapes = [
        jax.ShapeDtypeStruct((n, D_A), BF16),
        jax.ShapeDtypeStruct((n, N_HEADS * LANES), BF16),
        jax.ShapeDtypeStruct((n, KV_W), F32),
        jax.ShapeDtypeStruct((n, KV_W), F32),
        jax.ShapeDtypeStruct((n, KV_W), F32),
        jax.ShapeDtypeStruct((n, KV_W), F32),
        jax.ShapeDtypeStruct((n, KV_W), F32),
        jax.ShapeDtypeStruct((n, KV_W), F32),
        jax.ShapeDtypeStruct((n, 2 * LANES), BF16),
        jax.ShapeDtypeStruct((n, KV_W), BF16),
        jax.ShapeDtypeStruct((n, KV_W), BF16),
        jax.ShapeDtypeStruct((n, KV_W), BF16),
        jax.ShapeDtypeStruct((n, LANES), F32),
        jax.ShapeDtypeStruct((n, D_B), F32),
    ]
    out_specs = [pl.BlockSpec((tm, s.shape[1]), row) for s in out_shapes]
    in_specs = [
        pl.BlockSpec((tm, D_MODEL), row),
        pl.BlockSpec((1, D_MODEL), const2),
        pl.BlockSpec((D_MODEL, C_END), const2),
        pl.BlockSpec((1, D_A), const2),
        pl.BlockSpec((1, D_A), const2),
        pl.BlockSpec((A_GROUPS, CHUNK, CHUNK), lambda i: (0, 0, 0)),
        pl.BlockSpec((CHUNK, D_A), const2),
        pl.BlockSpec((tm, 3 * LANES), lambda i: (i % tiles_per_batch, 0)),
    ]
    return pl.pallas_call(
        functools.partial(_inproj_prompt_kernel, tm=tm, tiles_per_batch=tiles_per_batch),
        grid=(n // tm,),
        in_specs=in_specs,
        out_specs=out_specs,
        out_shape=out_shapes,
        scratch_shapes=[pltpu.VMEM((tm, C_END), F32)],
        compiler_params=pltpu.CompilerParams(
            dimension_semantics=("arbitrary",), vmem_limit_bytes=VMEM_LIMIT_BYTES),
        name="inproj_prompt",
    )(x2, g, w, lng, lnb, ws_bf, bias_full, rope_tab)


def _inproj_decode_kernel(x_ref, g_ref, w_ref, lng_ref, lnb_ref, wsv_ref, bsv_ref, rope_ref,
                          aout_ref, qa_ref, kc_ref, vc_ref, ks_ref, vs_ref, kw_ref, vw_ref,
                          gate_ref, szb_ref, vn_ref, z_scr):
    _project(x_ref, g_ref, w_ref, z_scr)
    tab = rope_ref[...]
    u = z_scr[:, C_U:C_U + D_A]
    v = z_scr[:, C_V:C_V + D_A]
    za = z_scr[:, C_ZA:C_ZA + D_A]
    vn = _layer_norm_rows(v, lng_ref[...], lnb_ref[...])
    vn_ref[...] = vn
    s = vn * wsv_ref[...] + bsv_ref[...]
    aout_ref[...] = (u * s * jax.nn.silu(za)).astype(BF16)

    q = z_scr[:, C_Q:C_Q + D_B]
    qr = jnp.concatenate(
        [_rope(q[:, j * LANES:(j + 1) * LANES], tab) for j in range(4)], axis=1) * (HEAD_DIM ** -0.5)
    for e, slab in enumerate(_head_slabs(qr)):
        qa_ref[:, e * LANES:(e + 1) * LANES] = slab.astype(BF16)
    kc_ref[...] = z_scr[:, C_KC:C_KC + KV_W]
    vc_ref[...] = z_scr[:, C_VC:C_VC + KV_W]
    ks_ref[...] = _rope(z_scr[:, C_KS:C_KS + KV_W], tab)
    vs_ref[...] = z_scr[:, C_VS:C_VS + KV_W]
    kw_ref[...] = _rope(z_scr[:, C_KW:C_KW + KV_W], tab)
    vw_ref[...] = z_scr[:, C_VW:C_VW + KV_W]
    gate_ref[...] = jax.nn.sigmoid(z_scr[:, C_G:C_G + LANES])
    szb = jax.nn.silu(z_scr[:, C_ZB:C_ZB + D_B])
    for e, slab in enumerate(_head_slabs(szb)):
        szb_ref[:, e * LANES:(e + 1) * LANES] = slab


def _inproj_decode(x2, g, w, lng, lnb, wsv, bsv, rope_tab):
    n = x2.shape[0]
    out_shapes = [
        jax.ShapeDtypeStruct((n, D_A), BF16),
        jax.ShapeDtypeStruct((n, N_HEADS * LANES), BF16),
        jax.ShapeDtypeStruct((n, KV_W), F32),
        jax.ShapeDtypeStruct((n, KV_W), F32),
        jax.ShapeDtypeStruct((n, KV_W), F32),
        jax.ShapeDtypeStruct((n, KV_W), F32),
        jax.ShapeDtypeStruct((n, KV_W), F32),
        jax.ShapeDtypeStruct((n, KV_W), F32),
        jax.ShapeDtypeStruct((n, LANES), F32),
        jax.ShapeDtypeStruct((n, N_HEADS * LANES), F32),
        jax.ShapeDtypeStruct((n, D_A), F32),
    ]
    return pl.pallas_call(
        _inproj_decode_kernel,
        out_shape=out_shapes,
        scratch_shapes=[pltpu.VMEM((n, C_END), F32)],
        compiler_params=pltpu.CompilerParams(vmem_limit_bytes=VMEM_LIMIT_BYTES),
        name="inproj_decode",
    )(x2, g, w, lng, lnb, wsv, bsv, rope_tab)


def _compress_chunks(ch, pe_ref, w_ref):
    a = _dot((ch + pe_ref[0:1, :]).astype(BF16), w_ref[:, 0:KV_W])
    b = _dot((ch + pe_ref[1:2, :]).astype(BF16), w_ref[:, KV_W:2 * KV_W])
    return a + pltpu.roll(b, ch.shape[0] - 1, 0)


def _compress_prompt_kernel(kc_ref, vc_ref, wk_ref, wv_ref, pek_ref, pev_ref, rope_ref, kcc_ref, vcc_ref):
    kcc = _compress_chunks(kc_ref[0], pek_ref, wk_ref)
    kcc_ref[0] = _rope(kcc, rope_ref[...]).astype(BF16)
    vcc_ref[0] = _compress_chunks(vc_ref[0], pev_ref, wv_ref).astype(BF16)


def _compress_prompt(kc3, vc3, wk, wv, pek, pev, rope_tab):
    b, c, width = kc3.shape
    blk = pl.BlockSpec((1, c, width), lambda i: (i, 0, 0))
    const2 = lambda i: (0, 0)
    return pl.pallas_call(
        _compress_prompt_kernel,
        grid=(b,),
        in_specs=[blk, blk,
                  pl.BlockSpec(wk.shape, const2), pl.BlockSpec(wv.shape, const2),
                  pl.BlockSpec(pek.shape, const2), pl.BlockSpec(pev.shape, const2),
                  pl.BlockSpec(rope_tab.shape, const2)],
        out_specs=[pl.BlockSpec((1, c, KV_W), lambda i: (i, 0, 0))] * 2,
        out_shape=[jax.ShapeDtypeStruct((b, c, KV_W), BF16)] * 2,
        compiler_params=pltpu.CompilerParams(
            dimension_semantics=("arbitrary",), vmem_limit_bytes=VMEM_LIMIT_BYTES),
        name="compress_prompt",
    )(kc3, vc3, wk, wv, pek, pev, rope_tab)


def _select_blocks(score, kk):
    picked = jnp.zeros(score.shape, F32)
    for _ in range(N_SELECT):
        cm = jnp.max(score, axis=0, keepdims=True)
        first = jnp.min(jnp.where(score == cm, kk, LANES), axis=0, keepdims=True)
        hit = kk == first
        picked = jnp.where(hit, 1.0, picked)
        score = jnp.where(hit, -jnp.inf, score)
    return picked


def _softmax_rows(s):
    mx = jnp.max(s, axis=1, keepdims=True)
    ex = jnp.exp(s - mx)
    return ex * (1.0 / jnp.sum(ex, axis=1, keepdims=True))


def _attn_prompt_kernel(qa_ref, gate_ref, szb_ref, kcc_ref, vcc_ref, ovl_ref,
                        ksa_ref, vsb_ref, kwb_ref, vwb_ref, out_ref,
                        qaug_scr, m_scr, l_scr, acc_scr):
    rows = N_HEADS * Q_BLOCK
    q0 = pl.program_id(1) * Q_BLOCK
    for e in range(N_HEADS):
        qaug_scr[e * Q_BLOCK:(e + 1) * Q_BLOCK, 0:LANES] = qa_ref[0, :, e * LANES:(e + 1) * LANES]
    qall = qaug_scr[:, 0:LANES]

    n_cmp = kcc_ref.shape[1]
    s = _dot_nt(qall, kcc_ref[0])
    t_c = q0 + (_row_iota((rows, n_cmp)) & (Q_BLOCK - 1))
    seen = (_lane_iota((rows, n_cmp)) * CMP_STRIDE + (CMP_BLOCK - 1)) <= t_c
    p_c = jnp.where(seen, _softmax_rows(jnp.where(seen, s, NEG)), 0.0)
    o_c = _dot(p_c.astype(BF16), vcc_ref[0])

    kk = _row_iota((LANES, Q_BLOCK))
    tq = q0 + _lane_iota((LANES, Q_BLOCK))
    valid = kk * SEL_BLOCK <= tq
    cur = tq >> SEL_SHIFT
    forced = (kk == 0) | (kk == cur) | (kk == cur - 1)
    scores = []
    for h in range(N_KV):
        base = h * GQA * Q_BLOCK
        p_sum = (p_c[base:base + Q_BLOCK] + p_c[base + Q_BLOCK:base + 2 * Q_BLOCK]
                 + p_c[base + 2 * Q_BLOCK:base + 3 * Q_BLOCK] + p_c[base + 3 * Q_BLOCK:base + 4 * Q_BLOCK])
        imp_t = _dot(p_sum.astype(BF16), ovl_ref[...]).T
        scores.append(jnp.where(valid, jnp.where(forced, imp_t + FORCE_SCORE, imp_t), -jnp.inf))
    picked = _select_blocks(jnp.concatenate(scores, axis=1), jnp.concatenate([kk, kk], axis=1))
    for h in range(N_KV):
        pk = picked[:, h * Q_BLOCK:(h + 1) * Q_BLOCK]
        selneg = jnp.where(valid, jnp.where(pk > 0.0, 0.0, NEG), NEG).T.astype(BF16)
        for g in range(GQA):
            r0 = (h * GQA + g) * Q_BLOCK
            qaug_scr[r0:r0 + Q_BLOCK, LANES:2 * LANES] = selneg

    m_scr[...] = jnp.full(m_scr.shape, -jnp.inf, F32)
    l_scr[...] = jnp.zeros(l_scr.shape, F32)
    acc_scr[...] = jnp.zeros(acc_scr.shape, F32)

    def sel_step(t0, causal):
        kt = ksa_ref[0, pl.ds(t0, SEL_TILE), :]
        vt = vsb_ref[0, pl.ds(t0, SEL_TILE), :]
        sc = _dot_nt(qaug_scr[...], kt)
        if causal:
            tok = t0 + _lane_iota((rows, SEL_TILE))
            tt = q0 + (_row_iota((rows, SEL_TILE)) & (Q_BLOCK - 1))
            sc = jnp.where(tok <= tt, sc, NEG)
        m_prev = m_scr[...]
        m_next = jnp.maximum(m_prev, jnp.max(sc, axis=1, keepdims=True))
        p = jnp.exp(sc - jnp.concatenate([m_next] * (SEL_TILE // LANES), axis=1))
        alpha = jnp.exp(m_prev - m_next)
        l_scr[...] = alpha * l_scr[...] + jnp.sum(p, axis=1, keepdims=True)
        acc_scr[...] = alpha * acc_scr[...] + _dot(p.astype(BF16), vt)
        m_scr[...] = m_next

    n_full = q0 // SEL_TILE

    def full_body(j, carry):
        sel_step(pl.multiple_of(j * SEL_TILE, SEL_TILE), False)
        return carry

    lax.fori_loop(0, n_full, full_body, 0)
    sel_step(pl.multiple_of(n_full * SEL_TILE, SEL_TILE), True)

    w0 = pl.multiple_of(jnp.maximum(q0 - WINDOW, 0), Q_BLOCK)
    kwt = kwb_ref[0, pl.ds(w0, WIN_SPAN), :]
    vwt = vwb_ref[0, pl.ds(w0, WIN_SPAN), :]
    sw = _dot_nt(qall, kwt)
    kp = w0 + _lane_iota((rows, WIN_SPAN))
    tw = q0 + (_row_iota((rows, WIN_SPAN)) & (Q_BLOCK - 1))
    sw = jnp.where(kp <= tw, jnp.where(kp > tw - WINDOW, sw, NEG), NEG)
    o_w = _dot(_softmax_rows(sw).astype(BF16), vwt)

    gates = gate_ref[0]
    heads = []
    for e in range(N_HEADS):
        r0 = e * Q_BLOCK
        o_s = acc_scr[r0:r0 + Q_BLOCK, :] * (1.0 / l_scr[r0:r0 + Q_BLOCK, :])
        heads.append(gates[:, 3 * e:3 * e + 1] * o_c[r0:r0 + Q_BLOCK]
                     + gates[:, 3 * e + 1:3 * e + 2] * o_s
                     + gates[:, 3 * e + 2:3 * e + 3] * o_w[r0:r0 + Q_BLOCK])
    lo = _lane_iota((Q_BLOCK, LANES)) < HEAD_DIM
    for j in range(N_HEADS // 2):
        a, b = heads[2 * j], heads[2 * j + 1]
        if j < 2:
            slab = jnp.where(lo, a, pltpu.roll(b, HEAD_DIM, 1))
        else:
            slab = jnp.where(lo, pltpu.roll(a, HEAD_DIM, 1), b)
        out_ref[0, :, j * LANES:(j + 1) * LANES] = (
            slab * szb_ref[0, :, j * LANES:(j + 1) * LANES]).astype(BF16)


def _attn_prompt(qa3, gate3, szb3, kcc, vcc, ovl, ksa3, vsb3, kwb3, vwb3):
    b, t, _ = qa3.shape
    nq = t // Q_BLOCK
    rows = N_HEADS * Q_BLOCK
    qblk = lambda w: pl.BlockSpec((1, Q_BLOCK, w), lambda bi, i: (bi, i, 0))
    seq = lambda a: pl.BlockSpec((1,) + a.shape[1:], lambda bi, i: (bi, 0, 0))
    return pl.pallas_call(
        _attn_prompt_kernel,
        grid=(b, nq),
        in_specs=[qblk(N_HEADS * LANES), qblk(LANES), qblk(D_B),
                  seq(kcc), seq(vcc), pl.BlockSpec(ovl.shape, lambda bi, i: (0, 0)),
                  seq(ksa3), seq(vsb3), seq(kwb3), seq(vwb3)],
        out_specs=qblk(D_B),
        out_shape=jax.ShapeDtypeStruct((b, t, D_B), BF16),
        scratch_shapes=[pltpu.VMEM((rows, 2 * LANES), BF16),
                        pltpu.VMEM((rows, LANES), F32),
                        pltpu.VMEM((rows, LANES), F32),
                        pltpu.VMEM((rows, LANES), F32)],
        compiler_params=pltpu.CompilerParams(
            dimension_semantics=("arbitrary", "arbitrary"), vmem_limit_bytes=VMEM_LIMIT_BYTES),
        name="attn_prompt",
    )(qa3, gate3, szb3, kcc, vcc, ovl, ksa3, vsb3, kwb3, vwb3)


def _outproj_kernel(x_ref, a_ref, m_ref, wa_ref, wb_ref, fg_ref, y_ref):
    y = x_ref[...] + _dot(a_ref[...], wa_ref[...]) + _dot(m_ref[...].astype(BF16), wb_ref[...])
    y_ref[...] = _rms_rows(y, fg_ref[...])


def _outproj(x2, a_out, mix_b, wa, wb, fg, tm):
    n = x2.shape[0]
    row = lambda i: (i, 0)
    const2 = lambda i: (0, 0)
    return pl.pallas_call(
        _outproj_kernel,
        grid=(n // tm,),
        in_specs=[pl.BlockSpec((tm, D_MODEL), row),
                  pl.BlockSpec((tm, a_out.shape[1]), row),
                  pl.BlockSpec((tm, mix_b.shape[1]), row),
                  pl.BlockSpec(wa.shape, const2), pl.BlockSpec(wb.shape, const2),
                  pl.BlockSpec((1, D_MODEL), const2)],
        out_specs=pl.BlockSpec((tm, D_MODEL), row),
        out_shape=jax.ShapeDtypeStruct((n, D_MODEL), F32),
        compiler_params=pltpu.CompilerParams(
            dimension_semantics=("arbitrary",), vmem_limit_bytes=VMEM_LIMIT_BYTES),
        name="outproj",
    )(x2, a_out, mix_b, wa, wb, fg)


def _decode_cmp_kernel(pt_ref, *refs, n_pages):
    kpages = refs[0:n_pages]
    vpages = refs[n_pages:2 * n_pages]
    (q_ref, wk_ref, wv_ref, pek_ref, pev_ref, rope_ref, ovl_ref, oc_ref, psum_ref) = refs[2 * n_pages:]
    del pt_ref
    kch = jnp.concatenate([r[0] for r in kpages], axis=0)
    vch = jnp.concatenate([r[0] for r in vpages], axis=0)
    kcc = _rope(_compress_chunks(kch, pek_ref, wk_ref), rope_ref[...]).astype(BF16)
    vcc = _compress_chunks(vch, pev_ref, wv_ref).astype(BF16)
    n_chunks = kch.shape[0]
    q = q_ref[0]
    rows = q.shape[0]
    s = _dot_nt(q, kcc)
    seen = _lane_iota((rows, n_chunks)) < n_chunks - 1
    p_c = jnp.where(seen, _softmax_rows(jnp.where(seen, s, NEG)), 0.0)
    oc_ref[0] = _dot(p_c.astype(BF16), vcc)[0:N_HEADS]
    ps0 = jnp.sum(p_c[0:GQA], axis=0, keepdims=True)
    ps1 = jnp.sum(p_c[GQA:2 * GQA], axis=0, keepdims=True)
    psum = jnp.where(_row_iota((rows, n_chunks)) < N_HEADS // 2, ps0, ps1)
    psum_ref[0] = _dot(psum.astype(BF16), ovl_ref[...])[0:N_HEADS]


def _decode_cmp(page_table, ck_pool, cv_pool, q16, wk, wv, pek, pev, rope_tab, ovl):
    nb, n_pages = page_table.shape
    chunks_per_page = PAGE_SIZE // CMP_STRIDE
    width = CMP_STRIDE * KV_W
    page = lambda p: pl.BlockSpec((1, chunks_per_page, width), lambda b, pt, p=p: (pt[b, p], 0, 0))
    const2 = lambda b, pt: (0, 0)
    per_b = lambda a: pl.BlockSpec((1,) + a.shape[1:], lambda b, pt: (b, 0, 0))
    in_specs = ([page(p) for p in range(n_pages)] * 2
                + [per_b(q16),
                   pl.BlockSpec(wk.shape, const2), pl.BlockSpec(wv.shape, const2),
                   pl.BlockSpec(pek.shape, const2), pl.BlockSpec(pev.shape, const2),
                   pl.BlockSpec(rope_tab.shape, const2), pl.BlockSpec(ovl.shape, const2)])
    out_blk = pl.BlockSpec((1, N_HEADS, LANES), lambda b, pt: (b, 0, 0))
    grid_spec = pltpu.PrefetchScalarGridSpec(
        num_scalar_prefetch=1, grid=(nb,), in_specs=in_specs, out_specs=[out_blk, out_blk])
    return pl.pallas_call(
        functools.partial(_decode_cmp_kernel, n_pages=n_pages),
        grid_spec=grid_spec,
        out_shape=[jax.ShapeDtypeStruct((nb, N_HEADS, LANES), F32)] * 2,
        compiler_params=pltpu.CompilerParams(
            dimension_semantics=("arbitrary",), vmem_limit_bytes=VMEM_LIMIT_BYTES),
        name="decode_cmp",
    )(page_table, *([ck_pool] * n_pages), *([cv_pool] * n_pages),
      q16, wk, wv, pek, pev, rope_tab, ovl)


def _decode_topk_kernel(imp_ref, sel_ref, *, n_blk, cur):
    n = imp_ref.shape[0]
    tiles = [imp_ref[i * LANES:(i + 1) * LANES, :].T for i in range(n // LANES)]
    imp_t = jnp.concatenate(tiles, axis=1)
    kk = _row_iota(imp_t.shape)
    forced = (kk == 0) | (kk == cur) | (kk == cur - 1)
    exists = kk < n_blk
    score = jnp.where(exists, jnp.where(forced, imp_t + FORCE_SCORE, imp_t), -jnp.inf)
    picked = _select_blocks(score, kk)
    selneg = jnp.where(exists, jnp.where(picked > 0.0, 0.0, NEG), NEG)
    for i in range(n // LANES):
        sel_ref[i * LANES:(i + 1) * LANES, :] = selneg[:, i * LANES:(i + 1) * LANES].T


def _decode_topk(imp2, n_blk, cur):
    return pl.pallas_call(
        functools.partial(_decode_topk_kernel, n_blk=n_blk, cur=cur),
        out_shape=jax.ShapeDtypeStruct(imp2.shape, F32),
        compiler_params=pltpu.CompilerParams(vmem_limit_bytes=VMEM_LIMIT_BYTES),
        name="decode_topk",
    )(imp2)


def _decode_attend(s, q32, vals_bf, k_new, v_new, bias_new):
    kn = k_new.astype(BF16).astype(F32)
    s_new = jnp.sum(q32 * kn, axis=1, keepdims=True) + bias_new
    mx = jnp.maximum(jnp.max(s, axis=1, keepdims=True), s_new)
    ex = jnp.exp(s - mx)
    ex_new = jnp.exp(s_new - mx)
    inv = 1.0 / (jnp.sum(ex, axis=1, keepdims=True) + ex_new)
    p = (ex * inv).astype(BF16)
    p_new = (ex_new * inv).astype(BF16).astype(F32)
    return _dot(p, vals_bf) + p_new * v_new.astype(BF16).astype(F32)


def _decode_sel_kernel(pt_ref, *refs, n_pages, cur):
    kpages = refs[0:n_pages]
    vpages = refs[n_pages:2 * n_pages]
    (ckw_ref, cvw_ref, q_ref, sel_ref, ksn_ref, vsn_ref, kwn_ref, vwn_ref,
     gate_ref, oc_ref, szb_ref, onehot_ref, mix_ref, okw_ref, ovw_ref) = refs[2 * n_pages:]
    del pt_ref
    q = q_ref[0]
    q32 = q.astype(F32)
    rows = q.shape[0]
    sel = sel_ref[0]

    kb = jnp.concatenate([r[0] for r in kpages], axis=0).astype(BF16)
    vb = jnp.concatenate([r[0] for r in vpages], axis=0).astype(BF16)
    kaug = jnp.concatenate([kb, onehot_ref[...]], axis=1)
    qaug = jnp.concatenate([q, sel], axis=1)
    s_sel = _dot_nt(qaug, kaug)
    o_s = _decode_attend(s_sel, q32, vb, ksn_ref[0], vsn_ref[0], sel.astype(F32)[:, cur:cur + 1])

    kwin = ckw_ref[0]
    vwin = cvw_ref[0]
    keep = kwin.shape[0]
    visible = _lane_iota((rows, keep)) > keep - WINDOW
    s_win = jnp.where(visible, _dot_nt(q, kwin.astype(BF16)), NEG)
    o_w = _decode_attend(s_win, q32, vwin.astype(BF16), kwn_ref[0], vwn_ref[0], 0.0)

    g = gate_ref[0]
    o = g[:, 0:1] * oc_ref[0] + g[:, 1:2] * o_s[0:N_HEADS] + g[:, 2:3] * o_w[0:N_HEADS]
    mix_ref[0] = o * szb_ref[0]

    last = _row_iota((keep, KV_W)) == keep - 1
    okw_ref[0] = jnp.where(last, kwn_ref[0], pltpu.roll(kwin, keep - 1, 0))
    ovw_ref[0] = jnp.where(last, vwn_ref[0], pltpu.roll(vwin, keep - 1, 0))


def _decode_sel(page_table, ks_pool, vs_pool, ckw, cvw, q16, sel16, ksn, vsn, kwn, vwn,
                gate3, oc, szb3, onehot, cur):
    nb, n_pages = page_table.shape
    keep = ckw.shape[1]
    page = lambda p: pl.BlockSpec((1, PAGE_SIZE, KV_W), lambda b, pt, p=p: (pt[b, p], 0, 0))
    per_b = lambda a: pl.BlockSpec((1,) + a.shape[1:], lambda b, pt: (b, 0, 0))
    in_specs = ([page(p) for p in range(n_pages)] * 2
                + [per_b(a) for a in (ckw, cvw, q16, sel16, ksn, vsn, kwn, vwn, gate3, oc, szb3)]
                + [pl.BlockSpec(onehot.shape, lambda b, pt: (0, 0))])
    out_specs = [pl.BlockSpec((1, N_HEADS, LANES), lambda b, pt: (b, 0, 0)),
                 pl.BlockSpec((1, keep, KV_W), lambda b, pt: (b, 0, 0)),
                 pl.BlockSpec((1, keep, KV_W), lambda b, pt: (b, 0, 0))]
    grid_spec = pltpu.PrefetchScalarGridSpec(
        num_scalar_prefetch=1, grid=(nb,), in_specs=in_specs, out_specs=out_specs)
    return pl.pallas_call(
        functools.partial(_decode_sel_kernel, n_pages=n_pages, cur=cur),
        grid_spec=grid_spec,
        out_shape=[jax.ShapeDtypeStruct((nb, N_HEADS, LANES), F32),
                   jax.ShapeDtypeStruct((nb, keep, KV_W), F32),
                   jax.ShapeDtypeStruct((nb, keep, KV_W), F32)],
        compiler_params=pltpu.CompilerParams(
            dimension_semantics=("arbitrary",), vmem_limit_bytes=VMEM_LIMIT_BYTES),
        name="decode_sel",
    )(page_table, *([ks_pool] * n_pages), *([vs_pool] * n_pages),
      ckw, cvw, q16, sel16, ksn, vsn, kwn, vwn, gate3, oc, szb3, onehot)


def _rope_table(pos):
    half = ROT_DIM // 2
    inv = jnp.power(jnp.float32(ROPE_THETA), -jnp.arange(half, dtype=F32) / half)
    ang = pos.astype(F32)[:, None] * inv[None, :]
    cos, sin = jnp.cos(ang), jnp.sin(ang)
    n = pos.shape[0]
    zeros = lambda w: jnp.zeros((n, w), F32)
    c = jnp.concatenate([cos, cos, jnp.ones((n, HEAD_DIM - ROT_DIM), F32)], axis=1)
    s1 = jnp.concatenate([-sin, zeros(HEAD_DIM - half)], axis=1)
    s2 = jnp.concatenate([zeros(half), sin, zeros(HEAD_DIM - ROT_DIM)], axis=1)
    return jnp.concatenate([c, c, s1, s1, s2, s2], axis=1)


def _compress_weights(w_c, pe_c):
    w4 = w_c.reshape(2, CMP_STRIDE, HEAD_DIM, HEAD_DIM)
    wb = jnp.einsum('arde,hg->rhdage', w4, jnp.eye(N_KV, dtype=w_c.dtype))
    wb = wb.reshape(CMP_STRIDE * KV_W, 2 * KV_W).astype(BF16)
    pe = jnp.broadcast_to(pe_c.reshape(2, CMP_STRIDE, 1, HEAD_DIM), (2, CMP_STRIDE, N_KV, HEAD_DIM))
    return wb, pe.reshape(2, CMP_STRIDE * KV_W)


def _overlap_matrix(n_rows, n_blk):
    cs = jnp.arange(n_rows, dtype=jnp.int32)[:, None] * CMP_STRIDE
    bs = jnp.arange(LANES, dtype=jnp.int32)[None, :] * SEL_BLOCK
    hit = (cs < bs + SEL_BLOCK) & (cs + CMP_BLOCK > bs) & (jnp.arange(LANES)[None, :] < n_blk)
    return hit.astype(BF16)


def kernel(x_prompt, x_sample, cache_k_cmp, cache_v_cmp, cache_k_sel, cache_v_sel, cache_k_win,
           cache_v_win, page_table, norm_g, w_in, ln_v_g, ln_v_b, w_s, b_s, w_ck, pe_ck, w_cv,
           pe_cv, w_out, final_g):
    depth = norm_g.shape[0]
    assert depth == 1
    bsz, seq, _ = x_prompt.shape
    nb, dec_seq, _ = x_sample.shape
    assert dec_seq == 1 and seq % SEL_TILE == 0
    n_pages = page_table.shape[1]
    past = n_pages * PAGE_SIZE
    keep = cache_k_win.shape[2]
    assert keep == WINDOW and past % SEL_BLOCK == 0

    w = w_in[0]
    w_pad = jnp.concatenate(
        [w[:, :C_G + N_GATE_COLS], jnp.zeros((D_MODEL, LANES - N_GATE_COLS), w.dtype),
         w[:, C_G + N_GATE_COLS:]], axis=1).astype(BF16)
    g_row = norm_g[0][None, :]
    lng = ln_v_g[0][None, :]
    lnb = ln_v_b[0][None, :]
    fg = final_g[None, :]
    wk_c, pek = _compress_weights(w_ck[0], pe_ck[0])
    wv_c, pev = _compress_weights(w_cv[0], pe_cv[0])
    w_o = w_out[0].astype(BF16)
    wo_a, wo_b = w_o[:D_A], w_o[D_A:]

    tril = jnp.tril(jnp.ones((CHUNK, CHUNK), w_s.dtype))
    ws_bf = (w_s[0] * tril).astype(BF16)
    bias_full = jnp.repeat(b_s[0].T, HEAD_DIM, axis=1)
    xp2 = x_prompt.reshape(bsz * seq, D_MODEL)
    (a_out, qa, kc, vc, ks, vs, kw, vw, ksa, vsb, kwb, vwb, gates, szb) = _inproj_prompt(
        xp2, g_row, w_pad, lng, lnb, ws_bf, bias_full,
        _rope_table(jnp.arange(seq, dtype=jnp.int32)), seq)
    n_chunks = seq // CMP_STRIDE
    kcc, vcc = _compress_prompt(
        kc.reshape(bsz, n_chunks, CMP_STRIDE * KV_W), vc.reshape(bsz, n_chunks, CMP_STRIDE * KV_W),
        wk_c, wv_c, pek, pev,
        _rope_table(jnp.arange(n_chunks, dtype=jnp.int32) * CMP_STRIDE))
    b3 = lambda a: a.reshape(bsz, seq, a.shape[-1])
    mix_p = _attn_prompt(b3(qa), b3(gates), b3(szb), kcc, vcc,
                         _overlap_matrix(n_chunks, seq // SEL_BLOCK),
                         b3(ksa), b3(vsb), b3(kwb), b3(vwb))
    y_prompt = _outproj(xp2, a_out, mix_p.reshape(bsz * seq, D_B), wo_a, wo_b, fg, 512)
    y_prompt = y_prompt.reshape(bsz, seq, D_MODEL)
    st = lambda a: a.reshape(1, bsz, seq, N_KV, HEAD_DIM)
    keep_p = min(WINDOW, seq)

    xs2 = x_sample.reshape(nb, D_MODEL)
    wsv = jnp.repeat(w_s[0][:, 0, 0], HEAD_DIM)[None, :]
    bsv = jnp.repeat(b_s[0][:, 0], HEAD_DIM)[None, :]
    (a_s, qa_s, kc_s, vc_s, ks_s, vs_s, kw_s, vw_s, gates_s, szb_s, vn_s) = _inproj_decode(
        xs2, g_row, w_pad, lng, lnb, wsv, bsv,
        _rope_table(jnp.full((nb,), past, dtype=jnp.int32)))
    q16 = jnp.pad(qa_s.reshape(nb, N_HEADS, LANES), ((0, 0), (0, N_HEADS), (0, 0)))
    chunks_per_page = PAGE_SIZE // CMP_STRIDE
    n_pool = cache_k_cmp.shape[1]
    d_chunks = past // CMP_STRIDE
    n_blk = -(-(past + 1) // SEL_BLOCK)
    cur = past // SEL_BLOCK
    oc_s, imp_s = _decode_cmp(
        page_table,
        cache_k_cmp[0].reshape(n_pool, chunks_per_page, CMP_STRIDE * KV_W),
        cache_v_cmp[0].reshape(n_pool, chunks_per_page, CMP_STRIDE * KV_W),
        q16, wk_c, wv_c, pek, pev,
        _rope_table(jnp.arange(d_chunks, dtype=jnp.int32) * CMP_STRIDE),
        _overlap_matrix(d_chunks, n_blk))
    selneg = _decode_topk(imp_s[:, ::GQA, :].reshape(nb * N_KV, LANES), n_blk, cur)
    sel16 = jnp.pad(jnp.repeat(selneg.reshape(nb, N_KV, LANES), GQA, axis=1),
                    ((0, 0), (0, N_HEADS), (0, 0))).astype(BF16)
    tok_blk = jnp.arange(past, dtype=jnp.int32)[:, None] // SEL_BLOCK
    onehot = (tok_blk == jnp.arange(LANES, dtype=jnp.int32)[None, :]).astype(BF16)
    r3 = lambda a: a.reshape(nb, 1, a.shape[-1])
    mix_s, okw, ovw = _decode_sel(
        page_table,
        cache_k_sel[0].reshape(n_pool, PAGE_SIZE, KV_W), cache_v_sel[0].reshape(n_pool, PAGE_SIZE, KV_W),
        cache_k_win[0].reshape(nb, keep, KV_W), cache_v_win[0].reshape(nb, keep, KV_W),
        q16, sel16, r3(ks_s), r3(vs_s), r3(kw_s), r3(vw_s),
        gates_s[:, :N_GATE_COLS].reshape(nb, N_HEADS, 3), oc_s,
        szb_s.reshape(nb, N_HEADS, LANES), onehot, cur)
    wo_slab = jnp.zeros((N_HEADS, N_KV, HEAD_DIM, D_MODEL), BF16)
    wo_heads = wo_b.reshape(N_HEADS, HEAD_DIM, D_MODEL)
    for e in range(N_HEADS):
        wo_slab = wo_slab.at[e, e // GQA].set(wo_heads[e])
    y_sample = _outproj(xs2, a_s, mix_s.reshape(nb, N_HEADS * LANES), wo_a,
                        wo_slab.reshape(N_HEADS * LANES, D_MODEL), fg, nb)
    y_sample = y_sample.reshape(nb, 1, D_MODEL)
    ss = lambda a: a.reshape(1, nb, 1, N_KV, HEAD_DIM)

    return (y_prompt, y_sample,
            st(kc), st(vc), st(ks), st(vs),
            st(kw)[:, :, seq - keep_p:], st(vw)[:, :, seq - keep_p:],
            ss(kc_s), ss(vc_s), ss(ks_s), ss(vs_s),
            okw.reshape(1, nb, keep, N_KV, HEAD_DIM), ovw.reshape(1, nb, keep, N_KV, HEAD_DIM),
            vn_s.reshape(1, nb, 1, D_A))
```

```python
import functools

import jax
import jax.numpy as jnp
from jax import lax
from jax.experimental import pallas as pl
from jax.experimental.pallas import tpu as pltpu

F32 = jnp.float32
BF16 = jnp.bfloat16

D_MODEL = 1024
HEAD_DIM = 64
D_A = 512
D_B = 512
A_GROUPS = 8
CHUNK = 128
N_HEADS = 8
N_KV = 2
GQA = 4
KV_W = 128
ROT_DIM = 16
ROPE_THETA = 500000.0
CMP_BLOCK = 32
CMP_STRIDE = 16
SEL_BLOCK = 64
SEL_SHIFT = 6
N_SELECT = 16
WINDOW = 512
Q_BLOCK = 128
PAGE_SIZE = 128
NORM_EPS = 1e-6
FORCE_SCORE = 1e4
NEG = -1e30

LANES = 128
VMEM_LIMIT_BYTES = 56 * 1024 * 1024

C_U, C_V, C_ZA, C_Q = 0, 512, 1024, 1536
C_KC, C_VC, C_KS, C_VS, C_KW, C_VW = 2048, 2176, 2304, 2432, 2560, 2688
C_G, C_ZB, C_END = 2816, 2944, 3456
N_GATE_COLS = 3 * N_HEADS

SEL_TILE = 512
WIN_SPAN = WINDOW + Q_BLOCK

_NT = (((1,), (1,)), ((), ()))


def _dot(a, b):
    return jnp.dot(a, b, preferred_element_type=F32)


def _dot_nt(a, b):
    return lax.dot_general(a, b, _NT, preferred_element_type=F32)


def _lane_iota(shape):
    return lax.broadcasted_iota(jnp.int32, shape, len(shape) - 1)


def _row_iota(shape):
    return lax.broadcasted_iota(jnp.int32, shape, len(shape) - 2)


def _rope(x, tab):
    c = tab[:, 0:LANES]
    s1 = tab[:, LANES:2 * LANES]
    s2 = tab[:, 2 * LANES:3 * LANES]
    return x * c + pltpu.roll(x, LANES - ROT_DIM // 2, 1) * s1 + pltpu.roll(x, ROT_DIM // 2, 1) * s2


def _rms_rows(x, g):
    ms = jnp.mean(x * x, axis=-1, keepdims=True)
    return x * lax.rsqrt(ms + NORM_EPS) * g


def _layer_norm_rows(v, g, b):
    mu = jnp.mean(v, axis=-1, keepdims=True)
    vc = v - mu
    var = jnp.mean(vc * vc, axis=-1, keepdims=True)
    return vc * lax.rsqrt(var + NORM_EPS) * g + b


def _head_slabs(x512):
    out = []
    rows = x512.shape[0]
    lane = _lane_iota((rows, LANES))
    lo = lane < HEAD_DIM
    for j in range(4):
        slab = x512[:, j * LANES:(j + 1) * LANES]
        swapped = pltpu.roll(slab, HEAD_DIM, 1)
        if j < 2:
            out.append(jnp.where(lo, slab, 0.0))
            out.append(jnp.where(lo, swapped, 0.0))
        else:
            out.append(jnp.where(lo, 0.0, swapped))
            out.append(jnp.where(lo, 0.0, slab))
    return out


def _project(x_ref, g_ref, w_ref, z_scr):
    hn = _rms_rows(x_ref[...], g_ref[...])
    z_scr[...] = _dot(hn.astype(BF16), w_ref[...])


def _inproj_prompt_kernel(x_ref, g_ref, w_ref, lng_ref, lnb_ref, ws_ref, bias_ref, rope_ref,
                          aout_ref, qa_ref, kc_ref, vc_ref,
                          kct_ref, vct_ref, kst_ref, vst_ref, kwt_ref, vwt_ref,
                          ksa_ref, vsb_ref, kwb_ref, vwb_ref, gate_ref, szb_ref,
                          z_scr, *, tm, tiles_per_batch):
    _project(x_ref, g_ref, w_ref, z_scr)
    tab = rope_ref[...]

    lane = _lane_iota((CHUNK, LANES))
    lo = lane < HEAD_DIM
    for c in range(tm // CHUNK):
        r0 = c * CHUNK
        u = z_scr[r0:r0 + CHUNK, C_U:C_U + D_A]
        v = z_scr[r0:r0 + CHUNK, C_V:C_V + D_A]
        za = z_scr[r0:r0 + CHUNK, C_ZA:C_ZA + D_A]
        vn = _layer_norm_rows(v, lng_ref[...], lnb_ref[...])
        parts = []
        for p in range(A_GROUPS // 2):
            vp = vn[:, p * LANES:(p + 1) * LANES]
            v_lo = jnp.where(lo, vp, 0.0).astype(BF16)
            v_hi = jnp.where(lo, 0.0, vp).astype(BF16)
            parts.append(_dot(ws_ref[2 * p], v_lo) + _dot(ws_ref[2 * p + 1], v_hi))
        s = jnp.concatenate(parts, axis=1) + bias_ref[...]
        aout_ref[r0:r0 + CHUNK, :] = (u * s * jax.nn.silu(za)).astype(BF16)

    q = z_scr[:, C_Q:C_Q + D_B]
    qr = jnp.concatenate(
        [_rope(q[:, j * LANES:(j + 1) * LANES], tab) for j in range(4)], axis=1) * (HEAD_DIM ** -0.5)
    for e, slab in enumerate(_head_slabs(qr)):
        qa_ref[:, e * LANES:(e + 1) * LANES] = slab.astype(BF16)
    kc = z_scr[:, C_KC:C_KC + KV_W]
    vc = z_scr[:, C_VC:C_VC + KV_W]
    kc_ref[...] = kc
    vc_ref[...] = vc
    ks = _rope(z_scr[:, C_KS:C_KS + KV_W], tab)
    vs = z_scr[:, C_VS:C_VS + KV_W]
    kw = _rope(z_scr[:, C_KW:C_KW + KV_W], tab)
    vw = z_scr[:, C_VW:C_VW + KV_W]
    pos0 = (pl.program_id(0) % tiles_per_batch) * tm
    for j in range(tm // LANES):
        r0, r1 = j * LANES, (j + 1) * LANES
        kct_ref[0, :, r0:r1] = kc[r0:r1].T
        vct_ref[0, :, r0:r1] = vc[r0:r1].T
        ks_t, vs_t, kw_t, vw_t = ks[r0:r1].T, vs[r0:r1].T, kw[r0:r1].T, vw[r0:r1].T
        kst_ref[0, :, r0:r1] = ks_t
        vst_ref[0, :, r0:r1] = vs_t
        kwt_ref[0, :, r0:r1] = kw_t
        vwt_ref[0, :, r0:r1] = vw_t
        tok_blk = (pos0 + r0 + _lane_iota((LANES, LANES))) >> SEL_SHIFT
        onehot_t = jnp.where(_row_iota((LANES, LANES)) == tok_blk, 1.0, 0.0)
        ksa_ref[0, j, 0:LANES, :] = ks_t.astype(BF16)
        ksa_ref[0, j, LANES:2 * LANES, :] = onehot_t.astype(BF16)
        vsb_ref[0, j] = vs_t.astype(BF16)
        kwb_ref[0, j] = kw_t.astype(BF16)
        vwb_ref[0, j] = vw_t.astype(BF16)
    gate_ref[...] = jax.nn.sigmoid(z_scr[:, C_G:C_G + LANES])
    szb_ref[...] = jax.nn.silu(z_scr[:, C_ZB:C_ZB + D_B])


def _inproj_prompt(x2, g, w, lng, lnb, ws_bf, bias_full, rope_tab, seq, tm=256):
    n = x2.shape[0]
    tiles_per_batch = seq // tm
    bsz = n // seq
    n_tok_tiles = seq // LANES
    row = lambda i: (i, 0)
    const2 = lambda i: (0, 0)
    rows2d = lambda width, dtype: (jax.ShapeDtypeStruct((n, width), dtype), pl.BlockSpec((tm, width), row))
    kv_t = (jax.ShapeDtypeStruct((bsz, KV_W, seq), F32),
            pl.BlockSpec((1, KV_W, tm), lambda i: (i // tiles_per_batch, 0, i % tiles_per_batch)))
    tiles = lambda r: (jax.ShapeDtypeStruct((bsz, n_tok_tiles, r, LANES), BF16),
                       pl.BlockSpec((1, tm // LANES, r, LANES),
                                    lambda i: (i // tiles_per_batch, i % tiles_per_batch, 0, 0)))
    outs = [
        rows2d(D_A, BF16),
        rows2d(N_HEADS * LANES, BF16),
        rows2d(KV_W, F32),
        rows2d(KV_W, F32),
        kv_t, kv_t, kv_t, kv_t, kv_t, kv_t,
        tiles(2 * KV_W),
        tiles(KV_W), tiles(KV_W), tiles(KV_W),
        rows2d(LANES, F32),
        rows2d(D_B, F32),
    ]
    out_shapes = [o[0] for o in outs]
    out_specs = [o[1] for o in outs]
    in_specs = [
        pl.BlockSpec((tm, D_MODEL), row),
        pl.BlockSpec((1, D_MODEL), const2),
        pl.BlockSpec((D_MODEL, C_END), const2),
        pl.BlockSpec((1, D_A), const2),
        pl.BlockSpec((1, D_A), const2),
        pl.BlockSpec((A_GROUPS, CHUNK, CHUNK), lambda i: (0, 0, 0)),
        pl.BlockSpec((CHUNK, D_A), const2),
        pl.BlockSpec((tm, 3 * LANES), lambda i: (i % tiles_per_batch, 0)),
    ]
    return pl.pallas_call(
        functools.partial(_inproj_prompt_kernel, tm=tm, tiles_per_batch=tiles_per_batch),
        grid=(n // tm,),
        in_specs=in_specs,
        out_specs=out_specs,
        out_shape=out_shapes,
        scratch_shapes=[pltpu.VMEM((tm, C_END), F32)],
        compiler_params=pltpu.CompilerParams(
            dimension_semantics=("arbitrary",), vmem_limit_bytes=VMEM_LIMIT_BYTES),
        name="inproj_prompt",
    )(x2, g, w, lng, lnb, ws_bf, bias_full, rope_tab)


def _inproj_decode_kernel(x_ref, g_ref, w_ref, lng_ref, lnb_ref, wsv_ref, bsv_ref, rope_ref,
                          aout_ref, qa_ref, kc_ref, vc_ref, ks_ref, vs_ref, kw_ref, vw_ref,
                          gate_ref, szb_ref, vn_ref, z_scr):
    _project(x_ref, g_ref, w_ref, z_scr)
    tab = rope_ref[...]
    u = z_scr[:, C_U:C_U + D_A]
    v = z_scr[:, C_V:C_V + D_A]
    za = z_scr[:, C_ZA:C_ZA + D_A]
    vn = _layer_norm_rows(v, lng_ref[...], lnb_ref[...])
    vn_ref[...] = vn
    s = vn * wsv_ref[...] + bsv_ref[...]
    aout_ref[...] = (u * s * jax.nn.silu(za)).astype(BF16)

    q = z_scr[:, C_Q:C_Q + D_B]
    qr = jnp.concatenate(
        [_rope(q[:, j * LANES:(j + 1) * LANES], tab) for j in range(4)], axis=1) * (HEAD_DIM ** -0.5)
    for e, slab in enumerate(_head_slabs(qr)):
        qa_ref[:, e * LANES:(e + 1) * LANES] = slab.astype(BF16)
    kc_ref[...] = z_scr[:, C_KC:C_KC + KV_W]
    vc_ref[...] = z_scr[:, C_VC:C_VC + KV_W]
    ks_ref[...] = _rope(z_scr[:, C_KS:C_KS + KV_W], tab)
    vs_ref[...] = z_scr[:, C_VS:C_VS + KV_W]
    kw_ref[...] = _rope(z_scr[:, C_KW:C_KW + KV_W], tab)
    vw_ref[...] = z_scr[:, C_VW:C_VW + KV_W]
    gate_ref[...] = jax.nn.sigmoid(z_scr[:, C_G:C_G + LANES])
    szb = jax.nn.silu(z_scr[:, C_ZB:C_ZB + D_B])
    for e, slab in enumerate(_head_slabs(szb)):
        szb_ref[:, e * LANES:(e + 1) * LANES] = slab


def _inproj_decode(x2, g, w, lng, lnb, wsv, bsv, rope_tab):
    n = x2.shape[0]
    out_shapes = [
        jax.ShapeDtypeStruct((n, D_A), BF16),
        jax.ShapeDtypeStruct((n, N_HEADS * LANES), BF16),
        jax.ShapeDtypeStruct((n, KV_W), F32),
        jax.ShapeDtypeStruct((n, KV_W), F32),
        jax.ShapeDtypeStruct((n, KV_W), F32),
        jax.ShapeDtypeStruct((n, KV_W), F32),
        jax.ShapeDtypeStruct((n, KV_W), F32),
        jax.ShapeDtypeStruct((n, KV_W), F32),
        jax.ShapeDtypeStruct((n, LANES), F32),
        jax.ShapeDtypeStruct((n, N_HEADS * LANES), F32),
        jax.ShapeDtypeStruct((n, D_A), F32),
    ]
    return pl.pallas_call(
        _inproj_decode_kernel,
        out_shape=out_shapes,
        scratch_shapes=[pltpu.VMEM((n, C_END), F32)],
        compiler_params=pltpu.CompilerParams(vmem_limit_bytes=VMEM_LIMIT_BYTES),
        name="inproj_decode",
    )(x2, g, w, lng, lnb, wsv, bsv, rope_tab)


def _compress_chunks(ch, pe_ref, w_ref):
    a = _dot((ch + pe_ref[0:1, :]).astype(BF16), w_ref[:, 0:KV_W])
    b = _dot((ch + pe_ref[1:2, :]).astype(BF16), w_ref[:, KV_W:2 * KV_W])
    return a + pltpu.roll(b, ch.shape[0] - 1, 0)


def _compress_prompt_kernel(kc_ref, vc_ref, wk_ref, wv_ref, pek_ref, pev_ref, rope_ref, kcc_ref, vcc_ref):
    kcc = _compress_chunks(kc_ref[0], pek_ref, wk_ref)
    kcc_ref[0] = _rope(kcc, rope_ref[...]).astype(BF16)
    vcc_ref[0] = _compress_chunks(vc_ref[0], pev_ref, wv_ref).astype(BF16)


def _compress_prompt(kc3, vc3, wk, wv, pek, pev, rope_tab):
    b, c, width = kc3.shape
    blk = pl.BlockSpec((1, c, width), lambda i: (i, 0, 0))
    const2 = lambda i: (0, 0)
    return pl.pallas_call(
        _compress_prompt_kernel,
        grid=(b,),
        in_specs=[blk, blk,
                  pl.BlockSpec(wk.shape, const2), pl.BlockSpec(wv.shape, const2),
                  pl.BlockSpec(pek.shape, const2), pl.BlockSpec(pev.shape, const2),
                  pl.BlockSpec(rope_tab.shape, const2)],
        out_specs=[pl.BlockSpec((1, c, KV_W), lambda i: (i, 0, 0))] * 2,
        out_shape=[jax.ShapeDtypeStruct((b, c, KV_W), BF16)] * 2,
        compiler_params=pltpu.CompilerParams(
            dimension_semantics=("arbitrary",), vmem_limit_bytes=VMEM_LIMIT_BYTES),
        name="compress_prompt",
    )(kc3, vc3, wk, wv, pek, pev, rope_tab)


def _select_blocks(score, kk):
    picked = jnp.zeros(score.shape, F32)
    for _ in range(N_SELECT):
        cm = jnp.max(score, axis=0, keepdims=True)
        first = jnp.min(jnp.where(score == cm, kk, LANES), axis=0, keepdims=True)
        hit = kk == first
        picked = jnp.where(hit, 1.0, picked)
        score = jnp.where(hit, -jnp.inf, score)
    return picked


def _softmax_rows(s):
    mx = jnp.max(s, axis=1, keepdims=True)
    ex = jnp.exp(s - mx)
    return ex * (1.0 / jnp.sum(ex, axis=1, keepdims=True))


def _attn_prompt_kernel(qa_ref, gate_ref, szb_ref, kcc_ref, vcc_ref, ovl_ref,
                        ksa_ref, vsb_ref, kwb_ref, vwb_ref, out_ref,
                        qaug_scr, m_scr, l_scr, acc_scr):
    rows = N_HEADS * Q_BLOCK
    q0 = pl.program_id(1) * Q_BLOCK
    for e in range(N_HEADS):
        qaug_scr[e * Q_BLOCK:(e + 1) * Q_BLOCK, 0:LANES] = qa_ref[0, :, e * LANES:(e + 1) * LANES]
    qall = qaug_scr[:, 0:LANES]

    n_cmp = kcc_ref.shape[1]
    s = _dot_nt(qall, kcc_ref[0])
    t_c = q0 + (_row_iota((rows, n_cmp)) & (Q_BLOCK - 1))
    seen = (_lane_iota((rows, n_cmp)) * CMP_STRIDE + (CMP_BLOCK - 1)) <= t_c
    p_c = jnp.where(seen, _softmax_rows(jnp.where(seen, s, NEG)), 0.0)
    o_c = _dot(p_c.astype(BF16), vcc_ref[0])

    kk = _row_iota((LANES, Q_BLOCK))
    tq = q0 + _lane_iota((LANES, Q_BLOCK))
    valid = kk * SEL_BLOCK <= tq
    cur = tq >> SEL_SHIFT
    forced = (kk == 0) | (kk == cur) | (kk == cur - 1)
    scores = []
    for h in range(N_KV):
        base = h * GQA * Q_BLOCK
        p_sum = (p_c[base:base + Q_BLOCK] + p_c[base + Q_BLOCK:base + 2 * Q_BLOCK]
                 + p_c[base + 2 * Q_BLOCK:base + 3 * Q_BLOCK] + p_c[base + 3 * Q_BLOCK:base + 4 * Q_BLOCK])
        imp_t = _dot(p_sum.astype(BF16), ovl_ref[...]).T
        scores.append(jnp.where(valid, jnp.where(forced, imp_t + FORCE_SCORE, imp_t), -jnp.inf))
    picked = _select_blocks(jnp.concatenate(scores, axis=1), jnp.concatenate([kk, kk], axis=1))
    for h in range(N_KV):
        pk = picked[:, h * Q_BLOCK:(h + 1) * Q_BLOCK]
        selneg = jnp.where(valid, jnp.where(pk > 0.0, 0.0, NEG), NEG).T.astype(BF16)
        for g in range(GQA):
            r0 = (h * GQA + g) * Q_BLOCK
            qaug_scr[r0:r0 + Q_BLOCK, LANES:2 * LANES] = selneg

    m_scr[...] = jnp.full(m_scr.shape, -jnp.inf, F32)
    l_scr[...] = jnp.zeros(l_scr.shape, F32)
    acc_scr[...] = jnp.zeros(acc_scr.shape, F32)

    tiles_per_step = SEL_TILE // LANES

    def sel_step(step, causal):
        j0 = step * tiles_per_step
        kt = jnp.concatenate([ksa_ref[0, j0 + i] for i in range(tiles_per_step)], axis=1)
        vt = jnp.concatenate([vsb_ref[0, j0 + i] for i in range(tiles_per_step)], axis=1)
        sc = _dot(qaug_scr[...], kt)
        if causal:
            tok = step * SEL_TILE + _lane_iota((rows, SEL_TILE))
            tt = q0 + (_row_iota((rows, SEL_TILE)) & (Q_BLOCK - 1))
            sc = jnp.where(tok <= tt, sc, NEG)
        m_prev = m_scr[...]
        m_next = jnp.maximum(m_prev, jnp.max(sc, axis=1, keepdims=True))
        p = jnp.exp(sc - jnp.concatenate([m_next] * (SEL_TILE // LANES), axis=1))
        alpha = jnp.exp(m_prev - m_next)
        l_scr[...] = alpha * l_scr[...] + jnp.sum(p, axis=1, keepdims=True)
        acc_scr[...] = alpha * acc_scr[...] + _dot_nt(p.astype(BF16), vt)
        m_scr[...] = m_next

    n_full = q0 // SEL_TILE

    def full_body(j, carry):
        sel_step(j, False)
        return carry

    lax.fori_loop(0, n_full, full_body, 0)
    sel_step(n_full, True)

    wj0 = jnp.maximum(pl.program_id(1) - WINDOW // Q_BLOCK, 0)
    n_win_tiles = WIN_SPAN // LANES
    kwt = jnp.concatenate([kwb_ref[0, wj0 + i] for i in range(n_win_tiles)], axis=1)
    vwt = jnp.concatenate([vwb_ref[0, wj0 + i] for i in range(n_win_tiles)], axis=1)
    sw = _dot(qall, kwt)
    kp = wj0 * LANES + _lane_iota((rows, WIN_SPAN))
    tw = q0 + (_row_iota((rows, WIN_SPAN)) & (Q_BLOCK - 1))
    sw = jnp.where(kp <= tw, jnp.where(kp > tw - WINDOW, sw, NEG), NEG)
    o_w = _dot_nt(_softmax_rows(sw).astype(BF16), vwt)

    gates = gate_ref[0]
    heads = []
    for e in range(N_HEADS):
        r0 = e * Q_BLOCK
        o_s = acc_scr[r0:r0 + Q_BLOCK, :] * (1.0 / l_scr[r0:r0 + Q_BLOCK, :])
        heads.append(gates[:, 3 * e:3 * e + 1] * o_c[r0:r0 + Q_BLOCK]
                     + gates[:, 3 * e + 1:3 * e + 2] * o_s
                     + gates[:, 3 * e + 2:3 * e + 3] * o_w[r0:r0 + Q_BLOCK])
    lo = _lane_iota((Q_BLOCK, LANES)) < HEAD_DIM
    for j in range(N_HEADS // 2):
        a, b = heads[2 * j], heads[2 * j + 1]
        if j < 2:
            slab = jnp.where(lo, a, pltpu.roll(b, HEAD_DIM, 1))
        else:
            slab = jnp.where(lo, pltpu.roll(a, HEAD_DIM, 1), b)
        out_ref[0, :, j * LANES:(j + 1) * LANES] = (
            slab * szb_ref[0, :, j * LANES:(j + 1) * LANES]).astype(BF16)


def _attn_prompt(qa3, gate3, szb3, kcc, vcc, ovl, ksa3, vsb3, kwb3, vwb3):
    b, t, _ = qa3.shape
    nq = t // Q_BLOCK
    rows = N_HEADS * Q_BLOCK
    qblk = lambda w: pl.BlockSpec((1, Q_BLOCK, w), lambda bi, i: (bi, i, 0))
    seq = lambda a: pl.BlockSpec((1,) + a.shape[1:], lambda bi, i, nd=a.ndim: (bi,) + (0,) * (nd - 1))
    return pl.pallas_call(
        _attn_prompt_kernel,
        grid=(b, nq),
        in_specs=[qblk(N_HEADS * LANES), qblk(LANES), qblk(D_B),
                  seq(kcc), seq(vcc), pl.BlockSpec(ovl.shape, lambda bi, i: (0, 0)),
                  seq(ksa3), seq(vsb3), seq(kwb3), seq(vwb3)],
        out_specs=qblk(D_B),
        out_shape=jax.ShapeDtypeStruct((b, t, D_B), BF16),
        scratch_shapes=[pltpu.VMEM((rows, 2 * LANES), BF16),
                        pltpu.VMEM((rows, LANES), F32),
                        pltpu.VMEM((rows, LANES), F32),
                        pltpu.VMEM((rows, LANES), F32)],
        compiler_params=pltpu.CompilerParams(
            dimension_semantics=("arbitrary", "arbitrary"), vmem_limit_bytes=VMEM_LIMIT_BYTES),
        name="attn_prompt",
    )(qa3, gate3, szb3, kcc, vcc, ovl, ksa3, vsb3, kwb3, vwb3)


def _outproj_kernel(x_ref, a_ref, m_ref, wa_ref, wb_ref, fg_ref, y_ref):
    y = x_ref[...] + _dot(a_ref[...], wa_ref[...]) + _dot(m_ref[...].astype(BF16), wb_ref[...])
    y_ref[...] = _rms_rows(y, fg_ref[...])


def _outproj(x2, a_out, mix_b, wa, wb, fg, tm):
    n = x2.shape[0]
    row = lambda i: (i, 0)
    const2 = lambda i: (0, 0)
    return pl.pallas_call(
        _outproj_kernel,
        grid=(n // tm,),
        in_specs=[pl.BlockSpec((tm, D_MODEL), row),
                  pl.BlockSpec((tm, a_out.shape[1]), row),
                  pl.BlockSpec((tm, mix_b.shape[1]), row),
                  pl.BlockSpec(wa.shape, const2), pl.BlockSpec(wb.shape, const2),
                  pl.BlockSpec((1, D_MODEL), const2)],
        out_specs=pl.BlockSpec((tm, D_MODEL), row),
        out_shape=jax.ShapeDtypeStruct((n, D_MODEL), F32),
        compiler_params=pltpu.CompilerParams(
            dimension_semantics=("arbitrary",), vmem_limit_bytes=VMEM_LIMIT_BYTES),
        name="outproj",
    )(x2, a_out, mix_b, wa, wb, fg)


def _decode_cmp_kernel(pt_ref, *refs, n_pages):
    kpages = refs[0:n_pages]
    vpages = refs[n_pages:2 * n_pages]
    (q_ref, wk_ref, wv_ref, pek_ref, pev_ref, rope_ref, ovl_ref, oc_ref, psum_ref) = refs[2 * n_pages:]
    del pt_ref
    kch = jnp.concatenate([r[0] for r in kpages], axis=0)
    vch = jnp.concatenate([r[0] for r in vpages], axis=0)
    kcc = _rope(_compress_chunks(kch, pek_ref, wk_ref), rope_ref[...]).astype(BF16)
    vcc = _compress_chunks(vch, pev_ref, wv_ref).astype(BF16)
    n_chunks = kch.shape[0]
    q = q_ref[0]
    rows = q.shape[0]
    s = _dot_nt(q, kcc)
    seen = _lane_iota((rows, n_chunks)) < n_chunks - 1
    p_c = jnp.where(seen, _softmax_rows(jnp.where(seen, s, NEG)), 0.0)
    oc_ref[0] = _dot(p_c.astype(BF16), vcc)[0:N_HEADS]
    ps0 = jnp.sum(p_c[0:GQA], axis=0, keepdims=True)
    ps1 = jnp.sum(p_c[GQA:2 * GQA], axis=0, keepdims=True)
    psum = jnp.where(_row_iota((rows, n_chunks)) < N_HEADS // 2, ps0, ps1)
    psum_ref[0] = _dot(psum.astype(BF16), ovl_ref[...])[0:N_HEADS]


def _decode_cmp(page_table, ck_pool, cv_pool, q16, wk, wv, pek, pev, rope_tab, ovl):
    nb, n_pages = page_table.shape
    chunks_per_page = PAGE_SIZE // CMP_STRIDE
    width = CMP_STRIDE * KV_W
    page = lambda p: pl.BlockSpec((1, chunks_per_page, width), lambda b, pt, p=p: (pt[b, p], 0, 0))
    const2 = lambda b, pt: (0, 0)
    per_b = lambda a: pl.BlockSpec((1,) + a.shape[1:], lambda b, pt: (b, 0, 0))
    in_specs = ([page(p) for p in range(n_pages)] * 2
                + [per_b(q16),
                   pl.BlockSpec(wk.shape, const2), pl.BlockSpec(wv.shape, const2),
                   pl.BlockSpec(pek.shape, const2), pl.BlockSpec(pev.shape, const2),
                   pl.BlockSpec(rope_tab.shape, const2), pl.BlockSpec(ovl.shape, const2)])
    out_blk = pl.BlockSpec((1, N_HEADS, LANES), lambda b, pt: (b, 0, 0))
    grid_spec = pltpu.PrefetchScalarGridSpec(
        num_scalar_prefetch=1, grid=(nb,), in_specs=in_specs, out_specs=[out_blk, out_blk])
    return pl.pallas_call(
        functools.partial(_decode_cmp_kernel, n_pages=n_pages),
        grid_spec=grid_spec,
        out_shape=[jax.ShapeDtypeStruct((nb, N_HEADS, LANES), F32)] * 2,
        compiler_params=pltpu.CompilerParams(
            dimension_semantics=("arbitrary",), vmem_limit_bytes=VMEM_LIMIT_BYTES),
        name="decode_cmp",
    )(page_table, *([ck_pool] * n_pages), *([cv_pool] * n_pages),
      q16, wk, wv, pek, pev, rope_tab, ovl)


def _decode_topk_kernel(imp_ref, sel_ref, *, n_blk, cur):
    n = imp_ref.shape[0]
    tiles = [imp_ref[i * LANES:(i + 1) * LANES, :].T for i in range(n // LANES)]
    imp_t = jnp.concatenate(tiles, axis=1)
    kk = _row_iota(imp_t.shape)
    forced = (kk == 0) | (kk == cur) | (kk == cur - 1)
    exists = kk < n_blk
    score = jnp.where(exists, jnp.where(forced, imp_t + FORCE_SCORE, imp_t), -jnp.inf)
    picked = _select_blocks(score, kk)
    selneg = jnp.where(exists, jnp.where(picked > 0.0, 0.0, NEG), NEG)
    for i in range(n // LANES):
        sel_ref[i * LANES:(i + 1) * LANES, :] = selneg[:, i * LANES:(i + 1) * LANES].T


def _decode_topk(imp2, n_blk, cur):
    return pl.pallas_call(
        functools.partial(_decode_topk_kernel, n_blk=n_blk, cur=cur),
        out_shape=jax.ShapeDtypeStruct(imp2.shape, F32),
        compiler_params=pltpu.CompilerParams(vmem_limit_bytes=VMEM_LIMIT_BYTES),
        name="decode_topk",
    )(imp2)


def _decode_attend(s, q32, vals_t_bf, k_new, v_new, bias_new):
    kn = k_new.astype(BF16).astype(F32)
    s_new = jnp.sum(q32 * kn, axis=1, keepdims=True) + bias_new
    mx = jnp.maximum(jnp.max(s, axis=1, keepdims=True), s_new)
    ex = jnp.exp(s - mx)
    ex_new = jnp.exp(s_new - mx)
    inv = 1.0 / (jnp.sum(ex, axis=1, keepdims=True) + ex_new)
    p = (ex * inv).astype(BF16)
    p_new = (ex_new * inv).astype(BF16).astype(F32)
    return _dot_nt(p, vals_t_bf) + p_new * v_new.astype(BF16).astype(F32)


def _decode_sel_kernel(pt_ref, *refs, n_pages, cur):
    kpages = refs[0:n_pages]
    vpages = refs[n_pages:2 * n_pages]
    (ckw_ref, cvw_ref, q_ref, sel_ref, ksn_ref, vsn_ref, kwn_ref, vwn_ref, kwc_ref, vwc_ref,
     gate_ref, oc_ref, szb_ref, onehot_ref, mix_ref, okw_ref, ovw_ref) = refs[2 * n_pages:]
    del pt_ref
    q = q_ref[0]
    q32 = q.astype(F32)
    rows = q.shape[0]
    sel = sel_ref[0]

    kt = jnp.concatenate([r[0] for r in kpages], axis=1).astype(BF16)
    vt = jnp.concatenate([r[0] for r in vpages], axis=1).astype(BF16)
    kaug = jnp.concatenate([kt, onehot_ref[...]], axis=0)
    qaug = jnp.concatenate([q, sel], axis=1)
    s_sel = _dot(qaug, kaug)
    o_s = _decode_attend(s_sel, q32, vt, ksn_ref[0], vsn_ref[0], sel.astype(F32)[:, cur:cur + 1])

    kwin = ckw_ref[0]
    vwin = cvw_ref[0]
    keep = kwin.shape[1]
    visible = _lane_iota((rows, keep)) > keep - WINDOW
    s_win = jnp.where(visible, _dot(q, kwin.astype(BF16)), NEG)
    o_w = _decode_attend(s_win, q32, vwin.astype(BF16), kwn_ref[0], vwn_ref[0], 0.0)

    g = gate_ref[0]
    o = g[:, 0:1] * oc_ref[0] + g[:, 1:2] * o_s[0:N_HEADS] + g[:, 2:3] * o_w[0:N_HEADS]
    mix_ref[0] = o * szb_ref[0]

    last = _lane_iota((KV_W, keep)) == keep - 1
    okw_ref[0] = jnp.where(last, kwc_ref[0], pltpu.roll(kwin, keep - 1, 1))
    ovw_ref[0] = jnp.where(last, vwc_ref[0], pltpu.roll(vwin, keep - 1, 1))


def _decode_sel(page_table, ks_pool, vs_pool, ckw, cvw, q16, sel16, ksn, vsn, kwn, vwn, kwc, vwc,
                gate3, oc, szb3, onehot, cur):
    nb, n_pages = page_table.shape
    keep = ckw.shape[2]
    page = lambda p: pl.BlockSpec((1, KV_W, PAGE_SIZE), lambda b, pt, p=p: (pt[b, p], 0, 0))
    per_b = lambda a: pl.BlockSpec((1,) + a.shape[1:], lambda b, pt: (b, 0, 0))
    in_specs = ([page(p) for p in range(n_pages)] * 2
                + [per_b(a) for a in (ckw, cvw, q16, sel16, ksn, vsn, kwn, vwn, kwc, vwc,
                                      gate3, oc, szb3)]
                + [pl.BlockSpec(onehot.shape, lambda b, pt: (0, 0))])
    out_specs = [pl.BlockSpec((1, N_HEADS, LANES), lambda b, pt: (b, 0, 0)),
                 pl.BlockSpec((1, KV_W, keep), lambda b, pt: (b, 0, 0)),
                 pl.BlockSpec((1, KV_W, keep), lambda b, pt: (b, 0, 0))]
    grid_spec = pltpu.PrefetchScalarGridSpec(
        num_scalar_prefetch=1, grid=(nb,), in_specs=in_specs, out_specs=out_specs)
    return pl.pallas_call(
        functools.partial(_decode_sel_kernel, n_pages=n_pages, cur=cur),
        grid_spec=grid_spec,
        out_shape=[jax.ShapeDtypeStruct((nb, N_HEADS, LANES), F32),
                   jax.ShapeDtypeStruct((nb, KV_W, keep), F32),
                   jax.ShapeDtypeStruct((nb, KV_W, keep), F32)],
        compiler_params=pltpu.CompilerParams(
            dimension_semantics=("arbitrary",), vmem_limit_bytes=VMEM_LIMIT_BYTES),
        name="decode_sel",
    )(page_table, *([ks_pool] * n_pages), *([vs_pool] * n_pages),
      ckw, cvw, q16, sel16, ksn, vsn, kwn, vwn, kwc, vwc, gate3, oc, szb3, onehot)


def _rope_table(pos):
    half = ROT_DIM // 2
    inv = jnp.power(jnp.float32(ROPE_THETA), -jnp.arange(half, dtype=F32) / half)
    ang = pos.astype(F32)[:, None] * inv[None, :]
    cs = jnp.concatenate([jnp.cos(ang), jnp.sin(ang)], axis=1)
    lane = jnp.arange(3 * LANES)
    seg, l64 = lane // LANES, lane % HEAD_DIM
    src = jnp.where(seg == 0, l64 % half, half + l64 % half)
    sign = jnp.where(seg == 0, l64 < ROT_DIM,
                     jnp.where(seg == 1, l64 < half, (l64 >= half) & (l64 < ROT_DIM))).astype(F32)
    sign = jnp.where(seg == 1, -sign, sign)
    place = (jnp.arange(ROT_DIM)[:, None] == src[None, :]).astype(F32) * sign[None, :]
    ones = ((seg == 0) & (l64 >= ROT_DIM)).astype(F32)
    return jnp.dot(cs, place, precision=lax.Precision.HIGHEST) + ones[None, :]


def _compress_weights(w_c, pe_c):
    w4 = w_c.reshape(2, CMP_STRIDE, HEAD_DIM, HEAD_DIM)
    wb = jnp.einsum('arde,hg->rhdage', w4, jnp.eye(N_KV, dtype=w_c.dtype))
    wb = wb.reshape(CMP_STRIDE * KV_W, 2 * KV_W).astype(BF16)
    pe = jnp.broadcast_to(pe_c.reshape(2, CMP_STRIDE, 1, HEAD_DIM), (2, CMP_STRIDE, N_KV, HEAD_DIM))
    return wb, pe.reshape(2, CMP_STRIDE * KV_W)


def _overlap_matrix(n_rows, n_blk):
    cs = jnp.arange(n_rows, dtype=jnp.int32)[:, None] * CMP_STRIDE
    bs = jnp.arange(LANES, dtype=jnp.int32)[None, :] * SEL_BLOCK
    hit = (cs < bs + SEL_BLOCK) & (cs + CMP_BLOCK > bs) & (jnp.arange(LANES)[None, :] < n_blk)
    return hit.astype(BF16)


def kernel(x_prompt, x_sample, cache_k_cmp, cache_v_cmp, cache_k_sel, cache_v_sel, cache_k_win,
           cache_v_win, page_table, norm_g, w_in, ln_v_g, ln_v_b, w_s, b_s, w_ck, pe_ck, w_cv,
           pe_cv, w_out, final_g):
    depth = norm_g.shape[0]
    assert depth == 1
    bsz, seq, _ = x_prompt.shape
    nb, dec_seq, _ = x_sample.shape
    assert dec_seq == 1 and seq % SEL_TILE == 0
    n_pages = page_table.shape[1]
    past = n_pages * PAGE_SIZE
    keep = cache_k_win.shape[2]
    assert keep == WINDOW and past % SEL_BLOCK == 0

    w = w_in[0]
    w_pad = jnp.concatenate(
        [w[:, :C_G + N_GATE_COLS], jnp.zeros((D_MODEL, LANES - N_GATE_COLS), w.dtype),
         w[:, C_G + N_GATE_COLS:]], axis=1).astype(BF16)
    g_row = norm_g[0][None, :]
    lng = ln_v_g[0][None, :]
    lnb = ln_v_b[0][None, :]
    fg = final_g[None, :]
    wk_c, pek = _compress_weights(w_ck[0], pe_ck[0])
    wv_c, pev = _compress_weights(w_cv[0], pe_cv[0])
    w_o = w_out[0].astype(BF16)
    wo_a, wo_b = w_o[:D_A], w_o[D_A:]

    tril = jnp.tril(jnp.ones((CHUNK, CHUNK), w_s.dtype))
    ws_bf = (w_s[0] * tril).astype(BF16)
    bias_full = jnp.repeat(b_s[0].T, HEAD_DIM, axis=1)
    xp2 = x_prompt.reshape(bsz * seq, D_MODEL)
    (a_out, qa, kc, vc, kc_t, vc_t, ks_t, vs_t, kw_t, vw_t, ksa, vsb, kwb, vwb, gates, szb) = _inproj_prompt(
        xp2, g_row, w_pad, lng, lnb, ws_bf, bias_full,
        _rope_table(jnp.arange(seq, dtype=jnp.int32)), seq)
    n_chunks = seq // CMP_STRIDE
    kcc, vcc = _compress_prompt(
        kc.reshape(bsz, n_chunks, CMP_STRIDE * KV_W), vc.reshape(bsz, n_chunks, CMP_STRIDE * KV_W),
        wk_c, wv_c, pek, pev,
        _rope_table(jnp.arange(n_chunks, dtype=jnp.int32) * CMP_STRIDE))
    b3 = lambda a: a.reshape(bsz, seq, a.shape[-1])
    mix_p = _attn_prompt(b3(qa), b3(gates), b3(szb), kcc, vcc,
                         _overlap_matrix(n_chunks, seq // SEL_BLOCK),
                         ksa, vsb, kwb, vwb)
    y_prompt = _outproj(xp2, a_out, mix_p.reshape(bsz * seq, D_B), wo_a, wo_b, fg, 512)
    y_prompt = y_prompt.reshape(bsz, seq, D_MODEL)
    st = lambda a: jnp.transpose(a.reshape(a.shape[0], N_KV, HEAD_DIM, a.shape[2]), (0, 3, 1, 2))[None]
    keep_p = min(WINDOW, seq)

    xs2 = x_sample.reshape(nb, D_MODEL)
    wsv = jnp.repeat(w_s[0][:, 0, 0], HEAD_DIM)[None, :]
    bsv = jnp.repeat(b_s[0][:, 0], HEAD_DIM)[None, :]
    (a_s, qa_s, kc_s, vc_s, ks_s, vs_s, kw_s, vw_s, gates_s, szb_s, vn_s) = _inproj_decode(
        xs2, g_row, w_pad, lng, lnb, wsv, bsv,
        _rope_table(jnp.full((nb,), past, dtype=jnp.int32)))
    q16 = jnp.pad(qa_s.reshape(nb, N_HEADS, LANES), ((0, 0), (0, N_HEADS), (0, 0)))
    chunks_per_page = PAGE_SIZE // CMP_STRIDE
    n_pool = cache_k_cmp.shape[1]
    d_chunks = past // CMP_STRIDE
    n_blk = -(-(past + 1) // SEL_BLOCK)
    cur = past // SEL_BLOCK
    oc_s, imp_s = _decode_cmp(
        page_table,
        cache_k_cmp[0].reshape(n_pool, chunks_per_page, CMP_STRIDE * KV_W),
        cache_v_cmp[0].reshape(n_pool, chunks_per_page, CMP_STRIDE * KV_W),
        q16, wk_c, wv_c, pek, pev,
        _rope_table(jnp.arange(d_chunks, dtype=jnp.int32) * CMP_STRIDE),
        _overlap_matrix(d_chunks, n_blk))
    selneg = _decode_topk(imp_s[:, ::GQA, :].reshape(nb * N_KV, LANES), n_blk, cur)
    sel16 = jnp.pad(jnp.repeat(selneg.reshape(nb, N_KV, LANES), GQA, axis=1),
                    ((0, 0), (0, N_HEADS), (0, 0))).astype(BF16)
    tok_blk = jnp.arange(past, dtype=jnp.int32)[None, :] // SEL_BLOCK
    onehot = (tok_blk == jnp.arange(LANES, dtype=jnp.int32)[:, None]).astype(BF16)
    r3 = lambda a: a.reshape(nb, 1, a.shape[-1])
    c3 = lambda a: a.reshape(nb, a.shape[-1], 1)
    kv_t = lambda c: jnp.transpose(c[0], (0, 2, 3, 1)).reshape(c.shape[1], KV_W, c.shape[2])
    mix_s, okw, ovw = _decode_sel(
        page_table, kv_t(cache_k_sel), kv_t(cache_v_sel), kv_t(cache_k_win), kv_t(cache_v_win),
        q16, sel16, r3(ks_s), r3(vs_s), r3(kw_s), r3(vw_s), c3(kw_s), c3(vw_s),
        gates_s[:, :N_GATE_COLS].reshape(nb, N_HEADS, 3), oc_s,
        szb_s.reshape(nb, N_HEADS, LANES), onehot, cur)
    wo_slab = jnp.zeros((N_HEADS, N_KV, HEAD_DIM, D_MODEL), BF16)
    wo_heads = wo_b.reshape(N_HEADS, HEAD_DIM, D_MODEL)
    for e in range(N_HEADS):
        wo_slab = wo_slab.at[e, e // GQA].set(wo_heads[e])
    y_sample = _outproj(xs2, a_s, mix_s.reshape(nb, N_HEADS * LANES), wo_a,
                        wo_slab.reshape(N_HEADS * LANES, D_MODEL), fg, nb)
    y_sample = y_sample.reshape(nb, 1, D_MODEL)
    ss = lambda a: a.reshape(1, nb, 1, N_KV, HEAD_DIM)

    return (y_prompt, y_sample,
            st(kc_t), st(vc_t), st(ks_t), st(vs_t),
            st(kw_t[:, :, seq - keep_p:]), st(vw_t[:, :, seq - keep_p:]),
            ss(kc_s), ss(vc_s), ss(ks_s), ss(vs_s),
            st(okw), st(ovw),
            vn_s.reshape(1, nb, 1, D_A))
```

```python
import functools

import jax
import jax.numpy as jnp
from jax import lax
from jax.experimental import pallas as pl
from jax.experimental.pallas import tpu as pltpu

F32 = jnp.float32
BF16 = jnp.bfloat16

D_MODEL = 1024
HEAD_DIM = 64
D_A = 512
D_B = 512
A_GROUPS = 8
CHUNK = 128
N_HEADS = 8
N_KV = 2
GQA = 4
KV_W = 128
ROT_DIM = 16
ROPE_THETA = 500000.0
CMP_BLOCK = 32
CMP_STRIDE = 16
SEL_BLOCK = 64
SEL_SHIFT = 6
N_SELECT = 16
WINDOW = 512
Q_BLOCK = 128
PAGE_SIZE = 128
NORM_EPS = 1e-6
FORCE_SCORE = 1e4
NEG = -1e30

LANES = 128
VMEM_LIMIT_BYTES = 56 * 1024 * 1024

C_U, C_V, C_ZA, C_Q = 0, 512, 1024, 1536
C_KC, C_VC, C_KS, C_VS, C_KW, C_VW = 2048, 2176, 2304, 2432, 2560, 2688
C_G, C_ZB, C_END = 2816, 2944, 3456
N_GATE_COLS = 3 * N_HEADS

SEL_TILE = 512
WIN_SPAN = WINDOW + Q_BLOCK

_NT = (((1,), (1,)), ((), ()))


def _dot(a, b):
    return jnp.dot(a, b, preferred_element_type=F32)


def _dot_nt(a, b):
    return lax.dot_general(a, b, _NT, preferred_element_type=F32)


def _lane_iota(shape):
    return lax.broadcasted_iota(jnp.int32, shape, len(shape) - 1)


def _row_iota(shape):
    return lax.broadcasted_iota(jnp.int32, shape, len(shape) - 2)


def _rope(x, tab):
    c = tab[:, 0:LANES]
    s1 = tab[:, LANES:2 * LANES]
    s2 = tab[:, 2 * LANES:3 * LANES]
    return x * c + pltpu.roll(x, LANES - ROT_DIM // 2, 1) * s1 + pltpu.roll(x, ROT_DIM // 2, 1) * s2


def _rms_rows(x, g):
    ms = jnp.mean(x * x, axis=-1, keepdims=True)
    return x * lax.rsqrt(ms + NORM_EPS) * g


def _layer_norm_rows(v, g, b):
    mu = jnp.mean(v, axis=-1, keepdims=True)
    vc = v - mu
    var = jnp.mean(vc * vc, axis=-1, keepdims=True)
    return vc * lax.rsqrt(var + NORM_EPS) * g + b


def _head_slabs(x512):
    out = []
    rows = x512.shape[0]
    lane = _lane_iota((rows, LANES))
    lo = lane < HEAD_DIM
    for j in range(4):
        slab = x512[:, j * LANES:(j + 1) * LANES]
        swapped = pltpu.roll(slab, HEAD_DIM, 1)
        if j < 2:
            out.append(jnp.where(lo, slab, 0.0))
            out.append(jnp.where(lo, swapped, 0.0))
        else:
            out.append(jnp.where(lo, 0.0, swapped))
            out.append(jnp.where(lo, 0.0, slab))
    return out


def _project(x_ref, g_ref, w_ref, z_scr):
    hn = _rms_rows(x_ref[...], g_ref[...])
    z_scr[...] = _dot(hn.astype(BF16), w_ref[...])


def _inproj_prompt_kernel(x_ref, g_ref, w_ref, lng_ref, lnb_ref, ws_ref, bias_ref, rope_ref,
                          aout_ref, qt_ref, kc_ref, vc_ref,
                          kct_ref, vct_ref, kst_ref, vst_ref, kwt_ref, vwt_ref,
                          ksa_ref, kwb_ref, vsb_ref, vwb_ref, gate_ref, szb_ref,
                          z_scr, *, tm, tiles_per_batch):
    _project(x_ref, g_ref, w_ref, z_scr)
    tab = rope_ref[...]

    lane = _lane_iota((CHUNK, LANES))
    lo = lane < HEAD_DIM
    for c in range(tm // CHUNK):
        r0 = c * CHUNK
        u = z_scr[r0:r0 + CHUNK, C_U:C_U + D_A]
        v = z_scr[r0:r0 + CHUNK, C_V:C_V + D_A]
        za = z_scr[r0:r0 + CHUNK, C_ZA:C_ZA + D_A]
        vn = _layer_norm_rows(v, lng_ref[...], lnb_ref[...])
        parts = []
        for p in range(A_GROUPS // 2):
            vp = vn[:, p * LANES:(p + 1) * LANES]
            v_lo = jnp.where(lo, vp, 0.0).astype(BF16)
            v_hi = jnp.where(lo, 0.0, vp).astype(BF16)
            parts.append(_dot(ws_ref[2 * p], v_lo) + _dot(ws_ref[2 * p + 1], v_hi))
        s = jnp.concatenate(parts, axis=1) + bias_ref[...]
        aout_ref[r0:r0 + CHUNK, :] = (u * s * jax.nn.silu(za)).astype(BF16)

    q = z_scr[:, C_Q:C_Q + D_B]
    qr = jnp.concatenate(
        [_rope(q[:, j * LANES:(j + 1) * LANES], tab) for j in range(4)], axis=1) * (HEAD_DIM ** -0.5)
    slabs = _head_slabs(qr)
    kc = z_scr[:, C_KC:C_KC + KV_W]
    vc = z_scr[:, C_VC:C_VC + KV_W]
    kc_ref[...] = kc
    vc_ref[...] = vc
    ks = _rope(z_scr[:, C_KS:C_KS + KV_W], tab)
    vs = z_scr[:, C_VS:C_VS + KV_W]
    kw = _rope(z_scr[:, C_KW:C_KW + KV_W], tab)
    vw = z_scr[:, C_VW:C_VW + KV_W]
    gates = jax.nn.sigmoid(z_scr[:, C_G:C_G + LANES])
    pos = (pl.program_id(0) % tiles_per_batch) * tm + _row_iota((tm, LANES))
    onehot = jnp.where((pos >> SEL_SHIFT) == _lane_iota((tm, LANES)), 1.0, 0.0)
    ksa_ref[:, 0:LANES] = ks.astype(BF16)
    ksa_ref[:, LANES:2 * LANES] = onehot.astype(BF16)
    kwb_ref[...] = kw.astype(BF16)
    for j in range(tm // LANES):
        r0, r1 = j * LANES, (j + 1) * LANES
        kct_ref[0, :, r0:r1] = kc[r0:r1].T
        vct_ref[0, :, r0:r1] = vc[r0:r1].T
        kst_ref[0, :, r0:r1] = ks[r0:r1].T
        kwt_ref[0, :, r0:r1] = kw[r0:r1].T
        vs_t, vw_t = vs[r0:r1].T, vw[r0:r1].T
        vst_ref[0, :, r0:r1] = vs_t
        vwt_ref[0, :, r0:r1] = vw_t
        vsb_ref[0, j] = vs_t.astype(BF16)
        vwb_ref[0, j] = vw_t.astype(BF16)
        for e in range(N_HEADS):
            qt_ref[0, j, :, e * Q_BLOCK:(e + 1) * Q_BLOCK] = slabs[e][r0:r1].T.astype(BF16)
        gate_ref[0, j] = gates[r0:r1].T
    szb_ref[...] = jax.nn.silu(z_scr[:, C_ZB:C_ZB + D_B])


def _inproj_prompt(x2, g, w, lng, lnb, ws_bf, bias_full, rope_tab, seq, tm=256):
    n = x2.shape[0]
    tiles_per_batch = seq // tm
    bsz = n // seq
    n_tok_tiles = seq // LANES
    row = lambda i: (i, 0)
    const2 = lambda i: (0, 0)
    rows2d = lambda width, dtype: (jax.ShapeDtypeStruct((n, width), dtype), pl.BlockSpec((tm, width), row))
    kv_t = (jax.ShapeDtypeStruct((bsz, KV_W, seq), F32),
            pl.BlockSpec((1, KV_W, tm), lambda i: (i // tiles_per_batch, 0, i % tiles_per_batch)))
    tiles = lambda r, c, dtype: (
        jax.ShapeDtypeStruct((bsz, n_tok_tiles, r, c), dtype),
        pl.BlockSpec((1, tm // LANES, r, c), lambda i: (i // tiles_per_batch, i % tiles_per_batch, 0, 0)))
    outs = [
        rows2d(D_A, BF16),
        tiles(KV_W, N_HEADS * Q_BLOCK, BF16),
        rows2d(KV_W, F32),
        rows2d(KV_W, F32),
        kv_t, kv_t, kv_t, kv_t, kv_t, kv_t,
        rows2d(2 * KV_W, BF16),
        rows2d(KV_W, BF16),
        tiles(KV_W, LANES, BF16),
        tiles(KV_W, LANES, BF16),
        tiles(LANES, LANES, F32),
        rows2d(D_B, F32),
    ]
    out_shapes = [o[0] for o in outs]
    out_specs = [o[1] for o in outs]
    in_specs = [
        pl.BlockSpec((tm, D_MODEL), row),
        pl.BlockSpec((1, D_MODEL), const2),
        pl.BlockSpec((D_MODEL, C_END), const2),
        pl.BlockSpec((1, D_A), const2),
        pl.BlockSpec((1, D_A), const2),
        pl.BlockSpec((A_GROUPS, CHUNK, CHUNK), lambda i: (0, 0, 0)),
        pl.BlockSpec((CHUNK, D_A), const2),
        pl.BlockSpec((tm, 3 * LANES), lambda i: (i % tiles_per_batch, 0)),
    ]
    return pl.pallas_call(
        functools.partial(_inproj_prompt_kernel, tm=tm, tiles_per_batch=tiles_per_batch),
        grid=(n // tm,),
        in_specs=in_specs,
        out_specs=out_specs,
        out_shape=out_shapes,
        scratch_shapes=[pltpu.VMEM((tm, C_END), F32)],
        compiler_params=pltpu.CompilerParams(
            dimension_semantics=("arbitrary",), vmem_limit_bytes=VMEM_LIMIT_BYTES),
        name="inproj_prompt",
    )(x2, g, w, lng, lnb, ws_bf, bias_full, rope_tab)


def _inproj_decode_kernel(x_ref, g_ref, w_ref, lng_ref, lnb_ref, wsv_ref, bsv_ref, rope_ref,
                          aout_ref, qa_ref, kc_ref, vc_ref, ks_ref, vs_ref, kw_ref, vw_ref,
                          gate_ref, szb_ref, vn_ref, z_scr):
    _project(x_ref, g_ref, w_ref, z_scr)
    tab = rope_ref[...]
    u = z_scr[:, C_U:C_U + D_A]
    v = z_scr[:, C_V:C_V + D_A]
    za = z_scr[:, C_ZA:C_ZA + D_A]
    vn = _layer_norm_rows(v, lng_ref[...], lnb_ref[...])
    vn_ref[...] = vn
    s = vn * wsv_ref[...] + bsv_ref[...]
    aout_ref[...] = (u * s * jax.nn.silu(za)).astype(BF16)

    q = z_scr[:, C_Q:C_Q + D_B]
    qr = jnp.concatenate(
        [_rope(q[:, j * LANES:(j + 1) * LANES], tab) for j in range(4)], axis=1) * (HEAD_DIM ** -0.5)
    for e, slab in enumerate(_head_slabs(qr)):
        qa_ref[:, e * LANES:(e + 1) * LANES] = slab.astype(BF16)
    kc_ref[...] = z_scr[:, C_KC:C_KC + KV_W]
    vc_ref[...] = z_scr[:, C_VC:C_VC + KV_W]
    ks_ref[...] = _rope(z_scr[:, C_KS:C_KS + KV_W], tab)
    vs_ref[...] = z_scr[:, C_VS:C_VS + KV_W]
    kw_ref[...] = _rope(z_scr[:, C_KW:C_KW + KV_W], tab)
    vw_ref[...] = z_scr[:, C_VW:C_VW + KV_W]
    gate_ref[...] = jax.nn.sigmoid(z_scr[:, C_G:C_G + LANES])
    szb = jax.nn.silu(z_scr[:, C_ZB:C_ZB + D_B])
    for e, slab in enumerate(_head_slabs(szb)):
        szb_ref[:, e * LANES:(e + 1) * LANES] = slab


def _inproj_decode(x2, g, w, lng, lnb, wsv, bsv, rope_tab):
    n = x2.shape[0]
    out_shapes = [
        jax.ShapeDtypeStruct((n, D_A), BF16),
        jax.ShapeDtypeStruct((n, N_HEADS * LANES), BF16),
        jax.ShapeDtypeStruct((n, KV_W), F32),
        jax.ShapeDtypeStruct((n, KV_W), F32),
        jax.ShapeDtypeStruct((n, KV_W), F32),
        jax.ShapeDtypeStruct((n, KV_W), F32),
        jax.ShapeDtypeStruct((n, KV_W), F32),
        jax.ShapeDtypeStruct((n, KV_W), F32),
        jax.ShapeDtypeStruct((n, LANES), F32),
        jax.ShapeDtypeStruct((n, N_HEADS * LANES), F32),
        jax.ShapeDtypeStruct((n, D_A), F32),
    ]
    return pl.pallas_call(
        _inproj_decode_kernel,
        out_shape=out_shapes,
        scratch_shapes=[pltpu.VMEM((n, C_END), F32)],
        compiler_params=pltpu.CompilerParams(vmem_limit_bytes=VMEM_LIMIT_BYTES),
        name="inproj_decode",
    )(x2, g, w, lng, lnb, wsv, bsv, rope_tab)


def _compress_chunks(ch, pe_ref, w_ref):
    a = _dot((ch + pe_ref[0:1, :]).astype(BF16), w_ref[:, 0:KV_W])
    b = _dot((ch + pe_ref[1:2, :]).astype(BF16), w_ref[:, KV_W:2 * KV_W])
    return a + pltpu.roll(b, ch.shape[0] - 1, 0)


def _compress_prompt_kernel(kc_ref, vc_ref, wk_ref, wv_ref, pek_ref, pev_ref, rope_ref, kcc_ref, vcct_ref):
    kcc = _compress_chunks(kc_ref[0], pek_ref, wk_ref)
    kcc_ref[0] = _rope(kcc, rope_ref[...]).astype(BF16)
    vcc = _compress_chunks(vc_ref[0], pev_ref, wv_ref)
    for j in range(vcc.shape[0] // LANES):
        vcct_ref[0, :, j * LANES:(j + 1) * LANES] = vcc[j * LANES:(j + 1) * LANES].T.astype(BF16)


def _compress_prompt(kc3, vc3, wk, wv, pek, pev, rope_tab):
    b, c, width = kc3.shape
    blk = pl.BlockSpec((1, c, width), lambda i: (i, 0, 0))
    const2 = lambda i: (0, 0)
    return pl.pallas_call(
        _compress_prompt_kernel,
        grid=(b,),
        in_specs=[blk, blk,
                  pl.BlockSpec(wk.shape, const2), pl.BlockSpec(wv.shape, const2),
                  pl.BlockSpec(pek.shape, const2), pl.BlockSpec(pev.shape, const2),
                  pl.BlockSpec(rope_tab.shape, const2)],
        out_specs=[pl.BlockSpec((1, c, KV_W), lambda i: (i, 0, 0)),
                   pl.BlockSpec((1, KV_W, c), lambda i: (i, 0, 0))],
        out_shape=[jax.ShapeDtypeStruct((b, c, KV_W), BF16), jax.ShapeDtypeStruct((b, KV_W, c), BF16)],
        compiler_params=pltpu.CompilerParams(
            dimension_semantics=("arbitrary",), vmem_limit_bytes=VMEM_LIMIT_BYTES),
        name="compress_prompt",
    )(kc3, vc3, wk, wv, pek, pev, rope_tab)


def _select_blocks(score, kk):
    picked = jnp.zeros(score.shape, F32)
    for _ in range(N_SELECT):
        cm = jnp.max(score, axis=0, keepdims=True)
        first = jnp.min(jnp.where(score == cm, kk, LANES), axis=0, keepdims=True)
        hit = kk == first
        picked = jnp.where(hit, 1.0, picked)
        score = jnp.where(hit, -jnp.inf, score)
    return picked


def _softmax_rows(s):
    mx = jnp.max(s, axis=1, keepdims=True)
    ex = jnp.exp(s - mx)
    return ex * (1.0 / jnp.sum(ex, axis=1, keepdims=True))


def _tile_heads(x):
    return jnp.concatenate([x] * N_HEADS, axis=1)


def _attn_prompt_kernel(qt_ref, gate_ref, szb_ref, kcc_ref, vcct_ref, ovlt_ref,
                        ksa_ref, vsb_ref, kwb_ref, vwb_ref, out_ref,
                        qaug_scr, acc_scr, sa_scr, sb_scr):
    i_blk = pl.program_id(1)
    q0 = i_blk * Q_BLOCK
    cols = N_HEADS * Q_BLOCK
    qt = qt_ref[0, 0]

    n_cmp = kcc_ref.shape[1]
    t_c = q0 + _lane_iota((n_cmp, Q_BLOCK))
    seen = (_row_iota((n_cmp, Q_BLOCK)) * CMP_STRIDE + (CMP_BLOCK - 1)) <= t_c
    s = _dot(kcc_ref[0], qt) + _tile_heads(jnp.where(seen, 0.0, NEG))
    mx = jnp.max(s, axis=0, keepdims=True)
    ex = jnp.exp(s - mx)
    inv = jnp.where(mx > 0.5 * NEG, 1.0 / jnp.sum(ex, axis=0, keepdims=True), 0.0)
    p_c = ex * inv
    o_c = _dot(vcct_ref[0], p_c.astype(BF16))

    kk = _row_iota((LANES, Q_BLOCK))
    tq = q0 + _lane_iota((LANES, Q_BLOCK))
    valid = kk * SEL_BLOCK <= tq
    cur = tq >> SEL_SHIFT
    forced = (kk == 0) | (kk == cur) | (kk == cur - 1)
    scores = []
    for h in range(N_KV):
        c0 = h * GQA * Q_BLOCK
        p_sum = (p_c[:, c0:c0 + Q_BLOCK] + p_c[:, c0 + Q_BLOCK:c0 + 2 * Q_BLOCK]
                 + p_c[:, c0 + 2 * Q_BLOCK:c0 + 3 * Q_BLOCK] + p_c[:, c0 + 3 * Q_BLOCK:c0 + 4 * Q_BLOCK])
        imp_t = _dot(ovlt_ref[...], p_sum.astype(BF16))
        scores.append(jnp.where(valid, jnp.where(forced, imp_t + FORCE_SCORE, imp_t), -jnp.inf))
    picked = _select_blocks(jnp.concatenate(scores, axis=1), jnp.concatenate([kk, kk], axis=1))
    qaug_scr[0:LANES, :] = qt
    for h in range(N_KV):
        pk = picked[:, h * Q_BLOCK:(h + 1) * Q_BLOCK]
        selneg = jnp.where(valid, jnp.where(pk > 0.0, 0.0, NEG), NEG).astype(BF16)
        for g in range(GQA):
            c0 = (h * GQA + g) * Q_BLOCK
            qaug_scr[LANES:2 * LANES, c0:c0 + Q_BLOCK] = selneg

    acc_scr[...] = jnp.zeros(acc_scr.shape, F32)
    tiles_per_step = SEL_TILE // LANES

    def score(step, dst):
        t0 = pl.multiple_of(step * SEL_TILE, SEL_TILE)
        dst[...] = _dot(ksa_ref[0, pl.ds(t0, SEL_TILE), :], qaug_scr[...])

    def attend(src, step, m_prev, l_prev, causal):
        vt = jnp.concatenate([vsb_ref[0, step * tiles_per_step + i] for i in range(tiles_per_step)], axis=1)
        sc = src[...]
        if causal:
            tok = step * SEL_TILE + _row_iota((SEL_TILE, Q_BLOCK))
            sc = sc + _tile_heads(jnp.where(tok <= q0 + _lane_iota((SEL_TILE, Q_BLOCK)), 0.0, NEG))
        m_next = jnp.maximum(m_prev, jnp.max(sc, axis=0, keepdims=True))
        p = jnp.exp(sc - m_next)
        alpha = jnp.exp(m_prev - m_next)
        l_next = alpha * l_prev + jnp.sum(p, axis=0, keepdims=True)
        acc_scr[...] = alpha * acc_scr[...] + _dot(vt, p.astype(BF16))
        return m_next, l_next

    n_full = q0 // SEL_TILE
    score(0, sa_scr)

    def pair(jj, carry):
        m_c, l_c = carry
        score(2 * jj + 1, sb_scr)
        m_c, l_c = attend(sa_scr, 2 * jj, m_c, l_c, False)
        score(2 * jj + 2, sa_scr)
        return attend(sb_scr, 2 * jj + 1, m_c, l_c, False)

    m_run, l_run = lax.fori_loop(
        0, n_full // 2, pair,
        (jnp.full((1, cols), -jnp.inf, F32), jnp.zeros((1, cols), F32)))

    def odd_tail(m_c, l_c):
        score(n_full, sb_scr)
        m_c, l_c = attend(sa_scr, n_full - 1, m_c, l_c, False)
        return attend(sb_scr, n_full, m_c, l_c, True)[1]

    def even_tail(m_c, l_c):
        return attend(sa_scr, n_full, m_c, l_c, True)[1]

    l_run = lax.cond(n_full % 2 == 1, odd_tail, even_tail, m_run, l_run)

    w0 = pl.multiple_of(jnp.maximum(q0 - WINDOW, 0), Q_BLOCK)
    wj0 = jnp.maximum(i_blk - WINDOW // Q_BLOCK, 0)
    kwin = kwb_ref[0, pl.ds(w0, WIN_SPAN), :]
    vwt = jnp.concatenate([vwb_ref[0, wj0 + i] for i in range(WIN_SPAN // LANES)], axis=1)
    kp = w0 + _row_iota((WIN_SPAN, Q_BLOCK))
    tw = q0 + _lane_iota((WIN_SPAN, Q_BLOCK))
    in_win = jnp.where(kp <= tw, jnp.where(kp > tw - WINDOW, 0.0, NEG), NEG)
    sw = _dot(kwin, qt) + _tile_heads(in_win)
    exw = jnp.exp(sw - jnp.max(sw, axis=0, keepdims=True))
    o_w = _dot(vwt, exw.astype(BF16)) * (1.0 / jnp.sum(exw, axis=0, keepdims=True))

    gt = gate_ref[0, 0]
    o_s = acc_scr[...] * (1.0 / l_run)
    heads = []
    for e in range(N_HEADS):
        c0 = e * Q_BLOCK
        o_e = (gt[3 * e:3 * e + 1, :] * o_c[:, c0:c0 + Q_BLOCK]
               + gt[3 * e + 1:3 * e + 2, :] * o_s[:, c0:c0 + Q_BLOCK]
               + gt[3 * e + 2:3 * e + 3, :] * o_w[:, c0:c0 + Q_BLOCK])
        heads.append(o_e.T)
    lo = _lane_iota((Q_BLOCK, LANES)) < HEAD_DIM
    for j in range(N_HEADS // 2):
        a, b = heads[2 * j], heads[2 * j + 1]
        if j < 2:
            slab = jnp.where(lo, a, pltpu.roll(b, HEAD_DIM, 1))
        else:
            slab = jnp.where(lo, pltpu.roll(a, HEAD_DIM, 1), b)
        out_ref[0, :, j * LANES:(j + 1) * LANES] = (
            slab * szb_ref[0, :, j * LANES:(j + 1) * LANES]).astype(BF16)


def _attn_prompt(qt4, gate4, szb3, kcc, vcct, ovlt, ksa3, vsb4, kwb3, vwb4):
    b, nq = qt4.shape[0], qt4.shape[1]
    t = nq * Q_BLOCK
    cols = N_HEADS * Q_BLOCK
    qtile = lambda a: pl.BlockSpec((1, 1) + a.shape[2:], lambda bi, i: (bi, i, 0, 0))
    seq = lambda a: pl.BlockSpec((1,) + a.shape[1:], lambda bi, i, nd=a.ndim: (bi,) + (0,) * (nd - 1))
    return pl.pallas_call(
        _attn_prompt_kernel,
        grid=(b, nq),
        in_specs=[qtile(qt4), qtile(gate4), pl.BlockSpec((1, Q_BLOCK, D_B), lambda bi, i: (bi, i, 0)),
                  seq(kcc), seq(vcct), pl.BlockSpec(ovlt.shape, lambda bi, i: (0, 0)),
                  seq(ksa3), seq(vsb4), seq(kwb3), seq(vwb4)],
        out_specs=pl.BlockSpec((1, Q_BLOCK, D_B), lambda bi, i: (bi, i, 0)),
        out_shape=jax.ShapeDtypeStruct((b, t, D_B), BF16),
        scratch_shapes=[pltpu.VMEM((2 * LANES, cols), BF16),
                        pltpu.VMEM((KV_W, cols), F32),
                        pltpu.VMEM((SEL_TILE, cols), F32),
                        pltpu.VMEM((SEL_TILE, cols), F32)],
        compiler_params=pltpu.CompilerParams(
            dimension_semantics=("arbitrary", "arbitrary"), vmem_limit_bytes=VMEM_LIMIT_BYTES),
        name="attn_prompt",
    )(qt4, gate4, szb3, kcc, vcct, ovlt, ksa3, vsb4, kwb3, vwb4)


def _outproj_kernel(x_ref, a_ref, m_ref, wa_ref, wb_ref, fg_ref, y_ref):
    y = x_ref[...] + _dot(a_ref[...], wa_ref[...]) + _dot(m_ref[...].astype(BF16), wb_ref[...])
    y_ref[...] = _rms_rows(y, fg_ref[...])


def _outproj(x2, a_out, mix_b, wa, wb, fg, tm):
    n = x2.shape[0]
    row = lambda i: (i, 0)
    const2 = lambda i: (0, 0)
    return pl.pallas_call(
        _outproj_kernel,
        grid=(n // tm,),
        in_specs=[pl.BlockSpec((tm, D_MODEL), row),
                  pl.BlockSpec((tm, a_out.shape[1]), row),
                  pl.BlockSpec((tm, mix_b.shape[1]), row),
                  pl.BlockSpec(wa.shape, const2), pl.BlockSpec(wb.shape, const2),
                  pl.BlockSpec((1, D_MODEL), const2)],
        out_specs=pl.BlockSpec((tm, D_MODEL), row),
        out_shape=jax.ShapeDtypeStruct((n, D_MODEL), F32),
        compiler_params=pltpu.CompilerParams(
            dimension_semantics=("arbitrary",), vmem_limit_bytes=VMEM_LIMIT_BYTES),
        name="outproj",
    )(x2, a_out, mix_b, wa, wb, fg)


def _decode_cmp_kernel(pt_ref, *refs, n_pages):
    kpages = refs[0:n_pages]
    vpages = refs[n_pages:2 * n_pages]
    (q_ref, wk_ref, wv_ref, pek_ref, pev_ref, rope_ref, ovl_ref, oc_ref, psum_ref) = refs[2 * n_pages:]
    del pt_ref
    kch = jnp.concatenate([r[0] for r in kpages], axis=0)
    vch = jnp.concatenate([r[0] for r in vpages], axis=0)
    kcc = _rope(_compress_chunks(kch, pek_ref, wk_ref), rope_ref[...]).astype(BF16)
    vcc = _compress_chunks(vch, pev_ref, wv_ref).astype(BF16)
    n_chunks = kch.shape[0]
    q = q_ref[0]
    rows = q.shape[0]
    s = _dot_nt(q, kcc)
    seen = _lane_iota((rows, n_chunks)) < n_chunks - 1
    p_c = jnp.where(seen, _softmax_rows(jnp.where(seen, s, NEG)), 0.0)
    oc_ref[0] = _dot(p_c.astype(BF16), vcc)[0:N_HEADS]
    ps0 = jnp.sum(p_c[0:GQA], axis=0, keepdims=True)
    ps1 = jnp.sum(p_c[GQA:2 * GQA], axis=0, keepdims=True)
    psum = jnp.where(_row_iota((rows, n_chunks)) < N_HEADS // 2, ps0, ps1)
    psum_ref[0] = _dot(psum.astype(BF16), ovl_ref[...])[0:N_HEADS]


def _decode_cmp(page_table, ck_pool, cv_pool, q16, wk, wv, pek, pev, rope_tab, ovl):
    nb, n_pages = page_table.shape
    chunks_per_page = PAGE_SIZE // CMP_STRIDE
    width = CMP_STRIDE * KV_W
    page = lambda p: pl.BlockSpec((1, chunks_per_page, width), lambda b, pt, p=p: (pt[b, p], 0, 0))
    const2 = lambda b, pt: (0, 0)
    per_b = lambda a: pl.BlockSpec((1,) + a.shape[1:], lambda b, pt: (b, 0, 0))
    in_specs = ([page(p) for p in range(n_pages)] * 2
                + [per_b(q16),
                   pl.BlockSpec(wk.shape, const2), pl.BlockSpec(wv.shape, const2),
                   pl.BlockSpec(pek.shape, const2), pl.BlockSpec(pev.shape, const2),
                   pl.BlockSpec(rope_tab.shape, const2), pl.BlockSpec(ovl.shape, const2)])
    out_blk = pl.BlockSpec((1, N_HEADS, LANES), lambda b, pt: (b, 0, 0))
    grid_spec = pltpu.PrefetchScalarGridSpec(
        num_scalar_prefetch=1, grid=(nb,), in_specs=in_specs, out_specs=[out_blk, out_blk])
    return pl.pallas_call(
        functools.partial(_decode_cmp_kernel, n_pages=n_pages),
        grid_spec=grid_spec,
        out_shape=[jax.ShapeDtypeStruct((nb, N_HEADS, LANES), F32)] * 2,
        compiler_params=pltpu.CompilerParams(
            dimension_semantics=("arbitrary",), vmem_limit_bytes=VMEM_LIMIT_BYTES),
        name="decode_cmp",
    )(page_table, *([ck_pool] * n_pages), *([cv_pool] * n_pages),
      q16, wk, wv, pek, pev, rope_tab, ovl)


def _decode_topk_kernel(imp_ref, sel_ref, *, n_blk, cur):
    n = imp_ref.shape[0]
    tiles = [imp_ref[i * LANES:(i + 1) * LANES, :].T for i in range(n // LANES)]
    imp_t = jnp.concatenate(tiles, axis=1)
    kk = _row_iota(imp_t.shape)
    forced = (kk == 0) | (kk == cur) | (kk == cur - 1)
    exists = kk < n_blk
    score = jnp.where(exists, jnp.where(forced, imp_t + FORCE_SCORE, imp_t), -jnp.inf)
    picked = _select_blocks(score, kk)
    selneg = jnp.where(exists, jnp.where(picked > 0.0, 0.0, NEG), NEG)
    for i in range(n // LANES):
        sel_ref[i * LANES:(i + 1) * LANES, :] = selneg[:, i * LANES:(i + 1) * LANES].T


def _decode_topk(imp2, n_blk, cur):
    return pl.pallas_call(
        functools.partial(_decode_topk_kernel, n_blk=n_blk, cur=cur),
        out_shape=jax.ShapeDtypeStruct(imp2.shape, F32),
        compiler_params=pltpu.CompilerParams(vmem_limit_bytes=VMEM_LIMIT_BYTES),
        name="decode_topk",
    )(imp2)


def _decode_attend(s, q32, vals_t_bf, k_new, v_new, bias_new):
    kn = k_new.astype(BF16).astype(F32)
    s_new = jnp.sum(q32 * kn, axis=1, keepdims=True) + bias_new
    mx = jnp.maximum(jnp.max(s, axis=1, keepdims=True), s_new)
    ex = jnp.exp(s - mx)
    ex_new = jnp.exp(s_new - mx)
    inv = 1.0 / (jnp.sum(ex, axis=1, keepdims=True) + ex_new)
    p = (ex * inv).astype(BF16)
    p_new = (ex_new * inv).astype(BF16).astype(F32)
    return _dot_nt(p, vals_t_bf) + p_new * v_new.astype(BF16).astype(F32)


def _decode_sel_kernel(pt_ref, *refs, n_pages, cur):
    kpages = refs[0:n_pages]
    vpages = refs[n_pages:2 * n_pages]
    (ckw_ref, cvw_ref, q_ref, sel_ref, ksn_ref, vsn_ref, kwn_ref, vwn_ref, kwc_ref, vwc_ref,
     gate_ref, oc_ref, szb_ref, onehot_ref, mix_ref, okw_ref, ovw_ref) = refs[2 * n_pages:]
    del pt_ref
    q = q_ref[0]
    q32 = q.astype(F32)
    rows = q.shape[0]
    sel = sel_ref[0]

    kt = jnp.concatenate([r[0] for r in kpages], axis=1).astype(BF16)
    vt = jnp.concatenate([r[0] for r in vpages], axis=1).astype(BF16)
    kaug = jnp.concatenate([kt, onehot_ref[...]], axis=0)
    qaug = jnp.concatenate([q, sel], axis=1)
    s_sel = _dot(qaug, kaug)
    o_s = _decode_attend(s_sel, q32, vt, ksn_ref[0], vsn_ref[0], sel.astype(F32)[:, cur:cur + 1])

    kwin = ckw_ref[0]
    vwin = cvw_ref[0]
    keep = kwin.shape[1]
    visible = _lane_iota((rows, keep)) > keep - WINDOW
    s_win = jnp.where(visible, _dot(q, kwin.astype(BF16)), NEG)
    o_w = _decode_attend(s_win, q32, vwin.astype(BF16), kwn_ref[0], vwn_ref[0], 0.0)

    g = gate_ref[0]
    o = g[:, 0:1] * oc_ref[0] + g[:, 1:2] * o_s[0:N_HEADS] + g[:, 2:3] * o_w[0:N_HEADS]
    mix_ref[0] = o * szb_ref[0]

    last = _lane_iota((KV_W, keep)) == keep - 1
    okw_ref[0] = jnp.where(last, kwc_ref[0], pltpu.roll(kwin, keep - 1, 1))
    ovw_ref[0] = jnp.where(last, vwc_ref[0], pltpu.roll(vwin, keep - 1, 1))


def _decode_sel(page_table, ks_pool, vs_pool, ckw, cvw, q16, sel16, ksn, vsn, kwn, vwn, kwc, vwc,
                gate3, oc, szb3, onehot, cur):
    nb, n_pages = page_table.shape
    keep = ckw.shape[2]
    page = lambda p: pl.BlockSpec((1, KV_W, PAGE_SIZE), lambda b, pt, p=p: (pt[b, p], 0, 0))
    per_b = lambda a: pl.BlockSpec((1,) + a.shape[1:], lambda b, pt: (b, 0, 0))
    in_specs = ([page(p) for p in range(n_pages)] * 2
                + [per_b(a) for a in (ckw, cvw, q16, sel16, ksn, vsn, kwn, vwn, kwc, vwc,
                                      gate3, oc, szb3)]
                + [pl.BlockSpec(onehot.shape, lambda b, pt: (0, 0))])
    out_specs = [pl.BlockSpec((1, N_HEADS, LANES), lambda b, pt: (b, 0, 0)),
                 pl.BlockSpec((1, KV_W, keep), lambda b, pt: (b, 0, 0)),
                 pl.BlockSpec((1, KV_W, keep), lambda b, pt: (b, 0, 0))]
    grid_spec = pltpu.PrefetchScalarGridSpec(
        num_scalar_prefetch=1, grid=(nb,), in_specs=in_specs, out_specs=out_specs)
    return pl.pallas_call(
        functools.partial(_decode_sel_kernel, n_pages=n_pages, cur=cur),
        grid_spec=grid_spec,
        out_shape=[jax.ShapeDtypeStruct((nb, N_HEADS, LANES), F32),
                   jax.ShapeDtypeStruct((nb, KV_W, keep), F32),
                   jax.ShapeDtypeStruct((nb, KV_W, keep), F32)],
        compiler_params=pltpu.CompilerParams(
            dimension_semantics=("arbitrary",), vmem_limit_bytes=VMEM_LIMIT_BYTES),
        name="decode_sel",
    )(page_table, *([ks_pool] * n_pages), *([vs_pool] * n_pages),
      ckw, cvw, q16, sel16, ksn, vsn, kwn, vwn, kwc, vwc, gate3, oc, szb3, onehot)


def _rope_table(pos):
    half = ROT_DIM // 2
    inv = jnp.power(jnp.float32(ROPE_THETA), -jnp.arange(half, dtype=F32) / half)
    ang = pos.astype(F32)[:, None] * inv[None, :]
    cs = jnp.concatenate([jnp.cos(ang), jnp.sin(ang)], axis=1)
    lane = jnp.arange(3 * LANES)
    seg, l64 = lane // LANES, lane % HEAD_DIM
    src = jnp.where(seg == 0, l64 % half, half + l64 % half)
    sign = jnp.where(seg == 0, l64 < ROT_DIM,
                     jnp.where(seg == 1, l64 < half, (l64 >= half) & (l64 < ROT_DIM))).astype(F32)
    sign = jnp.where(seg == 1, -sign, sign)
    place = (jnp.arange(ROT_DIM)[:, None] == src[None, :]).astype(F32) * sign[None, :]
    ones = ((seg == 0) & (l64 >= ROT_DIM)).astype(F32)
    return jnp.dot(cs, place, precision=lax.Precision.HIGHEST) + ones[None, :]


def _compress_weights(w_c, pe_c):
    w4 = w_c.reshape(2, CMP_STRIDE, HEAD_DIM, HEAD_DIM)
    wb = jnp.einsum('arde,hg->rhdage', w4, jnp.eye(N_KV, dtype=w_c.dtype))
    wb = wb.reshape(CMP_STRIDE * KV_W, 2 * KV_W).astype(BF16)
    pe = jnp.broadcast_to(pe_c.reshape(2, CMP_STRIDE, 1, HEAD_DIM), (2, CMP_STRIDE, N_KV, HEAD_DIM))
    return wb, pe.reshape(2, CMP_STRIDE * KV_W)


def _overlap_matrix(n_rows, n_blk):
    cs = jnp.arange(n_rows, dtype=jnp.int32)[:, None] * CMP_STRIDE
    bs = jnp.arange(LANES, dtype=jnp.int32)[None, :] * SEL_BLOCK
    hit = (cs < bs + SEL_BLOCK) & (cs + CMP_BLOCK > bs) & (jnp.arange(LANES)[None, :] < n_blk)
    return hit.astype(BF16)


def kernel(x_prompt, x_sample, cache_k_cmp, cache_v_cmp, cache_k_sel, cache_v_sel, cache_k_win,
           cache_v_win, page_table, norm_g, w_in, ln_v_g, ln_v_b, w_s, b_s, w_ck, pe_ck, w_cv,
           pe_cv, w_out, final_g):
    depth = norm_g.shape[0]
    assert depth == 1
    bsz, seq, _ = x_prompt.shape
    nb, dec_seq, _ = x_sample.shape
    assert dec_seq == 1 and seq % SEL_TILE == 0
    n_pages = page_table.shape[1]
    past = n_pages * PAGE_SIZE
    keep = cache_k_win.shape[2]
    assert keep == WINDOW and past % SEL_BLOCK == 0

    w = w_in[0]
    w_pad = jnp.concatenate(
        [w[:, :C_G + N_GATE_COLS], jnp.zeros((D_MODEL, LANES - N_GATE_COLS), w.dtype),
         w[:, C_G + N_GATE_COLS:]], axis=1).astype(BF16)
    g_row = norm_g[0][None, :]
    lng = ln_v_g[0][None, :]
    lnb = ln_v_b[0][None, :]
    fg = final_g[None, :]
    wk_c, pek = _compress_weights(w_ck[0], pe_ck[0])
    wv_c, pev = _compress_weights(w_cv[0], pe_cv[0])
    w_o = w_out[0].astype(BF16)
    wo_a, wo_b = w_o[:D_A], w_o[D_A:]

    tril = jnp.tril(jnp.ones((CHUNK, CHUNK), w_s.dtype))
    ws_bf = (w_s[0] * tril).astype(BF16)
    bias_full = jnp.repeat(b_s[0].T, HEAD_DIM, axis=1)
    xp2 = x_prompt.reshape(bsz * seq, D_MODEL)
    (a_out, qt, kc, vc, kc_t, vc_t, ks_t, vs_t, kw_t, vw_t, ksa, kwb, vsb, vwb, gates_t, szb) = _inproj_prompt(
        xp2, g_row, w_pad, lng, lnb, ws_bf, bias_full,
        _rope_table(jnp.arange(seq, dtype=jnp.int32)), seq)
    n_chunks = seq // CMP_STRIDE
    kcc, vcc_t = _compress_prompt(
        kc.reshape(bsz, n_chunks, CMP_STRIDE * KV_W), vc.reshape(bsz, n_chunks, CMP_STRIDE * KV_W),
        wk_c, wv_c, pek, pev,
        _rope_table(jnp.arange(n_chunks, dtype=jnp.int32) * CMP_STRIDE))
    b3 = lambda a: a.reshape(bsz, seq, a.shape[-1])
    mix_p = _attn_prompt(qt, gates_t, b3(szb), kcc, vcc_t,
                         _overlap_matrix(n_chunks, seq // SEL_BLOCK).T,
                         b3(ksa), vsb, b3(kwb), vwb)
    y_prompt = _outproj(xp2, a_out, mix_p.reshape(bsz * seq, D_B), wo_a, wo_b, fg, 512)
    y_prompt = y_prompt.reshape(bsz, seq, D_MODEL)
    st = lambda a: jnp.transpose(a.reshape(a.shape[0], N_KV, HEAD_DIM, a.shape[2]), (0, 3, 1, 2))[None]
    keep_p = min(WINDOW, seq)

    xs2 = x_sample.reshape(nb, D_MODEL)
    wsv = jnp.repeat(w_s[0][:, 0, 0], HEAD_DIM)[None, :]
    bsv = jnp.repeat(b_s[0][:, 0], HEAD_DIM)[None, :]
    (a_s, qa_s, kc_s, vc_s, ks_s, vs_s, kw_s, vw_s, gates_s, szb_s, vn_s) = _inproj_decode(
        xs2, g_row, w_pad, lng, lnb, wsv, bsv,
        _rope_table(jnp.full((nb,), past, dtype=jnp.int32)))
    q16 = jnp.pad(qa_s.reshape(nb, N_HEADS, LANES), ((0, 0), (0, N_HEADS), (0, 0)))
    chunks_per_page = PAGE_SIZE // CMP_STRIDE
    n_pool = cache_k_cmp.shape[1]
    d_chunks = past // CMP_STRIDE
    n_blk = -(-(past + 1) // SEL_BLOCK)
    cur = past // SEL_BLOCK
    oc_s, imp_s = _decode_cmp(
        page_table,
        cache_k_cmp[0].reshape(n_pool, chunks_per_page, CMP_STRIDE * KV_W),
        cache_v_cmp[0].reshape(n_pool, chunks_per_page, CMP_STRIDE * KV_W),
        q16, wk_c, wv_c, pek, pev,
        _rope_table(jnp.arange(d_chunks, dtype=jnp.int32) * CMP_STRIDE),
        _overlap_matrix(d_chunks, n_blk))
    selneg = _decode_topk(imp_s[:, ::GQA, :].reshape(nb * N_KV, LANES), n_blk, cur)
    sel16 = jnp.pad(jnp.repeat(selneg.reshape(nb, N_KV, LANES), GQA, axis=1),
                    ((0, 0), (0, N_HEADS), (0, 0))).astype(BF16)
    tok_blk = jnp.arange(past, dtype=jnp.int32)[None, :] // SEL_BLOCK
    onehot = (tok_blk == jnp.arange(LANES, dtype=jnp.int32)[:, None]).astype(BF16)
    r3 = lambda a: a.reshape(nb, 1, a.shape[-1])
    c3 = lambda a: a.reshape(nb, a.shape[-1], 1)
    kv_t = lambda c: jnp.transpose(c[0], (0, 2, 3, 1)).reshape(c.shape[1], KV_W, c.shape[2])
    mix_s, okw, ovw = _decode_sel(
        page_table, kv_t(cache_k_sel), kv_t(cache_v_sel), kv_t(cache_k_win), kv_t(cache_v_win),
        q16, sel16, r3(ks_s), r3(vs_s), r3(kw_s), r3(vw_s), c3(kw_s), c3(vw_s),
        gates_s[:, :N_GATE_COLS].reshape(nb, N_HEADS, 3), oc_s,
        szb_s.reshape(nb, N_HEADS, LANES), onehot, cur)
    wo_slab = jnp.zeros((N_HEADS, N_KV, HEAD_DIM, D_MODEL), BF16)
    wo_heads = wo_b.reshape(N_HEADS, HEAD_DIM, D_MODEL)
    for e in range(N_HEADS):
        wo_slab = wo_slab.at[e, e // GQA].set(wo_heads[e])
    y_sample = _outproj(xs2, a_s, mix_s.reshape(nb, N_HEADS * LANES), wo_a,
                        wo_slab.reshape(N_HEADS * LANES, D_MODEL), fg, nb)
    y_sample = y_sample.reshape(nb, 1, D_MODEL)
    ss = lambda a: a.reshape(1, nb, 1, N_KV, HEAD_DIM)

    return (y_prompt, y_sample,
            st(kc_t), st(vc_t), st(ks_t), st(vs_t),
            st(kw_t[:, :, seq - keep_p:]), st(vw_t[:, :, seq - keep_p:]),
            ss(kc_s), ss(vc_s), ss(ks_s), ss(vs_s),
            st(okw), st(ovw),
            vn_s.reshape(1, nb, 1, D_A))
```

```python
import functools

import jax
import jax.numpy as jnp
from jax import lax
from jax.experimental import pallas as pl
from jax.experimental.pallas import tpu as pltpu

F32 = jnp.float32
BF16 = jnp.bfloat16

D_MODEL = 1024
HEAD_DIM = 64
D_A = 512
D_B = 512
A_GROUPS = 8
CHUNK = 128
N_HEADS = 8
N_KV = 2
GQA = 4
KV_W = 128
ROT_DIM = 16
ROPE_THETA = 500000.0
CMP_BLOCK = 32
CMP_STRIDE = 16
SEL_BLOCK = 64
SEL_SHIFT = 6
N_SELECT = 16
WINDOW = 512
Q_BLOCK = 128
PAGE_SIZE = 128
NORM_EPS = 1e-6
FORCE_SCORE = 1e4
NEG = -1e30

LANES = 128
VMEM_LIMIT_BYTES = 56 * 1024 * 1024

C_U, C_V, C_ZA, C_Q = 0, 512, 1024, 1536
C_KC, C_VC, C_KS, C_VS, C_KW, C_VW = 2048, 2176, 2304, 2432, 2560, 2688
C_G, C_ZB, C_END = 2816, 2944, 3456
N_GATE_COLS = 3 * N_HEADS

SEL_TILE = 512
WIN_SPAN = WINDOW + Q_BLOCK

_NT = (((1,), (1,)), ((), ()))


def _dot(a, b):
    return jnp.dot(a, b, preferred_element_type=F32)


def _dot_nt(a, b):
    return lax.dot_general(a, b, _NT, preferred_element_type=F32)


def _lane_iota(shape):
    return lax.broadcasted_iota(jnp.int32, shape, len(shape) - 1)


def _row_iota(shape):
    return lax.broadcasted_iota(jnp.int32, shape, len(shape) - 2)


def _rope(x, tab):
    c = tab[:, 0:LANES]
    s1 = tab[:, LANES:2 * LANES]
    s2 = tab[:, 2 * LANES:3 * LANES]
    return x * c + pltpu.roll(x, LANES - ROT_DIM // 2, 1) * s1 + pltpu.roll(x, ROT_DIM // 2, 1) * s2


def _rms_rows(x, g):
    ms = jnp.mean(x * x, axis=-1, keepdims=True)
    return x * lax.rsqrt(ms + NORM_EPS) * g


def _layer_norm_rows(v, g, b):
    mu = jnp.mean(v, axis=-1, keepdims=True)
    vc = v - mu
    var = jnp.mean(vc * vc, axis=-1, keepdims=True)
    return vc * lax.rsqrt(var + NORM_EPS) * g + b


def _head_slabs(x512):
    out = []
    rows = x512.shape[0]
    lane = _lane_iota((rows, LANES))
    lo = lane < HEAD_DIM
    for j in range(4):
        slab = x512[:, j * LANES:(j + 1) * LANES]
        swapped = pltpu.roll(slab, HEAD_DIM, 1)
        if j < 2:
            out.append(jnp.where(lo, slab, 0.0))
            out.append(jnp.where(lo, swapped, 0.0))
        else:
            out.append(jnp.where(lo, 0.0, swapped))
            out.append(jnp.where(lo, 0.0, slab))
    return out


def _project(x_ref, g_ref, w_ref, z_scr):
    hn = _rms_rows(x_ref[...], g_ref[...])
    z_scr[...] = _dot(hn.astype(BF16), w_ref[...])


def _inproj_prompt_kernel(x_ref, g_ref, w_ref, lng_ref, lnb_ref, ws_ref, bias_ref, rope_ref,
                          aout_ref, qt_ref, kc_ref, vc_ref,
                          kct_ref, vct_ref, kst_ref, vst_ref, kwt_ref, vwt_ref,
                          ksa_ref, kwb_ref, vsb_ref, vwb_ref, gate_ref, szb_ref,
                          z_scr, *, tm, tiles_per_batch):
    _project(x_ref, g_ref, w_ref, z_scr)
    tab = rope_ref[...]

    lane = _lane_iota((CHUNK, LANES))
    lo = lane < HEAD_DIM
    for c in range(tm // CHUNK):
        r0 = c * CHUNK
        u = z_scr[r0:r0 + CHUNK, C_U:C_U + D_A]
        v = z_scr[r0:r0 + CHUNK, C_V:C_V + D_A]
        za = z_scr[r0:r0 + CHUNK, C_ZA:C_ZA + D_A]
        vn = _layer_norm_rows(v, lng_ref[...], lnb_ref[...])
        parts = []
        for p in range(A_GROUPS // 2):
            vp = vn[:, p * LANES:(p + 1) * LANES]
            v_lo = jnp.where(lo, vp, 0.0).astype(BF16)
            v_hi = jnp.where(lo, 0.0, vp).astype(BF16)
            parts.append(_dot(ws_ref[2 * p], v_lo) + _dot(ws_ref[2 * p + 1], v_hi))
        s = jnp.concatenate(parts, axis=1) + bias_ref[...]
        aout_ref[r0:r0 + CHUNK, :] = (u * s * jax.nn.silu(za)).astype(BF16)

    q = z_scr[:, C_Q:C_Q + D_B]
    qr = jnp.concatenate(
        [_rope(q[:, j * LANES:(j + 1) * LANES], tab) for j in range(4)], axis=1) * (HEAD_DIM ** -0.5)
    slabs = _head_slabs(qr)
    kc = z_scr[:, C_KC:C_KC + KV_W]
    vc = z_scr[:, C_VC:C_VC + KV_W]
    kc_ref[...] = kc
    vc_ref[...] = vc
    ks = _rope(z_scr[:, C_KS:C_KS + KV_W], tab)
    vs = z_scr[:, C_VS:C_VS + KV_W]
    kw = _rope(z_scr[:, C_KW:C_KW + KV_W], tab)
    vw = z_scr[:, C_VW:C_VW + KV_W]
    gates = jax.nn.sigmoid(z_scr[:, C_G:C_G + LANES])
    pos = (pl.program_id(0) % tiles_per_batch) * tm + _row_iota((tm, LANES))
    onehot = jnp.where((pos >> SEL_SHIFT) == _lane_iota((tm, LANES)), 1.0, 0.0)
    ksa_ref[:, 0:LANES] = ks.astype(BF16)
    ksa_ref[:, LANES:2 * LANES] = onehot.astype(BF16)
    kwb_ref[...] = kw.astype(BF16)
    for j in range(tm // LANES):
        r0, r1 = j * LANES, (j + 1) * LANES
        kct_ref[0, :, r0:r1] = kc[r0:r1].T
        vct_ref[0, :, r0:r1] = vc[r0:r1].T
        kst_ref[0, :, r0:r1] = ks[r0:r1].T
        kwt_ref[0, :, r0:r1] = kw[r0:r1].T
        vs_t, vw_t = vs[r0:r1].T, vw[r0:r1].T
        vst_ref[0, :, r0:r1] = vs_t
        vwt_ref[0, :, r0:r1] = vw_t
        vsb_ref[0, j] = vs_t.astype(BF16)
        vwb_ref[0, j] = vw_t.astype(BF16)
        for e in range(N_HEADS):
            qt_ref[0, j, :, e * Q_BLOCK:(e + 1) * Q_BLOCK] = slabs[e][r0:r1].T.astype(BF16)
        gate_ref[0, j] = gates[r0:r1].T
    szb_ref[...] = jax.nn.silu(z_scr[:, C_ZB:C_ZB + D_B])


def _inproj_prompt(x2, g, w, lng, lnb, ws_bf, bias_full, rope_tab, seq, tm=256):
    n = x2.shape[0]
    tiles_per_batch = seq // tm
    bsz = n // seq
    n_tok_tiles = seq // LANES
    row = lambda i: (i, 0)
    const2 = lambda i: (0, 0)
    rows2d = lambda width, dtype: (jax.ShapeDtypeStruct((n, width), dtype), pl.BlockSpec((tm, width), row))
    kv_t = (jax.ShapeDtypeStruct((bsz, KV_W, seq), F32),
            pl.BlockSpec((1, KV_W, tm), lambda i: (i // tiles_per_batch, 0, i % tiles_per_batch)))
    tiles = lambda r, c, dtype: (
        jax.ShapeDtypeStruct((bsz, n_tok_tiles, r, c), dtype),
        pl.BlockSpec((1, tm // LANES, r, c), lambda i: (i // tiles_per_batch, i % tiles_per_batch, 0, 0)))
    outs = [
        rows2d(D_A, BF16),
        tiles(KV_W, N_HEADS * Q_BLOCK, BF16),
        rows2d(KV_W, F32),
        rows2d(KV_W, F32),
        kv_t, kv_t, kv_t, kv_t, kv_t, kv_t,
        rows2d(2 * KV_W, BF16),
        rows2d(KV_W, BF16),
        tiles(KV_W, LANES, BF16),
        tiles(KV_W, LANES, BF16),
        tiles(LANES, LANES, F32),
        rows2d(D_B, F32),
    ]
    out_shapes = [o[0] for o in outs]
    out_specs = [o[1] for o in outs]
    in_specs = [
        pl.BlockSpec((tm, D_MODEL), row),
        pl.BlockSpec((1, D_MODEL), const2),
        pl.BlockSpec((D_MODEL, C_END), const2),
        pl.BlockSpec((1, D_A), const2),
        pl.BlockSpec((1, D_A), const2),
        pl.BlockSpec((A_GROUPS, CHUNK, CHUNK), lambda i: (0, 0, 0)),
        pl.BlockSpec((CHUNK, D_A), const2),
        pl.BlockSpec((tm, 3 * LANES), lambda i: (i % tiles_per_batch, 0)),
    ]
    return pl.pallas_call(
        functools.partial(_inproj_prompt_kernel, tm=tm, tiles_per_batch=tiles_per_batch),
        grid=(n // tm,),
        in_specs=in_specs,
        out_specs=out_specs,
        out_shape=out_shapes,
        scratch_shapes=[pltpu.VMEM((tm, C_END), F32)],
        compiler_params=pltpu.CompilerParams(
            dimension_semantics=("arbitrary",), vmem_limit_bytes=VMEM_LIMIT_BYTES),
        name="inproj_prompt",
    )(x2, g, w, lng, lnb, ws_bf, bias_full, rope_tab)


def _inproj_decode_kernel(x_ref, g_ref, w_ref, lng_ref, lnb_ref, wsv_ref, bsv_ref, rope_ref,
                          aout_ref, qa_ref, kc_ref, vc_ref, ks_ref, vs_ref, kw_ref, vw_ref,
                          gate_ref, szb_ref, vn_ref, z_scr):
    _project(x_ref, g_ref, w_ref, z_scr)
    tab = rope_ref[...]
    u = z_scr[:, C_U:C_U + D_A]
    v = z_scr[:, C_V:C_V + D_A]
    za = z_scr[:, C_ZA:C_ZA + D_A]
    vn = _layer_norm_rows(v, lng_ref[...], lnb_ref[...])
    vn_ref[...] = vn
    s = vn * wsv_ref[...] + bsv_ref[...]
    aout_ref[...] = (u * s * jax.nn.silu(za)).astype(BF16)

    q = z_scr[:, C_Q:C_Q + D_B]
    qr = jnp.concatenate(
        [_rope(q[:, j * LANES:(j + 1) * LANES], tab) for j in range(4)], axis=1) * (HEAD_DIM ** -0.5)
    for e, slab in enumerate(_head_slabs(qr)):
        qa_ref[:, e * LANES:(e + 1) * LANES] = slab.astype(BF16)
    kc_ref[...] = z_scr[:, C_KC:C_KC + KV_W]
    vc_ref[...] = z_scr[:, C_VC:C_VC + KV_W]
    ks_ref[...] = _rope(z_scr[:, C_KS:C_KS + KV_W], tab)
    vs_ref[...] = z_scr[:, C_VS:C_VS + KV_W]
    kw_ref[...] = _rope(z_scr[:, C_KW:C_KW + KV_W], tab)
    vw_ref[...] = z_scr[:, C_VW:C_VW + KV_W]
    gate_ref[...] = jax.nn.sigmoid(z_scr[:, C_G:C_G + LANES])
    szb = jax.nn.silu(z_scr[:, C_ZB:C_ZB + D_B])
    for e, slab in enumerate(_head_slabs(szb)):
        szb_ref[:, e * LANES:(e + 1) * LANES] = slab


def _inproj_decode(x2, g, w, lng, lnb, wsv, bsv, rope_tab):
    n = x2.shape[0]
    out_shapes = [
        jax.ShapeDtypeStruct((n, D_A), BF16),
        jax.ShapeDtypeStruct((n, N_HEADS * LANES), BF16),
        jax.ShapeDtypeStruct((n, KV_W), F32),
        jax.ShapeDtypeStruct((n, KV_W), F32),
        jax.ShapeDtypeStruct((n, KV_W), F32),
        jax.ShapeDtypeStruct((n, KV_W), F32),
        jax.ShapeDtypeStruct((n, KV_W), F32),
        jax.ShapeDtypeStruct((n, KV_W), F32),
        jax.ShapeDtypeStruct((n, LANES), F32),
        jax.ShapeDtypeStruct((n, N_HEADS * LANES), F32),
        jax.ShapeDtypeStruct((n, D_A), F32),
    ]
    return pl.pallas_call(
        _inproj_decode_kernel,
        out_shape=out_shapes,
        scratch_shapes=[pltpu.VMEM((n, C_END), F32)],
        compiler_params=pltpu.CompilerParams(vmem_limit_bytes=VMEM_LIMIT_BYTES),
        name="inproj_decode",
    )(x2, g, w, lng, lnb, wsv, bsv, rope_tab)


def _compress_chunks(ch, pe_ref, w_ref):
    a = _dot((ch + pe_ref[0:1, :]).astype(BF16), w_ref[:, 0:KV_W])
    b = _dot((ch + pe_ref[1:2, :]).astype(BF16), w_ref[:, KV_W:2 * KV_W])
    return a + pltpu.roll(b, ch.shape[0] - 1, 0)


def _compress_prompt_kernel(kc_ref, vc_ref, wk_ref, wv_ref, pek_ref, pev_ref, rope_ref, kcc_ref, vcct_ref):
    kcc = _compress_chunks(kc_ref[0], pek_ref, wk_ref)
    kcc_ref[0] = _rope(kcc, rope_ref[...]).astype(BF16)
    vcc = _compress_chunks(vc_ref[0], pev_ref, wv_ref)
    for j in range(vcc.shape[0] // LANES):
        vcct_ref[0, :, j * LANES:(j + 1) * LANES] = vcc[j * LANES:(j + 1) * LANES].T.astype(BF16)


def _compress_prompt(kc3, vc3, wk, wv, pek, pev, rope_tab):
    b, c, width = kc3.shape
    blk = pl.BlockSpec((1, c, width), lambda i: (i, 0, 0))
    const2 = lambda i: (0, 0)
    return pl.pallas_call(
        _compress_prompt_kernel,
        grid=(b,),
        in_specs=[blk, blk,
                  pl.BlockSpec(wk.shape, const2), pl.BlockSpec(wv.shape, const2),
                  pl.BlockSpec(pek.shape, const2), pl.BlockSpec(pev.shape, const2),
                  pl.BlockSpec(rope_tab.shape, const2)],
        out_specs=[pl.BlockSpec((1, c, KV_W), lambda i: (i, 0, 0)),
                   pl.BlockSpec((1, KV_W, c), lambda i: (i, 0, 0))],
        out_shape=[jax.ShapeDtypeStruct((b, c, KV_W), BF16), jax.ShapeDtypeStruct((b, KV_W, c), BF16)],
        compiler_params=pltpu.CompilerParams(
            dimension_semantics=("arbitrary",), vmem_limit_bytes=VMEM_LIMIT_BYTES),
        name="compress_prompt",
    )(kc3, vc3, wk, wv, pek, pev, rope_tab)


def _select_blocks(score, kk):
    picked = jnp.zeros(score.shape, F32)
    for _ in range(N_SELECT):
        cm = jnp.max(score, axis=0, keepdims=True)
        first = jnp.min(jnp.where(score == cm, kk, LANES), axis=0, keepdims=True)
        hit = kk == first
        picked = jnp.where(hit, 1.0, picked)
        score = jnp.where(hit, -jnp.inf, score)
    return picked


def _softmax_rows(s):
    mx = jnp.max(s, axis=1, keepdims=True)
    ex = jnp.exp(s - mx)
    return ex * (1.0 / jnp.sum(ex, axis=1, keepdims=True))


def _tile_heads(x):
    return jnp.concatenate([x] * N_HEADS, axis=1)


def _attn_prompt_kernel(qt_ref, gate_ref, szb_ref, kcc_ref, vcct_ref, ovlt_ref,
                        ksa_ref, vsb_ref, kwb_ref, vwb_ref, out_ref,
                        qaug_scr, acc_scr, sa_scr, sb_scr):
    i_blk = pl.program_id(1)
    q0 = i_blk * Q_BLOCK
    cols = N_HEADS * Q_BLOCK
    qt = qt_ref[0, 0]

    n_cmp = kcc_ref.shape[1]
    t_c = q0 + _lane_iota((n_cmp, Q_BLOCK))
    seen = (_row_iota((n_cmp, Q_BLOCK)) * CMP_STRIDE + (CMP_BLOCK - 1)) <= t_c
    s = _dot(kcc_ref[0], qt) + _tile_heads(jnp.where(seen, 0.0, NEG))
    mx = jnp.max(s, axis=0, keepdims=True)
    ex = jnp.exp(s - mx)
    inv = jnp.where(mx > 0.5 * NEG, 1.0 / jnp.sum(ex, axis=0, keepdims=True), 0.0)
    p_c = ex * inv
    o_c = _dot(vcct_ref[0], p_c.astype(BF16))

    kk = _row_iota((LANES, Q_BLOCK))
    tq = q0 + _lane_iota((LANES, Q_BLOCK))
    valid = kk * SEL_BLOCK <= tq
    cur = tq >> SEL_SHIFT
    forced = (kk == 0) | (kk == cur) | (kk == cur - 1)
    scores = []
    for h in range(N_KV):
        c0 = h * GQA * Q_BLOCK
        p_sum = (p_c[:, c0:c0 + Q_BLOCK] + p_c[:, c0 + Q_BLOCK:c0 + 2 * Q_BLOCK]
                 + p_c[:, c0 + 2 * Q_BLOCK:c0 + 3 * Q_BLOCK] + p_c[:, c0 + 3 * Q_BLOCK:c0 + 4 * Q_BLOCK])
        imp_t = _dot(ovlt_ref[...], p_sum.astype(BF16))
        scores.append(jnp.where(valid, jnp.where(forced, imp_t + FORCE_SCORE, imp_t), -jnp.inf))
    picked = _select_blocks(jnp.concatenate(scores, axis=1), jnp.concatenate([kk, kk], axis=1))
    qaug_scr[0:LANES, :] = qt
    for h in range(N_KV):
        pk = picked[:, h * Q_BLOCK:(h + 1) * Q_BLOCK]
        selneg = jnp.where(valid, jnp.where(pk > 0.0, 0.0, NEG), NEG).astype(BF16)
        for g in range(GQA):
            c0 = (h * GQA + g) * Q_BLOCK
            qaug_scr[LANES:2 * LANES, c0:c0 + Q_BLOCK] = selneg

    acc_scr[...] = jnp.zeros(acc_scr.shape, F32)
    tiles_per_step = SEL_TILE // LANES

    def score(step, dst):
        t0 = pl.multiple_of(step * SEL_TILE, SEL_TILE)
        dst[...] = _dot(ksa_ref[0, pl.ds(t0, SEL_TILE), :], qaug_scr[...])

    def attend(src, step, m_prev, l_prev, causal):
        vt = jnp.concatenate([vsb_ref[0, step * tiles_per_step + i] for i in range(tiles_per_step)], axis=1)
        sc = src[...]
        if causal:
            tok = step * SEL_TILE + _row_iota((SEL_TILE, Q_BLOCK))
            sc = sc + _tile_heads(jnp.where(tok <= q0 + _lane_iota((SEL_TILE, Q_BLOCK)), 0.0, NEG))
        m_next = jnp.maximum(m_prev, jnp.max(sc, axis=0, keepdims=True))
        p = jnp.exp(sc - m_next)
        alpha = jnp.exp(m_prev - m_next)
        l_next = alpha * l_prev + jnp.sum(p, axis=0, keepdims=True)
        acc_scr[...] = alpha * acc_scr[...] + _dot(vt, p.astype(BF16))
        return m_next, l_next

    n_full = q0 // SEL_TILE
    score(0, sa_scr)

    def pair(jj, carry):
        m_c, l_c = carry
        score(2 * jj + 1, sb_scr)
        m_c, l_c = attend(sa_scr, 2 * jj, m_c, l_c, False)
        score(2 * jj + 2, sa_scr)
        return attend(sb_scr, 2 * jj + 1, m_c, l_c, False)

    m_run, l_run = lax.fori_loop(
        0, n_full // 2, pair,
        (jnp.full((1, cols), -jnp.inf, F32), jnp.zeros((1, cols), F32)))

    def odd_tail(m_c, l_c):
        score(n_full, sb_scr)
        m_c, l_c = attend(sa_scr, n_full - 1, m_c, l_c, False)
        return attend(sb_scr, n_full, m_c, l_c, True)[1]

    def even_tail(m_c, l_c):
        return attend(sa_scr, n_full, m_c, l_c, True)[1]

    l_run = lax.cond(n_full % 2 == 1, odd_tail, even_tail, m_run, l_run)

    w0 = pl.multiple_of(jnp.maximum(q0 - WINDOW, 0), Q_BLOCK)
    wj0 = jnp.maximum(i_blk - WINDOW // Q_BLOCK, 0)
    kwin = kwb_ref[0, pl.ds(w0, WIN_SPAN), :]
    vwt = jnp.concatenate([vwb_ref[0, wj0 + i] for i in range(WIN_SPAN // LANES)], axis=1)
    kp = w0 + _row_iota((WIN_SPAN, Q_BLOCK))
    tw = q0 + _lane_iota((WIN_SPAN, Q_BLOCK))
    in_win = jnp.where(kp <= tw, jnp.where(kp > tw - WINDOW, 0.0, NEG), NEG)
    sw = _dot(kwin, qt) + _tile_heads(in_win)
    exw = jnp.exp(sw - jnp.max(sw, axis=0, keepdims=True))
    o_w = _dot(vwt, exw.astype(BF16)) * (1.0 / jnp.sum(exw, axis=0, keepdims=True))

    gt = gate_ref[0, 0]
    o_s = acc_scr[...] * (1.0 / l_run)
    heads = []
    for e in range(N_HEADS):
        c0 = e * Q_BLOCK
        o_e = (gt[3 * e:3 * e + 1, :] * o_c[:, c0:c0 + Q_BLOCK]
               + gt[3 * e + 1:3 * e + 2, :] * o_s[:, c0:c0 + Q_BLOCK]
               + gt[3 * e + 2:3 * e + 3, :] * o_w[:, c0:c0 + Q_BLOCK])
        heads.append(o_e.T)
    lo = _lane_iota((Q_BLOCK, LANES)) < HEAD_DIM
    for j in range(N_HEADS // 2):
        a, b = heads[2 * j], heads[2 * j + 1]
        if j < 2:
            slab = jnp.where(lo, a, pltpu.roll(b, HEAD_DIM, 1))
        else:
            slab = jnp.where(lo, pltpu.roll(a, HEAD_DIM, 1), b)
        out_ref[0, :, j * LANES:(j + 1) * LANES] = (
            slab * szb_ref[0, :, j * LANES:(j + 1) * LANES]).astype(BF16)


def _attn_prompt(qt4, gate4, szb3, kcc, vcct, ovlt, ksa3, vsb4, kwb3, vwb4):
    b, nq = qt4.shape[0], qt4.shape[1]
    t = nq * Q_BLOCK
    cols = N_HEADS * Q_BLOCK
    qtile = lambda a: pl.BlockSpec((1, 1) + a.shape[2:], lambda bi, i: (bi, i, 0, 0))
    seq = lambda a: pl.BlockSpec((1,) + a.shape[1:], lambda bi, i, nd=a.ndim: (bi,) + (0,) * (nd - 1))
    return pl.pallas_call(
        _attn_prompt_kernel,
        grid=(b, nq),
        in_specs=[qtile(qt4), qtile(gate4), pl.BlockSpec((1, Q_BLOCK, D_B), lambda bi, i: (bi, i, 0)),
                  seq(kcc), seq(vcct), pl.BlockSpec(ovlt.shape, lambda bi, i: (0, 0)),
                  seq(ksa3), seq(vsb4), seq(kwb3), seq(vwb4)],
        out_specs=pl.BlockSpec((1, Q_BLOCK, D_B), lambda bi, i: (bi, i, 0)),
        out_shape=jax.ShapeDtypeStruct((b, t, D_B), BF16),
        scratch_shapes=[pltpu.VMEM((2 * LANES, cols), BF16),
                        pltpu.VMEM((KV_W, cols), F32),
                        pltpu.VMEM((SEL_TILE, cols), F32),
                        pltpu.VMEM((SEL_TILE, cols), F32)],
        compiler_params=pltpu.CompilerParams(
            dimension_semantics=("arbitrary", "arbitrary"), vmem_limit_bytes=VMEM_LIMIT_BYTES),
        name="attn_prompt",
    )(qt4, gate4, szb3, kcc, vcct, ovlt, ksa3, vsb4, kwb3, vwb4)


def _outproj_kernel(x_ref, a_ref, m_ref, wa_ref, wb_ref, fg_ref, y_ref):
    y = x_ref[...] + _dot(a_ref[...], wa_ref[...]) + _dot(m_ref[...].astype(BF16), wb_ref[...])
    y_ref[...] = _rms_rows(y, fg_ref[...])


def _outproj(x2, a_out, mix_b, wa, wb, fg, tm):
    n = x2.shape[0]
    row = lambda i: (i, 0)
    const2 = lambda i: (0, 0)
    return pl.pallas_call(
        _outproj_kernel,
        grid=(n // tm,),
        in_specs=[pl.BlockSpec((tm, D_MODEL), row),
                  pl.BlockSpec((tm, a_out.shape[1]), row),
                  pl.BlockSpec((tm, mix_b.shape[1]), row),
                  pl.BlockSpec(wa.shape, const2), pl.BlockSpec(wb.shape, const2),
                  pl.BlockSpec((1, D_MODEL), const2)],
        out_specs=pl.BlockSpec((tm, D_MODEL), row),
        out_shape=jax.ShapeDtypeStruct((n, D_MODEL), F32),
        compiler_params=pltpu.CompilerParams(
            dimension_semantics=("arbitrary",), vmem_limit_bytes=VMEM_LIMIT_BYTES),
        name="outproj",
    )(x2, a_out, mix_b, wa, wb, fg)


def _decode_cmp_kernel(pt_ref, *refs, n_pages):
    kpages = refs[0:n_pages]
    vpages = refs[n_pages:2 * n_pages]
    (q_ref, wk_ref, wv_ref, pek_ref, pev_ref, rope_ref, ovl_ref, oc_ref, psum_ref,
     ktok_scr, vtok_scr) = refs[2 * n_pages:]
    del pt_ref

    def chunk_rows(pages, tok_scr):
        for p, page in enumerate(pages):
            tok_scr[p * PAGE_SIZE:(p + 1) * PAGE_SIZE, :] = page[0].T
        n = tok_scr.shape[0] // CMP_STRIDE
        return jnp.concatenate(
            [tok_scr[pl.ds(r, n, stride=CMP_STRIDE), :] for r in range(CMP_STRIDE)], axis=1)

    kch = chunk_rows(kpages, ktok_scr)
    vch = chunk_rows(vpages, vtok_scr)
    kcc = _rope(_compress_chunks(kch, pek_ref, wk_ref), rope_ref[...]).astype(BF16)
    vcc = _compress_chunks(vch, pev_ref, wv_ref).astype(BF16)
    n_chunks = kch.shape[0]
    q = q_ref[0]
    rows = q.shape[0]
    s = _dot_nt(q, kcc)
    seen = _lane_iota((rows, n_chunks)) < n_chunks - 1
    p_c = jnp.where(seen, _softmax_rows(jnp.where(seen, s, NEG)), 0.0)
    oc_ref[0] = _dot(p_c.astype(BF16), vcc)[0:N_HEADS]
    ps0 = jnp.sum(p_c[0:GQA], axis=0, keepdims=True)
    ps1 = jnp.sum(p_c[GQA:2 * GQA], axis=0, keepdims=True)
    psum = jnp.where(_row_iota((rows, n_chunks)) < N_HEADS // 2, ps0, ps1)
    psum_ref[0] = _dot(psum.astype(BF16), ovl_ref[...])[0:N_HEADS]


def _decode_cmp(page_table, ck_pool, cv_pool, q16, wk, wv, pek, pev, rope_tab, ovl):
    nb, n_pages = page_table.shape
    page = lambda p: pl.BlockSpec((1, KV_W, PAGE_SIZE), lambda b, pt, p=p: (pt[b, p], 0, 0))
    tok_buf = pltpu.VMEM((n_pages * PAGE_SIZE, KV_W), F32)
    const2 = lambda b, pt: (0, 0)
    per_b = lambda a: pl.BlockSpec((1,) + a.shape[1:], lambda b, pt: (b, 0, 0))
    in_specs = ([page(p) for p in range(n_pages)] * 2
                + [per_b(q16),
                   pl.BlockSpec(wk.shape, const2), pl.BlockSpec(wv.shape, const2),
                   pl.BlockSpec(pek.shape, const2), pl.BlockSpec(pev.shape, const2),
                   pl.BlockSpec(rope_tab.shape, const2), pl.BlockSpec(ovl.shape, const2)])
    out_blk = pl.BlockSpec((1, N_HEADS, LANES), lambda b, pt: (b, 0, 0))
    grid_spec = pltpu.PrefetchScalarGridSpec(
        num_scalar_prefetch=1, grid=(nb,), in_specs=in_specs, out_specs=[out_blk, out_blk],
        scratch_shapes=[tok_buf, tok_buf])
    return pl.pallas_call(
        functools.partial(_decode_cmp_kernel, n_pages=n_pages),
        grid_spec=grid_spec,
        out_shape=[jax.ShapeDtypeStruct((nb, N_HEADS, LANES), F32)] * 2,
        compiler_params=pltpu.CompilerParams(
            dimension_semantics=("arbitrary",), vmem_limit_bytes=VMEM_LIMIT_BYTES),
        name="decode_cmp",
    )(page_table, *([ck_pool] * n_pages), *([cv_pool] * n_pages),
      q16, wk, wv, pek, pev, rope_tab, ovl)


def _decode_topk_kernel(imp_ref, sel_ref, *, n_blk, cur):
    n = imp_ref.shape[0]
    tiles = [imp_ref[i * LANES:(i + 1) * LANES, :].T for i in range(n // LANES)]
    imp_t = jnp.concatenate(tiles, axis=1)
    kk = _row_iota(imp_t.shape)
    forced = (kk == 0) | (kk == cur) | (kk == cur - 1)
    exists = kk < n_blk
    score = jnp.where(exists, jnp.where(forced, imp_t + FORCE_SCORE, imp_t), -jnp.inf)
    picked = _select_blocks(score, kk)
    selneg = jnp.where(exists, jnp.where(picked > 0.0, 0.0, NEG), NEG)
    for i in range(n // LANES):
        sel_ref[i * LANES:(i + 1) * LANES, :] = selneg[:, i * LANES:(i + 1) * LANES].T


def _decode_topk(imp2, n_blk, cur):
    return pl.pallas_call(
        functools.partial(_decode_topk_kernel, n_blk=n_blk, cur=cur),
        out_shape=jax.ShapeDtypeStruct(imp2.shape, F32),
        compiler_params=pltpu.CompilerParams(vmem_limit_bytes=VMEM_LIMIT_BYTES),
        name="decode_topk",
    )(imp2)


def _decode_attend(s, q32, vals_t_bf, k_new, v_new, bias_new):
    kn = k_new.astype(BF16).astype(F32)
    s_new = jnp.sum(q32 * kn, axis=1, keepdims=True) + bias_new
    mx = jnp.maximum(jnp.max(s, axis=1, keepdims=True), s_new)
    ex = jnp.exp(s - mx)
    ex_new = jnp.exp(s_new - mx)
    inv = 1.0 / (jnp.sum(ex, axis=1, keepdims=True) + ex_new)
    p = (ex * inv).astype(BF16)
    p_new = (ex_new * inv).astype(BF16).astype(F32)
    return _dot_nt(p, vals_t_bf) + p_new * v_new.astype(BF16).astype(F32)


def _decode_sel_kernel(pt_ref, *refs, n_pages, cur):
    kpages = refs[0:n_pages]
    vpages = refs[n_pages:2 * n_pages]
    (ckw_ref, cvw_ref, q_ref, sel_ref, ksn_ref, vsn_ref, kwn_ref, vwn_ref, kwc_ref, vwc_ref,
     gate_ref, oc_ref, szb_ref, onehot_ref, mix_ref, okw_ref, ovw_ref) = refs[2 * n_pages:]
    del pt_ref
    q = q_ref[0]
    q32 = q.astype(F32)
    rows = q.shape[0]
    sel = sel_ref[0]

    kt = jnp.concatenate([r[0] for r in kpages], axis=1).astype(BF16)
    vt = jnp.concatenate([r[0] for r in vpages], axis=1).astype(BF16)
    kaug = jnp.concatenate([kt, onehot_ref[...]], axis=0)
    qaug = jnp.concatenate([q, sel], axis=1)
    s_sel = _dot(qaug, kaug)
    o_s = _decode_attend(s_sel, q32, vt, ksn_ref[0], vsn_ref[0], sel.astype(F32)[:, cur:cur + 1])

    kwin = ckw_ref[0]
    vwin = cvw_ref[0]
    keep = kwin.shape[1]
    visible = _lane_iota((rows, keep)) > keep - WINDOW
    s_win = jnp.where(visible, _dot(q, kwin.astype(BF16)), NEG)
    o_w = _decode_attend(s_win, q32, vwin.astype(BF16), kwn_ref[0], vwn_ref[0], 0.0)

    g = gate_ref[0]
    o = g[:, 0:1] * oc_ref[0] + g[:, 1:2] * o_s[0:N_HEADS] + g[:, 2:3] * o_w[0:N_HEADS]
    mix_ref[0] = o * szb_ref[0]

    last = _lane_iota((KV_W, keep)) == keep - 1
    okw_ref[0] = jnp.where(last, kwc_ref[0], pltpu.roll(kwin, keep - 1, 1))
    ovw_ref[0] = jnp.where(last, vwc_ref[0], pltpu.roll(vwin, keep - 1, 1))


def _decode_sel(page_table, ks_pool, vs_pool, ckw, cvw, q16, sel16, ksn, vsn, kwn, vwn, kwc, vwc,
                gate3, oc, szb3, onehot, cur):
    nb, n_pages = page_table.shape
    keep = ckw.shape[2]
    page = lambda p: pl.BlockSpec((1, KV_W, PAGE_SIZE), lambda b, pt, p=p: (pt[b, p], 0, 0))
    per_b = lambda a: pl.BlockSpec((1,) + a.shape[1:], lambda b, pt: (b, 0, 0))
    in_specs = ([page(p) for p in range(n_pages)] * 2
                + [per_b(a) for a in (ckw, cvw, q16, sel16, ksn, vsn, kwn, vwn, kwc, vwc,
                                      gate3, oc, szb3)]
                + [pl.BlockSpec(onehot.shape, lambda b, pt: (0, 0))])
    out_specs = [pl.BlockSpec((1, N_HEADS, LANES), lambda b, pt: (b, 0, 0)),
                 pl.BlockSpec((1, KV_W, keep), lambda b, pt: (b, 0, 0)),
                 pl.BlockSpec((1, KV_W, keep), lambda b, pt: (b, 0, 0))]
    grid_spec = pltpu.PrefetchScalarGridSpec(
        num_scalar_prefetch=1, grid=(nb,), in_specs=in_specs, out_specs=out_specs)
    return pl.pallas_call(
        functools.partial(_decode_sel_kernel, n_pages=n_pages, cur=cur),
        grid_spec=grid_spec,
        out_shape=[jax.ShapeDtypeStruct((nb, N_HEADS, LANES), F32),
                   jax.ShapeDtypeStruct((nb, KV_W, keep), F32),
                   jax.ShapeDtypeStruct((nb, KV_W, keep), F32)],
        compiler_params=pltpu.CompilerParams(
            dimension_semantics=("arbitrary",), vmem_limit_bytes=VMEM_LIMIT_BYTES),
        name="decode_sel",
    )(page_table, *([ks_pool] * n_pages), *([vs_pool] * n_pages),
      ckw, cvw, q16, sel16, ksn, vsn, kwn, vwn, kwc, vwc, gate3, oc, szb3, onehot)


def _rope_table(pos):
    half = ROT_DIM // 2
    inv = jnp.power(jnp.float32(ROPE_THETA), -jnp.arange(half, dtype=F32) / half)
    ang = pos.astype(F32)[:, None] * inv[None, :]
    cs = jnp.concatenate([jnp.cos(ang), jnp.sin(ang)], axis=1)
    lane = jnp.arange(3 * LANES)
    seg, l64 = lane // LANES, lane % HEAD_DIM
    src = jnp.where(seg == 0, l64 % half, half + l64 % half)
    sign = jnp.where(seg == 0, l64 < ROT_DIM,
                     jnp.where(seg == 1, l64 < half, (l64 >= half) & (l64 < ROT_DIM))).astype(F32)
    sign = jnp.where(seg == 1, -sign, sign)
    place = (jnp.arange(ROT_DIM)[:, None] == src[None, :]).astype(F32) * sign[None, :]
    ones = ((seg == 0) & (l64 >= ROT_DIM)).astype(F32)
    return jnp.dot(cs, place, precision=lax.Precision.HIGHEST) + ones[None, :]


def _compress_weights(w_c, pe_c):
    w4 = w_c.reshape(2, CMP_STRIDE, HEAD_DIM, HEAD_DIM)
    wb = jnp.einsum('arde,hg->rhdage', w4, jnp.eye(N_KV, dtype=w_c.dtype))
    wb = wb.reshape(CMP_STRIDE * KV_W, 2 * KV_W).astype(BF16)
    pe = jnp.broadcast_to(pe_c.reshape(2, CMP_STRIDE, 1, HEAD_DIM), (2, CMP_STRIDE, N_KV, HEAD_DIM))
    return wb, pe.reshape(2, CMP_STRIDE * KV_W)


def _overlap_matrix(n_rows, n_blk):
    cs = jnp.arange(n_rows, dtype=jnp.int32)[:, None] * CMP_STRIDE
    bs = jnp.arange(LANES, dtype=jnp.int32)[None, :] * SEL_BLOCK
    hit = (cs < bs + SEL_BLOCK) & (cs + CMP_BLOCK > bs) & (jnp.arange(LANES)[None, :] < n_blk)
    return hit.astype(BF16)


def kernel(x_prompt, x_sample, cache_k_cmp, cache_v_cmp, cache_k_sel, cache_v_sel, cache_k_win,
           cache_v_win, page_table, norm_g, w_in, ln_v_g, ln_v_b, w_s, b_s, w_ck, pe_ck, w_cv,
           pe_cv, w_out, final_g):
    depth = norm_g.shape[0]
    assert depth == 1
    bsz, seq, _ = x_prompt.shape
    nb, dec_seq, _ = x_sample.shape
    assert dec_seq == 1 and seq % SEL_TILE == 0 and seq >= WIN_SPAN and (seq // CMP_STRIDE) % LANES == 0
    n_pages = page_table.shape[1]
    past = n_pages * PAGE_SIZE
    keep = cache_k_win.shape[2]
    assert keep == WINDOW and past % SEL_BLOCK == 0

    w = w_in[0]
    w_pad = jnp.concatenate(
        [w[:, :C_G + N_GATE_COLS], jnp.zeros((D_MODEL, LANES - N_GATE_COLS), w.dtype),
         w[:, C_G + N_GATE_COLS:]], axis=1).astype(BF16)
    g_row = norm_g[0][None, :]
    lng = ln_v_g[0][None, :]
    lnb = ln_v_b[0][None, :]
    fg = final_g[None, :]
    wk_c, pek = _compress_weights(w_ck[0], pe_ck[0])
    wv_c, pev = _compress_weights(w_cv[0], pe_cv[0])
    w_o = w_out[0].astype(BF16)
    wo_a, wo_b = w_o[:D_A], w_o[D_A:]

    tril = jnp.tril(jnp.ones((CHUNK, CHUNK), w_s.dtype))
    ws_bf = (w_s[0] * tril).astype(BF16)
    bias_full = jnp.repeat(b_s[0].T, HEAD_DIM, axis=1)
    xp2 = x_prompt.reshape(bsz * seq, D_MODEL)
    (a_out, qt, kc, vc, kc_t, vc_t, ks_t, vs_t, kw_t, vw_t, ksa, kwb, vsb, vwb, gates_t, szb) = _inproj_prompt(
        xp2, g_row, w_pad, lng, lnb, ws_bf, bias_full,
        _rope_table(jnp.arange(seq, dtype=jnp.int32)), seq)
    n_chunks = seq // CMP_STRIDE
    kcc, vcc_t = _compress_prompt(
        kc.reshape(bsz, n_chunks, CMP_STRIDE * KV_W), vc.reshape(bsz, n_chunks, CMP_STRIDE * KV_W),
        wk_c, wv_c, pek, pev,
        _rope_table(jnp.arange(n_chunks, dtype=jnp.int32) * CMP_STRIDE))
    b3 = lambda a: a.reshape(bsz, seq, a.shape[-1])
    mix_p = _attn_prompt(qt, gates_t, b3(szb), kcc, vcc_t,
                         _overlap_matrix(n_chunks, seq // SEL_BLOCK).T,
                         b3(ksa), vsb, b3(kwb), vwb)
    y_prompt = _outproj(xp2, a_out, mix_p.reshape(bsz * seq, D_B), wo_a, wo_b, fg, 512)
    y_prompt = y_prompt.reshape(bsz, seq, D_MODEL)
    st = lambda a: jnp.transpose(a.reshape(a.shape[0], N_KV, HEAD_DIM, a.shape[2]), (0, 3, 1, 2))[None]
    keep_p = min(WINDOW, seq)

    xs2 = x_sample.reshape(nb, D_MODEL)
    wsv = jnp.repeat(w_s[0][:, 0, 0], HEAD_DIM)[None, :]
    bsv = jnp.repeat(b_s[0][:, 0], HEAD_DIM)[None, :]
    (a_s, qa_s, kc_s, vc_s, ks_s, vs_s, kw_s, vw_s, gates_s, szb_s, vn_s) = _inproj_decode(
        xs2, g_row, w_pad, lng, lnb, wsv, bsv,
        _rope_table(jnp.full((nb,), past, dtype=jnp.int32)))
    q16 = jnp.pad(qa_s.reshape(nb, N_HEADS, LANES), ((0, 0), (0, N_HEADS), (0, 0)))
    d_chunks = past // CMP_STRIDE
    n_blk = -(-(past + 1) // SEL_BLOCK)
    cur = past // SEL_BLOCK
    kv_t = lambda c: jnp.transpose(c[0], (0, 2, 3, 1)).reshape(c.shape[1], KV_W, c.shape[2])
    oc_s, imp_s = _decode_cmp(
        page_table, kv_t(cache_k_cmp), kv_t(cache_v_cmp),
        q16, wk_c, wv_c, pek, pev,
        _rope_table(jnp.arange(d_chunks, dtype=jnp.int32) * CMP_STRIDE),
        _overlap_matrix(d_chunks, n_blk))
    selneg = _decode_topk(imp_s[:, ::GQA, :].reshape(nb * N_KV, LANES), n_blk, cur)
    sel16 = jnp.pad(jnp.repeat(selneg.reshape(nb, N_KV, LANES), GQA, axis=1),
                    ((0, 0), (0, N_HEADS), (0, 0))).astype(BF16)
    tok_blk = jnp.arange(past, dtype=jnp.int32)[None, :] // SEL_BLOCK
    onehot = (tok_blk == jnp.arange(LANES, dtype=jnp.int32)[:, None]).astype(BF16)
    r3 = lambda a: a.reshape(nb, 1, a.shape[-1])
    c3 = lambda a: a.reshape(nb, a.shape[-1], 1)
    mix_s, okw, ovw = _decode_sel(
        page_table, kv_t(cache_k_sel), kv_t(cache_v_sel), kv_t(cache_k_win), kv_t(cache_v_win),
        q16, sel16, r3(ks_s), r3(vs_s), r3(kw_s), r3(vw_s), c3(kw_s), c3(vw_s),
        gates_s[:, :N_GATE_COLS].reshape(nb, N_HEADS, 3), oc_s,
        szb_s.reshape(nb, N_HEADS, LANES), onehot, cur)
    wo_slab = jnp.zeros((N_HEADS, N_KV, HEAD_DIM, D_MODEL), BF16)
    wo_heads = wo_b.reshape(N_HEADS, HEAD_DIM, D_MODEL)
    for e in range(N_HEADS):
        wo_slab = wo_slab.at[e, e // GQA].set(wo_heads[e])
    y_sample = _outproj(xs2, a_s, mix_s.reshape(nb, N_HEADS * LANES), wo_a,
                        wo_slab.reshape(N_HEADS * LANES, D_MODEL), fg, nb)
    y_sample = y_sample.reshape(nb, 1, D_MODEL)
    ss = lambda a: a.reshape(1, nb, 1, N_KV, HEAD_DIM)

    return (y_prompt, y_sample,
            st(kc_t), st(vc_t), st(ks_t), st(vs_t),
            st(kw_t[:, :, seq - keep_p:]), st(vw_t[:, :, seq - keep_p:]),
            ss(kc_s), ss(vc_s), ss(ks_s), ss(vs_s),
            st(okw), st(ovw),
            vn_s.reshape(1, nb, 1, D_A))
```

```python
import functools

import jax
import jax.numpy as jnp
from jax import lax
from jax.experimental import pallas as pl
from jax.experimental.pallas import tpu as pltpu

F32 = jnp.float32
BF16 = jnp.bfloat16

D_MODEL = 1024
HEAD_DIM = 64
D_A = 512
D_B = 512
A_GROUPS = 8
CHUNK = 128
N_HEADS = 8
N_KV = 2
GQA = 4
KV_W = 128
ROT_DIM = 16
ROPE_THETA = 500000.0
CMP_BLOCK = 32
CMP_STRIDE = 16
SEL_BLOCK = 64
SEL_SHIFT = 6
N_SELECT = 16
WINDOW = 512
Q_BLOCK = 128
PAGE_SIZE = 128
NORM_EPS = 1e-6
FORCE_SCORE = 1e4
NEG = -1e30
LOG2_E = 1.4426950408889634

LANES = 128
VMEM_LIMIT_BYTES = 56 * 1024 * 1024

C_U, C_V, C_ZA, C_Q = 0, 512, 1024, 1536
C_KC, C_VC, C_KS, C_VS, C_KW, C_VW = 2048, 2176, 2304, 2432, 2560, 2688
C_G, C_ZB, C_END = 2816, 2944, 3456
N_GATE_COLS = 3 * N_HEADS

SEL_TILE = 512
WIN_SPAN = WINDOW + Q_BLOCK

_NT = (((1,), (1,)), ((), ()))


def _dot(a, b):
    return jnp.dot(a, b, preferred_element_type=F32)


def _dot_nt(a, b):
    return lax.dot_general(a, b, _NT, preferred_element_type=F32)


def _lane_iota(shape):
    return lax.broadcasted_iota(jnp.int32, shape, len(shape) - 1)


def _row_iota(shape):
    return lax.broadcasted_iota(jnp.int32, shape, len(shape) - 2)


def _rope(x, tab):
    c = tab[:, 0:LANES]
    s1 = tab[:, LANES:2 * LANES]
    s2 = tab[:, 2 * LANES:3 * LANES]
    return x * c + pltpu.roll(x, LANES - ROT_DIM // 2, 1) * s1 + pltpu.roll(x, ROT_DIM // 2, 1) * s2


def _rms_rows(x, g):
    ms = jnp.mean(x * x, axis=-1, keepdims=True)
    return x * lax.rsqrt(ms + NORM_EPS) * g


def _layer_norm_rows(v, g, b):
    mu = jnp.mean(v, axis=-1, keepdims=True)
    vc = v - mu
    var = jnp.mean(vc * vc, axis=-1, keepdims=True)
    return vc * lax.rsqrt(var + NORM_EPS) * g + b


def _head_slabs(x512):
    out = []
    rows = x512.shape[0]
    lane = _lane_iota((rows, LANES))
    lo = lane < HEAD_DIM
    for j in range(4):
        slab = x512[:, j * LANES:(j + 1) * LANES]
        swapped = pltpu.roll(slab, HEAD_DIM, 1)
        if j < 2:
            out.append(jnp.where(lo, slab, 0.0))
            out.append(jnp.where(lo, swapped, 0.0))
        else:
            out.append(jnp.where(lo, 0.0, swapped))
            out.append(jnp.where(lo, 0.0, slab))
    return out


def _project(x_ref, g_ref, w_ref, z_scr):
    hn = _rms_rows(x_ref[...], g_ref[...])
    z_scr[...] = _dot(hn.astype(BF16), w_ref[...])


def _inproj_prompt_kernel(x_ref, g_ref, w_ref, lng_ref, lnb_ref, ws_ref, bias_ref, rope_ref,
                          aout_ref, qt_ref, kc_ref, vc_ref,
                          kct_ref, vct_ref, kst_ref, vst_ref, kwt_ref, vwt_ref,
                          ksa_ref, kwb_ref, vsb_ref, vwb_ref, gate_ref, szb_ref,
                          z_scr, *, tm, tiles_per_batch):
    _project(x_ref, g_ref, w_ref, z_scr)
    tab = rope_ref[...]

    lane = _lane_iota((CHUNK, LANES))
    lo = lane < HEAD_DIM
    for c in range(tm // CHUNK):
        r0 = c * CHUNK
        u = z_scr[r0:r0 + CHUNK, C_U:C_U + D_A]
        v = z_scr[r0:r0 + CHUNK, C_V:C_V + D_A]
        za = z_scr[r0:r0 + CHUNK, C_ZA:C_ZA + D_A]
        vn = _layer_norm_rows(v, lng_ref[...], lnb_ref[...])
        parts = []
        for p in range(A_GROUPS // 2):
            vp = vn[:, p * LANES:(p + 1) * LANES]
            v_lo = jnp.where(lo, vp, 0.0).astype(BF16)
            v_hi = jnp.where(lo, 0.0, vp).astype(BF16)
            parts.append(_dot(ws_ref[2 * p], v_lo) + _dot(ws_ref[2 * p + 1], v_hi))
        s = jnp.concatenate(parts, axis=1) + bias_ref[...]
        aout_ref[r0:r0 + CHUNK, :] = (u * s * jax.nn.silu(za)).astype(BF16)

    q = z_scr[:, C_Q:C_Q + D_B]
    qr = jnp.concatenate(
        [_rope(q[:, j * LANES:(j + 1) * LANES], tab) for j in range(4)], axis=1) * (HEAD_DIM ** -0.5 * LOG2_E)
    slabs = _head_slabs(qr)
    kc = z_scr[:, C_KC:C_KC + KV_W]
    vc = z_scr[:, C_VC:C_VC + KV_W]
    kc_ref[...] = kc
    vc_ref[...] = vc
    ks = _rope(z_scr[:, C_KS:C_KS + KV_W], tab)
    vs = z_scr[:, C_VS:C_VS + KV_W]
    kw = _rope(z_scr[:, C_KW:C_KW + KV_W], tab)
    vw = z_scr[:, C_VW:C_VW + KV_W]
    gates = jax.nn.sigmoid(z_scr[:, C_G:C_G + LANES])
    pos = (pl.program_id(0) % tiles_per_batch) * tm + _row_iota((tm, LANES))
    onehot = jnp.where((pos >> SEL_SHIFT) == _lane_iota((tm, LANES)), 1.0, 0.0)
    ksa_ref[:, 0:LANES] = ks.astype(BF16)
    ksa_ref[:, LANES:2 * LANES] = onehot.astype(BF16)
    kwb_ref[...] = kw.astype(BF16)
    for j in range(tm // LANES):
        r0, r1 = j * LANES, (j + 1) * LANES
        kct_ref[0, :, r0:r1] = kc[r0:r1].T
        vct_ref[0, :, r0:r1] = vc[r0:r1].T
        kst_ref[0, :, r0:r1] = ks[r0:r1].T
        kwt_ref[0, :, r0:r1] = kw[r0:r1].T
        vs_t, vw_t = vs[r0:r1].T, vw[r0:r1].T
        vst_ref[0, :, r0:r1] = vs_t
        vwt_ref[0, :, r0:r1] = vw_t
        vsb_ref[0, j] = vs_t.astype(BF16)
        vwb_ref[0, j] = vw_t.astype(BF16)
        for e in range(N_HEADS):
            qt_ref[0, j, :, e * Q_BLOCK:(e + 1) * Q_BLOCK] = slabs[e][r0:r1].T.astype(BF16)
        gate_ref[0, j] = gates[r0:r1].T
    szb_ref[...] = jax.nn.silu(z_scr[:, C_ZB:C_ZB + D_B])


def _inproj_prompt(x2, g, w, lng, lnb, ws_bf, bias_full, rope_tab, seq, tm=256):
    n = x2.shape[0]
    tiles_per_batch = seq // tm
    bsz = n // seq
    n_tok_tiles = seq // LANES
    row = lambda i: (i, 0)
    const2 = lambda i: (0, 0)
    rows2d = lambda width, dtype: (jax.ShapeDtypeStruct((n, width), dtype), pl.BlockSpec((tm, width), row))
    kv_t = (jax.ShapeDtypeStruct((bsz, KV_W, seq), F32),
            pl.BlockSpec((1, KV_W, tm), lambda i: (i // tiles_per_batch, 0, i % tiles_per_batch)))
    tiles = lambda r, c, dtype: (
        jax.ShapeDtypeStruct((bsz, n_tok_tiles, r, c), dtype),
        pl.BlockSpec((1, tm // LANES, r, c), lambda i: (i // tiles_per_batch, i % tiles_per_batch, 0, 0)))
    outs = [
        rows2d(D_A, BF16),
        tiles(KV_W, N_HEADS * Q_BLOCK, BF16),
        rows2d(KV_W, F32),
        rows2d(KV_W, F32),
        kv_t, kv_t, kv_t, kv_t, kv_t, kv_t,
        rows2d(2 * KV_W, BF16),
        rows2d(KV_W, BF16),
        tiles(KV_W, LANES, BF16),
        tiles(KV_W, LANES, BF16),
        tiles(LANES, LANES, F32),
        rows2d(D_B, F32),
    ]
    out_shapes = [o[0] for o in outs]
    out_specs = [o[1] for o in outs]
    in_specs = [
        pl.BlockSpec((tm, D_MODEL), row),
        pl.BlockSpec((1, D_MODEL), const2),
        pl.BlockSpec((D_MODEL, C_END), const2),
        pl.BlockSpec((1, D_A), const2),
        pl.BlockSpec((1, D_A), const2),
        pl.BlockSpec((A_GROUPS, CHUNK, CHUNK), lambda i: (0, 0, 0)),
        pl.BlockSpec((CHUNK, D_A), const2),
        pl.BlockSpec((tm, 3 * LANES), lambda i: (i % tiles_per_batch, 0)),
    ]
    return pl.pallas_call(
        functools.partial(_inproj_prompt_kernel, tm=tm, tiles_per_batch=tiles_per_batch),
        grid=(n // tm,),
        in_specs=in_specs,
        out_specs=out_specs,
        out_shape=out_shapes,
        scratch_shapes=[pltpu.VMEM((tm, C_END), F32)],
        compiler_params=pltpu.CompilerParams(
            dimension_semantics=("arbitrary",), vmem_limit_bytes=VMEM_LIMIT_BYTES),
        name="inproj_prompt",
    )(x2, g, w, lng, lnb, ws_bf, bias_full, rope_tab)


def _inproj_decode_kernel(x_ref, g_ref, w_ref, lng_ref, lnb_ref, wsv_ref, bsv_ref, rope_ref,
                          aout_ref, qa_ref, kc_ref, vc_ref, ks_ref, vs_ref, kw_ref, vw_ref,
                          gate_ref, szb_ref, vn_ref, z_scr):
    _project(x_ref, g_ref, w_ref, z_scr)
    tab = rope_ref[...]
    u = z_scr[:, C_U:C_U + D_A]
    v = z_scr[:, C_V:C_V + D_A]
    za = z_scr[:, C_ZA:C_ZA + D_A]
    vn = _layer_norm_rows(v, lng_ref[...], lnb_ref[...])
    vn_ref[...] = vn
    s = vn * wsv_ref[...] + bsv_ref[...]
    aout_ref[...] = (u * s * jax.nn.silu(za)).astype(BF16)

    q = z_scr[:, C_Q:C_Q + D_B]
    qr = jnp.concatenate(
        [_rope(q[:, j * LANES:(j + 1) * LANES], tab) for j in range(4)], axis=1) * (HEAD_DIM ** -0.5)
    for e, slab in enumerate(_head_slabs(qr)):
        qa_ref[:, e * LANES:(e + 1) * LANES] = slab.astype(BF16)
    kc_ref[...] = z_scr[:, C_KC:C_KC + KV_W]
    vc_ref[...] = z_scr[:, C_VC:C_VC + KV_W]
    ks_ref[...] = _rope(z_scr[:, C_KS:C_KS + KV_W], tab)
    vs_ref[...] = z_scr[:, C_VS:C_VS + KV_W]
    kw_ref[...] = _rope(z_scr[:, C_KW:C_KW + KV_W], tab)
    vw_ref[...] = z_scr[:, C_VW:C_VW + KV_W]
    gate_ref[...] = jax.nn.sigmoid(z_scr[:, C_G:C_G + LANES])
    szb = jax.nn.silu(z_scr[:, C_ZB:C_ZB + D_B])
    for e, slab in enumerate(_head_slabs(szb)):
        szb_ref[:, e * LANES:(e + 1) * LANES] = slab


def _inproj_decode(x2, g, w, lng, lnb, wsv, bsv, rope_tab):
    n = x2.shape[0]
    out_shapes = [
        jax.ShapeDtypeStruct((n, D_A), BF16),
        jax.ShapeDtypeStruct((n, N_HEADS * LANES), BF16),
        jax.ShapeDtypeStruct((n, KV_W), F32),
        jax.ShapeDtypeStruct((n, KV_W), F32),
        jax.ShapeDtypeStruct((n, KV_W), F32),
        jax.ShapeDtypeStruct((n, KV_W), F32),
        jax.ShapeDtypeStruct((n, KV_W), F32),
        jax.ShapeDtypeStruct((n, KV_W), F32),
        jax.ShapeDtypeStruct((n, LANES), F32),
        jax.ShapeDtypeStruct((n, N_HEADS * LANES), F32),
        jax.ShapeDtypeStruct((n, D_A), F32),
    ]
    return pl.pallas_call(
        _inproj_decode_kernel,
        out_shape=out_shapes,
        scratch_shapes=[pltpu.VMEM((n, C_END), F32)],
        compiler_params=pltpu.CompilerParams(vmem_limit_bytes=VMEM_LIMIT_BYTES),
        name="inproj_decode",
    )(x2, g, w, lng, lnb, wsv, bsv, rope_tab)


def _compress_chunks(ch, pe_ref, w_ref):
    a = _dot((ch + pe_ref[0:1, :]).astype(BF16), w_ref[:, 0:KV_W])
    b = _dot((ch + pe_ref[1:2, :]).astype(BF16), w_ref[:, KV_W:2 * KV_W])
    return a + pltpu.roll(b, ch.shape[0] - 1, 0)


def _compress_prompt_kernel(kc_ref, vc_ref, wk_ref, wv_ref, pek_ref, pev_ref, rope_ref, kcc_ref, vcct_ref):
    kcc = _compress_chunks(kc_ref[0], pek_ref, wk_ref)
    kcc_ref[0] = _rope(kcc, rope_ref[...]).astype(BF16)
    vcc = _compress_chunks(vc_ref[0], pev_ref, wv_ref)
    for j in range(vcc.shape[0] // LANES):
        vcct_ref[0, :, j * LANES:(j + 1) * LANES] = vcc[j * LANES:(j + 1) * LANES].T.astype(BF16)


def _compress_prompt(kc3, vc3, wk, wv, pek, pev, rope_tab):
    b, c, width = kc3.shape
    blk = pl.BlockSpec((1, c, width), lambda i: (i, 0, 0))
    const2 = lambda i: (0, 0)
    return pl.pallas_call(
        _compress_prompt_kernel,
        grid=(b,),
        in_specs=[blk, blk,
                  pl.BlockSpec(wk.shape, const2), pl.BlockSpec(wv.shape, const2),
                  pl.BlockSpec(pek.shape, const2), pl.BlockSpec(pev.shape, const2),
                  pl.BlockSpec(rope_tab.shape, const2)],
        out_specs=[pl.BlockSpec((1, c, KV_W), lambda i: (i, 0, 0)),
                   pl.BlockSpec((1, KV_W, c), lambda i: (i, 0, 0))],
        out_shape=[jax.ShapeDtypeStruct((b, c, KV_W), BF16), jax.ShapeDtypeStruct((b, KV_W, c), BF16)],
        compiler_params=pltpu.CompilerParams(
            dimension_semantics=("arbitrary",), vmem_limit_bytes=VMEM_LIMIT_BYTES),
        name="compress_prompt",
    )(kc3, vc3, wk, wv, pek, pev, rope_tab)


def _select_blocks(score, kk):
    picked = jnp.zeros(score.shape, F32)
    for _ in range(N_SELECT):
        cm = jnp.max(score, axis=0, keepdims=True)
        first = jnp.min(jnp.where(score == cm, kk, LANES), axis=0, keepdims=True)
        hit = kk == first
        picked = jnp.where(hit, 1.0, picked)
        score = jnp.where(hit, -jnp.inf, score)
    return picked


def _softmax_rows(s):
    mx = jnp.max(s, axis=1, keepdims=True)
    ex = jnp.exp(s - mx)
    return ex * (1.0 / jnp.sum(ex, axis=1, keepdims=True))


def _tile_heads(x):
    return jnp.concatenate([x] * N_HEADS, axis=1)


def _attn_prompt_kernel(qt_ref, gate_ref, szb_ref, kcc_ref, vcct_ref, ovlt_ref,
                        ksa_ref, vsb_ref, kwb_ref, vwb_ref, out_ref,
                        qaug_scr, acc_scr, sa_scr, sb_scr):
    i_blk = pl.program_id(1)
    q0 = i_blk * Q_BLOCK
    cols = N_HEADS * Q_BLOCK
    qt = qt_ref[0, 0]

    n_cmp = kcc_ref.shape[1]
    t_c = q0 + _lane_iota((n_cmp, Q_BLOCK))
    seen = (_row_iota((n_cmp, Q_BLOCK)) * CMP_STRIDE + (CMP_BLOCK - 1)) <= t_c
    s = _dot(kcc_ref[0], qt) + _tile_heads(jnp.where(seen, 0.0, NEG))
    mx = jnp.max(s, axis=0, keepdims=True)
    ex = jnp.exp2(s - mx)
    inv = jnp.where(mx > 0.5 * NEG, 1.0 / jnp.sum(ex, axis=0, keepdims=True), 0.0)
    p_c = ex * inv
    o_c = _dot(vcct_ref[0], p_c.astype(BF16))

    w0 = pl.multiple_of(jnp.maximum(q0 - WINDOW, 0), Q_BLOCK)
    wj0 = jnp.maximum(i_blk - WINDOW // Q_BLOCK, 0)
    kwin = kwb_ref[0, pl.ds(w0, WIN_SPAN), :]
    vwt = jnp.concatenate([vwb_ref[0, wj0 + i] for i in range(WIN_SPAN // LANES)], axis=1)
    kp = w0 + _row_iota((WIN_SPAN, Q_BLOCK))
    tw = q0 + _lane_iota((WIN_SPAN, Q_BLOCK))
    in_win = jnp.where(kp <= tw, jnp.where(kp > tw - WINDOW, 0.0, NEG), NEG)
    sw = _dot(kwin, qt) + _tile_heads(in_win)
    exw = jnp.exp2(sw - jnp.max(sw, axis=0, keepdims=True))
    o_w = _dot(vwt, exw.astype(BF16)) * (1.0 / jnp.sum(exw, axis=0, keepdims=True))

    kk = _row_iota((LANES, Q_BLOCK))
    tq = q0 + _lane_iota((LANES, Q_BLOCK))
    valid = kk * SEL_BLOCK <= tq
    cur = tq >> SEL_SHIFT
    forced = (kk == 0) | (kk == cur) | (kk == cur - 1)
    scores = []
    for h in range(N_KV):
        c0 = h * GQA * Q_BLOCK
        p_sum = (p_c[:, c0:c0 + Q_BLOCK] + p_c[:, c0 + Q_BLOCK:c0 + 2 * Q_BLOCK]
                 + p_c[:, c0 + 2 * Q_BLOCK:c0 + 3 * Q_BLOCK] + p_c[:, c0 + 3 * Q_BLOCK:c0 + 4 * Q_BLOCK])
        imp_t = _dot(ovlt_ref[...], p_sum.astype(BF16))
        scores.append(jnp.where(valid, jnp.where(forced, imp_t + FORCE_SCORE, imp_t), -jnp.inf))
    picked = _select_blocks(jnp.concatenate(scores, axis=1), jnp.concatenate([kk, kk], axis=1))
    qaug_scr[0:LANES, :] = qt
    for h in range(N_KV):
        pk = picked[:, h * Q_BLOCK:(h + 1) * Q_BLOCK]
        selneg = jnp.where(valid, jnp.where(pk > 0.0, 0.0, NEG), NEG).astype(BF16)
        for g in range(GQA):
            c0 = (h * GQA + g) * Q_BLOCK
            qaug_scr[LANES:2 * LANES, c0:c0 + Q_BLOCK] = selneg

    acc_scr[...] = jnp.zeros(acc_scr.shape, F32)
    tiles_per_step = SEL_TILE // LANES

    def score(step, dst):
        t0 = pl.multiple_of(step * SEL_TILE, SEL_TILE)
        dst[...] = _dot(ksa_ref[0, pl.ds(t0, SEL_TILE), :], qaug_scr[...])

    def attend(src, step, m_prev, l_prev, causal):
        vt = jnp.concatenate([vsb_ref[0, step * tiles_per_step + i] for i in range(tiles_per_step)], axis=1)
        sc = src[...]
        if causal:
            tok = step * SEL_TILE + _row_iota((SEL_TILE, Q_BLOCK))
            sc = sc + _tile_heads(jnp.where(tok <= q0 + _lane_iota((SEL_TILE, Q_BLOCK)), 0.0, NEG))
        m_next = jnp.maximum(m_prev, jnp.max(sc, axis=0, keepdims=True))
        p = jnp.exp2(sc - m_next)
        alpha = jnp.exp2(m_prev - m_next)
        l_next = alpha * l_prev + jnp.sum(p, axis=0, keepdims=True)
        acc_scr[...] = alpha * acc_scr[...] + _dot(vt, p.astype(BF16))
        return m_next, l_next

    n_full = q0 // SEL_TILE
    score(0, sa_scr)

    def pair(jj, carry):
        m_c, l_c = carry
        score(2 * jj + 1, sb_scr)
        m_c, l_c = attend(sa_scr, 2 * jj, m_c, l_c, False)
        score(2 * jj + 2, sa_scr)
        return attend(sb_scr, 2 * jj + 1, m_c, l_c, False)

    m_run, l_run = lax.fori_loop(
        0, n_full // 2, pair,
        (jnp.full((1, cols), -jnp.inf, F32), jnp.zeros((1, cols), F32)))

    def odd_tail(m_c, l_c):
        score(n_full, sb_scr)
        m_c, l_c = attend(sa_scr, n_full - 1, m_c, l_c, False)
        return attend(sb_scr, n_full, m_c, l_c, True)[1]

    def even_tail(m_c, l_c):
        return attend(sa_scr, n_full, m_c, l_c, True)[1]

    l_run = lax.cond(n_full % 2 == 1, odd_tail, even_tail, m_run, l_run)

    gt = gate_ref[0, 0]
    o_s = acc_scr[...] * (1.0 / l_run)
    heads = []
    for e in range(N_HEADS):
        c0 = e * Q_BLOCK
        o_e = (gt[3 * e:3 * e + 1, :] * o_c[:, c0:c0 + Q_BLOCK]
               + gt[3 * e + 1:3 * e + 2, :] * o_s[:, c0:c0 + Q_BLOCK]
               + gt[3 * e + 2:3 * e + 3, :] * o_w[:, c0:c0 + Q_BLOCK])
        heads.append(o_e.T)
    lo = _lane_iota((Q_BLOCK, LANES)) < HEAD_DIM
    for j in range(N_HEADS // 2):
        a, b = heads[2 * j], heads[2 * j + 1]
        if j < 2:
            slab = jnp.where(lo, a, pltpu.roll(b, HEAD_DIM, 1))
        else:
            slab = jnp.where(lo, pltpu.roll(a, HEAD_DIM, 1), b)
        out_ref[0, :, j * LANES:(j + 1) * LANES] = (
            slab * szb_ref[0, :, j * LANES:(j + 1) * LANES]).astype(BF16)


def _attn_prompt(qt4, gate4, szb3, kcc, vcct, ovlt, ksa3, vsb4, kwb3, vwb4):
    b, nq = qt4.shape[0], qt4.shape[1]
    t = nq * Q_BLOCK
    cols = N_HEADS * Q_BLOCK
    qtile = lambda a: pl.BlockSpec((1, 1) + a.shape[2:], lambda bi, i: (bi, i, 0, 0))
    seq = lambda a: pl.BlockSpec((1,) + a.shape[1:], lambda bi, i, nd=a.ndim: (bi,) + (0,) * (nd - 1))
    return pl.pallas_call(
        _attn_prompt_kernel,
        grid=(b, nq),
        in_specs=[qtile(qt4), qtile(gate4), pl.BlockSpec((1, Q_BLOCK, D_B), lambda bi, i: (bi, i, 0)),
                  seq(kcc), seq(vcct), pl.BlockSpec(ovlt.shape, lambda bi, i: (0, 0)),
                  seq(ksa3), seq(vsb4), seq(kwb3), seq(vwb4)],
        out_specs=pl.BlockSpec((1, Q_BLOCK, D_B), lambda bi, i: (bi, i, 0)),
        out_shape=jax.ShapeDtypeStruct((b, t, D_B), BF16),
        scratch_shapes=[pltpu.VMEM((2 * LANES, cols), BF16),
                        pltpu.VMEM((KV_W, cols), F32),
                        pltpu.VMEM((SEL_TILE, cols), F32),
                        pltpu.VMEM((SEL_TILE, cols), F32)],
        compiler_params=pltpu.CompilerParams(
            dimension_semantics=("arbitrary", "arbitrary"), vmem_limit_bytes=VMEM_LIMIT_BYTES),
        name="attn_prompt",
    )(qt4, gate4, szb3, kcc, vcct, ovlt, ksa3, vsb4, kwb3, vwb4)


def _outproj_kernel(x_ref, a_ref, m_ref, wa_ref, wb_ref, fg_ref, y_ref):
    y = x_ref[...] + _dot(a_ref[...], wa_ref[...]) + _dot(m_ref[...].astype(BF16), wb_ref[...])
    y_ref[...] = _rms_rows(y, fg_ref[...])


def _outproj(x2, a_out, mix_b, wa, wb, fg, tm):
    n = x2.shape[0]
    row = lambda i: (i, 0)
    const2 = lambda i: (0, 0)
    return pl.pallas_call(
        _outproj_kernel,
        grid=(n // tm,),
        in_specs=[pl.BlockSpec((tm, D_MODEL), row),
                  pl.BlockSpec((tm, a_out.shape[1]), row),
                  pl.BlockSpec((tm, mix_b.shape[1]), row),
                  pl.BlockSpec(wa.shape, const2), pl.BlockSpec(wb.shape, const2),
                  pl.BlockSpec((1, D_MODEL), const2)],
        out_specs=pl.BlockSpec((tm, D_MODEL), row),
        out_shape=jax.ShapeDtypeStruct((n, D_MODEL), F32),
        compiler_params=pltpu.CompilerParams(
            dimension_semantics=("arbitrary",), vmem_limit_bytes=VMEM_LIMIT_BYTES),
        name="outproj",
    )(x2, a_out, mix_b, wa, wb, fg)


def _decode_cmp_kernel(pt_ref, *refs, n_pages):
    kpages = refs[0:n_pages]
    vpages = refs[n_pages:2 * n_pages]
    (q_ref, wk_ref, wv_ref, pek_ref, pev_ref, rope_ref, ovl_ref, oc_ref, psum_ref,
     ktok_scr, vtok_scr) = refs[2 * n_pages:]
    del pt_ref

    def chunk_rows(pages, tok_scr):
        for p, page in enumerate(pages):
            tok_scr[p * PAGE_SIZE:(p + 1) * PAGE_SIZE, :] = page[0].T
        n = tok_scr.shape[0] // CMP_STRIDE
        return jnp.concatenate(
            [tok_scr[pl.ds(r, n, stride=CMP_STRIDE), :] for r in range(CMP_STRIDE)], axis=1)

    kch = chunk_rows(kpages, ktok_scr)
    vch = chunk_rows(vpages, vtok_scr)
    kcc = _rope(_compress_chunks(kch, pek_ref, wk_ref), rope_ref[...]).astype(BF16)
    vcc = _compress_chunks(vch, pev_ref, wv_ref).astype(BF16)
    n_chunks = kch.shape[0]
    q = q_ref[0]
    rows = q.shape[0]
    s = _dot_nt(q, kcc)
    seen = _lane_iota((rows, n_chunks)) < n_chunks - 1
    p_c = jnp.where(seen, _softmax_rows(jnp.where(seen, s, NEG)), 0.0)
    oc_ref[0] = _dot(p_c.astype(BF16), vcc)[0:N_HEADS]
    ps0 = jnp.sum(p_c[0:GQA], axis=0, keepdims=True)
    ps1 = jnp.sum(p_c[GQA:2 * GQA], axis=0, keepdims=True)
    psum = jnp.where(_row_iota((rows, n_chunks)) < N_HEADS // 2, ps0, ps1)
    psum_ref[0] = _dot(psum.astype(BF16), ovl_ref[...])[0:N_HEADS]


def _decode_cmp(page_table, ck_pool, cv_pool, q16, wk, wv, pek, pev, rope_tab, ovl):
    nb, n_pages = page_table.shape
    page = lambda p: pl.BlockSpec((1, KV_W, PAGE_SIZE), lambda b, pt, p=p: (pt[b, p], 0, 0))
    tok_buf = pltpu.VMEM((n_pages * PAGE_SIZE, KV_W), F32)
    const2 = lambda b, pt: (0, 0)
    per_b = lambda a: pl.BlockSpec((1,) + a.shape[1:], lambda b, pt: (b, 0, 0))
    in_specs = ([page(p) for p in range(n_pages)] * 2
                + [per_b(q16),
                   pl.BlockSpec(wk.shape, const2), pl.BlockSpec(wv.shape, const2),
                   pl.BlockSpec(pek.shape, const2), pl.BlockSpec(pev.shape, const2),
                   pl.BlockSpec(rope_tab.shape, const2), pl.BlockSpec(ovl.shape, const2)])
    out_blk = pl.BlockSpec((1, N_HEADS, LANES), lambda b, pt: (b, 0, 0))
    grid_spec = pltpu.PrefetchScalarGridSpec(
        num_scalar_prefetch=1, grid=(nb,), in_specs=in_specs, out_specs=[out_blk, out_blk],
        scratch_shapes=[tok_buf, tok_buf])
    return pl.pallas_call(
        functools.partial(_decode_cmp_kernel, n_pages=n_pages),
        grid_spec=grid_spec,
        out_shape=[jax.ShapeDtypeStruct((nb, N_HEADS, LANES), F32)] * 2,
        compiler_params=pltpu.CompilerParams(
            dimension_semantics=("arbitrary",), vmem_limit_bytes=VMEM_LIMIT_BYTES),
        name="decode_cmp",
    )(page_table, *([ck_pool] * n_pages), *([cv_pool] * n_pages),
      q16, wk, wv, pek, pev, rope_tab, ovl)


def _decode_topk_kernel(imp_ref, sel_ref, *, n_blk, cur):
    n = imp_ref.shape[0]
    tiles = [imp_ref[i * LANES:(i + 1) * LANES, :].T for i in range(n // LANES)]
    imp_t = jnp.concatenate(tiles, axis=1)
    kk = _row_iota(imp_t.shape)
    forced = (kk == 0) | (kk == cur) | (kk == cur - 1)
    exists = kk < n_blk
    score = jnp.where(exists, jnp.where(forced, imp_t + FORCE_SCORE, imp_t), -jnp.inf)
    picked = _select_blocks(score, kk)
    selneg = jnp.where(exists, jnp.where(picked > 0.0, 0.0, NEG), NEG)
    for i in range(n // LANES):
        sel_ref[i * LANES:(i + 1) * LANES, :] = selneg[:, i * LANES:(i + 1) * LANES].T


def _decode_topk(imp2, n_blk, cur):
    return pl.pallas_call(
        functools.partial(_decode_topk_kernel, n_blk=n_blk, cur=cur),
        out_shape=jax.ShapeDtypeStruct(imp2.shape, F32),
        compiler_params=pltpu.CompilerParams(vmem_limit_bytes=VMEM_LIMIT_BYTES),
        name="decode_topk",
    )(imp2)


def _decode_attend(s, q32, vals_t_bf, k_new, v_new, bias_new):
    kn = k_new.astype(BF16).astype(F32)
    s_new = jnp.sum(q32 * kn, axis=1, keepdims=True) + bias_new
    mx = jnp.maximum(jnp.max(s, axis=1, keepdims=True), s_new)
    ex = jnp.exp(s - mx)
    ex_new = jnp.exp(s_new - mx)
    inv = 1.0 / (jnp.sum(ex, axis=1, keepdims=True) + ex_new)
    p = (ex * inv).astype(BF16)
    p_new = (ex_new * inv).astype(BF16).astype(F32)
    return _dot_nt(p, vals_t_bf) + p_new * v_new.astype(BF16).astype(F32)


def _decode_sel_kernel(pt_ref, *refs, n_pages, cur):
    kpages = refs[0:n_pages]
    vpages = refs[n_pages:2 * n_pages]
    (ckw_ref, cvw_ref, q_ref, sel_ref, ksn_ref, vsn_ref, kwn_ref, vwn_ref, kwc_ref, vwc_ref,
     gate_ref, oc_ref, szb_ref, onehot_ref, mix_ref, okw_ref, ovw_ref) = refs[2 * n_pages:]
    del pt_ref
    q = q_ref[0]
    q32 = q.astype(F32)
    rows = q.shape[0]
    sel = sel_ref[0]

    kt = jnp.concatenate([r[0] for r in kpages], axis=1).astype(BF16)
    vt = jnp.concatenate([r[0] for r in vpages], axis=1).astype(BF16)
    kaug = jnp.concatenate([kt, onehot_ref[...]], axis=0)
    qaug = jnp.concatenate([q, sel], axis=1)
    s_sel = _dot(qaug, kaug)
    o_s = _decode_attend(s_sel, q32, vt, ksn_ref[0], vsn_ref[0], sel.astype(F32)[:, cur:cur + 1])

    kwin = ckw_ref[0]
    vwin = cvw_ref[0]
    keep = kwin.shape[1]
    visible = _lane_iota((rows, keep)) > keep - WINDOW
    s_win = jnp.where(visible, _dot(q, kwin.astype(BF16)), NEG)
    o_w = _decode_attend(s_win, q32, vwin.astype(BF16), kwn_ref[0], vwn_ref[0], 0.0)

    g = gate_ref[0]
    o = g[:, 0:1] * oc_ref[0] + g[:, 1:2] * o_s[0:N_HEADS] + g[:, 2:3] * o_w[0:N_HEADS]
    mix_ref[0] = o * szb_ref[0]

    last = _lane_iota((KV_W, keep)) == keep - 1
    okw_ref[0] = jnp.where(last, kwc_ref[0], pltpu.roll(kwin, keep - 1, 1))
    ovw_ref[0] = jnp.where(last, vwc_ref[0], pltpu.roll(vwin, keep - 1, 1))


def _decode_sel(page_table, ks_pool, vs_pool, ckw, cvw, q16, sel16, ksn, vsn, kwn, vwn, kwc, vwc,
                gate3, oc, szb3, onehot, cur):
    nb, n_pages = page_table.shape
    keep = ckw.shape[2]
    page = lambda p: pl.BlockSpec((1, KV_W, PAGE_SIZE), lambda b, pt, p=p: (pt[b, p], 0, 0))
    per_b = lambda a: pl.BlockSpec((1,) + a.shape[1:], lambda b, pt: (b, 0, 0))
    in_specs = ([page(p) for p in range(n_pages)] * 2
                + [per_b(a) for a in (ckw, cvw, q16, sel16, ksn, vsn, kwn, vwn, kwc, vwc,
                                      gate3, oc, szb3)]
                + [pl.BlockSpec(onehot.shape, lambda b, pt: (0, 0))])
    out_specs = [pl.BlockSpec((1, N_HEADS, LANES), lambda b, pt: (b, 0, 0)),
                 pl.BlockSpec((1, KV_W, keep), lambda b, pt: (b, 0, 0)),
                 pl.BlockSpec((1, KV_W, keep), lambda b, pt: (b, 0, 0))]
    grid_spec = pltpu.PrefetchScalarGridSpec(
        num_scalar_prefetch=1, grid=(nb,), in_specs=in_specs, out_specs=out_specs)
    return pl.pallas_call(
        functools.partial(_decode_sel_kernel, n_pages=n_pages, cur=cur),
        grid_spec=grid_spec,
        out_shape=[jax.ShapeDtypeStruct((nb, N_HEADS, LANES), F32),
                   jax.ShapeDtypeStruct((nb, KV_W, keep), F32),
                   jax.ShapeDtypeStruct((nb, KV_W, keep), F32)],
        compiler_params=pltpu.CompilerParams(
            dimension_semantics=("arbitrary",), vmem_limit_bytes=VMEM_LIMIT_BYTES),
        name="decode_sel",
    )(page_table, *([ks_pool] * n_pages), *([vs_pool] * n_pages),
      ckw, cvw, q16, sel16, ksn, vsn, kwn, vwn, kwc, vwc, gate3, oc, szb3, onehot)


def _rope_table(pos):
    half = ROT_DIM // 2
    inv = jnp.power(jnp.float32(ROPE_THETA), -jnp.arange(half, dtype=F32) / half)
    ang = pos.astype(F32)[:, None] * inv[None, :]
    cs = jnp.concatenate([jnp.cos(ang), jnp.sin(ang)], axis=1)
    lane = jnp.arange(3 * LANES)
    seg, l64 = lane // LANES, lane % HEAD_DIM
    src = jnp.where(seg == 0, l64 % half, half + l64 % half)
    sign = jnp.where(seg == 0, l64 < ROT_DIM,
                     jnp.where(seg == 1, l64 < half, (l64 >= half) & (l64 < ROT_DIM))).astype(F32)
    sign = jnp.where(seg == 1, -sign, sign)
    place = (jnp.arange(ROT_DIM)[:, None] == src[None, :]).astype(F32) * sign[None, :]
    ones = ((seg == 0) & (l64 >= ROT_DIM)).astype(F32)
    return jnp.dot(cs, place, precision=lax.Precision.HIGHEST) + ones[None, :]


def _compress_weights(w_c, pe_c):
    w4 = w_c.reshape(2, CMP_STRIDE, HEAD_DIM, HEAD_DIM)
    wb = jnp.einsum('arde,hg->rhdage', w4, jnp.eye(N_KV, dtype=w_c.dtype))
    wb = wb.reshape(CMP_STRIDE * KV_W, 2 * KV_W).astype(BF16)
    pe = jnp.broadcast_to(pe_c.reshape(2, CMP_STRIDE, 1, HEAD_DIM), (2, CMP_STRIDE, N_KV, HEAD_DIM))
    return wb, pe.reshape(2, CMP_STRIDE * KV_W)


def _overlap_matrix(n_rows, n_blk):
    cs = jnp.arange(n_rows, dtype=jnp.int32)[:, None] * CMP_STRIDE
    bs = jnp.arange(LANES, dtype=jnp.int32)[None, :] * SEL_BLOCK
    hit = (cs < bs + SEL_BLOCK) & (cs + CMP_BLOCK > bs) & (jnp.arange(LANES)[None, :] < n_blk)
    return hit.astype(BF16)


def kernel(x_prompt, x_sample, cache_k_cmp, cache_v_cmp, cache_k_sel, cache_v_sel, cache_k_win,
           cache_v_win, page_table, norm_g, w_in, ln_v_g, ln_v_b, w_s, b_s, w_ck, pe_ck, w_cv,
           pe_cv, w_out, final_g):
    depth = norm_g.shape[0]
    assert depth == 1
    bsz, seq, _ = x_prompt.shape
    nb, dec_seq, _ = x_sample.shape
    assert dec_seq == 1 and seq % SEL_TILE == 0 and seq >= WIN_SPAN and (seq // CMP_STRIDE) % LANES == 0
    n_pages = page_table.shape[1]
    past = n_pages * PAGE_SIZE
    keep = cache_k_win.shape[2]
    assert keep == WINDOW and past % SEL_BLOCK == 0

    w = w_in[0]
    w_pad = jnp.concatenate(
        [w[:, :C_G + N_GATE_COLS], jnp.zeros((D_MODEL, LANES - N_GATE_COLS), w.dtype),
         w[:, C_G + N_GATE_COLS:]], axis=1).astype(BF16)
    g_row = norm_g[0][None, :]
    lng = ln_v_g[0][None, :]
    lnb = ln_v_b[0][None, :]
    fg = final_g[None, :]
    wk_c, pek = _compress_weights(w_ck[0], pe_ck[0])
    wv_c, pev = _compress_weights(w_cv[0], pe_cv[0])
    w_o = w_out[0].astype(BF16)
    wo_a, wo_b = w_o[:D_A], w_o[D_A:]

    tril = jnp.tril(jnp.ones((CHUNK, CHUNK), w_s.dtype))
    ws_bf = (w_s[0] * tril).astype(BF16)
    bias_full = jnp.repeat(b_s[0].T, HEAD_DIM, axis=1)
    xp2 = x_prompt.reshape(bsz * seq, D_MODEL)
    (a_out, qt, kc, vc, kc_t, vc_t, ks_t, vs_t, kw_t, vw_t, ksa, kwb, vsb, vwb, gates_t, szb) = _inproj_prompt(
        xp2, g_row, w_pad, lng, lnb, ws_bf, bias_full,
        _rope_table(jnp.arange(seq, dtype=jnp.int32)), seq)
    n_chunks = seq // CMP_STRIDE
    kcc, vcc_t = _compress_prompt(
        kc.reshape(bsz, n_chunks, CMP_STRIDE * KV_W), vc.reshape(bsz, n_chunks, CMP_STRIDE * KV_W),
        wk_c, wv_c, pek, pev,
        _rope_table(jnp.arange(n_chunks, dtype=jnp.int32) * CMP_STRIDE))
    b3 = lambda a: a.reshape(bsz, seq, a.shape[-1])
    mix_p = _attn_prompt(qt, gates_t, b3(szb), kcc, vcc_t,
                         _overlap_matrix(n_chunks, seq // SEL_BLOCK).T,
                         b3(ksa), vsb, b3(kwb), vwb)
    y_prompt = _outproj(xp2, a_out, mix_p.reshape(bsz * seq, D_B), wo_a, wo_b, fg, 512)
    y_prompt = y_prompt.reshape(bsz, seq, D_MODEL)
    st = lambda a: jnp.transpose(a.reshape(a.shape[0], N_KV, HEAD_DIM, a.shape[2]), (0, 3, 1, 2))[None]
    keep_p = min(WINDOW, seq)

    xs2 = x_sample.reshape(nb, D_MODEL)
    wsv = jnp.repeat(w_s[0][:, 0, 0], HEAD_DIM)[None, :]
    bsv = jnp.repeat(b_s[0][:, 0], HEAD_DIM)[None, :]
    (a_s, qa_s, kc_s, vc_s, ks_s, vs_s, kw_s, vw_s, gates_s, szb_s, vn_s) = _inproj_decode(
        xs2, g_row, w_pad, lng, lnb, wsv, bsv,
        _rope_table(jnp.full((nb,), past, dtype=jnp.int32)))
    q16 = jnp.pad(qa_s.reshape(nb, N_HEADS, LANES), ((0, 0), (0, N_HEADS), (0, 0)))
    d_chunks = past // CMP_STRIDE
    n_blk = -(-(past + 1) // SEL_BLOCK)
    cur = past // SEL_BLOCK
    kv_t = lambda c: jnp.transpose(c[0], (0, 2, 3, 1)).reshape(c.shape[1], KV_W, c.shape[2])
    oc_s, imp_s = _decode_cmp(
        page_table, kv_t(cache_k_cmp), kv_t(cache_v_cmp),
        q16, wk_c, wv_c, pek, pev,
        _rope_table(jnp.arange(d_chunks, dtype=jnp.int32) * CMP_STRIDE),
        _overlap_matrix(d_chunks, n_blk))
    selneg = _decode_topk(imp_s[:, ::GQA, :].reshape(nb * N_KV, LANES), n_blk, cur)
    sel16 = jnp.pad(jnp.repeat(selneg.reshape(nb, N_KV, LANES), GQA, axis=1),
                    ((0, 0), (0, N_HEADS), (0, 0))).astype(BF16)
    tok_blk = jnp.arange(past, dtype=jnp.int32)[None, :] // SEL_BLOCK
    onehot = (tok_blk == jnp.arange(LANES, dtype=jnp.int32)[:, None]).astype(BF16)
    r3 = lambda a: a.reshape(nb, 1, a.shape[-1])
    c3 = lambda a: a.reshape(nb, a.shape[-1], 1)
    mix_s, okw, ovw = _decode_sel(
        page_table, kv_t(cache_k_sel), kv_t(cache_v_sel), kv_t(cache_k_win), kv_t(cache_v_win),
        q16, sel16, r3(ks_s), r3(vs_s), r3(kw_s), r3(vw_s), c3(kw_s), c3(vw_s),
        gates_s[:, :N_GATE_COLS].reshape(nb, N_HEADS, 3), oc_s,
        szb_s.reshape(nb, N_HEADS, LANES), onehot, cur)
    wo_slab = jnp.zeros((N_HEADS, N_KV, HEAD_DIM, D_MODEL), BF16)
    wo_heads = wo_b.reshape(N_HEADS, HEAD_DIM, D_MODEL)
    for e in range(N_HEADS):
        wo_slab = wo_slab.at[e, e // GQA].set(wo_heads[e])
    y_sample = _outproj(xs2, a_s, mix_s.reshape(nb, N_HEADS * LANES), wo_a,
                        wo_slab.reshape(N_HEADS * LANES, D_MODEL), fg, nb)
    y_sample = y_sample.reshape(nb, 1, D_MODEL)
    ss = lambda a: a.reshape(1, nb, 1, N_KV, HEAD_DIM)

    return (y_prompt, y_sample,
            st(kc_t), st(vc_t), st(ks_t), st(vs_t),
            st(kw_t[:, :, seq - keep_p:]), st(vw_t[:, :, seq - keep_p:]),
            ss(kc_s), ss(vc_s), ss(ks_s), ss(vs_s),
            st(okw), st(ovw),
            vn_s.reshape(1, nb, 1, D_A))
```

```python
import functools

import jax
import jax.numpy as jnp
from jax import lax
from jax.experimental import pallas as pl
from jax.experimental.pallas import tpu as pltpu

F32 = jnp.float32
BF16 = jnp.bfloat16

D_MODEL = 1024
HEAD_DIM = 64
D_A = 512
D_B = 512
A_GROUPS = 8
CHUNK = 128
N_HEADS = 8
N_KV = 2
GQA = 4
KV_W = 128
ROT_DIM = 16
ROPE_THETA = 500000.0
CMP_BLOCK = 32
CMP_STRIDE = 16
SEL_BLOCK = 64
SEL_SHIFT = 6
N_SELECT = 16
WINDOW = 512
Q_BLOCK = 128
PAGE_SIZE = 128
NORM_EPS = 1e-6
FORCE_SCORE = 1e4
NEG = -1e30
LOG2_E = 1.4426950408889634

LANES = 128
VMEM_LIMIT_BYTES = 56 * 1024 * 1024

C_U, C_V, C_ZA, C_Q = 0, 512, 1024, 1536
C_KC, C_VC, C_KS, C_VS, C_KW, C_VW = 2048, 2176, 2304, 2432, 2560, 2688
C_G, C_ZB, C_END = 2816, 2944, 3456
N_GATE_COLS = 3 * N_HEADS

SEL_TILE = 512
WIN_SPAN = WINDOW + Q_BLOCK

_NT = (((1,), (1,)), ((), ()))


def _dot(a, b):
    return jnp.dot(a, b, preferred_element_type=F32)


def _dot_nt(a, b):
    return lax.dot_general(a, b, _NT, preferred_element_type=F32)


def _lane_iota(shape):
    return lax.broadcasted_iota(jnp.int32, shape, len(shape) - 1)


def _row_iota(shape):
    return lax.broadcasted_iota(jnp.int32, shape, len(shape) - 2)


def _rope(x, tab):
    c = tab[:, 0:LANES]
    s1 = tab[:, LANES:2 * LANES]
    s2 = tab[:, 2 * LANES:3 * LANES]
    return x * c + pltpu.roll(x, LANES - ROT_DIM // 2, 1) * s1 + pltpu.roll(x, ROT_DIM // 2, 1) * s2


def _rms_rows(x, g):
    ms = jnp.mean(x * x, axis=-1, keepdims=True)
    return x * lax.rsqrt(ms + NORM_EPS) * g


def _layer_norm_rows(v, g, b):
    mu = jnp.mean(v, axis=-1, keepdims=True)
    vc = v - mu
    var = jnp.mean(vc * vc, axis=-1, keepdims=True)
    return vc * lax.rsqrt(var + NORM_EPS) * g + b


def _head_slabs(x512):
    out = []
    rows = x512.shape[0]
    lane = _lane_iota((rows, LANES))
    lo = lane < HEAD_DIM
    for j in range(4):
        slab = x512[:, j * LANES:(j + 1) * LANES]
        swapped = pltpu.roll(slab, HEAD_DIM, 1)
        if j < 2:
            out.append(jnp.where(lo, slab, 0.0))
            out.append(jnp.where(lo, swapped, 0.0))
        else:
            out.append(jnp.where(lo, 0.0, swapped))
            out.append(jnp.where(lo, 0.0, slab))
    return out


def _project(x_ref, g_ref, w_ref, z_scr):
    hn = _rms_rows(x_ref[...], g_ref[...])
    z_scr[...] = _dot(hn.astype(BF16), w_ref[...])


def _inproj_prompt_kernel(x_ref, g_ref, w_ref, lng_ref, lnb_ref, ws_ref, bias_ref, rope_ref,
                          aout_ref, qt_ref, kc_ref, vc_ref,
                          kct_ref, vct_ref, kst_ref, vst_ref, kwt_ref, vwt_ref,
                          ksa_ref, kwb_ref, vsb_ref, vwb_ref, gate_ref, szb_ref,
                          z_scr, *, tm, tiles_per_batch):
    _project(x_ref, g_ref, w_ref, z_scr)
    tab = rope_ref[...]

    lane = _lane_iota((CHUNK, LANES))
    lo = lane < HEAD_DIM
    for c in range(tm // CHUNK):
        r0 = c * CHUNK
        u = z_scr[r0:r0 + CHUNK, C_U:C_U + D_A]
        v = z_scr[r0:r0 + CHUNK, C_V:C_V + D_A]
        za = z_scr[r0:r0 + CHUNK, C_ZA:C_ZA + D_A]
        vn = _layer_norm_rows(v, lng_ref[...], lnb_ref[...])
        parts = []
        for p in range(A_GROUPS // 2):
            vp = vn[:, p * LANES:(p + 1) * LANES]
            v_lo = jnp.where(lo, vp, 0.0).astype(BF16)
            v_hi = jnp.where(lo, 0.0, vp).astype(BF16)
            parts.append(_dot(ws_ref[2 * p], v_lo) + _dot(ws_ref[2 * p + 1], v_hi))
        s = jnp.concatenate(parts, axis=1) + bias_ref[...]
        aout_ref[r0:r0 + CHUNK, :] = (u * s * jax.nn.silu(za)).astype(BF16)

    q = z_scr[:, C_Q:C_Q + D_B]
    qr = jnp.concatenate(
        [_rope(q[:, j * LANES:(j + 1) * LANES], tab) for j in range(4)], axis=1) * (HEAD_DIM ** -0.5 * LOG2_E)
    slabs = _head_slabs(qr)
    kc = z_scr[:, C_KC:C_KC + KV_W]
    vc = z_scr[:, C_VC:C_VC + KV_W]
    kc_ref[...] = kc
    vc_ref[...] = vc
    ks = _rope(z_scr[:, C_KS:C_KS + KV_W], tab)
    vs = z_scr[:, C_VS:C_VS + KV_W]
    kw = _rope(z_scr[:, C_KW:C_KW + KV_W], tab)
    vw = z_scr[:, C_VW:C_VW + KV_W]
    gates = jax.nn.sigmoid(z_scr[:, C_G:C_G + LANES])
    pos = (pl.program_id(0) % tiles_per_batch) * tm + _row_iota((tm, LANES))
    onehot = jnp.where((pos >> SEL_SHIFT) == _lane_iota((tm, LANES)), 1.0, 0.0)
    ksa_ref[:, 0:LANES] = ks.astype(BF16)
    ksa_ref[:, LANES:2 * LANES] = onehot.astype(BF16)
    kwb_ref[...] = kw.astype(BF16)
    for j in range(tm // LANES):
        r0, r1 = j * LANES, (j + 1) * LANES
        kct_ref[0, :, r0:r1] = kc[r0:r1].T
        vct_ref[0, :, r0:r1] = vc[r0:r1].T
        kst_ref[0, :, r0:r1] = ks[r0:r1].T
        kwt_ref[0, :, r0:r1] = kw[r0:r1].T
        vs_t, vw_t = vs[r0:r1].T, vw[r0:r1].T
        vst_ref[0, :, r0:r1] = vs_t
        vwt_ref[0, :, r0:r1] = vw_t
        vsb_ref[0, j] = vs_t.astype(BF16)
        vwb_ref[0, j] = vw_t.astype(BF16)
        for e in range(N_HEADS):
            qt_ref[0, j, :, e * Q_BLOCK:(e + 1) * Q_BLOCK] = slabs[e][r0:r1].T.astype(BF16)
        gate_ref[0, j] = gates[r0:r1].T
    szb_ref[...] = jax.nn.silu(z_scr[:, C_ZB:C_ZB + D_B])


def _inproj_prompt(x2, g, w, lng, lnb, ws_bf, bias_full, rope_tab, seq, tm=256):
    n = x2.shape[0]
    tiles_per_batch = seq // tm
    bsz = n // seq
    n_tok_tiles = seq // LANES
    row = lambda i: (i, 0)
    const2 = lambda i: (0, 0)
    rows2d = lambda width, dtype: (jax.ShapeDtypeStruct((n, width), dtype), pl.BlockSpec((tm, width), row))
    kv_t = (jax.ShapeDtypeStruct((bsz, KV_W, seq), F32),
            pl.BlockSpec((1, KV_W, tm), lambda i: (i // tiles_per_batch, 0, i % tiles_per_batch)))
    tiles = lambda r, c, dtype: (
        jax.ShapeDtypeStruct((bsz, n_tok_tiles, r, c), dtype),
        pl.BlockSpec((1, tm // LANES, r, c), lambda i: (i // tiles_per_batch, i % tiles_per_batch, 0, 0)))
    outs = [
        rows2d(D_A, BF16),
        tiles(KV_W, N_HEADS * Q_BLOCK, BF16),
        rows2d(KV_W, F32),
        rows2d(KV_W, F32),
        kv_t, kv_t, kv_t, kv_t, kv_t, kv_t,
        rows2d(2 * KV_W, BF16),
        rows2d(KV_W, BF16),
        tiles(KV_W, LANES, BF16),
        tiles(KV_W, LANES, BF16),
        tiles(LANES, LANES, F32),
        rows2d(D_B, F32),
    ]
    out_shapes = [o[0] for o in outs]
    out_specs = [o[1] for o in outs]
    in_specs = [
        pl.BlockSpec((tm, D_MODEL), row),
        pl.BlockSpec((1, D_MODEL), const2),
        pl.BlockSpec((D_MODEL, C_END), const2),
        pl.BlockSpec((1, D_A), const2),
        pl.BlockSpec((1, D_A), const2),
        pl.BlockSpec((A_GROUPS, CHUNK, CHUNK), lambda i: (0, 0, 0)),
        pl.BlockSpec((CHUNK, D_A), const2),
        pl.BlockSpec((tm, 3 * LANES), lambda i: (i % tiles_per_batch, 0)),
    ]
    return pl.pallas_call(
        functools.partial(_inproj_prompt_kernel, tm=tm, tiles_per_batch=tiles_per_batch),
        grid=(n // tm,),
        in_specs=in_specs,
        out_specs=out_specs,
        out_shape=out_shapes,
        scratch_shapes=[pltpu.VMEM((tm, C_END), F32)],
        compiler_params=pltpu.CompilerParams(
            dimension_semantics=("arbitrary",), vmem_limit_bytes=VMEM_LIMIT_BYTES),
        name="inproj_prompt",
    )(x2, g, w, lng, lnb, ws_bf, bias_full, rope_tab)


def _inproj_decode_kernel(x_ref, g_ref, w_ref, lng_ref, lnb_ref, wsv_ref, bsv_ref, rope_ref,
                          aout_ref, qa_ref, kc_ref, vc_ref, ks_ref, vs_ref, kw_ref, vw_ref,
                          gate_ref, szb_ref, vn_ref, z_scr):
    _project(x_ref, g_ref, w_ref, z_scr)
    tab = rope_ref[...]
    u = z_scr[:, C_U:C_U + D_A]
    v = z_scr[:, C_V:C_V + D_A]
    za = z_scr[:, C_ZA:C_ZA + D_A]
    vn = _layer_norm_rows(v, lng_ref[...], lnb_ref[...])
    vn_ref[...] = vn
    s = vn * wsv_ref[...] + bsv_ref[...]
    aout_ref[...] = (u * s * jax.nn.silu(za)).astype(BF16)

    q = z_scr[:, C_Q:C_Q + D_B]
    qr = jnp.concatenate(
        [_rope(q[:, j * LANES:(j + 1) * LANES], tab) for j in range(4)], axis=1) * (HEAD_DIM ** -0.5)
    for e, slab in enumerate(_head_slabs(qr)):
        qa_ref[:, e * LANES:(e + 1) * LANES] = slab.astype(BF16)
    kc_ref[...] = z_scr[:, C_KC:C_KC + KV_W]
    vc_ref[...] = z_scr[:, C_VC:C_VC + KV_W]
    ks_ref[...] = _rope(z_scr[:, C_KS:C_KS + KV_W], tab)
    vs_ref[...] = z_scr[:, C_VS:C_VS + KV_W]
    kw_ref[...] = _rope(z_scr[:, C_KW:C_KW + KV_W], tab)
    vw_ref[...] = z_scr[:, C_VW:C_VW + KV_W]
    gate_ref[...] = jax.nn.sigmoid(z_scr[:, C_G:C_G + LANES])
    szb = jax.nn.silu(z_scr[:, C_ZB:C_ZB + D_B])
    for e, slab in enumerate(_head_slabs(szb)):
        szb_ref[:, e * LANES:(e + 1) * LANES] = slab


def _inproj_decode(x2, g, w, lng, lnb, wsv, bsv, rope_tab):
    n = x2.shape[0]
    out_shapes = [
        jax.ShapeDtypeStruct((n, D_A), BF16),
        jax.ShapeDtypeStruct((n, N_HEADS * LANES), BF16),
        jax.ShapeDtypeStruct((n, KV_W), F32),
        jax.ShapeDtypeStruct((n, KV_W), F32),
        jax.ShapeDtypeStruct((n, KV_W), F32),
        jax.ShapeDtypeStruct((n, KV_W), F32),
        jax.ShapeDtypeStruct((n, KV_W), F32),
        jax.ShapeDtypeStruct((n, KV_W), F32),
        jax.ShapeDtypeStruct((n, LANES), F32),
        jax.ShapeDtypeStruct((n, N_HEADS * LANES), F32),
        jax.ShapeDtypeStruct((n, D_A), F32),
    ]
    return pl.pallas_call(
        _inproj_decode_kernel,
        out_shape=out_shapes,
        scratch_shapes=[pltpu.VMEM((n, C_END), F32)],
        compiler_params=pltpu.CompilerParams(vmem_limit_bytes=VMEM_LIMIT_BYTES),
        name="inproj_decode",
    )(x2, g, w, lng, lnb, wsv, bsv, rope_tab)


def _compress_chunks(ch, pe_ref, w_ref):
    a = _dot((ch + pe_ref[0:1, :]).astype(BF16), w_ref[:, 0:KV_W])
    b = _dot((ch + pe_ref[1:2, :]).astype(BF16), w_ref[:, KV_W:2 * KV_W])
    return a + pltpu.roll(b, ch.shape[0] - 1, 0)


def _compress_prompt_kernel(kc_ref, vc_ref, wk_ref, wv_ref, pek_ref, pev_ref, rope_ref, kcc_ref, vcct_ref):
    kcc = _compress_chunks(kc_ref[0], pek_ref, wk_ref)
    kcc_ref[0] = _rope(kcc, rope_ref[...]).astype(BF16)
    vcc = _compress_chunks(vc_ref[0], pev_ref, wv_ref)
    for j in range(vcc.shape[0] // LANES):
        vcct_ref[0, :, j * LANES:(j + 1) * LANES] = vcc[j * LANES:(j + 1) * LANES].T.astype(BF16)


def _compress_prompt(kc3, vc3, wk, wv, pek, pev, rope_tab):
    b, c, width = kc3.shape
    blk = pl.BlockSpec((1, c, width), lambda i: (i, 0, 0))
    const2 = lambda i: (0, 0)
    return pl.pallas_call(
        _compress_prompt_kernel,
        grid=(b,),
        in_specs=[blk, blk,
                  pl.BlockSpec(wk.shape, const2), pl.BlockSpec(wv.shape, const2),
                  pl.BlockSpec(pek.shape, const2), pl.BlockSpec(pev.shape, const2),
                  pl.BlockSpec(rope_tab.shape, const2)],
        out_specs=[pl.BlockSpec((1, c, KV_W), lambda i: (i, 0, 0)),
                   pl.BlockSpec((1, KV_W, c), lambda i: (i, 0, 0))],
        out_shape=[jax.ShapeDtypeStruct((b, c, KV_W), BF16), jax.ShapeDtypeStruct((b, KV_W, c), BF16)],
        compiler_params=pltpu.CompilerParams(
            dimension_semantics=("arbitrary",), vmem_limit_bytes=VMEM_LIMIT_BYTES),
        name="compress_prompt",
    )(kc3, vc3, wk, wv, pek, pev, rope_tab)


def _select_blocks(score, kk):
    picked = jnp.zeros(score.shape, F32)
    for _ in range(N_SELECT):
        cm = jnp.max(score, axis=0, keepdims=True)
        first = jnp.min(jnp.where(score == cm, kk, LANES), axis=0, keepdims=True)
        hit = kk == first
        picked = jnp.where(hit, 1.0, picked)
        score = jnp.where(hit, -jnp.inf, score)
    return picked


def _softmax_rows(s):
    mx = jnp.max(s, axis=1, keepdims=True)
    ex = jnp.exp(s - mx)
    return ex * (1.0 / jnp.sum(ex, axis=1, keepdims=True))


def _tile_heads(x):
    return jnp.concatenate([x] * N_HEADS, axis=1)


def _attn_prompt_kernel(qt_ref, gate_ref, szb_ref, kcc_ref, vcct_ref, ovlt_ref,
                        ksa_ref, vsb_ref, kwb_ref, vwb_ref, out_ref,
                        qaug_scr, acc_scr, sa_scr, sb_scr):
    i_blk = pl.program_id(1)
    q0 = i_blk * Q_BLOCK
    cols = N_HEADS * Q_BLOCK
    qt = qt_ref[0, 0]

    n_cmp = kcc_ref.shape[1]
    t_c = q0 + _lane_iota((n_cmp, Q_BLOCK))
    seen = (_row_iota((n_cmp, Q_BLOCK)) * CMP_STRIDE + (CMP_BLOCK - 1)) <= t_c
    s = _dot(kcc_ref[0], qt) + _tile_heads(jnp.where(seen, 0.0, NEG))
    mx = jnp.max(s, axis=0, keepdims=True)
    ex = jnp.exp2(s - mx)
    inv = jnp.where(mx > 0.5 * NEG, 1.0 / jnp.sum(ex, axis=0, keepdims=True), 0.0)
    p_c = ex * inv
    o_c = _dot(vcct_ref[0], p_c.astype(BF16))

    w0 = pl.multiple_of(jnp.maximum(q0 - WINDOW, 0), Q_BLOCK)
    wj0 = jnp.maximum(i_blk - WINDOW // Q_BLOCK, 0)
    kwin = kwb_ref[0, pl.ds(w0, WIN_SPAN), :]
    vwt = jnp.concatenate([vwb_ref[0, wj0 + i] for i in range(WIN_SPAN // LANES)], axis=1)
    kp = w0 + _row_iota((WIN_SPAN, Q_BLOCK))
    tw = q0 + _lane_iota((WIN_SPAN, Q_BLOCK))
    in_win = jnp.where(kp <= tw, jnp.where(kp > tw - WINDOW, 0.0, NEG), NEG)
    sw = _dot(kwin, qt) + _tile_heads(in_win)
    exw = jnp.exp2(sw - jnp.max(sw, axis=0, keepdims=True))
    o_w = _dot(vwt, exw.astype(BF16)) * (1.0 / jnp.sum(exw, axis=0, keepdims=True))

    kk = _row_iota((LANES, Q_BLOCK))
    tq = q0 + _lane_iota((LANES, Q_BLOCK))
    valid = kk * SEL_BLOCK <= tq
    cur = tq >> SEL_SHIFT
    forced = (kk == 0) | (kk == cur) | (kk == cur - 1)
    scores = []
    for h in range(N_KV):
        c0 = h * GQA * Q_BLOCK
        p_sum = (p_c[:, c0:c0 + Q_BLOCK] + p_c[:, c0 + Q_BLOCK:c0 + 2 * Q_BLOCK]
                 + p_c[:, c0 + 2 * Q_BLOCK:c0 + 3 * Q_BLOCK] + p_c[:, c0 + 3 * Q_BLOCK:c0 + 4 * Q_BLOCK])
        imp_t = _dot(ovlt_ref[...], p_sum.astype(BF16))
        scores.append(jnp.where(valid, jnp.where(forced, imp_t + FORCE_SCORE, imp_t), -jnp.inf))
    picked = _select_blocks(jnp.concatenate(scores, axis=1), jnp.concatenate([kk, kk], axis=1))
    qaug_scr[0:LANES, :] = qt
    for h in range(N_KV):
        pk = picked[:, h * Q_BLOCK:(h + 1) * Q_BLOCK]
        selneg = jnp.where(valid, jnp.where(pk > 0.0, 0.0, NEG), NEG).astype(BF16)
        for g in range(GQA):
            c0 = (h * GQA + g) * Q_BLOCK
            qaug_scr[LANES:2 * LANES, c0:c0 + Q_BLOCK] = selneg

    acc_scr[...] = jnp.zeros(acc_scr.shape, F32)
    tiles_per_step = SEL_TILE // LANES

    def score(step, dst):
        t0 = pl.multiple_of(step * SEL_TILE, SEL_TILE)
        dst[...] = _dot(ksa_ref[0, pl.ds(t0, SEL_TILE), :], qaug_scr[...])

    def attend(src, step, m_prev, l_prev, causal):
        vt = jnp.concatenate([vsb_ref[0, step * tiles_per_step + i] for i in range(tiles_per_step)], axis=1)
        sc = src[...]
        if causal:
            tok = step * SEL_TILE + _row_iota((SEL_TILE, Q_BLOCK))
            sc = sc + _tile_heads(jnp.where(tok <= q0 + _lane_iota((SEL_TILE, Q_BLOCK)), 0.0, NEG))
        m_next = jnp.maximum(m_prev, jnp.max(sc, axis=0, keepdims=True))
        p = jnp.exp2(sc - m_next)
        alpha = jnp.exp2(m_prev - m_next)
        l_next = alpha * l_prev + jnp.sum(p, axis=0, keepdims=True)
        acc_scr[...] = alpha * acc_scr[...] + _dot(vt, p.astype(BF16))
        return m_next, l_next

    n_full = q0 // SEL_TILE
    score(0, sa_scr)

    def pair(jj, carry):
        m_c, l_c = carry
        score(2 * jj + 1, sb_scr)
        m_c, l_c = attend(sa_scr, 2 * jj, m_c, l_c, False)
        score(2 * jj + 2, sa_scr)
        return attend(sb_scr, 2 * jj + 1, m_c, l_c, False)

    m_run, l_run = lax.fori_loop(
        0, n_full // 2, pair,
        (jnp.full((1, cols), -jnp.inf, F32), jnp.zeros((1, cols), F32)))

    def odd_tail(m_c, l_c):
        score(n_full, sb_scr)
        m_c, l_c = attend(sa_scr, n_full - 1, m_c, l_c, False)
        return attend(sb_scr, n_full, m_c, l_c, True)[1]

    def even_tail(m_c, l_c):
        return attend(sa_scr, n_full, m_c, l_c, True)[1]

    l_run = lax.cond(n_full % 2 == 1, odd_tail, even_tail, m_run, l_run)

    gt = gate_ref[0, 0]
    o_s = acc_scr[...] * (1.0 / l_run)
    heads = []
    for e in range(N_HEADS):
        c0 = e * Q_BLOCK
        o_e = (gt[3 * e:3 * e + 1, :] * o_c[:, c0:c0 + Q_BLOCK]
               + gt[3 * e + 1:3 * e + 2, :] * o_s[:, c0:c0 + Q_BLOCK]
               + gt[3 * e + 2:3 * e + 3, :] * o_w[:, c0:c0 + Q_BLOCK])
        heads.append(o_e.T)
    lo = _lane_iota((Q_BLOCK, LANES)) < HEAD_DIM
    for j in range(N_HEADS // 2):
        a, b = heads[2 * j], heads[2 * j + 1]
        if j < 2:
            slab = jnp.where(lo, a, pltpu.roll(b, HEAD_DIM, 1))
        else:
            slab = jnp.where(lo, pltpu.roll(a, HEAD_DIM, 1), b)
        out_ref[0, :, j * LANES:(j + 1) * LANES] = (
            slab * szb_ref[0, :, j * LANES:(j + 1) * LANES]).astype(BF16)


def _attn_prompt(qt4, gate4, szb3, kcc, vcct, ovlt, ksa3, vsb4, kwb3, vwb4):
    b, nq = qt4.shape[0], qt4.shape[1]
    t = nq * Q_BLOCK
    cols = N_HEADS * Q_BLOCK
    qtile = lambda a: pl.BlockSpec((1, 1) + a.shape[2:], lambda bi, i: (bi, i, 0, 0))
    seq = lambda a: pl.BlockSpec((1,) + a.shape[1:], lambda bi, i, nd=a.ndim: (bi,) + (0,) * (nd - 1))
    return pl.pallas_call(
        _attn_prompt_kernel,
        grid=(b, nq),
        in_specs=[qtile(qt4), qtile(gate4), pl.BlockSpec((1, Q_BLOCK, D_B), lambda bi, i: (bi, i, 0)),
                  seq(kcc), seq(vcct), pl.BlockSpec(ovlt.shape, lambda bi, i: (0, 0)),
                  seq(ksa3), seq(vsb4), seq(kwb3), seq(vwb4)],
        out_specs=pl.BlockSpec((1, Q_BLOCK, D_B), lambda bi, i: (bi, i, 0)),
        out_shape=jax.ShapeDtypeStruct((b, t, D_B), BF16),
        scratch_shapes=[pltpu.VMEM((2 * LANES, cols), BF16),
                        pltpu.VMEM((KV_W, cols), F32),
                        pltpu.VMEM((SEL_TILE, cols), F32),
                        pltpu.VMEM((SEL_TILE, cols), F32)],
        compiler_params=pltpu.CompilerParams(
            dimension_semantics=("arbitrary", "arbitrary"), vmem_limit_bytes=VMEM_LIMIT_BYTES),
        name="attn_prompt",
    )(qt4, gate4, szb3, kcc, vcct, ovlt, ksa3, vsb4, kwb3, vwb4)


def _outproj_kernel(x_ref, a_ref, m_ref, wa_ref, wb_ref, fg_ref, y_ref):
    y = x_ref[...] + _dot(a_ref[...], wa_ref[...]) + _dot(m_ref[...].astype(BF16), wb_ref[...])
    y_ref[...] = _rms_rows(y, fg_ref[...])


def _outproj(x2, a_out, mix_b, wa, wb, fg, tm):
    n = x2.shape[0]
    row = lambda i: (i, 0)
    const2 = lambda i: (0, 0)
    return pl.pallas_call(
        _outproj_kernel,
        grid=(n // tm,),
        in_specs=[pl.BlockSpec((tm, D_MODEL), row),
                  pl.BlockSpec((tm, a_out.shape[1]), row),
                  pl.BlockSpec((tm, mix_b.shape[1]), row),
                  pl.BlockSpec(wa.shape, const2), pl.BlockSpec(wb.shape, const2),
                  pl.BlockSpec((1, D_MODEL), const2)],
        out_specs=pl.BlockSpec((tm, D_MODEL), row),
        out_shape=jax.ShapeDtypeStruct((n, D_MODEL), F32),
        compiler_params=pltpu.CompilerParams(
            dimension_semantics=("arbitrary",), vmem_limit_bytes=VMEM_LIMIT_BYTES),
        name="outproj",
    )(x2, a_out, mix_b, wa, wb, fg)


def _page_copies(pt_ref, pool_ref, buf_ref, sem_ref, seq, slot):
    return [pltpu.make_async_copy(pool_ref.at[pt_ref[seq, p]], buf_ref.at[slot, p], sem_ref.at[slot])
            for p in range(buf_ref.shape[1])]


def _gather_pages(pt_ref, pools, bufs, sems):
    b = pl.program_id(0)
    slot = lax.rem(b, 2)

    def start(seq, sl):
        for pool, buf, sem in zip(pools, bufs, sems):
            for cp in _page_copies(pt_ref, pool, buf, sem, seq, sl):
                cp.start()

    @pl.when(b == 0)
    def _():
        start(0, 0)

    @pl.when(b + 1 < pl.num_programs(0))
    def _():
        start(b + 1, 1 - slot)

    for pool, buf, sem in zip(pools, bufs, sems):
        for cp in _page_copies(pt_ref, pool, buf, sem, b, slot):
            cp.wait()
    return slot


def _decode_cmp_kernel(pt_ref, ck_hbm, cv_hbm, q_ref, wk_ref, wv_ref, pek_ref, pev_ref, rope_ref, ovl_ref,
                       oc_ref, psum_ref, kbuf, vbuf, ksem, vsem, ktok_scr, vtok_scr):
    slot = _gather_pages(pt_ref, (ck_hbm, cv_hbm), (kbuf, vbuf), (ksem, vsem))

    def chunk_rows(buf, tok_scr):
        for p in range(buf.shape[1]):
            tok_scr[p * PAGE_SIZE:(p + 1) * PAGE_SIZE, :] = buf[slot, p].T
        n = tok_scr.shape[0] // CMP_STRIDE
        return jnp.concatenate(
            [tok_scr[pl.ds(r, n, stride=CMP_STRIDE), :] for r in range(CMP_STRIDE)], axis=1)

    kch = chunk_rows(kbuf, ktok_scr)
    vch = chunk_rows(vbuf, vtok_scr)
    kcc = _rope(_compress_chunks(kch, pek_ref, wk_ref), rope_ref[...]).astype(BF16)
    vcc = _compress_chunks(vch, pev_ref, wv_ref).astype(BF16)
    n_chunks = kch.shape[0]
    q = q_ref[0]
    rows = q.shape[0]
    s = _dot_nt(q, kcc)
    seen = _lane_iota((rows, n_chunks)) < n_chunks - 1
    p_c = jnp.where(seen, _softmax_rows(jnp.where(seen, s, NEG)), 0.0)
    oc_ref[0] = _dot(p_c.astype(BF16), vcc)[0:N_HEADS]
    ps0 = jnp.sum(p_c[0:GQA], axis=0, keepdims=True)
    ps1 = jnp.sum(p_c[GQA:2 * GQA], axis=0, keepdims=True)
    psum = jnp.where(_row_iota((rows, n_chunks)) < N_HEADS // 2, ps0, ps1)
    psum_ref[0] = _dot(psum.astype(BF16), ovl_ref[...])[0:N_HEADS]


def _decode_cmp(page_table, ck_pool, cv_pool, q16, wk, wv, pek, pev, rope_tab, ovl):
    nb, n_pages = page_table.shape
    hbm = pl.BlockSpec(memory_space=pl.ANY)
    page_buf = pltpu.VMEM((2, n_pages, KV_W, PAGE_SIZE), F32)
    tok_buf = pltpu.VMEM((n_pages * PAGE_SIZE, KV_W), F32)
    const2 = lambda b, pt: (0, 0)
    per_b = lambda a: pl.BlockSpec((1,) + a.shape[1:], lambda b, pt: (b, 0, 0))
    in_specs = [hbm, hbm, per_b(q16),
                pl.BlockSpec(wk.shape, const2), pl.BlockSpec(wv.shape, const2),
                pl.BlockSpec(pek.shape, const2), pl.BlockSpec(pev.shape, const2),
                pl.BlockSpec(rope_tab.shape, const2), pl.BlockSpec(ovl.shape, const2)]
    out_blk = pl.BlockSpec((1, N_HEADS, LANES), lambda b, pt: (b, 0, 0))
    grid_spec = pltpu.PrefetchScalarGridSpec(
        num_scalar_prefetch=1, grid=(nb,), in_specs=in_specs, out_specs=[out_blk, out_blk],
        scratch_shapes=[page_buf, page_buf, pltpu.SemaphoreType.DMA((2,)), pltpu.SemaphoreType.DMA((2,)),
                        tok_buf, tok_buf])
    return pl.pallas_call(
        _decode_cmp_kernel,
        grid_spec=grid_spec,
        out_shape=[jax.ShapeDtypeStruct((nb, N_HEADS, LANES), F32)] * 2,
        compiler_params=pltpu.CompilerParams(
            dimension_semantics=("arbitrary",), vmem_limit_bytes=VMEM_LIMIT_BYTES),
        name="decode_cmp",
    )(page_table, ck_pool, cv_pool, q16, wk, wv, pek, pev, rope_tab, ovl)


def _decode_topk_kernel(imp_ref, sel_ref, *, n_blk, cur):
    n = imp_ref.shape[0]
    tiles = [imp_ref[i * LANES:(i + 1) * LANES, :].T for i in range(n // LANES)]
    imp_t = jnp.concatenate(tiles, axis=1)
    kk = _row_iota(imp_t.shape)
    forced = (kk == 0) | (kk == cur) | (kk == cur - 1)
    exists = kk < n_blk
    score = jnp.where(exists, jnp.where(forced, imp_t + FORCE_SCORE, imp_t), -jnp.inf)
    picked = _select_blocks(score, kk)
    selneg = jnp.where(exists, jnp.where(picked > 0.0, 0.0, NEG), NEG)
    for i in range(n // LANES):
        sel_ref[i * LANES:(i + 1) * LANES, :] = selneg[:, i * LANES:(i + 1) * LANES].T


def _decode_topk(imp2, n_blk, cur):
    return pl.pallas_call(
        functools.partial(_decode_topk_kernel, n_blk=n_blk, cur=cur),
        out_shape=jax.ShapeDtypeStruct(imp2.shape, F32),
        compiler_params=pltpu.CompilerParams(vmem_limit_bytes=VMEM_LIMIT_BYTES),
        name="decode_topk",
    )(imp2)


def _decode_attend(s, q32, vals_t_bf, k_new, v_new, bias_new):
    kn = k_new.astype(BF16).astype(F32)
    s_new = jnp.sum(q32 * kn, axis=1, keepdims=True) + bias_new
    mx = jnp.maximum(jnp.max(s, axis=1, keepdims=True), s_new)
    ex = jnp.exp(s - mx)
    ex_new = jnp.exp(s_new - mx)
    inv = 1.0 / (jnp.sum(ex, axis=1, keepdims=True) + ex_new)
    p = (ex * inv).astype(BF16)
    p_new = (ex_new * inv).astype(BF16).astype(F32)
    return _dot_nt(p, vals_t_bf) + p_new * v_new.astype(BF16).astype(F32)


P_Q, P_SEL, P_OC, P_SZB, P_NEW, P_GATE, P_ROWS = 0, 16, 32, 40, 48, 56, 64


def _row_to_col(row):
    n = row.shape[1]
    diag = _row_iota((n, n)) == _lane_iota((n, n))
    return jnp.sum(jnp.where(diag, row, 0.0), axis=1, keepdims=True)


def _decode_sel_kernel(pt_ref, ks_hbm, vs_hbm, ckw_ref, cvw_ref, pack_ref, onehot_ref,
                       mix_ref, okw_ref, ovw_ref, kbuf, vbuf, ksem, vsem, *, cur):
    slot = _gather_pages(pt_ref, (ks_hbm, vs_hbm), (kbuf, vbuf), (ksem, vsem))
    n_pages = kbuf.shape[1]
    pk = pack_ref[0]
    q32 = pk[P_Q:P_Q + 2 * N_HEADS]
    q = q32.astype(BF16)
    sel = pk[P_SEL:P_SEL + 2 * N_HEADS].astype(BF16)
    rows = q.shape[0]
    ks_new, vs_new = pk[P_NEW:P_NEW + 1], pk[P_NEW + 1:P_NEW + 2]
    kw_new, vw_new = pk[P_NEW + 2:P_NEW + 3], pk[P_NEW + 3:P_NEW + 4]

    kt = jnp.concatenate([kbuf[slot, p] for p in range(n_pages)], axis=1).astype(BF16)
    vt = jnp.concatenate([vbuf[slot, p] for p in range(n_pages)], axis=1).astype(BF16)
    kaug = jnp.concatenate([kt, onehot_ref[...]], axis=0)
    qaug = jnp.concatenate([q, sel], axis=1)
    s_sel = _dot(qaug, kaug)
    o_s = _decode_attend(s_sel, q32, vt, ks_new, vs_new, sel.astype(F32)[:, cur:cur + 1])

    kwin = ckw_ref[0]
    vwin = cvw_ref[0]
    keep = kwin.shape[1]
    visible = _lane_iota((rows, keep)) > keep - WINDOW
    s_win = jnp.where(visible, _dot(q, kwin.astype(BF16)), NEG)
    o_w = _decode_attend(s_win, q32, vwin.astype(BF16), kw_new, vw_new, 0.0)

    g = pk[P_GATE:P_GATE + N_HEADS]
    o = (g[:, 0:1] * pk[P_OC:P_OC + N_HEADS] + g[:, 1:2] * o_s[0:N_HEADS] + g[:, 2:3] * o_w[0:N_HEADS])
    mix_ref[0] = o * pk[P_SZB:P_SZB + N_HEADS]

    last = _lane_iota((KV_W, keep)) == keep - 1
    okw_ref[0] = jnp.where(last, _row_to_col(kw_new), pltpu.roll(kwin, keep - 1, 1))
    ovw_ref[0] = jnp.where(last, _row_to_col(vw_new), pltpu.roll(vwin, keep - 1, 1))


def _decode_sel(page_table, ks_pool, vs_pool, ckw, cvw, pack, onehot, cur):
    nb, n_pages = page_table.shape
    keep = ckw.shape[2]
    hbm = pl.BlockSpec(memory_space=pl.ANY)
    page_buf = pltpu.VMEM((2, n_pages, KV_W, PAGE_SIZE), F32)
    per_b = lambda a: pl.BlockSpec((1,) + a.shape[1:], lambda b, pt: (b, 0, 0))
    in_specs = [hbm, hbm, per_b(ckw), per_b(cvw), per_b(pack),
                pl.BlockSpec(onehot.shape, lambda b, pt: (0, 0))]
    out_specs = [pl.BlockSpec((1, N_HEADS, LANES), lambda b, pt: (b, 0, 0)),
                 pl.BlockSpec((1, KV_W, keep), lambda b, pt: (b, 0, 0)),
                 pl.BlockSpec((1, KV_W, keep), lambda b, pt: (b, 0, 0))]
    grid_spec = pltpu.PrefetchScalarGridSpec(
        num_scalar_prefetch=1, grid=(nb,), in_specs=in_specs, out_specs=out_specs,
        scratch_shapes=[page_buf, page_buf, pltpu.SemaphoreType.DMA((2,)), pltpu.SemaphoreType.DMA((2,))])
    return pl.pallas_call(
        functools.partial(_decode_sel_kernel, cur=cur),
        grid_spec=grid_spec,
        out_shape=[jax.ShapeDtypeStruct((nb, N_HEADS, LANES), F32),
                   jax.ShapeDtypeStruct((nb, KV_W, keep), F32),
                   jax.ShapeDtypeStruct((nb, KV_W, keep), F32)],
        compiler_params=pltpu.CompilerParams(
            dimension_semantics=("arbitrary",), vmem_limit_bytes=VMEM_LIMIT_BYTES),
        name="decode_sel",
    )(page_table, ks_pool, vs_pool, ckw, cvw, pack, onehot)


def _rope_table(pos):
    half = ROT_DIM // 2
    inv = jnp.power(jnp.float32(ROPE_THETA), -jnp.arange(half, dtype=F32) / half)
    ang = pos.astype(F32)[:, None] * inv[None, :]
    cs = jnp.concatenate([jnp.cos(ang), jnp.sin(ang)], axis=1)
    lane = jnp.arange(3 * LANES)
    seg, l64 = lane // LANES, lane % HEAD_DIM
    src = jnp.where(seg == 0, l64 % half, half + l64 % half)
    sign = jnp.where(seg == 0, l64 < ROT_DIM,
                     jnp.where(seg == 1, l64 < half, (l64 >= half) & (l64 < ROT_DIM))).astype(F32)
    sign = jnp.where(seg == 1, -sign, sign)
    place = (jnp.arange(ROT_DIM)[:, None] == src[None, :]).astype(F32) * sign[None, :]
    ones = ((seg == 0) & (l64 >= ROT_DIM)).astype(F32)
    return jnp.dot(cs, place, precision=lax.Precision.HIGHEST) + ones[None, :]


def _compress_weights(w_c, pe_c):
    w4 = w_c.reshape(2, CMP_STRIDE, HEAD_DIM, HEAD_DIM)
    wb = jnp.einsum('arde,hg->rhdage', w4, jnp.eye(N_KV, dtype=w_c.dtype))
    wb = wb.reshape(CMP_STRIDE * KV_W, 2 * KV_W).astype(BF16)
    pe = jnp.broadcast_to(pe_c.reshape(2, CMP_STRIDE, 1, HEAD_DIM), (2, CMP_STRIDE, N_KV, HEAD_DIM))
    return wb, pe.reshape(2, CMP_STRIDE * KV_W)


def _overlap_matrix(n_rows, n_blk):
    cs = jnp.arange(n_rows, dtype=jnp.int32)[:, None] * CMP_STRIDE
    bs = jnp.arange(LANES, dtype=jnp.int32)[None, :] * SEL_BLOCK
    hit = (cs < bs + SEL_BLOCK) & (cs + CMP_BLOCK > bs) & (jnp.arange(LANES)[None, :] < n_blk)
    return hit.astype(BF16)


def kernel(x_prompt, x_sample, cache_k_cmp, cache_v_cmp, cache_k_sel, cache_v_sel, cache_k_win,
           cache_v_win, page_table, norm_g, w_in, ln_v_g, ln_v_b, w_s, b_s, w_ck, pe_ck, w_cv,
           pe_cv, w_out, final_g):
    depth = norm_g.shape[0]
    assert depth == 1
    bsz, seq, _ = x_prompt.shape
    nb, dec_seq, _ = x_sample.shape
    assert dec_seq == 1 and seq % SEL_TILE == 0 and seq >= WIN_SPAN and (seq // CMP_STRIDE) % LANES == 0
    n_pages = page_table.shape[1]
    past = n_pages * PAGE_SIZE
    keep = cache_k_win.shape[2]
    assert keep == WINDOW and past % SEL_BLOCK == 0

    w = w_in[0]
    w_pad = jnp.concatenate(
        [w[:, :C_G + N_GATE_COLS], jnp.zeros((D_MODEL, LANES - N_GATE_COLS), w.dtype),
         w[:, C_G + N_GATE_COLS:]], axis=1).astype(BF16)
    g_row = norm_g[0][None, :]
    lng = ln_v_g[0][None, :]
    lnb = ln_v_b[0][None, :]
    fg = final_g[None, :]
    wk_c, pek = _compress_weights(w_ck[0], pe_ck[0])
    wv_c, pev = _compress_weights(w_cv[0], pe_cv[0])
    w_o = w_out[0].astype(BF16)
    wo_a, wo_b = w_o[:D_A], w_o[D_A:]

    tril = jnp.tril(jnp.ones((CHUNK, CHUNK), w_s.dtype))
    ws_bf = (w_s[0] * tril).astype(BF16)
    bias_full = jnp.repeat(b_s[0].T, HEAD_DIM, axis=1)
    xp2 = x_prompt.reshape(bsz * seq, D_MODEL)
    (a_out, qt, kc, vc, kc_t, vc_t, ks_t, vs_t, kw_t, vw_t, ksa, kwb, vsb, vwb, gates_t, szb) = _inproj_prompt(
        xp2, g_row, w_pad, lng, lnb, ws_bf, bias_full,
        _rope_table(jnp.arange(seq, dtype=jnp.int32)), seq)
    n_chunks = seq // CMP_STRIDE
    kcc, vcc_t = _compress_prompt(
        kc.reshape(bsz, n_chunks, CMP_STRIDE * KV_W), vc.reshape(bsz, n_chunks, CMP_STRIDE * KV_W),
        wk_c, wv_c, pek, pev,
        _rope_table(jnp.arange(n_chunks, dtype=jnp.int32) * CMP_STRIDE))
    b3 = lambda a: a.reshape(bsz, seq, a.shape[-1])
    mix_p = _attn_prompt(qt, gates_t, b3(szb), kcc, vcc_t,
                         _overlap_matrix(n_chunks, seq // SEL_BLOCK).T,
                         b3(ksa), vsb, b3(kwb), vwb)
    y_prompt = _outproj(xp2, a_out, mix_p.reshape(bsz * seq, D_B), wo_a, wo_b, fg, 512)
    y_prompt = y_prompt.reshape(bsz, seq, D_MODEL)
    st = lambda a: jnp.transpose(a.reshape(a.shape[0], N_KV, HEAD_DIM, a.shape[2]), (0, 3, 1, 2))[None]
    keep_p = min(WINDOW, seq)

    xs2 = x_sample.reshape(nb, D_MODEL)
    wsv = jnp.repeat(w_s[0][:, 0, 0], HEAD_DIM)[None, :]
    bsv = jnp.repeat(b_s[0][:, 0], HEAD_DIM)[None, :]
    (a_s, qa_s, kc_s, vc_s, ks_s, vs_s, kw_s, vw_s, gates_s, szb_s, vn_s) = _inproj_decode(
        xs2, g_row, w_pad, lng, lnb, wsv, bsv,
        _rope_table(jnp.full((nb,), past, dtype=jnp.int32)))
    q16 = jnp.pad(qa_s.reshape(nb, N_HEADS, LANES), ((0, 0), (0, N_HEADS), (0, 0)))
    d_chunks = past // CMP_STRIDE
    n_blk = -(-(past + 1) // SEL_BLOCK)
    cur = past // SEL_BLOCK
    kv_t = lambda c: jnp.transpose(c[0], (0, 2, 3, 1)).reshape(c.shape[1], KV_W, c.shape[2])
    oc_s, imp_s = _decode_cmp(
        page_table, kv_t(cache_k_cmp), kv_t(cache_v_cmp),
        q16, wk_c, wv_c, pek, pev,
        _rope_table(jnp.arange(d_chunks, dtype=jnp.int32) * CMP_STRIDE),
        _overlap_matrix(d_chunks, n_blk))
    selneg = _decode_topk(imp_s[:, ::GQA, :].reshape(nb * N_KV, LANES), n_blk, cur)
    sel16 = jnp.pad(jnp.repeat(selneg.reshape(nb, N_KV, LANES), GQA, axis=1), ((0, 0), (0, N_HEADS), (0, 0)))
    tok_blk = jnp.arange(past, dtype=jnp.int32)[None, :] // SEL_BLOCK
    onehot = (tok_blk == jnp.arange(LANES, dtype=jnp.int32)[:, None]).astype(BF16)
    gate_rows = jnp.pad(gates_s[:, :N_GATE_COLS].reshape(nb, N_HEADS, 3), ((0, 0), (0, 0), (0, LANES - 3)))
    new_rows = jnp.pad(jnp.stack([ks_s, vs_s, kw_s, vw_s], axis=1), ((0, 0), (0, P_GATE - P_NEW - 4), (0, 0)))
    pack = jnp.concatenate(
        [q16.astype(F32), sel16, oc_s, szb_s.reshape(nb, N_HEADS, LANES), new_rows, gate_rows], axis=1)
    assert pack.shape[1] == P_ROWS
    mix_s, okw, ovw = _decode_sel(
        page_table, kv_t(cache_k_sel), kv_t(cache_v_sel), kv_t(cache_k_win), kv_t(cache_v_win),
        pack, onehot, cur)
    wo_slab = jnp.zeros((N_HEADS, N_KV, HEAD_DIM, D_MODEL), BF16)
    wo_heads = wo_b.reshape(N_HEADS, HEAD_DIM, D_MODEL)
    for e in range(N_HEADS):
        wo_slab = wo_slab.at[e, e // GQA].set(wo_heads[e])
    y_sample = _outproj(xs2, a_s, mix_s.reshape(nb, N_HEADS * LANES), wo_a,
                        wo_slab.reshape(N_HEADS * LANES, D_MODEL), fg, nb)
    y_sample = y_sample.reshape(nb, 1, D_MODEL)
    ss = lambda a: a.reshape(1, nb, 1, N_KV, HEAD_DIM)

    return (y_prompt, y_sample,
            st(kc_t), st(vc_t), st(ks_t), st(vs_t),
            st(kw_t[:, :, seq - keep_p:]), st(vw_t[:, :, seq - keep_p:]),
            ss(kc_s), ss(vc_s), ss(ks_s), ss(vs_s),
            st(okw), st(ovw),
            vn_s.reshape(1, nb, 1, D_A))
```

```python
import functools

import jax
import jax.numpy as jnp
from jax import lax
from jax.experimental import pallas as pl
from jax.experimental.pallas import tpu as pltpu

F32 = jnp.float32
BF16 = jnp.bfloat16

D_MODEL = 1024
HEAD_DIM = 64
D_A = 512
D_B = 512
A_GROUPS = 8
CHUNK = 128
N_HEADS = 8
N_KV = 2
GQA = 4
KV_W = 128
ROT_DIM = 16
ROPE_THETA = 500000.0
CMP_BLOCK = 32
CMP_STRIDE = 16
SEL_BLOCK = 64
SEL_SHIFT = 6
N_SELECT = 16
WINDOW = 512
Q_BLOCK = 128
PAGE_SIZE = 128
NORM_EPS = 1e-6
FORCE_SCORE = 1e4
NEG = -1e30
LOG2_E = 1.4426950408889634

LANES = 128
VMEM_LIMIT_BYTES = 56 * 1024 * 1024

C_U, C_V, C_ZA, C_Q = 0, 512, 1024, 1536
C_KC, C_VC, C_KS, C_VS, C_KW, C_VW = 2048, 2176, 2304, 2432, 2560, 2688
C_G, C_ZB, C_END = 2816, 2944, 3456
N_GATE_COLS = 3 * N_HEADS

SEL_TILE = 512
WIN_SPAN = WINDOW + Q_BLOCK

_NT = (((1,), (1,)), ((), ()))


def _dot(a, b):
    return jnp.dot(a, b, preferred_element_type=F32)


def _dot_nt(a, b):
    return lax.dot_general(a, b, _NT, preferred_element_type=F32)


def _lane_iota(shape):
    return lax.broadcasted_iota(jnp.int32, shape, len(shape) - 1)


def _row_iota(shape):
    return lax.broadcasted_iota(jnp.int32, shape, len(shape) - 2)


def _rope(x, tab):
    c = tab[:, 0:LANES]
    s1 = tab[:, LANES:2 * LANES]
    s2 = tab[:, 2 * LANES:3 * LANES]
    return x * c + pltpu.roll(x, LANES - ROT_DIM // 2, 1) * s1 + pltpu.roll(x, ROT_DIM // 2, 1) * s2


def _rms_rows(x, g):
    ms = jnp.mean(x * x, axis=-1, keepdims=True)
    return x * lax.rsqrt(ms + NORM_EPS) * g


def _layer_norm_rows(v, g, b):
    mu = jnp.mean(v, axis=-1, keepdims=True)
    vc = v - mu
    var = jnp.mean(vc * vc, axis=-1, keepdims=True)
    return vc * lax.rsqrt(var + NORM_EPS) * g + b


def _head_slabs(x512):
    out = []
    rows = x512.shape[0]
    lane = _lane_iota((rows, LANES))
    lo = lane < HEAD_DIM
    for j in range(4):
        slab = x512[:, j * LANES:(j + 1) * LANES]
        swapped = pltpu.roll(slab, HEAD_DIM, 1)
        if j < 2:
            out.append(jnp.where(lo, slab, 0.0))
            out.append(jnp.where(lo, swapped, 0.0))
        else:
            out.append(jnp.where(lo, 0.0, swapped))
            out.append(jnp.where(lo, 0.0, slab))
    return out


def _project(x_ref, g_ref, w_ref, z_scr):
    hn = _rms_rows(x_ref[...], g_ref[...])
    z_scr[...] = _dot(hn.astype(BF16), w_ref[...])


def _inproj_prompt_kernel(x_ref, g_ref, w_ref, lng_ref, lnb_ref, ws_ref, bias_ref, rope_ref,
                          aout_ref, qt_ref, kc_ref, vc_ref,
                          kct_ref, vct_ref, kst_ref, vst_ref, kwt_ref, vwt_ref,
                          ksa_ref, kwb_ref, vsb_ref, vwb_ref, gate_ref, szb_ref,
                          z_scr, kstage_scr, vstage_scr, *, tm, tiles_per_batch):
    _project(x_ref, g_ref, w_ref, z_scr)
    tab = rope_ref[...]

    lane = _lane_iota((CHUNK, LANES))
    lo = lane < HEAD_DIM
    for c in range(tm // CHUNK):
        r0 = c * CHUNK
        u = z_scr[r0:r0 + CHUNK, C_U:C_U + D_A]
        v = z_scr[r0:r0 + CHUNK, C_V:C_V + D_A]
        za = z_scr[r0:r0 + CHUNK, C_ZA:C_ZA + D_A]
        vn = _layer_norm_rows(v, lng_ref[...], lnb_ref[...])
        parts = []
        for p in range(A_GROUPS // 2):
            vp = vn[:, p * LANES:(p + 1) * LANES]
            v_lo = jnp.where(lo, vp, 0.0).astype(BF16)
            v_hi = jnp.where(lo, 0.0, vp).astype(BF16)
            parts.append(_dot(ws_ref[2 * p], v_lo) + _dot(ws_ref[2 * p + 1], v_hi))
        s = jnp.concatenate(parts, axis=1) + bias_ref[...]
        aout_ref[r0:r0 + CHUNK, :] = (u * s * jax.nn.silu(za)).astype(BF16)

    q = z_scr[:, C_Q:C_Q + D_B]
    qr = jnp.concatenate(
        [_rope(q[:, j * LANES:(j + 1) * LANES], tab) for j in range(4)], axis=1) * (HEAD_DIM ** -0.5 * LOG2_E)
    slabs = _head_slabs(qr)
    kc = z_scr[:, C_KC:C_KC + KV_W]
    vc = z_scr[:, C_VC:C_VC + KV_W]
    for src, stage, dst in ((kc, kstage_scr, kc_ref), (vc, vstage_scr, vc_ref)):
        stage[...] = src
        for r in range(CMP_STRIDE):
            dst[:, r * KV_W:(r + 1) * KV_W] = stage[pl.ds(r, tm // CMP_STRIDE, stride=CMP_STRIDE), :]
    ks = _rope(z_scr[:, C_KS:C_KS + KV_W], tab)
    vs = z_scr[:, C_VS:C_VS + KV_W]
    kw = _rope(z_scr[:, C_KW:C_KW + KV_W], tab)
    vw = z_scr[:, C_VW:C_VW + KV_W]
    gates = jax.nn.sigmoid(z_scr[:, C_G:C_G + LANES])
    pos = (pl.program_id(0) % tiles_per_batch) * tm + _row_iota((tm, LANES))
    onehot = jnp.where((pos >> SEL_SHIFT) == _lane_iota((tm, LANES)), 1.0, 0.0)
    ksa_ref[:, 0:LANES] = ks.astype(BF16)
    ksa_ref[:, LANES:2 * LANES] = onehot.astype(BF16)
    kwb_ref[...] = kw.astype(BF16)
    for j in range(tm // LANES):
        r0, r1 = j * LANES, (j + 1) * LANES
        kct_ref[0, :, r0:r1] = kc[r0:r1].T
        vct_ref[0, :, r0:r1] = vc[r0:r1].T
        kst_ref[0, :, r0:r1] = ks[r0:r1].T
        kwt_ref[0, :, r0:r1] = kw[r0:r1].T
        vs_t, vw_t = vs[r0:r1].T, vw[r0:r1].T
        vst_ref[0, :, r0:r1] = vs_t
        vwt_ref[0, :, r0:r1] = vw_t
        vsb_ref[0, j] = vs_t.astype(BF16)
        vwb_ref[0, j] = vw_t.astype(BF16)
        for e in range(N_HEADS):
            qt_ref[0, j, :, e * Q_BLOCK:(e + 1) * Q_BLOCK] = slabs[e][r0:r1].T.astype(BF16)
        gate_ref[0, j] = gates[r0:r1].T
    szb_ref[...] = jax.nn.silu(z_scr[:, C_ZB:C_ZB + D_B])


def _inproj_prompt(x2, g, w, lng, lnb, ws_bf, bias_full, rope_tab, seq, tm=256):
    n = x2.shape[0]
    tiles_per_batch = seq // tm
    bsz = n // seq
    n_tok_tiles = seq // LANES
    row = lambda i: (i, 0)
    const2 = lambda i: (0, 0)
    rows2d = lambda width, dtype: (jax.ShapeDtypeStruct((n, width), dtype), pl.BlockSpec((tm, width), row))
    kv_t = (jax.ShapeDtypeStruct((bsz, KV_W, seq), F32),
            pl.BlockSpec((1, KV_W, tm), lambda i: (i // tiles_per_batch, 0, i % tiles_per_batch)))
    chunk_rows = (jax.ShapeDtypeStruct((n // CMP_STRIDE, CMP_STRIDE * KV_W), F32),
                  pl.BlockSpec((tm // CMP_STRIDE, CMP_STRIDE * KV_W), row))
    tiles = lambda r, c, dtype: (
        jax.ShapeDtypeStruct((bsz, n_tok_tiles, r, c), dtype),
        pl.BlockSpec((1, tm // LANES, r, c), lambda i: (i // tiles_per_batch, i % tiles_per_batch, 0, 0)))
    outs = [
        rows2d(D_A, BF16),
        tiles(KV_W, N_HEADS * Q_BLOCK, BF16),
        chunk_rows, chunk_rows,
        kv_t, kv_t, kv_t, kv_t, kv_t, kv_t,
        rows2d(2 * KV_W, BF16),
        rows2d(KV_W, BF16),
        tiles(KV_W, LANES, BF16),
        tiles(KV_W, LANES, BF16),
        tiles(LANES, LANES, F32),
        rows2d(D_B, F32),
    ]
    out_shapes = [o[0] for o in outs]
    out_specs = [o[1] for o in outs]
    in_specs = [
        pl.BlockSpec((tm, D_MODEL), row),
        pl.BlockSpec((1, D_MODEL), const2),
        pl.BlockSpec((D_MODEL, C_END), const2),
        pl.BlockSpec((1, D_A), const2),
        pl.BlockSpec((1, D_A), const2),
        pl.BlockSpec((A_GROUPS, CHUNK, CHUNK), lambda i: (0, 0, 0)),
        pl.BlockSpec((CHUNK, D_A), const2),
        pl.BlockSpec((tm, 3 * LANES), lambda i: (i % tiles_per_batch, 0)),
    ]
    return pl.pallas_call(
        functools.partial(_inproj_prompt_kernel, tm=tm, tiles_per_batch=tiles_per_batch),
        grid=(n // tm,),
        in_specs=in_specs,
        out_specs=out_specs,
        out_shape=out_shapes,
        scratch_shapes=[pltpu.VMEM((tm, C_END), F32), pltpu.VMEM((tm, KV_W), F32), pltpu.VMEM((tm, KV_W), F32)],
        compiler_params=pltpu.CompilerParams(
            dimension_semantics=("arbitrary",), vmem_limit_bytes=VMEM_LIMIT_BYTES),
        name="inproj_prompt",
    )(x2, g, w, lng, lnb, ws_bf, bias_full, rope_tab)


def _inproj_decode_kernel(x_ref, g_ref, w_ref, lng_ref, lnb_ref, wsv_ref, bsv_ref, rope_ref,
                          aout_ref, qa_ref, kc_ref, vc_ref, ks_ref, vs_ref, kw_ref, vw_ref,
                          gate_ref, szb_ref, vn_ref, z_scr):
    _project(x_ref, g_ref, w_ref, z_scr)
    tab = rope_ref[...]
    u = z_scr[:, C_U:C_U + D_A]
    v = z_scr[:, C_V:C_V + D_A]
    za = z_scr[:, C_ZA:C_ZA + D_A]
    vn = _layer_norm_rows(v, lng_ref[...], lnb_ref[...])
    vn_ref[...] = vn
    s = vn * wsv_ref[...] + bsv_ref[...]
    aout_ref[...] = (u * s * jax.nn.silu(za)).astype(BF16)

    q = z_scr[:, C_Q:C_Q + D_B]
    qr = jnp.concatenate(
        [_rope(q[:, j * LANES:(j + 1) * LANES], tab) for j in range(4)], axis=1) * (HEAD_DIM ** -0.5)
    for e, slab in enumerate(_head_slabs(qr)):
        qa_ref[:, e * LANES:(e + 1) * LANES] = slab.astype(BF16)
    kc_ref[...] = z_scr[:, C_KC:C_KC + KV_W]
    vc_ref[...] = z_scr[:, C_VC:C_VC + KV_W]
    ks_ref[...] = _rope(z_scr[:, C_KS:C_KS + KV_W], tab)
    vs_ref[...] = z_scr[:, C_VS:C_VS + KV_W]
    kw_ref[...] = _rope(z_scr[:, C_KW:C_KW + KV_W], tab)
    vw_ref[...] = z_scr[:, C_VW:C_VW + KV_W]
    gate_ref[...] = jax.nn.sigmoid(z_scr[:, C_G:C_G + LANES])
    szb = jax.nn.silu(z_scr[:, C_ZB:C_ZB + D_B])
    for e, slab in enumerate(_head_slabs(szb)):
        szb_ref[:, e * LANES:(e + 1) * LANES] = slab


def _inproj_decode(x2, g, w, lng, lnb, wsv, bsv, rope_tab):
    n = x2.shape[0]
    out_shapes = [
        jax.ShapeDtypeStruct((n, D_A), BF16),
        jax.ShapeDtypeStruct((n, N_HEADS * LANES), BF16),
        jax.ShapeDtypeStruct((n, KV_W), F32),
        jax.ShapeDtypeStruct((n, KV_W), F32),
        jax.ShapeDtypeStruct((n, KV_W), F32),
        jax.ShapeDtypeStruct((n, KV_W), F32),
        jax.ShapeDtypeStruct((n, KV_W), F32),
        jax.ShapeDtypeStruct((n, KV_W), F32),
        jax.ShapeDtypeStruct((n, LANES), F32),
        jax.ShapeDtypeStruct((n, N_HEADS * LANES), F32),
        jax.ShapeDtypeStruct((n, D_A), F32),
    ]
    return pl.pallas_call(
        _inproj_decode_kernel,
        out_shape=out_shapes,
        scratch_shapes=[pltpu.VMEM((n, C_END), F32)],
        compiler_params=pltpu.CompilerParams(vmem_limit_bytes=VMEM_LIMIT_BYTES),
        name="inproj_decode",
    )(x2, g, w, lng, lnb, wsv, bsv, rope_tab)


def _compress_chunks(ch, pe_ref, w_ref):
    a = _dot((ch + pe_ref[0:1, :]).astype(BF16), w_ref[:, 0:KV_W])
    b = _dot((ch + pe_ref[1:2, :]).astype(BF16), w_ref[:, KV_W:2 * KV_W])
    return a + pltpu.roll(b, ch.shape[0] - 1, 0)


def _compress_prompt_kernel(kc_ref, vc_ref, wk_ref, wv_ref, pek_ref, pev_ref, rope_ref, kcc_ref, vcct_ref):
    kcc = _compress_chunks(kc_ref[0], pek_ref, wk_ref)
    kcc_ref[0] = _rope(kcc, rope_ref[...]).astype(BF16)
    vcc = _compress_chunks(vc_ref[0], pev_ref, wv_ref)
    for j in range(vcc.shape[0] // LANES):
        vcct_ref[0, :, j * LANES:(j + 1) * LANES] = vcc[j * LANES:(j + 1) * LANES].T.astype(BF16)


def _compress_prompt(kc3, vc3, wk, wv, pek, pev, rope_tab):
    b, c, width = kc3.shape
    blk = pl.BlockSpec((1, c, width), lambda i: (i, 0, 0))
    const2 = lambda i: (0, 0)
    return pl.pallas_call(
        _compress_prompt_kernel,
        grid=(b,),
        in_specs=[blk, blk,
                  pl.BlockSpec(wk.shape, const2), pl.BlockSpec(wv.shape, const2),
                  pl.BlockSpec(pek.shape, const2), pl.BlockSpec(pev.shape, const2),
                  pl.BlockSpec(rope_tab.shape, const2)],
        out_specs=[pl.BlockSpec((1, c, KV_W), lambda i: (i, 0, 0)),
                   pl.BlockSpec((1, KV_W, c), lambda i: (i, 0, 0))],
        out_shape=[jax.ShapeDtypeStruct((b, c, KV_W), BF16), jax.ShapeDtypeStruct((b, KV_W, c), BF16)],
        compiler_params=pltpu.CompilerParams(
            dimension_semantics=("arbitrary",), vmem_limit_bytes=VMEM_LIMIT_BYTES),
        name="compress_prompt",
    )(kc3, vc3, wk, wv, pek, pev, rope_tab)


def _select_blocks(score, kk):
    picked = jnp.zeros(score.shape, F32)
    for _ in range(N_SELECT):
        cm = jnp.max(score, axis=0, keepdims=True)
        first = jnp.min(jnp.where(score == cm, kk, LANES), axis=0, keepdims=True)
        hit = kk == first
        picked = jnp.where(hit, 1.0, picked)
        score = jnp.where(hit, -jnp.inf, score)
    return picked


def _softmax_rows(s):
    mx = jnp.max(s, axis=1, keepdims=True)
    ex = jnp.exp(s - mx)
    return ex * (1.0 / jnp.sum(ex, axis=1, keepdims=True))


def _tile_heads(x):
    return jnp.concatenate([x] * N_HEADS, axis=1)


def _values_by_kv_head(vt, p):
    half = GQA * Q_BLOCK
    return jnp.concatenate(
        [_dot(vt[h * HEAD_DIM:(h + 1) * HEAD_DIM], p[:, h * half:(h + 1) * half]) for h in range(N_KV)], axis=1)


def _attn_prompt_kernel(qt_ref, gate_ref, szb_ref, kcc_ref, vcct_ref, ovlt_ref,
                        ksa_ref, vsb_ref, kwb_ref, vwb_ref, out_ref,
                        qaug_scr, acc_scr, sa_scr, sb_scr):
    i_blk = pl.program_id(1)
    q0 = i_blk * Q_BLOCK
    cols = N_HEADS * Q_BLOCK
    qt = qt_ref[0, 0]

    n_cmp = kcc_ref.shape[1]

    def compressed(extent):
        t_c = q0 + _lane_iota((extent, Q_BLOCK))
        seen = (_row_iota((extent, Q_BLOCK)) * CMP_STRIDE + (CMP_BLOCK - 1)) <= t_c
        s = _dot(kcc_ref[0, 0:extent, :], qt) + _tile_heads(jnp.where(seen, 0.0, NEG))
        mx = jnp.max(s, axis=0, keepdims=True)
        ex = jnp.exp2(s - mx)
        inv = jnp.where(mx > 0.5 * NEG, 1.0 / jnp.sum(ex, axis=0, keepdims=True), 0.0)
        p_c = ex * inv
        imps = []
        for h in range(N_KV):
            c0 = h * GQA * Q_BLOCK
            p_sum = (p_c[:, c0:c0 + Q_BLOCK] + p_c[:, c0 + Q_BLOCK:c0 + 2 * Q_BLOCK]
                     + p_c[:, c0 + 2 * Q_BLOCK:c0 + 3 * Q_BLOCK] + p_c[:, c0 + 3 * Q_BLOCK:c0 + 4 * Q_BLOCK])
            imps.append(_dot(ovlt_ref[:, 0:extent], p_sum.astype(BF16)))
        return (_values_by_kv_head(vcct_ref[0, :, 0:extent], p_c.astype(BF16)), *imps)

    n_need = (q0 + Q_BLOCK) // CMP_STRIDE
    extents = list(range(LANES, n_cmp + 1, LANES))
    o_c, *imp_t = lax.switch(
        jnp.minimum((n_need - 1) // LANES, len(extents) - 1),
        [functools.partial(compressed, ext) for ext in extents])

    w0 = pl.multiple_of(jnp.maximum(q0 - WINDOW, 0), Q_BLOCK)
    wj0 = jnp.maximum(i_blk - WINDOW // Q_BLOCK, 0)
    kwin = kwb_ref[0, pl.ds(w0, WIN_SPAN), :]
    vwt = jnp.concatenate([vwb_ref[0, wj0 + i] for i in range(WIN_SPAN // LANES)], axis=1)
    kp = w0 + _row_iota((WIN_SPAN, Q_BLOCK))
    tw = q0 + _lane_iota((WIN_SPAN, Q_BLOCK))
    in_win = jnp.where(kp <= tw, jnp.where(kp > tw - WINDOW, 0.0, NEG), NEG)
    sw = _dot(kwin, qt) + _tile_heads(in_win)
    exw = jnp.exp2(sw - jnp.max(sw, axis=0, keepdims=True))
    o_w = _values_by_kv_head(vwt, exw.astype(BF16)) * (1.0 / jnp.sum(exw, axis=0, keepdims=True))

    kk = _row_iota((LANES, Q_BLOCK))
    tq = q0 + _lane_iota((LANES, Q_BLOCK))
    valid = kk * SEL_BLOCK <= tq
    cur = tq >> SEL_SHIFT
    forced = (kk == 0) | (kk == cur) | (kk == cur - 1)
    scores = []
    for h in range(N_KV):
        scores.append(jnp.where(valid, jnp.where(forced, imp_t[h] + FORCE_SCORE, imp_t[h]), -jnp.inf))
    picked = _select_blocks(jnp.concatenate(scores, axis=1), jnp.concatenate([kk, kk], axis=1))
    qaug_scr[0:LANES, :] = qt
    for h in range(N_KV):
        pk = picked[:, h * Q_BLOCK:(h + 1) * Q_BLOCK]
        selneg = jnp.where(valid, jnp.where(pk > 0.0, 0.0, NEG), NEG).astype(BF16)
        for g in range(GQA):
            c0 = (h * GQA + g) * Q_BLOCK
            qaug_scr[LANES:2 * LANES, c0:c0 + Q_BLOCK] = selneg

    acc_scr[...] = jnp.zeros(acc_scr.shape, F32)
    tiles_per_step = SEL_TILE // LANES

    def score(step, dst):
        t0 = pl.multiple_of(step * SEL_TILE, SEL_TILE)
        dst[...] = _dot(ksa_ref[0, pl.ds(t0, SEL_TILE), :], qaug_scr[...])

    def attend(src, step, m_prev, l_prev, causal):
        vt = jnp.concatenate([vsb_ref[0, step * tiles_per_step + i] for i in range(tiles_per_step)], axis=1)
        sc = src[...]
        if causal:
            tok = step * SEL_TILE + _row_iota((SEL_TILE, Q_BLOCK))
            sc = sc + _tile_heads(jnp.where(tok <= q0 + _lane_iota((SEL_TILE, Q_BLOCK)), 0.0, NEG))
        m_next = jnp.maximum(m_prev, jnp.max(sc, axis=0, keepdims=True))
        p = jnp.exp2(sc - m_next)
        alpha = jnp.exp2(m_prev - m_next)
        l_next = alpha * l_prev + jnp.sum(p, axis=0, keepdims=True)
        acc_scr[...] = alpha * acc_scr[...] + _values_by_kv_head(vt, p.astype(BF16))
        return m_next, l_next

    n_full = q0 // SEL_TILE
    score(0, sa_scr)

    def pair(jj, carry):
        m_c, l_c = carry
        score(2 * jj + 1, sb_scr)
        m_c, l_c = attend(sa_scr, 2 * jj, m_c, l_c, False)
        score(2 * jj + 2, sa_scr)
        return attend(sb_scr, 2 * jj + 1, m_c, l_c, False)

    m_run, l_run = lax.fori_loop(
        0, n_full // 2, pair,
        (jnp.full((1, cols), -jnp.inf, F32), jnp.zeros((1, cols), F32)))

    def odd_tail(m_c, l_c):
        score(n_full, sb_scr)
        m_c, l_c = attend(sa_scr, n_full - 1, m_c, l_c, False)
        return attend(sb_scr, n_full, m_c, l_c, True)[1]

    def even_tail(m_c, l_c):
        return attend(sa_scr, n_full, m_c, l_c, True)[1]

    l_run = lax.cond(n_full % 2 == 1, odd_tail, even_tail, m_run, l_run)

    gt = gate_ref[0, 0]
    o_s = acc_scr[...] * (1.0 / l_run)
    heads = []
    for e in range(N_HEADS):
        c0 = e * Q_BLOCK
        heads.append(gt[3 * e:3 * e + 1, :] * o_c[:, c0:c0 + Q_BLOCK]
                     + gt[3 * e + 1:3 * e + 2, :] * o_s[:, c0:c0 + Q_BLOCK]
                     + gt[3 * e + 2:3 * e + 3, :] * o_w[:, c0:c0 + Q_BLOCK])
    for j in range(N_HEADS // 2):
        slab = jnp.concatenate([heads[2 * j], heads[2 * j + 1]], axis=0).T
        out_ref[0, :, j * LANES:(j + 1) * LANES] = (
            slab * szb_ref[0, :, j * LANES:(j + 1) * LANES]).astype(BF16)


def _attn_prompt(qt4, gate4, szb3, kcc, vcct, ovlt, ksa3, vsb4, kwb3, vwb4):
    b, nq = qt4.shape[0], qt4.shape[1]
    t = nq * Q_BLOCK
    cols = N_HEADS * Q_BLOCK
    qtile = lambda a: pl.BlockSpec((1, 1) + a.shape[2:], lambda bi, i: (bi, i, 0, 0))
    seq = lambda a: pl.BlockSpec((1,) + a.shape[1:], lambda bi, i, nd=a.ndim: (bi,) + (0,) * (nd - 1))
    return pl.pallas_call(
        _attn_prompt_kernel,
        grid=(b, nq),
        in_specs=[qtile(qt4), qtile(gate4), pl.BlockSpec((1, Q_BLOCK, D_B), lambda bi, i: (bi, i, 0)),
                  seq(kcc), seq(vcct), pl.BlockSpec(ovlt.shape, lambda bi, i: (0, 0)),
                  seq(ksa3), seq(vsb4), seq(kwb3), seq(vwb4)],
        out_specs=pl.BlockSpec((1, Q_BLOCK, D_B), lambda bi, i: (bi, i, 0)),
        out_shape=jax.ShapeDtypeStruct((b, t, D_B), BF16),
        scratch_shapes=[pltpu.VMEM((2 * LANES, cols), BF16),
                        pltpu.VMEM((HEAD_DIM, cols), F32),
                        pltpu.VMEM((SEL_TILE, cols), F32),
                        pltpu.VMEM((SEL_TILE, cols), F32)],
        compiler_params=pltpu.CompilerParams(
            dimension_semantics=("arbitrary", "arbitrary"), vmem_limit_bytes=VMEM_LIMIT_BYTES),
        name="attn_prompt",
    )(qt4, gate4, szb3, kcc, vcct, ovlt, ksa3, vsb4, kwb3, vwb4)


def _outproj_kernel(x_ref, a_ref, m_ref, wa_ref, wb_ref, fg_ref, y_ref):
    y = x_ref[...] + _dot(a_ref[...], wa_ref[...]) + _dot(m_ref[...].astype(BF16), wb_ref[...])
    y_ref[...] = _rms_rows(y, fg_ref[...])


def _outproj(x2, a_out, mix_b, wa, wb, fg, tm):
    n = x2.shape[0]
    row = lambda i: (i, 0)
    const2 = lambda i: (0, 0)
    return pl.pallas_call(
        _outproj_kernel,
        grid=(n // tm,),
        in_specs=[pl.BlockSpec((tm, D_MODEL), row),
                  pl.BlockSpec((tm, a_out.shape[1]), row),
                  pl.BlockSpec((tm, mix_b.shape[1]), row),
                  pl.BlockSpec(wa.shape, const2), pl.BlockSpec(wb.shape, const2),
                  pl.BlockSpec((1, D_MODEL), const2)],
        out_specs=pl.BlockSpec((tm, D_MODEL), row),
        out_shape=jax.ShapeDtypeStruct((n, D_MODEL), F32),
        compiler_params=pltpu.CompilerParams(
            dimension_semantics=("arbitrary",), vmem_limit_bytes=VMEM_LIMIT_BYTES),
        name="outproj",
    )(x2, a_out, mix_b, wa, wb, fg)


def _page_copies(pt_ref, pool_ref, buf_ref, sem_ref, seq, slot):
    return [pltpu.make_async_copy(pool_ref.at[pt_ref[seq, p]], buf_ref.at[slot, p], sem_ref.at[slot])
            for p in range(buf_ref.shape[1])]


def _gather_pages(pt_ref, pools, bufs, sems):
    b = pl.program_id(0)
    slot = lax.rem(b, 2)

    def start(seq, sl):
        for pool, buf, sem in zip(pools, bufs, sems):
            for cp in _page_copies(pt_ref, pool, buf, sem, seq, sl):
                cp.start()

    @pl.when(b == 0)
    def _():
        start(0, 0)

    @pl.when(b + 1 < pl.num_programs(0))
    def _():
        start(b + 1, 1 - slot)

    for pool, buf, sem in zip(pools, bufs, sems):
        for cp in _page_copies(pt_ref, pool, buf, sem, b, slot):
            cp.wait()
    return slot


def _decode_cmp_kernel(pt_ref, ck_hbm, cv_hbm, q_ref, wk_ref, wv_ref, pek_ref, pev_ref, rope_ref, ovl_ref,
                       oc_ref, psum_ref, kbuf, vbuf, ksem, vsem, ktok_scr, vtok_scr):
    slot = _gather_pages(pt_ref, (ck_hbm, cv_hbm), (kbuf, vbuf), (ksem, vsem))

    def chunk_rows(buf, tok_scr):
        for p in range(buf.shape[1]):
            tok_scr[p * PAGE_SIZE:(p + 1) * PAGE_SIZE, :] = buf[slot, p].T
        n = tok_scr.shape[0] // CMP_STRIDE
        return jnp.concatenate(
            [tok_scr[pl.ds(r, n, stride=CMP_STRIDE), :] for r in range(CMP_STRIDE)], axis=1)

    kch = chunk_rows(kbuf, ktok_scr)
    vch = chunk_rows(vbuf, vtok_scr)
    kcc = _rope(_compress_chunks(kch, pek_ref, wk_ref), rope_ref[...]).astype(BF16)
    vcc = _compress_chunks(vch, pev_ref, wv_ref).astype(BF16)
    n_chunks = kch.shape[0]
    q = q_ref[0]
    rows = q.shape[0]
    s = _dot_nt(q, kcc)
    seen = _lane_iota((rows, n_chunks)) < n_chunks - 1
    p_c = jnp.where(seen, _softmax_rows(jnp.where(seen, s, NEG)), 0.0)
    oc_ref[0] = _dot(p_c.astype(BF16), vcc)[0:N_HEADS]
    ps0 = jnp.sum(p_c[0:GQA], axis=0, keepdims=True)
    ps1 = jnp.sum(p_c[GQA:2 * GQA], axis=0, keepdims=True)
    psum = jnp.where(_row_iota((rows, n_chunks)) < N_HEADS // 2, ps0, ps1)
    psum_ref[0] = _dot(psum.astype(BF16), ovl_ref[...])[0:N_HEADS]


def _decode_cmp(page_table, ck_pool, cv_pool, q16, wk, wv, pek, pev, rope_tab, ovl):
    nb, n_pages = page_table.shape
    hbm = pl.BlockSpec(memory_space=pl.ANY)
    page_buf = pltpu.VMEM((2, n_pages, KV_W, PAGE_SIZE), F32)
    tok_buf = pltpu.VMEM((n_pages * PAGE_SIZE, KV_W), F32)
    const2 = lambda b, pt: (0, 0)
    per_b = lambda a: pl.BlockSpec((1,) + a.shape[1:], lambda b, pt: (b, 0, 0))
    in_specs = [hbm, hbm, per_b(q16),
                pl.BlockSpec(wk.shape, const2), pl.BlockSpec(wv.shape, const2),
                pl.BlockSpec(pek.shape, const2), pl.BlockSpec(pev.shape, const2),
                pl.BlockSpec(rope_tab.shape, const2), pl.BlockSpec(ovl.shape, const2)]
    out_blk = pl.BlockSpec((1, N_HEADS, LANES), lambda b, pt: (b, 0, 0))
    grid_spec = pltpu.PrefetchScalarGridSpec(
        num_scalar_prefetch=1, grid=(nb,), in_specs=in_specs, out_specs=[out_blk, out_blk],
        scratch_shapes=[page_buf, page_buf, pltpu.SemaphoreType.DMA((2,)), pltpu.SemaphoreType.DMA((2,)),
                        tok_buf, tok_buf])
    return pl.pallas_call(
        _decode_cmp_kernel,
        grid_spec=grid_spec,
        out_shape=[jax.ShapeDtypeStruct((nb, N_HEADS, LANES), F32)] * 2,
        compiler_params=pltpu.CompilerParams(
            dimension_semantics=("arbitrary",), vmem_limit_bytes=VMEM_LIMIT_BYTES),
        name="decode_cmp",
    )(page_table, ck_pool, cv_pool, q16, wk, wv, pek, pev, rope_tab, ovl)


def _decode_topk_kernel(imp_ref, sel_ref, *, n_blk, cur):
    n = imp_ref.shape[0]
    tiles = [imp_ref[i * LANES:(i + 1) * LANES, :].T for i in range(n // LANES)]
    imp_t = jnp.concatenate(tiles, axis=1)
    kk = _row_iota(imp_t.shape)
    forced = (kk == 0) | (kk == cur) | (kk == cur - 1)
    exists = kk < n_blk
    score = jnp.where(exists, jnp.where(forced, imp_t + FORCE_SCORE, imp_t), -jnp.inf)
    picked = _select_blocks(score, kk)
    selneg = jnp.where(exists, jnp.where(picked > 0.0, 0.0, NEG), NEG)
    for i in range(n // LANES):
        sel_ref[i * LANES:(i + 1) * LANES, :] = selneg[:, i * LANES:(i + 1) * LANES].T


def _decode_topk(imp2, n_blk, cur):
    return pl.pallas_call(
        functools.partial(_decode_topk_kernel, n_blk=n_blk, cur=cur),
        out_shape=jax.ShapeDtypeStruct(imp2.shape, F32),
        compiler_params=pltpu.CompilerParams(vmem_limit_bytes=VMEM_LIMIT_BYTES),
        name="decode_topk",
    )(imp2)


def _decode_attend(s, q32, vals_t_bf, k_new, v_new, bias_new):
    kn = k_new.astype(BF16).astype(F32)
    s_new = jnp.sum(q32 * kn, axis=1, keepdims=True) + bias_new
    mx = jnp.maximum(jnp.max(s, axis=1, keepdims=True), s_new)
    ex = jnp.exp(s - mx)
    ex_new = jnp.exp(s_new - mx)
    inv = 1.0 / (jnp.sum(ex, axis=1, keepdims=True) + ex_new)
    p = (ex * inv).astype(BF16)
    p_new = (ex_new * inv).astype(BF16).astype(F32)
    return _dot_nt(p, vals_t_bf) + p_new * v_new.astype(BF16).astype(F32)


P_Q, P_SEL, P_OC, P_SZB, P_NEW, P_GATE, P_ROWS = 0, 16, 32, 40, 48, 56, 64


def _row_to_col(row):
    n = row.shape[1]
    diag = _row_iota((n, n)) == _lane_iota((n, n))
    return jnp.sum(jnp.where(diag, row, 0.0), axis=1, keepdims=True)


def _decode_sel_kernel(pt_ref, ks_hbm, vs_hbm, ckw_ref, cvw_ref, pack_ref, onehot_ref,
                       mix_ref, okw_ref, ovw_ref, kbuf, vbuf, ksem, vsem, *, cur):
    slot = _gather_pages(pt_ref, (ks_hbm, vs_hbm), (kbuf, vbuf), (ksem, vsem))
    n_pages = kbuf.shape[1]
    pk = pack_ref[0]
    q32 = pk[P_Q:P_Q + 2 * N_HEADS]
    q = q32.astype(BF16)
    sel = pk[P_SEL:P_SEL + 2 * N_HEADS].astype(BF16)
    rows = q.shape[0]
    ks_new, vs_new = pk[P_NEW:P_NEW + 1], pk[P_NEW + 1:P_NEW + 2]
    kw_new, vw_new = pk[P_NEW + 2:P_NEW + 3], pk[P_NEW + 3:P_NEW + 4]

    kt = jnp.concatenate([kbuf[slot, p] for p in range(n_pages)], axis=1).astype(BF16)
    vt = jnp.concatenate([vbuf[slot, p] for p in range(n_pages)], axis=1).astype(BF16)
    kaug = jnp.concatenate([kt, onehot_ref[...]], axis=0)
    qaug = jnp.concatenate([q, sel], axis=1)
    s_sel = _dot(qaug, kaug)
    o_s = _decode_attend(s_sel, q32, vt, ks_new, vs_new, sel.astype(F32)[:, cur:cur + 1])

    kwin = ckw_ref[0]
    vwin = cvw_ref[0]
    keep = kwin.shape[1]
    visible = _lane_iota((rows, keep)) > keep - WINDOW
    s_win = jnp.where(visible, _dot(q, kwin.astype(BF16)), NEG)
    o_w = _decode_attend(s_win, q32, vwin.astype(BF16), kw_new, vw_new, 0.0)

    g = pk[P_GATE:P_GATE + N_HEADS]
    o = (g[:, 0:1] * pk[P_OC:P_OC + N_HEADS] + g[:, 1:2] * o_s[0:N_HEADS] + g[:, 2:3] * o_w[0:N_HEADS])
    mix_ref[0] = o * pk[P_SZB:P_SZB + N_HEADS]

    last = _lane_iota((KV_W, keep)) == keep - 1
    okw_ref[0] = jnp.where(last, _row_to_col(kw_new), pltpu.roll(kwin, keep - 1, 1))
    ovw_ref[0] = jnp.where(last, _row_to_col(vw_new), pltpu.roll(vwin, keep - 1, 1))


def _decode_sel(page_table, ks_pool, vs_pool, ckw, cvw, pack, onehot, cur):
    nb, n_pages = page_table.shape
    keep = ckw.shape[2]
    hbm = pl.BlockSpec(memory_space=pl.ANY)
    page_buf = pltpu.VMEM((2, n_pages, KV_W, PAGE_SIZE), F32)
    per_b = lambda a: pl.BlockSpec((1,) + a.shape[1:], lambda b, pt: (b, 0, 0))
    in_specs = [hbm, hbm, per_b(ckw), per_b(cvw), per_b(pack),
                pl.BlockSpec(onehot.shape, lambda b, pt: (0, 0))]
    out_specs = [pl.BlockSpec((1, N_HEADS, LANES), lambda b, pt: (b, 0, 0)),
                 pl.BlockSpec((1, KV_W, keep), lambda b, pt: (b, 0, 0)),
                 pl.BlockSpec((1, KV_W, keep), lambda b, pt: (b, 0, 0))]
    grid_spec = pltpu.PrefetchScalarGridSpec(
        num_scalar_prefetch=1, grid=(nb,), in_specs=in_specs, out_specs=out_specs,
        scratch_shapes=[page_buf, page_buf, pltpu.SemaphoreType.DMA((2,)), pltpu.SemaphoreType.DMA((2,))])
    return pl.pallas_call(
        functools.partial(_decode_sel_kernel, cur=cur),
        grid_spec=grid_spec,
        out_shape=[jax.ShapeDtypeStruct((nb, N_HEADS, LANES), F32),
                   jax.ShapeDtypeStruct((nb, KV_W, keep), F32),
                   jax.ShapeDtypeStruct((nb, KV_W, keep), F32)],
        compiler_params=pltpu.CompilerParams(
            dimension_semantics=("arbitrary",), vmem_limit_bytes=VMEM_LIMIT_BYTES),
        name="decode_sel",
    )(page_table, ks_pool, vs_pool, ckw, cvw, pack, onehot)


def _rope_table(pos):
    half = ROT_DIM // 2
    inv = jnp.power(jnp.float32(ROPE_THETA), -jnp.arange(half, dtype=F32) / half)
    ang = pos.astype(F32)[:, None] * inv[None, :]
    cs = jnp.concatenate([jnp.cos(ang), jnp.sin(ang)], axis=1)
    lane = jnp.arange(3 * LANES)
    seg, l64 = lane // LANES, lane % HEAD_DIM
    src = jnp.where(seg == 0, l64 % half, half + l64 % half)
    sign = jnp.where(seg == 0, l64 < ROT_DIM,
                     jnp.where(seg == 1, l64 < half, (l64 >= half) & (l64 < ROT_DIM))).astype(F32)
    sign = jnp.where(seg == 1, -sign, sign)
    place = (jnp.arange(ROT_DIM)[:, None] == src[None, :]).astype(F32) * sign[None, :]
    ones = ((seg == 0) & (l64 >= ROT_DIM)).astype(F32)
    return jnp.dot(cs, place, precision=lax.Precision.HIGHEST) + ones[None, :]


def _compress_weights(w_c, pe_c):
    w4 = w_c.reshape(2, CMP_STRIDE, HEAD_DIM, HEAD_DIM)
    wb = jnp.einsum('arde,hg->rhdage', w4, jnp.eye(N_KV, dtype=w_c.dtype))
    wb = wb.reshape(CMP_STRIDE * KV_W, 2 * KV_W).astype(BF16)
    pe = jnp.broadcast_to(pe_c.reshape(2, CMP_STRIDE, 1, HEAD_DIM), (2, CMP_STRIDE, N_KV, HEAD_DIM))
    return wb, pe.reshape(2, CMP_STRIDE * KV_W)


def _overlap_matrix(n_rows, n_blk):
    cs = jnp.arange(n_rows, dtype=jnp.int32)[:, None] * CMP_STRIDE
    bs = jnp.arange(LANES, dtype=jnp.int32)[None, :] * SEL_BLOCK
    hit = (cs < bs + SEL_BLOCK) & (cs + CMP_BLOCK > bs) & (jnp.arange(LANES)[None, :] < n_blk)
    return hit.astype(BF16)


def kernel(x_prompt, x_sample, cache_k_cmp, cache_v_cmp, cache_k_sel, cache_v_sel, cache_k_win,
           cache_v_win, page_table, norm_g, w_in, ln_v_g, ln_v_b, w_s, b_s, w_ck, pe_ck, w_cv,
           pe_cv, w_out, final_g):
    depth = norm_g.shape[0]
    assert depth == 1
    bsz, seq, _ = x_prompt.shape
    nb, dec_seq, _ = x_sample.shape
    assert dec_seq == 1 and seq % SEL_TILE == 0 and seq >= WIN_SPAN and (seq // CMP_STRIDE) % LANES == 0
    n_pages = page_table.shape[1]
    past = n_pages * PAGE_SIZE
    keep = cache_k_win.shape[2]
    assert keep == WINDOW and past % SEL_BLOCK == 0

    w = w_in[0]
    w_pad = jnp.concatenate(
        [w[:, :C_G + N_GATE_COLS], jnp.zeros((D_MODEL, LANES - N_GATE_COLS), w.dtype),
         w[:, C_G + N_GATE_COLS:]], axis=1).astype(BF16)
    g_row = norm_g[0][None, :]
    lng = ln_v_g[0][None, :]
    lnb = ln_v_b[0][None, :]
    fg = final_g[None, :]
    wk_c, pek = _compress_weights(w_ck[0], pe_ck[0])
    wv_c, pev = _compress_weights(w_cv[0], pe_cv[0])
    w_o = w_out[0].astype(BF16)
    wo_a, wo_b = w_o[:D_A], w_o[D_A:]

    tril = jnp.tril(jnp.ones((CHUNK, CHUNK), w_s.dtype))
    ws_bf = (w_s[0] * tril).astype(BF16)
    bias_full = jnp.repeat(b_s[0].T, HEAD_DIM, axis=1)
    xp2 = x_prompt.reshape(bsz * seq, D_MODEL)
    (a_out, qt, kc, vc, kc_t, vc_t, ks_t, vs_t, kw_t, vw_t, ksa, kwb, vsb, vwb, gates_t, szb) = _inproj_prompt(
        xp2, g_row, w_pad, lng, lnb, ws_bf, bias_full,
        _rope_table(jnp.arange(seq, dtype=jnp.int32)), seq)
    n_chunks = seq // CMP_STRIDE
    kcc, vcc_t = _compress_prompt(
        kc.reshape(bsz, n_chunks, CMP_STRIDE * KV_W), vc.reshape(bsz, n_chunks, CMP_STRIDE * KV_W),
        wk_c, wv_c, pek, pev,
        _rope_table(jnp.arange(n_chunks, dtype=jnp.int32) * CMP_STRIDE))
    b3 = lambda a: a.reshape(bsz, seq, a.shape[-1])
    mix_p = _attn_prompt(qt, gates_t, b3(szb), kcc, vcc_t,
                         _overlap_matrix(n_chunks, seq // SEL_BLOCK).T,
                         b3(ksa), vsb, b3(kwb), vwb)
    y_prompt = _outproj(xp2, a_out, mix_p.reshape(bsz * seq, D_B), wo_a, wo_b, fg, 512)
    y_prompt = y_prompt.reshape(bsz, seq, D_MODEL)
    st = lambda a: jnp.transpose(a.reshape(a.shape[0], N_KV, HEAD_DIM, a.shape[2]), (0, 3, 1, 2))[None]
    keep_p = min(WINDOW, seq)

    xs2 = x_sample.reshape(nb, D_MODEL)
    wsv = jnp.repeat(w_s[0][:, 0, 0], HEAD_DIM)[None, :]
    bsv = jnp.repeat(b_s[0][:, 0], HEAD_DIM)[None, :]
    (a_s, qa_s, kc_s, vc_s, ks_s, vs_s, kw_s, vw_s, gates_s, szb_s, vn_s) = _inproj_decode(
        xs2, g_row, w_pad, lng, lnb, wsv, bsv,
        _rope_table(jnp.full((nb,), past, dtype=jnp.int32)))
    q16 = jnp.pad(qa_s.reshape(nb, N_HEADS, LANES), ((0, 0), (0, N_HEADS), (0, 0)))
    d_chunks = past // CMP_STRIDE
    n_blk = -(-(past + 1) // SEL_BLOCK)
    cur = past // SEL_BLOCK
    kv_t = lambda c: jnp.transpose(c[0], (0, 2, 3, 1)).reshape(c.shape[1], KV_W, c.shape[2])
    oc_s, imp_s = _decode_cmp(
        page_table, kv_t(cache_k_cmp), kv_t(cache_v_cmp),
        q16, wk_c, wv_c, pek, pev,
        _rope_table(jnp.arange(d_chunks, dtype=jnp.int32) * CMP_STRIDE),
        _overlap_matrix(d_chunks, n_blk))
    selneg = _decode_topk(imp_s[:, ::GQA, :].reshape(nb * N_KV, LANES), n_blk, cur)
    sel16 = jnp.pad(jnp.repeat(selneg.reshape(nb, N_KV, LANES), GQA, axis=1), ((0, 0), (0, N_HEADS), (0, 0)))
    tok_blk = jnp.arange(past, dtype=jnp.int32)[None, :] // SEL_BLOCK
    onehot = (tok_blk == jnp.arange(LANES, dtype=jnp.int32)[:, None]).astype(BF16)
    gate_rows = jnp.pad(gates_s[:, :N_GATE_COLS].reshape(nb, N_HEADS, 3), ((0, 0), (0, 0), (0, LANES - 3)))
    new_rows = jnp.pad(jnp.stack([ks_s, vs_s, kw_s, vw_s], axis=1), ((0, 0), (0, P_GATE - P_NEW - 4), (0, 0)))
    pack = jnp.concatenate(
        [q16.astype(F32), sel16, oc_s, szb_s.reshape(nb, N_HEADS, LANES), new_rows, gate_rows], axis=1)
    assert pack.shape[1] == P_ROWS
    mix_s, okw, ovw = _decode_sel(
        page_table, kv_t(cache_k_sel), kv_t(cache_v_sel), kv_t(cache_k_win), kv_t(cache_v_win),
        pack, onehot, cur)
    wo_slab = jnp.zeros((N_HEADS, N_KV, HEAD_DIM, D_MODEL), BF16)
    wo_heads = wo_b.reshape(N_HEADS, HEAD_DIM, D_MODEL)
    for e in range(N_HEADS):
        wo_slab = wo_slab.at[e, e // GQA].set(wo_heads[e])
    y_sample = _outproj(xs2, a_s, mix_s.reshape(nb, N_HEADS * LANES), wo_a,
                        wo_slab.reshape(N_HEADS * LANES, D_MODEL), fg, nb)
    y_sample = y_sample.reshape(nb, 1, D_MODEL)
    ss = lambda a: a.reshape(1, nb, 1, N_KV, HEAD_DIM)

    return (y_prompt, y_sample,
            st(kc_t), st(vc_t), st(ks_t), st(vs_t),
            st(kw_t[:, :, seq - keep_p:]), st(vw_t[:, :, seq - keep_p:]),
            ss(kc_s), ss(vc_s), ss(ks_s), ss(vs_s),
            st(okw), st(ovw),
            vn_s.reshape(1, nb, 1, D_A))
```

```python
import functools

import jax
import jax.numpy as jnp
from jax import lax
from jax.experimental import pallas as pl
from jax.experimental.pallas import tpu as pltpu

F32 = jnp.float32
BF16 = jnp.bfloat16

D_MODEL = 1024
HEAD_DIM = 64
D_A = 512
D_B = 512
A_GROUPS = 8
CHUNK = 128
N_HEADS = 8
N_KV = 2
GQA = 4
KV_W = 128
ROT_DIM = 16
ROPE_THETA = 500000.0
CMP_BLOCK = 32
CMP_STRIDE = 16
SEL_BLOCK = 64
SEL_SHIFT = 6
N_SELECT = 16
WINDOW = 512
Q_BLOCK = 128
PAGE_SIZE = 128
NORM_EPS = 1e-6
FORCE_SCORE = 1e4
NEG = -1e30
LOG2_E = 1.4426950408889634

LANES = 128
VMEM_LIMIT_BYTES = 56 * 1024 * 1024

C_U, C_V, C_ZA, C_Q = 0, 512, 1024, 1536
C_KC, C_VC, C_KS, C_VS, C_KW, C_VW = 2048, 2176, 2304, 2432, 2560, 2688
C_G, C_ZB, C_END = 2816, 2944, 3456
N_GATE_COLS = 3 * N_HEADS

SEL_TILE = 512
DECODE_GROUP = 2
WIN_SPAN = WINDOW + Q_BLOCK

_NT = (((1,), (1,)), ((), ()))


def _dot(a, b):
    return jnp.dot(a, b, preferred_element_type=F32)


def _dot_nt(a, b):
    return lax.dot_general(a, b, _NT, preferred_element_type=F32)


def _lane_iota(shape):
    return lax.broadcasted_iota(jnp.int32, shape, len(shape) - 1)


def _row_iota(shape):
    return lax.broadcasted_iota(jnp.int32, shape, len(shape) - 2)


def _rope(x, tab):
    c = tab[:, 0:LANES]
    s1 = tab[:, LANES:2 * LANES]
    s2 = tab[:, 2 * LANES:3 * LANES]
    return x * c + pltpu.roll(x, LANES - ROT_DIM // 2, 1) * s1 + pltpu.roll(x, ROT_DIM // 2, 1) * s2


def _rms_rows(x, g):
    ms = jnp.mean(x * x, axis=-1, keepdims=True)
    return x * lax.rsqrt(ms + NORM_EPS) * g


def _layer_norm_rows(v, g, b):
    mu = jnp.mean(v, axis=-1, keepdims=True)
    vc = v - mu
    var = jnp.mean(vc * vc, axis=-1, keepdims=True)
    return vc * lax.rsqrt(var + NORM_EPS) * g + b


def _head_slabs(x512):
    out = []
    rows = x512.shape[0]
    lane = _lane_iota((rows, LANES))
    lo = lane < HEAD_DIM
    for j in range(4):
        slab = x512[:, j * LANES:(j + 1) * LANES]
        swapped = pltpu.roll(slab, HEAD_DIM, 1)
        if j < 2:
            out.append(jnp.where(lo, slab, 0.0))
            out.append(jnp.where(lo, swapped, 0.0))
        else:
            out.append(jnp.where(lo, 0.0, swapped))
            out.append(jnp.where(lo, 0.0, slab))
    return out


def _project(x_ref, g_ref, w_refs, z_scr):
    hb = _rms_rows(x_ref[...], g_ref[...]).astype(BF16)
    wm_ref, wg_ref, wz_ref = w_refs
    z_scr[:, 0:C_G] = _dot(hb, wm_ref[...])
    z_scr[:, C_G:C_ZB] = _dot(hb, wg_ref[...])
    z_scr[:, C_ZB:C_END] = _dot(hb, wz_ref[...])


def _inproj_prompt_kernel(x_ref, g_ref, wm_ref, wg_ref, wz_ref, lng_ref, lnb_ref, ws_ref, bias_ref, rope_ref,
                          aout_ref, qt_ref, kc_ref, vc_ref,
                          kct_ref, vct_ref, kst_ref, vst_ref, kwt_ref, vwt_ref,
                          ksa_ref, kwb_ref, vsb_ref, vwb_ref, gate_ref, szb_ref,
                          z_scr, kstage_scr, vstage_scr, *, tm, tiles_per_batch):
    _project(x_ref, g_ref, (wm_ref, wg_ref, wz_ref), z_scr)
    tab = rope_ref[...]

    lane = _lane_iota((CHUNK, LANES))
    lo = lane < HEAD_DIM
    for c in range(tm // CHUNK):
        r0 = c * CHUNK
        u = z_scr[r0:r0 + CHUNK, C_U:C_U + D_A]
        v = z_scr[r0:r0 + CHUNK, C_V:C_V + D_A]
        za = z_scr[r0:r0 + CHUNK, C_ZA:C_ZA + D_A]
        vn = _layer_norm_rows(v, lng_ref[...], lnb_ref[...])
        parts = []
        for p in range(A_GROUPS // 2):
            vp = vn[:, p * LANES:(p + 1) * LANES]
            v_lo = jnp.where(lo, vp, 0.0).astype(BF16)
            v_hi = jnp.where(lo, 0.0, vp).astype(BF16)
            parts.append(_dot(ws_ref[2 * p], v_lo) + _dot(ws_ref[2 * p + 1], v_hi))
        s = jnp.concatenate(parts, axis=1) + bias_ref[...]
        aout_ref[r0:r0 + CHUNK, :] = (u * s * jax.nn.silu(za)).astype(BF16)

    q = z_scr[:, C_Q:C_Q + D_B]
    qr = jnp.concatenate(
        [_rope(q[:, j * LANES:(j + 1) * LANES], tab) for j in range(4)], axis=1) * (HEAD_DIM ** -0.5 * LOG2_E)
    slabs = _head_slabs(qr)
    kc = z_scr[:, C_KC:C_KC + KV_W]
    vc = z_scr[:, C_VC:C_VC + KV_W]
    for src, stage, dst in ((kc, kstage_scr, kc_ref), (vc, vstage_scr, vc_ref)):
        stage[...] = src
        for r in range(CMP_STRIDE):
            dst[:, r * KV_W:(r + 1) * KV_W] = stage[pl.ds(r, tm // CMP_STRIDE, stride=CMP_STRIDE), :]
    ks = _rope(z_scr[:, C_KS:C_KS + KV_W], tab)
    vs = z_scr[:, C_VS:C_VS + KV_W]
    kw = _rope(z_scr[:, C_KW:C_KW + KV_W], tab)
    vw = z_scr[:, C_VW:C_VW + KV_W]
    gates = jax.nn.sigmoid(z_scr[:, C_G:C_G + LANES])
    pos = (pl.program_id(0) % tiles_per_batch) * tm + _row_iota((tm, LANES))
    onehot = jnp.where((pos >> SEL_SHIFT) == _lane_iota((tm, LANES)), 1.0, 0.0)
    ksa_ref[:, 0:LANES] = ks.astype(BF16)
    ksa_ref[:, LANES:2 * LANES] = onehot.astype(BF16)
    kwb_ref[...] = kw.astype(BF16)
    for j in range(tm // LANES):
        r0, r1 = j * LANES, (j + 1) * LANES
        kct_ref[0, :, r0:r1] = kc[r0:r1].T
        vct_ref[0, :, r0:r1] = vc[r0:r1].T
        kst_ref[0, :, r0:r1] = ks[r0:r1].T
        kwt_ref[0, :, r0:r1] = kw[r0:r1].T
        vs_t, vw_t = vs[r0:r1].T, vw[r0:r1].T
        vst_ref[0, :, r0:r1] = vs_t
        vwt_ref[0, :, r0:r1] = vw_t
        vsb_ref[0, j] = vs_t.astype(BF16)
        vwb_ref[0, j] = vw_t.astype(BF16)
        for e in range(N_HEADS):
            qt_ref[0, j, :, e * Q_BLOCK:(e + 1) * Q_BLOCK] = slabs[e][r0:r1].T.astype(BF16)
        gate_ref[0, j] = gates[r0:r1].T
    szb_ref[...] = jax.nn.silu(z_scr[:, C_ZB:C_ZB + D_B])


def _inproj_prompt(x2, g, w, lng, lnb, ws_bf, bias_full, rope_tab, seq, tm=256):
    n = x2.shape[0]
    tiles_per_batch = seq // tm
    bsz = n // seq
    n_tok_tiles = seq // LANES
    row = lambda i: (i, 0)
    const2 = lambda i: (0, 0)
    rows2d = lambda width, dtype: (jax.ShapeDtypeStruct((n, width), dtype), pl.BlockSpec((tm, width), row))
    kv_t = (jax.ShapeDtypeStruct((bsz, KV_W, seq), F32),
            pl.BlockSpec((1, KV_W, tm), lambda i: (i // tiles_per_batch, 0, i % tiles_per_batch)))
    chunk_rows = (jax.ShapeDtypeStruct((n // CMP_STRIDE, CMP_STRIDE * KV_W), F32),
                  pl.BlockSpec((tm // CMP_STRIDE, CMP_STRIDE * KV_W), row))
    tiles = lambda r, c, dtype: (
        jax.ShapeDtypeStruct((bsz, n_tok_tiles, r, c), dtype),
        pl.BlockSpec((1, tm // LANES, r, c), lambda i: (i // tiles_per_batch, i % tiles_per_batch, 0, 0)))
    outs = [
        rows2d(D_A, BF16),
        tiles(KV_W, N_HEADS * Q_BLOCK, BF16),
        chunk_rows, chunk_rows,
        kv_t, kv_t, kv_t, kv_t, kv_t, kv_t,
        rows2d(2 * KV_W, BF16),
        rows2d(KV_W, BF16),
        tiles(KV_W, LANES, BF16),
        tiles(KV_W, LANES, BF16),
        tiles(LANES, LANES, F32),
        rows2d(D_B, F32),
    ]
    out_shapes = [o[0] for o in outs]
    out_specs = [o[1] for o in outs]
    in_specs = [
        pl.BlockSpec((tm, D_MODEL), row),
        pl.BlockSpec((1, D_MODEL), const2),
        pl.BlockSpec((D_MODEL, C_G), const2),
        pl.BlockSpec((D_MODEL, C_ZB - C_G), const2),
        pl.BlockSpec((D_MODEL, C_END - C_ZB), const2),
        pl.BlockSpec((1, D_A), const2),
        pl.BlockSpec((1, D_A), const2),
        pl.BlockSpec((A_GROUPS, CHUNK, CHUNK), lambda i: (0, 0, 0)),
        pl.BlockSpec((CHUNK, D_A), const2),
        pl.BlockSpec((tm, 3 * LANES), lambda i: (i % tiles_per_batch, 0)),
    ]
    return pl.pallas_call(
        functools.partial(_inproj_prompt_kernel, tm=tm, tiles_per_batch=tiles_per_batch),
        grid=(n // tm,),
        in_specs=in_specs,
        out_specs=out_specs,
        out_shape=out_shapes,
        scratch_shapes=[pltpu.VMEM((tm, C_END), F32), pltpu.VMEM((tm, KV_W), F32), pltpu.VMEM((tm, KV_W), F32)],
        compiler_params=pltpu.CompilerParams(
            dimension_semantics=("arbitrary",), vmem_limit_bytes=VMEM_LIMIT_BYTES),
        name="inproj_prompt",
    )(x2, g, *w, lng, lnb, ws_bf, bias_full, rope_tab)


def _inproj_decode_kernel(x_ref, g_ref, wm_ref, wg_ref, wz_ref, lng_ref, lnb_ref, wsv_ref, bsv_ref, rope_ref,
                          aout_ref, qa_ref, kc_ref, vc_ref, ks_ref, vs_ref, kw_ref, vw_ref,
                          gate_ref, szb_ref, vn_ref, z_scr):
    _project(x_ref, g_ref, (wm_ref, wg_ref, wz_ref), z_scr)
    tab = rope_ref[...]
    u = z_scr[:, C_U:C_U + D_A]
    v = z_scr[:, C_V:C_V + D_A]
    za = z_scr[:, C_ZA:C_ZA + D_A]
    vn = _layer_norm_rows(v, lng_ref[...], lnb_ref[...])
    vn_ref[...] = vn
    s = vn * wsv_ref[...] + bsv_ref[...]
    aout_ref[...] = (u * s * jax.nn.silu(za)).astype(BF16)

    q = z_scr[:, C_Q:C_Q + D_B]
    qr = jnp.concatenate(
        [_rope(q[:, j * LANES:(j + 1) * LANES], tab) for j in range(4)], axis=1) * (HEAD_DIM ** -0.5)
    for e, slab in enumerate(_head_slabs(qr)):
        qa_ref[:, e * LANES:(e + 1) * LANES] = slab.astype(BF16)
    kc_ref[...] = z_scr[:, C_KC:C_KC + KV_W]
    vc_ref[...] = z_scr[:, C_VC:C_VC + KV_W]
    ks_ref[...] = _rope(z_scr[:, C_KS:C_KS + KV_W], tab)
    vs_ref[...] = z_scr[:, C_VS:C_VS + KV_W]
    kw_ref[...] = _rope(z_scr[:, C_KW:C_KW + KV_W], tab)
    vw_ref[...] = z_scr[:, C_VW:C_VW + KV_W]
    gate_ref[...] = jax.nn.sigmoid(z_scr[:, C_G:C_G + LANES])
    szb = jax.nn.silu(z_scr[:, C_ZB:C_ZB + D_B])
    for e, slab in enumerate(_head_slabs(szb)):
        szb_ref[:, e * LANES:(e + 1) * LANES] = slab


def _inproj_decode(x2, g, w, lng, lnb, wsv, bsv, rope_tab):
    n = x2.shape[0]
    out_shapes = [
        jax.ShapeDtypeStruct((n, D_A), BF16),
        jax.ShapeDtypeStruct((n, N_HEADS * LANES), BF16),
        jax.ShapeDtypeStruct((n, KV_W), F32),
        jax.ShapeDtypeStruct((n, KV_W), F32),
        jax.ShapeDtypeStruct((n, KV_W), F32),
        jax.ShapeDtypeStruct((n, KV_W), F32),
        jax.ShapeDtypeStruct((n, KV_W), F32),
        jax.ShapeDtypeStruct((n, KV_W), F32),
        jax.ShapeDtypeStruct((n, LANES), F32),
        jax.ShapeDtypeStruct((n, N_HEADS * LANES), F32),
        jax.ShapeDtypeStruct((n, D_A), F32),
    ]
    return pl.pallas_call(
        _inproj_decode_kernel,
        out_shape=out_shapes,
        scratch_shapes=[pltpu.VMEM((n, C_END), F32)],
        compiler_params=pltpu.CompilerParams(vmem_limit_bytes=VMEM_LIMIT_BYTES),
        name="inproj_decode",
    )(x2, g, *w, lng, lnb, wsv, bsv, rope_tab)


def _compress_chunks(ch, pe_ref, w_ref):
    a = _dot((ch + pe_ref[0:1, :]).astype(BF16), w_ref[:, 0:KV_W])
    b = _dot((ch + pe_ref[1:2, :]).astype(BF16), w_ref[:, KV_W:2 * KV_W])
    return a + pltpu.roll(b, ch.shape[0] - 1, 0)


def _compress_prompt_kernel(kc_ref, vc_ref, wk_ref, wv_ref, pek_ref, pev_ref, rope_ref, kcc_ref, vcct_ref):
    kcc = _compress_chunks(kc_ref[0], pek_ref, wk_ref)
    kcc_ref[0] = _rope(kcc, rope_ref[...]).astype(BF16)
    vcc = _compress_chunks(vc_ref[0], pev_ref, wv_ref)
    for j in range(vcc.shape[0] // LANES):
        vcct_ref[0, :, j * LANES:(j + 1) * LANES] = vcc[j * LANES:(j + 1) * LANES].T.astype(BF16)


def _compress_prompt(kc3, vc3, wk, wv, pek, pev, rope_tab):
    b, c, width = kc3.shape
    blk = pl.BlockSpec((1, c, width), lambda i: (i, 0, 0))
    const2 = lambda i: (0, 0)
    return pl.pallas_call(
        _compress_prompt_kernel,
        grid=(b,),
        in_specs=[blk, blk,
                  pl.BlockSpec(wk.shape, const2), pl.BlockSpec(wv.shape, const2),
                  pl.BlockSpec(pek.shape, const2), pl.BlockSpec(pev.shape, const2),
                  pl.BlockSpec(rope_tab.shape, const2)],
        out_specs=[pl.BlockSpec((1, c, KV_W), lambda i: (i, 0, 0)),
                   pl.BlockSpec((1, KV_W, c), lambda i: (i, 0, 0))],
        out_shape=[jax.ShapeDtypeStruct((b, c, KV_W), BF16), jax.ShapeDtypeStruct((b, KV_W, c), BF16)],
        compiler_params=pltpu.CompilerParams(
            dimension_semantics=("arbitrary",), vmem_limit_bytes=VMEM_LIMIT_BYTES),
        name="compress_prompt",
    )(kc3, vc3, wk, wv, pek, pev, rope_tab)


def _select_blocks(score, kk):
    picked = jnp.zeros(score.shape, F32)
    for _ in range(N_SELECT):
        cm = jnp.max(score, axis=0, keepdims=True)
        first = jnp.min(jnp.where(score == cm, kk, LANES), axis=0, keepdims=True)
        hit = kk == first
        picked = jnp.where(hit, 1.0, picked)
        score = jnp.where(hit, -jnp.inf, score)
    return picked


def _softmax_rows(s):
    mx = jnp.max(s, axis=1, keepdims=True)
    ex = jnp.exp(s - mx)
    return ex * (1.0 / jnp.sum(ex, axis=1, keepdims=True))


def _tile_heads(x):
    return jnp.concatenate([x] * N_HEADS, axis=1)


def _values_by_kv_head(vt, p):
    half = GQA * Q_BLOCK
    return jnp.concatenate(
        [_dot(vt[h * HEAD_DIM:(h + 1) * HEAD_DIM], p[:, h * half:(h + 1) * half]) for h in range(N_KV)], axis=1)


def _attn_prompt_kernel(qt_ref, gate_ref, szb_ref, kcc_ref, vcct_ref, ovlt_ref,
                        ksa_ref, vsb_ref, kwb_ref, vwb_ref, out_ref,
                        qaug_scr, acc_scr, sa_scr, sb_scr):
    i_blk = pl.program_id(1)
    q0 = i_blk * Q_BLOCK
    cols = N_HEADS * Q_BLOCK
    qt = qt_ref[0, 0]

    n_cmp = kcc_ref.shape[1]

    def compressed(extent):
        t_c = q0 + _lane_iota((extent, Q_BLOCK))
        seen = (_row_iota((extent, Q_BLOCK)) * CMP_STRIDE + (CMP_BLOCK - 1)) <= t_c
        s = _dot(kcc_ref[0, 0:extent, :], qt) + _tile_heads(jnp.where(seen, 0.0, NEG))
        mx = jnp.max(s, axis=0, keepdims=True)
        ex = jnp.exp2(s - mx)
        inv = jnp.where(mx > 0.5 * NEG, 1.0 / jnp.sum(ex, axis=0, keepdims=True), 0.0)
        p_c = ex * inv
        imps = []
        for h in range(N_KV):
            c0 = h * GQA * Q_BLOCK
            p_sum = (p_c[:, c0:c0 + Q_BLOCK] + p_c[:, c0 + Q_BLOCK:c0 + 2 * Q_BLOCK]
                     + p_c[:, c0 + 2 * Q_BLOCK:c0 + 3 * Q_BLOCK] + p_c[:, c0 + 3 * Q_BLOCK:c0 + 4 * Q_BLOCK])
            imps.append(_dot(ovlt_ref[:, 0:extent], p_sum.astype(BF16)))
        return (_values_by_kv_head(vcct_ref[0, :, 0:extent], p_c.astype(BF16)), *imps)

    n_need = (q0 + Q_BLOCK) // CMP_STRIDE
    extents = list(range(LANES, n_cmp + 1, LANES))
    o_c, *imp_t = lax.switch(
        jnp.minimum((n_need - 1) // LANES, len(extents) - 1),
        [functools.partial(compressed, ext) for ext in extents])

    w0 = pl.multiple_of(jnp.maximum(q0 - WINDOW, 0), Q_BLOCK)
    wj0 = jnp.maximum(i_blk - WINDOW // Q_BLOCK, 0)
    kwin = kwb_ref[0, pl.ds(w0, WIN_SPAN), :]
    vwt = jnp.concatenate([vwb_ref[0, wj0 + i] for i in range(WIN_SPAN // LANES)], axis=1)
    kp = w0 + _row_iota((WIN_SPAN, Q_BLOCK))
    tw = q0 + _lane_iota((WIN_SPAN, Q_BLOCK))
    in_win = jnp.where(kp <= tw, jnp.where(kp > tw - WINDOW, 0.0, NEG), NEG)
    sw = _dot(kwin, qt) + _tile_heads(in_win)
    exw = jnp.exp2(sw - jnp.max(sw, axis=0, keepdims=True))
    o_w = _values_by_kv_head(vwt, exw.astype(BF16)) * (1.0 / jnp.sum(exw, axis=0, keepdims=True))

    kk = _row_iota((LANES, Q_BLOCK))
    tq = q0 + _lane_iota((LANES, Q_BLOCK))
    valid = kk * SEL_BLOCK <= tq
    cur = tq >> SEL_SHIFT
    forced = (kk == 0) | (kk == cur) | (kk == cur - 1)
    scores = []
    for h in range(N_KV):
        scores.append(jnp.where(valid, jnp.where(forced, imp_t[h] + FORCE_SCORE, imp_t[h]), -jnp.inf))
    picked = _select_blocks(jnp.concatenate(scores, axis=1), jnp.concatenate([kk, kk], axis=1))
    qaug_scr[0:LANES, :] = qt
    for h in range(N_KV):
        pk = picked[:, h * Q_BLOCK:(h + 1) * Q_BLOCK]
        selneg = jnp.where(valid, jnp.where(pk > 0.0, 0.0, NEG), NEG).astype(BF16)
        for g in range(GQA):
            c0 = (h * GQA + g) * Q_BLOCK
            qaug_scr[LANES:2 * LANES, c0:c0 + Q_BLOCK] = selneg

    acc_scr[...] = jnp.zeros(acc_scr.shape, F32)
    tiles_per_step = SEL_TILE // LANES

    def score(step, dst):
        t0 = pl.multiple_of(step * SEL_TILE, SEL_TILE)
        dst[...] = _dot(ksa_ref[0, pl.ds(t0, SEL_TILE), :], qaug_scr[...])

    def attend(src, step, m_prev, l_prev, causal):
        vt = jnp.concatenate([vsb_ref[0, step * tiles_per_step + i] for i in range(tiles_per_step)], axis=1)
        sc = src[...]
        if causal:
            tok = step * SEL_TILE + _row_iota((SEL_TILE, Q_BLOCK))
            sc = sc + _tile_heads(jnp.where(tok <= q0 + _lane_iota((SEL_TILE, Q_BLOCK)), 0.0, NEG))
        m_next = jnp.maximum(m_prev, jnp.max(sc, axis=0, keepdims=True))
        p = jnp.exp2(sc - m_next)
        alpha = jnp.exp2(m_prev - m_next)
        l_next = alpha * l_prev + jnp.sum(p, axis=0, keepdims=True)
        acc_scr[...] = alpha * acc_scr[...] + _values_by_kv_head(vt, p.astype(BF16))
        return m_next, l_next

    n_full = q0 // SEL_TILE
    score(0, sa_scr)

    def pair(jj, carry):
        m_c, l_c = carry
        score(2 * jj + 1, sb_scr)
        m_c, l_c = attend(sa_scr, 2 * jj, m_c, l_c, False)
        score(2 * jj + 2, sa_scr)
        return attend(sb_scr, 2 * jj + 1, m_c, l_c, False)

    m_run, l_run = lax.fori_loop(
        0, n_full // 2, pair,
        (jnp.full((1, cols), -jnp.inf, F32), jnp.zeros((1, cols), F32)))

    def odd_tail(m_c, l_c):
        score(n_full, sb_scr)
        m_c, l_c = attend(sa_scr, n_full - 1, m_c, l_c, False)
        return attend(sb_scr, n_full, m_c, l_c, True)[1]

    def even_tail(m_c, l_c):
        return attend(sa_scr, n_full, m_c, l_c, True)[1]

    l_run = lax.cond(n_full % 2 == 1, odd_tail, even_tail, m_run, l_run)

    gt = gate_ref[0, 0]
    o_s = acc_scr[...] * (1.0 / l_run)
    heads = []
    for e in range(N_HEADS):
        c0 = e * Q_BLOCK
        heads.append(gt[3 * e:3 * e + 1, :] * o_c[:, c0:c0 + Q_BLOCK]
                     + gt[3 * e + 1:3 * e + 2, :] * o_s[:, c0:c0 + Q_BLOCK]
                     + gt[3 * e + 2:3 * e + 3, :] * o_w[:, c0:c0 + Q_BLOCK])
    for j in range(N_HEADS // 2):
        slab = jnp.concatenate([heads[2 * j], heads[2 * j + 1]], axis=0).T
        out_ref[0, :, j * LANES:(j + 1) * LANES] = (
            slab * szb_ref[0, :, j * LANES:(j + 1) * LANES]).astype(BF16)


def _attn_prompt(qt4, gate4, szb3, kcc, vcct, ovlt, ksa3, vsb4, kwb3, vwb4):
    b, nq = qt4.shape[0], qt4.shape[1]
    t = nq * Q_BLOCK
    cols = N_HEADS * Q_BLOCK
    qtile = lambda a: pl.BlockSpec((1, 1) + a.shape[2:], lambda bi, i: (bi, i, 0, 0))
    seq = lambda a: pl.BlockSpec((1,) + a.shape[1:], lambda bi, i, nd=a.ndim: (bi,) + (0,) * (nd - 1))
    return pl.pallas_call(
        _attn_prompt_kernel,
        grid=(b, nq),
        in_specs=[qtile(qt4), qtile(gate4), pl.BlockSpec((1, Q_BLOCK, D_B), lambda bi, i: (bi, i, 0)),
                  seq(kcc), seq(vcct), pl.BlockSpec(ovlt.shape, lambda bi, i: (0, 0)),
                  seq(ksa3), seq(vsb4), seq(kwb3), seq(vwb4)],
        out_specs=pl.BlockSpec((1, Q_BLOCK, D_B), lambda bi, i: (bi, i, 0)),
        out_shape=jax.ShapeDtypeStruct((b, t, D_B), BF16),
        scratch_shapes=[pltpu.VMEM((2 * LANES, cols), BF16),
                        pltpu.VMEM((HEAD_DIM, cols), F32),
                        pltpu.VMEM((SEL_TILE, cols), F32),
                        pltpu.VMEM((SEL_TILE, cols), F32)],
        compiler_params=pltpu.CompilerParams(
            dimension_semantics=("arbitrary", "arbitrary"), vmem_limit_bytes=VMEM_LIMIT_BYTES),
        name="attn_prompt",
    )(qt4, gate4, szb3, kcc, vcct, ovlt, ksa3, vsb4, kwb3, vwb4)


def _outproj_kernel(x_ref, a_ref, m_ref, wa_ref, wb_ref, fg_ref, y_ref):
    y = x_ref[...] + _dot(a_ref[...], wa_ref[...]) + _dot(m_ref[...].astype(BF16), wb_ref[...])
    y_ref[...] = _rms_rows(y, fg_ref[...])


def _outproj(x2, a_out, mix_b, wa, wb, fg, tm):
    n = x2.shape[0]
    row = lambda i: (i, 0)
    const2 = lambda i: (0, 0)
    return pl.pallas_call(
        _outproj_kernel,
        grid=(n // tm,),
        in_specs=[pl.BlockSpec((tm, D_MODEL), row),
                  pl.BlockSpec((tm, a_out.shape[1]), row),
                  pl.BlockSpec((tm, mix_b.shape[1]), row),
                  pl.BlockSpec(wa.shape, const2), pl.BlockSpec(wb.shape, const2),
                  pl.BlockSpec((1, D_MODEL), const2)],
        out_specs=pl.BlockSpec((tm, D_MODEL), row),
        out_shape=jax.ShapeDtypeStruct((n, D_MODEL), F32),
        compiler_params=pltpu.CompilerParams(
            dimension_semantics=("arbitrary",), vmem_limit_bytes=VMEM_LIMIT_BYTES),
        name="outproj",
    )(x2, a_out, mix_b, wa, wb, fg)


def _page_copies(pt_ref, pool_ref, buf_ref, sem_ref, step, slot):
    group, n_pages = buf_ref.shape[1], buf_ref.shape[2]
    return [pltpu.make_async_copy(pool_ref.at[pt_ref[step * group + g, p]], buf_ref.at[slot, g, p],
                                  sem_ref.at[slot])
            for g in range(group) for p in range(n_pages)]


def _gather_pages(pt_ref, pools, bufs, sems):
    b = pl.program_id(0)
    slot = lax.rem(b, 2)

    def start(step, sl):
        for pool, buf, sem in zip(pools, bufs, sems):
            for cp in _page_copies(pt_ref, pool, buf, sem, step, sl):
                cp.start()

    @pl.when(b == 0)
    def _():
        start(0, 0)

    @pl.when(b + 1 < pl.num_programs(0))
    def _():
        start(b + 1, 1 - slot)

    for pool, buf, sem in zip(pools, bufs, sems):
        for cp in _page_copies(pt_ref, pool, buf, sem, b, slot):
            cp.wait()
    return slot


def _decode_cmp_kernel(pt_ref, ck_hbm, cv_hbm, q_ref, wk_ref, wv_ref, pek_ref, pev_ref, rope_ref, ovl_ref,
                       oc_ref, psum_ref, kbuf, vbuf, ksem, vsem, ktok_scr, vtok_scr):
    slot = _gather_pages(pt_ref, (ck_hbm, cv_hbm), (kbuf, vbuf), (ksem, vsem))
    group, n_pages = kbuf.shape[1], kbuf.shape[2]

    def chunk_rows(buf, tok_scr, g):
        for p in range(n_pages):
            tok_scr[g, p * PAGE_SIZE:(p + 1) * PAGE_SIZE, :] = buf[slot, g, p].T
        n = tok_scr.shape[1] // CMP_STRIDE
        return jnp.concatenate(
            [tok_scr[g, pl.ds(r, n, stride=CMP_STRIDE), :] for r in range(CMP_STRIDE)], axis=1)

    for g in range(group):
        kch = chunk_rows(kbuf, ktok_scr, g)
        vch = chunk_rows(vbuf, vtok_scr, g)
        kcc = _rope(_compress_chunks(kch, pek_ref, wk_ref), rope_ref[...]).astype(BF16)
        vcc = _compress_chunks(vch, pev_ref, wv_ref).astype(BF16)
        n_chunks = kch.shape[0]
        q = q_ref[g]
        rows = q.shape[0]
        s = _dot_nt(q, kcc)
        seen = _lane_iota((rows, n_chunks)) < n_chunks - 1
        p_c = jnp.where(seen, _softmax_rows(jnp.where(seen, s, NEG)), 0.0)
        oc_ref[g] = _dot(p_c.astype(BF16), vcc)[0:N_HEADS]
        ps0 = jnp.sum(p_c[0:GQA], axis=0, keepdims=True)
        ps1 = jnp.sum(p_c[GQA:2 * GQA], axis=0, keepdims=True)
        psum = jnp.where(_row_iota((rows, n_chunks)) < N_HEADS // 2, ps0, ps1)
        psum_ref[g] = _dot(psum.astype(BF16), ovl_ref[...])[0:N_HEADS]


def _decode_cmp(page_table, ck_pool, cv_pool, q16, wk, wv, pek, pev, rope_tab, ovl):
    nb, n_pages = page_table.shape
    group = DECODE_GROUP
    assert nb % group == 0
    hbm = pl.BlockSpec(memory_space=pl.ANY)
    page_buf = pltpu.VMEM((2, group, n_pages, KV_W, PAGE_SIZE), F32)
    tok_buf = pltpu.VMEM((group, n_pages * PAGE_SIZE, KV_W), F32)
    const2 = lambda b, pt: (0, 0)
    per_b = lambda a: pl.BlockSpec((group,) + a.shape[1:], lambda b, pt: (b, 0, 0))
    in_specs = [hbm, hbm, per_b(q16),
                pl.BlockSpec(wk.shape, const2), pl.BlockSpec(wv.shape, const2),
                pl.BlockSpec(pek.shape, const2), pl.BlockSpec(pev.shape, const2),
                pl.BlockSpec(rope_tab.shape, const2), pl.BlockSpec(ovl.shape, const2)]
    out_blk = pl.BlockSpec((group, N_HEADS, LANES), lambda b, pt: (b, 0, 0))
    grid_spec = pltpu.PrefetchScalarGridSpec(
        num_scalar_prefetch=1, grid=(nb // group,), in_specs=in_specs, out_specs=[out_blk, out_blk],
        scratch_shapes=[page_buf, page_buf, pltpu.SemaphoreType.DMA((2,)), pltpu.SemaphoreType.DMA((2,)),
                        tok_buf, tok_buf])
    return pl.pallas_call(
        _decode_cmp_kernel,
        grid_spec=grid_spec,
        out_shape=[jax.ShapeDtypeStruct((nb, N_HEADS, LANES), F32)] * 2,
        compiler_params=pltpu.CompilerParams(
            dimension_semantics=("arbitrary",), vmem_limit_bytes=VMEM_LIMIT_BYTES),
        name="decode_cmp",
    )(page_table, ck_pool, cv_pool, q16, wk, wv, pek, pev, rope_tab, ovl)


def _decode_topk_kernel(imp_ref, sel_ref, *, n_blk, cur):
    n = imp_ref.shape[0]
    tiles = [imp_ref[i * LANES:(i + 1) * LANES, :].T for i in range(n // LANES)]
    imp_t = jnp.concatenate(tiles, axis=1)
    kk = _row_iota(imp_t.shape)
    forced = (kk == 0) | (kk == cur) | (kk == cur - 1)
    exists = kk < n_blk
    score = jnp.where(exists, jnp.where(forced, imp_t + FORCE_SCORE, imp_t), -jnp.inf)
    picked = _select_blocks(score, kk)
    selneg = jnp.where(exists, jnp.where(picked > 0.0, 0.0, NEG), NEG)
    for i in range(n // LANES):
        sel_ref[i * LANES:(i + 1) * LANES, :] = selneg[:, i * LANES:(i + 1) * LANES].T


def _decode_topk(imp2, n_blk, cur):
    return pl.pallas_call(
        functools.partial(_decode_topk_kernel, n_blk=n_blk, cur=cur),
        out_shape=jax.ShapeDtypeStruct(imp2.shape, F32),
        compiler_params=pltpu.CompilerParams(vmem_limit_bytes=VMEM_LIMIT_BYTES),
        name="decode_topk",
    )(imp2)


def _decode_attend(s, q32, vals_t_bf, k_new, v_new, bias_new):
    kn = k_new.astype(BF16).astype(F32)
    s_new = jnp.sum(q32 * kn, axis=1, keepdims=True) + bias_new
    mx = jnp.maximum(jnp.max(s, axis=1, keepdims=True), s_new)
    ex = jnp.exp(s - mx)
    ex_new = jnp.exp(s_new - mx)
    inv = 1.0 / (jnp.sum(ex, axis=1, keepdims=True) + ex_new)
    p = (ex * inv).astype(BF16)
    p_new = (ex_new * inv).astype(BF16).astype(F32)
    return _dot_nt(p, vals_t_bf) + p_new * v_new.astype(BF16).astype(F32)


P_Q, P_SEL, P_OC, P_SZB, P_NEW, P_GATE, P_ROWS = 0, 16, 32, 40, 48, 56, 64


def _row_to_col(row):
    n = row.shape[1]
    diag = _row_iota((n, n)) == _lane_iota((n, n))
    return jnp.sum(jnp.where(diag, row, 0.0), axis=1, keepdims=True)


def _decode_sel_kernel(pt_ref, ks_hbm, vs_hbm, ckw_ref, cvw_ref, pack_ref, onehot_ref,
                       mix_ref, okw_ref, ovw_ref, kbuf, vbuf, ksem, vsem, *, cur):
    slot = _gather_pages(pt_ref, (ks_hbm, vs_hbm), (kbuf, vbuf), (ksem, vsem))
    group, n_pages = kbuf.shape[1], kbuf.shape[2]
    for g in range(group):
        pk = pack_ref[g]
        q32 = pk[P_Q:P_Q + 2 * N_HEADS]
        q = q32.astype(BF16)
        sel = pk[P_SEL:P_SEL + 2 * N_HEADS].astype(BF16)
        rows = q.shape[0]
        ks_new, vs_new = pk[P_NEW:P_NEW + 1], pk[P_NEW + 1:P_NEW + 2]
        kw_new, vw_new = pk[P_NEW + 2:P_NEW + 3], pk[P_NEW + 3:P_NEW + 4]

        kt = jnp.concatenate([kbuf[slot, g, p] for p in range(n_pages)], axis=1).astype(BF16)
        vt = jnp.concatenate([vbuf[slot, g, p] for p in range(n_pages)], axis=1).astype(BF16)
        kaug = jnp.concatenate([kt, onehot_ref[...]], axis=0)
        qaug = jnp.concatenate([q, sel], axis=1)
        s_sel = _dot(qaug, kaug)
        o_s = _decode_attend(s_sel, q32, vt, ks_new, vs_new, sel.astype(F32)[:, cur:cur + 1])

        kwin = ckw_ref[g]
        vwin = cvw_ref[g]
        keep = kwin.shape[1]
        visible = _lane_iota((rows, keep)) > keep - WINDOW
        s_win = jnp.where(visible, _dot(q, kwin.astype(BF16)), NEG)
        o_w = _decode_attend(s_win, q32, vwin.astype(BF16), kw_new, vw_new, 0.0)

        gates = pk[P_GATE:P_GATE + N_HEADS]
        o = (gates[:, 0:1] * pk[P_OC:P_OC + N_HEADS] + gates[:, 1:2] * o_s[0:N_HEADS]
             + gates[:, 2:3] * o_w[0:N_HEADS])
        mix_ref[g] = o * pk[P_SZB:P_SZB + N_HEADS]

        last = _lane_iota((KV_W, keep)) == keep - 1
        okw_ref[g] = jnp.where(last, _row_to_col(kw_new), pltpu.roll(kwin, keep - 1, 1))
        ovw_ref[g] = jnp.where(last, _row_to_col(vw_new), pltpu.roll(vwin, keep - 1, 1))


def _decode_sel(page_table, ks_pool, vs_pool, ckw, cvw, pack, onehot, cur):
    nb, n_pages = page_table.shape
    group = DECODE_GROUP
    assert nb % group == 0
    keep = ckw.shape[2]
    hbm = pl.BlockSpec(memory_space=pl.ANY)
    page_buf = pltpu.VMEM((2, group, n_pages, KV_W, PAGE_SIZE), F32)
    per_b = lambda a: pl.BlockSpec((group,) + a.shape[1:], lambda b, pt: (b, 0, 0))
    in_specs = [hbm, hbm, per_b(ckw), per_b(cvw), per_b(pack),
                pl.BlockSpec(onehot.shape, lambda b, pt: (0, 0))]
    out_specs = [pl.BlockSpec((group, N_HEADS, LANES), lambda b, pt: (b, 0, 0)),
                 pl.BlockSpec((group, KV_W, keep), lambda b, pt: (b, 0, 0)),
                 pl.BlockSpec((group, KV_W, keep), lambda b, pt: (b, 0, 0))]
    grid_spec = pltpu.PrefetchScalarGridSpec(
        num_scalar_prefetch=1, grid=(nb // group,), in_specs=in_specs, out_specs=out_specs,
        scratch_shapes=[page_buf, page_buf, pltpu.SemaphoreType.DMA((2,)), pltpu.SemaphoreType.DMA((2,))])
    return pl.pallas_call(
        functools.partial(_decode_sel_kernel, cur=cur),
        grid_spec=grid_spec,
        out_shape=[jax.ShapeDtypeStruct((nb, N_HEADS, LANES), F32),
                   jax.ShapeDtypeStruct((nb, KV_W, keep), F32),
                   jax.ShapeDtypeStruct((nb, KV_W, keep), F32)],
        compiler_params=pltpu.CompilerParams(
            dimension_semantics=("arbitrary",), vmem_limit_bytes=VMEM_LIMIT_BYTES),
        name="decode_sel",
    )(page_table, ks_pool, vs_pool, ckw, cvw, pack, onehot)


def _rope_table(pos):
    half = ROT_DIM // 2
    inv = jnp.power(jnp.float32(ROPE_THETA), -jnp.arange(half, dtype=F32) / half)
    ang = pos.astype(F32)[:, None] * inv[None, :]
    cs = jnp.concatenate([jnp.cos(ang), jnp.sin(ang)], axis=1)
    lane = jnp.arange(3 * LANES)
    seg, l64 = lane // LANES, lane % HEAD_DIM
    src = jnp.where(seg == 0, l64 % half, half + l64 % half)
    sign = jnp.where(seg == 0, l64 < ROT_DIM,
                     jnp.where(seg == 1, l64 < half, (l64 >= half) & (l64 < ROT_DIM))).astype(F32)
    sign = jnp.where(seg == 1, -sign, sign)
    place = (jnp.arange(ROT_DIM)[:, None] == src[None, :]).astype(F32) * sign[None, :]
    ones = ((seg == 0) & (l64 >= ROT_DIM)).astype(F32)
    return jnp.dot(cs, place, precision=lax.Precision.HIGHEST) + ones[None, :]


def _compress_weights(w_c, pe_c):
    w4 = w_c.reshape(2, CMP_STRIDE, HEAD_DIM, HEAD_DIM)
    wb = jnp.einsum('arde,hg->rhdage', w4, jnp.eye(N_KV, dtype=w_c.dtype))
    wb = wb.reshape(CMP_STRIDE * KV_W, 2 * KV_W).astype(BF16)
    pe = jnp.broadcast_to(pe_c.reshape(2, CMP_STRIDE, 1, HEAD_DIM), (2, CMP_STRIDE, N_KV, HEAD_DIM))
    return wb, pe.reshape(2, CMP_STRIDE * KV_W)


def _overlap_matrix(n_rows, n_blk):
    cs = jnp.arange(n_rows, dtype=jnp.int32)[:, None] * CMP_STRIDE
    bs = jnp.arange(LANES, dtype=jnp.int32)[None, :] * SEL_BLOCK
    hit = (cs < bs + SEL_BLOCK) & (cs + CMP_BLOCK > bs) & (jnp.arange(LANES)[None, :] < n_blk)
    return hit.astype(BF16)


def kernel(x_prompt, x_sample, cache_k_cmp, cache_v_cmp, cache_k_sel, cache_v_sel, cache_k_win,
           cache_v_win, page_table, norm_g, w_in, ln_v_g, ln_v_b, w_s, b_s, w_ck, pe_ck, w_cv,
           pe_cv, w_out, final_g):
    depth = norm_g.shape[0]
    assert depth == 1
    bsz, seq, _ = x_prompt.shape
    nb, dec_seq, _ = x_sample.shape
    assert dec_seq == 1 and seq % SEL_TILE == 0 and seq >= WIN_SPAN and (seq // CMP_STRIDE) % LANES == 0
    n_pages = page_table.shape[1]
    past = n_pages * PAGE_SIZE
    keep = cache_k_win.shape[2]
    assert keep == WINDOW and past % SEL_BLOCK == 0

    w = w_in[0]
    w_pad = (w[:, :C_G].astype(BF16),
             jnp.pad(w[:, C_G:C_G + N_GATE_COLS], ((0, 0), (0, LANES - N_GATE_COLS))).astype(BF16),
             w[:, C_G + N_GATE_COLS:].astype(BF16))
    g_row = norm_g[0][None, :]
    lng = ln_v_g[0][None, :]
    lnb = ln_v_b[0][None, :]
    fg = final_g[None, :]
    wk_c, pek = _compress_weights(w_ck[0], pe_ck[0])
    wv_c, pev = _compress_weights(w_cv[0], pe_cv[0])
    w_o = w_out[0].astype(BF16)
    wo_a, wo_b = w_o[:D_A], w_o[D_A:]

    tril = jnp.tril(jnp.ones((CHUNK, CHUNK), w_s.dtype))
    ws_bf = (w_s[0] * tril).astype(BF16)
    bias_full = jnp.repeat(b_s[0].T, HEAD_DIM, axis=1)
    xp2 = x_prompt.reshape(bsz * seq, D_MODEL)
    (a_out, qt, kc, vc, kc_t, vc_t, ks_t, vs_t, kw_t, vw_t, ksa, kwb, vsb, vwb, gates_t, szb) = _inproj_prompt(
        xp2, g_row, w_pad, lng, lnb, ws_bf, bias_full,
        _rope_table(jnp.arange(seq, dtype=jnp.int32)), seq)
    n_chunks = seq // CMP_STRIDE
    kcc, vcc_t = _compress_prompt(
        kc.reshape(bsz, n_chunks, CMP_STRIDE * KV_W), vc.reshape(bsz, n_chunks, CMP_STRIDE * KV_W),
        wk_c, wv_c, pek, pev,
        _rope_table(jnp.arange(n_chunks, dtype=jnp.int32) * CMP_STRIDE))
    b3 = lambda a: a.reshape(bsz, seq, a.shape[-1])
    mix_p = _attn_prompt(qt, gates_t, b3(szb), kcc, vcc_t,
                         _overlap_matrix(n_chunks, seq // SEL_BLOCK).T,
                         b3(ksa), vsb, b3(kwb), vwb)
    y_prompt = _outproj(xp2, a_out, mix_p.reshape(bsz * seq, D_B), wo_a, wo_b, fg, 512)
    y_prompt = y_prompt.reshape(bsz, seq, D_MODEL)
    st = lambda a: jnp.transpose(a.reshape(a.shape[0], N_KV, HEAD_DIM, a.shape[2]), (0, 3, 1, 2))[None]
    keep_p = min(WINDOW, seq)

    xs2 = x_sample.reshape(nb, D_MODEL)
    wsv = jnp.repeat(w_s[0][:, 0, 0], HEAD_DIM)[None, :]
    bsv = jnp.repeat(b_s[0][:, 0], HEAD_DIM)[None, :]
    (a_s, qa_s, kc_s, vc_s, ks_s, vs_s, kw_s, vw_s, gates_s, szb_s, vn_s) = _inproj_decode(
        xs2, g_row, w_pad, lng, lnb, wsv, bsv,
        _rope_table(jnp.full((nb,), past, dtype=jnp.int32)))
    q16 = jnp.pad(qa_s.reshape(nb, N_HEADS, LANES), ((0, 0), (0, N_HEADS), (0, 0)))
    d_chunks = past // CMP_STRIDE
    n_blk = -(-(past + 1) // SEL_BLOCK)
    cur = past // SEL_BLOCK
    kv_t = lambda c: jnp.transpose(c[0], (0, 2, 3, 1)).reshape(c.shape[1], KV_W, c.shape[2])
    oc_s, imp_s = _decode_cmp(
        page_table, kv_t(cache_k_cmp), kv_t(cache_v_cmp),
        q16, wk_c, wv_c, pek, pev,
        _rope_table(jnp.arange(d_chunks, dtype=jnp.int32) * CMP_STRIDE),
        _overlap_matrix(d_chunks, n_blk))
    selneg = _decode_topk(imp_s[:, ::GQA, :].reshape(nb * N_KV, LANES), n_blk, cur)
    sel16 = jnp.pad(jnp.repeat(selneg.reshape(nb, N_KV, LANES), GQA, axis=1), ((0, 0), (0, N_HEADS), (0, 0)))
    tok_blk = jnp.arange(past, dtype=jnp.int32)[None, :] // SEL_BLOCK
    onehot = (tok_blk == jnp.arange(LANES, dtype=jnp.int32)[:, None]).astype(BF16)
    gate_rows = jnp.pad(gates_s[:, :N_GATE_COLS].reshape(nb, N_HEADS, 3), ((0, 0), (0, 0), (0, LANES - 3)))
    new_rows = jnp.pad(jnp.stack([ks_s, vs_s, kw_s, vw_s], axis=1), ((0, 0), (0, P_GATE - P_NEW - 4), (0, 0)))
    pack = jnp.concatenate(
        [q16.astype(F32), sel16, oc_s, szb_s.reshape(nb, N_HEADS, LANES), new_rows, gate_rows], axis=1)
    assert pack.shape[1] == P_ROWS
    mix_s, okw, ovw = _decode_sel(
        page_table, kv_t(cache_k_sel), kv_t(cache_v_sel), kv_t(cache_k_win), kv_t(cache_v_win),
        pack, onehot, cur)
    wo_slab = jnp.zeros((N_HEADS, N_KV, HEAD_DIM, D_MODEL), BF16)
    wo_heads = wo_b.reshape(N_HEADS, HEAD_DIM, D_MODEL)
    for e in range(N_HEADS):
        wo_slab = wo_slab.at[e, e // GQA].set(wo_heads[e])
    y_sample = _outproj(xs2, a_s, mix_s.reshape(nb, N_HEADS * LANES), wo_a,
                        wo_slab.reshape(N_HEADS * LANES, D_MODEL), fg, nb)
    y_sample = y_sample.reshape(nb, 1, D_MODEL)
    ss = lambda a: a.reshape(1, nb, 1, N_KV, HEAD_DIM)

    return (y_prompt, y_sample,
            st(kc_t), st(vc_t), st(ks_t), st(vs_t),
            st(kw_t[:, :, seq - keep_p:]), st(vw_t[:, :, seq - keep_p:]),
            ss(kc_s), ss(vc_s), ss(ks_s), ss(vs_s),
            st(okw), st(ovw),
            vn_s.reshape(1, nb, 1, D_A))
```

```python
import functools

import jax
import jax.numpy as jnp
from jax import lax
from jax.experimental import pallas as pl
from jax.experimental.pallas import tpu as pltpu

F32 = jnp.float32
BF16 = jnp.bfloat16

D_MODEL = 1024
HEAD_DIM = 64
D_A = 512
D_B = 512
A_GROUPS = 8
CHUNK = 128
N_HEADS = 8
N_KV = 2
GQA = 4
KV_W = 128
ROT_DIM = 16
ROPE_THETA = 500000.0
CMP_BLOCK = 32
CMP_STRIDE = 16
SEL_BLOCK = 64
SEL_SHIFT = 6
N_SELECT = 16
WINDOW = 512
Q_BLOCK = 128
PAGE_SIZE = 128
NORM_EPS = 1e-6
FORCE_SCORE = 1e4
N_FORCED = 3
NEG = -1e30
LOG2_E = 1.4426950408889634

LANES = 128
VMEM_LIMIT_BYTES = 56 * 1024 * 1024

C_U, C_V, C_ZA, C_Q = 0, 512, 1024, 1536
C_KC, C_VC, C_KS, C_VS, C_KW, C_VW = 2048, 2176, 2304, 2432, 2560, 2688
C_G, C_ZB, C_END = 2816, 2944, 3456
N_GATE_COLS = 3 * N_HEADS

SEL_TILE = 512
DECODE_GROUP = 4
WIN_SPAN = WINDOW + Q_BLOCK

_NT = (((1,), (1,)), ((), ()))


def _dot(a, b):
    return jnp.dot(a, b, preferred_element_type=F32)


def _dot_nt(a, b):
    return lax.dot_general(a, b, _NT, preferred_element_type=F32)


def _lane_iota(shape):
    return lax.broadcasted_iota(jnp.int32, shape, len(shape) - 1)


def _row_iota(shape):
    return lax.broadcasted_iota(jnp.int32, shape, len(shape) - 2)


def _rope(x, tab):
    c = tab[:, 0:LANES]
    s1 = tab[:, LANES:2 * LANES]
    s2 = tab[:, 2 * LANES:3 * LANES]
    return x * c + pltpu.roll(x, LANES - ROT_DIM // 2, 1) * s1 + pltpu.roll(x, ROT_DIM // 2, 1) * s2


def _rms_rows(x, g):
    ms = jnp.mean(x * x, axis=-1, keepdims=True)
    return x * lax.rsqrt(ms + NORM_EPS) * g


def _layer_norm_rows(v, g, b):
    mu = jnp.mean(v, axis=-1, keepdims=True)
    vc = v - mu
    var = jnp.mean(vc * vc, axis=-1, keepdims=True)
    return vc * lax.rsqrt(var + NORM_EPS) * g + b


def _head_slabs(x512):
    out = []
    rows = x512.shape[0]
    lane = _lane_iota((rows, LANES))
    lo = lane < HEAD_DIM
    for j in range(4):
        slab = x512[:, j * LANES:(j + 1) * LANES]
        swapped = pltpu.roll(slab, HEAD_DIM, 1)
        if j < 2:
            out.append(jnp.where(lo, slab, 0.0))
            out.append(jnp.where(lo, swapped, 0.0))
        else:
            out.append(jnp.where(lo, 0.0, swapped))
            out.append(jnp.where(lo, 0.0, slab))
    return out


def _project(x_ref, g_ref, w_refs, z_scr):
    hb = _rms_rows(x_ref[...], g_ref[...]).astype(BF16)
    wm_ref, wg_ref, wz_ref = w_refs
    z_scr[:, 0:C_G] = _dot(hb, wm_ref[...])
    z_scr[:, C_G:C_ZB] = _dot(hb, wg_ref[...])
    z_scr[:, C_ZB:C_END] = _dot(hb, wz_ref[...])


def _inproj_prompt_kernel(x_ref, g_ref, wm_ref, wg_ref, wz_ref, lng_ref, lnb_ref, ws_ref, bias_ref, rope_ref,
                          aout_ref, qt_ref, kc_ref, vc_ref,
                          kct_ref, vct_ref, kst_ref, vst_ref, kwt_ref, vwt_ref,
                          ksa_ref, kwb_ref, vsb_ref, vwb_ref, gate_ref, szb_ref,
                          z_scr, kstage_scr, vstage_scr, *, tm, tiles_per_batch):
    _project(x_ref, g_ref, (wm_ref, wg_ref, wz_ref), z_scr)
    tab = rope_ref[...]

    lane = _lane_iota((CHUNK, LANES))
    lo = lane < HEAD_DIM
    for c in range(tm // CHUNK):
        r0 = c * CHUNK
        u = z_scr[r0:r0 + CHUNK, C_U:C_U + D_A]
        v = z_scr[r0:r0 + CHUNK, C_V:C_V + D_A]
        za = z_scr[r0:r0 + CHUNK, C_ZA:C_ZA + D_A]
        vn = _layer_norm_rows(v, lng_ref[...], lnb_ref[...])
        parts = []
        for p in range(A_GROUPS // 2):
            vp = vn[:, p * LANES:(p + 1) * LANES]
            v_lo = jnp.where(lo, vp, 0.0).astype(BF16)
            v_hi = jnp.where(lo, 0.0, vp).astype(BF16)
            parts.append(_dot(ws_ref[2 * p], v_lo) + _dot(ws_ref[2 * p + 1], v_hi))
        s = jnp.concatenate(parts, axis=1) + bias_ref[...]
        aout_ref[r0:r0 + CHUNK, :] = (u * s * jax.nn.silu(za)).astype(BF16)

    q = z_scr[:, C_Q:C_Q + D_B]
    qr = jnp.concatenate(
        [_rope(q[:, j * LANES:(j + 1) * LANES], tab) for j in range(4)], axis=1) * (HEAD_DIM ** -0.5 * LOG2_E)
    slabs = _head_slabs(qr)
    kc = z_scr[:, C_KC:C_KC + KV_W]
    vc = z_scr[:, C_VC:C_VC + KV_W]
    for src, stage, dst in ((kc, kstage_scr, kc_ref), (vc, vstage_scr, vc_ref)):
        stage[...] = src
        for r in range(CMP_STRIDE):
            dst[:, r * KV_W:(r + 1) * KV_W] = stage[pl.ds(r, tm // CMP_STRIDE, stride=CMP_STRIDE), :]
    ks = _rope(z_scr[:, C_KS:C_KS + KV_W], tab)
    vs = z_scr[:, C_VS:C_VS + KV_W]
    kw = _rope(z_scr[:, C_KW:C_KW + KV_W], tab)
    vw = z_scr[:, C_VW:C_VW + KV_W]
    gates = jax.nn.sigmoid(z_scr[:, C_G:C_G + LANES])
    pos = (pl.program_id(0) % tiles_per_batch) * tm + _row_iota((tm, LANES))
    onehot = jnp.where((pos >> SEL_SHIFT) == _lane_iota((tm, LANES)), 1.0, 0.0)
    ksa_ref[:, 0:LANES] = ks.astype(BF16)
    ksa_ref[:, LANES:2 * LANES] = onehot.astype(BF16)
    kwb_ref[...] = kw.astype(BF16)
    for j in range(tm // LANES):
        r0, r1 = j * LANES, (j + 1) * LANES
        kct_ref[0, :, r0:r1] = kc[r0:r1].T
        vct_ref[0, :, r0:r1] = vc[r0:r1].T
        kst_ref[0, :, r0:r1] = ks[r0:r1].T
        kwt_ref[0, :, r0:r1] = kw[r0:r1].T
        vs_t, vw_t = vs[r0:r1].T, vw[r0:r1].T
        vst_ref[0, :, r0:r1] = vs_t
        vwt_ref[0, :, r0:r1] = vw_t
        vsb_ref[0, j] = vs_t.astype(BF16)
        vwb_ref[0, j] = vw_t.astype(BF16)
        for e in range(N_HEADS):
            qt_ref[0, j, :, e * Q_BLOCK:(e + 1) * Q_BLOCK] = slabs[e][r0:r1].T.astype(BF16)
        gate_ref[0, j] = gates[r0:r1].T
    szb_ref[...] = jax.nn.silu(z_scr[:, C_ZB:C_ZB + D_B])


def _inproj_prompt(x2, g, w, lng, lnb, ws_bf, bias_full, rope_tab, seq, tm=512):
    n = x2.shape[0]
    tiles_per_batch = seq // tm
    bsz = n // seq
    n_tok_tiles = seq // LANES
    row = lambda i: (i, 0)
    const2 = lambda i: (0, 0)
    rows2d = lambda width, dtype: (jax.ShapeDtypeStruct((n, width), dtype), pl.BlockSpec((tm, width), row))
    kv_t = (jax.ShapeDtypeStruct((bsz, KV_W, seq), F32),
            pl.BlockSpec((1, KV_W, tm), lambda i: (i // tiles_per_batch, 0, i % tiles_per_batch)))
    chunk_rows = (jax.ShapeDtypeStruct((n // CMP_STRIDE, CMP_STRIDE * KV_W), F32),
                  pl.BlockSpec((tm // CMP_STRIDE, CMP_STRIDE * KV_W), row))
    tiles = lambda r, c, dtype: (
        jax.ShapeDtypeStruct((bsz, n_tok_tiles, r, c), dtype),
        pl.BlockSpec((1, tm // LANES, r, c), lambda i: (i // tiles_per_batch, i % tiles_per_batch, 0, 0)))
    outs = [
        rows2d(D_A, BF16),
        tiles(KV_W, N_HEADS * Q_BLOCK, BF16),
        chunk_rows, chunk_rows,
        kv_t, kv_t, kv_t, kv_t, kv_t, kv_t,
        rows2d(2 * KV_W, BF16),
        rows2d(KV_W, BF16),
        tiles(KV_W, LANES, BF16),
        tiles(KV_W, LANES, BF16),
        tiles(LANES, LANES, F32),
        rows2d(D_B, F32),
    ]
    out_shapes = [o[0] for o in outs]
    out_specs = [o[1] for o in outs]
    in_specs = [
        pl.BlockSpec((tm, D_MODEL), row),
        pl.BlockSpec((1, D_MODEL), const2),
        pl.BlockSpec((D_MODEL, C_G), const2),
        pl.BlockSpec((D_MODEL, C_ZB - C_G), const2),
        pl.BlockSpec((D_MODEL, C_END - C_ZB), const2),
        pl.BlockSpec((1, D_A), const2),
        pl.BlockSpec((1, D_A), const2),
        pl.BlockSpec((A_GROUPS, CHUNK, CHUNK), lambda i: (0, 0, 0)),
        pl.BlockSpec((CHUNK, D_A), const2),
        pl.BlockSpec((tm, 3 * LANES), lambda i: (i % tiles_per_batch, 0)),
    ]
    return pl.pallas_call(
        functools.partial(_inproj_prompt_kernel, tm=tm, tiles_per_batch=tiles_per_batch),
        grid=(n // tm,),
        in_specs=in_specs,
        out_specs=out_specs,
        out_shape=out_shapes,
        scratch_shapes=[pltpu.VMEM((tm, C_END), F32), pltpu.VMEM((tm, KV_W), F32), pltpu.VMEM((tm, KV_W), F32)],
        compiler_params=pltpu.CompilerParams(
            dimension_semantics=("arbitrary",), vmem_limit_bytes=VMEM_LIMIT_BYTES),
        name="inproj_prompt",
    )(x2, g, *w, lng, lnb, ws_bf, bias_full, rope_tab)


def _inproj_decode_kernel(x_ref, g_ref, wm_ref, wg_ref, wz_ref, lng_ref, lnb_ref, wsv_ref, bsv_ref, rope_ref,
                          aout_ref, qa_ref, kc_ref, vc_ref, ks_ref, vs_ref, kw_ref, vw_ref,
                          gate_ref, szb_ref, vn_ref, z_scr):
    _project(x_ref, g_ref, (wm_ref, wg_ref, wz_ref), z_scr)
    tab = rope_ref[...]
    u = z_scr[:, C_U:C_U + D_A]
    v = z_scr[:, C_V:C_V + D_A]
    za = z_scr[:, C_ZA:C_ZA + D_A]
    vn = _layer_norm_rows(v, lng_ref[...], lnb_ref[...])
    vn_ref[...] = vn
    s = vn * wsv_ref[...] + bsv_ref[...]
    aout_ref[...] = (u * s * jax.nn.silu(za)).astype(BF16)

    q = z_scr[:, C_Q:C_Q + D_B]
    qr = jnp.concatenate(
        [_rope(q[:, j * LANES:(j + 1) * LANES], tab) for j in range(4)], axis=1) * (HEAD_DIM ** -0.5)
    for e, slab in enumerate(_head_slabs(qr)):
        qa_ref[:, e * LANES:(e + 1) * LANES] = slab.astype(BF16)
    kc_ref[...] = z_scr[:, C_KC:C_KC + KV_W]
    vc_ref[...] = z_scr[:, C_VC:C_VC + KV_W]
    ks_ref[...] = _rope(z_scr[:, C_KS:C_KS + KV_W], tab)
    vs_ref[...] = z_scr[:, C_VS:C_VS + KV_W]
    kw_ref[...] = _rope(z_scr[:, C_KW:C_KW + KV_W], tab)
    vw_ref[...] = z_scr[:, C_VW:C_VW + KV_W]
    gate_ref[...] = jax.nn.sigmoid(z_scr[:, C_G:C_G + LANES])
    szb = jax.nn.silu(z_scr[:, C_ZB:C_ZB + D_B])
    for e, slab in enumerate(_head_slabs(szb)):
        szb_ref[:, e * LANES:(e + 1) * LANES] = slab


def _inproj_decode(x2, g, w, lng, lnb, wsv, bsv, rope_tab):
    n = x2.shape[0]
    out_shapes = [
        jax.ShapeDtypeStruct((n, D_A), BF16),
        jax.ShapeDtypeStruct((n, N_HEADS * LANES), BF16),
        jax.ShapeDtypeStruct((n, KV_W), F32),
        jax.ShapeDtypeStruct((n, KV_W), F32),
        jax.ShapeDtypeStruct((n, KV_W), F32),
        jax.ShapeDtypeStruct((n, KV_W), F32),
        jax.ShapeDtypeStruct((n, KV_W), F32),
        jax.ShapeDtypeStruct((n, KV_W), F32),
        jax.ShapeDtypeStruct((n, LANES), F32),
        jax.ShapeDtypeStruct((n, N_HEADS * LANES), F32),
        jax.ShapeDtypeStruct((n, D_A), F32),
    ]
    return pl.pallas_call(
        _inproj_decode_kernel,
        out_shape=out_shapes,
        scratch_shapes=[pltpu.VMEM((n, C_END), F32)],
        compiler_params=pltpu.CompilerParams(vmem_limit_bytes=VMEM_LIMIT_BYTES),
        name="inproj_decode",
    )(x2, g, *w, lng, lnb, wsv, bsv, rope_tab)


def _compress_chunks(ch, pe_ref, w_ref):
    a = _dot((ch + pe_ref[0:1, :]).astype(BF16), w_ref[:, 0:KV_W])
    b = _dot((ch + pe_ref[1:2, :]).astype(BF16), w_ref[:, KV_W:2 * KV_W])
    return a + pltpu.roll(b, ch.shape[0] - 1, 0)


def _compress_prompt_kernel(kc_ref, vc_ref, wk_ref, wv_ref, pek_ref, pev_ref, rope_ref, kcc_ref, vcct_ref):
    kcc = _compress_chunks(kc_ref[0], pek_ref, wk_ref)
    kcc_ref[0] = _rope(kcc, rope_ref[...]).astype(BF16)
    vcc = _compress_chunks(vc_ref[0], pev_ref, wv_ref)
    for j in range(vcc.shape[0] // LANES):
        vcct_ref[0, :, j * LANES:(j + 1) * LANES] = vcc[j * LANES:(j + 1) * LANES].T.astype(BF16)


def _compress_prompt(kc3, vc3, wk, wv, pek, pev, rope_tab):
    b, c, width = kc3.shape
    blk = pl.BlockSpec((1, c, width), lambda i: (i, 0, 0))
    const2 = lambda i: (0, 0)
    return pl.pallas_call(
        _compress_prompt_kernel,
        grid=(b,),
        in_specs=[blk, blk,
                  pl.BlockSpec(wk.shape, const2), pl.BlockSpec(wv.shape, const2),
                  pl.BlockSpec(pek.shape, const2), pl.BlockSpec(pev.shape, const2),
                  pl.BlockSpec(rope_tab.shape, const2)],
        out_specs=[pl.BlockSpec((1, c, KV_W), lambda i: (i, 0, 0)),
                   pl.BlockSpec((1, KV_W, c), lambda i: (i, 0, 0))],
        out_shape=[jax.ShapeDtypeStruct((b, c, KV_W), BF16), jax.ShapeDtypeStruct((b, KV_W, c), BF16)],
        compiler_params=pltpu.CompilerParams(
            dimension_semantics=("arbitrary",), vmem_limit_bytes=VMEM_LIMIT_BYTES),
        name="compress_prompt",
    )(kc3, vc3, wk, wv, pek, pev, rope_tab)


def _select_blocks(imp, kk, candidate, forced):
    assert FORCE_SCORE > 2 * GQA
    start = jnp.where(candidate, jnp.where(forced, -jnp.inf, imp), -jnp.inf)
    score = start
    for _ in range(N_SELECT - N_FORCED):
        cm = jnp.max(score, axis=0, keepdims=True)
        first = jnp.min(jnp.where(score == cm, kk, LANES), axis=0, keepdims=True)
        score = jnp.where(kk == first, -jnp.inf, score)
    taken = (score == -jnp.inf) & (start > -jnp.inf)
    return candidate & (forced | taken)


def _softmax_rows(s):
    mx = jnp.max(s, axis=1, keepdims=True)
    ex = jnp.exp(s - mx)
    return ex * (1.0 / jnp.sum(ex, axis=1, keepdims=True))


def _tile_heads(x):
    return jnp.concatenate([x] * N_HEADS, axis=1)


def _values_by_kv_head(vt, p):
    half = GQA * Q_BLOCK
    return jnp.concatenate(
        [_dot(vt[h * HEAD_DIM:(h + 1) * HEAD_DIM], p[:, h * half:(h + 1) * half]) for h in range(N_KV)], axis=1)


def _attn_prompt_kernel(qt_ref, gate_ref, szb_ref, kcc_ref, vcct_ref, ovlt_ref,
                        ksa_ref, vsb_ref, kwb_ref, vwb_ref, out_ref,
                        qaug_scr, acc_scr, sa_scr, sb_scr):
    i_blk = pl.program_id(1)
    q0 = i_blk * Q_BLOCK
    cols = N_HEADS * Q_BLOCK
    qt = qt_ref[0, 0]

    n_cmp = kcc_ref.shape[1]

    def compressed(extent):
        t_c = q0 + _lane_iota((extent, Q_BLOCK))
        seen = (_row_iota((extent, Q_BLOCK)) * CMP_STRIDE + (CMP_BLOCK - 1)) <= t_c
        s = _dot(kcc_ref[0, 0:extent, :], qt) + _tile_heads(jnp.where(seen, 0.0, NEG))
        mx = jnp.max(s, axis=0, keepdims=True)
        ex = jnp.exp2(s - mx)
        inv = jnp.where(mx > 0.5 * NEG, 1.0 / jnp.sum(ex, axis=0, keepdims=True), 0.0)
        p_c = ex * inv
        imps = []
        for h in range(N_KV):
            c0 = h * GQA * Q_BLOCK
            p_sum = (p_c[:, c0:c0 + Q_BLOCK] + p_c[:, c0 + Q_BLOCK:c0 + 2 * Q_BLOCK]
                     + p_c[:, c0 + 2 * Q_BLOCK:c0 + 3 * Q_BLOCK] + p_c[:, c0 + 3 * Q_BLOCK:c0 + 4 * Q_BLOCK])
            imps.append(_dot(ovlt_ref[:, 0:extent], p_sum.astype(BF16)))
        return (_values_by_kv_head(vcct_ref[0, :, 0:extent], p_c.astype(BF16)), *imps)

    n_need = (q0 + Q_BLOCK) // CMP_STRIDE
    extents = list(range(LANES, n_cmp + 1, LANES))
    o_c, *imp_t = lax.switch(
        jnp.minimum((n_need - 1) // LANES, len(extents) - 1),
        [functools.partial(compressed, ext) for ext in extents])

    w0 = pl.multiple_of(jnp.maximum(q0 - WINDOW, 0), Q_BLOCK)
    wj0 = jnp.maximum(i_blk - WINDOW // Q_BLOCK, 0)
    kwin = kwb_ref[0, pl.ds(w0, WIN_SPAN), :]
    vwt = jnp.concatenate([vwb_ref[0, wj0 + i] for i in range(WIN_SPAN // LANES)], axis=1)
    kp = w0 + _row_iota((WIN_SPAN, Q_BLOCK))
    tw = q0 + _lane_iota((WIN_SPAN, Q_BLOCK))
    in_win = jnp.where(kp <= tw, jnp.where(kp > tw - WINDOW, 0.0, NEG), NEG)
    sw = _dot(kwin, qt) + _tile_heads(in_win)
    exw = jnp.exp2(sw - jnp.max(sw, axis=0, keepdims=True))
    o_w = _values_by_kv_head(vwt, exw.astype(BF16)) * (1.0 / jnp.sum(exw, axis=0, keepdims=True))

    kk = _row_iota((LANES, Q_BLOCK))
    tq = q0 + _lane_iota((LANES, Q_BLOCK))
    valid = kk * SEL_BLOCK <= tq
    cur = tq >> SEL_SHIFT
    forced = (kk == 0) | (kk == cur) | (kk == cur - 1)
    both = lambda a: jnp.concatenate([a] * N_KV, axis=1)
    picked = _select_blocks(jnp.concatenate(imp_t, axis=1), both(kk), both(valid), both(forced))
    qaug_scr[0:LANES, :] = qt
    for h in range(N_KV):
        selneg = jnp.where(picked[:, h * Q_BLOCK:(h + 1) * Q_BLOCK], 0.0, NEG).astype(BF16)
        for g in range(GQA):
            c0 = (h * GQA + g) * Q_BLOCK
            qaug_scr[LANES:2 * LANES, c0:c0 + Q_BLOCK] = selneg

    acc_scr[...] = jnp.zeros(acc_scr.shape, F32)
    tiles_per_step = SEL_TILE // LANES

    def score(step, dst):
        t0 = pl.multiple_of(step * SEL_TILE, SEL_TILE)
        dst[...] = _dot(ksa_ref[0, pl.ds(t0, SEL_TILE), :], qaug_scr[...])

    def attend(src, step, m_prev, l_prev, causal):
        vt = jnp.concatenate([vsb_ref[0, step * tiles_per_step + i] for i in range(tiles_per_step)], axis=1)
        sc = src[...]
        if causal:
            tok = step * SEL_TILE + _row_iota((SEL_TILE, Q_BLOCK))
            sc = sc + _tile_heads(jnp.where(tok <= q0 + _lane_iota((SEL_TILE, Q_BLOCK)), 0.0, NEG))
        m_next = jnp.maximum(m_prev, jnp.max(sc, axis=0, keepdims=True))
        p = jnp.exp2(sc - m_next)
        alpha = jnp.exp2(m_prev - m_next)
        l_next = alpha * l_prev + jnp.sum(p, axis=0, keepdims=True)
        acc_scr[...] = alpha * acc_scr[...] + _values_by_kv_head(vt, p.astype(BF16))
        return m_next, l_next

    n_full = q0 // SEL_TILE
    score(0, sa_scr)

    def pair(jj, carry):
        m_c, l_c = carry
        score(2 * jj + 1, sb_scr)
        m_c, l_c = attend(sa_scr, 2 * jj, m_c, l_c, False)
        score(2 * jj + 2, sa_scr)
        return attend(sb_scr, 2 * jj + 1, m_c, l_c, False)

    m_run, l_run = lax.fori_loop(
        0, n_full // 2, pair,
        (jnp.full((1, cols), -jnp.inf, F32), jnp.zeros((1, cols), F32)))

    def odd_tail(m_c, l_c):
        score(n_full, sb_scr)
        m_c, l_c = attend(sa_scr, n_full - 1, m_c, l_c, False)
        return attend(sb_scr, n_full, m_c, l_c, True)[1]

    def even_tail(m_c, l_c):
        return attend(sa_scr, n_full, m_c, l_c, True)[1]

    l_run = lax.cond(n_full % 2 == 1, odd_tail, even_tail, m_run, l_run)

    gt = gate_ref[0, 0]
    o_s = acc_scr[...] * (1.0 / l_run)
    heads = []
    for e in range(N_HEADS):
        c0 = e * Q_BLOCK
        heads.append(gt[3 * e:3 * e + 1, :] * o_c[:, c0:c0 + Q_BLOCK]
                     + gt[3 * e + 1:3 * e + 2, :] * o_s[:, c0:c0 + Q_BLOCK]
                     + gt[3 * e + 2:3 * e + 3, :] * o_w[:, c0:c0 + Q_BLOCK])
    for j in range(N_HEADS // 2):
        slab = jnp.concatenate([heads[2 * j], heads[2 * j + 1]], axis=0).T
        out_ref[0, :, j * LANES:(j + 1) * LANES] = (
            slab * szb_ref[0, :, j * LANES:(j + 1) * LANES]).astype(BF16)


def _attn_prompt(qt4, gate4, szb3, kcc, vcct, ovlt, ksa3, vsb4, kwb3, vwb4):
    b, nq = qt4.shape[0], qt4.shape[1]
    t = nq * Q_BLOCK
    cols = N_HEADS * Q_BLOCK
    qtile = lambda a: pl.BlockSpec((1, 1) + a.shape[2:], lambda bi, i: (bi, i, 0, 0))
    seq = lambda a: pl.BlockSpec((1,) + a.shape[1:], lambda bi, i, nd=a.ndim: (bi,) + (0,) * (nd - 1))
    return pl.pallas_call(
        _attn_prompt_kernel,
        grid=(b, nq),
        in_specs=[qtile(qt4), qtile(gate4), pl.BlockSpec((1, Q_BLOCK, D_B), lambda bi, i: (bi, i, 0)),
                  seq(kcc), seq(vcct), pl.BlockSpec(ovlt.shape, lambda bi, i: (0, 0)),
                  seq(ksa3), seq(vsb4), seq(kwb3), seq(vwb4)],
        out_specs=pl.BlockSpec((1, Q_BLOCK, D_B), lambda bi, i: (bi, i, 0)),
        out_shape=jax.ShapeDtypeStruct((b, t, D_B), BF16),
        scratch_shapes=[pltpu.VMEM((2 * LANES, cols), BF16),
                        pltpu.VMEM((HEAD_DIM, cols), F32),
                        pltpu.VMEM((SEL_TILE, cols), F32),
                        pltpu.VMEM((SEL_TILE, cols), F32)],
        compiler_params=pltpu.CompilerParams(
            dimension_semantics=("arbitrary", "arbitrary"), vmem_limit_bytes=VMEM_LIMIT_BYTES),
        name="attn_prompt",
    )(qt4, gate4, szb3, kcc, vcct, ovlt, ksa3, vsb4, kwb3, vwb4)


def _outproj_kernel(x_ref, a_ref, m_ref, wa_ref, wb_ref, fg_ref, y_ref):
    y = x_ref[...] + _dot(a_ref[...], wa_ref[...]) + _dot(m_ref[...].astype(BF16), wb_ref[...])
    y_ref[...] = _rms_rows(y, fg_ref[...])


def _outproj(x2, a_out, mix_b, wa, wb, fg, tm):
    n = x2.shape[0]
    row = lambda i: (i, 0)
    const2 = lambda i: (0, 0)
    return pl.pallas_call(
        _outproj_kernel,
        grid=(n // tm,),
        in_specs=[pl.BlockSpec((tm, D_MODEL), row),
                  pl.BlockSpec((tm, a_out.shape[1]), row),
                  pl.BlockSpec((tm, mix_b.shape[1]), row),
                  pl.BlockSpec(wa.shape, const2), pl.BlockSpec(wb.shape, const2),
                  pl.BlockSpec((1, D_MODEL), const2)],
        out_specs=pl.BlockSpec((tm, D_MODEL), row),
        out_shape=jax.ShapeDtypeStruct((n, D_MODEL), F32),
        compiler_params=pltpu.CompilerParams(
            dimension_semantics=("arbitrary",), vmem_limit_bytes=VMEM_LIMIT_BYTES),
        name="outproj",
    )(x2, a_out, mix_b, wa, wb, fg)


def _page_copies(pt_ref, pool_ref, buf_ref, sem_ref, step, slot):
    group, n_pages = buf_ref.shape[1], buf_ref.shape[2]
    return [pltpu.make_async_copy(pool_ref.at[pt_ref[step * group + g, p]], buf_ref.at[slot, g, p],
                                  sem_ref.at[slot])
            for g in range(group) for p in range(n_pages)]


def _gather_pages(pt_ref, pools, bufs, sems):
    b = pl.program_id(0)
    slot = lax.rem(b, 2)

    def start(step, sl):
        for pool, buf, sem in zip(pools, bufs, sems):
            for cp in _page_copies(pt_ref, pool, buf, sem, step, sl):
                cp.start()

    @pl.when(b == 0)
    def _():
        start(0, 0)

    @pl.when(b + 1 < pl.num_programs(0))
    def _():
        start(b + 1, 1 - slot)

    for pool, buf, sem in zip(pools, bufs, sems):
        for cp in _page_copies(pt_ref, pool, buf, sem, b, slot):
            cp.wait()
    return slot


def _decode_cmp_kernel(pt_ref, ck_hbm, cv_hbm, q_ref, wk_ref, wv_ref, pek_ref, pev_ref, rope_ref, ovl_ref,
                       oc_ref, psum_ref, kbuf, vbuf, ksem, vsem, ktok_scr, vtok_scr):
    slot = _gather_pages(pt_ref, (ck_hbm, cv_hbm), (kbuf, vbuf), (ksem, vsem))
    group, n_pages = kbuf.shape[1], kbuf.shape[2]

    def chunk_rows(buf, tok_scr, g):
        for p in range(n_pages):
            tok_scr[g, p * PAGE_SIZE:(p + 1) * PAGE_SIZE, :] = buf[slot, g, p].T
        n = tok_scr.shape[1] // CMP_STRIDE
        return jnp.concatenate(
            [tok_scr[g, pl.ds(r, n, stride=CMP_STRIDE), :] for r in range(CMP_STRIDE)], axis=1)

    for g in range(group):
        kch = chunk_rows(kbuf, ktok_scr, g)
        vch = chunk_rows(vbuf, vtok_scr, g)
        kcc = _rope(_compress_chunks(kch, pek_ref, wk_ref), rope_ref[...]).astype(BF16)
        vcc = _compress_chunks(vch, pev_ref, wv_ref).astype(BF16)
        n_chunks = kch.shape[0]
        q = q_ref[g]
        rows = q.shape[0]
        s = _dot_nt(q, kcc)
        seen = _lane_iota((rows, n_chunks)) < n_chunks - 1
        p_c = jnp.where(seen, _softmax_rows(jnp.where(seen, s, NEG)), 0.0)
        oc_ref[g] = _dot(p_c.astype(BF16), vcc)[0:N_HEADS]
        ps0 = jnp.sum(p_c[0:GQA], axis=0, keepdims=True)
        ps1 = jnp.sum(p_c[GQA:2 * GQA], axis=0, keepdims=True)
        psum = jnp.where(_row_iota((rows, n_chunks)) < N_HEADS // 2, ps0, ps1)
        psum_ref[g] = _dot(psum.astype(BF16), ovl_ref[...])[0:N_HEADS]


def _decode_cmp(page_table, ck_pool, cv_pool, q16, wk, wv, pek, pev, rope_tab, ovl):
    nb, n_pages = page_table.shape
    group = DECODE_GROUP
    assert nb % group == 0
    hbm = pl.BlockSpec(memory_space=pl.ANY)
    page_buf = pltpu.VMEM((2, group, n_pages, KV_W, PAGE_SIZE), F32)
    tok_buf = pltpu.VMEM((group, n_pages * PAGE_SIZE, KV_W), F32)
    const2 = lambda b, pt: (0, 0)
    per_b = lambda a: pl.BlockSpec((group,) + a.shape[1:], lambda b, pt: (b, 0, 0))
    in_specs = [hbm, hbm, per_b(q16),
                pl.BlockSpec(wk.shape, const2), pl.BlockSpec(wv.shape, const2),
                pl.BlockSpec(pek.shape, const2), pl.BlockSpec(pev.shape, const2),
                pl.BlockSpec(rope_tab.shape, const2), pl.BlockSpec(ovl.shape, const2)]
    out_blk = pl.BlockSpec((group, N_HEADS, LANES), lambda b, pt: (b, 0, 0))
    grid_spec = pltpu.PrefetchScalarGridSpec(
        num_scalar_prefetch=1, grid=(nb // group,), in_specs=in_specs, out_specs=[out_blk, out_blk],
        scratch_shapes=[page_buf, page_buf, pltpu.SemaphoreType.DMA((2,)), pltpu.SemaphoreType.DMA((2,)),
                        tok_buf, tok_buf])
    return pl.pallas_call(
        _decode_cmp_kernel,
        grid_spec=grid_spec,
        out_shape=[jax.ShapeDtypeStruct((nb, N_HEADS, LANES), F32)] * 2,
        compiler_params=pltpu.CompilerParams(
            dimension_semantics=("arbitrary",), vmem_limit_bytes=VMEM_LIMIT_BYTES),
        name="decode_cmp",
    )(page_table, ck_pool, cv_pool, q16, wk, wv, pek, pev, rope_tab, ovl)


def _decode_topk_kernel(imp_ref, sel_ref, *, n_blk, cur):
    n = imp_ref.shape[0]
    tiles = [imp_ref[i * LANES:(i + 1) * LANES, :].T for i in range(n // LANES)]
    imp_t = jnp.concatenate(tiles, axis=1)
    kk = _row_iota(imp_t.shape)
    forced = (kk == 0) | (kk == cur) | (kk == cur - 1)
    selneg = jnp.where(_select_blocks(imp_t, kk, kk < n_blk, forced), 0.0, NEG)
    for i in range(n // LANES):
        sel_ref[i * LANES:(i + 1) * LANES, :] = selneg[:, i * LANES:(i + 1) * LANES].T


def _decode_topk(imp2, n_blk, cur):
    return pl.pallas_call(
        functools.partial(_decode_topk_kernel, n_blk=n_blk, cur=cur),
        out_shape=jax.ShapeDtypeStruct(imp2.shape, F32),
        compiler_params=pltpu.CompilerParams(vmem_limit_bytes=VMEM_LIMIT_BYTES),
        name="decode_topk",
    )(imp2)


def _decode_attend(s, q32, vals_t_bf, k_new, v_new, bias_new):
    kn = k_new.astype(BF16).astype(F32)
    s_new = jnp.sum(q32 * kn, axis=1, keepdims=True) + bias_new
    mx = jnp.maximum(jnp.max(s, axis=1, keepdims=True), s_new)
    ex = jnp.exp(s - mx)
    ex_new = jnp.exp(s_new - mx)
    inv = 1.0 / (jnp.sum(ex, axis=1, keepdims=True) + ex_new)
    p = (ex * inv).astype(BF16)
    p_new = (ex_new * inv).astype(BF16).astype(F32)
    return _dot_nt(p, vals_t_bf) + p_new * v_new.astype(BF16).astype(F32)


P_Q, P_SEL, P_OC, P_SZB, P_NEW, P_GATE, P_ROWS = 0, 16, 32, 40, 48, 56, 64


def _row_to_col(row):
    n = row.shape[1]
    diag = _row_iota((n, n)) == _lane_iota((n, n))
    return jnp.sum(jnp.where(diag, row, 0.0), axis=1, keepdims=True)


def _decode_sel_kernel(pt_ref, ks_hbm, vs_hbm, ckw_ref, cvw_ref, pack_ref, onehot_ref,
                       mix_ref, okw_ref, ovw_ref, kbuf, vbuf, ksem, vsem, *, cur):
    slot = _gather_pages(pt_ref, (ks_hbm, vs_hbm), (kbuf, vbuf), (ksem, vsem))
    group, n_pages = kbuf.shape[1], kbuf.shape[2]
    for g in range(group):
        pk = pack_ref[g]
        q32 = pk[P_Q:P_Q + 2 * N_HEADS]
        q = q32.astype(BF16)
        sel = pk[P_SEL:P_SEL + 2 * N_HEADS].astype(BF16)
        rows = q.shape[0]
        ks_new, vs_new = pk[P_NEW:P_NEW + 1], pk[P_NEW + 1:P_NEW + 2]
        kw_new, vw_new = pk[P_NEW + 2:P_NEW + 3], pk[P_NEW + 3:P_NEW + 4]

        kt = jnp.concatenate([kbuf[slot, g, p] for p in range(n_pages)], axis=1).astype(BF16)
        vt = jnp.concatenate([vbuf[slot, g, p] for p in range(n_pages)], axis=1).astype(BF16)
        kaug = jnp.concatenate([kt, onehot_ref[...]], axis=0)
        qaug = jnp.concatenate([q, sel], axis=1)
        s_sel = _dot(qaug, kaug)
        o_s = _decode_attend(s_sel, q32, vt, ks_new, vs_new, sel.astype(F32)[:, cur:cur + 1])

        kwin = ckw_ref[g]
        vwin = cvw_ref[g]
        keep = kwin.shape[1]
        visible = _lane_iota((rows, keep)) > keep - WINDOW
        s_win = jnp.where(visible, _dot(q, kwin.astype(BF16)), NEG)
        o_w = _decode_attend(s_win, q32, vwin.astype(BF16), kw_new, vw_new, 0.0)

        gates = pk[P_GATE:P_GATE + N_HEADS]
        o = (gates[:, 0:1] * pk[P_OC:P_OC + N_HEADS] + gates[:, 1:2] * o_s[0:N_HEADS]
             + gates[:, 2:3] * o_w[0:N_HEADS])
        mix_ref[g] = o * pk[P_SZB:P_SZB + N_HEADS]

        last = _lane_iota((KV_W, keep)) == keep - 1
        okw_ref[g] = jnp.where(last, _row_to_col(kw_new), pltpu.roll(kwin, keep - 1, 1))
        ovw_ref[g] = jnp.where(last, _row_to_col(vw_new), pltpu.roll(vwin, keep - 1, 1))


def _decode_sel(page_table, ks_pool, vs_pool, ckw, cvw, pack, onehot, cur):
    nb, n_pages = page_table.shape
    group = DECODE_GROUP
    assert nb % group == 0
    keep = ckw.shape[2]
    hbm = pl.BlockSpec(memory_space=pl.ANY)
    page_buf = pltpu.VMEM((2, group, n_pages, KV_W, PAGE_SIZE), F32)
    per_b = lambda a: pl.BlockSpec((group,) + a.shape[1:], lambda b, pt: (b, 0, 0))
    in_specs = [hbm, hbm, per_b(ckw), per_b(cvw), per_b(pack),
                pl.BlockSpec(onehot.shape, lambda b, pt: (0, 0))]
    out_specs = [pl.BlockSpec((group, N_HEADS, LANES), lambda b, pt: (b, 0, 0)),
                 pl.BlockSpec((group, KV_W, keep), lambda b, pt: (b, 0, 0)),
                 pl.BlockSpec((group, KV_W, keep), lambda b, pt: (b, 0, 0))]
    grid_spec = pltpu.PrefetchScalarGridSpec(
        num_scalar_prefetch=1, grid=(nb // group,), in_specs=in_specs, out_specs=out_specs,
        scratch_shapes=[page_buf, page_buf, pltpu.SemaphoreType.DMA((2,)), pltpu.SemaphoreType.DMA((2,))])
    return pl.pallas_call(
        functools.partial(_decode_sel_kernel, cur=cur),
        grid_spec=grid_spec,
        out_shape=[jax.ShapeDtypeStruct((nb, N_HEADS, LANES), F32),
                   jax.ShapeDtypeStruct((nb, KV_W, keep), F32),
                   jax.ShapeDtypeStruct((nb, KV_W, keep), F32)],
        compiler_params=pltpu.CompilerParams(
            dimension_semantics=("arbitrary",), vmem_limit_bytes=VMEM_LIMIT_BYTES),
        name="decode_sel",
    )(page_table, ks_pool, vs_pool, ckw, cvw, pack, onehot)


def _rope_table(pos):
    half = ROT_DIM // 2
    inv = jnp.power(jnp.float32(ROPE_THETA), -jnp.arange(half, dtype=F32) / half)
    ang = pos.astype(F32)[:, None] * inv[None, :]
    cs = jnp.concatenate([jnp.cos(ang), jnp.sin(ang)], axis=1)
    lane = jnp.arange(3 * LANES)
    seg, l64 = lane // LANES, lane % HEAD_DIM
    src = jnp.where(seg == 0, l64 % half, half + l64 % half)
    sign = jnp.where(seg == 0, l64 < ROT_DIM,
                     jnp.where(seg == 1, l64 < half, (l64 >= half) & (l64 < ROT_DIM))).astype(F32)
    sign = jnp.where(seg == 1, -sign, sign)
    place = (jnp.arange(ROT_DIM)[:, None] == src[None, :]).astype(F32) * sign[None, :]
    ones = ((seg == 0) & (l64 >= ROT_DIM)).astype(F32)
    return jnp.dot(cs, place, precision=lax.Precision.HIGHEST) + ones[None, :]


def _compress_weights(w_c, pe_c):
    w4 = w_c.reshape(2, CMP_STRIDE, HEAD_DIM, HEAD_DIM)
    wb = jnp.einsum('arde,hg->rhdage', w4, jnp.eye(N_KV, dtype=w_c.dtype))
    wb = wb.reshape(CMP_STRIDE * KV_W, 2 * KV_W).astype(BF16)
    pe = jnp.broadcast_to(pe_c.reshape(2, CMP_STRIDE, 1, HEAD_DIM), (2, CMP_STRIDE, N_KV, HEAD_DIM))
    return wb, pe.reshape(2, CMP_STRIDE * KV_W)


def _overlap_matrix(n_rows, n_blk):
    cs = jnp.arange(n_rows, dtype=jnp.int32)[:, None] * CMP_STRIDE
    bs = jnp.arange(LANES, dtype=jnp.int32)[None, :] * SEL_BLOCK
    hit = (cs < bs + SEL_BLOCK) & (cs + CMP_BLOCK > bs) & (jnp.arange(LANES)[None, :] < n_blk)
    return hit.astype(BF16)


def kernel(x_prompt, x_sample, cache_k_cmp, cache_v_cmp, cache_k_sel, cache_v_sel, cache_k_win,
           cache_v_win, page_table, norm_g, w_in, ln_v_g, ln_v_b, w_s, b_s, w_ck, pe_ck, w_cv,
           pe_cv, w_out, final_g):
    depth = norm_g.shape[0]
    assert depth == 1
    bsz, seq, _ = x_prompt.shape
    nb, dec_seq, _ = x_sample.shape
    assert dec_seq == 1 and seq % SEL_TILE == 0 and seq >= WIN_SPAN and (seq // CMP_STRIDE) % LANES == 0
    n_pages = page_table.shape[1]
    past = n_pages * PAGE_SIZE
    keep = cache_k_win.shape[2]
    assert keep == WINDOW and past % SEL_BLOCK == 0

    w = w_in[0]
    w_pad = (w[:, :C_G].astype(BF16),
             jnp.pad(w[:, C_G:C_G + N_GATE_COLS], ((0, 0), (0, LANES - N_GATE_COLS))).astype(BF16),
             w[:, C_G + N_GATE_COLS:].astype(BF16))
    g_row = norm_g[0][None, :]
    lng = ln_v_g[0][None, :]
    lnb = ln_v_b[0][None, :]
    fg = final_g[None, :]
    wk_c, pek = _compress_weights(w_ck[0], pe_ck[0])
    wv_c, pev = _compress_weights(w_cv[0], pe_cv[0])
    w_o = w_out[0].astype(BF16)
    wo_a, wo_b = w_o[:D_A], w_o[D_A:]

    tril = jnp.tril(jnp.ones((CHUNK, CHUNK), w_s.dtype))
    ws_bf = (w_s[0] * tril).astype(BF16)
    bias_full = jnp.repeat(b_s[0].T, HEAD_DIM, axis=1)
    xp2 = x_prompt.reshape(bsz * seq, D_MODEL)
    (a_out, qt, kc, vc, kc_t, vc_t, ks_t, vs_t, kw_t, vw_t, ksa, kwb, vsb, vwb, gates_t, szb) = _inproj_prompt(
        xp2, g_row, w_pad, lng, lnb, ws_bf, bias_full,
        _rope_table(jnp.arange(seq, dtype=jnp.int32)), seq)
    n_chunks = seq // CMP_STRIDE
    kcc, vcc_t = _compress_prompt(
        kc.reshape(bsz, n_chunks, CMP_STRIDE * KV_W), vc.reshape(bsz, n_chunks, CMP_STRIDE * KV_W),
        wk_c, wv_c, pek, pev,
        _rope_table(jnp.arange(n_chunks, dtype=jnp.int32) * CMP_STRIDE))
    b3 = lambda a: a.reshape(bsz, seq, a.shape[-1])
    mix_p = _attn_prompt(qt, gates_t, b3(szb), kcc, vcc_t,
                         _overlap_matrix(n_chunks, seq // SEL_BLOCK).T,
                         b3(ksa), vsb, b3(kwb), vwb)
    y_prompt = _outproj(xp2, a_out, mix_p.reshape(bsz * seq, D_B), wo_a, wo_b, fg, 512)
    y_prompt = y_prompt.reshape(bsz, seq, D_MODEL)
    st = lambda a: jnp.transpose(a.reshape(a.shape[0], N_KV, HEAD_DIM, a.shape[2]), (0, 3, 1, 2))[None]
    keep_p = min(WINDOW, seq)

    xs2 = x_sample.reshape(nb, D_MODEL)
    wsv = jnp.repeat(w_s[0][:, 0, 0], HEAD_DIM)[None, :]
    bsv = jnp.repeat(b_s[0][:, 0], HEAD_DIM)[None, :]
    (a_s, qa_s, kc_s, vc_s, ks_s, vs_s, kw_s, vw_s, gates_s, szb_s, vn_s) = _inproj_decode(
        xs2, g_row, w_pad, lng, lnb, wsv, bsv,
        _rope_table(jnp.full((nb,), past, dtype=jnp.int32)))
    q16 = jnp.pad(qa_s.reshape(nb, N_HEADS, LANES), ((0, 0), (0, N_HEADS), (0, 0)))
    d_chunks = past // CMP_STRIDE
    n_blk = -(-(past + 1) // SEL_BLOCK)
    cur = past // SEL_BLOCK
    kv_t = lambda c: jnp.transpose(c[0], (0, 2, 3, 1)).reshape(c.shape[1], KV_W, c.shape[2])
    oc_s, imp_s = _decode_cmp(
        page_table, kv_t(cache_k_cmp), kv_t(cache_v_cmp),
        q16, wk_c, wv_c, pek, pev,
        _rope_table(jnp.arange(d_chunks, dtype=jnp.int32) * CMP_STRIDE),
        _overlap_matrix(d_chunks, n_blk))
    selneg = _decode_topk(imp_s[:, ::GQA, :].reshape(nb * N_KV, LANES), n_blk, cur)
    sel16 = jnp.pad(jnp.repeat(selneg.reshape(nb, N_KV, LANES), GQA, axis=1), ((0, 0), (0, N_HEADS), (0, 0)))
    tok_blk = jnp.arange(past, dtype=jnp.int32)[None, :] // SEL_BLOCK
    onehot = (tok_blk == jnp.arange(LANES, dtype=jnp.int32)[:, None]).astype(BF16)
    gate_rows = jnp.pad(gates_s[:, :N_GATE_COLS].reshape(nb, N_HEADS, 3), ((0, 0), (0, 0), (0, LANES - 3)))
    new_rows = jnp.pad(jnp.stack([ks_s, vs_s, kw_s, vw_s], axis=1), ((0, 0), (0, P_GATE - P_NEW - 4), (0, 0)))
    pack = jnp.concatenate(
        [q16.astype(F32), sel16, oc_s, szb_s.reshape(nb, N_HEADS, LANES), new_rows, gate_rows], axis=1)
    assert pack.shape[1] == P_ROWS
    mix_s, okw, ovw = _decode_sel(
        page_table, kv_t(cache_k_sel), kv_t(cache_v_sel), kv_t(cache_k_win), kv_t(cache_v_win),
        pack, onehot, cur)
    wo_slab = jnp.zeros((N_HEADS, N_KV, HEAD_DIM, D_MODEL), BF16)
    wo_heads = wo_b.reshape(N_HEADS, HEAD_DIM, D_MODEL)
    for e in range(N_HEADS):
        wo_slab = wo_slab.at[e, e // GQA].set(wo_heads[e])
    y_sample = _outproj(xs2, a_s, mix_s.reshape(nb, N_HEADS * LANES), wo_a,
                        wo_slab.reshape(N_HEADS * LANES, D_MODEL), fg, nb)
    y_sample = y_sample.reshape(nb, 1, D_MODEL)
    ss = lambda a: a.reshape(1, nb, 1, N_KV, HEAD_DIM)

    return (y_prompt, y_sample,
            st(kc_t), st(vc_t), st(ks_t), st(vs_t),
            st(kw_t[:, :, seq - keep_p:]), st(vw_t[:, :, seq - keep_p:]),
            ss(kc_s), ss(vc_s), ss(ks_s), ss(vs_s),
            st(okw), st(ovw),
            vn_s.reshape(1, nb, 1, D_A))
```

```python
import functools

import jax
import jax.numpy as jnp
from jax import lax
from jax.experimental import pallas as pl
from jax.experimental.pallas import tpu as pltpu

F32 = jnp.float32
BF16 = jnp.bfloat16

D_MODEL = 1024
HEAD_DIM = 64
D_A = 512
D_B = 512
A_GROUPS = 8
CHUNK = 128
N_HEADS = 8
N_KV = 2
GQA = 4
KV_W = 128
ROT_DIM = 16
ROPE_THETA = 500000.0
CMP_BLOCK = 32
CMP_STRIDE = 16
SEL_BLOCK = 64
SEL_SHIFT = 6
N_SELECT = 16
WINDOW = 512
Q_BLOCK = 128
PAGE_SIZE = 128
NORM_EPS = 1e-6
FORCE_SCORE = 1e4
N_FORCED = 3
NEG = -1e30
LOG2_E = 1.4426950408889634

LANES = 128
VMEM_LIMIT_BYTES = 56 * 1024 * 1024

C_U, C_V, C_ZA, C_Q = 0, 512, 1024, 1536
C_KC, C_VC, C_KS, C_VS, C_KW, C_VW = 2048, 2176, 2304, 2432, 2560, 2688
C_G, C_ZB, C_END = 2816, 2944, 3456
N_GATE_COLS = 3 * N_HEADS

SEL_TILE = 512
DECODE_GROUP = 4
WIN_SPAN = WINDOW + Q_BLOCK

_NT = (((1,), (1,)), ((), ()))


def _dot(a, b):
    return jnp.dot(a, b, preferred_element_type=F32)


def _dot_nt(a, b):
    return lax.dot_general(a, b, _NT, preferred_element_type=F32)


def _lane_iota(shape):
    return lax.broadcasted_iota(jnp.int32, shape, len(shape) - 1)


def _row_iota(shape):
    return lax.broadcasted_iota(jnp.int32, shape, len(shape) - 2)


def _rope(x, tab):
    c = tab[:, 0:LANES]
    s1 = tab[:, LANES:2 * LANES]
    s2 = tab[:, 2 * LANES:3 * LANES]
    return x * c + pltpu.roll(x, LANES - ROT_DIM // 2, 1) * s1 + pltpu.roll(x, ROT_DIM // 2, 1) * s2


def _rms_rows(x, g):
    ms = jnp.mean(x * x, axis=-1, keepdims=True)
    return x * lax.rsqrt(ms + NORM_EPS) * g


def _layer_norm_rows(v, g, b):
    mu = jnp.mean(v, axis=-1, keepdims=True)
    vc = v - mu
    var = jnp.mean(vc * vc, axis=-1, keepdims=True)
    return vc * lax.rsqrt(var + NORM_EPS) * g + b


def _head_slabs(x512):
    out = []
    rows = x512.shape[0]
    lane = _lane_iota((rows, LANES))
    lo = lane < HEAD_DIM
    for j in range(4):
        slab = x512[:, j * LANES:(j + 1) * LANES]
        swapped = pltpu.roll(slab, HEAD_DIM, 1)
        if j < 2:
            out.append(jnp.where(lo, slab, 0.0))
            out.append(jnp.where(lo, swapped, 0.0))
        else:
            out.append(jnp.where(lo, 0.0, swapped))
            out.append(jnp.where(lo, 0.0, slab))
    return out


def _project(x_ref, g_ref, w_refs, z_scr):
    hb = _rms_rows(x_ref[...], g_ref[...]).astype(BF16)
    wm_ref, wg_ref, wz_ref = w_refs
    z_scr[:, 0:C_G] = _dot(hb, wm_ref[...])
    z_scr[:, C_G:C_ZB] = _dot(hb, wg_ref[...])
    z_scr[:, C_ZB:C_END] = _dot(hb, wz_ref[...])


def _inproj_prompt_kernel(x_ref, g_ref, wm_ref, wg_ref, wz_ref, lng_ref, lnb_ref, ws_ref, bias_ref, rope_ref,
                          aout_ref, qt_ref, kc_ref, vc_ref,
                          kct_ref, vct_ref, kst_ref, vst_ref, kwt_ref, vwt_ref,
                          ksa_ref, kwb_ref, vsb_ref, vwb_ref, gate_ref, szb_ref,
                          z_scr, kstage_scr, vstage_scr, *, tm, tiles_per_batch):
    _project(x_ref, g_ref, (wm_ref, wg_ref, wz_ref), z_scr)
    tab = rope_ref[...]

    lane = _lane_iota((CHUNK, LANES))
    lo = lane < HEAD_DIM
    for c in range(tm // CHUNK):
        r0 = c * CHUNK
        u = z_scr[r0:r0 + CHUNK, C_U:C_U + D_A]
        v = z_scr[r0:r0 + CHUNK, C_V:C_V + D_A]
        za = z_scr[r0:r0 + CHUNK, C_ZA:C_ZA + D_A]
        vn = _layer_norm_rows(v, lng_ref[...], lnb_ref[...])
        parts = []
        for p in range(A_GROUPS // 2):
            vp = vn[:, p * LANES:(p + 1) * LANES]
            v_lo = jnp.where(lo, vp, 0.0).astype(BF16)
            v_hi = jnp.where(lo, 0.0, vp).astype(BF16)
            parts.append(_dot(ws_ref[2 * p], v_lo) + _dot(ws_ref[2 * p + 1], v_hi))
        s = jnp.concatenate(parts, axis=1) + bias_ref[...]
        aout_ref[r0:r0 + CHUNK, :] = (u * s * jax.nn.silu(za)).astype(BF16)

    q = z_scr[:, C_Q:C_Q + D_B]
    qr = jnp.concatenate(
        [_rope(q[:, j * LANES:(j + 1) * LANES], tab) for j in range(4)], axis=1) * (HEAD_DIM ** -0.5 * LOG2_E)
    slabs = _head_slabs(qr)
    kc = z_scr[:, C_KC:C_KC + KV_W]
    vc = z_scr[:, C_VC:C_VC + KV_W]
    for src, stage, dst in ((kc, kstage_scr, kc_ref), (vc, vstage_scr, vc_ref)):
        stage[...] = src
        for r in range(CMP_STRIDE):
            dst[:, r * KV_W:(r + 1) * KV_W] = stage[pl.ds(r, tm // CMP_STRIDE, stride=CMP_STRIDE), :]
    ks = _rope(z_scr[:, C_KS:C_KS + KV_W], tab)
    vs = z_scr[:, C_VS:C_VS + KV_W]
    kw = _rope(z_scr[:, C_KW:C_KW + KV_W], tab)
    vw = z_scr[:, C_VW:C_VW + KV_W]
    gates = jax.nn.sigmoid(z_scr[:, C_G:C_G + LANES])
    pos = (pl.program_id(0) % tiles_per_batch) * tm + _row_iota((tm, LANES))
    onehot = jnp.where((pos >> SEL_SHIFT) == _lane_iota((tm, LANES)), 1.0, 0.0)
    ksa_ref[:, 0:LANES] = ks.astype(BF16)
    ksa_ref[:, LANES:2 * LANES] = onehot.astype(BF16)
    kwb_ref[...] = kw.astype(BF16)
    for j in range(tm // LANES):
        r0, r1 = j * LANES, (j + 1) * LANES
        kct_ref[0, :, r0:r1] = kc[r0:r1].T
        vct_ref[0, :, r0:r1] = vc[r0:r1].T
        kst_ref[0, :, r0:r1] = ks[r0:r1].T
        kwt_ref[0, :, r0:r1] = kw[r0:r1].T
        vs_t, vw_t = vs[r0:r1].T, vw[r0:r1].T
        vst_ref[0, :, r0:r1] = vs_t
        vwt_ref[0, :, r0:r1] = vw_t
        vsb_ref[0, j] = vs_t.astype(BF16)
        vwb_ref[0, j] = vw_t.astype(BF16)
        for e in range(N_HEADS):
            qt_ref[0, j, :, e * Q_BLOCK:(e + 1) * Q_BLOCK] = slabs[e][r0:r1].T.astype(BF16)
        gate_ref[0, j] = gates[r0:r1].T
    szb_ref[...] = jax.nn.silu(z_scr[:, C_ZB:C_ZB + D_B])


def _inproj_prompt(x2, g, w, lng, lnb, ws_bf, bias_full, rope_tab, seq, tm=512):
    n = x2.shape[0]
    tiles_per_batch = seq // tm
    bsz = n // seq
    n_tok_tiles = seq // LANES
    row = lambda i: (i, 0)
    const2 = lambda i: (0, 0)
    rows2d = lambda width, dtype: (jax.ShapeDtypeStruct((n, width), dtype), pl.BlockSpec((tm, width), row))
    kv_t = (jax.ShapeDtypeStruct((bsz, KV_W, seq), F32),
            pl.BlockSpec((1, KV_W, tm), lambda i: (i // tiles_per_batch, 0, i % tiles_per_batch)))
    chunk_rows = (jax.ShapeDtypeStruct((n // CMP_STRIDE, CMP_STRIDE * KV_W), F32),
                  pl.BlockSpec((tm // CMP_STRIDE, CMP_STRIDE * KV_W), row))
    tiles = lambda r, c, dtype: (
        jax.ShapeDtypeStruct((bsz, n_tok_tiles, r, c), dtype),
        pl.BlockSpec((1, tm // LANES, r, c), lambda i: (i // tiles_per_batch, i % tiles_per_batch, 0, 0)))
    outs = [
        rows2d(D_A, BF16),
        tiles(KV_W, N_HEADS * Q_BLOCK, BF16),
        chunk_rows, chunk_rows,
        kv_t, kv_t, kv_t, kv_t, kv_t, kv_t,
        rows2d(2 * KV_W, BF16),
        rows2d(KV_W, BF16),
        tiles(KV_W, LANES, BF16),
        tiles(KV_W, LANES, BF16),
        tiles(LANES, LANES, F32),
        rows2d(D_B, F32),
    ]
    out_shapes = [o[0] for o in outs]
    out_specs = [o[1] for o in outs]
    in_specs = [
        pl.BlockSpec((tm, D_MODEL), row),
        pl.BlockSpec((1, D_MODEL), const2),
        pl.BlockSpec((D_MODEL, C_G), const2),
        pl.BlockSpec((D_MODEL, C_ZB - C_G), const2),
        pl.BlockSpec((D_MODEL, C_END - C_ZB), const2),
        pl.BlockSpec((1, D_A), const2),
        pl.BlockSpec((1, D_A), const2),
        pl.BlockSpec((A_GROUPS, CHUNK, CHUNK), lambda i: (0, 0, 0)),
        pl.BlockSpec((CHUNK, D_A), const2),
        pl.BlockSpec((tm, 3 * LANES), lambda i: (i % tiles_per_batch, 0)),
    ]
    return pl.pallas_call(
        functools.partial(_inproj_prompt_kernel, tm=tm, tiles_per_batch=tiles_per_batch),
        grid=(n // tm,),
        in_specs=in_specs,
        out_specs=out_specs,
        out_shape=out_shapes,
        scratch_shapes=[pltpu.VMEM((tm, C_END), F32), pltpu.VMEM((tm, KV_W), F32), pltpu.VMEM((tm, KV_W), F32)],
        compiler_params=pltpu.CompilerParams(
            dimension_semantics=("arbitrary",), vmem_limit_bytes=VMEM_LIMIT_BYTES),
        name="inproj_prompt",
    )(x2, g, *w, lng, lnb, ws_bf, bias_full, rope_tab)


def _inproj_decode_kernel(x_ref, g_ref, wm_ref, wg_ref, wz_ref, lng_ref, lnb_ref, wsv_ref, bsv_ref, rope_ref,
                          aout_ref, qa_ref, kc_ref, vc_ref, ks_ref, vs_ref, kw_ref, vw_ref,
                          gate_ref, szb_ref, vn_ref, z_scr):
    _project(x_ref, g_ref, (wm_ref, wg_ref, wz_ref), z_scr)
    tab = rope_ref[...]
    u = z_scr[:, C_U:C_U + D_A]
    v = z_scr[:, C_V:C_V + D_A]
    za = z_scr[:, C_ZA:C_ZA + D_A]
    vn = _layer_norm_rows(v, lng_ref[...], lnb_ref[...])
    vn_ref[...] = vn
    s = vn * wsv_ref[...] + bsv_ref[...]
    aout_ref[...] = (u * s * jax.nn.silu(za)).astype(BF16)

    q = z_scr[:, C_Q:C_Q + D_B]
    qr = jnp.concatenate(
        [_rope(q[:, j * LANES:(j + 1) * LANES], tab) for j in range(4)], axis=1) * (HEAD_DIM ** -0.5)
    for e, slab in enumerate(_head_slabs(qr)):
        qa_ref[:, e * LANES:(e + 1) * LANES] = slab.astype(BF16)
    kc_ref[...] = z_scr[:, C_KC:C_KC + KV_W]
    vc_ref[...] = z_scr[:, C_VC:C_VC + KV_W]
    ks_ref[...] = _rope(z_scr[:, C_KS:C_KS + KV_W], tab)
    vs_ref[...] = z_scr[:, C_VS:C_VS + KV_W]
    kw_ref[...] = _rope(z_scr[:, C_KW:C_KW + KV_W], tab)
    vw_ref[...] = z_scr[:, C_VW:C_VW + KV_W]
    gate_ref[...] = jax.nn.sigmoid(z_scr[:, C_G:C_G + LANES])
    szb = jax.nn.silu(z_scr[:, C_ZB:C_ZB + D_B])
    for e, slab in enumerate(_head_slabs(szb)):
        szb_ref[:, e * LANES:(e + 1) * LANES] = slab


def _inproj_decode(x2, g, w, lng, lnb, wsv, bsv, rope_tab):
    n = x2.shape[0]
    out_shapes = [
        jax.ShapeDtypeStruct((n, D_A), BF16),
        jax.ShapeDtypeStruct((n, N_HEADS * LANES), BF16),
        jax.ShapeDtypeStruct((n, KV_W), F32),
        jax.ShapeDtypeStruct((n, KV_W), F32),
        jax.ShapeDtypeStruct((n, KV_W), F32),
        jax.ShapeDtypeStruct((n, KV_W), F32),
        jax.ShapeDtypeStruct((n, KV_W), F32),
        jax.ShapeDtypeStruct((n, KV_W), F32),
        jax.ShapeDtypeStruct((n, LANES), F32),
        jax.ShapeDtypeStruct((n, N_HEADS * LANES), F32),
        jax.ShapeDtypeStruct((n, D_A), F32),
    ]
    return pl.pallas_call(
        _inproj_decode_kernel,
        out_shape=out_shapes,
        scratch_shapes=[pltpu.VMEM((n, C_END), F32)],
        compiler_params=pltpu.CompilerParams(vmem_limit_bytes=VMEM_LIMIT_BYTES),
        name="inproj_decode",
    )(x2, g, *w, lng, lnb, wsv, bsv, rope_tab)


def _compress_chunks(ch, pe_ref, w_ref):
    a = _dot((ch + pe_ref[0:1, :]).astype(BF16), w_ref[:, 0:KV_W])
    b = _dot((ch + pe_ref[1:2, :]).astype(BF16), w_ref[:, KV_W:2 * KV_W])
    return a + pltpu.roll(b, ch.shape[0] - 1, 0)


def _compress_prompt_kernel(kc_ref, vc_ref, wk_ref, wv_ref, pek_ref, pev_ref, rope_ref, kcc_ref, vcct_ref):
    kcc = _compress_chunks(kc_ref[0], pek_ref, wk_ref)
    kcc_ref[0] = _rope(kcc, rope_ref[...]).astype(BF16)
    vcc = _compress_chunks(vc_ref[0], pev_ref, wv_ref)
    for j in range(vcc.shape[0] // LANES):
        vcct_ref[0, :, j * LANES:(j + 1) * LANES] = vcc[j * LANES:(j + 1) * LANES].T.astype(BF16)


def _compress_prompt(kc3, vc3, wk, wv, pek, pev, rope_tab):
    b, c, width = kc3.shape
    blk = pl.BlockSpec((1, c, width), lambda i: (i, 0, 0))
    const2 = lambda i: (0, 0)
    return pl.pallas_call(
        _compress_prompt_kernel,
        grid=(b,),
        in_specs=[blk, blk,
                  pl.BlockSpec(wk.shape, const2), pl.BlockSpec(wv.shape, const2),
                  pl.BlockSpec(pek.shape, const2), pl.BlockSpec(pev.shape, const2),
                  pl.BlockSpec(rope_tab.shape, const2)],
        out_specs=[pl.BlockSpec((1, c, KV_W), lambda i: (i, 0, 0)),
                   pl.BlockSpec((1, KV_W, c), lambda i: (i, 0, 0))],
        out_shape=[jax.ShapeDtypeStruct((b, c, KV_W), BF16), jax.ShapeDtypeStruct((b, KV_W, c), BF16)],
        compiler_params=pltpu.CompilerParams(
            dimension_semantics=("arbitrary",), vmem_limit_bytes=VMEM_LIMIT_BYTES),
        name="compress_prompt",
    )(kc3, vc3, wk, wv, pek, pev, rope_tab)


def _select_blocks(imp, kk, candidate, forced):
    assert FORCE_SCORE > 2 * GQA
    start = jnp.where(candidate, jnp.where(forced, -jnp.inf, imp), -jnp.inf)
    score = start
    for _ in range(N_SELECT - N_FORCED):
        cm = jnp.max(score, axis=0, keepdims=True)
        first = jnp.min(jnp.where(score == cm, kk, LANES), axis=0, keepdims=True)
        score = jnp.where(kk == first, -jnp.inf, score)
    taken = (score == -jnp.inf) & (start > -jnp.inf)
    return candidate & (forced | taken)


def _softmax_rows(s):
    mx = jnp.max(s, axis=1, keepdims=True)
    ex = jnp.exp(s - mx)
    return ex * (1.0 / jnp.sum(ex, axis=1, keepdims=True))


def _tile_heads(x):
    return jnp.concatenate([x] * N_HEADS, axis=1)


def _values_by_kv_head(vt, p):
    half = GQA * Q_BLOCK
    return jnp.concatenate(
        [_dot(vt[h * HEAD_DIM:(h + 1) * HEAD_DIM], p[:, h * half:(h + 1) * half]) for h in range(N_KV)], axis=1)


def _attn_prompt_kernel(qt_ref, gate_ref, szb_ref, kcc_ref, vcct_ref, ovlt_ref,
                        ksa_ref, vsb_ref, kwb_ref, vwb_ref, out_ref,
                        qaug_scr, acc_scr, sa_scr, sb_scr):
    i_blk = pl.program_id(1)
    q0 = i_blk * Q_BLOCK
    cols = N_HEADS * Q_BLOCK
    qt = qt_ref[0, 0]

    n_cmp = kcc_ref.shape[1]

    def compressed(extent):
        t_c = q0 + _lane_iota((extent, Q_BLOCK))
        seen = (_row_iota((extent, Q_BLOCK)) * CMP_STRIDE + (CMP_BLOCK - 1)) <= t_c
        s = _dot(kcc_ref[0, 0:extent, :], qt) + _tile_heads(jnp.where(seen, 0.0, NEG))
        mx = jnp.max(s, axis=0, keepdims=True)
        ex = jnp.exp2(s - mx)
        inv = jnp.where(mx > 0.5 * NEG, 1.0 / jnp.sum(ex, axis=0, keepdims=True), 0.0)
        p_c = ex * inv
        imps = []
        for h in range(N_KV):
            c0 = h * GQA * Q_BLOCK
            p_sum = (p_c[:, c0:c0 + Q_BLOCK] + p_c[:, c0 + Q_BLOCK:c0 + 2 * Q_BLOCK]
                     + p_c[:, c0 + 2 * Q_BLOCK:c0 + 3 * Q_BLOCK] + p_c[:, c0 + 3 * Q_BLOCK:c0 + 4 * Q_BLOCK])
            imps.append(_dot(ovlt_ref[:, 0:extent], p_sum.astype(BF16)))
        return (_values_by_kv_head(vcct_ref[0, :, 0:extent], p_c.astype(BF16)), *imps)

    n_need = (q0 + Q_BLOCK) // CMP_STRIDE
    extents = list(range(LANES, n_cmp + 1, LANES))
    o_c, *imp_t = lax.switch(
        jnp.minimum((n_need - 1) // LANES, len(extents) - 1),
        [functools.partial(compressed, ext) for ext in extents])

    w0 = pl.multiple_of(jnp.maximum(q0 - WINDOW, 0), Q_BLOCK)
    wj0 = jnp.maximum(i_blk - WINDOW // Q_BLOCK, 0)
    kwin = kwb_ref[0, pl.ds(w0, WIN_SPAN), :]
    vwt = jnp.concatenate([vwb_ref[0, wj0 + i] for i in range(WIN_SPAN // LANES)], axis=1)
    kp = w0 + _row_iota((WIN_SPAN, Q_BLOCK))
    tw = q0 + _lane_iota((WIN_SPAN, Q_BLOCK))
    in_win = jnp.where(kp <= tw, jnp.where(kp > tw - WINDOW, 0.0, NEG), NEG)
    sw = _dot(kwin, qt) + _tile_heads(in_win)
    exw = jnp.exp2(sw - jnp.max(sw, axis=0, keepdims=True))
    o_w = _values_by_kv_head(vwt, exw.astype(BF16)) * (1.0 / jnp.sum(exw, axis=0, keepdims=True))

    kk = _row_iota((LANES, Q_BLOCK))
    tq = q0 + _lane_iota((LANES, Q_BLOCK))
    valid = kk * SEL_BLOCK <= tq
    cur = tq >> SEL_SHIFT
    forced = (kk == 0) | (kk == cur) | (kk == cur - 1)
    both = lambda a: jnp.concatenate([a] * N_KV, axis=1)
    picked = _select_blocks(jnp.concatenate(imp_t, axis=1), both(kk), both(valid), both(forced))
    qaug_scr[0:LANES, :] = qt
    for h in range(N_KV):
        selneg = jnp.where(picked[:, h * Q_BLOCK:(h + 1) * Q_BLOCK], 0.0, NEG).astype(BF16)
        for g in range(GQA):
            c0 = (h * GQA + g) * Q_BLOCK
            qaug_scr[LANES:2 * LANES, c0:c0 + Q_BLOCK] = selneg

    acc_scr[...] = jnp.zeros(acc_scr.shape, F32)
    tiles_per_step = SEL_TILE // LANES

    def score(step, dst):
        t0 = pl.multiple_of(step * SEL_TILE, SEL_TILE)
        dst[...] = _dot(ksa_ref[0, pl.ds(t0, SEL_TILE), :], qaug_scr[...])

    def attend(src, step, m_prev, l_prev, causal_rows=None):
        rows = SEL_TILE if causal_rows is None else causal_rows
        vt = jnp.concatenate([vsb_ref[0, step * tiles_per_step + i] for i in range(rows // LANES)], axis=1)
        sc = src[0:rows, :]
        if causal_rows is not None:
            tok = step * SEL_TILE + _row_iota((rows, Q_BLOCK))
            sc = sc + _tile_heads(jnp.where(tok <= q0 + _lane_iota((rows, Q_BLOCK)), 0.0, NEG))
        m_next = jnp.maximum(m_prev, jnp.max(sc, axis=0, keepdims=True))
        p = jnp.exp2(sc - m_next)
        alpha = jnp.exp2(m_prev - m_next)
        l_next = alpha * l_prev + jnp.sum(p, axis=0, keepdims=True)
        acc_scr[...] = alpha * acc_scr[...] + _values_by_kv_head(vt, p.astype(BF16))
        return m_next, l_next

    n_full = q0 // SEL_TILE
    score(0, sa_scr)

    def pair(jj, carry):
        m_c, l_c = carry
        score(2 * jj + 1, sb_scr)
        m_c, l_c = attend(sa_scr, 2 * jj, m_c, l_c)
        score(2 * jj + 2, sa_scr)
        return attend(sb_scr, 2 * jj + 1, m_c, l_c)

    m_run, l_run = lax.fori_loop(
        0, n_full // 2, pair,
        (jnp.full((1, cols), -jnp.inf, F32), jnp.zeros((1, cols), F32)))

    def last_step(src, m_c, l_c):
        extents = list(range(Q_BLOCK, SEL_TILE + 1, Q_BLOCK))
        return lax.switch(
            (q0 % SEL_TILE) // Q_BLOCK,
            [lambda m, l, ext=ext: attend(src, n_full, m, l, causal_rows=ext)[1] for ext in extents],
            m_c, l_c)

    def odd_tail(m_c, l_c):
        score(n_full, sb_scr)
        m_c, l_c = attend(sa_scr, n_full - 1, m_c, l_c)
        return last_step(sb_scr, m_c, l_c)

    def even_tail(m_c, l_c):
        return last_step(sa_scr, m_c, l_c)

    l_run = lax.cond(n_full % 2 == 1, odd_tail, even_tail, m_run, l_run)

    gt = gate_ref[0, 0]
    o_s = acc_scr[...] * (1.0 / l_run)
    heads = []
    for e in range(N_HEADS):
        c0 = e * Q_BLOCK
        heads.append(gt[3 * e:3 * e + 1, :] * o_c[:, c0:c0 + Q_BLOCK]
                     + gt[3 * e + 1:3 * e + 2, :] * o_s[:, c0:c0 + Q_BLOCK]
                     + gt[3 * e + 2:3 * e + 3, :] * o_w[:, c0:c0 + Q_BLOCK])
    for j in range(N_HEADS // 2):
        slab = jnp.concatenate([heads[2 * j], heads[2 * j + 1]], axis=0).T
        out_ref[0, :, j * LANES:(j + 1) * LANES] = (
            slab * szb_ref[0, :, j * LANES:(j + 1) * LANES]).astype(BF16)


def _attn_prompt(qt4, gate4, szb3, kcc, vcct, ovlt, ksa3, vsb4, kwb3, vwb4):
    b, nq = qt4.shape[0], qt4.shape[1]
    t = nq * Q_BLOCK
    cols = N_HEADS * Q_BLOCK
    qtile = lambda a: pl.BlockSpec((1, 1) + a.shape[2:], lambda bi, i: (bi, i, 0, 0))
    seq = lambda a: pl.BlockSpec((1,) + a.shape[1:], lambda bi, i, nd=a.ndim: (bi,) + (0,) * (nd - 1))
    return pl.pallas_call(
        _attn_prompt_kernel,
        grid=(b, nq),
        in_specs=[qtile(qt4), qtile(gate4), pl.BlockSpec((1, Q_BLOCK, D_B), lambda bi, i: (bi, i, 0)),
                  seq(kcc), seq(vcct), pl.BlockSpec(ovlt.shape, lambda bi, i: (0, 0)),
                  seq(ksa3), seq(vsb4), seq(kwb3), seq(vwb4)],
        out_specs=pl.BlockSpec((1, Q_BLOCK, D_B), lambda bi, i: (bi, i, 0)),
        out_shape=jax.ShapeDtypeStruct((b, t, D_B), BF16),
        scratch_shapes=[pltpu.VMEM((2 * LANES, cols), BF16),
                        pltpu.VMEM((HEAD_DIM, cols), F32),
                        pltpu.VMEM((SEL_TILE, cols), F32),
                        pltpu.VMEM((SEL_TILE, cols), F32)],
        compiler_params=pltpu.CompilerParams(
            dimension_semantics=("arbitrary", "arbitrary"), vmem_limit_bytes=VMEM_LIMIT_BYTES),
        name="attn_prompt",
    )(qt4, gate4, szb3, kcc, vcct, ovlt, ksa3, vsb4, kwb3, vwb4)


def _outproj_kernel(x_ref, a_ref, m_ref, wa_ref, wb_ref, fg_ref, y_ref):
    y = x_ref[...] + _dot(a_ref[...], wa_ref[...]) + _dot(m_ref[...].astype(BF16), wb_ref[...])
    y_ref[...] = _rms_rows(y, fg_ref[...])


def _outproj(x2, a_out, mix_b, wa, wb, fg, tm):
    n = x2.shape[0]
    row = lambda i: (i, 0)
    const2 = lambda i: (0, 0)
    return pl.pallas_call(
        _outproj_kernel,
        grid=(n // tm,),
        in_specs=[pl.BlockSpec((tm, D_MODEL), row),
                  pl.BlockSpec((tm, a_out.shape[1]), row),
                  pl.BlockSpec((tm, mix_b.shape[1]), row),
                  pl.BlockSpec(wa.shape, const2), pl.BlockSpec(wb.shape, const2),
                  pl.BlockSpec((1, D_MODEL), const2)],
        out_specs=pl.BlockSpec((tm, D_MODEL), row),
        out_shape=jax.ShapeDtypeStruct((n, D_MODEL), F32),
        compiler_params=pltpu.CompilerParams(
            dimension_semantics=("arbitrary",), vmem_limit_bytes=VMEM_LIMIT_BYTES),
        name="outproj",
    )(x2, a_out, mix_b, wa, wb, fg)


def _page_copies(pt_ref, pool_ref, buf_ref, sem_ref, step, slot):
    group, n_pages = buf_ref.shape[1], buf_ref.shape[2]
    return [pltpu.make_async_copy(pool_ref.at[pt_ref[step * group + g, p]], buf_ref.at[slot, g, p],
                                  sem_ref.at[slot])
            for g in range(group) for p in range(n_pages)]


def _gather_pages(pt_ref, pools, bufs, sems):
    b = pl.program_id(0)
    slot = lax.rem(b, 2)

    def start(step, sl):
        for pool, buf, sem in zip(pools, bufs, sems):
            for cp in _page_copies(pt_ref, pool, buf, sem, step, sl):
                cp.start()

    @pl.when(b == 0)
    def _():
        start(0, 0)

    @pl.when(b + 1 < pl.num_programs(0))
    def _():
        start(b + 1, 1 - slot)

    for pool, buf, sem in zip(pools, bufs, sems):
        for cp in _page_copies(pt_ref, pool, buf, sem, b, slot):
            cp.wait()
    return slot


def _decode_cmp_kernel(pt_ref, ck_hbm, cv_hbm, q_ref, wk_ref, wv_ref, pek_ref, pev_ref, rope_ref, ovl_ref,
                       oc_ref, psum_ref, kbuf, vbuf, ksem, vsem, ktok_scr, vtok_scr):
    slot = _gather_pages(pt_ref, (ck_hbm, cv_hbm), (kbuf, vbuf), (ksem, vsem))
    group, n_pages = kbuf.shape[1], kbuf.shape[2]

    def chunk_rows(buf, tok_scr, g):
        for p in range(n_pages):
            tok_scr[g, p * PAGE_SIZE:(p + 1) * PAGE_SIZE, :] = buf[slot, g, p].T
        n = tok_scr.shape[1] // CMP_STRIDE
        return jnp.concatenate(
            [tok_scr[g, pl.ds(r, n, stride=CMP_STRIDE), :] for r in range(CMP_STRIDE)], axis=1)

    for g in range(group):
        kch = chunk_rows(kbuf, ktok_scr, g)
        vch = chunk_rows(vbuf, vtok_scr, g)
        kcc = _rope(_compress_chunks(kch, pek_ref, wk_ref), rope_ref[...]).astype(BF16)
        vcc = _compress_chunks(vch, pev_ref, wv_ref).astype(BF16)
        n_chunks = kch.shape[0]
        q = q_ref[g]
        rows = q.shape[0]
        s = _dot_nt(q, kcc)
        seen = _lane_iota((rows, n_chunks)) < n_chunks - 1
        p_c = jnp.where(seen, _softmax_rows(jnp.where(seen, s, NEG)), 0.0)
        oc_ref[g] = _dot(p_c.astype(BF16), vcc)[0:N_HEADS]
        ps0 = jnp.sum(p_c[0:GQA], axis=0, keepdims=True)
        ps1 = jnp.sum(p_c[GQA:2 * GQA], axis=0, keepdims=True)
        psum = jnp.where(_row_iota((rows, n_chunks)) < N_HEADS // 2, ps0, ps1)
        psum_ref[g] = _dot(psum.astype(BF16), ovl_ref[...])[0:N_HEADS]


def _decode_cmp(page_table, ck_pool, cv_pool, q16, wk, wv, pek, pev, rope_tab, ovl):
    nb, n_pages = page_table.shape
    group = DECODE_GROUP
    assert nb % group == 0
    hbm = pl.BlockSpec(memory_space=pl.ANY)
    page_buf = pltpu.VMEM((2, group, n_pages, KV_W, PAGE_SIZE), F32)
    tok_buf = pltpu.VMEM((group, n_pages * PAGE_SIZE, KV_W), F32)
    const2 = lambda b, pt: (0, 0)
    per_b = lambda a: pl.BlockSpec((group,) + a.shape[1:], lambda b, pt: (b, 0, 0))
    in_specs = [hbm, hbm, per_b(q16),
                pl.BlockSpec(wk.shape, const2), pl.BlockSpec(wv.shape, const2),
                pl.BlockSpec(pek.shape, const2), pl.BlockSpec(pev.shape, const2),
                pl.BlockSpec(rope_tab.shape, const2), pl.BlockSpec(ovl.shape, const2)]
    out_blk = pl.BlockSpec((group, N_HEADS, LANES), lambda b, pt: (b, 0, 0))
    grid_spec = pltpu.PrefetchScalarGridSpec(
        num_scalar_prefetch=1, grid=(nb // group,), in_specs=in_specs, out_specs=[out_blk, out_blk],
        scratch_shapes=[page_buf, page_buf, pltpu.SemaphoreType.DMA((2,)), pltpu.SemaphoreType.DMA((2,)),
                        tok_buf, tok_buf])
    return pl.pallas_call(
        _decode_cmp_kernel,
        grid_spec=grid_spec,
        out_shape=[jax.ShapeDtypeStruct((nb, N_HEADS, LANES), F32)] * 2,
        compiler_params=pltpu.CompilerParams(
            dimension_semantics=("arbitrary",), vmem_limit_bytes=VMEM_LIMIT_BYTES),
        name="decode_cmp",
    )(page_table, ck_pool, cv_pool, q16, wk, wv, pek, pev, rope_tab, ovl)


def _decode_topk_kernel(imp_ref, sel_ref, *, n_blk, cur):
    n = imp_ref.shape[0]
    tiles = [imp_ref[i * LANES:(i + 1) * LANES, :].T for i in range(n // LANES)]
    imp_t = jnp.concatenate(tiles, axis=1)
    kk = _row_iota(imp_t.shape)
    forced = (kk == 0) | (kk == cur) | (kk == cur - 1)
    selneg = jnp.where(_select_blocks(imp_t, kk, kk < n_blk, forced), 0.0, NEG)
    for i in range(n // LANES):
        sel_ref[i * LANES:(i + 1) * LANES, :] = selneg[:, i * LANES:(i + 1) * LANES].T


def _decode_topk(imp2, n_blk, cur):
    return pl.pallas_call(
        functools.partial(_decode_topk_kernel, n_blk=n_blk, cur=cur),
        out_shape=jax.ShapeDtypeStruct(imp2.shape, F32),
        compiler_params=pltpu.CompilerParams(vmem_limit_bytes=VMEM_LIMIT_BYTES),
        name="decode_topk",
    )(imp2)


def _decode_attend(s, q32, vals_t_bf, k_new, v_new, bias_new):
    kn = k_new.astype(BF16).astype(F32)
    s_new = jnp.sum(q32 * kn, axis=1, keepdims=True) + bias_new
    mx = jnp.maximum(jnp.max(s, axis=1, keepdims=True), s_new)
    ex = jnp.exp(s - mx)
    ex_new = jnp.exp(s_new - mx)
    inv = 1.0 / (jnp.sum(ex, axis=1, keepdims=True) + ex_new)
    p = (ex * inv).astype(BF16)
    p_new = (ex_new * inv).astype(BF16).astype(F32)
    return _dot_nt(p, vals_t_bf) + p_new * v_new.astype(BF16).astype(F32)


P_Q, P_SEL, P_OC, P_SZB, P_NEW, P_GATE, P_ROWS = 0, 16, 32, 40, 48, 56, 64


def _row_to_col(row):
    n = row.shape[1]
    diag = _row_iota((n, n)) == _lane_iota((n, n))
    return jnp.sum(jnp.where(diag, row, 0.0), axis=1, keepdims=True)


def _decode_sel_kernel(pt_ref, ks_hbm, vs_hbm, ckw_ref, cvw_ref, pack_ref, onehot_ref,
                       mix_ref, okw_ref, ovw_ref, kbuf, vbuf, ksem, vsem, *, cur):
    slot = _gather_pages(pt_ref, (ks_hbm, vs_hbm), (kbuf, vbuf), (ksem, vsem))
    group, n_pages = kbuf.shape[1], kbuf.shape[2]
    for g in range(group):
        pk = pack_ref[g]
        q32 = pk[P_Q:P_Q + 2 * N_HEADS]
        q = q32.astype(BF16)
        sel = pk[P_SEL:P_SEL + 2 * N_HEADS].astype(BF16)
        rows = q.shape[0]
        ks_new, vs_new = pk[P_NEW:P_NEW + 1], pk[P_NEW + 1:P_NEW + 2]
        kw_new, vw_new = pk[P_NEW + 2:P_NEW + 3], pk[P_NEW + 3:P_NEW + 4]

        kt = jnp.concatenate([kbuf[slot, g, p] for p in range(n_pages)], axis=1).astype(BF16)
        vt = jnp.concatenate([vbuf[slot, g, p] for p in range(n_pages)], axis=1).astype(BF16)
        kaug = jnp.concatenate([kt, onehot_ref[...]], axis=0)
        qaug = jnp.concatenate([q, sel], axis=1)
        s_sel = _dot(qaug, kaug)
        o_s = _decode_attend(s_sel, q32, vt, ks_new, vs_new, sel.astype(F32)[:, cur:cur + 1])

        kwin = ckw_ref[g]
        vwin = cvw_ref[g]
        keep = kwin.shape[1]
        visible = _lane_iota((rows, keep)) > keep - WINDOW
        s_win = jnp.where(visible, _dot(q, kwin.astype(BF16)), NEG)
        o_w = _decode_attend(s_win, q32, vwin.astype(BF16), kw_new, vw_new, 0.0)

        gates = pk[P_GATE:P_GATE + N_HEADS]
        o = (gates[:, 0:1] * pk[P_OC:P_OC + N_HEADS] + gates[:, 1:2] * o_s[0:N_HEADS]
             + gates[:, 2:3] * o_w[0:N_HEADS])
        mix_ref[g] = o * pk[P_SZB:P_SZB + N_HEADS]

        last = _lane_iota((KV_W, keep)) == keep - 1
        okw_ref[g] = jnp.where(last, _row_to_col(kw_new), pltpu.roll(kwin, keep - 1, 1))
        ovw_ref[g] = jnp.where(last, _row_to_col(vw_new), pltpu.roll(vwin, keep - 1, 1))


def _decode_sel(page_table, ks_pool, vs_pool, ckw, cvw, pack, onehot, cur):
    nb, n_pages = page_table.shape
    group = DECODE_GROUP
    assert nb % group == 0
    keep = ckw.shape[2]
    hbm = pl.BlockSpec(memory_space=pl.ANY)
    page_buf = pltpu.VMEM((2, group, n_pages, KV_W, PAGE_SIZE), F32)
    per_b = lambda a: pl.BlockSpec((group,) + a.shape[1:], lambda b, pt: (b, 0, 0))
    in_specs = [hbm, hbm, per_b(ckw), per_b(cvw), per_b(pack),
                pl.BlockSpec(onehot.shape, lambda b, pt: (0, 0))]
    out_specs = [pl.BlockSpec((group, N_HEADS, LANES), lambda b, pt: (b, 0, 0)),
                 pl.BlockSpec((group, KV_W, keep), lambda b, pt: (b, 0, 0)),
                 pl.BlockSpec((group, KV_W, keep), lambda b, pt: (b, 0, 0))]
    grid_spec = pltpu.PrefetchScalarGridSpec(
        num_scalar_prefetch=1, grid=(nb // group,), in_specs=in_specs, out_specs=out_specs,
        scratch_shapes=[page_buf, page_buf, pltpu.SemaphoreType.DMA((2,)), pltpu.SemaphoreType.DMA((2,))])
    return pl.pallas_call(
        functools.partial(_decode_sel_kernel, cur=cur),
        grid_spec=grid_spec,
        out_shape=[jax.ShapeDtypeStruct((nb, N_HEADS, LANES), F32),
                   jax.ShapeDtypeStruct((nb, KV_W, keep), F32),
                   jax.ShapeDtypeStruct((nb, KV_W, keep), F32)],
        compiler_params=pltpu.CompilerParams(
            dimension_semantics=("arbitrary",), vmem_limit_bytes=VMEM_LIMIT_BYTES),
        name="decode_sel",
    )(page_table, ks_pool, vs_pool, ckw, cvw, pack, onehot)


def _rope_table(pos):
    half = ROT_DIM // 2
    inv = jnp.power(jnp.float32(ROPE_THETA), -jnp.arange(half, dtype=F32) / half)
    ang = pos.astype(F32)[:, None] * inv[None, :]
    cs = jnp.concatenate([jnp.cos(ang), jnp.sin(ang)], axis=1)
    lane = jnp.arange(3 * LANES)
    seg, l64 = lane // LANES, lane % HEAD_DIM
    src = jnp.where(seg == 0, l64 % half, half + l64 % half)
    sign = jnp.where(seg == 0, l64 < ROT_DIM,
                     jnp.where(seg == 1, l64 < half, (l64 >= half) & (l64 < ROT_DIM))).astype(F32)
    sign = jnp.where(seg == 1, -sign, sign)
    place = (jnp.arange(ROT_DIM)[:, None] == src[None, :]).astype(F32) * sign[None, :]
    ones = ((seg == 0) & (l64 >= ROT_DIM)).astype(F32)
    return jnp.dot(cs, place, precision=lax.Precision.HIGHEST) + ones[None, :]


def _compress_weights(w_c, pe_c):
    w4 = w_c.reshape(2, CMP_STRIDE, HEAD_DIM, HEAD_DIM)
    wb = jnp.einsum('arde,hg->rhdage', w4, jnp.eye(N_KV, dtype=w_c.dtype))
    wb = wb.reshape(CMP_STRIDE * KV_W, 2 * KV_W).astype(BF16)
    pe = jnp.broadcast_to(pe_c.reshape(2, CMP_STRIDE, 1, HEAD_DIM), (2, CMP_STRIDE, N_KV, HEAD_DIM))
    return wb, pe.reshape(2, CMP_STRIDE * KV_W)


def _overlap_matrix(n_rows, n_blk):
    cs = jnp.arange(n_rows, dtype=jnp.int32)[:, None] * CMP_STRIDE
    bs = jnp.arange(LANES, dtype=jnp.int32)[None, :] * SEL_BLOCK
    hit = (cs < bs + SEL_BLOCK) & (cs + CMP_BLOCK > bs) & (jnp.arange(LANES)[None, :] < n_blk)
    return hit.astype(BF16)


def kernel(x_prompt, x_sample, cache_k_cmp, cache_v_cmp, cache_k_sel, cache_v_sel, cache_k_win,
           cache_v_win, page_table, norm_g, w_in, ln_v_g, ln_v_b, w_s, b_s, w_ck, pe_ck, w_cv,
           pe_cv, w_out, final_g):
    depth = norm_g.shape[0]
    assert depth == 1
    bsz, seq, _ = x_prompt.shape
    nb, dec_seq, _ = x_sample.shape
    assert dec_seq == 1 and seq % SEL_TILE == 0 and seq >= WIN_SPAN and (seq // CMP_STRIDE) % LANES == 0
    n_pages = page_table.shape[1]
    past = n_pages * PAGE_SIZE
    keep = cache_k_win.shape[2]
    assert keep == WINDOW and past % SEL_BLOCK == 0

    w = w_in[0]
    w_pad = (w[:, :C_G].astype(BF16),
             jnp.pad(w[:, C_G:C_G + N_GATE_COLS], ((0, 0), (0, LANES - N_GATE_COLS))).astype(BF16),
             w[:, C_G + N_GATE_COLS:].astype(BF16))
    g_row = norm_g[0][None, :]
    lng = ln_v_g[0][None, :]
    lnb = ln_v_b[0][None, :]
    fg = final_g[None, :]
    wk_c, pek = _compress_weights(w_ck[0], pe_ck[0])
    wv_c, pev = _compress_weights(w_cv[0], pe_cv[0])
    w_o = w_out[0].astype(BF16)
    wo_a, wo_b = w_o[:D_A], w_o[D_A:]

    tril = jnp.tril(jnp.ones((CHUNK, CHUNK), w_s.dtype))
    ws_bf = (w_s[0] * tril).astype(BF16)
    bias_full = jnp.repeat(b_s[0].T, HEAD_DIM, axis=1)
    xp2 = x_prompt.reshape(bsz * seq, D_MODEL)
    (a_out, qt, kc, vc, kc_t, vc_t, ks_t, vs_t, kw_t, vw_t, ksa, kwb, vsb, vwb, gates_t, szb) = _inproj_prompt(
        xp2, g_row, w_pad, lng, lnb, ws_bf, bias_full,
        _rope_table(jnp.arange(seq, dtype=jnp.int32)), seq)
    n_chunks = seq // CMP_STRIDE
    kcc, vcc_t = _compress_prompt(
        kc.reshape(bsz, n_chunks, CMP_STRIDE * KV_W), vc.reshape(bsz, n_chunks, CMP_STRIDE * KV_W),
        wk_c, wv_c, pek, pev,
        _rope_table(jnp.arange(n_chunks, dtype=jnp.int32) * CMP_STRIDE))
    b3 = lambda a: a.reshape(bsz, seq, a.shape[-1])
    mix_p = _attn_prompt(qt, gates_t, b3(szb), kcc, vcc_t,
                         _overlap_matrix(n_chunks, seq // SEL_BLOCK).T,
                         b3(ksa), vsb, b3(kwb), vwb)
    y_prompt = _outproj(xp2, a_out, mix_p.reshape(bsz * seq, D_B), wo_a, wo_b, fg, 512)
    y_prompt = y_prompt.reshape(bsz, seq, D_MODEL)
    st = lambda a: jnp.transpose(a.reshape(a.shape[0], N_KV, HEAD_DIM, a.shape[2]), (0, 3, 1, 2))[None]
    keep_p = min(WINDOW, seq)

    xs2 = x_sample.reshape(nb, D_MODEL)
    wsv = jnp.repeat(w_s[0][:, 0, 0], HEAD_DIM)[None, :]
    bsv = jnp.repeat(b_s[0][:, 0], HEAD_DIM)[None, :]
    (a_s, qa_s, kc_s, vc_s, ks_s, vs_s, kw_s, vw_s, gates_s, szb_s, vn_s) = _inproj_decode(
        xs2, g_row, w_pad, lng, lnb, wsv, bsv,
        _rope_table(jnp.full((nb,), past, dtype=jnp.int32)))
    q16 = jnp.pad(qa_s.reshape(nb, N_HEADS, LANES), ((0, 0), (0, N_HEADS), (0, 0)))
    d_chunks = past // CMP_STRIDE
    n_blk = -(-(past + 1) // SEL_BLOCK)
    cur = past // SEL_BLOCK
    kv_t = lambda c: jnp.transpose(c[0], (0, 2, 3, 1)).reshape(c.shape[1], KV_W, c.shape[2])
    oc_s, imp_s = _decode_cmp(
        page_table, kv_t(cache_k_cmp), kv_t(cache_v_cmp),
        q16, wk_c, wv_c, pek, pev,
        _rope_table(jnp.arange(d_chunks, dtype=jnp.int32) * CMP_STRIDE),
        _overlap_matrix(d_chunks, n_blk))
    selneg = _decode_topk(imp_s[:, ::GQA, :].reshape(nb * N_KV, LANES), n_blk, cur)
    sel16 = jnp.pad(jnp.repeat(selneg.reshape(nb, N_KV, LANES), GQA, axis=1), ((0, 0), (0, N_HEADS), (0, 0)))
    tok_blk = jnp.arange(past, dtype=jnp.int32)[None, :] // SEL_BLOCK
    onehot = (tok_blk == jnp.arange(LANES, dtype=jnp.int32)[:, None]).astype(BF16)
    gate_rows = jnp.pad(gates_s[:, :N_GATE_COLS].reshape(nb, N_HEADS, 3), ((0, 0), (0, 0), (0, LANES - 3)))
    new_rows = jnp.pad(jnp.stack([ks_s, vs_s, kw_s, vw_s], axis=1), ((0, 0), (0, P_GATE - P_NEW - 4), (0, 0)))
    pack = jnp.concatenate(
        [q16.astype(F32), sel16, oc_s, szb_s.reshape(nb, N_HEADS, LANES), new_rows, gate_rows], axis=1)
    assert pack.shape[1] == P_ROWS
    mix_s, okw, ovw = _decode_sel(
        page_table, kv_t(cache_k_sel), kv_t(cache_v_sel), kv_t(cache_k_win), kv_t(cache_v_win),
        pack, onehot, cur)
    wo_slab = jnp.zeros((N_HEADS, N_KV, HEAD_DIM, D_MODEL), BF16)
    wo_heads = wo_b.reshape(N_HEADS, HEAD_DIM, D_MODEL)
    for e in range(N_HEADS):
        wo_slab = wo_slab.at[e, e // GQA].set(wo_heads[e])
    y_sample = _outproj(xs2, a_s, mix_s.reshape(nb, N_HEADS * LANES), wo_a,
                        wo_slab.reshape(N_HEADS * LANES, D_MODEL), fg, nb)
    y_sample = y_sample.reshape(nb, 1, D_MODEL)
    ss = lambda a: a.reshape(1, nb, 1, N_KV, HEAD_DIM)

    return (y_prompt, y_sample,
            st(kc_t), st(vc_t), st(ks_t), st(vs_t),
            st(kw_t[:, :, seq - keep_p:]), st(vw_t[:, :, seq - keep_p:]),
            ss(kc_s), ss(vc_s), ss(ks_s), ss(vs_s),
            st(okw), st(ovw),
            vn_s.reshape(1, nb, 1, D_A))
```

```python
import functools

import jax
import jax.numpy as jnp
from jax import lax
from jax.experimental import pallas as pl
from jax.experimental.pallas import tpu as pltpu

F32 = jnp.float32
BF16 = jnp.bfloat16

D_MODEL = 1024
HEAD_DIM = 64
D_A = 512
D_B = 512
A_GROUPS = 8
CHUNK = 128
N_HEADS = 8
N_KV = 2
GQA = 4
KV_W = 128
ROT_DIM = 16
ROPE_THETA = 500000.0
CMP_BLOCK = 32
CMP_STRIDE = 16
SEL_BLOCK = 64
SEL_SHIFT = 6
N_SELECT = 16
WINDOW = 512
Q_BLOCK = 128
PAGE_SIZE = 128
NORM_EPS = 1e-6
FORCE_SCORE = 1e4
N_FORCED = 3
NEG = -1e30
LOG2_E = 1.4426950408889634

LANES = 128
VMEM_LIMIT_BYTES = 56 * 1024 * 1024

C_U, C_V, C_ZA, C_Q = 0, 512, 1024, 1536
C_KC, C_VC, C_KS, C_VS, C_KW, C_VW = 2048, 2176, 2304, 2432, 2560, 2688
C_G, C_ZB, C_END = 2816, 2944, 3456
N_GATE_COLS = 3 * N_HEADS

SEL_TILE = 512
DECODE_GROUP = 4
WIN_SPAN = WINDOW + Q_BLOCK

_NT = (((1,), (1,)), ((), ()))


def _dot(a, b):
    return jnp.dot(a, b, preferred_element_type=F32)


def _dot_nt(a, b):
    return lax.dot_general(a, b, _NT, preferred_element_type=F32)


def _lane_iota(shape):
    return lax.broadcasted_iota(jnp.int32, shape, len(shape) - 1)


def _row_iota(shape):
    return lax.broadcasted_iota(jnp.int32, shape, len(shape) - 2)


def _rope(x, tab):
    c = tab[:, 0:LANES]
    s1 = tab[:, LANES:2 * LANES]
    s2 = tab[:, 2 * LANES:3 * LANES]
    return x * c + pltpu.roll(x, LANES - ROT_DIM // 2, 1) * s1 + pltpu.roll(x, ROT_DIM // 2, 1) * s2


def _rms_rows(x, g):
    ms = jnp.mean(x * x, axis=-1, keepdims=True)
    return x * lax.rsqrt(ms + NORM_EPS) * g


def _layer_norm_rows(v, g, b):
    mu = jnp.mean(v, axis=-1, keepdims=True)
    vc = v - mu
    var = jnp.mean(vc * vc, axis=-1, keepdims=True)
    return vc * lax.rsqrt(var + NORM_EPS) * g + b


def _head_slabs(x512):
    out = []
    rows = x512.shape[0]
    lane = _lane_iota((rows, LANES))
    lo = lane < HEAD_DIM
    for j in range(4):
        slab = x512[:, j * LANES:(j + 1) * LANES]
        swapped = pltpu.roll(slab, HEAD_DIM, 1)
        if j < 2:
            out.append(jnp.where(lo, slab, 0.0))
            out.append(jnp.where(lo, swapped, 0.0))
        else:
            out.append(jnp.where(lo, 0.0, swapped))
            out.append(jnp.where(lo, 0.0, slab))
    return out


def _project(x_ref, g_ref, w_refs, z_scr):
    hb = _rms_rows(x_ref[...], g_ref[...]).astype(BF16)
    wm_ref, wg_ref, wz_ref = w_refs
    z_scr[:, 0:C_G] = _dot(hb, wm_ref[...])
    z_scr[:, C_G:C_ZB] = _dot(hb, wg_ref[...])
    z_scr[:, C_ZB:C_END] = _dot(hb, wz_ref[...])


def _inproj_prompt_kernel(x_ref, g_ref, wm_ref, wg_ref, wz_ref, lng_ref, lnb_ref, ws_ref, bias_ref, rope_ref,
                          aout_ref, qt_ref, kc_ref, vc_ref,
                          kct_ref, vct_ref, kst_ref, vst_ref, kwt_ref, vwt_ref,
                          ksa_ref, kwb_ref, vsb_ref, vwb_ref, gate_ref, szb_ref,
                          z_scr, kstage_scr, vstage_scr, *, tm, tiles_per_batch):
    _project(x_ref, g_ref, (wm_ref, wg_ref, wz_ref), z_scr)
    tab = rope_ref[...]

    lane = _lane_iota((CHUNK, LANES))
    lo = lane < HEAD_DIM
    for c in range(tm // CHUNK):
        r0 = c * CHUNK
        u = z_scr[r0:r0 + CHUNK, C_U:C_U + D_A]
        v = z_scr[r0:r0 + CHUNK, C_V:C_V + D_A]
        za = z_scr[r0:r0 + CHUNK, C_ZA:C_ZA + D_A]
        vn = _layer_norm_rows(v, lng_ref[...], lnb_ref[...])
        parts = []
        for p in range(A_GROUPS // 2):
            vp = vn[:, p * LANES:(p + 1) * LANES]
            v_lo = jnp.where(lo, vp, 0.0).astype(BF16)
            v_hi = jnp.where(lo, 0.0, vp).astype(BF16)
            parts.append(_dot(ws_ref[2 * p], v_lo) + _dot(ws_ref[2 * p + 1], v_hi))
        s = jnp.concatenate(parts, axis=1) + bias_ref[...]
        aout_ref[r0:r0 + CHUNK, :] = (u * s * jax.nn.silu(za)).astype(BF16)

    q = z_scr[:, C_Q:C_Q + D_B]
    qr = jnp.concatenate(
        [_rope(q[:, j * LANES:(j + 1) * LANES], tab) for j in range(4)], axis=1) * (HEAD_DIM ** -0.5 * LOG2_E)
    slabs = _head_slabs(qr)
    kc = z_scr[:, C_KC:C_KC + KV_W]
    vc = z_scr[:, C_VC:C_VC + KV_W]
    for src, stage, dst in ((kc, kstage_scr, kc_ref), (vc, vstage_scr, vc_ref)):
        stage[...] = src
        for r in range(CMP_STRIDE):
            dst[:, r * KV_W:(r + 1) * KV_W] = stage[pl.ds(r, tm // CMP_STRIDE, stride=CMP_STRIDE), :]
    ks = _rope(z_scr[:, C_KS:C_KS + KV_W], tab)
    vs = z_scr[:, C_VS:C_VS + KV_W]
    kw = _rope(z_scr[:, C_KW:C_KW + KV_W], tab)
    vw = z_scr[:, C_VW:C_VW + KV_W]
    gates = jax.nn.sigmoid(z_scr[:, C_G:C_G + LANES])
    pos = (pl.program_id(0) % tiles_per_batch) * tm + _row_iota((tm, LANES))
    onehot = jnp.where((pos >> SEL_SHIFT) == _lane_iota((tm, LANES)), 1.0, 0.0)
    ksa_ref[:, 0:LANES] = ks.astype(BF16)
    ksa_ref[:, LANES:2 * LANES] = onehot.astype(BF16)
    kwb_ref[...] = kw.astype(BF16)
    for j in range(tm // LANES):
        r0, r1 = j * LANES, (j + 1) * LANES
        kct_ref[0, :, r0:r1] = kc[r0:r1].T
        vct_ref[0, :, r0:r1] = vc[r0:r1].T
        kst_ref[0, :, r0:r1] = ks[r0:r1].T
        kwt_ref[0, :, r0:r1] = kw[r0:r1].T
        vs_t, vw_t = vs[r0:r1].T, vw[r0:r1].T
        vst_ref[0, :, r0:r1] = vs_t
        vwt_ref[0, :, r0:r1] = vw_t
        vsb_ref[0, j] = vs_t.astype(BF16)
        vwb_ref[0, j] = vw_t.astype(BF16)
        for e in range(N_HEADS):
            qt_ref[0, j, :, e * Q_BLOCK:(e + 1) * Q_BLOCK] = slabs[e][r0:r1].T.astype(BF16)
        gate_ref[0, j] = gates[r0:r1].T
    szb_ref[...] = jax.nn.silu(z_scr[:, C_ZB:C_ZB + D_B])


def _inproj_prompt(x2, g, w, lng, lnb, ws_bf, bias_full, rope_tab, seq, tm=512):
    n = x2.shape[0]
    tiles_per_batch = seq // tm
    bsz = n // seq
    n_tok_tiles = seq // LANES
    row = lambda i: (i, 0)
    const2 = lambda i: (0, 0)
    rows2d = lambda width, dtype: (jax.ShapeDtypeStruct((n, width), dtype), pl.BlockSpec((tm, width), row))
    kv_t = (jax.ShapeDtypeStruct((bsz, KV_W, seq), F32),
            pl.BlockSpec((1, KV_W, tm), lambda i: (i // tiles_per_batch, 0, i % tiles_per_batch)))
    chunk_rows = (jax.ShapeDtypeStruct((n // CMP_STRIDE, CMP_STRIDE * KV_W), F32),
                  pl.BlockSpec((tm // CMP_STRIDE, CMP_STRIDE * KV_W), row))
    tiles = lambda r, c, dtype: (
        jax.ShapeDtypeStruct((bsz, n_tok_tiles, r, c), dtype),
        pl.BlockSpec((1, tm // LANES, r, c), lambda i: (i // tiles_per_batch, i % tiles_per_batch, 0, 0)))
    outs = [
        rows2d(D_A, BF16),
        tiles(KV_W, N_HEADS * Q_BLOCK, BF16),
        chunk_rows, chunk_rows,
        kv_t, kv_t, kv_t, kv_t, kv_t, kv_t,
        rows2d(2 * KV_W, BF16),
        rows2d(KV_W, BF16),
        tiles(KV_W, LANES, BF16),
        tiles(KV_W, LANES, BF16),
        tiles(LANES, LANES, F32),
        rows2d(D_B, F32),
    ]
    out_shapes = [o[0] for o in outs]
    out_specs = [o[1] for o in outs]
    in_specs = [
        pl.BlockSpec((tm, D_MODEL), row),
        pl.BlockSpec((1, D_MODEL), const2),
        pl.BlockSpec((D_MODEL, C_G), const2),
        pl.BlockSpec((D_MODEL, C_ZB - C_G), const2),
        pl.BlockSpec((D_MODEL, C_END - C_ZB), const2),
        pl.BlockSpec((1, D_A), const2),
        pl.BlockSpec((1, D_A), const2),
        pl.BlockSpec((A_GROUPS, CHUNK, CHUNK), lambda i: (0, 0, 0)),
        pl.BlockSpec((CHUNK, D_A), const2),
        pl.BlockSpec((tm, 3 * LANES), lambda i: (i % tiles_per_batch, 0)),
    ]
    return pl.pallas_call(
        functools.partial(_inproj_prompt_kernel, tm=tm, tiles_per_batch=tiles_per_batch),
        grid=(n // tm,),
        in_specs=in_specs,
        out_specs=out_specs,
        out_shape=out_shapes,
        scratch_shapes=[pltpu.VMEM((tm, C_END), F32), pltpu.VMEM((tm, KV_W), F32), pltpu.VMEM((tm, KV_W), F32)],
        compiler_params=pltpu.CompilerParams(
            dimension_semantics=("arbitrary",), vmem_limit_bytes=VMEM_LIMIT_BYTES),
        name="inproj_prompt",
    )(x2, g, *w, lng, lnb, ws_bf, bias_full, rope_tab)


def _inproj_decode_kernel(x_ref, g_ref, wm_ref, wg_ref, wz_ref, lng_ref, lnb_ref, wsv_ref, bsv_ref, rope_ref,
                          aout_ref, qa_ref, kc_ref, vc_ref, ks_ref, vs_ref, kw_ref, vw_ref,
                          gate_ref, szb_ref, vn_ref, z_scr):
    _project(x_ref, g_ref, (wm_ref, wg_ref, wz_ref), z_scr)
    tab = rope_ref[...]
    u = z_scr[:, C_U:C_U + D_A]
    v = z_scr[:, C_V:C_V + D_A]
    za = z_scr[:, C_ZA:C_ZA + D_A]
    vn = _layer_norm_rows(v, lng_ref[...], lnb_ref[...])
    vn_ref[...] = vn
    s = vn * wsv_ref[...] + bsv_ref[...]
    aout_ref[...] = (u * s * jax.nn.silu(za)).astype(BF16)

    q = z_scr[:, C_Q:C_Q + D_B]
    qr = jnp.concatenate(
        [_rope(q[:, j * LANES:(j + 1) * LANES], tab) for j in range(4)], axis=1) * (HEAD_DIM ** -0.5)
    for e, slab in enumerate(_head_slabs(qr)):
        qa_ref[:, e * LANES:(e + 1) * LANES] = slab.astype(BF16)
    kc_ref[...] = z_scr[:, C_KC:C_KC + KV_W]
    vc_ref[...] = z_scr[:, C_VC:C_VC + KV_W]
    ks_ref[...] = _rope(z_scr[:, C_KS:C_KS + KV_W], tab)
    vs_ref[...] = z_scr[:, C_VS:C_VS + KV_W]
    kw_ref[...] = _rope(z_scr[:, C_KW:C_KW + KV_W], tab)
    vw_ref[...] = z_scr[:, C_VW:C_VW + KV_W]
    gate_ref[...] = jax.nn.sigmoid(z_scr[:, C_G:C_G + LANES])
    szb = jax.nn.silu(z_scr[:, C_ZB:C_ZB + D_B])
    for e, slab in enumerate(_head_slabs(szb)):
        szb_ref[:, e * LANES:(e + 1) * LANES] = slab


def _inproj_decode(x2, g, w, lng, lnb, wsv, bsv, rope_tab):
    n = x2.shape[0]
    out_shapes = [
        jax.ShapeDtypeStruct((n, D_A), BF16),
        jax.ShapeDtypeStruct((n, N_HEADS * LANES), BF16),
        jax.ShapeDtypeStruct((n, KV_W), F32),
        jax.ShapeDtypeStruct((n, KV_W), F32),
        jax.ShapeDtypeStruct((n, KV_W), F32),
        jax.ShapeDtypeStruct((n, KV_W), F32),
        jax.ShapeDtypeStruct((n, KV_W), F32),
        jax.ShapeDtypeStruct((n, KV_W), F32),
        jax.ShapeDtypeStruct((n, LANES), F32),
        jax.ShapeDtypeStruct((n, N_HEADS * LANES), F32),
        jax.ShapeDtypeStruct((n, D_A), F32),
    ]
    return pl.pallas_call(
        _inproj_decode_kernel,
        out_shape=out_shapes,
        scratch_shapes=[pltpu.VMEM((n, C_END), F32)],
        compiler_params=pltpu.CompilerParams(vmem_limit_bytes=VMEM_LIMIT_BYTES),
        name="inproj_decode",
    )(x2, g, *w, lng, lnb, wsv, bsv, rope_tab)


def _compress_chunks(ch, pos_term, w_ref):
    ab = _dot(ch.astype(BF16), w_ref[...])
    return ab[:, 0:KV_W] + pltpu.roll(ab[:, KV_W:2 * KV_W], ch.shape[0] - 1, 0) + pos_term


def _position_term(pe_ref, w_ref):
    pe = pe_ref[...]
    hi = pe.astype(BF16)
    r1 = pe - hi.astype(F32)
    mid = r1.astype(BF16)
    lo = (r1 - mid.astype(F32)).astype(BF16)
    pw = _dot(hi, w_ref[...]) + _dot(mid, w_ref[...]) + _dot(lo, w_ref[...])
    return pw[0:1, 0:KV_W] + pw[1:2, KV_W:2 * KV_W]


def _compress_prompt_kernel(kc_ref, vc_ref, wk_ref, wv_ref, pek_ref, pev_ref, rope_ref, kcc_ref, vcct_ref):
    kcc = _compress_chunks(kc_ref[0], _position_term(pek_ref, wk_ref), wk_ref)
    kcc_ref[0] = _rope(kcc, rope_ref[...]).astype(BF16)
    vcc = _compress_chunks(vc_ref[0], _position_term(pev_ref, wv_ref), wv_ref)
    for j in range(vcc.shape[0] // LANES):
        vcct_ref[0, :, j * LANES:(j + 1) * LANES] = vcc[j * LANES:(j + 1) * LANES].T.astype(BF16)


def _compress_prompt(kc3, vc3, wk, wv, pek, pev, rope_tab):
    b, c, width = kc3.shape
    blk = pl.BlockSpec((1, c, width), lambda i: (i, 0, 0))
    const2 = lambda i: (0, 0)
    return pl.pallas_call(
        _compress_prompt_kernel,
        grid=(b,),
        in_specs=[blk, blk,
                  pl.BlockSpec(wk.shape, const2), pl.BlockSpec(wv.shape, const2),
                  pl.BlockSpec(pek.shape, const2), pl.BlockSpec(pev.shape, const2),
                  pl.BlockSpec(rope_tab.shape, const2)],
        out_specs=[pl.BlockSpec((1, c, KV_W), lambda i: (i, 0, 0)),
                   pl.BlockSpec((1, KV_W, c), lambda i: (i, 0, 0))],
        out_shape=[jax.ShapeDtypeStruct((b, c, KV_W), BF16), jax.ShapeDtypeStruct((b, KV_W, c), BF16)],
        compiler_params=pltpu.CompilerParams(
            dimension_semantics=("arbitrary",), vmem_limit_bytes=VMEM_LIMIT_BYTES),
        name="compress_prompt",
    )(kc3, vc3, wk, wv, pek, pev, rope_tab)


def _select_blocks(imp, kk, candidate, forced):
    assert FORCE_SCORE > 2 * GQA
    start = jnp.where(candidate, jnp.where(forced, -jnp.inf, imp), -jnp.inf)
    score = start
    for _ in range(N_SELECT - N_FORCED):
        cm = jnp.max(score, axis=0, keepdims=True)
        first = jnp.min(jnp.where(score == cm, kk, LANES), axis=0, keepdims=True)
        score = jnp.where(kk == first, -jnp.inf, score)
    taken = (score == -jnp.inf) & (start > -jnp.inf)
    return candidate & (forced | taken)


def _softmax_rows(s):
    mx = jnp.max(s, axis=1, keepdims=True)
    ex = jnp.exp(s - mx)
    return ex * (1.0 / jnp.sum(ex, axis=1, keepdims=True))


def _tile_heads(x):
    return jnp.concatenate([x] * N_HEADS, axis=1)


def _values_by_kv_head(vt, p):
    half = GQA * Q_BLOCK
    return jnp.concatenate(
        [_dot(vt[h * HEAD_DIM:(h + 1) * HEAD_DIM], p[:, h * half:(h + 1) * half]) for h in range(N_KV)], axis=1)


def _attn_prompt_kernel(qt_ref, gate_ref, szb_ref, kcc_ref, vcct_ref, ovlt_ref,
                        ksa_ref, vsb_ref, kwb_ref, vwb_ref, out_ref,
                        qaug_scr, acc_scr, sa_scr, sb_scr):
    i_blk = pl.program_id(1)
    q0 = i_blk * Q_BLOCK
    cols = N_HEADS * Q_BLOCK
    qt = qt_ref[0, 0]

    n_cmp = kcc_ref.shape[1]

    def compressed(extent):
        t_c = q0 + _lane_iota((extent, Q_BLOCK))
        seen = (_row_iota((extent, Q_BLOCK)) * CMP_STRIDE + (CMP_BLOCK - 1)) <= t_c
        s = _dot(kcc_ref[0, 0:extent, :], qt) + _tile_heads(jnp.where(seen, 0.0, NEG))
        mx = jnp.max(s, axis=0, keepdims=True)
        ex = jnp.exp2(s - mx)
        inv = jnp.where(mx > 0.5 * NEG, 1.0 / jnp.sum(ex, axis=0, keepdims=True), 0.0)
        p_c = ex * inv
        imps = []
        for h in range(N_KV):
            c0 = h * GQA * Q_BLOCK
            p_sum = (p_c[:, c0:c0 + Q_BLOCK] + p_c[:, c0 + Q_BLOCK:c0 + 2 * Q_BLOCK]
                     + p_c[:, c0 + 2 * Q_BLOCK:c0 + 3 * Q_BLOCK] + p_c[:, c0 + 3 * Q_BLOCK:c0 + 4 * Q_BLOCK])
            imps.append(_dot(ovlt_ref[:, 0:extent], p_sum.astype(BF16)))
        return (_values_by_kv_head(vcct_ref[0, :, 0:extent], p_c.astype(BF16)), *imps)

    n_need = (q0 + Q_BLOCK) // CMP_STRIDE
    extents = list(range(LANES, n_cmp + 1, LANES))
    o_c, *imp_t = lax.switch(
        jnp.minimum((n_need - 1) // LANES, len(extents) - 1),
        [functools.partial(compressed, ext) for ext in extents])

    w0 = pl.multiple_of(jnp.maximum(q0 - WINDOW, 0), Q_BLOCK)
    wj0 = jnp.maximum(i_blk - WINDOW // Q_BLOCK, 0)
    kwin = kwb_ref[0, pl.ds(w0, WIN_SPAN), :]
    vwt = jnp.concatenate([vwb_ref[0, wj0 + i] for i in range(WIN_SPAN // LANES)], axis=1)
    kp = w0 + _row_iota((WIN_SPAN, Q_BLOCK))
    tw = q0 + _lane_iota((WIN_SPAN, Q_BLOCK))
    in_win = jnp.where(kp <= tw, jnp.where(kp > tw - WINDOW, 0.0, NEG), NEG)
    sw = _dot(kwin, qt) + _tile_heads(in_win)
    exw = jnp.exp2(sw - jnp.max(sw, axis=0, keepdims=True))
    o_w = _values_by_kv_head(vwt, exw.astype(BF16)) * (1.0 / jnp.sum(exw, axis=0, keepdims=True))

    kk = _row_iota((LANES, Q_BLOCK))
    tq = q0 + _lane_iota((LANES, Q_BLOCK))
    valid = kk * SEL_BLOCK <= tq
    cur = tq >> SEL_SHIFT
    forced = (kk == 0) | (kk == cur) | (kk == cur - 1)
    both = lambda a: jnp.concatenate([a] * N_KV, axis=1)
    picked = _select_blocks(jnp.concatenate(imp_t, axis=1), both(kk), both(valid), both(forced))
    qaug_scr[0:LANES, :] = qt
    for h in range(N_KV):
        selneg = jnp.where(picked[:, h * Q_BLOCK:(h + 1) * Q_BLOCK], 0.0, NEG).astype(BF16)
        for g in range(GQA):
            c0 = (h * GQA + g) * Q_BLOCK
            qaug_scr[LANES:2 * LANES, c0:c0 + Q_BLOCK] = selneg

    acc_scr[...] = jnp.zeros(acc_scr.shape, F32)
    tiles_per_step = SEL_TILE // LANES

    def score(step, dst):
        t0 = pl.multiple_of(step * SEL_TILE, SEL_TILE)
        dst[...] = _dot(ksa_ref[0, pl.ds(t0, SEL_TILE), :], qaug_scr[...])

    def attend(src, step, m_prev, l_prev, causal_rows=None):
        rows = SEL_TILE if causal_rows is None else causal_rows
        vt = jnp.concatenate([vsb_ref[0, step * tiles_per_step + i] for i in range(rows // LANES)], axis=1)
        sc = src[0:rows, :]
        if causal_rows is not None:
            tok = step * SEL_TILE + _row_iota((rows, Q_BLOCK))
            sc = sc + _tile_heads(jnp.where(tok <= q0 + _lane_iota((rows, Q_BLOCK)), 0.0, NEG))
        m_next = jnp.maximum(m_prev, jnp.max(sc, axis=0, keepdims=True))
        p = jnp.exp2(sc - m_next)
        alpha = jnp.exp2(m_prev - m_next)
        l_next = alpha * l_prev + jnp.sum(p, axis=0, keepdims=True)
        acc_scr[...] = alpha * acc_scr[...] + _values_by_kv_head(vt, p.astype(BF16))
        return m_next, l_next

    n_full = q0 // SEL_TILE
    score(0, sa_scr)

    def pair(jj, carry):
        m_c, l_c = carry
        score(2 * jj + 1, sb_scr)
        m_c, l_c = attend(sa_scr, 2 * jj, m_c, l_c)
        score(2 * jj + 2, sa_scr)
        return attend(sb_scr, 2 * jj + 1, m_c, l_c)

    m_run, l_run = lax.fori_loop(
        0, n_full // 2, pair,
        (jnp.full((1, cols), -jnp.inf, F32), jnp.zeros((1, cols), F32)))

    def last_step(src, m_c, l_c):
        extents = list(range(Q_BLOCK, SEL_TILE + 1, Q_BLOCK))
        return lax.switch(
            (q0 % SEL_TILE) // Q_BLOCK,
            [lambda m, l, ext=ext: attend(src, n_full, m, l, causal_rows=ext)[1] for ext in extents],
            m_c, l_c)

    def odd_tail(m_c, l_c):
        score(n_full, sb_scr)
        m_c, l_c = attend(sa_scr, n_full - 1, m_c, l_c)
        return last_step(sb_scr, m_c, l_c)

    def even_tail(m_c, l_c):
        return last_step(sa_scr, m_c, l_c)

    l_run = lax.cond(n_full % 2 == 1, odd_tail, even_tail, m_run, l_run)

    gt = gate_ref[0, 0]
    o_s = acc_scr[...] * (1.0 / l_run)
    heads = []
    for e in range(N_HEADS):
        c0 = e * Q_BLOCK
        heads.append(gt[3 * e:3 * e + 1, :] * o_c[:, c0:c0 + Q_BLOCK]
                     + gt[3 * e + 1:3 * e + 2, :] * o_s[:, c0:c0 + Q_BLOCK]
                     + gt[3 * e + 2:3 * e + 3, :] * o_w[:, c0:c0 + Q_BLOCK])
    for j in range(N_HEADS // 2):
        slab = jnp.concatenate([heads[2 * j], heads[2 * j + 1]], axis=0).T
        out_ref[0, :, j * LANES:(j + 1) * LANES] = (
            slab * szb_ref[0, :, j * LANES:(j + 1) * LANES]).astype(BF16)


def _attn_prompt(qt4, gate4, szb3, kcc, vcct, ovlt, ksa3, vsb4, kwb3, vwb4):
    b, nq = qt4.shape[0], qt4.shape[1]
    t = nq * Q_BLOCK
    cols = N_HEADS * Q_BLOCK
    qtile = lambda a: pl.BlockSpec((1, 1) + a.shape[2:], lambda bi, i: (bi, i, 0, 0))
    seq = lambda a: pl.BlockSpec((1,) + a.shape[1:], lambda bi, i, nd=a.ndim: (bi,) + (0,) * (nd - 1))
    return pl.pallas_call(
        _attn_prompt_kernel,
        grid=(b, nq),
        in_specs=[qtile(qt4), qtile(gate4), pl.BlockSpec((1, Q_BLOCK, D_B), lambda bi, i: (bi, i, 0)),
                  seq(kcc), seq(vcct), pl.BlockSpec(ovlt.shape, lambda bi, i: (0, 0)),
                  seq(ksa3), seq(vsb4), seq(kwb3), seq(vwb4)],
        out_specs=pl.BlockSpec((1, Q_BLOCK, D_B), lambda bi, i: (bi, i, 0)),
        out_shape=jax.ShapeDtypeStruct((b, t, D_B), BF16),
        scratch_shapes=[pltpu.VMEM((2 * LANES, cols), BF16),
                        pltpu.VMEM((HEAD_DIM, cols), F32),
                        pltpu.VMEM((SEL_TILE, cols), F32),
                        pltpu.VMEM((SEL_TILE, cols), F32)],
        compiler_params=pltpu.CompilerParams(
            dimension_semantics=("arbitrary", "arbitrary"), vmem_limit_bytes=VMEM_LIMIT_BYTES),
        name="attn_prompt",
    )(qt4, gate4, szb3, kcc, vcct, ovlt, ksa3, vsb4, kwb3, vwb4)


def _outproj_kernel(x_ref, a_ref, m_ref, wa_ref, wb_ref, fg_ref, y_ref):
    y = x_ref[...] + _dot(a_ref[...], wa_ref[...]) + _dot(m_ref[...].astype(BF16), wb_ref[...])
    y_ref[...] = _rms_rows(y, fg_ref[...])


def _outproj(x2, a_out, mix_b, wa, wb, fg, tm):
    n = x2.shape[0]
    row = lambda i: (i, 0)
    const2 = lambda i: (0, 0)
    return pl.pallas_call(
        _outproj_kernel,
        grid=(n // tm,),
        in_specs=[pl.BlockSpec((tm, D_MODEL), row),
                  pl.BlockSpec((tm, a_out.shape[1]), row),
                  pl.BlockSpec((tm, mix_b.shape[1]), row),
                  pl.BlockSpec(wa.shape, const2), pl.BlockSpec(wb.shape, const2),
                  pl.BlockSpec((1, D_MODEL), const2)],
        out_specs=pl.BlockSpec((tm, D_MODEL), row),
        out_shape=jax.ShapeDtypeStruct((n, D_MODEL), F32),
        compiler_params=pltpu.CompilerParams(
            dimension_semantics=("arbitrary",), vmem_limit_bytes=VMEM_LIMIT_BYTES),
        name="outproj",
    )(x2, a_out, mix_b, wa, wb, fg)


def _page_copies(pt_ref, pool_ref, buf_ref, sem_ref, step, slot):
    group, n_pages = buf_ref.shape[1], buf_ref.shape[2]
    return [pltpu.make_async_copy(pool_ref.at[pt_ref[step * group + g, p]], buf_ref.at[slot, g, p],
                                  sem_ref.at[slot])
            for g in range(group) for p in range(n_pages)]


def _gather_pages(pt_ref, pools, bufs, sems):
    b = pl.program_id(0)
    slot = lax.rem(b, 2)

    def start(step, sl):
        for pool, buf, sem in zip(pools, bufs, sems):
            for cp in _page_copies(pt_ref, pool, buf, sem, step, sl):
                cp.start()

    @pl.when(b == 0)
    def _():
        start(0, 0)

    @pl.when(b + 1 < pl.num_programs(0))
    def _():
        start(b + 1, 1 - slot)

    for pool, buf, sem in zip(pools, bufs, sems):
        for cp in _page_copies(pt_ref, pool, buf, sem, b, slot):
            cp.wait()
    return slot


def _decode_cmp_kernel(pt_ref, ck_hbm, cv_hbm, q_ref, wk_ref, wv_ref, pek_ref, pev_ref, rope_ref, ovl_ref,
                       oc_ref, psum_ref, kbuf, vbuf, ksem, vsem, ktok_scr, vtok_scr, kpos_scr, vpos_scr):
    slot = _gather_pages(pt_ref, (ck_hbm, cv_hbm), (kbuf, vbuf), (ksem, vsem))
    group, n_pages = kbuf.shape[1], kbuf.shape[2]

    @pl.when(pl.program_id(0) == 0)
    def _():
        kpos_scr[...] = _position_term(pek_ref, wk_ref)
        vpos_scr[...] = _position_term(pev_ref, wv_ref)

    def chunk_rows(buf, tok_scr, g):
        for p in range(n_pages):
            tok_scr[g, p * PAGE_SIZE:(p + 1) * PAGE_SIZE, :] = buf[slot, g, p].T
        n = tok_scr.shape[1] // CMP_STRIDE
        return jnp.concatenate(
            [tok_scr[g, pl.ds(r, n, stride=CMP_STRIDE), :] for r in range(CMP_STRIDE)], axis=1)

    for g in range(group):
        kch = chunk_rows(kbuf, ktok_scr, g)
        vch = chunk_rows(vbuf, vtok_scr, g)
        kcc = _rope(_compress_chunks(kch, kpos_scr[...], wk_ref), rope_ref[...]).astype(BF16)
        vcc = _compress_chunks(vch, vpos_scr[...], wv_ref).astype(BF16)
        n_chunks = kch.shape[0]
        q = q_ref[g]
        rows = q.shape[0]
        s = _dot_nt(q, kcc)
        seen = _lane_iota((rows, n_chunks)) < n_chunks - 1
        p_c = jnp.where(seen, _softmax_rows(jnp.where(seen, s, NEG)), 0.0)
        oc_ref[g] = _dot(p_c.astype(BF16), vcc)[0:N_HEADS]
        ps0 = jnp.sum(p_c[0:GQA], axis=0, keepdims=True)
        ps1 = jnp.sum(p_c[GQA:2 * GQA], axis=0, keepdims=True)
        psum = jnp.where(_row_iota((rows, n_chunks)) < N_HEADS // 2, ps0, ps1)
        psum_ref[g] = _dot(psum.astype(BF16), ovl_ref[...])[0:N_HEADS]


def _decode_cmp(page_table, ck_pool, cv_pool, q16, wk, wv, pek, pev, rope_tab, ovl):
    nb, n_pages = page_table.shape
    group = DECODE_GROUP
    assert nb % group == 0
    hbm = pl.BlockSpec(memory_space=pl.ANY)
    page_buf = pltpu.VMEM((2, group, n_pages, KV_W, PAGE_SIZE), F32)
    tok_buf = pltpu.VMEM((group, n_pages * PAGE_SIZE, KV_W), F32)
    const2 = lambda b, pt: (0, 0)
    per_b = lambda a: pl.BlockSpec((group,) + a.shape[1:], lambda b, pt: (b, 0, 0))
    in_specs = [hbm, hbm, per_b(q16),
                pl.BlockSpec(wk.shape, const2), pl.BlockSpec(wv.shape, const2),
                pl.BlockSpec(pek.shape, const2), pl.BlockSpec(pev.shape, const2),
                pl.BlockSpec(rope_tab.shape, const2), pl.BlockSpec(ovl.shape, const2)]
    out_blk = pl.BlockSpec((group, N_HEADS, LANES), lambda b, pt: (b, 0, 0))
    grid_spec = pltpu.PrefetchScalarGridSpec(
        num_scalar_prefetch=1, grid=(nb // group,), in_specs=in_specs, out_specs=[out_blk, out_blk],
        scratch_shapes=[page_buf, page_buf, pltpu.SemaphoreType.DMA((2,)), pltpu.SemaphoreType.DMA((2,)),
                        tok_buf, tok_buf, pltpu.VMEM((1, KV_W), F32), pltpu.VMEM((1, KV_W), F32)])
    return pl.pallas_call(
        _decode_cmp_kernel,
        grid_spec=grid_spec,
        out_shape=[jax.ShapeDtypeStruct((nb, N_HEADS, LANES), F32)] * 2,
        compiler_params=pltpu.CompilerParams(
            dimension_semantics=("arbitrary",), vmem_limit_bytes=VMEM_LIMIT_BYTES),
        name="decode_cmp",
    )(page_table, ck_pool, cv_pool, q16, wk, wv, pek, pev, rope_tab, ovl)


def _decode_topk_kernel(imp_ref, sel_ref, *, n_blk, cur):
    n = imp_ref.shape[0]
    tiles = [imp_ref[i * LANES:(i + 1) * LANES, :].T for i in range(n // LANES)]
    imp_t = jnp.concatenate(tiles, axis=1)
    kk = _row_iota(imp_t.shape)
    forced = (kk == 0) | (kk == cur) | (kk == cur - 1)
    selneg = jnp.where(_select_blocks(imp_t, kk, kk < n_blk, forced), 0.0, NEG)
    for i in range(n // LANES):
        sel_ref[i * LANES:(i + 1) * LANES, :] = selneg[:, i * LANES:(i + 1) * LANES].T


def _decode_topk(imp2, n_blk, cur):
    return pl.pallas_call(
        functools.partial(_decode_topk_kernel, n_blk=n_blk, cur=cur),
        out_shape=jax.ShapeDtypeStruct(imp2.shape, F32),
        compiler_params=pltpu.CompilerParams(vmem_limit_bytes=VMEM_LIMIT_BYTES),
        name="decode_topk",
    )(imp2)


def _decode_attend(s, q32, vals_t_bf, k_new, v_new, bias_new):
    kn = k_new.astype(BF16).astype(F32)
    s_new = jnp.sum(q32 * kn, axis=1, keepdims=True) + bias_new
    mx = jnp.maximum(jnp.max(s, axis=1, keepdims=True), s_new)
    ex = jnp.exp(s - mx)
    ex_new = jnp.exp(s_new - mx)
    inv = 1.0 / (jnp.sum(ex, axis=1, keepdims=True) + ex_new)
    p = (ex * inv).astype(BF16)
    p_new = (ex_new * inv).astype(BF16).astype(F32)
    return _dot_nt(p, vals_t_bf) + p_new * v_new.astype(BF16).astype(F32)


P_Q, P_SEL, P_OC, P_SZB, P_NEW, P_GATE, P_ROWS = 0, 16, 32, 40, 48, 56, 64


def _row_to_col(row):
    n = row.shape[1]
    diag = _row_iota((n, n)) == _lane_iota((n, n))
    return jnp.sum(jnp.where(diag, row, 0.0), axis=1, keepdims=True)


def _decode_sel_kernel(pt_ref, ks_hbm, vs_hbm, ckw_ref, cvw_ref, pack_ref, onehot_ref,
                       mix_ref, okw_ref, ovw_ref, kbuf, vbuf, ksem, vsem, *, cur):
    slot = _gather_pages(pt_ref, (ks_hbm, vs_hbm), (kbuf, vbuf), (ksem, vsem))
    group, n_pages = kbuf.shape[1], kbuf.shape[2]
    for g in range(group):
        pk = pack_ref[g]
        q32 = pk[P_Q:P_Q + 2 * N_HEADS]
        q = q32.astype(BF16)
        sel = pk[P_SEL:P_SEL + 2 * N_HEADS].astype(BF16)
        rows = q.shape[0]
        ks_new, vs_new = pk[P_NEW:P_NEW + 1], pk[P_NEW + 1:P_NEW + 2]
        kw_new, vw_new = pk[P_NEW + 2:P_NEW + 3], pk[P_NEW + 3:P_NEW + 4]

        kt = jnp.concatenate([kbuf[slot, g, p] for p in range(n_pages)], axis=1).astype(BF16)
        vt = jnp.concatenate([vbuf[slot, g, p] for p in range(n_pages)], axis=1).astype(BF16)
        kaug = jnp.concatenate([kt, onehot_ref[...]], axis=0)
        qaug = jnp.concatenate([q, sel], axis=1)
        s_sel = _dot(qaug, kaug)
        o_s = _decode_attend(s_sel, q32, vt, ks_new, vs_new, sel.astype(F32)[:, cur:cur + 1])

        kwin = ckw_ref[g]
        vwin = cvw_ref[g]
        keep = kwin.shape[1]
        visible = _lane_iota((rows, keep)) > keep - WINDOW
        s_win = jnp.where(visible, _dot(q, kwin.astype(BF16)), NEG)
        o_w = _decode_attend(s_win, q32, vwin.astype(BF16), kw_new, vw_new, 0.0)

        gates = pk[P_GATE:P_GATE + N_HEADS]
        o = (gates[:, 0:1] * pk[P_OC:P_OC + N_HEADS] + gates[:, 1:2] * o_s[0:N_HEADS]
             + gates[:, 2:3] * o_w[0:N_HEADS])
        mix_ref[g] = o * pk[P_SZB:P_SZB + N_HEADS]

        last = _lane_iota((KV_W, keep)) == keep - 1
        okw_ref[g] = jnp.where(last, _row_to_col(kw_new), pltpu.roll(kwin, keep - 1, 1))
        ovw_ref[g] = jnp.where(last, _row_to_col(vw_new), pltpu.roll(vwin, keep - 1, 1))


def _decode_sel(page_table, ks_pool, vs_pool, ckw, cvw, pack, onehot, cur):
    nb, n_pages = page_table.shape
    group = DECODE_GROUP
    assert nb % group == 0
    keep = ckw.shape[2]
    hbm = pl.BlockSpec(memory_space=pl.ANY)
    page_buf = pltpu.VMEM((2, group, n_pages, KV_W, PAGE_SIZE), F32)
    per_b = lambda a: pl.BlockSpec((group,) + a.shape[1:], lambda b, pt: (b, 0, 0))
    in_specs = [hbm, hbm, per_b(ckw), per_b(cvw), per_b(pack),
                pl.BlockSpec(onehot.shape, lambda b, pt: (0, 0))]
    out_specs = [pl.BlockSpec((group, N_HEADS, LANES), lambda b, pt: (b, 0, 0)),
                 pl.BlockSpec((group, KV_W, keep), lambda b, pt: (b, 0, 0)),
                 pl.BlockSpec((group, KV_W, keep), lambda b, pt: (b, 0, 0))]
    grid_spec = pltpu.PrefetchScalarGridSpec(
        num_scalar_prefetch=1, grid=(nb // group,), in_specs=in_specs, out_specs=out_specs,
        scratch_shapes=[page_buf, page_buf, pltpu.SemaphoreType.DMA((2,)), pltpu.SemaphoreType.DMA((2,))])
    return pl.pallas_call(
        functools.partial(_decode_sel_kernel, cur=cur),
        grid_spec=grid_spec,
        out_shape=[jax.ShapeDtypeStruct((nb, N_HEADS, LANES), F32),
                   jax.ShapeDtypeStruct((nb, KV_W, keep), F32),
                   jax.ShapeDtypeStruct((nb, KV_W, keep), F32)],
        compiler_params=pltpu.CompilerParams(
            dimension_semantics=("arbitrary",), vmem_limit_bytes=VMEM_LIMIT_BYTES),
        name="decode_sel",
    )(page_table, ks_pool, vs_pool, ckw, cvw, pack, onehot)


def _rope_table(pos):
    half = ROT_DIM // 2
    inv = jnp.power(jnp.float32(ROPE_THETA), -jnp.arange(half, dtype=F32) / half)
    ang = pos.astype(F32)[:, None] * inv[None, :]
    cs = jnp.concatenate([jnp.cos(ang), jnp.sin(ang)], axis=1)
    lane = jnp.arange(3 * LANES)
    seg, l64 = lane // LANES, lane % HEAD_DIM
    src = jnp.where(seg == 0, l64 % half, half + l64 % half)
    sign = jnp.where(seg == 0, l64 < ROT_DIM,
                     jnp.where(seg == 1, l64 < half, (l64 >= half) & (l64 < ROT_DIM))).astype(F32)
    sign = jnp.where(seg == 1, -sign, sign)
    place = (jnp.arange(ROT_DIM)[:, None] == src[None, :]).astype(F32) * sign[None, :]
    ones = ((seg == 0) & (l64 >= ROT_DIM)).astype(F32)
    return jnp.dot(cs, place, precision=lax.Precision.HIGHEST) + ones[None, :]


def _compress_weights(w_c, pe_c):
    w4 = w_c.reshape(2, CMP_STRIDE, HEAD_DIM, HEAD_DIM)
    wb = jnp.einsum('arde,hg->rhdage', w4, jnp.eye(N_KV, dtype=w_c.dtype))
    wb = wb.reshape(CMP_STRIDE * KV_W, 2 * KV_W).astype(BF16)
    pe = jnp.broadcast_to(pe_c.reshape(2, CMP_STRIDE, 1, HEAD_DIM), (2, CMP_STRIDE, N_KV, HEAD_DIM))
    return wb, jnp.pad(pe.reshape(2, CMP_STRIDE * KV_W), ((0, 14), (0, 0)))


def _overlap_matrix(n_rows, n_blk):
    cs = jnp.arange(n_rows, dtype=jnp.int32)[:, None] * CMP_STRIDE
    bs = jnp.arange(LANES, dtype=jnp.int32)[None, :] * SEL_BLOCK
    hit = (cs < bs + SEL_BLOCK) & (cs + CMP_BLOCK > bs) & (jnp.arange(LANES)[None, :] < n_blk)
    return hit.astype(BF16)


def kernel(x_prompt, x_sample, cache_k_cmp, cache_v_cmp, cache_k_sel, cache_v_sel, cache_k_win,
           cache_v_win, page_table, norm_g, w_in, ln_v_g, ln_v_b, w_s, b_s, w_ck, pe_ck, w_cv,
           pe_cv, w_out, final_g):
    depth = norm_g.shape[0]
    assert depth == 1
    bsz, seq, _ = x_prompt.shape
    nb, dec_seq, _ = x_sample.shape
    assert dec_seq == 1 and seq % SEL_TILE == 0 and seq >= WIN_SPAN and (seq // CMP_STRIDE) % LANES == 0
    n_pages = page_table.shape[1]
    past = n_pages * PAGE_SIZE
    keep = cache_k_win.shape[2]
    assert keep == WINDOW and past % SEL_BLOCK == 0

    w = w_in[0]
    w_pad = (w[:, :C_G].astype(BF16),
             jnp.pad(w[:, C_G:C_G + N_GATE_COLS], ((0, 0), (0, LANES - N_GATE_COLS))).astype(BF16),
             w[:, C_G + N_GATE_COLS:].astype(BF16))
    g_row = norm_g[0][None, :]
    lng = ln_v_g[0][None, :]
    lnb = ln_v_b[0][None, :]
    fg = final_g[None, :]
    wk_c, pek = _compress_weights(w_ck[0], pe_ck[0])
    wv_c, pev = _compress_weights(w_cv[0], pe_cv[0])
    w_o = w_out[0].astype(BF16)
    wo_a, wo_b = w_o[:D_A], w_o[D_A:]

    tril = jnp.tril(jnp.ones((CHUNK, CHUNK), w_s.dtype))
    ws_bf = (w_s[0] * tril).astype(BF16)
    bias_full = jnp.repeat(b_s[0].T, HEAD_DIM, axis=1)
    xp2 = x_prompt.reshape(bsz * seq, D_MODEL)
    (a_out, qt, kc, vc, kc_t, vc_t, ks_t, vs_t, kw_t, vw_t, ksa, kwb, vsb, vwb, gates_t, szb) = _inproj_prompt(
        xp2, g_row, w_pad, lng, lnb, ws_bf, bias_full,
        _rope_table(jnp.arange(seq, dtype=jnp.int32)), seq)
    n_chunks = seq // CMP_STRIDE
    kcc, vcc_t = _compress_prompt(
        kc.reshape(bsz, n_chunks, CMP_STRIDE * KV_W), vc.reshape(bsz, n_chunks, CMP_STRIDE * KV_W),
        wk_c, wv_c, pek, pev,
        _rope_table(jnp.arange(n_chunks, dtype=jnp.int32) * CMP_STRIDE))
    b3 = lambda a: a.reshape(bsz, seq, a.shape[-1])
    mix_p = _attn_prompt(qt, gates_t, b3(szb), kcc, vcc_t,
                         _overlap_matrix(n_chunks, seq // SEL_BLOCK).T,
                         b3(ksa), vsb, b3(kwb), vwb)
    y_prompt = _outproj(xp2, a_out, mix_p.reshape(bsz * seq, D_B), wo_a, wo_b, fg, 512)
    y_prompt = y_prompt.reshape(bsz, seq, D_MODEL)
    st = lambda a: jnp.transpose(a.reshape(a.shape[0], N_KV, HEAD_DIM, a.shape[2]), (0, 3, 1, 2))[None]
    keep_p = min(WINDOW, seq)

    xs2 = x_sample.reshape(nb, D_MODEL)
    wsv = jnp.repeat(w_s[0][:, 0, 0], HEAD_DIM)[None, :]
    bsv = jnp.repeat(b_s[0][:, 0], HEAD_DIM)[None, :]
    (a_s, qa_s, kc_s, vc_s, ks_s, vs_s, kw_s, vw_s, gates_s, szb_s, vn_s) = _inproj_decode(
        xs2, g_row, w_pad, lng, lnb, wsv, bsv,
        _rope_table(jnp.full((nb,), past, dtype=jnp.int32)))
    q16 = jnp.pad(qa_s.reshape(nb, N_HEADS, LANES), ((0, 0), (0, N_HEADS), (0, 0)))
    d_chunks = past // CMP_STRIDE
    n_blk = -(-(past + 1) // SEL_BLOCK)
    cur = past // SEL_BLOCK
    kv_t = lambda c: jnp.transpose(c[0], (0, 2, 3, 1)).reshape(c.shape[1], KV_W, c.shape[2])
    oc_s, imp_s = _decode_cmp(
        page_table, kv_t(cache_k_cmp), kv_t(cache_v_cmp),
        q16, wk_c, wv_c, pek, pev,
        _rope_table(jnp.arange(d_chunks, dtype=jnp.int32) * CMP_STRIDE),
        _overlap_matrix(d_chunks, n_blk))
    selneg = _decode_topk(imp_s[:, ::GQA, :].reshape(nb * N_KV, LANES), n_blk, cur)
    sel16 = jnp.pad(jnp.repeat(selneg.reshape(nb, N_KV, LANES), GQA, axis=1), ((0, 0), (0, N_HEADS), (0, 0)))
    tok_blk = jnp.arange(past, dtype=jnp.int32)[None, :] // SEL_BLOCK
    onehot = (tok_blk == jnp.arange(LANES, dtype=jnp.int32)[:, None]).astype(BF16)
    gate_rows = jnp.pad(gates_s[:, :N_GATE_COLS].reshape(nb, N_HEADS, 3), ((0, 0), (0, 0), (0, LANES - 3)))
    new_rows = jnp.pad(jnp.stack([ks_s, vs_s, kw_s, vw_s], axis=1), ((0, 0), (0, P_GATE - P_NEW - 4), (0, 0)))
    pack = jnp.concatenate(
        [q16.astype(F32), sel16, oc_s, szb_s.reshape(nb, N_HEADS, LANES), new_rows, gate_rows], axis=1)
    assert pack.shape[1] == P_ROWS
    mix_s, okw, ovw = _decode_sel(
        page_table, kv_t(cache_k_sel), kv_t(cache_v_sel), kv_t(cache_k_win), kv_t(cache_v_win),
        pack, onehot, cur)
    wo_slab = jnp.zeros((N_HEADS, N_KV, HEAD_DIM, D_MODEL), BF16)
    wo_heads = wo_b.reshape(N_HEADS, HEAD_DIM, D_MODEL)
    for e in range(N_HEADS):
        wo_slab = wo_slab.at[e, e // GQA].set(wo_heads[e])
    y_sample = _outproj(xs2, a_s, mix_s.reshape(nb, N_HEADS * LANES), wo_a,
                        wo_slab.reshape(N_HEADS * LANES, D_MODEL), fg, nb)
    y_sample = y_sample.reshape(nb, 1, D_MODEL)
    ss = lambda a: a.reshape(1, nb, 1, N_KV, HEAD_DIM)

    return (y_prompt, y_sample,
            st(kc_t), st(vc_t), st(ks_t), st(vs_t),
            st(kw_t[:, :, seq - keep_p:]), st(vw_t[:, :, seq - keep_p:]),
            ss(kc_s), ss(vc_s), ss(ks_s), ss(vs_s),
            st(okw), st(ovw),
            vn_s.reshape(1, nb, 1, D_A))
```

```python
import functools

import jax
import jax.numpy as jnp
from jax import lax
from jax.experimental import pallas as pl
from jax.experimental.pallas import tpu as pltpu

F32 = jnp.float32
BF16 = jnp.bfloat16

D_MODEL = 1024
HEAD_DIM = 64
D_A = 512
D_B = 512
A_GROUPS = 8
CHUNK = 128
N_HEADS = 8
N_KV = 2
GQA = 4
KV_W = 128
ROT_DIM = 16
ROPE_THETA = 500000.0
CMP_BLOCK = 32
CMP_STRIDE = 16
SEL_BLOCK = 64
SEL_SHIFT = 6
N_SELECT = 16
WINDOW = 512
Q_BLOCK = 128
PAGE_SIZE = 128
NORM_EPS = 1e-6
FORCE_SCORE = 1e4
N_FORCED = 3
NEG = -1e30
LOG2_E = 1.4426950408889634

LANES = 128
VMEM_LIMIT_BYTES = 56 * 1024 * 1024

C_U, C_V, C_ZA, C_Q = 0, 512, 1024, 1536
C_KC, C_VC, C_KS, C_VS, C_KW, C_VW = 2048, 2176, 2304, 2432, 2560, 2688
C_G, C_ZB, C_END = 2816, 2944, 3456
N_GATE_COLS = 3 * N_HEADS

ROW_TILE = 512
Q_SUB = 4
SEL_TILE = 512
DECODE_GROUP = 4
WIN_SPAN = WINDOW + Q_BLOCK

_NT = (((1,), (1,)), ((), ()))


def _dot(a, b):
    return jnp.dot(a, b, preferred_element_type=F32)


def _dot_nt(a, b):
    return lax.dot_general(a, b, _NT, preferred_element_type=F32)


def _lane_iota(shape):
    return lax.broadcasted_iota(jnp.int32, shape, len(shape) - 1)


def _row_iota(shape):
    return lax.broadcasted_iota(jnp.int32, shape, len(shape) - 2)


def _rope(x, tab):
    c = tab[:, 0:LANES]
    s1 = tab[:, LANES:2 * LANES]
    s2 = tab[:, 2 * LANES:3 * LANES]
    return x * c + pltpu.roll(x, LANES - ROT_DIM // 2, 1) * s1 + pltpu.roll(x, ROT_DIM // 2, 1) * s2


def _rms_rows(x, g):
    ms = jnp.mean(x * x, axis=-1, keepdims=True)
    return x * lax.rsqrt(ms + NORM_EPS) * g


def _layer_norm_rows(v, g, b):
    mu = jnp.mean(v, axis=-1, keepdims=True)
    vc = v - mu
    var = jnp.mean(vc * vc, axis=-1, keepdims=True)
    return vc * lax.rsqrt(var + NORM_EPS) * g + b


def _head_slabs(x512):
    out = []
    rows = x512.shape[0]
    lane = _lane_iota((rows, LANES))
    lo = lane < HEAD_DIM
    for j in range(4):
        slab = x512[:, j * LANES:(j + 1) * LANES]
        swapped = pltpu.roll(slab, HEAD_DIM, 1)
        if j < 2:
            out.append(jnp.where(lo, slab, 0.0))
            out.append(jnp.where(lo, swapped, 0.0))
        else:
            out.append(jnp.where(lo, 0.0, swapped))
            out.append(jnp.where(lo, 0.0, slab))
    return out


def _project(x_ref, g_ref, w_refs, z_scr):
    hb = _rms_rows(x_ref[...], g_ref[...]).astype(BF16)
    wm_ref, wg_ref, wz_ref = w_refs
    z_scr[:, 0:C_G] = _dot(hb, wm_ref[...])
    z_scr[:, C_G:C_ZB] = _dot(hb, wg_ref[...])
    z_scr[:, C_ZB:C_END] = _dot(hb, wz_ref[...])


def _inproj_prompt_kernel(x_ref, g_ref, wm_ref, wg_ref, wz_ref, lng_ref, lnb_ref, ws_ref, bias_ref, rope_ref,
                          aout_ref, qt_ref, kc_ref, vc_ref,
                          kct_ref, vct_ref, kst_ref, vst_ref, kwt_ref, vwt_ref,
                          ksa_ref, kwb_ref, vsb_ref, vwb_ref, gate_ref, szb_ref,
                          z_scr, kstage_scr, vstage_scr, *, tm, tiles_per_batch):
    _project(x_ref, g_ref, (wm_ref, wg_ref, wz_ref), z_scr)
    tab = rope_ref[...]

    lane = _lane_iota((CHUNK, LANES))
    lo = lane < HEAD_DIM
    for c in range(tm // CHUNK):
        r0 = c * CHUNK
        u = z_scr[r0:r0 + CHUNK, C_U:C_U + D_A]
        v = z_scr[r0:r0 + CHUNK, C_V:C_V + D_A]
        za = z_scr[r0:r0 + CHUNK, C_ZA:C_ZA + D_A]
        vn = _layer_norm_rows(v, lng_ref[...], lnb_ref[...])
        parts = []
        for p in range(A_GROUPS // 2):
            vp = vn[:, p * LANES:(p + 1) * LANES]
            v_lo = jnp.where(lo, vp, 0.0).astype(BF16)
            v_hi = jnp.where(lo, 0.0, vp).astype(BF16)
            parts.append(_dot(ws_ref[2 * p], v_lo) + _dot(ws_ref[2 * p + 1], v_hi))
        s = jnp.concatenate(parts, axis=1) + bias_ref[...]
        aout_ref[r0:r0 + CHUNK, :] = (u * s * jax.nn.silu(za)).astype(BF16)

    q = z_scr[:, C_Q:C_Q + D_B]
    qr = jnp.concatenate(
        [_rope(q[:, j * LANES:(j + 1) * LANES], tab) for j in range(4)], axis=1) * (HEAD_DIM ** -0.5 * LOG2_E)
    slabs = _head_slabs(qr)
    kc = z_scr[:, C_KC:C_KC + KV_W]
    vc = z_scr[:, C_VC:C_VC + KV_W]
    for src, stage, dst in ((kc, kstage_scr, kc_ref), (vc, vstage_scr, vc_ref)):
        stage[...] = src
        for r in range(CMP_STRIDE):
            dst[:, r * KV_W:(r + 1) * KV_W] = stage[pl.ds(r, tm // CMP_STRIDE, stride=CMP_STRIDE), :]
    ks = _rope(z_scr[:, C_KS:C_KS + KV_W], tab)
    vs = z_scr[:, C_VS:C_VS + KV_W]
    kw = _rope(z_scr[:, C_KW:C_KW + KV_W], tab)
    vw = z_scr[:, C_VW:C_VW + KV_W]
    gates = jax.nn.sigmoid(z_scr[:, C_G:C_G + LANES])
    pos = (pl.program_id(0) % tiles_per_batch) * tm + _row_iota((tm, LANES))
    onehot = jnp.where((pos >> SEL_SHIFT) == _lane_iota((tm, LANES)), 1.0, 0.0)
    ksa_ref[:, 0:LANES] = ks.astype(BF16)
    ksa_ref[:, LANES:2 * LANES] = onehot.astype(BF16)
    kwb_ref[...] = kw.astype(BF16)
    for j in range(tm // LANES):
        r0, r1 = j * LANES, (j + 1) * LANES
        kct_ref[0, :, r0:r1] = kc[r0:r1].T
        vct_ref[0, :, r0:r1] = vc[r0:r1].T
        kst_ref[0, :, r0:r1] = ks[r0:r1].T
        kwt_ref[0, :, r0:r1] = kw[r0:r1].T
        vs_t, vw_t = vs[r0:r1].T, vw[r0:r1].T
        vst_ref[0, :, r0:r1] = vs_t
        vwt_ref[0, :, r0:r1] = vw_t
        vsb_ref[0, j] = vs_t.astype(BF16)
        vwb_ref[0, j] = vw_t.astype(BF16)
        for e in range(N_HEADS):
            qt_ref[0, j, :, e * Q_BLOCK:(e + 1) * Q_BLOCK] = slabs[e][r0:r1].T.astype(BF16)
        gate_ref[0, j] = gates[r0:r1].T
    szb_ref[...] = jax.nn.silu(z_scr[:, C_ZB:C_ZB + D_B])


def _inproj_prompt(x2, g, w, lng, lnb, ws_bf, bias_full, rope_tab, seq, tm=ROW_TILE):
    n = x2.shape[0]
    tiles_per_batch = seq // tm
    bsz = n // seq
    n_tok_tiles = seq // LANES
    row = lambda i: (i, 0)
    const2 = lambda i: (0, 0)
    rows2d = lambda width, dtype: (jax.ShapeDtypeStruct((n, width), dtype), pl.BlockSpec((tm, width), row))
    kv_t = (jax.ShapeDtypeStruct((bsz, KV_W, seq), F32),
            pl.BlockSpec((1, KV_W, tm), lambda i: (i // tiles_per_batch, 0, i % tiles_per_batch)))
    chunk_rows = (jax.ShapeDtypeStruct((n // CMP_STRIDE, CMP_STRIDE * KV_W), F32),
                  pl.BlockSpec((tm // CMP_STRIDE, CMP_STRIDE * KV_W), row))
    tiles = lambda r, c, dtype: (
        jax.ShapeDtypeStruct((bsz, n_tok_tiles, r, c), dtype),
        pl.BlockSpec((1, tm // LANES, r, c), lambda i: (i // tiles_per_batch, i % tiles_per_batch, 0, 0)))
    outs = [
        rows2d(D_A, BF16),
        tiles(KV_W, N_HEADS * Q_BLOCK, BF16),
        chunk_rows, chunk_rows,
        kv_t, kv_t, kv_t, kv_t, kv_t, kv_t,
        rows2d(2 * KV_W, BF16),
        rows2d(KV_W, BF16),
        tiles(KV_W, LANES, BF16),
        tiles(KV_W, LANES, BF16),
        tiles(LANES, LANES, F32),
        rows2d(D_B, F32),
    ]
    out_shapes = [o[0] for o in outs]
    out_specs = [o[1] for o in outs]
    in_specs = [
        pl.BlockSpec((tm, D_MODEL), row),
        pl.BlockSpec((1, D_MODEL), const2),
        pl.BlockSpec((D_MODEL, C_G), const2),
        pl.BlockSpec((D_MODEL, C_ZB - C_G), const2),
        pl.BlockSpec((D_MODEL, C_END - C_ZB), const2),
        pl.BlockSpec((1, D_A), const2),
        pl.BlockSpec((1, D_A), const2),
        pl.BlockSpec((A_GROUPS, CHUNK, CHUNK), lambda i: (0, 0, 0)),
        pl.BlockSpec((CHUNK, D_A), const2),
        pl.BlockSpec((tm, 3 * LANES), lambda i: (i % tiles_per_batch, 0)),
    ]
    return pl.pallas_call(
        functools.partial(_inproj_prompt_kernel, tm=tm, tiles_per_batch=tiles_per_batch),
        grid=(n // tm,),
        in_specs=in_specs,
        out_specs=out_specs,
        out_shape=out_shapes,
        scratch_shapes=[pltpu.VMEM((tm, C_END), F32), pltpu.VMEM((tm, KV_W), F32), pltpu.VMEM((tm, KV_W), F32)],
        compiler_params=pltpu.CompilerParams(
            dimension_semantics=("arbitrary",), vmem_limit_bytes=VMEM_LIMIT_BYTES),
        name="inproj_prompt",
    )(x2, g, *w, lng, lnb, ws_bf, bias_full, rope_tab)


def _inproj_decode_kernel(x_ref, g_ref, wm_ref, wg_ref, wz_ref, lng_ref, lnb_ref, wsv_ref, bsv_ref, rope_ref,
                          aout_ref, qa_ref, kc_ref, vc_ref, ks_ref, vs_ref, kw_ref, vw_ref,
                          gate_ref, szb_ref, vn_ref, z_scr):
    _project(x_ref, g_ref, (wm_ref, wg_ref, wz_ref), z_scr)
    tab = rope_ref[...]
    u = z_scr[:, C_U:C_U + D_A]
    v = z_scr[:, C_V:C_V + D_A]
    za = z_scr[:, C_ZA:C_ZA + D_A]
    vn = _layer_norm_rows(v, lng_ref[...], lnb_ref[...])
    vn_ref[...] = vn
    s = vn * wsv_ref[...] + bsv_ref[...]
    aout_ref[...] = (u * s * jax.nn.silu(za)).astype(BF16)

    q = z_scr[:, C_Q:C_Q + D_B]
    qr = jnp.concatenate(
        [_rope(q[:, j * LANES:(j + 1) * LANES], tab) for j in range(4)], axis=1) * (HEAD_DIM ** -0.5)
    for e, slab in enumerate(_head_slabs(qr)):
        qa_ref[:, e * LANES:(e + 1) * LANES] = slab.astype(BF16)
    kc_ref[...] = z_scr[:, C_KC:C_KC + KV_W]
    vc_ref[...] = z_scr[:, C_VC:C_VC + KV_W]
    ks_ref[...] = _rope(z_scr[:, C_KS:C_KS + KV_W], tab)
    vs_ref[...] = z_scr[:, C_VS:C_VS + KV_W]
    kw_ref[...] = _rope(z_scr[:, C_KW:C_KW + KV_W], tab)
    vw_ref[...] = z_scr[:, C_VW:C_VW + KV_W]
    gate_ref[...] = jax.nn.sigmoid(z_scr[:, C_G:C_G + LANES])
    szb = jax.nn.silu(z_scr[:, C_ZB:C_ZB + D_B])
    for e, slab in enumerate(_head_slabs(szb)):
        szb_ref[:, e * LANES:(e + 1) * LANES] = slab


def _inproj_decode(x2, g, w, lng, lnb, wsv, bsv, rope_tab):
    n = x2.shape[0]
    out_shapes = [
        jax.ShapeDtypeStruct((n, D_A), BF16),
        jax.ShapeDtypeStruct((n, N_HEADS * LANES), BF16),
        jax.ShapeDtypeStruct((n, KV_W), F32),
        jax.ShapeDtypeStruct((n, KV_W), F32),
        jax.ShapeDtypeStruct((n, KV_W), F32),
        jax.ShapeDtypeStruct((n, KV_W), F32),
        jax.ShapeDtypeStruct((n, KV_W), F32),
        jax.ShapeDtypeStruct((n, KV_W), F32),
        jax.ShapeDtypeStruct((n, LANES), F32),
        jax.ShapeDtypeStruct((n, N_HEADS * LANES), F32),
        jax.ShapeDtypeStruct((n, D_A), F32),
    ]
    return pl.pallas_call(
        _inproj_decode_kernel,
        out_shape=out_shapes,
        scratch_shapes=[pltpu.VMEM((n, C_END), F32)],
        compiler_params=pltpu.CompilerParams(vmem_limit_bytes=VMEM_LIMIT_BYTES),
        name="inproj_decode",
    )(x2, g, *w, lng, lnb, wsv, bsv, rope_tab)


def _compress_chunks(ch, pos_term, w_ref):
    ab = _dot(ch.astype(BF16), w_ref[...])
    return ab[:, 0:KV_W] + pltpu.roll(ab[:, KV_W:2 * KV_W], ch.shape[0] - 1, 0) + pos_term


def _position_term(pe_ref, w_ref):
    pe = pe_ref[...]
    hi = pe.astype(BF16)
    r1 = pe - hi.astype(F32)
    mid = r1.astype(BF16)
    lo = (r1 - mid.astype(F32)).astype(BF16)
    pw = _dot(hi, w_ref[...]) + _dot(mid, w_ref[...]) + _dot(lo, w_ref[...])
    return pw[0:1, 0:KV_W] + pw[1:2, KV_W:2 * KV_W]


def _compress_prompt_kernel(kc_ref, vc_ref, wk_ref, wv_ref, pek_ref, pev_ref, rope_ref, kcc_ref, vcct_ref):
    kcc = _compress_chunks(kc_ref[0], _position_term(pek_ref, wk_ref), wk_ref)
    kcc_ref[0] = _rope(kcc, rope_ref[...]).astype(BF16)
    vcc = _compress_chunks(vc_ref[0], _position_term(pev_ref, wv_ref), wv_ref)
    for j in range(vcc.shape[0] // LANES):
        vcct_ref[0, :, j * LANES:(j + 1) * LANES] = vcc[j * LANES:(j + 1) * LANES].T.astype(BF16)


def _compress_prompt(kc3, vc3, wk, wv, pek, pev, rope_tab):
    b, c, width = kc3.shape
    blk = pl.BlockSpec((1, c, width), lambda i: (i, 0, 0))
    const2 = lambda i: (0, 0)
    return pl.pallas_call(
        _compress_prompt_kernel,
        grid=(b,),
        in_specs=[blk, blk,
                  pl.BlockSpec(wk.shape, const2), pl.BlockSpec(wv.shape, const2),
                  pl.BlockSpec(pek.shape, const2), pl.BlockSpec(pev.shape, const2),
                  pl.BlockSpec(rope_tab.shape, const2)],
        out_specs=[pl.BlockSpec((1, c, KV_W), lambda i: (i, 0, 0)),
                   pl.BlockSpec((1, KV_W, c), lambda i: (i, 0, 0))],
        out_shape=[jax.ShapeDtypeStruct((b, c, KV_W), BF16), jax.ShapeDtypeStruct((b, KV_W, c), BF16)],
        compiler_params=pltpu.CompilerParams(
            dimension_semantics=("arbitrary",), vmem_limit_bytes=VMEM_LIMIT_BYTES),
        name="compress_prompt",
    )(kc3, vc3, wk, wv, pek, pev, rope_tab)


def _select_blocks(imp, kk, candidate, forced):
    assert FORCE_SCORE > 2 * GQA
    start = jnp.where(candidate, jnp.where(forced, -jnp.inf, imp), -jnp.inf)
    score = start
    for _ in range(N_SELECT - N_FORCED):
        cm = jnp.max(score, axis=0, keepdims=True)
        first = jnp.min(jnp.where(score == cm, kk, LANES), axis=0, keepdims=True)
        score = jnp.where(kk == first, -jnp.inf, score)
    taken = (score == -jnp.inf) & (start > -jnp.inf)
    return candidate & (forced | taken)


def _softmax_rows(s):
    mx = jnp.max(s, axis=1, keepdims=True)
    ex = jnp.exp(s - mx)
    return ex * (1.0 / jnp.sum(ex, axis=1, keepdims=True))


def _tile_heads(x):
    return jnp.concatenate([x] * N_HEADS, axis=1)


def _values_by_kv_head(vt, p):
    half = GQA * Q_BLOCK
    return jnp.concatenate(
        [_dot(vt[h * HEAD_DIM:(h + 1) * HEAD_DIM], p[:, h * half:(h + 1) * half]) for h in range(N_KV)], axis=1)


def _attn_prompt_kernel(*refs):
    def body(sub, carry):
        _attn_query_block(sub, *refs)
        return carry

    lax.fori_loop(0, Q_SUB, body, 0)


def _attn_query_block(sub, qt_ref, gate_ref, szb_ref, kcc_ref, vcct_ref, ovlt_ref,
                      ksa_ref, vsb_ref, kwb_ref, vwb_ref, out_ref,
                      qaug_scr, acc_scr, sa_scr, sb_scr):
    i_blk = pl.program_id(1) * Q_SUB + sub
    q0 = i_blk * Q_BLOCK
    cols = N_HEADS * Q_BLOCK
    qt = qt_ref[0, sub]
    out_rows = pl.ds(pl.multiple_of(sub * Q_BLOCK, Q_BLOCK), Q_BLOCK)

    n_cmp = kcc_ref.shape[1]

    def compressed(extent):
        t_c = q0 + _lane_iota((extent, Q_BLOCK))
        seen = (_row_iota((extent, Q_BLOCK)) * CMP_STRIDE + (CMP_BLOCK - 1)) <= t_c
        s = _dot(kcc_ref[0, 0:extent, :], qt) + _tile_heads(jnp.where(seen, 0.0, NEG))
        mx = jnp.max(s, axis=0, keepdims=True)
        ex = jnp.exp2(s - mx)
        inv = jnp.where(mx > 0.5 * NEG, 1.0 / jnp.sum(ex, axis=0, keepdims=True), 0.0)
        p_c = ex * inv
        imps = []
        for h in range(N_KV):
            c0 = h * GQA * Q_BLOCK
            p_sum = (p_c[:, c0:c0 + Q_BLOCK] + p_c[:, c0 + Q_BLOCK:c0 + 2 * Q_BLOCK]
                     + p_c[:, c0 + 2 * Q_BLOCK:c0 + 3 * Q_BLOCK] + p_c[:, c0 + 3 * Q_BLOCK:c0 + 4 * Q_BLOCK])
            imps.append(_dot(ovlt_ref[:, 0:extent], p_sum.astype(BF16)))
        return (_values_by_kv_head(vcct_ref[0, :, 0:extent], p_c.astype(BF16)), *imps)

    n_need = (q0 + Q_BLOCK) // CMP_STRIDE
    extents = list(range(LANES, n_cmp + 1, LANES))
    o_c, *imp_t = lax.switch(
        jnp.minimum((n_need - 1) // LANES, len(extents) - 1),
        [functools.partial(compressed, ext) for ext in extents])

    w0 = pl.multiple_of(jnp.maximum(q0 - WINDOW, 0), Q_BLOCK)
    wj0 = jnp.maximum(i_blk - WINDOW // Q_BLOCK, 0)
    kwin = kwb_ref[0, pl.ds(w0, WIN_SPAN), :]
    vwt = jnp.concatenate([vwb_ref[0, wj0 + i] for i in range(WIN_SPAN // LANES)], axis=1)
    kp = w0 + _row_iota((WIN_SPAN, Q_BLOCK))
    tw = q0 + _lane_iota((WIN_SPAN, Q_BLOCK))
    in_win = jnp.where(kp <= tw, jnp.where(kp > tw - WINDOW, 0.0, NEG), NEG)
    sw = _dot(kwin, qt) + _tile_heads(in_win)
    exw = jnp.exp2(sw - jnp.max(sw, axis=0, keepdims=True))
    o_w = _values_by_kv_head(vwt, exw.astype(BF16)) * (1.0 / jnp.sum(exw, axis=0, keepdims=True))

    kk = _row_iota((LANES, Q_BLOCK))
    tq = q0 + _lane_iota((LANES, Q_BLOCK))
    valid = kk * SEL_BLOCK <= tq
    cur = tq >> SEL_SHIFT
    forced = (kk == 0) | (kk == cur) | (kk == cur - 1)
    both = lambda a: jnp.concatenate([a] * N_KV, axis=1)
    picked = _select_blocks(jnp.concatenate(imp_t, axis=1), both(kk), both(valid), both(forced))
    qaug_scr[0:LANES, :] = qt
    for h in range(N_KV):
        selneg = jnp.where(picked[:, h * Q_BLOCK:(h + 1) * Q_BLOCK], 0.0, NEG).astype(BF16)
        for g in range(GQA):
            c0 = (h * GQA + g) * Q_BLOCK
            qaug_scr[LANES:2 * LANES, c0:c0 + Q_BLOCK] = selneg

    acc_scr[...] = jnp.zeros(acc_scr.shape, F32)
    tiles_per_step = SEL_TILE // LANES

    def score(step, dst):
        t0 = pl.multiple_of(step * SEL_TILE, SEL_TILE)
        dst[...] = _dot(ksa_ref[0, pl.ds(t0, SEL_TILE), :], qaug_scr[...])

    def attend(src, step, m_prev, l_prev, causal_rows=None):
        rows = SEL_TILE if causal_rows is None else causal_rows
        vt = jnp.concatenate([vsb_ref[0, step * tiles_per_step + i] for i in range(rows // LANES)], axis=1)
        sc = src[0:rows, :]
        if causal_rows is not None:
            tok = step * SEL_TILE + _row_iota((rows, Q_BLOCK))
            sc = sc + _tile_heads(jnp.where(tok <= q0 + _lane_iota((rows, Q_BLOCK)), 0.0, NEG))
        m_next = jnp.maximum(m_prev, jnp.max(sc, axis=0, keepdims=True))
        p = jnp.exp2(sc - m_next)
        alpha = jnp.exp2(m_prev - m_next)
        l_next = alpha * l_prev + jnp.sum(p, axis=0, keepdims=True)
        acc_scr[...] = alpha * acc_scr[...] + _values_by_kv_head(vt, p.astype(BF16))
        return m_next, l_next

    n_full = q0 // SEL_TILE
    score(0, sa_scr)

    def pair(jj, carry):
        m_c, l_c = carry
        score(2 * jj + 1, sb_scr)
        m_c, l_c = attend(sa_scr, 2 * jj, m_c, l_c)
        score(2 * jj + 2, sa_scr)
        return attend(sb_scr, 2 * jj + 1, m_c, l_c)

    m_run, l_run = lax.fori_loop(
        0, n_full // 2, pair,
        (jnp.full((1, cols), -jnp.inf, F32), jnp.zeros((1, cols), F32)))

    def last_step(src, m_c, l_c):
        extents = list(range(Q_BLOCK, SEL_TILE + 1, Q_BLOCK))
        return lax.switch(
            (q0 % SEL_TILE) // Q_BLOCK,
            [lambda m, l, ext=ext: attend(src, n_full, m, l, causal_rows=ext)[1] for ext in extents],
            m_c, l_c)

    def odd_tail(m_c, l_c):
        score(n_full, sb_scr)
        m_c, l_c = attend(sa_scr, n_full - 1, m_c, l_c)
        return last_step(sb_scr, m_c, l_c)

    def even_tail(m_c, l_c):
        return last_step(sa_scr, m_c, l_c)

    l_run = lax.cond(n_full % 2 == 1, odd_tail, even_tail, m_run, l_run)

    gt = gate_ref[0, sub]
    o_s = acc_scr[...] * (1.0 / l_run)
    heads = []
    for e in range(N_HEADS):
        c0 = e * Q_BLOCK
        heads.append(gt[3 * e:3 * e + 1, :] * o_c[:, c0:c0 + Q_BLOCK]
                     + gt[3 * e + 1:3 * e + 2, :] * o_s[:, c0:c0 + Q_BLOCK]
                     + gt[3 * e + 2:3 * e + 3, :] * o_w[:, c0:c0 + Q_BLOCK])
    for j in range(N_HEADS // 2):
        slab = jnp.concatenate([heads[2 * j], heads[2 * j + 1]], axis=0).T
        out_ref[0, out_rows, j * LANES:(j + 1) * LANES] = (
            slab * szb_ref[0, out_rows, j * LANES:(j + 1) * LANES]).astype(BF16)


def _attn_prompt(qt4, gate4, szb3, kcc, vcct, ovlt, ksa3, vsb4, kwb3, vwb4):
    b, nq = qt4.shape[0], qt4.shape[1]
    t = nq * Q_BLOCK
    cols = N_HEADS * Q_BLOCK
    assert nq % Q_SUB == 0
    qtile = lambda a: pl.BlockSpec((1, Q_SUB) + a.shape[2:], lambda bi, i: (bi, i, 0, 0))
    rows = pl.BlockSpec((1, Q_SUB * Q_BLOCK, D_B), lambda bi, i: (bi, i, 0))
    seq = lambda a: pl.BlockSpec((1,) + a.shape[1:], lambda bi, i, nd=a.ndim: (bi,) + (0,) * (nd - 1))
    return pl.pallas_call(
        _attn_prompt_kernel,
        grid=(b, nq // Q_SUB),
        in_specs=[qtile(qt4), qtile(gate4), rows,
                  seq(kcc), seq(vcct), pl.BlockSpec(ovlt.shape, lambda bi, i: (0, 0)),
                  seq(ksa3), seq(vsb4), seq(kwb3), seq(vwb4)],
        out_specs=rows,
        out_shape=jax.ShapeDtypeStruct((b, t, D_B), BF16),
        scratch_shapes=[pltpu.VMEM((2 * LANES, cols), BF16),
                        pltpu.VMEM((HEAD_DIM, cols), F32),
                        pltpu.VMEM((SEL_TILE, cols), F32),
                        pltpu.VMEM((SEL_TILE, cols), F32)],
        compiler_params=pltpu.CompilerParams(
            dimension_semantics=("arbitrary", "arbitrary"), vmem_limit_bytes=VMEM_LIMIT_BYTES),
        name="attn_prompt",
    )(qt4, gate4, szb3, kcc, vcct, ovlt, ksa3, vsb4, kwb3, vwb4)


def _outproj_kernel(x_ref, a_ref, m_ref, wa_ref, wb_ref, fg_ref, y_ref):
    y = x_ref[...] + _dot(a_ref[...], wa_ref[...]) + _dot(m_ref[...].astype(BF16), wb_ref[...])
    y_ref[...] = _rms_rows(y, fg_ref[...])


def _outproj(x2, a_out, mix_b, wa, wb, fg, tm):
    n = x2.shape[0]
    row = lambda i: (i, 0)
    const2 = lambda i: (0, 0)
    return pl.pallas_call(
        _outproj_kernel,
        grid=(n // tm,),
        in_specs=[pl.BlockSpec((tm, D_MODEL), row),
                  pl.BlockSpec((tm, a_out.shape[1]), row),
                  pl.BlockSpec((tm, mix_b.shape[1]), row),
                  pl.BlockSpec(wa.shape, const2), pl.BlockSpec(wb.shape, const2),
                  pl.BlockSpec((1, D_MODEL), const2)],
        out_specs=pl.BlockSpec((tm, D_MODEL), row),
        out_shape=jax.ShapeDtypeStruct((n, D_MODEL), F32),
        compiler_params=pltpu.CompilerParams(
            dimension_semantics=("arbitrary",), vmem_limit_bytes=VMEM_LIMIT_BYTES),
        name="outproj",
    )(x2, a_out, mix_b, wa, wb, fg)


def _page_copies(pt_ref, pool_ref, buf_ref, sem_ref, step, slot):
    group, n_pages = buf_ref.shape[1], buf_ref.shape[2]
    return [pltpu.make_async_copy(pool_ref.at[pt_ref[step * group + g, p]], buf_ref.at[slot, g, p],
                                  sem_ref.at[slot])
            for g in range(group) for p in range(n_pages)]


def _gather_pages(pt_ref, pools, bufs, sems):
    b = pl.program_id(0)
    slot = lax.rem(b, 2)

    def start(step, sl):
        for pool, buf, sem in zip(pools, bufs, sems):
            for cp in _page_copies(pt_ref, pool, buf, sem, step, sl):
                cp.start()

    @pl.when(b == 0)
    def _():
        start(0, 0)

    @pl.when(b + 1 < pl.num_programs(0))
    def _():
        start(b + 1, 1 - slot)

    for pool, buf, sem in zip(pools, bufs, sems):
        for cp in _page_copies(pt_ref, pool, buf, sem, b, slot):
            cp.wait()
    return slot


def _decode_cmp_kernel(pt_ref, ck_hbm, cv_hbm, q_ref, wk_ref, wv_ref, pek_ref, pev_ref, rope_ref, ovl_ref,
                       oc_ref, psum_ref, kbuf, vbuf, ksem, vsem, ktok_scr, vtok_scr, kpos_scr, vpos_scr):
    slot = _gather_pages(pt_ref, (ck_hbm, cv_hbm), (kbuf, vbuf), (ksem, vsem))
    group, n_pages = kbuf.shape[1], kbuf.shape[2]

    @pl.when(pl.program_id(0) == 0)
    def _():
        kpos_scr[...] = _position_term(pek_ref, wk_ref)
        vpos_scr[...] = _position_term(pev_ref, wv_ref)

    def chunk_rows(buf, tok_scr, g):
        for p in range(n_pages):
            tok_scr[g, p * PAGE_SIZE:(p + 1) * PAGE_SIZE, :] = buf[slot, g, p].T
        n = tok_scr.shape[1] // CMP_STRIDE
        return jnp.concatenate(
            [tok_scr[g, pl.ds(r, n, stride=CMP_STRIDE), :] for r in range(CMP_STRIDE)], axis=1)

    for g in range(group):
        kch = chunk_rows(kbuf, ktok_scr, g)
        vch = chunk_rows(vbuf, vtok_scr, g)
        kcc = _rope(_compress_chunks(kch, kpos_scr[...], wk_ref), rope_ref[...]).astype(BF16)
        vcc = _compress_chunks(vch, vpos_scr[...], wv_ref).astype(BF16)
        n_chunks = kch.shape[0]
        q = q_ref[g]
        rows = q.shape[0]
        s = _dot_nt(q, kcc)
        seen = _lane_iota((rows, n_chunks)) < n_chunks - 1
        p_c = jnp.where(seen, _softmax_rows(jnp.where(seen, s, NEG)), 0.0)
        oc_ref[g] = _dot(p_c.astype(BF16), vcc)[0:N_HEADS]
        ps0 = jnp.sum(p_c[0:GQA], axis=0, keepdims=True)
        ps1 = jnp.sum(p_c[GQA:2 * GQA], axis=0, keepdims=True)
        psum = jnp.where(_row_iota((rows, n_chunks)) < N_HEADS // 2, ps0, ps1)
        psum_ref[g] = _dot(psum.astype(BF16), ovl_ref[...])[0:N_HEADS]


def _decode_cmp(page_table, ck_pool, cv_pool, q16, wk, wv, pek, pev, rope_tab, ovl):
    nb, n_pages = page_table.shape
    group = DECODE_GROUP
    assert nb % group == 0
    hbm = pl.BlockSpec(memory_space=pl.ANY)
    page_buf = pltpu.VMEM((2, group, n_pages, KV_W, PAGE_SIZE), F32)
    tok_buf = pltpu.VMEM((group, n_pages * PAGE_SIZE, KV_W), F32)
    const2 = lambda b, pt: (0, 0)
    per_b = lambda a: pl.BlockSpec((group,) + a.shape[1:], lambda b, pt: (b, 0, 0))
    in_specs = [hbm, hbm, per_b(q16),
                pl.BlockSpec(wk.shape, const2), pl.BlockSpec(wv.shape, const2),
                pl.BlockSpec(pek.shape, const2), pl.BlockSpec(pev.shape, const2),
                pl.BlockSpec(rope_tab.shape, const2), pl.BlockSpec(ovl.shape, const2)]
    out_blk = pl.BlockSpec((group, N_HEADS, LANES), lambda b, pt: (b, 0, 0))
    grid_spec = pltpu.PrefetchScalarGridSpec(
        num_scalar_prefetch=1, grid=(nb // group,), in_specs=in_specs, out_specs=[out_blk, out_blk],
        scratch_shapes=[page_buf, page_buf, pltpu.SemaphoreType.DMA((2,)), pltpu.SemaphoreType.DMA((2,)),
                        tok_buf, tok_buf, pltpu.VMEM((1, KV_W), F32), pltpu.VMEM((1, KV_W), F32)])
    return pl.pallas_call(
        _decode_cmp_kernel,
        grid_spec=grid_spec,
        out_shape=[jax.ShapeDtypeStruct((nb, N_HEADS, LANES), F32)] * 2,
        compiler_params=pltpu.CompilerParams(
            dimension_semantics=("arbitrary",), vmem_limit_bytes=VMEM_LIMIT_BYTES),
        name="decode_cmp",
    )(page_table, ck_pool, cv_pool, q16, wk, wv, pek, pev, rope_tab, ovl)


def _decode_topk_kernel(imp_ref, sel_ref, *, n_blk, cur):
    n = imp_ref.shape[0]
    tiles = [imp_ref[i * LANES:(i + 1) * LANES, :].T for i in range(n // LANES)]
    imp_t = jnp.concatenate(tiles, axis=1)
    kk = _row_iota(imp_t.shape)
    forced = (kk == 0) | (kk == cur) | (kk == cur - 1)
    selneg = jnp.where(_select_blocks(imp_t, kk, kk < n_blk, forced), 0.0, NEG)
    for i in range(n // LANES):
        sel_ref[i * LANES:(i + 1) * LANES, :] = selneg[:, i * LANES:(i + 1) * LANES].T


def _decode_topk(imp2, n_blk, cur):
    return pl.pallas_call(
        functools.partial(_decode_topk_kernel, n_blk=n_blk, cur=cur),
        out_shape=jax.ShapeDtypeStruct(imp2.shape, F32),
        compiler_params=pltpu.CompilerParams(vmem_limit_bytes=VMEM_LIMIT_BYTES),
        name="decode_topk",
    )(imp2)


def _decode_attend(s, q32, vals_t_bf, k_new, v_new, bias_new):
    kn = k_new.astype(BF16).astype(F32)
    s_new = jnp.sum(q32 * kn, axis=1, keepdims=True) + bias_new
    mx = jnp.maximum(jnp.max(s, axis=1, keepdims=True), s_new)
    ex = jnp.exp(s - mx)
    ex_new = jnp.exp(s_new - mx)
    inv = 1.0 / (jnp.sum(ex, axis=1, keepdims=True) + ex_new)
    p = (ex * inv).astype(BF16)
    p_new = (ex_new * inv).astype(BF16).astype(F32)
    return _dot_nt(p, vals_t_bf) + p_new * v_new.astype(BF16).astype(F32)


P_Q, P_SEL, P_OC, P_SZB, P_NEW, P_GATE, P_ROWS = 0, 16, 32, 40, 48, 56, 64


def _row_to_col(row):
    n = row.shape[1]
    diag = _row_iota((n, n)) == _lane_iota((n, n))
    return jnp.sum(jnp.where(diag, row, 0.0), axis=1, keepdims=True)


def _decode_sel_kernel(pt_ref, ks_hbm, vs_hbm, ckw_ref, cvw_ref, pack_ref, onehot_ref,
                       mix_ref, okw_ref, ovw_ref, kbuf, vbuf, ksem, vsem, *, cur):
    slot = _gather_pages(pt_ref, (ks_hbm, vs_hbm), (kbuf, vbuf), (ksem, vsem))
    group, n_pages = kbuf.shape[1], kbuf.shape[2]
    for g in range(group):
        pk = pack_ref[g]
        q32 = pk[P_Q:P_Q + 2 * N_HEADS]
        q = q32.astype(BF16)
        sel = pk[P_SEL:P_SEL + 2 * N_HEADS].astype(BF16)
        rows = q.shape[0]
        ks_new, vs_new = pk[P_NEW:P_NEW + 1], pk[P_NEW + 1:P_NEW + 2]
        kw_new, vw_new = pk[P_NEW + 2:P_NEW + 3], pk[P_NEW + 3:P_NEW + 4]

        kt = jnp.concatenate([kbuf[slot, g, p] for p in range(n_pages)], axis=1).astype(BF16)
        vt = jnp.concatenate([vbuf[slot, g, p] for p in range(n_pages)], axis=1).astype(BF16)
        kaug = jnp.concatenate([kt, onehot_ref[...]], axis=0)
        qaug = jnp.concatenate([q, sel], axis=1)
        s_sel = _dot(qaug, kaug)
        o_s = _decode_attend(s_sel, q32, vt, ks_new, vs_new, sel.astype(F32)[:, cur:cur + 1])

        kwin = ckw_ref[g]
        vwin = cvw_ref[g]
        keep = kwin.shape[1]
        visible = _lane_iota((rows, keep)) > keep - WINDOW
        s_win = jnp.where(visible, _dot(q, kwin.astype(BF16)), NEG)
        o_w = _decode_attend(s_win, q32, vwin.astype(BF16), kw_new, vw_new, 0.0)

        gates = pk[P_GATE:P_GATE + N_HEADS]
        o = (gates[:, 0:1] * pk[P_OC:P_OC + N_HEADS] + gates[:, 1:2] * o_s[0:N_HEADS]
             + gates[:, 2:3] * o_w[0:N_HEADS])
        mix_ref[g] = o * pk[P_SZB:P_SZB + N_HEADS]

        last = _lane_iota((KV_W, keep)) == keep - 1
        okw_ref[g] = jnp.where(last, _row_to_col(kw_new), pltpu.roll(kwin, keep - 1, 1))
        ovw_ref[g] = jnp.where(last, _row_to_col(vw_new), pltpu.roll(vwin, keep - 1, 1))


def _decode_sel(page_table, ks_pool, vs_pool, ckw, cvw, pack, onehot, cur):
    nb, n_pages = page_table.shape
    group = DECODE_GROUP
    assert nb % group == 0
    keep = ckw.shape[2]
    hbm = pl.BlockSpec(memory_space=pl.ANY)
    page_buf = pltpu.VMEM((2, group, n_pages, KV_W, PAGE_SIZE), F32)
    per_b = lambda a: pl.BlockSpec((group,) + a.shape[1:], lambda b, pt: (b, 0, 0))
    in_specs = [hbm, hbm, per_b(ckw), per_b(cvw), per_b(pack),
                pl.BlockSpec(onehot.shape, lambda b, pt: (0, 0))]
    out_specs = [pl.BlockSpec((group, N_HEADS, LANES), lambda b, pt: (b, 0, 0)),
                 pl.BlockSpec((group, KV_W, keep), lambda b, pt: (b, 0, 0)),
                 pl.BlockSpec((group, KV_W, keep), lambda b, pt: (b, 0, 0))]
    grid_spec = pltpu.PrefetchScalarGridSpec(
        num_scalar_prefetch=1, grid=(nb // group,), in_specs=in_specs, out_specs=out_specs,
        scratch_shapes=[page_buf, page_buf, pltpu.SemaphoreType.DMA((2,)), pltpu.SemaphoreType.DMA((2,))])
    return pl.pallas_call(
        functools.partial(_decode_sel_kernel, cur=cur),
        grid_spec=grid_spec,
        out_shape=[jax.ShapeDtypeStruct((nb, N_HEADS, LANES), F32),
                   jax.ShapeDtypeStruct((nb, KV_W, keep), F32),
                   jax.ShapeDtypeStruct((nb, KV_W, keep), F32)],
        compiler_params=pltpu.CompilerParams(
            dimension_semantics=("arbitrary",), vmem_limit_bytes=VMEM_LIMIT_BYTES),
        name="decode_sel",
    )(page_table, ks_pool, vs_pool, ckw, cvw, pack, onehot)


def _rope_table(pos):
    half = ROT_DIM // 2
    inv = jnp.power(jnp.float32(ROPE_THETA), -jnp.arange(half, dtype=F32) / half)
    ang = pos.astype(F32)[:, None] * inv[None, :]
    cs = jnp.concatenate([jnp.cos(ang), jnp.sin(ang)], axis=1)
    lane = jnp.arange(3 * LANES)
    seg, l64 = lane // LANES, lane % HEAD_DIM
    src = jnp.where(seg == 0, l64 % half, half + l64 % half)
    sign = jnp.where(seg == 0, l64 < ROT_DIM,
                     jnp.where(seg == 1, l64 < half, (l64 >= half) & (l64 < ROT_DIM))).astype(F32)
    sign = jnp.where(seg == 1, -sign, sign)
    place = (jnp.arange(ROT_DIM)[:, None] == src[None, :]).astype(F32) * sign[None, :]
    ones = ((seg == 0) & (l64 >= ROT_DIM)).astype(F32)
    return jnp.dot(cs, place, precision=lax.Precision.HIGHEST) + ones[None, :]


def _compress_weights(w_c, pe_c):
    w4 = w_c.reshape(2, CMP_STRIDE, HEAD_DIM, HEAD_DIM)
    wb = jnp.einsum('arde,hg->rhdage', w4, jnp.eye(N_KV, dtype=w_c.dtype))
    wb = wb.reshape(CMP_STRIDE * KV_W, 2 * KV_W).astype(BF16)
    pe = jnp.broadcast_to(pe_c.reshape(2, CMP_STRIDE, 1, HEAD_DIM), (2, CMP_STRIDE, N_KV, HEAD_DIM))
    return wb, jnp.pad(pe.reshape(2, CMP_STRIDE * KV_W), ((0, 14), (0, 0)))


def _overlap_matrix(n_rows, n_blk):
    cs = jnp.arange(n_rows, dtype=jnp.int32)[:, None] * CMP_STRIDE
    bs = jnp.arange(LANES, dtype=jnp.int32)[None, :] * SEL_BLOCK
    hit = (cs < bs + SEL_BLOCK) & (cs + CMP_BLOCK > bs) & (jnp.arange(LANES)[None, :] < n_blk)
    return hit.astype(BF16)


def kernel(x_prompt, x_sample, cache_k_cmp, cache_v_cmp, cache_k_sel, cache_v_sel, cache_k_win,
           cache_v_win, page_table, norm_g, w_in, ln_v_g, ln_v_b, w_s, b_s, w_ck, pe_ck, w_cv,
           pe_cv, w_out, final_g):
    depth = norm_g.shape[0]
    assert depth == 1
    bsz, seq, _ = x_prompt.shape
    nb, dec_seq, _ = x_sample.shape
    assert dec_seq == 1 and seq % SEL_TILE == 0 and seq >= WIN_SPAN and (seq // CMP_STRIDE) % LANES == 0
    n_pages = page_table.shape[1]
    past = n_pages * PAGE_SIZE
    keep = cache_k_win.shape[2]
    assert keep == WINDOW and past % SEL_BLOCK == 0

    w = w_in[0]
    w_pad = (w[:, :C_G].astype(BF16),
             jnp.pad(w[:, C_G:C_G + N_GATE_COLS], ((0, 0), (0, LANES - N_GATE_COLS))).astype(BF16),
             w[:, C_G + N_GATE_COLS:].astype(BF16))
    g_row = norm_g[0][None, :]
    lng = ln_v_g[0][None, :]
    lnb = ln_v_b[0][None, :]
    fg = final_g[None, :]
    wk_c, pek = _compress_weights(w_ck[0], pe_ck[0])
    wv_c, pev = _compress_weights(w_cv[0], pe_cv[0])
    w_o = w_out[0].astype(BF16)
    wo_a, wo_b = w_o[:D_A], w_o[D_A:]

    tril = jnp.tril(jnp.ones((CHUNK, CHUNK), w_s.dtype))
    ws_bf = (w_s[0] * tril).astype(BF16)
    bias_full = jnp.repeat(b_s[0].T, HEAD_DIM, axis=1)
    xp2 = x_prompt.reshape(bsz * seq, D_MODEL)
    (a_out, qt, kc, vc, kc_t, vc_t, ks_t, vs_t, kw_t, vw_t, ksa, kwb, vsb, vwb, gates_t, szb) = _inproj_prompt(
        xp2, g_row, w_pad, lng, lnb, ws_bf, bias_full,
        _rope_table(jnp.arange(seq, dtype=jnp.int32)), seq)
    n_chunks = seq // CMP_STRIDE
    kcc, vcc_t = _compress_prompt(
        kc.reshape(bsz, n_chunks, CMP_STRIDE * KV_W), vc.reshape(bsz, n_chunks, CMP_STRIDE * KV_W),
        wk_c, wv_c, pek, pev,
        _rope_table(jnp.arange(n_chunks, dtype=jnp.int32) * CMP_STRIDE))
    b3 = lambda a: a.reshape(bsz, seq, a.shape[-1])
    mix_p = _attn_prompt(qt, gates_t, b3(szb), kcc, vcc_t,
                         _overlap_matrix(n_chunks, seq // SEL_BLOCK).T,
                         b3(ksa), vsb, b3(kwb), vwb)
    y_prompt = _outproj(xp2, a_out, mix_p.reshape(bsz * seq, D_B), wo_a, wo_b, fg, ROW_TILE)
    y_prompt = y_prompt.reshape(bsz, seq, D_MODEL)
    st = lambda a: jnp.transpose(a.reshape(a.shape[0], N_KV, HEAD_DIM, a.shape[2]), (0, 3, 1, 2))[None]
    keep_p = min(WINDOW, seq)

    xs2 = x_sample.reshape(nb, D_MODEL)
    wsv = jnp.repeat(w_s[0][:, 0, 0], HEAD_DIM)[None, :]
    bsv = jnp.repeat(b_s[0][:, 0], HEAD_DIM)[None, :]
    (a_s, qa_s, kc_s, vc_s, ks_s, vs_s, kw_s, vw_s, gates_s, szb_s, vn_s) = _inproj_decode(
        xs2, g_row, w_pad, lng, lnb, wsv, bsv,
        _rope_table(jnp.full((nb,), past, dtype=jnp.int32)))
    q16 = jnp.pad(qa_s.reshape(nb, N_HEADS, LANES), ((0, 0), (0, N_HEADS), (0, 0)))
    d_chunks = past // CMP_STRIDE
    n_blk = -(-(past + 1) // SEL_BLOCK)
    cur = past // SEL_BLOCK
    kv_t = lambda c: jnp.transpose(c[0], (0, 2, 3, 1)).reshape(c.shape[1], KV_W, c.shape[2])
    oc_s, imp_s = _decode_cmp(
        page_table, kv_t(cache_k_cmp), kv_t(cache_v_cmp),
        q16, wk_c, wv_c, pek, pev,
        _rope_table(jnp.arange(d_chunks, dtype=jnp.int32) * CMP_STRIDE),
        _overlap_matrix(d_chunks, n_blk))
    selneg = _decode_topk(imp_s[:, ::GQA, :].reshape(nb * N_KV, LANES), n_blk, cur)
    sel16 = jnp.pad(jnp.repeat(selneg.reshape(nb, N_KV, LANES), GQA, axis=1), ((0, 0), (0, N_HEADS), (0, 0)))
    tok_blk = jnp.arange(past, dtype=jnp.int32)[None, :] // SEL_BLOCK
    onehot = (tok_blk == jnp.arange(LANES, dtype=jnp.int32)[:, None]).astype(BF16)
    gate_rows = jnp.pad(gates_s[:, :N_GATE_COLS].reshape(nb, N_HEADS, 3), ((0, 0), (0, 0), (0, LANES - 3)))
    new_rows = jnp.pad(jnp.stack([ks_s, vs_s, kw_s, vw_s], axis=1), ((0, 0), (0, P_GATE - P_NEW - 4), (0, 0)))
    pack = jnp.concatenate(
        [q16.astype(F32), sel16, oc_s, szb_s.reshape(nb, N_HEADS, LANES), new_rows, gate_rows], axis=1)
    assert pack.shape[1] == P_ROWS
    mix_s, okw, ovw = _decode_sel(
        page_table, kv_t(cache_k_sel), kv_t(cache_v_sel), kv_t(cache_k_win), kv_t(cache_v_win),
        pack, onehot, cur)
    wo_slab = jnp.zeros((N_HEADS, N_KV, HEAD_DIM, D_MODEL), BF16)
    wo_heads = wo_b.reshape(N_HEADS, HEAD_DIM, D_MODEL)
    for e in range(N_HEADS):
        wo_slab = wo_slab.at[e, e // GQA].set(wo_heads[e])
    y_sample = _outproj(xs2, a_s, mix_s.reshape(nb, N_HEADS * LANES), wo_a,
                        wo_slab.reshape(N_HEADS * LANES, D_MODEL), fg, nb)
    y_sample = y_sample.reshape(nb, 1, D_MODEL)
    ss = lambda a: a.reshape(1, nb, 1, N_KV, HEAD_DIM)

    return (y_prompt, y_sample,
            st(kc_t), st(vc_t), st(ks_t), st(vs_t),
            st(kw_t[:, :, seq - keep_p:]), st(vw_t[:, :, seq - keep_p:]),
            ss(kc_s), ss(vc_s), ss(ks_s), ss(vs_s),
            st(okw), st(ovw),
            vn_s.reshape(1, nb, 1, D_A))
```

```python
import functools

import jax
import jax.numpy as jnp
from jax import lax
from jax.experimental import pallas as pl
from jax.experimental.pallas import tpu as pltpu

F32 = jnp.float32
BF16 = jnp.bfloat16

D_MODEL = 1024
HEAD_DIM = 64
D_A = 512
D_B = 512
A_GROUPS = 8
CHUNK = 128
N_HEADS = 8
N_KV = 2
GQA = 4
KV_W = 128
ROT_DIM = 16
ROPE_THETA = 500000.0
CMP_BLOCK = 32
CMP_STRIDE = 16
SEL_BLOCK = 64
SEL_SHIFT = 6
N_SELECT = 16
WINDOW = 512
Q_BLOCK = 128
PAGE_SIZE = 128
NORM_EPS = 1e-6
FORCE_SCORE = 1e4
N_FORCED = 3
NEG = -1e30
LOG2_E = 1.4426950408889634

LANES = 128
VMEM_LIMIT_BYTES = 56 * 1024 * 1024

C_U, C_V, C_ZA, C_Q = 0, 512, 1024, 1536
C_KC, C_VC, C_KS, C_VS, C_KW, C_VW = 2048, 2176, 2304, 2432, 2560, 2688
C_G, C_ZB, C_END = 2816, 2944, 3456
N_GATE_COLS = 3 * N_HEADS

ROW_TILE = 512
Q_SUB = 4
SCORE_PITCH_PAD = 128
SEL_TILE = 512
DECODE_GROUP = 4
WIN_SPAN = WINDOW + Q_BLOCK

_NT = (((1,), (1,)), ((), ()))


def _dot(a, b):
    return jnp.dot(a, b, preferred_element_type=F32)


def _dot_nt(a, b):
    return lax.dot_general(a, b, _NT, preferred_element_type=F32)


def _lane_iota(shape):
    return lax.broadcasted_iota(jnp.int32, shape, len(shape) - 1)


def _row_iota(shape):
    return lax.broadcasted_iota(jnp.int32, shape, len(shape) - 2)


def _rope(x, tab):
    c = tab[:, 0:LANES]
    s1 = tab[:, LANES:2 * LANES]
    s2 = tab[:, 2 * LANES:3 * LANES]
    return x * c + pltpu.roll(x, LANES - ROT_DIM // 2, 1) * s1 + pltpu.roll(x, ROT_DIM // 2, 1) * s2


def _rms_rows(x, g):
    ms = jnp.mean(x * x, axis=-1, keepdims=True)
    return x * lax.rsqrt(ms + NORM_EPS) * g


def _layer_norm_rows(v, g, b):
    mu = jnp.mean(v, axis=-1, keepdims=True)
    vc = v - mu
    var = jnp.mean(vc * vc, axis=-1, keepdims=True)
    return vc * lax.rsqrt(var + NORM_EPS) * g + b


def _head_slabs(x512):
    out = []
    rows = x512.shape[0]
    lane = _lane_iota((rows, LANES))
    lo = lane < HEAD_DIM
    for j in range(4):
        slab = x512[:, j * LANES:(j + 1) * LANES]
        swapped = pltpu.roll(slab, HEAD_DIM, 1)
        if j < 2:
            out.append(jnp.where(lo, slab, 0.0))
            out.append(jnp.where(lo, swapped, 0.0))
        else:
            out.append(jnp.where(lo, 0.0, swapped))
            out.append(jnp.where(lo, 0.0, slab))
    return out


def _project(x_ref, g_ref, w_refs, z_scr):
    hb = _rms_rows(x_ref[...], g_ref[...]).astype(BF16)
    wm_ref, wg_ref, wz_ref = w_refs
    z_scr[:, 0:C_G] = _dot(hb, wm_ref[...])
    z_scr[:, C_G:C_ZB] = _dot(hb, wg_ref[...])
    z_scr[:, C_ZB:C_END] = _dot(hb, wz_ref[...])


def _inproj_prompt_kernel(x_ref, g_ref, wm_ref, wg_ref, wz_ref, lng_ref, lnb_ref, ws_ref, bias_ref, rope_ref,
                          aout_ref, qt_ref, kc_ref, vc_ref,
                          kct_ref, vct_ref, kst_ref, vst_ref, kwt_ref, vwt_ref,
                          ksa_ref, kwb_ref, vsb_ref, vwb_ref, gate_ref, szb_ref,
                          z_scr, kstage_scr, vstage_scr, *, tm, tiles_per_batch):
    _project(x_ref, g_ref, (wm_ref, wg_ref, wz_ref), z_scr)
    tab = rope_ref[...]

    lane = _lane_iota((CHUNK, LANES))
    lo = lane < HEAD_DIM
    for c in range(tm // CHUNK):
        r0 = c * CHUNK
        u = z_scr[r0:r0 + CHUNK, C_U:C_U + D_A]
        v = z_scr[r0:r0 + CHUNK, C_V:C_V + D_A]
        za = z_scr[r0:r0 + CHUNK, C_ZA:C_ZA + D_A]
        vn = _layer_norm_rows(v, lng_ref[...], lnb_ref[...])
        parts = []
        for p in range(A_GROUPS // 2):
            vp = vn[:, p * LANES:(p + 1) * LANES]
            v_lo = jnp.where(lo, vp, 0.0).astype(BF16)
            v_hi = jnp.where(lo, 0.0, vp).astype(BF16)
            parts.append(_dot(ws_ref[2 * p], v_lo) + _dot(ws_ref[2 * p + 1], v_hi))
        s = jnp.concatenate(parts, axis=1) + bias_ref[...]
        aout_ref[r0:r0 + CHUNK, :] = (u * s * jax.nn.silu(za)).astype(BF16)

    q = z_scr[:, C_Q:C_Q + D_B]
    qr = jnp.concatenate(
        [_rope(q[:, j * LANES:(j + 1) * LANES], tab) for j in range(4)], axis=1) * (HEAD_DIM ** -0.5 * LOG2_E)
    slabs = _head_slabs(qr)
    kc = z_scr[:, C_KC:C_KC + KV_W]
    vc = z_scr[:, C_VC:C_VC + KV_W]
    for src, stage, dst in ((kc, kstage_scr, kc_ref), (vc, vstage_scr, vc_ref)):
        stage[...] = src
        for r in range(CMP_STRIDE):
            dst[:, r * KV_W:(r + 1) * KV_W] = stage[pl.ds(r, tm // CMP_STRIDE, stride=CMP_STRIDE), :]
    ks = _rope(z_scr[:, C_KS:C_KS + KV_W], tab)
    vs = z_scr[:, C_VS:C_VS + KV_W]
    kw = _rope(z_scr[:, C_KW:C_KW + KV_W], tab)
    vw = z_scr[:, C_VW:C_VW + KV_W]
    gates = jax.nn.sigmoid(z_scr[:, C_G:C_G + LANES])
    pos = (pl.program_id(0) % tiles_per_batch) * tm + _row_iota((tm, LANES))
    onehot = jnp.where((pos >> SEL_SHIFT) == _lane_iota((tm, LANES)), 1.0, 0.0)
    ksa_ref[:, 0:LANES] = ks.astype(BF16)
    ksa_ref[:, LANES:2 * LANES] = onehot.astype(BF16)
    kwb_ref[...] = kw.astype(BF16)
    for j in range(tm // LANES):
        r0, r1 = j * LANES, (j + 1) * LANES
        kct_ref[0, :, r0:r1] = kc[r0:r1].T
        vct_ref[0, :, r0:r1] = vc[r0:r1].T
        kst_ref[0, :, r0:r1] = ks[r0:r1].T
        kwt_ref[0, :, r0:r1] = kw[r0:r1].T
        vs_t, vw_t = vs[r0:r1].T, vw[r0:r1].T
        vst_ref[0, :, r0:r1] = vs_t
        vwt_ref[0, :, r0:r1] = vw_t
        vsb_ref[0, j] = vs_t.astype(BF16)
        vwb_ref[0, j] = vw_t.astype(BF16)
        for e in range(N_HEADS):
            qt_ref[0, j, :, e * Q_BLOCK:(e + 1) * Q_BLOCK] = slabs[e][r0:r1].T.astype(BF16)
        gate_ref[0, j] = gates[r0:r1].T
    szb_ref[...] = jax.nn.silu(z_scr[:, C_ZB:C_ZB + D_B])


def _inproj_prompt(x2, g, w, lng, lnb, ws_bf, bias_full, rope_tab, seq, tm=ROW_TILE):
    n = x2.shape[0]
    tiles_per_batch = seq // tm
    bsz = n // seq
    n_tok_tiles = seq // LANES
    row = lambda i: (i, 0)
    const2 = lambda i: (0, 0)
    rows2d = lambda width, dtype: (jax.ShapeDtypeStruct((n, width), dtype), pl.BlockSpec((tm, width), row))
    kv_t = (jax.ShapeDtypeStruct((bsz, KV_W, seq), F32),
            pl.BlockSpec((1, KV_W, tm), lambda i: (i // tiles_per_batch, 0, i % tiles_per_batch)))
    chunk_rows = (jax.ShapeDtypeStruct((n // CMP_STRIDE, CMP_STRIDE * KV_W), F32),
                  pl.BlockSpec((tm // CMP_STRIDE, CMP_STRIDE * KV_W), row))
    tiles = lambda r, c, dtype: (
        jax.ShapeDtypeStruct((bsz, n_tok_tiles, r, c), dtype),
        pl.BlockSpec((1, tm // LANES, r, c), lambda i: (i // tiles_per_batch, i % tiles_per_batch, 0, 0)))
    outs = [
        rows2d(D_A, BF16),
        tiles(KV_W, N_HEADS * Q_BLOCK, BF16),
        chunk_rows, chunk_rows,
        kv_t, kv_t, kv_t, kv_t, kv_t, kv_t,
        rows2d(2 * KV_W, BF16),
        rows2d(KV_W, BF16),
        tiles(KV_W, LANES, BF16),
        tiles(KV_W, LANES, BF16),
        tiles(LANES, LANES, F32),
        rows2d(D_B, F32),
    ]
    out_shapes = [o[0] for o in outs]
    out_specs = [o[1] for o in outs]
    in_specs = [
        pl.BlockSpec((tm, D_MODEL), row),
        pl.BlockSpec((1, D_MODEL), const2),
        pl.BlockSpec((D_MODEL, C_G), const2),
        pl.BlockSpec((D_MODEL, C_ZB - C_G), const2),
        pl.BlockSpec((D_MODEL, C_END - C_ZB), const2),
        pl.BlockSpec((1, D_A), const2),
        pl.BlockSpec((1, D_A), const2),
        pl.BlockSpec((A_GROUPS, CHUNK, CHUNK), lambda i: (0, 0, 0)),
        pl.BlockSpec((CHUNK, D_A), const2),
        pl.BlockSpec((tm, 3 * LANES), lambda i: (i % tiles_per_batch, 0)),
    ]
    return pl.pallas_call(
        functools.partial(_inproj_prompt_kernel, tm=tm, tiles_per_batch=tiles_per_batch),
        grid=(n // tm,),
        in_specs=in_specs,
        out_specs=out_specs,
        out_shape=out_shapes,
        scratch_shapes=[pltpu.VMEM((tm, C_END), F32), pltpu.VMEM((tm, KV_W), F32), pltpu.VMEM((tm, KV_W), F32)],
        compiler_params=pltpu.CompilerParams(
            dimension_semantics=("arbitrary",), vmem_limit_bytes=VMEM_LIMIT_BYTES),
        name="inproj_prompt",
    )(x2, g, *w, lng, lnb, ws_bf, bias_full, rope_tab)


def _inproj_decode_kernel(x_ref, g_ref, wm_ref, wg_ref, wz_ref, lng_ref, lnb_ref, wsv_ref, bsv_ref, rope_ref,
                          aout_ref, qa_ref, kc_ref, vc_ref, ks_ref, vs_ref, kw_ref, vw_ref,
                          gate_ref, szb_ref, vn_ref, z_scr):
    _project(x_ref, g_ref, (wm_ref, wg_ref, wz_ref), z_scr)
    tab = rope_ref[...]
    u = z_scr[:, C_U:C_U + D_A]
    v = z_scr[:, C_V:C_V + D_A]
    za = z_scr[:, C_ZA:C_ZA + D_A]
    vn = _layer_norm_rows(v, lng_ref[...], lnb_ref[...])
    vn_ref[...] = vn
    s = vn * wsv_ref[...] + bsv_ref[...]
    aout_ref[...] = (u * s * jax.nn.silu(za)).astype(BF16)

    q = z_scr[:, C_Q:C_Q + D_B]
    qr = jnp.concatenate(
        [_rope(q[:, j * LANES:(j + 1) * LANES], tab) for j in range(4)], axis=1) * (HEAD_DIM ** -0.5)
    for e, slab in enumerate(_head_slabs(qr)):
        qa_ref[:, e * LANES:(e + 1) * LANES] = slab.astype(BF16)
    kc_ref[...] = z_scr[:, C_KC:C_KC + KV_W]
    vc_ref[...] = z_scr[:, C_VC:C_VC + KV_W]
    ks_ref[...] = _rope(z_scr[:, C_KS:C_KS + KV_W], tab)
    vs_ref[...] = z_scr[:, C_VS:C_VS + KV_W]
    kw_ref[...] = _rope(z_scr[:, C_KW:C_KW + KV_W], tab)
    vw_ref[...] = z_scr[:, C_VW:C_VW + KV_W]
    gate_ref[...] = jax.nn.sigmoid(z_scr[:, C_G:C_G + LANES])
    szb = jax.nn.silu(z_scr[:, C_ZB:C_ZB + D_B])
    for e, slab in enumerate(_head_slabs(szb)):
        szb_ref[:, e * LANES:(e + 1) * LANES] = slab


def _inproj_decode(x2, g, w, lng, lnb, wsv, bsv, rope_tab):
    n = x2.shape[0]
    out_shapes = [
        jax.ShapeDtypeStruct((n, D_A), BF16),
        jax.ShapeDtypeStruct((n, N_HEADS * LANES), BF16),
        jax.ShapeDtypeStruct((n, KV_W), F32),
        jax.ShapeDtypeStruct((n, KV_W), F32),
        jax.ShapeDtypeStruct((n, KV_W), F32),
        jax.ShapeDtypeStruct((n, KV_W), F32),
        jax.ShapeDtypeStruct((n, KV_W), F32),
        jax.ShapeDtypeStruct((n, KV_W), F32),
        jax.ShapeDtypeStruct((n, LANES), F32),
        jax.ShapeDtypeStruct((n, N_HEADS * LANES), F32),
        jax.ShapeDtypeStruct((n, D_A), F32),
    ]
    return pl.pallas_call(
        _inproj_decode_kernel,
        out_shape=out_shapes,
        scratch_shapes=[pltpu.VMEM((n, C_END), F32)],
        compiler_params=pltpu.CompilerParams(vmem_limit_bytes=VMEM_LIMIT_BYTES),
        name="inproj_decode",
    )(x2, g, *w, lng, lnb, wsv, bsv, rope_tab)


def _compress_chunks(ch, pos_term, w_ref):
    ab = _dot(ch.astype(BF16), w_ref[...])
    return ab[:, 0:KV_W] + pltpu.roll(ab[:, KV_W:2 * KV_W], ch.shape[0] - 1, 0) + pos_term


def _position_term(pe_ref, w_ref):
    pe = pe_ref[...]
    hi = pe.astype(BF16)
    r1 = pe - hi.astype(F32)
    mid = r1.astype(BF16)
    lo = (r1 - mid.astype(F32)).astype(BF16)
    pw = _dot(hi, w_ref[...]) + _dot(mid, w_ref[...]) + _dot(lo, w_ref[...])
    return pw[0:1, 0:KV_W] + pw[1:2, KV_W:2 * KV_W]


def _compress_prompt_kernel(kc_ref, vc_ref, wk_ref, wv_ref, pek_ref, pev_ref, rope_ref, kcc_ref, vcct_ref):
    kcc = _compress_chunks(kc_ref[0], _position_term(pek_ref, wk_ref), wk_ref)
    kcc_ref[0] = _rope(kcc, rope_ref[...]).astype(BF16)
    vcc = _compress_chunks(vc_ref[0], _position_term(pev_ref, wv_ref), wv_ref)
    for j in range(vcc.shape[0] // LANES):
        vcct_ref[0, :, j * LANES:(j + 1) * LANES] = vcc[j * LANES:(j + 1) * LANES].T.astype(BF16)


def _compress_prompt(kc3, vc3, wk, wv, pek, pev, rope_tab):
    b, c, width = kc3.shape
    blk = pl.BlockSpec((1, c, width), lambda i: (i, 0, 0))
    const2 = lambda i: (0, 0)
    return pl.pallas_call(
        _compress_prompt_kernel,
        grid=(b,),
        in_specs=[blk, blk,
                  pl.BlockSpec(wk.shape, const2), pl.BlockSpec(wv.shape, const2),
                  pl.BlockSpec(pek.shape, const2), pl.BlockSpec(pev.shape, const2),
                  pl.BlockSpec(rope_tab.shape, const2)],
        out_specs=[pl.BlockSpec((1, c, KV_W), lambda i: (i, 0, 0)),
                   pl.BlockSpec((1, KV_W, c), lambda i: (i, 0, 0))],
        out_shape=[jax.ShapeDtypeStruct((b, c, KV_W), BF16), jax.ShapeDtypeStruct((b, KV_W, c), BF16)],
        compiler_params=pltpu.CompilerParams(
            dimension_semantics=("arbitrary",), vmem_limit_bytes=VMEM_LIMIT_BYTES),
        name="compress_prompt",
    )(kc3, vc3, wk, wv, pek, pev, rope_tab)


def _select_blocks(imp, kk, candidate, forced):
    assert FORCE_SCORE > 2 * GQA
    start = jnp.where(candidate, jnp.where(forced, -jnp.inf, imp), -jnp.inf)
    score = start
    for _ in range(N_SELECT - N_FORCED):
        cm = jnp.max(score, axis=0, keepdims=True)
        first = jnp.min(jnp.where(score == cm, kk, LANES), axis=0, keepdims=True)
        score = jnp.where(kk == first, -jnp.inf, score)
    taken = (score == -jnp.inf) & (start > -jnp.inf)
    return candidate & (forced | taken)


def _softmax_rows(s):
    mx = jnp.max(s, axis=1, keepdims=True)
    ex = jnp.exp(s - mx)
    return ex * (1.0 / jnp.sum(ex, axis=1, keepdims=True))


def _tile_heads(x):
    return jnp.concatenate([x] * N_HEADS, axis=1)


def _values_by_kv_head(vt, p):
    half = GQA * Q_BLOCK
    return jnp.concatenate(
        [_dot(vt[h * HEAD_DIM:(h + 1) * HEAD_DIM], p[:, h * half:(h + 1) * half]) for h in range(N_KV)], axis=1)


def _attn_prompt_kernel(*refs):
    def body(sub, carry):
        _attn_query_block(sub, *refs)
        return carry

    lax.fori_loop(0, Q_SUB, body, 0)


def _attn_query_block(sub, qt_ref, gate_ref, szb_ref, kcc_ref, vcct_ref, ovlt_ref,
                      ksa_ref, vsb_ref, kwb_ref, vwb_ref, out_ref,
                      qaug_scr, acc_scr, sa_scr, sb_scr):
    i_blk = pl.program_id(1) * Q_SUB + sub
    q0 = i_blk * Q_BLOCK
    cols = N_HEADS * Q_BLOCK
    qt = qt_ref[0, sub]
    out_rows = pl.ds(pl.multiple_of(sub * Q_BLOCK, Q_BLOCK), Q_BLOCK)

    n_cmp = kcc_ref.shape[1]

    def compressed(extent):
        t_c = q0 + _lane_iota((extent, Q_BLOCK))
        seen = (_row_iota((extent, Q_BLOCK)) * CMP_STRIDE + (CMP_BLOCK - 1)) <= t_c
        s = _dot(kcc_ref[0, 0:extent, :], qt) + _tile_heads(jnp.where(seen, 0.0, NEG))
        mx = jnp.max(s, axis=0, keepdims=True)
        ex = jnp.exp2(s - mx)
        inv = jnp.where(mx > 0.5 * NEG, 1.0 / jnp.sum(ex, axis=0, keepdims=True), 0.0)
        p_c = ex * inv
        imps = []
        for h in range(N_KV):
            c0 = h * GQA * Q_BLOCK
            p_sum = (p_c[:, c0:c0 + Q_BLOCK] + p_c[:, c0 + Q_BLOCK:c0 + 2 * Q_BLOCK]
                     + p_c[:, c0 + 2 * Q_BLOCK:c0 + 3 * Q_BLOCK] + p_c[:, c0 + 3 * Q_BLOCK:c0 + 4 * Q_BLOCK])
            imps.append(_dot(ovlt_ref[:, 0:extent], p_sum.astype(BF16)))
        return (_values_by_kv_head(vcct_ref[0, :, 0:extent], p_c.astype(BF16)), *imps)

    n_need = (q0 + Q_BLOCK) // CMP_STRIDE
    extents = list(range(LANES, n_cmp + 1, LANES))
    o_c, *imp_t = lax.switch(
        jnp.minimum((n_need - 1) // LANES, len(extents) - 1),
        [functools.partial(compressed, ext) for ext in extents])

    w0 = pl.multiple_of(jnp.maximum(q0 - WINDOW, 0), Q_BLOCK)
    wj0 = jnp.maximum(i_blk - WINDOW // Q_BLOCK, 0)
    kwin = kwb_ref[0, pl.ds(w0, WIN_SPAN), :]
    vwt = jnp.concatenate([vwb_ref[0, wj0 + i] for i in range(WIN_SPAN // LANES)], axis=1)
    kp = w0 + _row_iota((WIN_SPAN, Q_BLOCK))
    tw = q0 + _lane_iota((WIN_SPAN, Q_BLOCK))
    in_win = jnp.where(kp <= tw, jnp.where(kp > tw - WINDOW, 0.0, NEG), NEG)
    sw = _dot(kwin, qt) + _tile_heads(in_win)
    exw = jnp.exp2(sw - jnp.max(sw, axis=0, keepdims=True))
    o_w = _values_by_kv_head(vwt, exw.astype(BF16)) * (1.0 / jnp.sum(exw, axis=0, keepdims=True))

    kk = _row_iota((LANES, Q_BLOCK))
    tq = q0 + _lane_iota((LANES, Q_BLOCK))
    valid = kk * SEL_BLOCK <= tq
    cur = tq >> SEL_SHIFT
    forced = (kk == 0) | (kk == cur) | (kk == cur - 1)
    both = lambda a: jnp.concatenate([a] * N_KV, axis=1)
    picked = _select_blocks(jnp.concatenate(imp_t, axis=1), both(kk), both(valid), both(forced))
    qaug_scr[0:LANES, :] = qt
    for h in range(N_KV):
        selneg = jnp.where(picked[:, h * Q_BLOCK:(h + 1) * Q_BLOCK], 0.0, NEG).astype(BF16)
        for g in range(GQA):
            c0 = (h * GQA + g) * Q_BLOCK
            qaug_scr[LANES:2 * LANES, c0:c0 + Q_BLOCK] = selneg

    acc_scr[...] = jnp.zeros(acc_scr.shape, F32)
    tiles_per_step = SEL_TILE // LANES

    def score(step, dst):
        t0 = pl.multiple_of(step * SEL_TILE, SEL_TILE)
        dst[:, 0:cols] = _dot(ksa_ref[0, pl.ds(t0, SEL_TILE), :], qaug_scr[...])

    def attend(src, step, m_prev, l_prev, causal_rows=None):
        rows = SEL_TILE if causal_rows is None else causal_rows
        vt = jnp.concatenate([vsb_ref[0, step * tiles_per_step + i] for i in range(rows // LANES)], axis=1)
        sc = src[0:rows, 0:cols]
        if causal_rows is not None:
            tok = step * SEL_TILE + _row_iota((rows, Q_BLOCK))
            sc = sc + _tile_heads(jnp.where(tok <= q0 + _lane_iota((rows, Q_BLOCK)), 0.0, NEG))
        m_next = jnp.maximum(m_prev, jnp.max(sc, axis=0, keepdims=True))
        p = jnp.exp2(sc - m_next)
        alpha = jnp.exp2(m_prev - m_next)
        l_next = alpha * l_prev + jnp.sum(p, axis=0, keepdims=True)
        acc_scr[...] = alpha * acc_scr[...] + _values_by_kv_head(vt, p.astype(BF16))
        return m_next, l_next

    n_full = q0 // SEL_TILE
    score(0, sa_scr)

    def pair(jj, carry):
        m_c, l_c = carry
        score(2 * jj + 1, sb_scr)
        m_c, l_c = attend(sa_scr, 2 * jj, m_c, l_c)
        score(2 * jj + 2, sa_scr)
        return attend(sb_scr, 2 * jj + 1, m_c, l_c)

    m_run, l_run = lax.fori_loop(
        0, n_full // 2, pair,
        (jnp.full((1, cols), -jnp.inf, F32), jnp.zeros((1, cols), F32)))

    def last_step(src, m_c, l_c):
        extents = list(range(Q_BLOCK, SEL_TILE + 1, Q_BLOCK))
        return lax.switch(
            (q0 % SEL_TILE) // Q_BLOCK,
            [lambda m, l, ext=ext: attend(src, n_full, m, l, causal_rows=ext)[1] for ext in extents],
            m_c, l_c)

    def odd_tail(m_c, l_c):
        score(n_full, sb_scr)
        m_c, l_c = attend(sa_scr, n_full - 1, m_c, l_c)
        return last_step(sb_scr, m_c, l_c)

    def even_tail(m_c, l_c):
        return last_step(sa_scr, m_c, l_c)

    l_run = lax.cond(n_full % 2 == 1, odd_tail, even_tail, m_run, l_run)

    gt = gate_ref[0, sub]
    o_s = acc_scr[...] * (1.0 / l_run)
    heads = []
    for e in range(N_HEADS):
        c0 = e * Q_BLOCK
        heads.append(gt[3 * e:3 * e + 1, :] * o_c[:, c0:c0 + Q_BLOCK]
                     + gt[3 * e + 1:3 * e + 2, :] * o_s[:, c0:c0 + Q_BLOCK]
                     + gt[3 * e + 2:3 * e + 3, :] * o_w[:, c0:c0 + Q_BLOCK])
    for j in range(N_HEADS // 2):
        slab = jnp.concatenate([heads[2 * j], heads[2 * j + 1]], axis=0).T
        out_ref[0, out_rows, j * LANES:(j + 1) * LANES] = (
            slab * szb_ref[0, out_rows, j * LANES:(j + 1) * LANES]).astype(BF16)


def _attn_prompt(qt4, gate4, szb3, kcc, vcct, ovlt, ksa3, vsb4, kwb3, vwb4):
    b, nq = qt4.shape[0], qt4.shape[1]
    t = nq * Q_BLOCK
    cols = N_HEADS * Q_BLOCK
    assert nq % Q_SUB == 0
    qtile = lambda a: pl.BlockSpec((1, Q_SUB) + a.shape[2:], lambda bi, i: (bi, i, 0, 0))
    rows = pl.BlockSpec((1, Q_SUB * Q_BLOCK, D_B), lambda bi, i: (bi, i, 0))
    seq = lambda a: pl.BlockSpec((1,) + a.shape[1:], lambda bi, i, nd=a.ndim: (bi,) + (0,) * (nd - 1))
    return pl.pallas_call(
        _attn_prompt_kernel,
        grid=(b, nq // Q_SUB),
        in_specs=[qtile(qt4), qtile(gate4), rows,
                  seq(kcc), seq(vcct), pl.BlockSpec(ovlt.shape, lambda bi, i: (0, 0)),
                  seq(ksa3), seq(vsb4), seq(kwb3), seq(vwb4)],
        out_specs=rows,
        out_shape=jax.ShapeDtypeStruct((b, t, D_B), BF16),
        scratch_shapes=[pltpu.VMEM((2 * LANES, cols), BF16),
                        pltpu.VMEM((HEAD_DIM, cols), F32),
                        pltpu.VMEM((SEL_TILE, cols + SCORE_PITCH_PAD), F32),
                        pltpu.VMEM((SEL_TILE, cols + SCORE_PITCH_PAD), F32)],
        compiler_params=pltpu.CompilerParams(
            dimension_semantics=("arbitrary", "arbitrary"), vmem_limit_bytes=VMEM_LIMIT_BYTES),
        name="attn_prompt",
    )(qt4, gate4, szb3, kcc, vcct, ovlt, ksa3, vsb4, kwb3, vwb4)


def _outproj_kernel(x_ref, a_ref, m_ref, wa_ref, wb_ref, fg_ref, y_ref):
    y = x_ref[...] + _dot(a_ref[...], wa_ref[...]) + _dot(m_ref[...].astype(BF16), wb_ref[...])
    y_ref[...] = _rms_rows(y, fg_ref[...])


def _outproj(x2, a_out, mix_b, wa, wb, fg, tm):
    n = x2.shape[0]
    row = lambda i: (i, 0)
    const2 = lambda i: (0, 0)
    return pl.pallas_call(
        _outproj_kernel,
        grid=(n // tm,),
        in_specs=[pl.BlockSpec((tm, D_MODEL), row),
                  pl.BlockSpec((tm, a_out.shape[1]), row),
                  pl.BlockSpec((tm, mix_b.shape[1]), row),
                  pl.BlockSpec(wa.shape, const2), pl.BlockSpec(wb.shape, const2),
                  pl.BlockSpec((1, D_MODEL), const2)],
        out_specs=pl.BlockSpec((tm, D_MODEL), row),
        out_shape=jax.ShapeDtypeStruct((n, D_MODEL), F32),
        compiler_params=pltpu.CompilerParams(
            dimension_semantics=("arbitrary",), vmem_limit_bytes=VMEM_LIMIT_BYTES),
        name="outproj",
    )(x2, a_out, mix_b, wa, wb, fg)


def _page_copies(pt_ref, pool_ref, buf_ref, sem_ref, step, slot):
    group, n_pages = buf_ref.shape[1], buf_ref.shape[2]
    return [pltpu.make_async_copy(pool_ref.at[pt_ref[step * group + g, p]], buf_ref.at[slot, g, p],
                                  sem_ref.at[slot])
            for g in range(group) for p in range(n_pages)]


def _gather_pages(pt_ref, pools, bufs, sems):
    b = pl.program_id(0)
    slot = lax.rem(b, 2)

    def start(step, sl):
        for pool, buf, sem in zip(pools, bufs, sems):
            for cp in _page_copies(pt_ref, pool, buf, sem, step, sl):
                cp.start()

    @pl.when(b == 0)
    def _():
        start(0, 0)

    @pl.when(b + 1 < pl.num_programs(0))
    def _():
        start(b + 1, 1 - slot)

    for pool, buf, sem in zip(pools, bufs, sems):
        for cp in _page_copies(pt_ref, pool, buf, sem, b, slot):
            cp.wait()
    return slot


def _decode_cmp_kernel(pt_ref, ck_hbm, cv_hbm, q_ref, wk_ref, wv_ref, pek_ref, pev_ref, rope_ref, ovl_ref,
                       oc_ref, psum_ref, kbuf, vbuf, ksem, vsem, ktok_scr, vtok_scr, kpos_scr, vpos_scr):
    slot = _gather_pages(pt_ref, (ck_hbm, cv_hbm), (kbuf, vbuf), (ksem, vsem))
    group, n_pages = kbuf.shape[1], kbuf.shape[2]

    @pl.when(pl.program_id(0) == 0)
    def _():
        kpos_scr[...] = _position_term(pek_ref, wk_ref)
        vpos_scr[...] = _position_term(pev_ref, wv_ref)

    def chunk_rows(buf, tok_scr, g):
        for p in range(n_pages):
            tok_scr[g, p * PAGE_SIZE:(p + 1) * PAGE_SIZE, :] = buf[slot, g, p].T
        n = tok_scr.shape[1] // CMP_STRIDE
        return jnp.concatenate(
            [tok_scr[g, pl.ds(r, n, stride=CMP_STRIDE), :] for r in range(CMP_STRIDE)], axis=1)

    for g in range(group):
        kch = chunk_rows(kbuf, ktok_scr, g)
        vch = chunk_rows(vbuf, vtok_scr, g)
        kcc = _rope(_compress_chunks(kch, kpos_scr[...], wk_ref), rope_ref[...]).astype(BF16)
        vcc = _compress_chunks(vch, vpos_scr[...], wv_ref).astype(BF16)
        n_chunks = kch.shape[0]
        q = q_ref[g]
        rows = q.shape[0]
        s = _dot_nt(q, kcc)
        seen = _lane_iota((rows, n_chunks)) < n_chunks - 1
        p_c = jnp.where(seen, _softmax_rows(jnp.where(seen, s, NEG)), 0.0)
        oc_ref[g] = _dot(p_c.astype(BF16), vcc)[0:N_HEADS]
        ps0 = jnp.sum(p_c[0:GQA], axis=0, keepdims=True)
        ps1 = jnp.sum(p_c[GQA:2 * GQA], axis=0, keepdims=True)
        psum = jnp.where(_row_iota((rows, n_chunks)) < N_HEADS // 2, ps0, ps1)
        psum_ref[g] = _dot(psum.astype(BF16), ovl_ref[...])[0:N_HEADS]


def _decode_cmp(page_table, ck_pool, cv_pool, q16, wk, wv, pek, pev, rope_tab, ovl):
    nb, n_pages = page_table.shape
    group = DECODE_GROUP
    assert nb % group == 0
    hbm = pl.BlockSpec(memory_space=pl.ANY)
    page_buf = pltpu.VMEM((2, group, n_pages, KV_W, PAGE_SIZE), F32)
    tok_buf = pltpu.VMEM((group, n_pages * PAGE_SIZE, KV_W), F32)
    const2 = lambda b, pt: (0, 0)
    per_b = lambda a: pl.BlockSpec((group,) + a.shape[1:], lambda b, pt: (b, 0, 0))
    in_specs = [hbm, hbm, per_b(q16),
                pl.BlockSpec(wk.shape, const2), pl.BlockSpec(wv.shape, const2),
                pl.BlockSpec(pek.shape, const2), pl.BlockSpec(pev.shape, const2),
                pl.BlockSpec(rope_tab.shape, const2), pl.BlockSpec(ovl.shape, const2)]
    out_blk = pl.BlockSpec((group, N_HEADS, LANES), lambda b, pt: (b, 0, 0))
    grid_spec = pltpu.PrefetchScalarGridSpec(
        num_scalar_prefetch=1, grid=(nb // group,), in_specs=in_specs, out_specs=[out_blk, out_blk],
        scratch_shapes=[page_buf, page_buf, pltpu.SemaphoreType.DMA((2,)), pltpu.SemaphoreType.DMA((2,)),
                        tok_buf, tok_buf, pltpu.VMEM((1, KV_W), F32), pltpu.VMEM((1, KV_W), F32)])
    return pl.pallas_call(
        _decode_cmp_kernel,
        grid_spec=grid_spec,
        out_shape=[jax.ShapeDtypeStruct((nb, N_HEADS, LANES), F32)] * 2,
        compiler_params=pltpu.CompilerParams(
            dimension_semantics=("arbitrary",), vmem_limit_bytes=VMEM_LIMIT_BYTES),
        name="decode_cmp",
    )(page_table, ck_pool, cv_pool, q16, wk, wv, pek, pev, rope_tab, ovl)


def _decode_topk_kernel(imp_ref, sel_ref, *, n_blk, cur):
    n = imp_ref.shape[0]
    tiles = [imp_ref[i * LANES:(i + 1) * LANES, :].T for i in range(n // LANES)]
    imp_t = jnp.concatenate(tiles, axis=1)
    kk = _row_iota(imp_t.shape)
    forced = (kk == 0) | (kk == cur) | (kk == cur - 1)
    selneg = jnp.where(_select_blocks(imp_t, kk, kk < n_blk, forced), 0.0, NEG)
    for i in range(n // LANES):
        sel_ref[i * LANES:(i + 1) * LANES, :] = selneg[:, i * LANES:(i + 1) * LANES].T


def _decode_topk(imp2, n_blk, cur):
    return pl.pallas_call(
        functools.partial(_decode_topk_kernel, n_blk=n_blk, cur=cur),
        out_shape=jax.ShapeDtypeStruct(imp2.shape, F32),
        compiler_params=pltpu.CompilerParams(vmem_limit_bytes=VMEM_LIMIT_BYTES),
        name="decode_topk",
    )(imp2)


def _decode_attend(s, q32, vals_t_bf, k_new, v_new, bias_new):
    kn = k_new.astype(BF16).astype(F32)
    s_new = jnp.sum(q32 * kn, axis=1, keepdims=True) + bias_new
    mx = jnp.maximum(jnp.max(s, axis=1, keepdims=True), s_new)
    ex = jnp.exp(s - mx)
    ex_new = jnp.exp(s_new - mx)
    inv = 1.0 / (jnp.sum(ex, axis=1, keepdims=True) + ex_new)
    p = (ex * inv).astype(BF16)
    p_new = (ex_new * inv).astype(BF16).astype(F32)
    return _dot_nt(p, vals_t_bf) + p_new * v_new.astype(BF16).astype(F32)


P_Q, P_SEL, P_OC, P_SZB, P_NEW, P_GATE, P_ROWS = 0, 16, 32, 40, 48, 56, 64


def _row_to_col(row):
    n = row.shape[1]
    diag = _row_iota((n, n)) == _lane_iota((n, n))
    return jnp.sum(jnp.where(diag, row, 0.0), axis=1, keepdims=True)


def _decode_sel_kernel(pt_ref, ks_hbm, vs_hbm, ckw_ref, cvw_ref, pack_ref, onehot_ref,
                       mix_ref, okw_ref, ovw_ref, kbuf, vbuf, ksem, vsem, *, cur):
    slot = _gather_pages(pt_ref, (ks_hbm, vs_hbm), (kbuf, vbuf), (ksem, vsem))
    group, n_pages = kbuf.shape[1], kbuf.shape[2]
    for g in range(group):
        pk = pack_ref[g]
        q32 = pk[P_Q:P_Q + 2 * N_HEADS]
        q = q32.astype(BF16)
        sel = pk[P_SEL:P_SEL + 2 * N_HEADS].astype(BF16)
        rows = q.shape[0]
        ks_new, vs_new = pk[P_NEW:P_NEW + 1], pk[P_NEW + 1:P_NEW + 2]
        kw_new, vw_new = pk[P_NEW + 2:P_NEW + 3], pk[P_NEW + 3:P_NEW + 4]

        kt = jnp.concatenate([kbuf[slot, g, p] for p in range(n_pages)], axis=1).astype(BF16)
        vt = jnp.concatenate([vbuf[slot, g, p] for p in range(n_pages)], axis=1).astype(BF16)
        kaug = jnp.concatenate([kt, onehot_ref[...]], axis=0)
        qaug = jnp.concatenate([q, sel], axis=1)
        s_sel = _dot(qaug, kaug)
        o_s = _decode_attend(s_sel, q32, vt, ks_new, vs_new, sel.astype(F32)[:, cur:cur + 1])

        kwin = ckw_ref[g]
        vwin = cvw_ref[g]
        keep = kwin.shape[1]
        visible = _lane_iota((rows, keep)) > keep - WINDOW
        s_win = jnp.where(visible, _dot(q, kwin.astype(BF16)), NEG)
        o_w = _decode_attend(s_win, q32, vwin.astype(BF16), kw_new, vw_new, 0.0)

        gates = pk[P_GATE:P_GATE + N_HEADS]
        o = (gates[:, 0:1] * pk[P_OC:P_OC + N_HEADS] + gates[:, 1:2] * o_s[0:N_HEADS]
             + gates[:, 2:3] * o_w[0:N_HEADS])
        mix_ref[g] = o * pk[P_SZB:P_SZB + N_HEADS]

        last = _lane_iota((KV_W, keep)) == keep - 1
        okw_ref[g] = jnp.where(last, _row_to_col(kw_new), pltpu.roll(kwin, keep - 1, 1))
        ovw_ref[g] = jnp.where(last, _row_to_col(vw_new), pltpu.roll(vwin, keep - 1, 1))


def _decode_sel(page_table, ks_pool, vs_pool, ckw, cvw, pack, onehot, cur):
    nb, n_pages = page_table.shape
    group = DECODE_GROUP
    assert nb % group == 0
    keep = ckw.shape[2]
    hbm = pl.BlockSpec(memory_space=pl.ANY)
    page_buf = pltpu.VMEM((2, group, n_pages, KV_W, PAGE_SIZE), F32)
    per_b = lambda a: pl.BlockSpec((group,) + a.shape[1:], lambda b, pt: (b, 0, 0))
    in_specs = [hbm, hbm, per_b(ckw), per_b(cvw), per_b(pack),
                pl.BlockSpec(onehot.shape, lambda b, pt: (0, 0))]
    out_specs = [pl.BlockSpec((group, N_HEADS, LANES), lambda b, pt: (b, 0, 0)),
                 pl.BlockSpec((group, KV_W, keep), lambda b, pt: (b, 0, 0)),
                 pl.BlockSpec((group, KV_W, keep), lambda b, pt: (b, 0, 0))]
    grid_spec = pltpu.PrefetchScalarGridSpec(
        num_scalar_prefetch=1, grid=(nb // group,), in_specs=in_specs, out_specs=out_specs,
        scratch_shapes=[page_buf, page_buf, pltpu.SemaphoreType.DMA((2,)), pltpu.SemaphoreType.DMA((2,))])
    return pl.pallas_call(
        functools.partial(_decode_sel_kernel, cur=cur),
        grid_spec=grid_spec,
        out_shape=[jax.ShapeDtypeStruct((nb, N_HEADS, LANES), F32),
                   jax.ShapeDtypeStruct((nb, KV_W, keep), F32),
                   jax.ShapeDtypeStruct((nb, KV_W, keep), F32)],
        compiler_params=pltpu.CompilerParams(
            dimension_semantics=("arbitrary",), vmem_limit_bytes=VMEM_LIMIT_BYTES),
        name="decode_sel",
    )(page_table, ks_pool, vs_pool, ckw, cvw, pack, onehot)


def _rope_table(pos):
    half = ROT_DIM // 2
    inv = jnp.power(jnp.float32(ROPE_THETA), -jnp.arange(half, dtype=F32) / half)
    ang = pos.astype(F32)[:, None] * inv[None, :]
    cs = jnp.concatenate([jnp.cos(ang), jnp.sin(ang)], axis=1)
    lane = jnp.arange(3 * LANES)
    seg, l64 = lane // LANES, lane % HEAD_DIM
    src = jnp.where(seg == 0, l64 % half, half + l64 % half)
    sign = jnp.where(seg == 0, l64 < ROT_DIM,
                     jnp.where(seg == 1, l64 < half, (l64 >= half) & (l64 < ROT_DIM))).astype(F32)
    sign = jnp.where(seg == 1, -sign, sign)
    place = (jnp.arange(ROT_DIM)[:, None] == src[None, :]).astype(F32) * sign[None, :]
    ones = ((seg == 0) & (l64 >= ROT_DIM)).astype(F32)
    return jnp.dot(cs, place, precision=lax.Precision.HIGHEST) + ones[None, :]


def _compress_weights(w_c, pe_c):
    w4 = w_c.reshape(2, CMP_STRIDE, HEAD_DIM, HEAD_DIM)
    wb = jnp.einsum('arde,hg->rhdage', w4, jnp.eye(N_KV, dtype=w_c.dtype))
    wb = wb.reshape(CMP_STRIDE * KV_W, 2 * KV_W).astype(BF16)
    pe = jnp.broadcast_to(pe_c.reshape(2, CMP_STRIDE, 1, HEAD_DIM), (2, CMP_STRIDE, N_KV, HEAD_DIM))
    return wb, jnp.pad(pe.reshape(2, CMP_STRIDE * KV_W), ((0, 14), (0, 0)))


def _overlap_matrix(n_rows, n_blk):
    cs = jnp.arange(n_rows, dtype=jnp.int32)[:, None] * CMP_STRIDE
    bs = jnp.arange(LANES, dtype=jnp.int32)[None, :] * SEL_BLOCK
    hit = (cs < bs + SEL_BLOCK) & (cs + CMP_BLOCK > bs) & (jnp.arange(LANES)[None, :] < n_blk)
    return hit.astype(BF16)


def kernel(x_prompt, x_sample, cache_k_cmp, cache_v_cmp, cache_k_sel, cache_v_sel, cache_k_win,
           cache_v_win, page_table, norm_g, w_in, ln_v_g, ln_v_b, w_s, b_s, w_ck, pe_ck, w_cv,
           pe_cv, w_out, final_g):
    depth = norm_g.shape[0]
    assert depth == 1
    bsz, seq, _ = x_prompt.shape
    nb, dec_seq, _ = x_sample.shape
    assert dec_seq == 1 and seq % SEL_TILE == 0 and seq >= WIN_SPAN and (seq // CMP_STRIDE) % LANES == 0
    n_pages = page_table.shape[1]
    past = n_pages * PAGE_SIZE
    keep = cache_k_win.shape[2]
    assert keep == WINDOW and past % SEL_BLOCK == 0

    w = w_in[0]
    w_pad = (w[:, :C_G].astype(BF16),
             jnp.pad(w[:, C_G:C_G + N_GATE_COLS], ((0, 0), (0, LANES - N_GATE_COLS))).astype(BF16),
             w[:, C_G + N_GATE_COLS:].astype(BF16))
    g_row = norm_g[0][None, :]
    lng = ln_v_g[0][None, :]
    lnb = ln_v_b[0][None, :]
    fg = final_g[None, :]
    wk_c, pek = _compress_weights(w_ck[0], pe_ck[0])
    wv_c, pev = _compress_weights(w_cv[0], pe_cv[0])
    w_o = w_out[0].astype(BF16)
    wo_a, wo_b = w_o[:D_A], w_o[D_A:]

    tril = jnp.tril(jnp.ones((CHUNK, CHUNK), w_s.dtype))
    ws_bf = (w_s[0] * tril).astype(BF16)
    bias_full = jnp.repeat(b_s[0].T, HEAD_DIM, axis=1)
    xp2 = x_prompt.reshape(bsz * seq, D_MODEL)
    (a_out, qt, kc, vc, kc_t, vc_t, ks_t, vs_t, kw_t, vw_t, ksa, kwb, vsb, vwb, gates_t, szb) = _inproj_prompt(
        xp2, g_row, w_pad, lng, lnb, ws_bf, bias_full,
        _rope_table(jnp.arange(seq, dtype=jnp.int32)), seq)
    n_chunks = seq // CMP_STRIDE
    kcc, vcc_t = _compress_prompt(
        kc.reshape(bsz, n_chunks, CMP_STRIDE * KV_W), vc.reshape(bsz, n_chunks, CMP_STRIDE * KV_W),
        wk_c, wv_c, pek, pev,
        _rope_table(jnp.arange(n_chunks, dtype=jnp.int32) * CMP_STRIDE))
    b3 = lambda a: a.reshape(bsz, seq, a.shape[-1])
    mix_p = _attn_prompt(qt, gates_t, b3(szb), kcc, vcc_t,
                         _overlap_matrix(n_chunks, seq // SEL_BLOCK).T,
                         b3(ksa), vsb, b3(kwb), vwb)
    y_prompt = _outproj(xp2, a_out, mix_p.reshape(bsz * seq, D_B), wo_a, wo_b, fg, ROW_TILE)
    y_prompt = y_prompt.reshape(bsz, seq, D_MODEL)
    st = lambda a: jnp.transpose(a.reshape(a.shape[0], N_KV, HEAD_DIM, a.shape[2]), (0, 3, 1, 2))[None]
    keep_p = min(WINDOW, seq)

    xs2 = x_sample.reshape(nb, D_MODEL)
    wsv = jnp.repeat(w_s[0][:, 0, 0], HEAD_DIM)[None, :]
    bsv = jnp.repeat(b_s[0][:, 0], HEAD_DIM)[None, :]
    (a_s, qa_s, kc_s, vc_s, ks_s, vs_s, kw_s, vw_s, gates_s, szb_s, vn_s) = _inproj_decode(
        xs2, g_row, w_pad, lng, lnb, wsv, bsv,
        _rope_table(jnp.full((nb,), past, dtype=jnp.int32)))
    q16 = jnp.pad(qa_s.reshape(nb, N_HEADS, LANES), ((0, 0), (0, N_HEADS), (0, 0)))
    d_chunks = past // CMP_STRIDE
    n_blk = -(-(past + 1) // SEL_BLOCK)
    cur = past // SEL_BLOCK
    kv_t = lambda c: jnp.transpose(c[0], (0, 2, 3, 1)).reshape(c.shape[1], KV_W, c.shape[2])
    oc_s, imp_s = _decode_cmp(
        page_table, kv_t(cache_k_cmp), kv_t(cache_v_cmp),
        q16, wk_c, wv_c, pek, pev,
        _rope_table(jnp.arange(d_chunks, dtype=jnp.int32) * CMP_STRIDE),
        _overlap_matrix(d_chunks, n_blk))
    selneg = _decode_topk(imp_s[:, ::GQA, :].reshape(nb * N_KV, LANES), n_blk, cur)
    sel16 = jnp.pad(jnp.repeat(selneg.reshape(nb, N_KV, LANES), GQA, axis=1), ((0, 0), (0, N_HEADS), (0, 0)))
    tok_blk = jnp.arange(past, dtype=jnp.int32)[None, :] // SEL_BLOCK
    onehot = (tok_blk == jnp.arange(LANES, dtype=jnp.int32)[:, None]).astype(BF16)
    gate_rows = jnp.pad(gates_s[:, :N_GATE_COLS].reshape(nb, N_HEADS, 3), ((0, 0), (0, 0), (0, LANES - 3)))
    new_rows = jnp.pad(jnp.stack([ks_s, vs_s, kw_s, vw_s], axis=1), ((0, 0), (0, P_GATE - P_NEW - 4), (0, 0)))
    pack = jnp.concatenate(
        [q16.astype(F32), sel16, oc_s, szb_s.reshape(nb, N_HEADS, LANES), new_rows, gate_rows], axis=1)
    assert pack.shape[1] == P_ROWS
    mix_s, okw, ovw = _decode_sel(
        page_table, kv_t(cache_k_sel), kv_t(cache_v_sel), kv_t(cache_k_win), kv_t(cache_v_win),
        pack, onehot, cur)
    wo_slab = jnp.zeros((N_HEADS, N_KV, HEAD_DIM, D_MODEL), BF16)
    wo_heads = wo_b.reshape(N_HEADS, HEAD_DIM, D_MODEL)
    for e in range(N_HEADS):
        wo_slab = wo_slab.at[e, e // GQA].set(wo_heads[e])
    y_sample = _outproj(xs2, a_s, mix_s.reshape(nb, N_HEADS * LANES), wo_a,
                        wo_slab.reshape(N_HEADS * LANES, D_MODEL), fg, nb)
    y_sample = y_sample.reshape(nb, 1, D_MODEL)
    ss = lambda a: a.reshape(1, nb, 1, N_KV, HEAD_DIM)

    return (y_prompt, y_sample,
            st(kc_t), st(vc_t), st(ks_t), st(vs_t),
            st(kw_t[:, :, seq - keep_p:]), st(vw_t[:, :, seq - keep_p:]),
            ss(kc_s), ss(vc_s), ss(ks_s), ss(vs_s),
            st(okw), st(ovw),
            vn_s.reshape(1, nb, 1, D_A))
```

```python
import functools

import jax
import jax.numpy as jnp
from jax import lax
from jax.experimental import pallas as pl
from jax.experimental.pallas import tpu as pltpu

F32 = jnp.float32
BF16 = jnp.bfloat16

D_MODEL = 1024
HEAD_DIM = 64
D_A = 512
D_B = 512
A_GROUPS = 8
CHUNK = 128
N_HEADS = 8
N_KV = 2
GQA = 4
KV_W = 128
ROT_DIM = 16
ROPE_THETA = 500000.0
CMP_BLOCK = 32
CMP_STRIDE = 16
SEL_BLOCK = 64
SEL_SHIFT = 6
N_SELECT = 16
WINDOW = 512
Q_BLOCK = 128
PAGE_SIZE = 128
NORM_EPS = 1e-6
FORCE_SCORE = 1e4
N_FORCED = 3
NEG = -1e30
LOG2_E = 1.4426950408889634

LANES = 128
VMEM_LIMIT_BYTES = 56 * 1024 * 1024

C_U, C_V, C_ZA, C_Q = 0, 512, 1024, 1536
C_KC, C_VC, C_KS, C_VS, C_KW, C_VW = 2048, 2176, 2304, 2432, 2560, 2688
C_G, C_ZB, C_END = 2816, 2944, 3456
N_GATE_COLS = 3 * N_HEADS

ROW_TILE = 512
OUT_ROW_TILE = 1024
Q_SUB = 4
SEL_TILE = 512
DECODE_GROUP = 4
WIN_SPAN = WINDOW + Q_BLOCK

_NT = (((1,), (1,)), ((), ()))


def _dot(a, b):
    return jnp.dot(a, b, preferred_element_type=F32)


def _dot_nt(a, b):
    return lax.dot_general(a, b, _NT, preferred_element_type=F32)


def _lane_iota(shape):
    return lax.broadcasted_iota(jnp.int32, shape, len(shape) - 1)


def _row_iota(shape):
    return lax.broadcasted_iota(jnp.int32, shape, len(shape) - 2)


def _rope(x, tab):
    c = tab[:, 0:LANES]
    s1 = tab[:, LANES:2 * LANES]
    return x * c + pltpu.roll(x, LANES - ROT_DIM // 2, 1) * s1 - pltpu.roll(x * s1, ROT_DIM // 2, 1)


def _rms_rows(x, g):
    ms = jnp.mean(x * x, axis=-1, keepdims=True)
    return x * lax.rsqrt(ms + NORM_EPS) * g


def _layer_norm_rows(v, g, b):
    mu = jnp.mean(v, axis=-1, keepdims=True)
    vc = v - mu
    var = jnp.mean(vc * vc, axis=-1, keepdims=True)
    return vc * lax.rsqrt(var + NORM_EPS) * g + b


def _head_slabs(x512):
    out = []
    rows = x512.shape[0]
    lane = _lane_iota((rows, LANES))
    lo = lane < HEAD_DIM
    for j in range(4):
        slab = x512[:, j * LANES:(j + 1) * LANES]
        swapped = pltpu.roll(slab, HEAD_DIM, 1)
        if j < 2:
            out.append(jnp.where(lo, slab, 0.0))
            out.append(jnp.where(lo, swapped, 0.0))
        else:
            out.append(jnp.where(lo, 0.0, swapped))
            out.append(jnp.where(lo, 0.0, slab))
    return out


def _project(x_ref, g_ref, w_refs, z_scr):
    hb = _rms_rows(x_ref[...], g_ref[...]).astype(BF16)
    wm_ref, wg_ref, wz_ref = w_refs
    z_scr[:, 0:C_G] = _dot(hb, wm_ref[...])
    z_scr[:, C_G:C_ZB] = _dot(hb, wg_ref[...])
    z_scr[:, C_ZB:C_END] = _dot(hb, wz_ref[...])


def _inproj_prompt_kernel(x_ref, g_ref, wm_ref, wg_ref, wz_ref, lng_ref, lnb_ref, ws_ref, bias_ref, rope_ref,
                          aout_ref, qt_ref, kc_ref, vc_ref,
                          kct_ref, vct_ref, kst_ref, vst_ref, kwt_ref, vwt_ref,
                          ksa_ref, kwb_ref, vsb_ref, vwb_ref, gate_ref, szb_ref,
                          z_scr, kstage_scr, vstage_scr, *, tm, tiles_per_batch):
    _project(x_ref, g_ref, (wm_ref, wg_ref, wz_ref), z_scr)
    tab = rope_ref[...]

    lane = _lane_iota((CHUNK, LANES))
    lo = lane < HEAD_DIM
    for c in range(tm // CHUNK):
        r0 = c * CHUNK
        u = z_scr[r0:r0 + CHUNK, C_U:C_U + D_A]
        v = z_scr[r0:r0 + CHUNK, C_V:C_V + D_A]
        za = z_scr[r0:r0 + CHUNK, C_ZA:C_ZA + D_A]
        vn = _layer_norm_rows(v, lng_ref[...], lnb_ref[...])
        parts = []
        for p in range(A_GROUPS // 2):
            vp = vn[:, p * LANES:(p + 1) * LANES]
            v_lo = jnp.where(lo, vp, 0.0).astype(BF16)
            v_hi = jnp.where(lo, 0.0, vp).astype(BF16)
            parts.append(_dot(ws_ref[2 * p], v_lo) + _dot(ws_ref[2 * p + 1], v_hi))
        s = jnp.concatenate(parts, axis=1) + bias_ref[...]
        aout_ref[r0:r0 + CHUNK, :] = (u * s * jax.nn.silu(za)).astype(BF16)

    q = z_scr[:, C_Q:C_Q + D_B]
    qr = jnp.concatenate(
        [_rope(q[:, j * LANES:(j + 1) * LANES], tab) for j in range(4)], axis=1) * (HEAD_DIM ** -0.5 * LOG2_E)
    slabs = _head_slabs(qr)
    kc = z_scr[:, C_KC:C_KC + KV_W]
    vc = z_scr[:, C_VC:C_VC + KV_W]
    for src, stage, dst in ((kc, kstage_scr, kc_ref), (vc, vstage_scr, vc_ref)):
        stage[...] = src
        for r in range(CMP_STRIDE):
            dst[:, r * KV_W:(r + 1) * KV_W] = stage[pl.ds(r, tm // CMP_STRIDE, stride=CMP_STRIDE), :]
    ks = _rope(z_scr[:, C_KS:C_KS + KV_W], tab)
    vs = z_scr[:, C_VS:C_VS + KV_W]
    kw = _rope(z_scr[:, C_KW:C_KW + KV_W], tab)
    vw = z_scr[:, C_VW:C_VW + KV_W]
    gates = jax.nn.sigmoid(z_scr[:, C_G:C_G + LANES])
    pos = (pl.program_id(0) % tiles_per_batch) * tm + _row_iota((tm, LANES))
    onehot = jnp.where((pos >> SEL_SHIFT) == _lane_iota((tm, LANES)), 1.0, 0.0)
    ksa_ref[:, 0:LANES] = ks.astype(BF16)
    ksa_ref[:, LANES:2 * LANES] = onehot.astype(BF16)
    kwb_ref[...] = kw.astype(BF16)
    for j in range(tm // LANES):
        r0, r1 = j * LANES, (j + 1) * LANES
        kct_ref[0, :, r0:r1] = kc[r0:r1].T
        vct_ref[0, :, r0:r1] = vc[r0:r1].T
        kst_ref[0, :, r0:r1] = ks[r0:r1].T
        kwt_ref[0, :, r0:r1] = kw[r0:r1].T
        vs_t, vw_t = vs[r0:r1].T, vw[r0:r1].T
        vst_ref[0, :, r0:r1] = vs_t
        vwt_ref[0, :, r0:r1] = vw_t
        vsb_ref[0, j] = vs_t.astype(BF16)
        vwb_ref[0, j] = vw_t.astype(BF16)
        for e in range(N_HEADS):
            qt_ref[0, j, :, e * Q_BLOCK:(e + 1) * Q_BLOCK] = slabs[e][r0:r1].T.astype(BF16)
        gate_ref[0, j] = gates[r0:r1].T
    szb_ref[...] = jax.nn.silu(z_scr[:, C_ZB:C_ZB + D_B])


def _inproj_prompt(x2, g, w, lng, lnb, ws_bf, bias_full, rope_tab, seq, tm=ROW_TILE):
    n = x2.shape[0]
    tiles_per_batch = seq // tm
    bsz = n // seq
    n_tok_tiles = seq // LANES
    row = lambda i: (i, 0)
    const2 = lambda i: (0, 0)
    rows2d = lambda width, dtype: (jax.ShapeDtypeStruct((n, width), dtype), pl.BlockSpec((tm, width), row))
    kv_t = (jax.ShapeDtypeStruct((bsz, KV_W, seq), F32),
            pl.BlockSpec((1, KV_W, tm), lambda i: (i // tiles_per_batch, 0, i % tiles_per_batch)))
    chunk_rows = (jax.ShapeDtypeStruct((n // CMP_STRIDE, CMP_STRIDE * KV_W), F32),
                  pl.BlockSpec((tm // CMP_STRIDE, CMP_STRIDE * KV_W), row))
    tiles = lambda r, c, dtype: (
        jax.ShapeDtypeStruct((bsz, n_tok_tiles, r, c), dtype),
        pl.BlockSpec((1, tm // LANES, r, c), lambda i: (i // tiles_per_batch, i % tiles_per_batch, 0, 0)))
    outs = [
        rows2d(D_A, BF16),
        tiles(KV_W, N_HEADS * Q_BLOCK, BF16),
        chunk_rows, chunk_rows,
        kv_t, kv_t, kv_t, kv_t, kv_t, kv_t,
        rows2d(2 * KV_W, BF16),
        rows2d(KV_W, BF16),
        tiles(KV_W, LANES, BF16),
        tiles(KV_W, LANES, BF16),
        tiles(LANES, LANES, F32),
        rows2d(D_B, F32),
    ]
    out_shapes = [o[0] for o in outs]
    out_specs = [o[1] for o in outs]
    in_specs = [
        pl.BlockSpec((tm, D_MODEL), row),
        pl.BlockSpec((1, D_MODEL), const2),
        pl.BlockSpec((D_MODEL, C_G), const2),
        pl.BlockSpec((D_MODEL, C_ZB - C_G), const2),
        pl.BlockSpec((D_MODEL, C_END - C_ZB), const2),
        pl.BlockSpec((1, D_A), const2),
        pl.BlockSpec((1, D_A), const2),
        pl.BlockSpec((A_GROUPS, CHUNK, CHUNK), lambda i: (0, 0, 0)),
        pl.BlockSpec((CHUNK, D_A), const2),
        pl.BlockSpec((tm, 2 * LANES), lambda i: (i % tiles_per_batch, 0)),
    ]
    return pl.pallas_call(
        functools.partial(_inproj_prompt_kernel, tm=tm, tiles_per_batch=tiles_per_batch),
        grid=(n // tm,),
        in_specs=in_specs,
        out_specs=out_specs,
        out_shape=out_shapes,
        scratch_shapes=[pltpu.VMEM((tm, C_END), F32), pltpu.VMEM((tm, KV_W), F32), pltpu.VMEM((tm, KV_W), F32)],
        compiler_params=pltpu.CompilerParams(
            dimension_semantics=("arbitrary",), vmem_limit_bytes=VMEM_LIMIT_BYTES),
        name="inproj_prompt",
    )(x2, g, *w, lng, lnb, ws_bf, bias_full, rope_tab)


def _inproj_decode_kernel(x_ref, g_ref, wm_ref, wg_ref, wz_ref, lng_ref, lnb_ref, wsv_ref, bsv_ref, rope_ref,
                          aout_ref, qa_ref, kc_ref, vc_ref, ks_ref, vs_ref, kw_ref, vw_ref,
                          gate_ref, szb_ref, vn_ref, z_scr):
    _project(x_ref, g_ref, (wm_ref, wg_ref, wz_ref), z_scr)
    tab = rope_ref[...]
    u = z_scr[:, C_U:C_U + D_A]
    v = z_scr[:, C_V:C_V + D_A]
    za = z_scr[:, C_ZA:C_ZA + D_A]
    vn = _layer_norm_rows(v, lng_ref[...], lnb_ref[...])
    vn_ref[...] = vn
    s = vn * wsv_ref[...] + bsv_ref[...]
    aout_ref[...] = (u * s * jax.nn.silu(za)).astype(BF16)

    q = z_scr[:, C_Q:C_Q + D_B]
    qr = jnp.concatenate(
        [_rope(q[:, j * LANES:(j + 1) * LANES], tab) for j in range(4)], axis=1) * (HEAD_DIM ** -0.5)
    for e, slab in enumerate(_head_slabs(qr)):
        qa_ref[:, e * LANES:(e + 1) * LANES] = slab.astype(BF16)
    kc_ref[...] = z_scr[:, C_KC:C_KC + KV_W]
    vc_ref[...] = z_scr[:, C_VC:C_VC + KV_W]
    ks_ref[...] = _rope(z_scr[:, C_KS:C_KS + KV_W], tab)
    vs_ref[...] = z_scr[:, C_VS:C_VS + KV_W]
    kw_ref[...] = _rope(z_scr[:, C_KW:C_KW + KV_W], tab)
    vw_ref[...] = z_scr[:, C_VW:C_VW + KV_W]
    gate_ref[...] = jax.nn.sigmoid(z_scr[:, C_G:C_G + LANES])
    szb = jax.nn.silu(z_scr[:, C_ZB:C_ZB + D_B])
    for e, slab in enumerate(_head_slabs(szb)):
        szb_ref[:, e * LANES:(e + 1) * LANES] = slab


def _inproj_decode(x2, g, w, lng, lnb, wsv, bsv, rope_tab):
    n = x2.shape[0]
    out_shapes = [
        jax.ShapeDtypeStruct((n, D_A), BF16),
        jax.ShapeDtypeStruct((n, N_HEADS * LANES), BF16),
        jax.ShapeDtypeStruct((n, KV_W), F32),
        jax.ShapeDtypeStruct((n, KV_W), F32),
        jax.ShapeDtypeStruct((n, KV_W), F32),
        jax.ShapeDtypeStruct((n, KV_W), F32),
        jax.ShapeDtypeStruct((n, KV_W), F32),
        jax.ShapeDtypeStruct((n, KV_W), F32),
        jax.ShapeDtypeStruct((n, LANES), F32),
        jax.ShapeDtypeStruct((n, N_HEADS * LANES), F32),
        jax.ShapeDtypeStruct((n, D_A), F32),
    ]
    return pl.pallas_call(
        _inproj_decode_kernel,
        out_shape=out_shapes,
        scratch_shapes=[pltpu.VMEM((n, C_END), F32)],
        compiler_params=pltpu.CompilerParams(vmem_limit_bytes=VMEM_LIMIT_BYTES),
        name="inproj_decode",
    )(x2, g, *w, lng, lnb, wsv, bsv, rope_tab)


def _compress_chunks(ch, pos_term, w_ref):
    ab = _dot(ch.astype(BF16), w_ref[...])
    return ab[:, 0:KV_W] + pltpu.roll(ab[:, KV_W:2 * KV_W], ch.shape[0] - 1, 0) + pos_term


def _position_term(pe_ref, w_ref):
    pe = pe_ref[...]
    hi = pe.astype(BF16)
    r1 = pe - hi.astype(F32)
    mid = r1.astype(BF16)
    lo = (r1 - mid.astype(F32)).astype(BF16)
    pw = _dot(hi, w_ref[...]) + _dot(mid, w_ref[...]) + _dot(lo, w_ref[...])
    return pw[0:1, 0:KV_W] + pw[1:2, KV_W:2 * KV_W]


def _compress_prompt_kernel(kc_ref, vc_ref, wk_ref, wv_ref, pek_ref, pev_ref, rope_ref, kcc_ref, vcct_ref):
    kcc = _compress_chunks(kc_ref[0], _position_term(pek_ref, wk_ref), wk_ref)
    kcc_ref[0] = _rope(kcc, rope_ref[...]).astype(BF16)
    vcc = _compress_chunks(vc_ref[0], _position_term(pev_ref, wv_ref), wv_ref)
    for j in range(vcc.shape[0] // LANES):
        vcct_ref[0, :, j * LANES:(j + 1) * LANES] = vcc[j * LANES:(j + 1) * LANES].T.astype(BF16)


def _compress_prompt(kc3, vc3, wk, wv, pek, pev, rope_tab):
    b, c, width = kc3.shape
    blk = pl.BlockSpec((1, c, width), lambda i: (i, 0, 0))
    const2 = lambda i: (0, 0)
    return pl.pallas_call(
        _compress_prompt_kernel,
        grid=(b,),
        in_specs=[blk, blk,
                  pl.BlockSpec(wk.shape, const2), pl.BlockSpec(wv.shape, const2),
                  pl.BlockSpec(pek.shape, const2), pl.BlockSpec(pev.shape, const2),
                  pl.BlockSpec(rope_tab.shape, const2)],
        out_specs=[pl.BlockSpec((1, c, KV_W), lambda i: (i, 0, 0)),
                   pl.BlockSpec((1, KV_W, c), lambda i: (i, 0, 0))],
        out_shape=[jax.ShapeDtypeStruct((b, c, KV_W), BF16), jax.ShapeDtypeStruct((b, KV_W, c), BF16)],
        compiler_params=pltpu.CompilerParams(
            dimension_semantics=("arbitrary",), vmem_limit_bytes=VMEM_LIMIT_BYTES),
        name="compress_prompt",
    )(kc3, vc3, wk, wv, pek, pev, rope_tab)


def _select_blocks(imp, kk, candidate, forced):
    assert FORCE_SCORE > 2 * GQA
    start = jnp.where(candidate, jnp.where(forced, -jnp.inf, imp), -jnp.inf)
    score = start
    for _ in range(N_SELECT - N_FORCED):
        cm = jnp.max(score, axis=0, keepdims=True)
        first = jnp.min(jnp.where(score == cm, kk, LANES), axis=0, keepdims=True)
        score = jnp.where(kk == first, -jnp.inf, score)
    taken = (score == -jnp.inf) & (start > -jnp.inf)
    return candidate & (forced | taken)


def _softmax_rows(s):
    mx = jnp.max(s, axis=1, keepdims=True)
    ex = jnp.exp(s - mx)
    return ex * (1.0 / jnp.sum(ex, axis=1, keepdims=True))


def _tile_heads(x):
    return jnp.concatenate([x] * N_HEADS, axis=1)


def _values_by_kv_head(vt, p):
    half = GQA * Q_BLOCK
    return jnp.concatenate(
        [_dot(vt[h * HEAD_DIM:(h + 1) * HEAD_DIM], p[:, h * half:(h + 1) * half]) for h in range(N_KV)], axis=1)


def _attn_prompt_kernel(*refs):
    def body(sub, carry):
        _attn_query_block(sub, *refs)
        return carry

    lax.fori_loop(0, Q_SUB, body, 0)


def _attn_query_block(sub, qt_ref, gate_ref, szb_ref, kcc_ref, vcct_ref, ovlt_ref,
                      ksa_ref, vsb_ref, kwb_ref, vwb_ref, out_ref,
                      qaug_scr, acc_scr, sa_scr, sb_scr):
    i_blk = pl.program_id(1) * Q_SUB + sub
    q0 = i_blk * Q_BLOCK
    cols = N_HEADS * Q_BLOCK
    qt = qt_ref[0, sub]
    out_rows = pl.ds(pl.multiple_of(sub * Q_BLOCK, Q_BLOCK), Q_BLOCK)

    n_cmp = kcc_ref.shape[1]

    def compressed(extent):
        t_c = q0 + _lane_iota((extent, Q_BLOCK))
        seen = (_row_iota((extent, Q_BLOCK)) * CMP_STRIDE + (CMP_BLOCK - 1)) <= t_c
        s = _dot(kcc_ref[0, 0:extent, :], qt) + _tile_heads(jnp.where(seen, 0.0, NEG))
        mx = jnp.max(s, axis=0, keepdims=True)
        ex = jnp.exp2(s - mx)
        inv = jnp.where(mx > 0.5 * NEG, 1.0 / jnp.sum(ex, axis=0, keepdims=True), 0.0)
        p_c = ex * inv
        imps = []
        for h in range(N_KV):
            c0 = h * GQA * Q_BLOCK
            p_sum = (p_c[:, c0:c0 + Q_BLOCK] + p_c[:, c0 + Q_BLOCK:c0 + 2 * Q_BLOCK]
                     + p_c[:, c0 + 2 * Q_BLOCK:c0 + 3 * Q_BLOCK] + p_c[:, c0 + 3 * Q_BLOCK:c0 + 4 * Q_BLOCK])
            imps.append(_dot(ovlt_ref[:, 0:extent], p_sum.astype(BF16)))
        return (_values_by_kv_head(vcct_ref[0, :, 0:extent], p_c.astype(BF16)), *imps)

    n_need = (q0 + Q_BLOCK) // CMP_STRIDE
    extents = list(range(LANES, n_cmp + 1, LANES))
    o_c, *imp_t = lax.switch(
        jnp.minimum((n_need - 1) // LANES, len(extents) - 1),
        [functools.partial(compressed, ext) for ext in extents])

    w0 = pl.multiple_of(jnp.maximum(q0 - WINDOW, 0), Q_BLOCK)
    wj0 = jnp.maximum(i_blk - WINDOW // Q_BLOCK, 0)
    kwin = kwb_ref[0, pl.ds(w0, WIN_SPAN), :]
    vwt = jnp.concatenate([vwb_ref[0, wj0 + i] for i in range(WIN_SPAN // LANES)], axis=1)
    kp = w0 + _row_iota((WIN_SPAN, Q_BLOCK))
    tw = q0 + _lane_iota((WIN_SPAN, Q_BLOCK))
    in_win = jnp.where(kp <= tw, jnp.where(kp > tw - WINDOW, 0.0, NEG), NEG)
    sw = _dot(kwin, qt) + _tile_heads(in_win)
    exw = jnp.exp2(sw - jnp.max(sw, axis=0, keepdims=True))
    o_w = _values_by_kv_head(vwt, exw.astype(BF16)) * (1.0 / jnp.sum(exw, axis=0, keepdims=True))

    kk = _row_iota((LANES, Q_BLOCK))
    tq = q0 + _lane_iota((LANES, Q_BLOCK))
    valid = kk * SEL_BLOCK <= tq
    cur = tq >> SEL_SHIFT
    forced = (kk == 0) | (kk == cur) | (kk == cur - 1)
    both = lambda a: jnp.concatenate([a] * N_KV, axis=1)
    picked = _select_blocks(jnp.concatenate(imp_t, axis=1), both(kk), both(valid), both(forced))
    qaug_scr[0:LANES, :] = qt
    for h in range(N_KV):
        selneg = jnp.where(picked[:, h * Q_BLOCK:(h + 1) * Q_BLOCK], 0.0, NEG).astype(BF16)
        for g in range(GQA):
            c0 = (h * GQA + g) * Q_BLOCK
            qaug_scr[LANES:2 * LANES, c0:c0 + Q_BLOCK] = selneg

    acc_scr[...] = jnp.zeros(acc_scr.shape, F32)
    tiles_per_step = SEL_TILE // LANES

    def score(step, dst):
        t0 = pl.multiple_of(step * SEL_TILE, SEL_TILE)
        dst[...] = _dot(ksa_ref[0, pl.ds(t0, SEL_TILE), :], qaug_scr[...])

    def attend(src, step, m_prev, l_prev, causal_rows=None):
        rows = SEL_TILE if causal_rows is None else causal_rows
        vt = jnp.concatenate([vsb_ref[0, step * tiles_per_step + i] for i in range(rows // LANES)], axis=1)
        sc = src[0:rows, :]
        if causal_rows is not None:
            tok = step * SEL_TILE + _row_iota((rows, Q_BLOCK))
            sc = sc + _tile_heads(jnp.where(tok <= q0 + _lane_iota((rows, Q_BLOCK)), 0.0, NEG))
        m_next = jnp.maximum(m_prev, jnp.max(sc, axis=0, keepdims=True))
        p = jnp.exp2(sc - m_next)
        alpha = jnp.exp2(m_prev - m_next)
        l_next = alpha * l_prev + jnp.sum(p, axis=0, keepdims=True)
        acc_scr[...] = alpha * acc_scr[...] + _values_by_kv_head(vt, p.astype(BF16))
        return m_next, l_next

    n_full = q0 // SEL_TILE
    score(0, sa_scr)

    def pair(jj, carry):
        m_c, l_c = carry
        score(2 * jj + 1, sb_scr)
        m_c, l_c = attend(sa_scr, 2 * jj, m_c, l_c)
        score(2 * jj + 2, sa_scr)
        return attend(sb_scr, 2 * jj + 1, m_c, l_c)

    m_run, l_run = lax.fori_loop(
        0, n_full // 2, pair,
        (jnp.full((1, cols), -jnp.inf, F32), jnp.zeros((1, cols), F32)))

    def last_step(src, m_c, l_c):
        extents = list(range(Q_BLOCK, SEL_TILE + 1, Q_BLOCK))
        return lax.switch(
            (q0 % SEL_TILE) // Q_BLOCK,
            [lambda m, l, ext=ext: attend(src, n_full, m, l, causal_rows=ext)[1] for ext in extents],
            m_c, l_c)

    def odd_tail(m_c, l_c):
        score(n_full, sb_scr)
        m_c, l_c = attend(sa_scr, n_full - 1, m_c, l_c)
        return last_step(sb_scr, m_c, l_c)

    def even_tail(m_c, l_c):
        return last_step(sa_scr, m_c, l_c)

    l_run = lax.cond(n_full % 2 == 1, odd_tail, even_tail, m_run, l_run)

    gt = gate_ref[0, sub]
    o_s = acc_scr[...] * (1.0 / l_run)
    heads = []
    for e in range(N_HEADS):
        c0 = e * Q_BLOCK
        heads.append(gt[3 * e:3 * e + 1, :] * o_c[:, c0:c0 + Q_BLOCK]
                     + gt[3 * e + 1:3 * e + 2, :] * o_s[:, c0:c0 + Q_BLOCK]
                     + gt[3 * e + 2:3 * e + 3, :] * o_w[:, c0:c0 + Q_BLOCK])
    for j in range(N_HEADS // 2):
        slab = jnp.concatenate([heads[2 * j], heads[2 * j + 1]], axis=0).T
        out_ref[0, out_rows, j * LANES:(j + 1) * LANES] = (
            slab * szb_ref[0, out_rows, j * LANES:(j + 1) * LANES]).astype(BF16)


def _attn_prompt(qt4, gate4, szb3, kcc, vcct, ovlt, ksa3, vsb4, kwb3, vwb4):
    b, nq = qt4.shape[0], qt4.shape[1]
    t = nq * Q_BLOCK
    cols = N_HEADS * Q_BLOCK
    assert nq % Q_SUB == 0
    qtile = lambda a: pl.BlockSpec((1, Q_SUB) + a.shape[2:], lambda bi, i: (bi, i, 0, 0))
    rows = pl.BlockSpec((1, Q_SUB * Q_BLOCK, D_B), lambda bi, i: (bi, i, 0))
    seq = lambda a: pl.BlockSpec((1,) + a.shape[1:], lambda bi, i, nd=a.ndim: (bi,) + (0,) * (nd - 1))
    return pl.pallas_call(
        _attn_prompt_kernel,
        grid=(b, nq // Q_SUB),
        in_specs=[qtile(qt4), qtile(gate4), rows,
                  seq(kcc), seq(vcct), pl.BlockSpec(ovlt.shape, lambda bi, i: (0, 0)),
                  seq(ksa3), seq(vsb4), seq(kwb3), seq(vwb4)],
        out_specs=rows,
        out_shape=jax.ShapeDtypeStruct((b, t, D_B), BF16),
        scratch_shapes=[pltpu.VMEM((2 * LANES, cols), BF16),
                        pltpu.VMEM((HEAD_DIM, cols), F32),
                        pltpu.VMEM((SEL_TILE, cols), F32),
                        pltpu.VMEM((SEL_TILE, cols), F32)],
        compiler_params=pltpu.CompilerParams(
            dimension_semantics=("arbitrary", "arbitrary"), vmem_limit_bytes=VMEM_LIMIT_BYTES),
        name="attn_prompt",
    )(qt4, gate4, szb3, kcc, vcct, ovlt, ksa3, vsb4, kwb3, vwb4)


def _outproj_kernel(x_ref, a_ref, m_ref, wa_ref, wb_ref, fg_ref, y_ref):
    y = x_ref[...] + _dot(a_ref[...], wa_ref[...]) + _dot(m_ref[...].astype(BF16), wb_ref[...])
    y_ref[...] = _rms_rows(y, fg_ref[...])


def _outproj(x2, a_out, mix_b, wa, wb, fg, tm):
    n = x2.shape[0]
    row = lambda i: (i, 0)
    const2 = lambda i: (0, 0)
    return pl.pallas_call(
        _outproj_kernel,
        grid=(n // tm,),
        in_specs=[pl.BlockSpec((tm, D_MODEL), row),
                  pl.BlockSpec((tm, a_out.shape[1]), row),
                  pl.BlockSpec((tm, mix_b.shape[1]), row),
                  pl.BlockSpec(wa.shape, const2), pl.BlockSpec(wb.shape, const2),
                  pl.BlockSpec((1, D_MODEL), const2)],
        out_specs=pl.BlockSpec((tm, D_MODEL), row),
        out_shape=jax.ShapeDtypeStruct((n, D_MODEL), F32),
        compiler_params=pltpu.CompilerParams(
            dimension_semantics=("arbitrary",), vmem_limit_bytes=VMEM_LIMIT_BYTES),
        name="outproj",
    )(x2, a_out, mix_b, wa, wb, fg)


def _page_copies(pt_ref, pool_ref, buf_ref, sem_ref, step, slot):
    group, n_pages = buf_ref.shape[1], buf_ref.shape[2]
    return [pltpu.make_async_copy(pool_ref.at[pt_ref[step * group + g, p]], buf_ref.at[slot, g, p],
                                  sem_ref.at[slot])
            for g in range(group) for p in range(n_pages)]


def _gather_pages(pt_ref, pools, bufs, sems):
    b = pl.program_id(0)
    slot = lax.rem(b, 2)

    def start(step, sl):
        for pool, buf, sem in zip(pools, bufs, sems):
            for cp in _page_copies(pt_ref, pool, buf, sem, step, sl):
                cp.start()

    @pl.when(b == 0)
    def _():
        start(0, 0)

    @pl.when(b + 1 < pl.num_programs(0))
    def _():
        start(b + 1, 1 - slot)

    for pool, buf, sem in zip(pools, bufs, sems):
        for cp in _page_copies(pt_ref, pool, buf, sem, b, slot):
            cp.wait()
    return slot


def _decode_cmp_kernel(pt_ref, ck_hbm, cv_hbm, q_ref, wk_ref, wv_ref, pek_ref, pev_ref, rope_ref, ovl_ref,
                       oc_ref, psum_ref, kbuf, vbuf, ksem, vsem, ktok_scr, vtok_scr, kpos_scr, vpos_scr):
    slot = _gather_pages(pt_ref, (ck_hbm, cv_hbm), (kbuf, vbuf), (ksem, vsem))
    group, n_pages = kbuf.shape[1], kbuf.shape[2]

    @pl.when(pl.program_id(0) == 0)
    def _():
        kpos_scr[...] = _position_term(pek_ref, wk_ref)
        vpos_scr[...] = _position_term(pev_ref, wv_ref)

    def chunk_rows(buf, tok_scr, g):
        for p in range(n_pages):
            tok_scr[g, p * PAGE_SIZE:(p + 1) * PAGE_SIZE, :] = buf[slot, g, p].T
        n = tok_scr.shape[1] // CMP_STRIDE
        return jnp.concatenate(
            [tok_scr[g, pl.ds(r, n, stride=CMP_STRIDE), :] for r in range(CMP_STRIDE)], axis=1)

    for g in range(group):
        kch = chunk_rows(kbuf, ktok_scr, g)
        vch = chunk_rows(vbuf, vtok_scr, g)
        kcc = _rope(_compress_chunks(kch, kpos_scr[...], wk_ref), rope_ref[...]).astype(BF16)
        vcc = _compress_chunks(vch, vpos_scr[...], wv_ref).astype(BF16)
        n_chunks = kch.shape[0]
        q = q_ref[g]
        rows = q.shape[0]
        s = _dot_nt(q, kcc)
        seen = _lane_iota((rows, n_chunks)) < n_chunks - 1
        p_c = jnp.where(seen, _softmax_rows(jnp.where(seen, s, NEG)), 0.0)
        oc_ref[g] = _dot(p_c.astype(BF16), vcc)[0:N_HEADS]
        ps0 = jnp.sum(p_c[0:GQA], axis=0, keepdims=True)
        ps1 = jnp.sum(p_c[GQA:2 * GQA], axis=0, keepdims=True)
        psum = jnp.where(_row_iota((rows, n_chunks)) < N_HEADS // 2, ps0, ps1)
        psum_ref[g] = _dot(psum.astype(BF16), ovl_ref[...])[0:N_HEADS]


def _decode_cmp(page_table, ck_pool, cv_pool, q16, wk, wv, pek, pev, rope_tab, ovl):
    nb, n_pages = page_table.shape
    group = DECODE_GROUP
    assert nb % group == 0
    hbm = pl.BlockSpec(memory_space=pl.ANY)
    page_buf = pltpu.VMEM((2, group, n_pages, KV_W, PAGE_SIZE), F32)
    tok_buf = pltpu.VMEM((group, n_pages * PAGE_SIZE, KV_W), F32)
    const2 = lambda b, pt: (0, 0)
    per_b = lambda a: pl.BlockSpec((group,) + a.shape[1:], lambda b, pt: (b, 0, 0))
    in_specs = [hbm, hbm, per_b(q16),
                pl.BlockSpec(wk.shape, const2), pl.BlockSpec(wv.shape, const2),
                pl.BlockSpec(pek.shape, const2), pl.BlockSpec(pev.shape, const2),
                pl.BlockSpec(rope_tab.shape, const2), pl.BlockSpec(ovl.shape, const2)]
    out_blk = pl.BlockSpec((group, N_HEADS, LANES), lambda b, pt: (b, 0, 0))
    grid_spec = pltpu.PrefetchScalarGridSpec(
        num_scalar_prefetch=1, grid=(nb // group,), in_specs=in_specs, out_specs=[out_blk, out_blk],
        scratch_shapes=[page_buf, page_buf, pltpu.SemaphoreType.DMA((2,)), pltpu.SemaphoreType.DMA((2,)),
                        tok_buf, tok_buf, pltpu.VMEM((1, KV_W), F32), pltpu.VMEM((1, KV_W), F32)])
    return pl.pallas_call(
        _decode_cmp_kernel,
        grid_spec=grid_spec,
        out_shape=[jax.ShapeDtypeStruct((nb, N_HEADS, LANES), F32)] * 2,
        compiler_params=pltpu.CompilerParams(
            dimension_semantics=("arbitrary",), vmem_limit_bytes=VMEM_LIMIT_BYTES),
        name="decode_cmp",
    )(page_table, ck_pool, cv_pool, q16, wk, wv, pek, pev, rope_tab, ovl)


def _decode_topk_kernel(imp_ref, sel_ref, *, n_blk, cur):
    n = imp_ref.shape[0]
    tiles = [imp_ref[i * LANES:(i + 1) * LANES, :].T for i in range(n // LANES)]
    imp_t = jnp.concatenate(tiles, axis=1)
    kk = _row_iota(imp_t.shape)
    forced = (kk == 0) | (kk == cur) | (kk == cur - 1)
    selneg = jnp.where(_select_blocks(imp_t, kk, kk < n_blk, forced), 0.0, NEG)
    for i in range(n // LANES):
        sel_ref[i * LANES:(i + 1) * LANES, :] = selneg[:, i * LANES:(i + 1) * LANES].T


def _decode_topk(imp2, n_blk, cur):
    return pl.pallas_call(
        functools.partial(_decode_topk_kernel, n_blk=n_blk, cur=cur),
        out_shape=jax.ShapeDtypeStruct(imp2.shape, F32),
        compiler_params=pltpu.CompilerParams(vmem_limit_bytes=VMEM_LIMIT_BYTES),
        name="decode_topk",
    )(imp2)


def _decode_attend(s, q32, vals_t_bf, k_new, v_new, bias_new):
    kn = k_new.astype(BF16).astype(F32)
    s_new = jnp.sum(q32 * kn, axis=1, keepdims=True) + bias_new
    mx = jnp.maximum(jnp.max(s, axis=1, keepdims=True), s_new)
    ex = jnp.exp(s - mx)
    ex_new = jnp.exp(s_new - mx)
    inv = 1.0 / (jnp.sum(ex, axis=1, keepdims=True) + ex_new)
    p = (ex * inv).astype(BF16)
    p_new = (ex_new * inv).astype(BF16).astype(F32)
    return _dot_nt(p, vals_t_bf) + p_new * v_new.astype(BF16).astype(F32)


P_Q, P_SEL, P_OC, P_SZB, P_NEW, P_GATE, P_ROWS = 0, 16, 32, 40, 48, 56, 64


def _row_to_col(row):
    n = row.shape[1]
    diag = _row_iota((n, n)) == _lane_iota((n, n))
    return jnp.sum(jnp.where(diag, row, 0.0), axis=1, keepdims=True)


def _decode_sel_kernel(pt_ref, ks_hbm, vs_hbm, ckw_ref, cvw_ref, pack_ref, onehot_ref,
                       mix_ref, okw_ref, ovw_ref, kbuf, vbuf, ksem, vsem, *, cur):
    slot = _gather_pages(pt_ref, (ks_hbm, vs_hbm), (kbuf, vbuf), (ksem, vsem))
    group, n_pages = kbuf.shape[1], kbuf.shape[2]
    for g in range(group):
        pk = pack_ref[g]
        q32 = pk[P_Q:P_Q + 2 * N_HEADS]
        q = q32.astype(BF16)
        sel = pk[P_SEL:P_SEL + 2 * N_HEADS].astype(BF16)
        rows = q.shape[0]
        ks_new, vs_new = pk[P_NEW:P_NEW + 1], pk[P_NEW + 1:P_NEW + 2]
        kw_new, vw_new = pk[P_NEW + 2:P_NEW + 3], pk[P_NEW + 3:P_NEW + 4]

        kt = jnp.concatenate([kbuf[slot, g, p] for p in range(n_pages)], axis=1).astype(BF16)
        vt = jnp.concatenate([vbuf[slot, g, p] for p in range(n_pages)], axis=1).astype(BF16)
        kaug = jnp.concatenate([kt, onehot_ref[...]], axis=0)
        qaug = jnp.concatenate([q, sel], axis=1)
        s_sel = _dot(qaug, kaug)
        o_s = _decode_attend(s_sel, q32, vt, ks_new, vs_new, sel.astype(F32)[:, cur:cur + 1])

        kwin = ckw_ref[g]
        vwin = cvw_ref[g]
        keep = kwin.shape[1]
        visible = _lane_iota((rows, keep)) > keep - WINDOW
        s_win = jnp.where(visible, _dot(q, kwin.astype(BF16)), NEG)
        o_w = _decode_attend(s_win, q32, vwin.astype(BF16), kw_new, vw_new, 0.0)

        gates = pk[P_GATE:P_GATE + N_HEADS]
        o = (gates[:, 0:1] * pk[P_OC:P_OC + N_HEADS] + gates[:, 1:2] * o_s[0:N_HEADS]
             + gates[:, 2:3] * o_w[0:N_HEADS])
        mix_ref[g] = o * pk[P_SZB:P_SZB + N_HEADS]

        last = _lane_iota((KV_W, keep)) == keep - 1
        okw_ref[g] = jnp.where(last, _row_to_col(kw_new), pltpu.roll(kwin, keep - 1, 1))
        ovw_ref[g] = jnp.where(last, _row_to_col(vw_new), pltpu.roll(vwin, keep - 1, 1))


def _decode_sel(page_table, ks_pool, vs_pool, ckw, cvw, pack, onehot, cur):
    nb, n_pages = page_table.shape
    group = DECODE_GROUP
    assert nb % group == 0
    keep = ckw.shape[2]
    hbm = pl.BlockSpec(memory_space=pl.ANY)
    page_buf = pltpu.VMEM((2, group, n_pages, KV_W, PAGE_SIZE), F32)
    per_b = lambda a: pl.BlockSpec((group,) + a.shape[1:], lambda b, pt: (b, 0, 0))
    in_specs = [hbm, hbm, per_b(ckw), per_b(cvw), per_b(pack),
                pl.BlockSpec(onehot.shape, lambda b, pt: (0, 0))]
    out_specs = [pl.BlockSpec((group, N_HEADS, LANES), lambda b, pt: (b, 0, 0)),
                 pl.BlockSpec((group, KV_W, keep), lambda b, pt: (b, 0, 0)),
                 pl.BlockSpec((group, KV_W, keep), lambda b, pt: (b, 0, 0))]
    grid_spec = pltpu.PrefetchScalarGridSpec(
        num_scalar_prefetch=1, grid=(nb // group,), in_specs=in_specs, out_specs=out_specs,
        scratch_shapes=[page_buf, page_buf, pltpu.SemaphoreType.DMA((2,)), pltpu.SemaphoreType.DMA((2,))])
    return pl.pallas_call(
        functools.partial(_decode_sel_kernel, cur=cur),
        grid_spec=grid_spec,
        out_shape=[jax.ShapeDtypeStruct((nb, N_HEADS, LANES), F32),
                   jax.ShapeDtypeStruct((nb, KV_W, keep), F32),
                   jax.ShapeDtypeStruct((nb, KV_W, keep), F32)],
        compiler_params=pltpu.CompilerParams(
            dimension_semantics=("arbitrary",), vmem_limit_bytes=VMEM_LIMIT_BYTES),
        name="decode_sel",
    )(page_table, ks_pool, vs_pool, ckw, cvw, pack, onehot)


def _rope_table(pos):
    half = ROT_DIM // 2
    inv = jnp.power(jnp.float32(ROPE_THETA), -jnp.arange(half, dtype=F32) / half)
    ang = pos.astype(F32)[:, None] * inv[None, :]
    cs = jnp.concatenate([jnp.cos(ang), jnp.sin(ang)], axis=1)
    lane = jnp.arange(2 * LANES)
    seg, l64 = lane // LANES, lane % HEAD_DIM
    src = jnp.where(seg == 0, l64 % half, half + l64 % half)
    sign = jnp.where(seg == 0, l64 < ROT_DIM, l64 < half).astype(F32)
    sign = jnp.where(seg == 1, -sign, sign)
    place = (jnp.arange(ROT_DIM)[:, None] == src[None, :]).astype(F32) * sign[None, :]
    ones = ((seg == 0) & (l64 >= ROT_DIM)).astype(F32)
    return jnp.dot(cs, place, precision=lax.Precision.HIGHEST) + ones[None, :]


def _compress_weights(w_c, pe_c):
    w4 = w_c.reshape(2, CMP_STRIDE, HEAD_DIM, HEAD_DIM)
    wb = jnp.einsum('arde,hg->rhdage', w4, jnp.eye(N_KV, dtype=w_c.dtype))
    wb = wb.reshape(CMP_STRIDE * KV_W, 2 * KV_W).astype(BF16)
    pe = jnp.broadcast_to(pe_c.reshape(2, CMP_STRIDE, 1, HEAD_DIM), (2, CMP_STRIDE, N_KV, HEAD_DIM))
    return wb, jnp.pad(pe.reshape(2, CMP_STRIDE * KV_W), ((0, 14), (0, 0)))


def _overlap_matrix(n_rows, n_blk):
    cs = jnp.arange(n_rows, dtype=jnp.int32)[:, None] * CMP_STRIDE
    bs = jnp.arange(LANES, dtype=jnp.int32)[None, :] * SEL_BLOCK
    hit = (cs < bs + SEL_BLOCK) & (cs + CMP_BLOCK > bs) & (jnp.arange(LANES)[None, :] < n_blk)
    return hit.astype(BF16)


def kernel(x_prompt, x_sample, cache_k_cmp, cache_v_cmp, cache_k_sel, cache_v_sel, cache_k_win,
           cache_v_win, page_table, norm_g, w_in, ln_v_g, ln_v_b, w_s, b_s, w_ck, pe_ck, w_cv,
           pe_cv, w_out, final_g):
    depth = norm_g.shape[0]
    assert depth == 1
    bsz, seq, _ = x_prompt.shape
    nb, dec_seq, _ = x_sample.shape
    assert dec_seq == 1 and seq % SEL_TILE == 0 and seq >= WIN_SPAN and (seq // CMP_STRIDE) % LANES == 0
    n_pages = page_table.shape[1]
    past = n_pages * PAGE_SIZE
    keep = cache_k_win.shape[2]
    assert keep == WINDOW and past % SEL_BLOCK == 0

    w = w_in[0]
    w_pad = (w[:, :C_G].astype(BF16),
             jnp.pad(w[:, C_G:C_G + N_GATE_COLS], ((0, 0), (0, LANES - N_GATE_COLS))).astype(BF16),
             w[:, C_G + N_GATE_COLS:].astype(BF16))
    g_row = norm_g[0][None, :]
    lng = ln_v_g[0][None, :]
    lnb = ln_v_b[0][None, :]
    fg = final_g[None, :]
    wk_c, pek = _compress_weights(w_ck[0], pe_ck[0])
    wv_c, pev = _compress_weights(w_cv[0], pe_cv[0])
    w_o = w_out[0].astype(BF16)
    wo_a, wo_b = w_o[:D_A], w_o[D_A:]

    tril = jnp.tril(jnp.ones((CHUNK, CHUNK), w_s.dtype))
    ws_bf = (w_s[0] * tril).astype(BF16)
    bias_full = jnp.repeat(b_s[0].T, HEAD_DIM, axis=1)
    xp2 = x_prompt.reshape(bsz * seq, D_MODEL)
    (a_out, qt, kc, vc, kc_t, vc_t, ks_t, vs_t, kw_t, vw_t, ksa, kwb, vsb, vwb, gates_t, szb) = _inproj_prompt(
        xp2, g_row, w_pad, lng, lnb, ws_bf, bias_full,
        _rope_table(jnp.arange(seq, dtype=jnp.int32)), seq)
    n_chunks = seq // CMP_STRIDE
    kcc, vcc_t = _compress_prompt(
        kc.reshape(bsz, n_chunks, CMP_STRIDE * KV_W), vc.reshape(bsz, n_chunks, CMP_STRIDE * KV_W),
        wk_c, wv_c, pek, pev,
        _rope_table(jnp.arange(n_chunks, dtype=jnp.int32) * CMP_STRIDE))
    b3 = lambda a: a.reshape(bsz, seq, a.shape[-1])
    mix_p = _attn_prompt(qt, gates_t, b3(szb), kcc, vcc_t,
                         _overlap_matrix(n_chunks, seq // SEL_BLOCK).T,
                         b3(ksa), vsb, b3(kwb), vwb)
    y_prompt = _outproj(xp2, a_out, mix_p.reshape(bsz * seq, D_B), wo_a, wo_b, fg, OUT_ROW_TILE)
    y_prompt = y_prompt.reshape(bsz, seq, D_MODEL)
    st = lambda a: jnp.transpose(a.reshape(a.shape[0], N_KV, HEAD_DIM, a.shape[2]), (0, 3, 1, 2))[None]
    keep_p = min(WINDOW, seq)

    xs2 = x_sample.reshape(nb, D_MODEL)
    wsv = jnp.repeat(w_s[0][:, 0, 0], HEAD_DIM)[None, :]
    bsv = jnp.repeat(b_s[0][:, 0], HEAD_DIM)[None, :]
    (a_s, qa_s, kc_s, vc_s, ks_s, vs_s, kw_s, vw_s, gates_s, szb_s, vn_s) = _inproj_decode(
        xs2, g_row, w_pad, lng, lnb, wsv, bsv,
        _rope_table(jnp.full((nb,), past, dtype=jnp.int32)))
    q16 = jnp.pad(qa_s.reshape(nb, N_HEADS, LANES), ((0, 0), (0, N_HEADS), (0, 0)))
    d_chunks = past // CMP_STRIDE
    n_blk = -(-(past + 1) // SEL_BLOCK)
    cur = past // SEL_BLOCK
    kv_t = lambda c: jnp.transpose(c[0], (0, 2, 3, 1)).reshape(c.shape[1], KV_W, c.shape[2])
    oc_s, imp_s = _decode_cmp(
        page_table, kv_t(cache_k_cmp), kv_t(cache_v_cmp),
        q16, wk_c, wv_c, pek, pev,
        _rope_table(jnp.arange(d_chunks, dtype=jnp.int32) * CMP_STRIDE),
        _overlap_matrix(d_chunks, n_blk))
    selneg = _decode_topk(imp_s[:, ::GQA, :].reshape(nb * N_KV, LANES), n_blk, cur)
    sel16 = jnp.pad(jnp.repeat(selneg.reshape(nb, N_KV, LANES), GQA, axis=1), ((0, 0), (0, N_HEADS), (0, 0)))
    tok_blk = jnp.arange(past, dtype=jnp.int32)[None, :] // SEL_BLOCK
    onehot = (tok_blk == jnp.arange(LANES, dtype=jnp.int32)[:, None]).astype(BF16)
    gate_rows = jnp.pad(gates_s[:, :N_GATE_COLS].reshape(nb, N_HEADS, 3), ((0, 0), (0, 0), (0, LANES - 3)))
    new_rows = jnp.pad(jnp.stack([ks_s, vs_s, kw_s, vw_s], axis=1), ((0, 0), (0, P_GATE - P_NEW - 4), (0, 0)))
    pack = jnp.concatenate(
        [q16.astype(F32), sel16, oc_s, szb_s.reshape(nb, N_HEADS, LANES), new_rows, gate_rows], axis=1)
    assert pack.shape[1] == P_ROWS
    mix_s, okw, ovw = _decode_sel(
        page_table, kv_t(cache_k_sel), kv_t(cache_v_sel), kv_t(cache_k_win), kv_t(cache_v_win),
        pack, onehot, cur)
    wo_slab = jnp.zeros((N_HEADS, N_KV, HEAD_DIM, D_MODEL), BF16)
    wo_heads = wo_b.reshape(N_HEADS, HEAD_DIM, D_MODEL)
    for e in range(N_HEADS):
        wo_slab = wo_slab.at[e, e // GQA].set(wo_heads[e])
    y_sample = _outproj(xs2, a_s, mix_s.reshape(nb, N_HEADS * LANES), wo_a,
                        wo_slab.reshape(N_HEADS * LANES, D_MODEL), fg, nb)
    y_sample = y_sample.reshape(nb, 1, D_MODEL)
    ss = lambda a: a.reshape(1, nb, 1, N_KV, HEAD_DIM)

    return (y_prompt, y_sample,
            st(kc_t), st(vc_t), st(ks_t), st(vs_t),
            st(kw_t[:, :, seq - keep_p:]), st(vw_t[:, :, seq - keep_p:]),
            ss(kc_s), ss(vc_s), ss(ks_s), ss(vs_s),
            st(okw), st(ovw),
            vn_s.reshape(1, nb, 1, D_A))
```

```python
import functools

import jax
import jax.numpy as jnp
from jax import lax
from jax.experimental import pallas as pl
from jax.experimental.pallas import tpu as pltpu

F32 = jnp.float32
BF16 = jnp.bfloat16

D_MODEL = 1024
HEAD_DIM = 64
D_A = 512
D_B = 512
A_GROUPS = 8
CHUNK = 128
N_HEADS = 8
N_KV = 2
GQA = 4
KV_W = 128
ROT_DIM = 16
ROPE_THETA = 500000.0
CMP_BLOCK = 32
CMP_STRIDE = 16
SEL_BLOCK = 64
SEL_SHIFT = 6
N_SELECT = 16
WINDOW = 512
Q_BLOCK = 128
PAGE_SIZE = 128
NORM_EPS = 1e-6
FORCE_SCORE = 1e4
N_FORCED = 3
NEG = -1e30
LOG2_E = 1.4426950408889634

LANES = 128
VMEM_LIMIT_BYTES = 56 * 1024 * 1024

C_U, C_V, C_ZA, C_Q = 0, 512, 1024, 1536
C_KC, C_VC, C_KS, C_VS, C_KW, C_VW = 2048, 2176, 2304, 2432, 2560, 2688
C_G, C_ZB, C_END = 2816, 2944, 3456
N_GATE_COLS = 3 * N_HEADS

ROW_TILE = 512
OUT_ROW_TILE = 1024
Q_SUB = 4
SEL_TILE = 512
DECODE_GROUP = 4
WIN_SPAN = WINDOW + Q_BLOCK

_NT = (((1,), (1,)), ((), ()))


def _dot(a, b):
    return jnp.dot(a, b, preferred_element_type=F32)


def _dot_nt(a, b):
    return lax.dot_general(a, b, _NT, preferred_element_type=F32)


def _lane_iota(shape):
    return lax.broadcasted_iota(jnp.int32, shape, len(shape) - 1)


def _row_iota(shape):
    return lax.broadcasted_iota(jnp.int32, shape, len(shape) - 2)


def _rope(x, tab):
    c = tab[:, 0:LANES]
    s1 = tab[:, LANES:2 * LANES]
    return x * c + pltpu.roll(x, LANES - ROT_DIM // 2, 1) * s1 - pltpu.roll(x * s1, ROT_DIM // 2, 1)


def _rms_rows(x, g):
    ms = jnp.mean(x * x, axis=-1, keepdims=True)
    return x * lax.rsqrt(ms + NORM_EPS) * g


def _layer_norm_rows(v, g, b):
    mu = jnp.mean(v, axis=-1, keepdims=True)
    vc = v - mu
    var = jnp.mean(vc * vc, axis=-1, keepdims=True)
    return vc * lax.rsqrt(var + NORM_EPS) * g + b


def _head_slabs(x512):
    out = []
    rows = x512.shape[0]
    lane = _lane_iota((rows, LANES))
    lo = lane < HEAD_DIM
    for j in range(4):
        slab = x512[:, j * LANES:(j + 1) * LANES]
        swapped = pltpu.roll(slab, HEAD_DIM, 1)
        if j < 2:
            out.append(jnp.where(lo, slab, 0.0))
            out.append(jnp.where(lo, swapped, 0.0))
        else:
            out.append(jnp.where(lo, 0.0, swapped))
            out.append(jnp.where(lo, 0.0, slab))
    return out


def _project(x_ref, g_ref, w_refs, z_scr):
    hb = _rms_rows(x_ref[...], g_ref[...]).astype(BF16)
    wm_ref, wg_ref, wz_ref = w_refs
    z_scr[:, 0:C_G] = _dot_nt(hb, wm_ref[...])
    z_scr[:, C_G:C_ZB] = _dot_nt(hb, wg_ref[...])
    z_scr[:, C_ZB:C_END] = _dot_nt(hb, wz_ref[...])


def _inproj_prompt_kernel(x_ref, g_ref, wm_ref, wg_ref, wz_ref, lng_ref, lnb_ref, ws_ref, bias_ref, rope_ref,
                          aout_ref, qt_ref, kc_ref, vc_ref,
                          kct_ref, vct_ref, kst_ref, vst_ref, kwt_ref, vwt_ref,
                          ksa_ref, kwb_ref, vsb_ref, vwb_ref, gate_ref, szb_ref,
                          z_scr, kstage_scr, vstage_scr, *, tm, tiles_per_batch):
    _project(x_ref, g_ref, (wm_ref, wg_ref, wz_ref), z_scr)
    tab = rope_ref[...]

    lane = _lane_iota((CHUNK, LANES))
    lo = lane < HEAD_DIM
    for c in range(tm // CHUNK):
        r0 = c * CHUNK
        u = z_scr[r0:r0 + CHUNK, C_U:C_U + D_A]
        v = z_scr[r0:r0 + CHUNK, C_V:C_V + D_A]
        za = z_scr[r0:r0 + CHUNK, C_ZA:C_ZA + D_A]
        vn = _layer_norm_rows(v, lng_ref[...], lnb_ref[...])
        parts = []
        for p in range(A_GROUPS // 2):
            vp = vn[:, p * LANES:(p + 1) * LANES]
            v_lo = jnp.where(lo, vp, 0.0).astype(BF16)
            v_hi = jnp.where(lo, 0.0, vp).astype(BF16)
            parts.append(_dot(ws_ref[2 * p], v_lo) + _dot(ws_ref[2 * p + 1], v_hi))
        s = jnp.concatenate(parts, axis=1) + bias_ref[...]
        aout_ref[r0:r0 + CHUNK, :] = (u * s * jax.nn.silu(za)).astype(BF16)

    q = z_scr[:, C_Q:C_Q + D_B]
    qr = jnp.concatenate(
        [_rope(q[:, j * LANES:(j + 1) * LANES], tab) for j in range(4)], axis=1) * (HEAD_DIM ** -0.5 * LOG2_E)
    slabs = _head_slabs(qr)
    kc = z_scr[:, C_KC:C_KC + KV_W]
    vc = z_scr[:, C_VC:C_VC + KV_W]
    for src, stage, dst in ((kc, kstage_scr, kc_ref), (vc, vstage_scr, vc_ref)):
        stage[...] = src
        for r in range(CMP_STRIDE):
            dst[:, r * KV_W:(r + 1) * KV_W] = stage[pl.ds(r, tm // CMP_STRIDE, stride=CMP_STRIDE), :]
    ks = _rope(z_scr[:, C_KS:C_KS + KV_W], tab)
    vs = z_scr[:, C_VS:C_VS + KV_W]
    kw = _rope(z_scr[:, C_KW:C_KW + KV_W], tab)
    vw = z_scr[:, C_VW:C_VW + KV_W]
    gates = jax.nn.sigmoid(z_scr[:, C_G:C_G + LANES])
    pos = (pl.program_id(0) % tiles_per_batch) * tm + _row_iota((tm, LANES))
    onehot = jnp.where((pos >> SEL_SHIFT) == _lane_iota((tm, LANES)), 1.0, 0.0)
    ksa_ref[:, 0:LANES] = ks.astype(BF16)
    ksa_ref[:, LANES:2 * LANES] = onehot.astype(BF16)
    kwb_ref[...] = kw.astype(BF16)
    for j in range(tm // LANES):
        r0, r1 = j * LANES, (j + 1) * LANES
        kct_ref[0, :, r0:r1] = kc[r0:r1].T
        vct_ref[0, :, r0:r1] = vc[r0:r1].T
        kst_ref[0, :, r0:r1] = ks[r0:r1].T
        kwt_ref[0, :, r0:r1] = kw[r0:r1].T
        vs_t, vw_t = vs[r0:r1].T, vw[r0:r1].T
        vst_ref[0, :, r0:r1] = vs_t
        vwt_ref[0, :, r0:r1] = vw_t
        vsb_ref[0, j] = vs_t.astype(BF16)
        vwb_ref[0, j] = vw_t.astype(BF16)
        for e in range(N_HEADS):
            qt_ref[0, j, :, e * Q_BLOCK:(e + 1) * Q_BLOCK] = slabs[e][r0:r1].T.astype(BF16)
        gate_ref[0, j] = gates[r0:r1].T
    szb_ref[...] = jax.nn.silu(z_scr[:, C_ZB:C_ZB + D_B])


def _inproj_prompt(x2, g, w, lng, lnb, ws_bf, bias_full, rope_tab, seq, tm=ROW_TILE):
    n = x2.shape[0]
    tiles_per_batch = seq // tm
    bsz = n // seq
    n_tok_tiles = seq // LANES
    row = lambda i: (i, 0)
    const2 = lambda i: (0, 0)
    rows2d = lambda width, dtype: (jax.ShapeDtypeStruct((n, width), dtype), pl.BlockSpec((tm, width), row))
    kv_t = (jax.ShapeDtypeStruct((bsz, KV_W, seq), F32),
            pl.BlockSpec((1, KV_W, tm), lambda i: (i // tiles_per_batch, 0, i % tiles_per_batch)))
    chunk_rows = (jax.ShapeDtypeStruct((n // CMP_STRIDE, CMP_STRIDE * KV_W), F32),
                  pl.BlockSpec((tm // CMP_STRIDE, CMP_STRIDE * KV_W), row))
    tiles = lambda r, c, dtype: (
        jax.ShapeDtypeStruct((bsz, n_tok_tiles, r, c), dtype),
        pl.BlockSpec((1, tm // LANES, r, c), lambda i: (i // tiles_per_batch, i % tiles_per_batch, 0, 0)))
    outs = [
        rows2d(D_A, BF16),
        tiles(KV_W, N_HEADS * Q_BLOCK, BF16),
        chunk_rows, chunk_rows,
        kv_t, kv_t, kv_t, kv_t, kv_t, kv_t,
        rows2d(2 * KV_W, BF16),
        rows2d(KV_W, BF16),
        tiles(KV_W, LANES, BF16),
        tiles(KV_W, LANES, BF16),
        tiles(LANES, LANES, F32),
        rows2d(D_B, F32),
    ]
    out_shapes = [o[0] for o in outs]
    out_specs = [o[1] for o in outs]
    in_specs = [
        pl.BlockSpec((tm, D_MODEL), row),
        pl.BlockSpec((1, D_MODEL), const2),
        pl.BlockSpec((C_G, D_MODEL), const2),
        pl.BlockSpec((C_ZB - C_G, D_MODEL), const2),
        pl.BlockSpec((C_END - C_ZB, D_MODEL), const2),
        pl.BlockSpec((1, D_A), const2),
        pl.BlockSpec((1, D_A), const2),
        pl.BlockSpec((A_GROUPS, CHUNK, CHUNK), lambda i: (0, 0, 0)),
        pl.BlockSpec((CHUNK, D_A), const2),
        pl.BlockSpec((tm, 2 * LANES), lambda i: (i % tiles_per_batch, 0)),
    ]
    return pl.pallas_call(
        functools.partial(_inproj_prompt_kernel, tm=tm, tiles_per_batch=tiles_per_batch),
        grid=(n // tm,),
        in_specs=in_specs,
        out_specs=out_specs,
        out_shape=out_shapes,
        scratch_shapes=[pltpu.VMEM((tm, C_END), F32), pltpu.VMEM((tm, KV_W), F32), pltpu.VMEM((tm, KV_W), F32)],
        compiler_params=pltpu.CompilerParams(
            dimension_semantics=("arbitrary",), vmem_limit_bytes=VMEM_LIMIT_BYTES),
        name="inproj_prompt",
    )(x2, g, *w, lng, lnb, ws_bf, bias_full, rope_tab)


def _inproj_decode_kernel(x_ref, g_ref, wm_ref, wg_ref, wz_ref, lng_ref, lnb_ref, wsv_ref, bsv_ref, rope_ref,
                          aout_ref, qa_ref, kc_ref, vc_ref, ks_ref, vs_ref, kw_ref, vw_ref,
                          gate_ref, szb_ref, vn_ref, z_scr):
    _project(x_ref, g_ref, (wm_ref, wg_ref, wz_ref), z_scr)
    tab = rope_ref[...]
    u = z_scr[:, C_U:C_U + D_A]
    v = z_scr[:, C_V:C_V + D_A]
    za = z_scr[:, C_ZA:C_ZA + D_A]
    vn = _layer_norm_rows(v, lng_ref[...], lnb_ref[...])
    vn_ref[...] = vn
    s = vn * wsv_ref[...] + bsv_ref[...]
    aout_ref[...] = (u * s * jax.nn.silu(za)).astype(BF16)

    q = z_scr[:, C_Q:C_Q + D_B]
    qr = jnp.concatenate(
        [_rope(q[:, j * LANES:(j + 1) * LANES], tab) for j in range(4)], axis=1) * (HEAD_DIM ** -0.5)
    for e, slab in enumerate(_head_slabs(qr)):
        qa_ref[:, e * LANES:(e + 1) * LANES] = slab.astype(BF16)
    kc_ref[...] = z_scr[:, C_KC:C_KC + KV_W]
    vc_ref[...] = z_scr[:, C_VC:C_VC + KV_W]
    ks_ref[...] = _rope(z_scr[:, C_KS:C_KS + KV_W], tab)
    vs_ref[...] = z_scr[:, C_VS:C_VS + KV_W]
    kw_ref[...] = _rope(z_scr[:, C_KW:C_KW + KV_W], tab)
    vw_ref[...] = z_scr[:, C_VW:C_VW + KV_W]
    gate_ref[...] = jax.nn.sigmoid(z_scr[:, C_G:C_G + LANES])
    szb = jax.nn.silu(z_scr[:, C_ZB:C_ZB + D_B])
    for e, slab in enumerate(_head_slabs(szb)):
        szb_ref[:, e * LANES:(e + 1) * LANES] = slab


def _inproj_decode(x2, g, w, lng, lnb, wsv, bsv, rope_tab):
    n = x2.shape[0]
    out_shapes = [
        jax.ShapeDtypeStruct((n, D_A), BF16),
        jax.ShapeDtypeStruct((n, N_HEADS * LANES), BF16),
        jax.ShapeDtypeStruct((n, KV_W), F32),
        jax.ShapeDtypeStruct((n, KV_W), F32),
        jax.ShapeDtypeStruct((n, KV_W), F32),
        jax.ShapeDtypeStruct((n, KV_W), F32),
        jax.ShapeDtypeStruct((n, KV_W), F32),
        jax.ShapeDtypeStruct((n, KV_W), F32),
        jax.ShapeDtypeStruct((n, LANES), F32),
        jax.ShapeDtypeStruct((n, N_HEADS * LANES), F32),
        jax.ShapeDtypeStruct((n, D_A), F32),
    ]
    return pl.pallas_call(
        _inproj_decode_kernel,
        out_shape=out_shapes,
        scratch_shapes=[pltpu.VMEM((n, C_END), F32)],
        compiler_params=pltpu.CompilerParams(vmem_limit_bytes=VMEM_LIMIT_BYTES),
        name="inproj_decode",
    )(x2, g, *w, lng, lnb, wsv, bsv, rope_tab)


def _compress_chunks(ch, pos_term, w_ref):
    ab = _dot(ch.astype(BF16), w_ref[...])
    return ab[:, 0:KV_W] + pltpu.roll(ab[:, KV_W:2 * KV_W], ch.shape[0] - 1, 0) + pos_term


def _position_term(pe_ref, w_ref):
    pe = pe_ref[...]
    hi = pe.astype(BF16)
    r1 = pe - hi.astype(F32)
    mid = r1.astype(BF16)
    lo = (r1 - mid.astype(F32)).astype(BF16)
    pw = _dot(hi, w_ref[...]) + _dot(mid, w_ref[...]) + _dot(lo, w_ref[...])
    return pw[0:1, 0:KV_W] + pw[1:2, KV_W:2 * KV_W]


def _compress_prompt_kernel(kc_ref, vc_ref, wk_ref, wv_ref, pek_ref, pev_ref, rope_ref, kcc_ref, vcct_ref):
    kcc = _compress_chunks(kc_ref[0], _position_term(pek_ref, wk_ref), wk_ref)
    kcc_ref[0] = _rope(kcc, rope_ref[...]).astype(BF16)
    vcc = _compress_chunks(vc_ref[0], _position_term(pev_ref, wv_ref), wv_ref)
    for j in range(vcc.shape[0] // LANES):
        vcct_ref[0, :, j * LANES:(j + 1) * LANES] = vcc[j * LANES:(j + 1) * LANES].T.astype(BF16)


def _compress_prompt(kc3, vc3, wk, wv, pek, pev, rope_tab):
    b, c, width = kc3.shape
    blk = pl.BlockSpec((1, c, width), lambda i: (i, 0, 0))
    const2 = lambda i: (0, 0)
    return pl.pallas_call(
        _compress_prompt_kernel,
        grid=(b,),
        in_specs=[blk, blk,
                  pl.BlockSpec(wk.shape, const2), pl.BlockSpec(wv.shape, const2),
                  pl.BlockSpec(pek.shape, const2), pl.BlockSpec(pev.shape, const2),
                  pl.BlockSpec(rope_tab.shape, const2)],
        out_specs=[pl.BlockSpec((1, c, KV_W), lambda i: (i, 0, 0)),
                   pl.BlockSpec((1, KV_W, c), lambda i: (i, 0, 0))],
        out_shape=[jax.ShapeDtypeStruct((b, c, KV_W), BF16), jax.ShapeDtypeStruct((b, KV_W, c), BF16)],
        compiler_params=pltpu.CompilerParams(
            dimension_semantics=("arbitrary",), vmem_limit_bytes=VMEM_LIMIT_BYTES),
        name="compress_prompt",
    )(kc3, vc3, wk, wv, pek, pev, rope_tab)


def _select_blocks(imp, kk, candidate, forced):
    assert FORCE_SCORE > 2 * GQA
    start = jnp.where(candidate, jnp.where(forced, -jnp.inf, imp), -jnp.inf)
    score = start
    for _ in range(N_SELECT - N_FORCED):
        cm = jnp.max(score, axis=0, keepdims=True)
        first = jnp.min(jnp.where(score == cm, kk, LANES), axis=0, keepdims=True)
        score = jnp.where(kk == first, -jnp.inf, score)
    taken = (score == -jnp.inf) & (start > -jnp.inf)
    return candidate & (forced | taken)


def _softmax_rows(s):
    mx = jnp.max(s, axis=1, keepdims=True)
    ex = jnp.exp(s - mx)
    return ex * (1.0 / jnp.sum(ex, axis=1, keepdims=True))


def _tile_heads(x):
    return jnp.concatenate([x] * N_HEADS, axis=1)


def _values_by_kv_head(vt, p):
    half = GQA * Q_BLOCK
    return jnp.concatenate(
        [_dot(vt[h * HEAD_DIM:(h + 1) * HEAD_DIM], p[:, h * half:(h + 1) * half]) for h in range(N_KV)], axis=1)


def _attn_prompt_kernel(*refs):
    def body(sub, carry):
        _attn_query_block(sub, *refs)
        return carry

    lax.fori_loop(0, Q_SUB, body, 0)


def _attn_query_block(sub, qt_ref, gate_ref, szb_ref, kcc_ref, vcct_ref, ovlt_ref,
                      ksa_ref, vsb_ref, kwb_ref, vwb_ref, out_ref,
                      qaug_scr, acc_scr, sa_scr, sb_scr):
    i_blk = pl.program_id(1) * Q_SUB + sub
    q0 = i_blk * Q_BLOCK
    cols = N_HEADS * Q_BLOCK
    qt = qt_ref[0, sub]
    out_rows = pl.ds(pl.multiple_of(sub * Q_BLOCK, Q_BLOCK), Q_BLOCK)

    n_cmp = kcc_ref.shape[1]

    def compressed(extent):
        t_c = q0 + _lane_iota((extent, Q_BLOCK))
        seen = (_row_iota((extent, Q_BLOCK)) * CMP_STRIDE + (CMP_BLOCK - 1)) <= t_c
        s = _dot(kcc_ref[0, 0:extent, :], qt) + _tile_heads(jnp.where(seen, 0.0, NEG))
        mx = jnp.max(s, axis=0, keepdims=True)
        ex = jnp.exp2(s - mx)
        inv = jnp.where(mx > 0.5 * NEG, 1.0 / jnp.sum(ex, axis=0, keepdims=True), 0.0)
        p_c = ex * inv
        imps = []
        for h in range(N_KV):
            c0 = h * GQA * Q_BLOCK
            p_sum = (p_c[:, c0:c0 + Q_BLOCK] + p_c[:, c0 + Q_BLOCK:c0 + 2 * Q_BLOCK]
                     + p_c[:, c0 + 2 * Q_BLOCK:c0 + 3 * Q_BLOCK] + p_c[:, c0 + 3 * Q_BLOCK:c0 + 4 * Q_BLOCK])
            imps.append(_dot(ovlt_ref[:, 0:extent], p_sum.astype(BF16)))
        return (_values_by_kv_head(vcct_ref[0, :, 0:extent], p_c.astype(BF16)), *imps)

    n_need = (q0 + Q_BLOCK) // CMP_STRIDE
    extents = list(range(LANES, n_cmp + 1, LANES))
    o_c, *imp_t = lax.switch(
        jnp.minimum((n_need - 1) // LANES, len(extents) - 1),
        [functools.partial(compressed, ext) for ext in extents])

    w0 = pl.multiple_of(jnp.maximum(q0 - WINDOW, 0), Q_BLOCK)
    wj0 = jnp.maximum(i_blk - WINDOW // Q_BLOCK, 0)
    kwin = kwb_ref[0, pl.ds(w0, WIN_SPAN), :]
    vwt = jnp.concatenate([vwb_ref[0, wj0 + i] for i in range(WIN_SPAN // LANES)], axis=1)
    kp = w0 + _row_iota((WIN_SPAN, Q_BLOCK))
    tw = q0 + _lane_iota((WIN_SPAN, Q_BLOCK))
    in_win = jnp.where(kp <= tw, jnp.where(kp > tw - WINDOW, 0.0, NEG), NEG)
    sw = _dot(kwin, qt) + _tile_heads(in_win)
    exw = jnp.exp2(sw - jnp.max(sw, axis=0, keepdims=True))
    o_w = _values_by_kv_head(vwt, exw.astype(BF16)) * (1.0 / jnp.sum(exw, axis=0, keepdims=True))

    kk = _row_iota((LANES, Q_BLOCK))
    tq = q0 + _lane_iota((LANES, Q_BLOCK))
    valid = kk * SEL_BLOCK <= tq
    cur = tq >> SEL_SHIFT
    forced = (kk == 0) | (kk == cur) | (kk == cur - 1)
    both = lambda a: jnp.concatenate([a] * N_KV, axis=1)
    picked = _select_blocks(jnp.concatenate(imp_t, axis=1), both(kk), both(valid), both(forced))
    qaug_scr[0:LANES, :] = qt
    for h in range(N_KV):
        selneg = jnp.where(picked[:, h * Q_BLOCK:(h + 1) * Q_BLOCK], 0.0, NEG).astype(BF16)
        for g in range(GQA):
            c0 = (h * GQA + g) * Q_BLOCK
            qaug_scr[LANES:2 * LANES, c0:c0 + Q_BLOCK] = selneg

    acc_scr[...] = jnp.zeros(acc_scr.shape, F32)
    tiles_per_step = SEL_TILE // LANES

    def score(step, dst):
        t0 = pl.multiple_of(step * SEL_TILE, SEL_TILE)
        dst[...] = _dot(ksa_ref[0, pl.ds(t0, SEL_TILE), :], qaug_scr[...])

    def attend(src, step, m_prev, l_prev, causal_rows=None):
        rows = SEL_TILE if causal_rows is None else causal_rows
        vt = jnp.concatenate([vsb_ref[0, step * tiles_per_step + i] for i in range(rows // LANES)], axis=1)
        sc = src[0:rows, :]
        if causal_rows is not None:
            tok = step * SEL_TILE + _row_iota((rows, Q_BLOCK))
            sc = sc + _tile_heads(jnp.where(tok <= q0 + _lane_iota((rows, Q_BLOCK)), 0.0, NEG))
        m_next = jnp.maximum(m_prev, jnp.max(sc, axis=0, keepdims=True))
        p = jnp.exp2(sc - m_next)
        alpha = jnp.exp2(m_prev - m_next)
        l_next = alpha * l_prev + jnp.sum(p, axis=0, keepdims=True)
        acc_scr[...] = alpha * acc_scr[...] + _values_by_kv_head(vt, p.astype(BF16))
        return m_next, l_next

    n_full = q0 // SEL_TILE
    score(0, sa_scr)

    def pair(jj, carry):
        m_c, l_c = carry
        score(2 * jj + 1, sb_scr)
        m_c, l_c = attend(sa_scr, 2 * jj, m_c, l_c)
        score(2 * jj + 2, sa_scr)
        return attend(sb_scr, 2 * jj + 1, m_c, l_c)

    m_run, l_run = lax.fori_loop(
        0, n_full // 2, pair,
        (jnp.full((1, cols), -jnp.inf, F32), jnp.zeros((1, cols), F32)))

    def last_step(src, m_c, l_c):
        extents = list(range(Q_BLOCK, SEL_TILE + 1, Q_BLOCK))
        return lax.switch(
            (q0 % SEL_TILE) // Q_BLOCK,
            [lambda m, l, ext=ext: attend(src, n_full, m, l, causal_rows=ext)[1] for ext in extents],
            m_c, l_c)

    def odd_tail(m_c, l_c):
        score(n_full, sb_scr)
        m_c, l_c = attend(sa_scr, n_full - 1, m_c, l_c)
        return last_step(sb_scr, m_c, l_c)

    def even_tail(m_c, l_c):
        return last_step(sa_scr, m_c, l_c)

    l_run = lax.cond(n_full % 2 == 1, odd_tail, even_tail, m_run, l_run)

    gt = gate_ref[0, sub]
    o_s = acc_scr[...] * (1.0 / l_run)
    heads = []
    for e in range(N_HEADS):
        c0 = e * Q_BLOCK
        heads.append(gt[3 * e:3 * e + 1, :] * o_c[:, c0:c0 + Q_BLOCK]
                     + gt[3 * e + 1:3 * e + 2, :] * o_s[:, c0:c0 + Q_BLOCK]
                     + gt[3 * e + 2:3 * e + 3, :] * o_w[:, c0:c0 + Q_BLOCK])
    for j in range(N_HEADS // 2):
        slab = jnp.concatenate([heads[2 * j], heads[2 * j + 1]], axis=0).T
        out_ref[0, out_rows, j * LANES:(j + 1) * LANES] = (
            slab * szb_ref[0, out_rows, j * LANES:(j + 1) * LANES]).astype(BF16)


def _attn_prompt(qt4, gate4, szb3, kcc, vcct, ovlt, ksa3, vsb4, kwb3, vwb4):
    b, nq = qt4.shape[0], qt4.shape[1]
    t = nq * Q_BLOCK
    cols = N_HEADS * Q_BLOCK
    assert nq % Q_SUB == 0
    qtile = lambda a: pl.BlockSpec((1, Q_SUB) + a.shape[2:], lambda bi, i: (bi, i, 0, 0))
    rows = pl.BlockSpec((1, Q_SUB * Q_BLOCK, D_B), lambda bi, i: (bi, i, 0))
    seq = lambda a: pl.BlockSpec((1,) + a.shape[1:], lambda bi, i, nd=a.ndim: (bi,) + (0,) * (nd - 1))
    return pl.pallas_call(
        _attn_prompt_kernel,
        grid=(b, nq // Q_SUB),
        in_specs=[qtile(qt4), qtile(gate4), rows,
                  seq(kcc), seq(vcct), pl.BlockSpec(ovlt.shape, lambda bi, i: (0, 0)),
                  seq(ksa3), seq(vsb4), seq(kwb3), seq(vwb4)],
        out_specs=rows,
        out_shape=jax.ShapeDtypeStruct((b, t, D_B), BF16),
        scratch_shapes=[pltpu.VMEM((2 * LANES, cols), BF16),
                        pltpu.VMEM((HEAD_DIM, cols), F32),
                        pltpu.VMEM((SEL_TILE, cols), F32),
                        pltpu.VMEM((SEL_TILE, cols), F32)],
        compiler_params=pltpu.CompilerParams(
            dimension_semantics=("arbitrary", "arbitrary"), vmem_limit_bytes=VMEM_LIMIT_BYTES),
        name="attn_prompt",
    )(qt4, gate4, szb3, kcc, vcct, ovlt, ksa3, vsb4, kwb3, vwb4)


def _outproj_kernel(x_ref, a_ref, m_ref, wa_ref, wb_ref, fg_ref, y_ref):
    y = x_ref[...] + _dot(a_ref[...], wa_ref[...]) + _dot(m_ref[...].astype(BF16), wb_ref[...])
    y_ref[...] = _rms_rows(y, fg_ref[...])


def _outproj(x2, a_out, mix_b, wa, wb, fg, tm):
    n = x2.shape[0]
    row = lambda i: (i, 0)
    const2 = lambda i: (0, 0)
    return pl.pallas_call(
        _outproj_kernel,
        grid=(n // tm,),
        in_specs=[pl.BlockSpec((tm, D_MODEL), row),
                  pl.BlockSpec((tm, a_out.shape[1]), row),
                  pl.BlockSpec((tm, mix_b.shape[1]), row),
                  pl.BlockSpec(wa.shape, const2), pl.BlockSpec(wb.shape, const2),
                  pl.BlockSpec((1, D_MODEL), const2)],
        out_specs=pl.BlockSpec((tm, D_MODEL), row),
        out_shape=jax.ShapeDtypeStruct((n, D_MODEL), F32),
        compiler_params=pltpu.CompilerParams(
            dimension_semantics=("arbitrary",), vmem_limit_bytes=VMEM_LIMIT_BYTES),
        name="outproj",
    )(x2, a_out, mix_b, wa, wb, fg)


def _page_copies(pt_ref, pool_ref, buf_ref, sem_ref, step, slot):
    group, n_pages = buf_ref.shape[1], buf_ref.shape[2]
    return [pltpu.make_async_copy(pool_ref.at[pt_ref[step * group + g, p]], buf_ref.at[slot, g, p],
                                  sem_ref.at[slot])
            for g in range(group) for p in range(n_pages)]


def _gather_pages(pt_ref, pools, bufs, sems):
    b = pl.program_id(0)
    slot = lax.rem(b, 2)

    def start(step, sl):
        for pool, buf, sem in zip(pools, bufs, sems):
            for cp in _page_copies(pt_ref, pool, buf, sem, step, sl):
                cp.start()

    @pl.when(b == 0)
    def _():
        start(0, 0)

    @pl.when(b + 1 < pl.num_programs(0))
    def _():
        start(b + 1, 1 - slot)

    for pool, buf, sem in zip(pools, bufs, sems):
        for cp in _page_copies(pt_ref, pool, buf, sem, b, slot):
            cp.wait()
    return slot


def _decode_cmp_kernel(pt_ref, ck_hbm, cv_hbm, q_ref, wk_ref, wv_ref, pek_ref, pev_ref, rope_ref, ovl_ref,
                       oc_ref, psum_ref, kbuf, vbuf, ksem, vsem, ktok_scr, vtok_scr, kpos_scr, vpos_scr):
    slot = _gather_pages(pt_ref, (ck_hbm, cv_hbm), (kbuf, vbuf), (ksem, vsem))
    group, n_pages = kbuf.shape[1], kbuf.shape[2]

    @pl.when(pl.program_id(0) == 0)
    def _():
        kpos_scr[...] = _position_term(pek_ref, wk_ref)
        vpos_scr[...] = _position_term(pev_ref, wv_ref)

    def chunk_rows(buf, tok_scr, g):
        for p in range(n_pages):
            tok_scr[g, p * PAGE_SIZE:(p + 1) * PAGE_SIZE, :] = buf[slot, g, p].T
        n = tok_scr.shape[1] // CMP_STRIDE
        return jnp.concatenate(
            [tok_scr[g, pl.ds(r, n, stride=CMP_STRIDE), :] for r in range(CMP_STRIDE)], axis=1)

    for g in range(group):
        kch = chunk_rows(kbuf, ktok_scr, g)
        vch = chunk_rows(vbuf, vtok_scr, g)
        kcc = _rope(_compress_chunks(kch, kpos_scr[...], wk_ref), rope_ref[...]).astype(BF16)
        vcc = _compress_chunks(vch, vpos_scr[...], wv_ref).astype(BF16)
        n_chunks = kch.shape[0]
        q = q_ref[g]
        rows = q.shape[0]
        s = _dot_nt(q, kcc)
        seen = _lane_iota((rows, n_chunks)) < n_chunks - 1
        p_c = jnp.where(seen, _softmax_rows(jnp.where(seen, s, NEG)), 0.0)
        oc_ref[g] = _dot(p_c.astype(BF16), vcc)[0:N_HEADS]
        ps0 = jnp.sum(p_c[0:GQA], axis=0, keepdims=True)
        ps1 = jnp.sum(p_c[GQA:2 * GQA], axis=0, keepdims=True)
        psum = jnp.where(_row_iota((rows, n_chunks)) < N_HEADS // 2, ps0, ps1)
        psum_ref[g] = _dot(psum.astype(BF16), ovl_ref[...])[0:N_HEADS]


def _decode_cmp(page_table, ck_pool, cv_pool, q16, wk, wv, pek, pev, rope_tab, ovl):
    nb, n_pages = page_table.shape
    group = DECODE_GROUP
    assert nb % group == 0
    hbm = pl.BlockSpec(memory_space=pl.ANY)
    page_buf = pltpu.VMEM((2, group, n_pages, KV_W, PAGE_SIZE), F32)
    tok_buf = pltpu.VMEM((group, n_pages * PAGE_SIZE, KV_W), F32)
    const2 = lambda b, pt: (0, 0)
    per_b = lambda a: pl.BlockSpec((group,) + a.shape[1:], lambda b, pt: (b, 0, 0))
    in_specs = [hbm, hbm, per_b(q16),
                pl.BlockSpec(wk.shape, const2), pl.BlockSpec(wv.shape, const2),
                pl.BlockSpec(pek.shape, const2), pl.BlockSpec(pev.shape, const2),
                pl.BlockSpec(rope_tab.shape, const2), pl.BlockSpec(ovl.shape, const2)]
    out_blk = pl.BlockSpec((group, N_HEADS, LANES), lambda b, pt: (b, 0, 0))
    grid_spec = pltpu.PrefetchScalarGridSpec(
        num_scalar_prefetch=1, grid=(nb // group,), in_specs=in_specs, out_specs=[out_blk, out_blk],
        scratch_shapes=[page_buf, page_buf, pltpu.SemaphoreType.DMA((2,)), pltpu.SemaphoreType.DMA((2,)),
                        tok_buf, tok_buf, pltpu.VMEM((1, KV_W), F32), pltpu.VMEM((1, KV_W), F32)])
    return pl.pallas_call(
        _decode_cmp_kernel,
        grid_spec=grid_spec,
        out_shape=[jax.ShapeDtypeStruct((nb, N_HEADS, LANES), F32)] * 2,
        compiler_params=pltpu.CompilerParams(
            dimension_semantics=("arbitrary",), vmem_limit_bytes=VMEM_LIMIT_BYTES),
        name="decode_cmp",
    )(page_table, ck_pool, cv_pool, q16, wk, wv, pek, pev, rope_tab, ovl)


def _decode_topk_kernel(imp_ref, sel_ref, *, n_blk, cur):
    n = imp_ref.shape[0]
    tiles = [imp_ref[i * LANES:(i + 1) * LANES, :].T for i in range(n // LANES)]
    imp_t = jnp.concatenate(tiles, axis=1)
    kk = _row_iota(imp_t.shape)
    forced = (kk == 0) | (kk == cur) | (kk == cur - 1)
    selneg = jnp.where(_select_blocks(imp_t, kk, kk < n_blk, forced), 0.0, NEG)
    for i in range(n // LANES):
        sel_ref[i * LANES:(i + 1) * LANES, :] = selneg[:, i * LANES:(i + 1) * LANES].T


def _decode_topk(imp2, n_blk, cur):
    return pl.pallas_call(
        functools.partial(_decode_topk_kernel, n_blk=n_blk, cur=cur),
        out_shape=jax.ShapeDtypeStruct(imp2.shape, F32),
        compiler_params=pltpu.CompilerParams(vmem_limit_bytes=VMEM_LIMIT_BYTES),
        name="decode_topk",
    )(imp2)


def _decode_attend(s, q32, vals_t_bf, k_new, v_new, bias_new):
    kn = k_new.astype(BF16).astype(F32)
    s_new = jnp.sum(q32 * kn, axis=1, keepdims=True) + bias_new
    mx = jnp.maximum(jnp.max(s, axis=1, keepdims=True), s_new)
    ex = jnp.exp(s - mx)
    ex_new = jnp.exp(s_new - mx)
    inv = 1.0 / (jnp.sum(ex, axis=1, keepdims=True) + ex_new)
    p = (ex * inv).astype(BF16)
    p_new = (ex_new * inv).astype(BF16).astype(F32)
    return _dot_nt(p, vals_t_bf) + p_new * v_new.astype(BF16).astype(F32)


P_Q, P_SEL, P_OC, P_SZB, P_NEW, P_GATE, P_ROWS = 0, 16, 32, 40, 48, 56, 64


def _row_to_col(row):
    n = row.shape[1]
    diag = _row_iota((n, n)) == _lane_iota((n, n))
    return jnp.sum(jnp.where(diag, row, 0.0), axis=1, keepdims=True)


def _decode_sel_kernel(pt_ref, ks_hbm, vs_hbm, ckw_ref, cvw_ref, pack_ref, onehot_ref,
                       mix_ref, okw_ref, ovw_ref, kbuf, vbuf, ksem, vsem, *, cur):
    slot = _gather_pages(pt_ref, (ks_hbm, vs_hbm), (kbuf, vbuf), (ksem, vsem))
    group, n_pages = kbuf.shape[1], kbuf.shape[2]
    for g in range(group):
        pk = pack_ref[g]
        q32 = pk[P_Q:P_Q + 2 * N_HEADS]
        q = q32.astype(BF16)
        sel = pk[P_SEL:P_SEL + 2 * N_HEADS].astype(BF16)
        rows = q.shape[0]
        ks_new, vs_new = pk[P_NEW:P_NEW + 1], pk[P_NEW + 1:P_NEW + 2]
        kw_new, vw_new = pk[P_NEW + 2:P_NEW + 3], pk[P_NEW + 3:P_NEW + 4]

        kt = jnp.concatenate([kbuf[slot, g, p] for p in range(n_pages)], axis=1).astype(BF16)
        vt = jnp.concatenate([vbuf[slot, g, p] for p in range(n_pages)], axis=1).astype(BF16)
        kaug = jnp.concatenate([kt, onehot_ref[...]], axis=0)
        qaug = jnp.concatenate([q, sel], axis=1)
        s_sel = _dot(qaug, kaug)
        o_s = _decode_attend(s_sel, q32, vt, ks_new, vs_new, sel.astype(F32)[:, cur:cur + 1])

        kwin = ckw_ref[g]
        vwin = cvw_ref[g]
        keep = kwin.shape[1]
        visible = _lane_iota((rows, keep)) > keep - WINDOW
        s_win = jnp.where(visible, _dot(q, kwin.astype(BF16)), NEG)
        o_w = _decode_attend(s_win, q32, vwin.astype(BF16), kw_new, vw_new, 0.0)

        gates = pk[P_GATE:P_GATE + N_HEADS]
        o = (gates[:, 0:1] * pk[P_OC:P_OC + N_HEADS] + gates[:, 1:2] * o_s[0:N_HEADS]
             + gates[:, 2:3] * o_w[0:N_HEADS])
        mix_ref[g] = o * pk[P_SZB:P_SZB + N_HEADS]

        last = _lane_iota((KV_W, keep)) == keep - 1
        okw_ref[g] = jnp.where(last, _row_to_col(kw_new), pltpu.roll(kwin, keep - 1, 1))
        ovw_ref[g] = jnp.where(last, _row_to_col(vw_new), pltpu.roll(vwin, keep - 1, 1))


def _decode_sel(page_table, ks_pool, vs_pool, ckw, cvw, pack, onehot, cur):
    nb, n_pages = page_table.shape
    group = DECODE_GROUP
    assert nb % group == 0
    keep = ckw.shape[2]
    hbm = pl.BlockSpec(memory_space=pl.ANY)
    page_buf = pltpu.VMEM((2, group, n_pages, KV_W, PAGE_SIZE), F32)
    per_b = lambda a: pl.BlockSpec((group,) + a.shape[1:], lambda b, pt: (b, 0, 0))
    in_specs = [hbm, hbm, per_b(ckw), per_b(cvw), per_b(pack),
                pl.BlockSpec(onehot.shape, lambda b, pt: (0, 0))]
    out_specs = [pl.BlockSpec((group, N_HEADS, LANES), lambda b, pt: (b, 0, 0)),
                 pl.BlockSpec((group, KV_W, keep), lambda b, pt: (b, 0, 0)),
                 pl.BlockSpec((group, KV_W, keep), lambda b, pt: (b, 0, 0))]
    grid_spec = pltpu.PrefetchScalarGridSpec(
        num_scalar_prefetch=1, grid=(nb // group,), in_specs=in_specs, out_specs=out_specs,
        scratch_shapes=[page_buf, page_buf, pltpu.SemaphoreType.DMA((2,)), pltpu.SemaphoreType.DMA((2,))])
    return pl.pallas_call(
        functools.partial(_decode_sel_kernel, cur=cur),
        grid_spec=grid_spec,
        out_shape=[jax.ShapeDtypeStruct((nb, N_HEADS, LANES), F32),
                   jax.ShapeDtypeStruct((nb, KV_W, keep), F32),
                   jax.ShapeDtypeStruct((nb, KV_W, keep), F32)],
        compiler_params=pltpu.CompilerParams(
            dimension_semantics=("arbitrary",), vmem_limit_bytes=VMEM_LIMIT_BYTES),
        name="decode_sel",
    )(page_table, ks_pool, vs_pool, ckw, cvw, pack, onehot)


def _rope_table(pos):
    half = ROT_DIM // 2
    inv = jnp.power(jnp.float32(ROPE_THETA), -jnp.arange(half, dtype=F32) / half)
    ang = pos.astype(F32)[:, None] * inv[None, :]
    cs = jnp.concatenate([jnp.cos(ang), jnp.sin(ang)], axis=1)
    lane = jnp.arange(2 * LANES)
    seg, l64 = lane // LANES, lane % HEAD_DIM
    src = jnp.where(seg == 0, l64 % half, half + l64 % half)
    sign = jnp.where(seg == 0, l64 < ROT_DIM, l64 < half).astype(F32)
    sign = jnp.where(seg == 1, -sign, sign)
    place = (jnp.arange(ROT_DIM)[:, None] == src[None, :]).astype(F32) * sign[None, :]
    ones = ((seg == 0) & (l64 >= ROT_DIM)).astype(F32)
    return jnp.dot(cs, place, precision=lax.Precision.HIGHEST) + ones[None, :]


def _compress_weights(w_c, pe_c):
    w4 = w_c.reshape(2, CMP_STRIDE, HEAD_DIM, HEAD_DIM)
    wb = jnp.einsum('arde,hg->rhdage', w4, jnp.eye(N_KV, dtype=w_c.dtype))
    wb = wb.reshape(CMP_STRIDE * KV_W, 2 * KV_W).astype(BF16)
    pe = jnp.broadcast_to(pe_c.reshape(2, CMP_STRIDE, 1, HEAD_DIM), (2, CMP_STRIDE, N_KV, HEAD_DIM))
    return wb, jnp.pad(pe.reshape(2, CMP_STRIDE * KV_W), ((0, 14), (0, 0)))


def _overlap_matrix(n_rows, n_blk):
    cs = jnp.arange(n_rows, dtype=jnp.int32)[:, None] * CMP_STRIDE
    bs = jnp.arange(LANES, dtype=jnp.int32)[None, :] * SEL_BLOCK
    hit = (cs < bs + SEL_BLOCK) & (cs + CMP_BLOCK > bs) & (jnp.arange(LANES)[None, :] < n_blk)
    return hit.astype(BF16)


def kernel(x_prompt, x_sample, cache_k_cmp, cache_v_cmp, cache_k_sel, cache_v_sel, cache_k_win,
           cache_v_win, page_table, norm_g, w_in, ln_v_g, ln_v_b, w_s, b_s, w_ck, pe_ck, w_cv,
           pe_cv, w_out, final_g):
    depth = norm_g.shape[0]
    assert depth == 1
    bsz, seq, _ = x_prompt.shape
    nb, dec_seq, _ = x_sample.shape
    assert dec_seq == 1 and seq % SEL_TILE == 0 and seq >= WIN_SPAN and (seq // CMP_STRIDE) % LANES == 0
    n_pages = page_table.shape[1]
    past = n_pages * PAGE_SIZE
    keep = cache_k_win.shape[2]
    assert keep == WINDOW and past % SEL_BLOCK == 0

    w = w_in[0]
    wt = jnp.transpose(w).astype(BF16)
    w_pad = (wt[:C_G],
             jnp.pad(wt[C_G:C_G + N_GATE_COLS], ((0, LANES - N_GATE_COLS), (0, 0))),
             wt[C_G + N_GATE_COLS:])
    g_row = norm_g[0][None, :]
    lng = ln_v_g[0][None, :]
    lnb = ln_v_b[0][None, :]
    fg = final_g[None, :]
    wk_c, pek = _compress_weights(w_ck[0], pe_ck[0])
    wv_c, pev = _compress_weights(w_cv[0], pe_cv[0])
    w_o = w_out[0].astype(BF16)
    wo_a, wo_b = w_o[:D_A], w_o[D_A:]

    tril = jnp.tril(jnp.ones((CHUNK, CHUNK), w_s.dtype))
    ws_bf = (w_s[0] * tril).astype(BF16)
    bias_full = jnp.repeat(b_s[0].T, HEAD_DIM, axis=1)
    xp2 = x_prompt.reshape(bsz * seq, D_MODEL)
    (a_out, qt, kc, vc, kc_t, vc_t, ks_t, vs_t, kw_t, vw_t, ksa, kwb, vsb, vwb, gates_t, szb) = _inproj_prompt(
        xp2, g_row, w_pad, lng, lnb, ws_bf, bias_full,
        _rope_table(jnp.arange(seq, dtype=jnp.int32)), seq)
    n_chunks = seq // CMP_STRIDE
    kcc, vcc_t = _compress_prompt(
        kc.reshape(bsz, n_chunks, CMP_STRIDE * KV_W), vc.reshape(bsz, n_chunks, CMP_STRIDE * KV_W),
        wk_c, wv_c, pek, pev,
        _rope_table(jnp.arange(n_chunks, dtype=jnp.int32) * CMP_STRIDE))
    b3 = lambda a: a.reshape(bsz, seq, a.shape[-1])
    mix_p = _attn_prompt(qt, gates_t, b3(szb), kcc, vcc_t,
                         _overlap_matrix(n_chunks, seq // SEL_BLOCK).T,
                         b3(ksa), vsb, b3(kwb), vwb)
    y_prompt = _outproj(xp2, a_out, mix_p.reshape(bsz * seq, D_B), wo_a, wo_b, fg, OUT_ROW_TILE)
    y_prompt = y_prompt.reshape(bsz, seq, D_MODEL)
    st = lambda a: jnp.transpose(a.reshape(a.shape[0], N_KV, HEAD_DIM, a.shape[2]), (0, 3, 1, 2))[None]
    keep_p = min(WINDOW, seq)

    xs2 = x_sample.reshape(nb, D_MODEL)
    wsv = jnp.repeat(w_s[0][:, 0, 0], HEAD_DIM)[None, :]
    bsv = jnp.repeat(b_s[0][:, 0], HEAD_DIM)[None, :]
    (a_s, qa_s, kc_s, vc_s, ks_s, vs_s, kw_s, vw_s, gates_s, szb_s, vn_s) = _inproj_decode(
        xs2, g_row, w_pad, lng, lnb, wsv, bsv,
        _rope_table(jnp.full((nb,), past, dtype=jnp.int32)))
    q16 = jnp.pad(qa_s.reshape(nb, N_HEADS, LANES), ((0, 0), (0, N_HEADS), (0, 0)))
    d_chunks = past // CMP_STRIDE
    n_blk = -(-(past + 1) // SEL_BLOCK)
    cur = past // SEL_BLOCK
    kv_t = lambda c: jnp.transpose(c[0], (0, 2, 3, 1)).reshape(c.shape[1], KV_W, c.shape[2])
    oc_s, imp_s = _decode_cmp(
        page_table, kv_t(cache_k_cmp), kv_t(cache_v_cmp),
        q16, wk_c, wv_c, pek, pev,
        _rope_table(jnp.arange(d_chunks, dtype=jnp.int32) * CMP_STRIDE),
        _overlap_matrix(d_chunks, n_blk))
    selneg = _decode_topk(imp_s[:, ::GQA, :].reshape(nb * N_KV, LANES), n_blk, cur)
    sel16 = jnp.pad(jnp.repeat(selneg.reshape(nb, N_KV, LANES), GQA, axis=1), ((0, 0), (0, N_HEADS), (0, 0)))
    tok_blk = jnp.arange(past, dtype=jnp.int32)[None, :] // SEL_BLOCK
    onehot = (tok_blk == jnp.arange(LANES, dtype=jnp.int32)[:, None]).astype(BF16)
    gate_rows = jnp.pad(gates_s[:, :N_GATE_COLS].reshape(nb, N_HEADS, 3), ((0, 0), (0, 0), (0, LANES - 3)))
    new_rows = jnp.pad(jnp.stack([ks_s, vs_s, kw_s, vw_s], axis=1), ((0, 0), (0, P_GATE - P_NEW - 4), (0, 0)))
    pack = jnp.concatenate(
        [q16.astype(F32), sel16, oc_s, szb_s.reshape(nb, N_HEADS, LANES), new_rows, gate_rows], axis=1)
    assert pack.shape[1] == P_ROWS
    mix_s, okw, ovw = _decode_sel(
        page_table, kv_t(cache_k_sel), kv_t(cache_v_sel), kv_t(cache_k_win), kv_t(cache_v_win),
        pack, onehot, cur)
    wo_slab = jnp.zeros((N_HEADS, N_KV, HEAD_DIM, D_MODEL), BF16)
    wo_heads = wo_b.reshape(N_HEADS, HEAD_DIM, D_MODEL)
    for e in range(N_HEADS):
        wo_slab = wo_slab.at[e, e // GQA].set(wo_heads[e])
    y_sample = _outproj(xs2, a_s, mix_s.reshape(nb, N_HEADS * LANES), wo_a,
                        wo_slab.reshape(N_HEADS * LANES, D_MODEL), fg, nb)
    y_sample = y_sample.reshape(nb, 1, D_MODEL)
    ss = lambda a: a.reshape(1, nb, 1, N_KV, HEAD_DIM)

    return (y_prompt, y_sample,
            st(kc_t), st(vc_t), st(ks_t), st(vs_t),
            st(kw_t[:, :, seq - keep_p:]), st(vw_t[:, :, seq - keep_p:]),
            ss(kc_s), ss(vc_s), ss(ks_s), ss(vs_s),
            st(okw), st(ovw),
            vn_s.reshape(1, nb, 1, D_A))
```

```python
import functools

import jax
import jax.numpy as jnp
from jax import lax
from jax.experimental import pallas as pl
from jax.experimental.pallas import tpu as pltpu

F32 = jnp.float32
BF16 = jnp.bfloat16

D_MODEL = 1024
HEAD_DIM = 64
D_A = 512
D_B = 512
A_GROUPS = 8
CHUNK = 128
N_HEADS = 8
N_KV = 2
GQA = 4
KV_W = 128
ROT_DIM = 16
ROPE_THETA = 500000.0
CMP_BLOCK = 32
CMP_STRIDE = 16
SEL_BLOCK = 64
SEL_SHIFT = 6
N_SELECT = 16
WINDOW = 512
Q_BLOCK = 128
PAGE_SIZE = 128
NORM_EPS = 1e-6
FORCE_SCORE = 1e4
N_FORCED = 3
NEG = -1e30
LOG2_E = 1.4426950408889634

LANES = 128
VMEM_LIMIT_BYTES = 56 * 1024 * 1024

C_U, C_V, C_ZA, C_Q = 0, 512, 1024, 1536
C_KC, C_VC, C_KS, C_VS, C_KW, C_VW = 2048, 2176, 2304, 2432, 2560, 2688
C_G, C_ZB, C_END = 2816, 2944, 3456
N_GATE_COLS = 3 * N_HEADS

ROW_TILE = 512
OUT_ROW_TILE = 1024
Q_SUB = 4
SEL_TILE = 512
DECODE_GROUP = 4
WIN_SPAN = WINDOW + Q_BLOCK

_NT = (((1,), (1,)), ((), ()))


def _dot(a, b):
    return jnp.dot(a, b, preferred_element_type=F32)


def _dot_nt(a, b):
    return lax.dot_general(a, b, _NT, preferred_element_type=F32)


def _lane_iota(shape):
    return lax.broadcasted_iota(jnp.int32, shape, len(shape) - 1)


def _row_iota(shape):
    return lax.broadcasted_iota(jnp.int32, shape, len(shape) - 2)


def _rope(x, tab):
    c = tab[:, 0:LANES]
    s1 = tab[:, LANES:2 * LANES]
    return x * c + pltpu.roll(x, LANES - ROT_DIM // 2, 1) * s1 - pltpu.roll(x * s1, ROT_DIM // 2, 1)


def _rms_rows(x, g):
    ms = jnp.mean(x * x, axis=-1, keepdims=True)
    return x * lax.rsqrt(ms + NORM_EPS) * g


def _layer_norm_rows(v, g, b):
    mu = jnp.mean(v, axis=-1, keepdims=True)
    vc = v - mu
    var = jnp.mean(vc * vc, axis=-1, keepdims=True)
    return vc * lax.rsqrt(var + NORM_EPS) * g + b


def _head_slabs(x512):
    out = []
    rows = x512.shape[0]
    lane = _lane_iota((rows, LANES))
    lo = lane < HEAD_DIM
    for j in range(4):
        slab = x512[:, j * LANES:(j + 1) * LANES]
        swapped = pltpu.roll(slab, HEAD_DIM, 1)
        if j < 2:
            out.append(jnp.where(lo, slab, 0.0))
            out.append(jnp.where(lo, swapped, 0.0))
        else:
            out.append(jnp.where(lo, 0.0, swapped))
            out.append(jnp.where(lo, 0.0, slab))
    return out


def _project(x_ref, g_ref, w_refs, z_scr):
    hb = _rms_rows(x_ref[...], g_ref[...]).astype(BF16)
    wm_ref, wg_ref, wz_ref = w_refs
    z_scr[:, 0:C_G] = _dot(hb, wm_ref[...])
    z_scr[:, C_G:C_ZB] = _dot(hb, wg_ref[...])
    z_scr[:, C_ZB:C_END] = _dot(hb, wz_ref[...])


def _inproj_prompt_kernel(x_ref, g_ref, wm_ref, wg_ref, wz_ref, lng_ref, lnb_ref, ws_ref, bias_ref, rope_ref,
                          aout_ref, qt_ref, kc_ref, vc_ref,
                          kct_ref, vct_ref, kst_ref, vst_ref, kwt_ref, vwt_ref,
                          ksa_ref, kwb_ref, vsb_ref, vwb_ref, gate_ref, szb_ref,
                          z_scr, kstage_scr, vstage_scr, *, tm, tiles_per_batch):
    _project(x_ref, g_ref, (wm_ref, wg_ref, wz_ref), z_scr)
    tab = rope_ref[...]

    lane = _lane_iota((CHUNK, LANES))
    lo = lane < HEAD_DIM
    for c in range(tm // CHUNK):
        r0 = c * CHUNK
        u = z_scr[r0:r0 + CHUNK, C_U:C_U + D_A]
        v = z_scr[r0:r0 + CHUNK, C_V:C_V + D_A]
        za = z_scr[r0:r0 + CHUNK, C_ZA:C_ZA + D_A]
        vn = _layer_norm_rows(v, lng_ref[...], lnb_ref[...])
        parts = []
        for p in range(A_GROUPS // 2):
            vp = vn[:, p * LANES:(p + 1) * LANES]
            v_lo = jnp.where(lo, vp, 0.0).astype(BF16)
            v_hi = jnp.where(lo, 0.0, vp).astype(BF16)
            parts.append(_dot(ws_ref[2 * p], v_lo) + _dot(ws_ref[2 * p + 1], v_hi))
        s = jnp.concatenate(parts, axis=1) + bias_ref[...]
        aout_ref[r0:r0 + CHUNK, :] = (u * s * jax.nn.silu(za)).astype(BF16)

    q = z_scr[:, C_Q:C_Q + D_B]
    qr = jnp.concatenate(
        [_rope(q[:, j * LANES:(j + 1) * LANES], tab) for j in range(4)], axis=1) * (HEAD_DIM ** -0.5 * LOG2_E)
    slabs = _head_slabs(qr)
    kc = z_scr[:, C_KC:C_KC + KV_W]
    vc = z_scr[:, C_VC:C_VC + KV_W]
    for src, stage, dst in ((kc, kstage_scr, kc_ref), (vc, vstage_scr, vc_ref)):
        stage[...] = src
        for r in range(CMP_STRIDE):
            dst[:, r * KV_W:(r + 1) * KV_W] = stage[pl.ds(r, tm // CMP_STRIDE, stride=CMP_STRIDE), :]
    ks = _rope(z_scr[:, C_KS:C_KS + KV_W], tab)
    vs = z_scr[:, C_VS:C_VS + KV_W]
    kw = _rope(z_scr[:, C_KW:C_KW + KV_W], tab)
    vw = z_scr[:, C_VW:C_VW + KV_W]
    gates = jax.nn.sigmoid(z_scr[:, C_G:C_G + LANES])
    pos = (pl.program_id(0) % tiles_per_batch) * tm + _row_iota((tm, LANES))
    onehot = jnp.where((pos >> SEL_SHIFT) == _lane_iota((tm, LANES)), 1.0, 0.0)
    ksa_ref[:, 0:LANES] = ks.astype(BF16)
    ksa_ref[:, LANES:2 * LANES] = onehot.astype(BF16)
    kwb_ref[...] = kw.astype(BF16)
    for j in range(tm // LANES):
        r0, r1 = j * LANES, (j + 1) * LANES
        kct_ref[0, :, r0:r1] = kc[r0:r1].T
        vct_ref[0, :, r0:r1] = vc[r0:r1].T
        kst_ref[0, :, r0:r1] = ks[r0:r1].T
        kwt_ref[0, :, r0:r1] = kw[r0:r1].T
        vs_t, vw_t = vs[r0:r1].T, vw[r0:r1].T
        vst_ref[0, :, r0:r1] = vs_t
        vwt_ref[0, :, r0:r1] = vw_t
        vsb_ref[0, j] = vs_t.astype(BF16)
        vwb_ref[0, j] = vw_t.astype(BF16)
        for e in range(N_HEADS):
            qt_ref[0, j, :, e * Q_BLOCK:(e + 1) * Q_BLOCK] = slabs[e][r0:r1].T.astype(BF16)
        gate_ref[0, j] = gates[r0:r1].T
    szb_ref[...] = jax.nn.silu(z_scr[:, C_ZB:C_ZB + D_B])


def _inproj_prompt(x2, g, w, lng, lnb, ws_bf, bias_full, rope_tab, seq, tm=ROW_TILE):
    n = x2.shape[0]
    tiles_per_batch = seq // tm
    bsz = n // seq
    n_tok_tiles = seq // LANES
    row = lambda i: (i, 0)
    const2 = lambda i: (0, 0)
    rows2d = lambda width, dtype: (jax.ShapeDtypeStruct((n, width), dtype), pl.BlockSpec((tm, width), row))
    kv_t = (jax.ShapeDtypeStruct((bsz, KV_W, seq), F32),
            pl.BlockSpec((1, KV_W, tm), lambda i: (i // tiles_per_batch, 0, i % tiles_per_batch)))
    chunk_rows = (jax.ShapeDtypeStruct((n // CMP_STRIDE, CMP_STRIDE * KV_W), F32),
                  pl.BlockSpec((tm // CMP_STRIDE, CMP_STRIDE * KV_W), row))
    tiles = lambda r, c, dtype: (
        jax.ShapeDtypeStruct((bsz, n_tok_tiles, r, c), dtype),
        pl.BlockSpec((1, tm // LANES, r, c), lambda i: (i // tiles_per_batch, i % tiles_per_batch, 0, 0)))
    outs = [
        rows2d(D_A, BF16),
        tiles(KV_W, N_HEADS * Q_BLOCK, BF16),
        chunk_rows, chunk_rows,
        kv_t, kv_t, kv_t, kv_t, kv_t, kv_t,
        rows2d(2 * KV_W, BF16),
        rows2d(KV_W, BF16),
        tiles(KV_W, LANES, BF16),
        tiles(KV_W, LANES, BF16),
        tiles(LANES, LANES, F32),
        rows2d(D_B, F32),
    ]
    out_shapes = [o[0] for o in outs]
    out_specs = [o[1] for o in outs]
    in_specs = [
        pl.BlockSpec((tm, D_MODEL), row),
        pl.BlockSpec((1, D_MODEL), const2),
        pl.BlockSpec((D_MODEL, C_G), const2),
        pl.BlockSpec((D_MODEL, C_ZB - C_G), const2),
        pl.BlockSpec((D_MODEL, C_END - C_ZB), const2),
        pl.BlockSpec((1, D_A), const2),
        pl.BlockSpec((1, D_A), const2),
        pl.BlockSpec((A_GROUPS, CHUNK, CHUNK), lambda i: (0, 0, 0)),
        pl.BlockSpec((CHUNK, D_A), const2),
        pl.BlockSpec((tm, 2 * LANES), lambda i: (i % tiles_per_batch, 0)),
    ]
    return pl.pallas_call(
        functools.partial(_inproj_prompt_kernel, tm=tm, tiles_per_batch=tiles_per_batch),
        grid=(n // tm,),
        in_specs=in_specs,
        out_specs=out_specs,
        out_shape=out_shapes,
        scratch_shapes=[pltpu.VMEM((tm, C_END), F32), pltpu.VMEM((tm, KV_W), F32), pltpu.VMEM((tm, KV_W), F32)],
        compiler_params=pltpu.CompilerParams(
            dimension_semantics=("arbitrary",), vmem_limit_bytes=VMEM_LIMIT_BYTES),
        name="inproj_prompt",
    )(x2, g, *w, lng, lnb, ws_bf, bias_full, rope_tab)


def _inproj_decode_kernel(x_ref, g_ref, wm_ref, wg_ref, wz_ref, lng_ref, lnb_ref, wsv_ref, bsv_ref, rope_ref,
                          aout_ref, qa_ref, kc_ref, vc_ref, ks_ref, vs_ref, kw_ref, vw_ref,
                          gate_ref, szb_ref, vn_ref, z_scr):
    _project(x_ref, g_ref, (wm_ref, wg_ref, wz_ref), z_scr)
    tab = rope_ref[...]
    u = z_scr[:, C_U:C_U + D_A]
    v = z_scr[:, C_V:C_V + D_A]
    za = z_scr[:, C_ZA:C_ZA + D_A]
    vn = _layer_norm_rows(v, lng_ref[...], lnb_ref[...])
    vn_ref[...] = vn
    s = vn * wsv_ref[...] + bsv_ref[...]
    aout_ref[...] = (u * s * jax.nn.silu(za)).astype(BF16)

    q = z_scr[:, C_Q:C_Q + D_B]
    qr = jnp.concatenate(
        [_rope(q[:, j * LANES:(j + 1) * LANES], tab) for j in range(4)], axis=1) * (HEAD_DIM ** -0.5)
    for e, slab in enumerate(_head_slabs(qr)):
        qa_ref[:, e * LANES:(e + 1) * LANES] = slab.astype(BF16)
    kc_ref[...] = z_scr[:, C_KC:C_KC + KV_W]
    vc_ref[...] = z_scr[:, C_VC:C_VC + KV_W]
    ks_ref[...] = _rope(z_scr[:, C_KS:C_KS + KV_W], tab)
    vs_ref[...] = z_scr[:, C_VS:C_VS + KV_W]
    kw_ref[...] = _rope(z_scr[:, C_KW:C_KW + KV_W], tab)
    vw_ref[...] = z_scr[:, C_VW:C_VW + KV_W]
    gate_ref[...] = jax.nn.sigmoid(z_scr[:, C_G:C_G + LANES])
    szb = jax.nn.silu(z_scr[:, C_ZB:C_ZB + D_B])
    for e, slab in enumerate(_head_slabs(szb)):
        szb_ref[:, e * LANES:(e + 1) * LANES] = slab


def _inproj_decode(x2, g, w, lng, lnb, wsv, bsv, rope_tab):
    n = x2.shape[0]
    out_shapes = [
        jax.ShapeDtypeStruct((n, D_A), BF16),
        jax.ShapeDtypeStruct((n, N_HEADS * LANES), BF16),
        jax.ShapeDtypeStruct((n, KV_W), F32),
        jax.ShapeDtypeStruct((n, KV_W), F32),
        jax.ShapeDtypeStruct((n, KV_W), F32),
        jax.ShapeDtypeStruct((n, KV_W), F32),
        jax.ShapeDtypeStruct((n, KV_W), F32),
        jax.ShapeDtypeStruct((n, KV_W), F32),
        jax.ShapeDtypeStruct((n, LANES), F32),
        jax.ShapeDtypeStruct((n, N_HEADS * LANES), F32),
        jax.ShapeDtypeStruct((n, D_A), F32),
    ]
    return pl.pallas_call(
        _inproj_decode_kernel,
        out_shape=out_shapes,
        scratch_shapes=[pltpu.VMEM((n, C_END), F32)],
        compiler_params=pltpu.CompilerParams(vmem_limit_bytes=VMEM_LIMIT_BYTES),
        name="inproj_decode",
    )(x2, g, *w, lng, lnb, wsv, bsv, rope_tab)


def _compress_chunks(ch, pos_term, w_ref):
    ab = _dot(ch.astype(BF16), w_ref[...])
    return ab[:, 0:KV_W] + pltpu.roll(ab[:, KV_W:2 * KV_W], ch.shape[0] - 1, 0) + pos_term


def _position_term(pe_ref, w_ref):
    pe = pe_ref[...]
    hi = pe.astype(BF16)
    r1 = pe - hi.astype(F32)
    mid = r1.astype(BF16)
    lo = (r1 - mid.astype(F32)).astype(BF16)
    pw = _dot(hi, w_ref[...]) + _dot(mid, w_ref[...]) + _dot(lo, w_ref[...])
    return pw[0:1, 0:KV_W] + pw[1:2, KV_W:2 * KV_W]


def _compress_prompt_kernel(kc_ref, vc_ref, wk_ref, wv_ref, pek_ref, pev_ref, rope_ref, kcc_ref, vcct_ref):
    kcc = _compress_chunks(kc_ref[0], _position_term(pek_ref, wk_ref), wk_ref)
    kcc_ref[0] = _rope(kcc, rope_ref[...]).astype(BF16)
    vcc = _compress_chunks(vc_ref[0], _position_term(pev_ref, wv_ref), wv_ref)
    for j in range(vcc.shape[0] // LANES):
        vcct_ref[0, :, j * LANES:(j + 1) * LANES] = vcc[j * LANES:(j + 1) * LANES].T.astype(BF16)


def _compress_prompt(kc3, vc3, wk, wv, pek, pev, rope_tab):
    b, c, width = kc3.shape
    blk = pl.BlockSpec((1, c, width), lambda i: (i, 0, 0))
    const2 = lambda i: (0, 0)
    return pl.pallas_call(
        _compress_prompt_kernel,
        grid=(b,),
        in_specs=[blk, blk,
                  pl.BlockSpec(wk.shape, const2), pl.BlockSpec(wv.shape, const2),
                  pl.BlockSpec(pek.shape, const2), pl.BlockSpec(pev.shape, const2),
                  pl.BlockSpec(rope_tab.shape, const2)],
        out_specs=[pl.BlockSpec((1, c, KV_W), lambda i: (i, 0, 0)),
                   pl.BlockSpec((1, KV_W, c), lambda i: (i, 0, 0))],
        out_shape=[jax.ShapeDtypeStruct((b, c, KV_W), BF16), jax.ShapeDtypeStruct((b, KV_W, c), BF16)],
        compiler_params=pltpu.CompilerParams(
            dimension_semantics=("arbitrary",), vmem_limit_bytes=VMEM_LIMIT_BYTES),
        name="compress_prompt",
    )(kc3, vc3, wk, wv, pek, pev, rope_tab)


def _select_blocks(imp, kk, candidate, forced):
    assert FORCE_SCORE > 2 * GQA
    start = jnp.where(candidate, jnp.where(forced, -jnp.inf, imp), -jnp.inf)
    score = start
    for _ in range(N_SELECT - N_FORCED):
        cm = jnp.max(score, axis=0, keepdims=True)
        first = jnp.min(jnp.where(score == cm, kk, LANES), axis=0, keepdims=True)
        score = jnp.where(kk == first, -jnp.inf, score)
    taken = (score == -jnp.inf) & (start > -jnp.inf)
    return candidate & (forced | taken)


def _softmax_rows(s):
    mx = jnp.max(s, axis=1, keepdims=True)
    ex = jnp.exp(s - mx)
    return ex * (1.0 / jnp.sum(ex, axis=1, keepdims=True))


def _tile_heads(x):
    return jnp.concatenate([x] * N_HEADS, axis=1)


def _values_by_kv_head(vt, p):
    half = GQA * Q_BLOCK
    return jnp.concatenate(
        [_dot(vt[h * HEAD_DIM:(h + 1) * HEAD_DIM], p[:, h * half:(h + 1) * half]) for h in range(N_KV)], axis=1)


def _attn_prompt_kernel(*refs):
    def body(sub, carry):
        _attn_query_block(sub, *refs)
        return carry

    lax.fori_loop(0, Q_SUB, body, 0)


def _attn_query_block(sub, qt_ref, gate_ref, szb_ref, kcc_ref, vcct_ref, ovlt_ref,
                      ksa_ref, vsb_ref, kwb_ref, vwb_ref, out_ref,
                      qaug_scr, acc_scr, sa_scr, sb_scr):
    i_blk = pl.program_id(1) * Q_SUB + sub
    q0 = i_blk * Q_BLOCK
    cols = N_HEADS * Q_BLOCK
    qt = qt_ref[0, sub]
    out_rows = pl.ds(pl.multiple_of(sub * Q_BLOCK, Q_BLOCK), Q_BLOCK)

    n_cmp = kcc_ref.shape[1]

    def compressed(extent):
        t_c = q0 + _lane_iota((extent, Q_BLOCK))
        seen = (_row_iota((extent, Q_BLOCK)) * CMP_STRIDE + (CMP_BLOCK - 1)) <= t_c
        s = _dot(kcc_ref[0, 0:extent, :], qt) + _tile_heads(jnp.where(seen, 0.0, NEG))
        mx = jnp.max(s, axis=0, keepdims=True)
        ex = jnp.exp2(s - mx)
        inv = jnp.where(mx > 0.5 * NEG, 1.0 / jnp.sum(ex, axis=0, keepdims=True), 0.0)
        p_c = ex * inv
        imps = []
        for h in range(N_KV):
            c0 = h * GQA * Q_BLOCK
            p_sum = (p_c[:, c0:c0 + Q_BLOCK] + p_c[:, c0 + Q_BLOCK:c0 + 2 * Q_BLOCK]
                     + p_c[:, c0 + 2 * Q_BLOCK:c0 + 3 * Q_BLOCK] + p_c[:, c0 + 3 * Q_BLOCK:c0 + 4 * Q_BLOCK])
            imps.append(_dot(ovlt_ref[:, 0:extent], p_sum.astype(BF16)))
        return (_values_by_kv_head(vcct_ref[0, :, 0:extent], p_c.astype(BF16)), *imps)

    n_need = (q0 + Q_BLOCK) // CMP_STRIDE
    extents = list(range(LANES, n_cmp + 1, LANES))
    o_c, *imp_t = lax.switch(
        jnp.minimum((n_need - 1) // LANES, len(extents) - 1),
        [functools.partial(compressed, ext) for ext in extents])

    w0 = pl.multiple_of(jnp.maximum(q0 - WINDOW, 0), Q_BLOCK)
    wj0 = jnp.maximum(i_blk - WINDOW // Q_BLOCK, 0)
    kwin = kwb_ref[0, pl.ds(w0, WIN_SPAN), :]
    vwt = jnp.concatenate([vwb_ref[0, wj0 + i] for i in range(WIN_SPAN // LANES)], axis=1)
    kp = w0 + _row_iota((WIN_SPAN, Q_BLOCK))
    tw = q0 + _lane_iota((WIN_SPAN, Q_BLOCK))
    in_win = jnp.where(kp <= tw, jnp.where(kp > tw - WINDOW, 0.0, NEG), NEG)
    sw = _dot(kwin, qt) + _tile_heads(in_win)
    exw = jnp.exp2(sw - jnp.max(sw, axis=0, keepdims=True))
    o_w = _values_by_kv_head(vwt, exw.astype(BF16)) * (1.0 / jnp.sum(exw, axis=0, keepdims=True))

    kk = _row_iota((LANES, Q_BLOCK))
    tq = q0 + _lane_iota((LANES, Q_BLOCK))
    valid = kk * SEL_BLOCK <= tq
    cur = tq >> SEL_SHIFT
    forced = (kk == 0) | (kk == cur) | (kk == cur - 1)
    both = lambda a: jnp.concatenate([a] * N_KV, axis=1)
    picked = _select_blocks(jnp.concatenate(imp_t, axis=1), both(kk), both(valid), both(forced))
    qaug_scr[0:LANES, :] = qt
    for h in range(N_KV):
        selneg = jnp.where(picked[:, h * Q_BLOCK:(h + 1) * Q_BLOCK], 0.0, NEG).astype(BF16)
        for g in range(GQA):
            c0 = (h * GQA + g) * Q_BLOCK
            qaug_scr[LANES:2 * LANES, c0:c0 + Q_BLOCK] = selneg

    acc_scr[...] = jnp.zeros(acc_scr.shape, F32)
    tiles_per_step = SEL_TILE // LANES

    def score(step, dst):
        t0 = pl.multiple_of(step * SEL_TILE, SEL_TILE)
        dst[...] = _dot(ksa_ref[0, pl.ds(t0, SEL_TILE), :], qaug_scr[...])

    def attend(src, step, m_prev, l_prev, causal_rows=None):
        rows = SEL_TILE if causal_rows is None else causal_rows
        vt = jnp.concatenate([vsb_ref[0, step * tiles_per_step + i] for i in range(rows // LANES)], axis=1)
        sc = src[0:rows, :]
        if causal_rows is not None:
            tok = step * SEL_TILE + _row_iota((rows, Q_BLOCK))
            sc = sc + _tile_heads(jnp.where(tok <= q0 + _lane_iota((rows, Q_BLOCK)), 0.0, NEG))
        m_next = jnp.maximum(m_prev, jnp.max(sc, axis=0, keepdims=True))
        p = jnp.exp2(sc - m_next)
        alpha = jnp.exp2(m_prev - m_next)
        l_next = alpha * l_prev + jnp.sum(p, axis=0, keepdims=True)
        acc_scr[...] = alpha * acc_scr[...] + _values_by_kv_head(vt, p.astype(BF16))
        return m_next, l_next

    n_full = q0 // SEL_TILE
    score(0, sa_scr)

    def pair(jj, carry):
        m_c, l_c = carry
        score(2 * jj + 1, sb_scr)
        m_c, l_c = attend(sa_scr, 2 * jj, m_c, l_c)
        score(2 * jj + 2, sa_scr)
        return attend(sb_scr, 2 * jj + 1, m_c, l_c)

    m_run, l_run = lax.fori_loop(
        0, n_full // 2, pair,
        (jnp.full((1, cols), -jnp.inf, F32), jnp.zeros((1, cols), F32)))

    def last_step(src, m_c, l_c):
        extents = list(range(Q_BLOCK, SEL_TILE + 1, Q_BLOCK))
        return lax.switch(
            (q0 % SEL_TILE) // Q_BLOCK,
            [lambda m, l, ext=ext: attend(src, n_full, m, l, causal_rows=ext)[1] for ext in extents],
            m_c, l_c)

    def odd_tail(m_c, l_c):
        score(n_full, sb_scr)
        m_c, l_c = attend(sa_scr, n_full - 1, m_c, l_c)
        return last_step(sb_scr, m_c, l_c)

    def even_tail(m_c, l_c):
        return last_step(sa_scr, m_c, l_c)

    l_run = lax.cond(n_full % 2 == 1, odd_tail, even_tail, m_run, l_run)

    gt = gate_ref[0, sub]
    o_s = acc_scr[...] * (1.0 / l_run)
    heads = []
    for e in range(N_HEADS):
        c0 = e * Q_BLOCK
        heads.append(gt[3 * e:3 * e + 1, :] * o_c[:, c0:c0 + Q_BLOCK]
                     + gt[3 * e + 1:3 * e + 2, :] * o_s[:, c0:c0 + Q_BLOCK]
                     + gt[3 * e + 2:3 * e + 3, :] * o_w[:, c0:c0 + Q_BLOCK])
    for j in range(N_HEADS // 2):
        slab = jnp.concatenate([heads[2 * j], heads[2 * j + 1]], axis=0).T
        out_ref[0, out_rows, j * LANES:(j + 1) * LANES] = (
            slab * szb_ref[0, out_rows, j * LANES:(j + 1) * LANES]).astype(BF16)


def _attn_prompt(qt4, gate4, szb3, kcc, vcct, ovlt, ksa3, vsb4, kwb3, vwb4):
    b, nq = qt4.shape[0], qt4.shape[1]
    t = nq * Q_BLOCK
    cols = N_HEADS * Q_BLOCK
    assert nq % Q_SUB == 0
    qtile = lambda a: pl.BlockSpec((1, Q_SUB) + a.shape[2:], lambda bi, i: (bi, i, 0, 0))
    rows = pl.BlockSpec((1, Q_SUB * Q_BLOCK, D_B), lambda bi, i: (bi, i, 0))
    seq = lambda a: pl.BlockSpec((1,) + a.shape[1:], lambda bi, i, nd=a.ndim: (bi,) + (0,) * (nd - 1))
    return pl.pallas_call(
        _attn_prompt_kernel,
        grid=(b, nq // Q_SUB),
        in_specs=[qtile(qt4), qtile(gate4), rows,
                  seq(kcc), seq(vcct), pl.BlockSpec(ovlt.shape, lambda bi, i: (0, 0)),
                  seq(ksa3), seq(vsb4), seq(kwb3), seq(vwb4)],
        out_specs=rows,
        out_shape=jax.ShapeDtypeStruct((b, t, D_B), BF16),
        scratch_shapes=[pltpu.VMEM((2 * LANES, cols), BF16),
                        pltpu.VMEM((HEAD_DIM, cols), F32),
                        pltpu.VMEM((SEL_TILE, cols), F32),
                        pltpu.VMEM((SEL_TILE, cols), F32)],
        compiler_params=pltpu.CompilerParams(
            dimension_semantics=("arbitrary", "arbitrary"), vmem_limit_bytes=VMEM_LIMIT_BYTES),
        name="attn_prompt",
    )(qt4, gate4, szb3, kcc, vcct, ovlt, ksa3, vsb4, kwb3, vwb4)


def _outproj_kernel(x_ref, a_ref, m_ref, wa_ref, wb_ref, fg_ref, y_ref):
    y = x_ref[...] + _dot(a_ref[...], wa_ref[...]) + _dot(m_ref[...].astype(BF16), wb_ref[...])
    y_ref[...] = _rms_rows(y, fg_ref[...])


def _outproj(x2, a_out, mix_b, wa, wb, fg, tm):
    n = x2.shape[0]
    row = lambda i: (i, 0)
    const2 = lambda i: (0, 0)
    return pl.pallas_call(
        _outproj_kernel,
        grid=(n // tm,),
        in_specs=[pl.BlockSpec((tm, D_MODEL), row),
                  pl.BlockSpec((tm, a_out.shape[1]), row),
                  pl.BlockSpec((tm, mix_b.shape[1]), row),
                  pl.BlockSpec(wa.shape, const2), pl.BlockSpec(wb.shape, const2),
                  pl.BlockSpec((1, D_MODEL), const2)],
        out_specs=pl.BlockSpec((tm, D_MODEL), row),
        out_shape=jax.ShapeDtypeStruct((n, D_MODEL), F32),
        compiler_params=pltpu.CompilerParams(
            dimension_semantics=("arbitrary",), vmem_limit_bytes=VMEM_LIMIT_BYTES),
        name="outproj",
    )(x2, a_out, mix_b, wa, wb, fg)


def _page_copies(pt_ref, pool_ref, buf_ref, sem_ref, step, slot):
    group, n_pages = buf_ref.shape[1], buf_ref.shape[2]
    return [pltpu.make_async_copy(pool_ref.at[pt_ref[step * group + g, p]], buf_ref.at[slot, g, p],
                                  sem_ref.at[slot])
            for g in range(group) for p in range(n_pages)]


def _gather_pages(pt_ref, pools, bufs, sems):
    b = pl.program_id(0)
    slot = lax.rem(b, 2)

    def start(step, sl):
        for pool, buf, sem in zip(pools, bufs, sems):
            for cp in _page_copies(pt_ref, pool, buf, sem, step, sl):
                cp.start()

    @pl.when(b == 0)
    def _():
        start(0, 0)

    @pl.when(b + 1 < pl.num_programs(0))
    def _():
        start(b + 1, 1 - slot)

    for pool, buf, sem in zip(pools, bufs, sems):
        for cp in _page_copies(pt_ref, pool, buf, sem, b, slot):
            cp.wait()
    return slot


def _decode_cmp_kernel(pt_ref, ck_hbm, cv_hbm, q_ref, wk_ref, wv_ref, pek_ref, pev_ref, rope_ref, ovl_ref,
                       oc_ref, psum_ref, kbuf, vbuf, ksem, vsem, ktok_scr, vtok_scr, kpos_scr, vpos_scr):
    slot = _gather_pages(pt_ref, (ck_hbm, cv_hbm), (kbuf, vbuf), (ksem, vsem))
    group, n_pages = kbuf.shape[1], kbuf.shape[2]

    @pl.when(pl.program_id(0) == 0)
    def _():
        kpos_scr[...] = _position_term(pek_ref, wk_ref)
        vpos_scr[...] = _position_term(pev_ref, wv_ref)

    def chunk_rows(buf, tok_scr, g):
        for p in range(n_pages):
            tok_scr[g, p * PAGE_SIZE:(p + 1) * PAGE_SIZE, :] = buf[slot, g, p].T
        n = tok_scr.shape[1] // CMP_STRIDE
        return jnp.concatenate(
            [tok_scr[g, pl.ds(r, n, stride=CMP_STRIDE), :] for r in range(CMP_STRIDE)], axis=1)

    for g in range(group):
        kch = chunk_rows(kbuf, ktok_scr, g)
        vch = chunk_rows(vbuf, vtok_scr, g)
        kcc = _rope(_compress_chunks(kch, kpos_scr[...], wk_ref), rope_ref[...]).astype(BF16)
        vcc = _compress_chunks(vch, vpos_scr[...], wv_ref).astype(BF16)
        n_chunks = kch.shape[0]
        q = q_ref[g]
        rows = q.shape[0]
        s = _dot_nt(q, kcc)
        seen = _lane_iota((rows, n_chunks)) < n_chunks - 1
        p_c = jnp.where(seen, _softmax_rows(jnp.where(seen, s, NEG)), 0.0)
        oc_ref[g] = _dot(p_c.astype(BF16), vcc)[0:N_HEADS]
        ps0 = jnp.sum(p_c[0:GQA], axis=0, keepdims=True)
        ps1 = jnp.sum(p_c[GQA:2 * GQA], axis=0, keepdims=True)
        psum = jnp.where(_row_iota((rows, n_chunks)) < N_HEADS // 2, ps0, ps1)
        psum_ref[g] = _dot(psum.astype(BF16), ovl_ref[...])[0:N_HEADS]


def _decode_cmp(page_table, ck_pool, cv_pool, q16, wk, wv, pek, pev, rope_tab, ovl):
    nb, n_pages = page_table.shape
    group = DECODE_GROUP
    assert nb % group == 0
    hbm = pl.BlockSpec(memory_space=pl.ANY)
    page_buf = pltpu.VMEM((2, group, n_pages, KV_W, PAGE_SIZE), F32)
    tok_buf = pltpu.VMEM((group, n_pages * PAGE_SIZE, KV_W), F32)
    const2 = lambda b, pt: (0, 0)
    per_b = lambda a: pl.BlockSpec((group,) + a.shape[1:], lambda b, pt: (b, 0, 0))
    in_specs = [hbm, hbm, per_b(q16),
                pl.BlockSpec(wk.shape, const2), pl.BlockSpec(wv.shape, const2),
                pl.BlockSpec(pek.shape, const2), pl.BlockSpec(pev.shape, const2),
                pl.BlockSpec(rope_tab.shape, const2), pl.BlockSpec(ovl.shape, const2)]
    out_blk = pl.BlockSpec((group, N_HEADS, LANES), lambda b, pt: (b, 0, 0))
    grid_spec = pltpu.PrefetchScalarGridSpec(
        num_scalar_prefetch=1, grid=(nb // group,), in_specs=in_specs, out_specs=[out_blk, out_blk],
        scratch_shapes=[page_buf, page_buf, pltpu.SemaphoreType.DMA((2,)), pltpu.SemaphoreType.DMA((2,)),
                        tok_buf, tok_buf, pltpu.VMEM((1, KV_W), F32), pltpu.VMEM((1, KV_W), F32)])
    return pl.pallas_call(
        _decode_cmp_kernel,
        grid_spec=grid_spec,
        out_shape=[jax.ShapeDtypeStruct((nb, N_HEADS, LANES), F32)] * 2,
        compiler_params=pltpu.CompilerParams(
            dimension_semantics=("arbitrary",), vmem_limit_bytes=VMEM_LIMIT_BYTES),
        name="decode_cmp",
    )(page_table, ck_pool, cv_pool, q16, wk, wv, pek, pev, rope_tab, ovl)


def _decode_topk_kernel(imp_ref, sel_ref, *, n_blk, cur):
    n = imp_ref.shape[0]
    tiles = [imp_ref[i * LANES:(i + 1) * LANES, :].T for i in range(n // LANES)]
    imp_t = jnp.concatenate(tiles, axis=1)
    kk = _row_iota(imp_t.shape)
    forced = (kk == 0) | (kk == cur) | (kk == cur - 1)
    selneg = jnp.where(_select_blocks(imp_t, kk, kk < n_blk, forced), 0.0, NEG)
    for i in range(n // LANES):
        sel_ref[i * LANES:(i + 1) * LANES, :] = selneg[:, i * LANES:(i + 1) * LANES].T


def _decode_topk(imp2, n_blk, cur):
    return pl.pallas_call(
        functools.partial(_decode_topk_kernel, n_blk=n_blk, cur=cur),
        out_shape=jax.ShapeDtypeStruct(imp2.shape, F32),
        compiler_params=pltpu.CompilerParams(vmem_limit_bytes=VMEM_LIMIT_BYTES),
        name="decode_topk",
    )(imp2)


def _decode_attend(s, q32, vals_t_bf, k_new, v_new, bias_new):
    kn = k_new.astype(BF16).astype(F32)
    s_new = jnp.sum(q32 * kn, axis=1, keepdims=True) + bias_new
    mx = jnp.maximum(jnp.max(s, axis=1, keepdims=True), s_new)
    ex = jnp.exp(s - mx)
    ex_new = jnp.exp(s_new - mx)
    inv = 1.0 / (jnp.sum(ex, axis=1, keepdims=True) + ex_new)
    p = (ex * inv).astype(BF16)
    p_new = (ex_new * inv).astype(BF16).astype(F32)
    return _dot_nt(p, vals_t_bf) + p_new * v_new.astype(BF16).astype(F32)


P_Q, P_SEL, P_OC, P_SZB, P_NEW, P_GATE, P_ROWS = 0, 16, 32, 40, 48, 56, 64


def _row_to_col(row):
    n = row.shape[1]
    diag = _row_iota((n, n)) == _lane_iota((n, n))
    return jnp.sum(jnp.where(diag, row, 0.0), axis=1, keepdims=True)


def _decode_sel_kernel(pt_ref, ks_hbm, vs_hbm, ckw_ref, cvw_ref, pack_ref, onehot_ref,
                       mix_ref, okw_ref, ovw_ref, kbuf, vbuf, ksem, vsem, *, cur):
    slot = _gather_pages(pt_ref, (ks_hbm, vs_hbm), (kbuf, vbuf), (ksem, vsem))
    group, n_pages = kbuf.shape[1], kbuf.shape[2]
    for g in range(group):
        pk = pack_ref[g]
        q32 = pk[P_Q:P_Q + 2 * N_HEADS]
        q = q32.astype(BF16)
        sel = pk[P_SEL:P_SEL + 2 * N_HEADS].astype(BF16)
        rows = q.shape[0]
        ks_new, vs_new = pk[P_NEW:P_NEW + 1], pk[P_NEW + 1:P_NEW + 2]
        kw_new, vw_new = pk[P_NEW + 2:P_NEW + 3], pk[P_NEW + 3:P_NEW + 4]

        kt = jnp.concatenate([kbuf[slot, g, p] for p in range(n_pages)], axis=1).astype(BF16)
        vt = jnp.concatenate([vbuf[slot, g, p] for p in range(n_pages)], axis=1).astype(BF16)
        kaug = jnp.concatenate([kt, onehot_ref[...]], axis=0)
        qaug = jnp.concatenate([q, sel], axis=1)
        s_sel = _dot(qaug, kaug)
        o_s = _decode_attend(s_sel, q32, vt, ks_new, vs_new, sel.astype(F32)[:, cur:cur + 1])

        kwin = ckw_ref[g]
        vwin = cvw_ref[g]
        keep = kwin.shape[1]
        visible = _lane_iota((rows, keep)) > keep - WINDOW
        s_win = jnp.where(visible, _dot(q, kwin.astype(BF16)), NEG)
        o_w = _decode_attend(s_win, q32, vwin.astype(BF16), kw_new, vw_new, 0.0)

        gates = pk[P_GATE:P_GATE + N_HEADS]
        o = (gates[:, 0:1] * pk[P_OC:P_OC + N_HEADS] + gates[:, 1:2] * o_s[0:N_HEADS]
             + gates[:, 2:3] * o_w[0:N_HEADS])
        mix_ref[g] = o * pk[P_SZB:P_SZB + N_HEADS]

        last = _lane_iota((KV_W, keep)) == keep - 1
        okw_ref[g] = jnp.where(last, _row_to_col(kw_new), pltpu.roll(kwin, keep - 1, 1))
        ovw_ref[g] = jnp.where(last, _row_to_col(vw_new), pltpu.roll(vwin, keep - 1, 1))


def _decode_sel(page_table, ks_pool, vs_pool, ckw, cvw, pack, onehot, cur):
    nb, n_pages = page_table.shape
    group = DECODE_GROUP
    assert nb % group == 0
    keep = ckw.shape[2]
    hbm = pl.BlockSpec(memory_space=pl.ANY)
    page_buf = pltpu.VMEM((2, group, n_pages, KV_W, PAGE_SIZE), F32)
    per_b = lambda a: pl.BlockSpec((group,) + a.shape[1:], lambda b, pt: (b, 0, 0))
    in_specs = [hbm, hbm, per_b(ckw), per_b(cvw), per_b(pack),
                pl.BlockSpec(onehot.shape, lambda b, pt: (0, 0))]
    out_specs = [pl.BlockSpec((group, N_HEADS, LANES), lambda b, pt: (b, 0, 0)),
                 pl.BlockSpec((group, KV_W, keep), lambda b, pt: (b, 0, 0)),
                 pl.BlockSpec((group, KV_W, keep), lambda b, pt: (b, 0, 0))]
    grid_spec = pltpu.PrefetchScalarGridSpec(
        num_scalar_prefetch=1, grid=(nb // group,), in_specs=in_specs, out_specs=out_specs,
        scratch_shapes=[page_buf, page_buf, pltpu.SemaphoreType.DMA((2,)), pltpu.SemaphoreType.DMA((2,))])
    return pl.pallas_call(
        functools.partial(_decode_sel_kernel, cur=cur),
        grid_spec=grid_spec,
        out_shape=[jax.ShapeDtypeStruct((nb, N_HEADS, LANES), F32),
                   jax.ShapeDtypeStruct((nb, KV_W, keep), F32),
                   jax.ShapeDtypeStruct((nb, KV_W, keep), F32)],
        compiler_params=pltpu.CompilerParams(
            dimension_semantics=("arbitrary",), vmem_limit_bytes=VMEM_LIMIT_BYTES),
        name="decode_sel",
    )(page_table, ks_pool, vs_pool, ckw, cvw, pack, onehot)


def _rope_table(pos):
    half = ROT_DIM // 2
    inv = jnp.power(jnp.float32(ROPE_THETA), -jnp.arange(half, dtype=F32) / half)
    ang = pos.astype(F32)[:, None] * inv[None, :]
    cs = jnp.concatenate([jnp.cos(ang), jnp.sin(ang)], axis=1)
    top16 = lambda a: lax.bitcast_convert_type(
        lax.bitcast_convert_type(a, jnp.uint32) & jnp.uint32(0xFFFF0000), F32)
    hi = top16(cs)
    mid = top16(cs - hi)
    pieces = jnp.concatenate([hi, mid, cs - hi - mid], axis=1)
    lane = jnp.arange(2 * LANES)
    seg, l64 = lane // LANES, lane % HEAD_DIM
    src = jnp.where(seg == 0, l64 % half, half + l64 % half)
    sign = jnp.where(seg == 0, l64 < ROT_DIM, l64 < half).astype(F32)
    sign = jnp.where(seg == 1, -sign, sign)
    place = (jnp.arange(ROT_DIM)[:, None] == src[None, :]).astype(F32) * sign[None, :]
    ones = ((seg == 0) & (l64 >= ROT_DIM)).astype(F32)
    return jnp.dot(pieces, jnp.concatenate([place] * 3, axis=0)) + ones[None, :]


def _compress_weights(w_c, pe_c):
    w4 = w_c.reshape(2, CMP_STRIDE, HEAD_DIM, HEAD_DIM).astype(BF16)
    zero = jnp.zeros_like(w4[0])
    wb = jnp.stack([jnp.concatenate([w4[a] if g == h else zero for a in range(2) for g in range(N_KV)], axis=-1)
                    for h in range(N_KV)], axis=1)
    wb = wb.reshape(CMP_STRIDE * KV_W, 2 * KV_W)
    pe = jnp.broadcast_to(pe_c.reshape(2, CMP_STRIDE, 1, HEAD_DIM), (2, CMP_STRIDE, N_KV, HEAD_DIM))
    return wb, jnp.pad(pe.reshape(2, CMP_STRIDE * KV_W), ((0, 14), (0, 0)))


def _overlap_matrix(n_rows, n_blk):
    cs = jnp.arange(n_rows, dtype=jnp.int32)[:, None] * CMP_STRIDE
    bs = jnp.arange(LANES, dtype=jnp.int32)[None, :] * SEL_BLOCK
    hit = (cs < bs + SEL_BLOCK) & (cs + CMP_BLOCK > bs) & (jnp.arange(LANES)[None, :] < n_blk)
    return hit.astype(BF16)


def kernel(x_prompt, x_sample, cache_k_cmp, cache_v_cmp, cache_k_sel, cache_v_sel, cache_k_win,
           cache_v_win, page_table, norm_g, w_in, ln_v_g, ln_v_b, w_s, b_s, w_ck, pe_ck, w_cv,
           pe_cv, w_out, final_g):
    depth = norm_g.shape[0]
    assert depth == 1
    bsz, seq, _ = x_prompt.shape
    nb, dec_seq, _ = x_sample.shape
    assert dec_seq == 1 and seq % SEL_TILE == 0 and seq >= WIN_SPAN and (seq // CMP_STRIDE) % LANES == 0
    n_pages = page_table.shape[1]
    past = n_pages * PAGE_SIZE
    keep = cache_k_win.shape[2]
    assert keep == WINDOW and past % SEL_BLOCK == 0

    w = w_in[0]
    w_pad = (w[:, :C_G].astype(BF16),
             jnp.pad(w[:, C_G:C_G + N_GATE_COLS], ((0, 0), (0, LANES - N_GATE_COLS))).astype(BF16),
             w[:, C_G + N_GATE_COLS:].astype(BF16))
    g_row = norm_g[0][None, :]
    lng = ln_v_g[0][None, :]
    lnb = ln_v_b[0][None, :]
    fg = final_g[None, :]
    wk_c, pek = _compress_weights(w_ck[0], pe_ck[0])
    wv_c, pev = _compress_weights(w_cv[0], pe_cv[0])
    w_o = w_out[0].astype(BF16)
    wo_a, wo_b = w_o[:D_A], w_o[D_A:]

    tril = jnp.tril(jnp.ones((CHUNK, CHUNK), w_s.dtype))
    ws_bf = (w_s[0] * tril).astype(BF16)
    bias_full = jnp.repeat(b_s[0].T, HEAD_DIM, axis=1)
    xp2 = x_prompt.reshape(bsz * seq, D_MODEL)
    (a_out, qt, kc, vc, kc_t, vc_t, ks_t, vs_t, kw_t, vw_t, ksa, kwb, vsb, vwb, gates_t, szb) = _inproj_prompt(
        xp2, g_row, w_pad, lng, lnb, ws_bf, bias_full,
        _rope_table(jnp.arange(seq, dtype=jnp.int32)), seq)
    n_chunks = seq // CMP_STRIDE
    kcc, vcc_t = _compress_prompt(
        kc.reshape(bsz, n_chunks, CMP_STRIDE * KV_W), vc.reshape(bsz, n_chunks, CMP_STRIDE * KV_W),
        wk_c, wv_c, pek, pev,
        _rope_table(jnp.arange(n_chunks, dtype=jnp.int32) * CMP_STRIDE))
    b3 = lambda a: a.reshape(bsz, seq, a.shape[-1])
    mix_p = _attn_prompt(qt, gates_t, b3(szb), kcc, vcc_t,
                         _overlap_matrix(n_chunks, seq // SEL_BLOCK).T,
                         b3(ksa), vsb, b3(kwb), vwb)
    y_prompt = _outproj(xp2, a_out, mix_p.reshape(bsz * seq, D_B), wo_a, wo_b, fg, OUT_ROW_TILE)
    y_prompt = y_prompt.reshape(bsz, seq, D_MODEL)
    st = lambda a: jnp.transpose(a.reshape(a.shape[0], N_KV, HEAD_DIM, a.shape[2]), (0, 3, 1, 2))[None]
    keep_p = min(WINDOW, seq)

    xs2 = x_sample.reshape(nb, D_MODEL)
    wsv = jnp.repeat(w_s[0][:, 0, 0], HEAD_DIM)[None, :]
    bsv = jnp.repeat(b_s[0][:, 0], HEAD_DIM)[None, :]
    (a_s, qa_s, kc_s, vc_s, ks_s, vs_s, kw_s, vw_s, gates_s, szb_s, vn_s) = _inproj_decode(
        xs2, g_row, w_pad, lng, lnb, wsv, bsv,
        _rope_table(jnp.full((nb,), past, dtype=jnp.int32)))
    q16 = jnp.pad(qa_s.reshape(nb, N_HEADS, LANES), ((0, 0), (0, N_HEADS), (0, 0)))
    d_chunks = past // CMP_STRIDE
    n_blk = -(-(past + 1) // SEL_BLOCK)
    cur = past // SEL_BLOCK
    kv_t = lambda c: jnp.transpose(c[0], (0, 2, 3, 1)).reshape(c.shape[1], KV_W, c.shape[2])
    oc_s, imp_s = _decode_cmp(
        page_table, kv_t(cache_k_cmp), kv_t(cache_v_cmp),
        q16, wk_c, wv_c, pek, pev,
        _rope_table(jnp.arange(d_chunks, dtype=jnp.int32) * CMP_STRIDE),
        _overlap_matrix(d_chunks, n_blk))
    selneg = _decode_topk(imp_s[:, ::GQA, :].reshape(nb * N_KV, LANES), n_blk, cur)
    sel16 = jnp.pad(jnp.repeat(selneg.reshape(nb, N_KV, LANES), GQA, axis=1), ((0, 0), (0, N_HEADS), (0, 0)))
    tok_blk = jnp.arange(past, dtype=jnp.int32)[None, :] // SEL_BLOCK
    onehot = (tok_blk == jnp.arange(LANES, dtype=jnp.int32)[:, None]).astype(BF16)
    gate_rows = jnp.pad(gates_s[:, :N_GATE_COLS].reshape(nb, N_HEADS, 3), ((0, 0), (0, 0), (0, LANES - 3)))
    new_rows = jnp.pad(jnp.stack([ks_s, vs_s, kw_s, vw_s], axis=1), ((0, 0), (0, P_GATE - P_NEW - 4), (0, 0)))
    pack = jnp.concatenate(
        [q16.astype(F32), sel16, oc_s, szb_s.reshape(nb, N_HEADS, LANES), new_rows, gate_rows], axis=1)
    assert pack.shape[1] == P_ROWS
    mix_s, okw, ovw = _decode_sel(
        page_table, kv_t(cache_k_sel), kv_t(cache_v_sel), kv_t(cache_k_win), kv_t(cache_v_win),
        pack, onehot, cur)
    wo_slab = jnp.zeros((N_HEADS, N_KV, HEAD_DIM, D_MODEL), BF16)
    wo_heads = wo_b.reshape(N_HEADS, HEAD_DIM, D_MODEL)
    for e in range(N_HEADS):
        wo_slab = wo_slab.at[e, e // GQA].set(wo_heads[e])
    y_sample = _outproj(xs2, a_s, mix_s.reshape(nb, N_HEADS * LANES), wo_a,
                        wo_slab.reshape(N_HEADS * LANES, D_MODEL), fg, nb)
    y_sample = y_sample.reshape(nb, 1, D_MODEL)
    ss = lambda a: a.reshape(1, nb, 1, N_KV, HEAD_DIM)

    return (y_prompt, y_sample,
            st(kc_t), st(vc_t), st(ks_t), st(vs_t),
            st(kw_t[:, :, seq - keep_p:]), st(vw_t[:, :, seq - keep_p:]),
            ss(kc_s), ss(vc_s), ss(ks_s), ss(vs_s),
            st(okw), st(ovw),
            vn_s.reshape(1, nb, 1, D_A))
```

```python
import functools

import jax
import jax.numpy as jnp
from jax import lax
from jax.experimental import pallas as pl
from jax.experimental.pallas import tpu as pltpu

F32 = jnp.float32
BF16 = jnp.bfloat16

D_MODEL = 1024
HEAD_DIM = 64
D_A = 512
D_B = 512
A_GROUPS = 8
CHUNK = 128
N_HEADS = 8
N_KV = 2
GQA = 4
KV_W = 128
ROT_DIM = 16
ROPE_THETA = 500000.0
CMP_BLOCK = 32
CMP_STRIDE = 16
SEL_BLOCK = 64
SEL_SHIFT = 6
N_SELECT = 16
WINDOW = 512
Q_BLOCK = 128
PAGE_SIZE = 128
NORM_EPS = 1e-6
FORCE_SCORE = 1e4
N_FORCED = 3
NEG = -1e30
LOG2_E = 1.4426950408889634

LANES = 128
VMEM_LIMIT_BYTES = 56 * 1024 * 1024

C_U, C_V, C_ZA, C_Q = 0, 512, 1024, 1536
C_KC, C_VC, C_KS, C_VS, C_KW, C_VW = 2048, 2176, 2304, 2432, 2560, 2688
C_G, C_ZB, C_END = 2816, 2944, 3456
N_GATE_COLS = 3 * N_HEADS

ROW_TILE = 512
OUT_ROW_TILE = 1024
Q_SUB = 4
SEL_TILE = 512
DECODE_GROUP = 4
WIN_SPAN = WINDOW + Q_BLOCK

_NT = (((1,), (1,)), ((), ()))


def _dot(a, b):
    return jnp.dot(a, b, preferred_element_type=F32)


def _dot_nt(a, b):
    return lax.dot_general(a, b, _NT, preferred_element_type=F32)


def _lane_iota(shape):
    return lax.broadcasted_iota(jnp.int32, shape, len(shape) - 1)


def _row_iota(shape):
    return lax.broadcasted_iota(jnp.int32, shape, len(shape) - 2)


def _rope(x, tab):
    c = tab[:, 0:LANES]
    s1 = tab[:, LANES:2 * LANES]
    return x * c + pltpu.roll(x, LANES - ROT_DIM // 2, 1) * s1 - pltpu.roll(x * s1, ROT_DIM // 2, 1)


def _rms_rows(x, g):
    ms = jnp.mean(x * x, axis=-1, keepdims=True)
    return x * lax.rsqrt(ms + NORM_EPS) * g


def _layer_norm_rows(v, g, b):
    mu = jnp.mean(v, axis=-1, keepdims=True)
    vc = v - mu
    var = jnp.mean(vc * vc, axis=-1, keepdims=True)
    return vc * lax.rsqrt(var + NORM_EPS) * g + b


def _head_slabs(x512):
    out = []
    rows = x512.shape[0]
    lane = _lane_iota((rows, LANES))
    lo = lane < HEAD_DIM
    for j in range(4):
        slab = x512[:, j * LANES:(j + 1) * LANES]
        swapped = pltpu.roll(slab, HEAD_DIM, 1)
        if j < 2:
            out.append(jnp.where(lo, slab, 0.0))
            out.append(jnp.where(lo, swapped, 0.0))
        else:
            out.append(jnp.where(lo, 0.0, swapped))
            out.append(jnp.where(lo, 0.0, slab))
    return out


def _project(x_ref, g_ref, w_refs, z_scr):
    hb = _rms_rows(x_ref[...], g_ref[...]).astype(BF16)
    wm_ref, wg_ref, wz_ref = w_refs
    z_scr[:, 0:C_G] = _dot(hb, wm_ref[...])
    z_scr[:, C_G:C_ZB] = _dot(hb, wg_ref[...])
    z_scr[:, C_ZB:C_END] = _dot(hb, wz_ref[...])


def _inproj_prompt_kernel(x_ref, g_ref, wm_ref, wg_ref, wz_ref, lng_ref, lnb_ref, ws_ref, bias_ref, rope_ref,
                          aout_ref, qt_ref, kc_ref, vc_ref,
                          kct_ref, vct_ref, kst_ref, vst_ref, kwt_ref, vwt_ref,
                          ksa_ref, kwb_ref, vsb_ref, vwb_ref, gate_ref, szb_ref,
                          z_scr, kstage_scr, vstage_scr, *, tm, tiles_per_batch):
    _project(x_ref, g_ref, (wm_ref, wg_ref, wz_ref), z_scr)
    tab = rope_ref[...]

    lane = _lane_iota((CHUNK, LANES))
    lo = lane < HEAD_DIM
    for c in range(tm // CHUNK):
        r0 = c * CHUNK
        u = z_scr[r0:r0 + CHUNK, C_U:C_U + D_A]
        v = z_scr[r0:r0 + CHUNK, C_V:C_V + D_A]
        za = z_scr[r0:r0 + CHUNK, C_ZA:C_ZA + D_A]
        vn = _layer_norm_rows(v, lng_ref[...], lnb_ref[...])
        parts = []
        for p in range(A_GROUPS // 2):
            vp = vn[:, p * LANES:(p + 1) * LANES]
            v_lo = jnp.where(lo, vp, 0.0).astype(BF16)
            v_hi = jnp.where(lo, 0.0, vp).astype(BF16)
            parts.append(_dot(ws_ref[2 * p], v_lo) + _dot(ws_ref[2 * p + 1], v_hi))
        s = jnp.concatenate(parts, axis=1) + bias_ref[...]
        aout_ref[r0:r0 + CHUNK, :] = (u * s * jax.nn.silu(za)).astype(BF16)

    q = z_scr[:, C_Q:C_Q + D_B]
    qr = jnp.concatenate(
        [_rope(q[:, j * LANES:(j + 1) * LANES], tab) for j in range(4)], axis=1) * (HEAD_DIM ** -0.5 * LOG2_E)
    slabs = _head_slabs(qr)
    kc = z_scr[:, C_KC:C_KC + KV_W]
    vc = z_scr[:, C_VC:C_VC + KV_W]
    for src, stage, dst in ((kc, kstage_scr, kc_ref), (vc, vstage_scr, vc_ref)):
        stage[...] = src
        for r in range(CMP_STRIDE):
            dst[:, r * KV_W:(r + 1) * KV_W] = stage[pl.ds(r, tm // CMP_STRIDE, stride=CMP_STRIDE), :]
    ks = _rope(z_scr[:, C_KS:C_KS + KV_W], tab)
    vs = z_scr[:, C_VS:C_VS + KV_W]
    kw = _rope(z_scr[:, C_KW:C_KW + KV_W], tab)
    vw = z_scr[:, C_VW:C_VW + KV_W]
    gates = jax.nn.sigmoid(z_scr[:, C_G:C_G + LANES])
    pos = (pl.program_id(0) % tiles_per_batch) * tm + _row_iota((tm, LANES))
    onehot = jnp.where((pos >> SEL_SHIFT) == _lane_iota((tm, LANES)), 1.0, 0.0)
    ksa_ref[:, 0:LANES] = ks.astype(BF16)
    ksa_ref[:, LANES:2 * LANES] = onehot.astype(BF16)
    kwb_ref[...] = kw.astype(BF16)
    for j in range(tm // LANES):
        r0, r1 = j * LANES, (j + 1) * LANES
        kct_ref[0, :, r0:r1] = kc[r0:r1].T
        vct_ref[0, :, r0:r1] = vc[r0:r1].T
        kst_ref[0, :, r0:r1] = ks[r0:r1].T
        kwt_ref[0, :, r0:r1] = kw[r0:r1].T
        vs_t, vw_t = vs[r0:r1].T, vw[r0:r1].T
        vst_ref[0, :, r0:r1] = vs_t
        vwt_ref[0, :, r0:r1] = vw_t
        vsb_ref[0, j] = vs_t.astype(BF16)
        vwb_ref[0, j] = vw_t.astype(BF16)
        for e in range(N_HEADS):
            qt_ref[0, j, :, e * Q_BLOCK:(e + 1) * Q_BLOCK] = slabs[e][r0:r1].T.astype(BF16)
        gate_ref[0, j] = gates[r0:r1].T
    szb_ref[...] = jax.nn.silu(z_scr[:, C_ZB:C_ZB + D_B])


def _inproj_prompt(x2, g, w, lng, lnb, ws_bf, bias_full, rope_tab, seq, tm=ROW_TILE):
    n = x2.shape[0]
    tiles_per_batch = seq // tm
    bsz = n // seq
    n_tok_tiles = seq // LANES
    row = lambda i: (i, 0)
    const2 = lambda i: (0, 0)
    rows2d = lambda width, dtype: (jax.ShapeDtypeStruct((n, width), dtype), pl.BlockSpec((tm, width), row))
    kv_t = (jax.ShapeDtypeStruct((bsz, KV_W, seq), F32),
            pl.BlockSpec((1, KV_W, tm), lambda i: (i // tiles_per_batch, 0, i % tiles_per_batch)))
    chunk_rows = (jax.ShapeDtypeStruct((n // CMP_STRIDE, CMP_STRIDE * KV_W), F32),
                  pl.BlockSpec((tm // CMP_STRIDE, CMP_STRIDE * KV_W), row))
    tiles = lambda r, c, dtype: (
        jax.ShapeDtypeStruct((bsz, n_tok_tiles, r, c), dtype),
        pl.BlockSpec((1, tm // LANES, r, c), lambda i: (i // tiles_per_batch, i % tiles_per_batch, 0, 0)))
    outs = [
        rows2d(D_A, BF16),
        tiles(KV_W, N_HEADS * Q_BLOCK, BF16),
        chunk_rows, chunk_rows,
        kv_t, kv_t, kv_t, kv_t, kv_t, kv_t,
        rows2d(2 * KV_W, BF16),
        rows2d(KV_W, BF16),
        tiles(KV_W, LANES, BF16),
        tiles(KV_W, LANES, BF16),
        tiles(LANES, LANES, F32),
        rows2d(D_B, F32),
    ]
    out_shapes = [o[0] for o in outs]
    out_specs = [o[1] for o in outs]
    in_specs = [
        pl.BlockSpec((tm, D_MODEL), row),
        pl.BlockSpec((1, D_MODEL), const2),
        pl.BlockSpec((D_MODEL, C_G), const2),
        pl.BlockSpec((D_MODEL, C_ZB - C_G), const2),
        pl.BlockSpec((D_MODEL, C_END - C_ZB), const2),
        pl.BlockSpec((1, D_A), const2),
        pl.BlockSpec((1, D_A), const2),
        pl.BlockSpec((A_GROUPS, CHUNK, CHUNK), lambda i: (0, 0, 0)),
        pl.BlockSpec((CHUNK, D_A), const2),
        pl.BlockSpec((tm, 2 * LANES), lambda i: (i % tiles_per_batch, 0)),
    ]
    return pl.pallas_call(
        functools.partial(_inproj_prompt_kernel, tm=tm, tiles_per_batch=tiles_per_batch),
        grid=(n // tm,),
        in_specs=in_specs,
        out_specs=out_specs,
        out_shape=out_shapes,
        scratch_shapes=[pltpu.VMEM((tm, C_END), F32), pltpu.VMEM((tm, KV_W), F32), pltpu.VMEM((tm, KV_W), F32)],
        compiler_params=pltpu.CompilerParams(
            dimension_semantics=("arbitrary",), vmem_limit_bytes=VMEM_LIMIT_BYTES),
        name="inproj_prompt",
    )(x2, g, *w, lng, lnb, ws_bf, bias_full, rope_tab)


def _inproj_decode_kernel(x_ref, g_ref, wm_ref, wg_ref, wz_ref, lng_ref, lnb_ref, wsv_ref, bsv_ref, rope_ref,
                          aout_ref, qa_ref, kc_ref, vc_ref, ks_ref, vs_ref, kw_ref, vw_ref,
                          gate_ref, szb_ref, vn_ref, z_scr):
    _project(x_ref, g_ref, (wm_ref, wg_ref, wz_ref), z_scr)
    tab = rope_ref[...]
    u = z_scr[:, C_U:C_U + D_A]
    v = z_scr[:, C_V:C_V + D_A]
    za = z_scr[:, C_ZA:C_ZA + D_A]
    vn = _layer_norm_rows(v, lng_ref[...], lnb_ref[...])
    vn_ref[...] = vn
    s = vn * wsv_ref[...] + bsv_ref[...]
    aout_ref[...] = (u * s * jax.nn.silu(za)).astype(BF16)

    q = z_scr[:, C_Q:C_Q + D_B]
    qr = jnp.concatenate(
        [_rope(q[:, j * LANES:(j + 1) * LANES], tab) for j in range(4)], axis=1) * (HEAD_DIM ** -0.5)
    for e, slab in enumerate(_head_slabs(qr)):
        qa_ref[:, e * LANES:(e + 1) * LANES] = slab.astype(BF16)
    kc_ref[...] = z_scr[:, C_KC:C_KC + KV_W]
    vc_ref[...] = z_scr[:, C_VC:C_VC + KV_W]
    ks_ref[...] = _rope(z_scr[:, C_KS:C_KS + KV_W], tab)
    vs_ref[...] = z_scr[:, C_VS:C_VS + KV_W]
    kw_ref[...] = _rope(z_scr[:, C_KW:C_KW + KV_W], tab)
    vw_ref[...] = z_scr[:, C_VW:C_VW + KV_W]
    gate_ref[...] = jax.nn.sigmoid(z_scr[:, C_G:C_G + LANES])
    szb = jax.nn.silu(z_scr[:, C_ZB:C_ZB + D_B])
    for e, slab in enumerate(_head_slabs(szb)):
        szb_ref[:, e * LANES:(e + 1) * LANES] = slab


def _inproj_decode(x2, g, w, lng, lnb, wsv, bsv, rope_tab):
    n = x2.shape[0]
    out_shapes = [
        jax.ShapeDtypeStruct((n, D_A), BF16),
        jax.ShapeDtypeStruct((n, N_HEADS * LANES), BF16),
        jax.ShapeDtypeStruct((n, KV_W), F32),
        jax.ShapeDtypeStruct((n, KV_W), F32),
        jax.ShapeDtypeStruct((n, KV_W), F32),
        jax.ShapeDtypeStruct((n, KV_W), F32),
        jax.ShapeDtypeStruct((n, KV_W), F32),
        jax.ShapeDtypeStruct((n, KV_W), F32),
        jax.ShapeDtypeStruct((n, LANES), F32),
        jax.ShapeDtypeStruct((n, N_HEADS * LANES), F32),
        jax.ShapeDtypeStruct((n, D_A), F32),
    ]
    return pl.pallas_call(
        _inproj_decode_kernel,
        out_shape=out_shapes,
        scratch_shapes=[pltpu.VMEM((n, C_END), F32)],
        compiler_params=pltpu.CompilerParams(vmem_limit_bytes=VMEM_LIMIT_BYTES),
        name="inproj_decode",
    )(x2, g, *w, lng, lnb, wsv, bsv, rope_tab)


def _compress_chunks(ch, pos_term, w_ref):
    ab = _dot(ch.astype(BF16), w_ref[...])
    return ab[:, 0:KV_W] + pltpu.roll(ab[:, KV_W:2 * KV_W], ch.shape[0] - 1, 0) + pos_term


def _position_term(pe_ref, w_ref):
    pe = pe_ref[...]
    hi = pe.astype(BF16)
    r1 = pe - hi.astype(F32)
    mid = r1.astype(BF16)
    lo = (r1 - mid.astype(F32)).astype(BF16)
    pw = _dot(hi, w_ref[...]) + _dot(mid, w_ref[...]) + _dot(lo, w_ref[...])
    return pw[0:1, 0:KV_W] + pw[1:2, KV_W:2 * KV_W]


def _compress_prompt_kernel(kc_ref, vc_ref, wk_ref, wv_ref, pek_ref, pev_ref, rope_ref, kcc_ref, vcct_ref):
    kcc = _compress_chunks(kc_ref[0], _position_term(pek_ref, wk_ref), wk_ref)
    kcc_ref[0] = _rope(kcc, rope_ref[...]).astype(BF16)
    vcc = _compress_chunks(vc_ref[0], _position_term(pev_ref, wv_ref), wv_ref)
    for j in range(vcc.shape[0] // LANES):
        vcct_ref[0, :, j * LANES:(j + 1) * LANES] = vcc[j * LANES:(j + 1) * LANES].T.astype(BF16)


def _compress_prompt(kc3, vc3, wk, wv, pek, pev, rope_tab):
    b, c, width = kc3.shape
    blk = pl.BlockSpec((1, c, width), lambda i: (i, 0, 0))
    const2 = lambda i: (0, 0)
    return pl.pallas_call(
        _compress_prompt_kernel,
        grid=(b,),
        in_specs=[blk, blk,
                  pl.BlockSpec(wk.shape, const2), pl.BlockSpec(wv.shape, const2),
                  pl.BlockSpec(pek.shape, const2), pl.BlockSpec(pev.shape, const2),
                  pl.BlockSpec(rope_tab.shape, const2)],
        out_specs=[pl.BlockSpec((1, c, KV_W), lambda i: (i, 0, 0)),
                   pl.BlockSpec((1, KV_W, c), lambda i: (i, 0, 0))],
        out_shape=[jax.ShapeDtypeStruct((b, c, KV_W), BF16), jax.ShapeDtypeStruct((b, KV_W, c), BF16)],
        compiler_params=pltpu.CompilerParams(
            dimension_semantics=("arbitrary",), vmem_limit_bytes=VMEM_LIMIT_BYTES),
        name="compress_prompt",
    )(kc3, vc3, wk, wv, pek, pev, rope_tab)


def _select_blocks(imp, kk, candidate, forced):
    assert FORCE_SCORE > 2 * GQA
    start = jnp.where(candidate, jnp.where(forced, -jnp.inf, imp), -jnp.inf)
    score = start
    for _ in range(N_SELECT - N_FORCED):
        cm = jnp.max(score, axis=0, keepdims=True)
        first = jnp.min(jnp.where(score == cm, kk, LANES), axis=0, keepdims=True)
        score = jnp.where(kk == first, -jnp.inf, score)
    taken = (score == -jnp.inf) & (start > -jnp.inf)
    return candidate & (forced | taken)


def _softmax_rows(s):
    mx = jnp.max(s, axis=1, keepdims=True)
    ex = jnp.exp(s - mx)
    return ex * (1.0 / jnp.sum(ex, axis=1, keepdims=True))


def _tile_heads(x):
    return jnp.concatenate([x] * N_HEADS, axis=1)


def _values_by_kv_head(vt, p):
    half = GQA * Q_BLOCK
    return jnp.concatenate(
        [_dot(vt[h * HEAD_DIM:(h + 1) * HEAD_DIM], p[:, h * half:(h + 1) * half]) for h in range(N_KV)], axis=1)


def _attn_prompt_kernel(*refs):
    def body(sub, carry):
        _attn_query_block(sub, *refs)
        return carry

    lax.fori_loop(0, Q_SUB, body, 0)


def _attn_query_block(sub, qt_ref, gate_ref, szb_ref, kcc_ref, vcct_ref, ovlt_ref,
                      ksa_ref, vsb_ref, kwb_ref, vwb_ref, out_ref,
                      qaug_scr, acc_scr, sa_scr, sb_scr):
    i_blk = pl.program_id(1) * Q_SUB + sub
    q0 = i_blk * Q_BLOCK
    cols = N_HEADS * Q_BLOCK
    qt = qt_ref[0, sub]
    out_rows = pl.ds(pl.multiple_of(sub * Q_BLOCK, Q_BLOCK), Q_BLOCK)

    n_cmp = kcc_ref.shape[1]

    def compressed(extent):
        t_c = q0 + _lane_iota((extent, Q_BLOCK))
        seen = (_row_iota((extent, Q_BLOCK)) * CMP_STRIDE + (CMP_BLOCK - 1)) <= t_c
        s = _dot(kcc_ref[0, 0:extent, :], qt) + _tile_heads(jnp.where(seen, 0.0, NEG))
        mx = jnp.max(s, axis=0, keepdims=True)
        ex = jnp.exp2(s - mx)
        inv = jnp.where(mx > 0.5 * NEG, 1.0 / jnp.sum(ex, axis=0, keepdims=True), 0.0)
        p_c = ex * inv
        imps = []
        for h in range(N_KV):
            c0 = h * GQA * Q_BLOCK
            p_sum = (p_c[:, c0:c0 + Q_BLOCK] + p_c[:, c0 + Q_BLOCK:c0 + 2 * Q_BLOCK]
                     + p_c[:, c0 + 2 * Q_BLOCK:c0 + 3 * Q_BLOCK] + p_c[:, c0 + 3 * Q_BLOCK:c0 + 4 * Q_BLOCK])
            imps.append(_dot(ovlt_ref[:, 0:extent], p_sum.astype(BF16)))
        return (_values_by_kv_head(vcct_ref[0, :, 0:extent], p_c.astype(BF16)), *imps)

    n_need = (q0 + Q_BLOCK) // CMP_STRIDE
    extents = list(range(LANES, n_cmp + 1, LANES))
    o_c, *imp_t = lax.switch(
        jnp.minimum((n_need - 1) // LANES, len(extents) - 1),
        [functools.partial(compressed, ext) for ext in extents])

    n_wt = WIN_SPAN // LANES
    r_w, l_w = _row_iota((LANES, Q_BLOCK)), _lane_iota((LANES, Q_BLOCK))
    k_tiles, v_tiles = [], []
    for t in range(n_wt):
        j = i_blk - (n_wt - 1) + t
        jc = jnp.maximum(j, 0)
        off = jnp.where((l_w == 0) & (j < 0), 1.0, 0.0).astype(BF16)
        k_tiles.append(jnp.concatenate(
            [kwb_ref[0, pl.ds(pl.multiple_of(jc * LANES, LANES), LANES), :], off], axis=1))
        v_tiles.append(vwb_ref[0, jc])
    vwt = jnp.concatenate(v_tiles, axis=1)
    q_off = jnp.where(_row_iota((LANES, cols)) == 0, NEG, 0.0).astype(BF16)
    sw = _dot(jnp.concatenate(k_tiles, axis=0), jnp.concatenate([qt, q_off], axis=0))
    sw = jnp.concatenate([
        sw[0:LANES] + _tile_heads(jnp.where(r_w > l_w, 0.0, NEG)),
        sw[LANES:WIN_SPAN - LANES],
        sw[WIN_SPAN - LANES:] + _tile_heads(jnp.where(r_w <= l_w, 0.0, NEG))], axis=0)
    exw = jnp.exp2(sw - jnp.max(sw, axis=0, keepdims=True))
    o_w = _values_by_kv_head(vwt, exw.astype(BF16)) * (1.0 / jnp.sum(exw, axis=0, keepdims=True))

    kk = _row_iota((LANES, Q_BLOCK))
    tq = q0 + _lane_iota((LANES, Q_BLOCK))
    valid = kk * SEL_BLOCK <= tq
    cur = tq >> SEL_SHIFT
    forced = (kk == 0) | (kk == cur) | (kk == cur - 1)
    both = lambda a: jnp.concatenate([a] * N_KV, axis=1)
    picked = _select_blocks(jnp.concatenate(imp_t, axis=1), both(kk), both(valid), both(forced))
    qaug_scr[0:LANES, :] = qt
    for h in range(N_KV):
        selneg = jnp.where(picked[:, h * Q_BLOCK:(h + 1) * Q_BLOCK], 0.0, NEG).astype(BF16)
        for g in range(GQA):
            c0 = (h * GQA + g) * Q_BLOCK
            qaug_scr[LANES:2 * LANES, c0:c0 + Q_BLOCK] = selneg

    acc_scr[...] = jnp.zeros(acc_scr.shape, F32)
    tiles_per_step = SEL_TILE // LANES

    def score(step, dst):
        t0 = pl.multiple_of(step * SEL_TILE, SEL_TILE)
        dst[...] = _dot(ksa_ref[0, pl.ds(t0, SEL_TILE), :], qaug_scr[...])

    def attend(src, step, m_prev, l_prev, causal_rows=None):
        rows = SEL_TILE if causal_rows is None else causal_rows
        vt = jnp.concatenate([vsb_ref[0, step * tiles_per_step + i] for i in range(rows // LANES)], axis=1)
        sc = src[0:rows, :]
        if causal_rows is not None:
            tok = step * SEL_TILE + _row_iota((rows, Q_BLOCK))
            sc = sc + _tile_heads(jnp.where(tok <= q0 + _lane_iota((rows, Q_BLOCK)), 0.0, NEG))
        m_next = jnp.maximum(m_prev, jnp.max(sc, axis=0, keepdims=True))
        p = jnp.exp2(sc - m_next)
        alpha = jnp.exp2(m_prev - m_next)
        l_next = alpha * l_prev + jnp.sum(p, axis=0, keepdims=True)
        acc_scr[...] = alpha * acc_scr[...] + _values_by_kv_head(vt, p.astype(BF16))
        return m_next, l_next

    n_full = q0 // SEL_TILE
    score(0, sa_scr)

    def pair(jj, carry):
        m_c, l_c = carry
        score(2 * jj + 1, sb_scr)
        m_c, l_c = attend(sa_scr, 2 * jj, m_c, l_c)
        score(2 * jj + 2, sa_scr)
        return attend(sb_scr, 2 * jj + 1, m_c, l_c)

    m_run, l_run = lax.fori_loop(
        0, n_full // 2, pair,
        (jnp.full((1, cols), -jnp.inf, F32), jnp.zeros((1, cols), F32)))

    def last_step(src, m_c, l_c):
        extents = list(range(Q_BLOCK, SEL_TILE + 1, Q_BLOCK))
        return lax.switch(
            (q0 % SEL_TILE) // Q_BLOCK,
            [lambda m, l, ext=ext: attend(src, n_full, m, l, causal_rows=ext)[1] for ext in extents],
            m_c, l_c)

    def odd_tail(m_c, l_c):
        score(n_full, sb_scr)
        m_c, l_c = attend(sa_scr, n_full - 1, m_c, l_c)
        return last_step(sb_scr, m_c, l_c)

    def even_tail(m_c, l_c):
        return last_step(sa_scr, m_c, l_c)

    l_run = lax.cond(n_full % 2 == 1, odd_tail, even_tail, m_run, l_run)

    gt = gate_ref[0, sub]
    o_s = acc_scr[...] * (1.0 / l_run)
    heads = []
    for e in range(N_HEADS):
        c0 = e * Q_BLOCK
        heads.append(gt[3 * e:3 * e + 1, :] * o_c[:, c0:c0 + Q_BLOCK]
                     + gt[3 * e + 1:3 * e + 2, :] * o_s[:, c0:c0 + Q_BLOCK]
                     + gt[3 * e + 2:3 * e + 3, :] * o_w[:, c0:c0 + Q_BLOCK])
    for j in range(N_HEADS // 2):
        slab = jnp.concatenate([heads[2 * j], heads[2 * j + 1]], axis=0).T
        out_ref[0, out_rows, j * LANES:(j + 1) * LANES] = (
            slab * szb_ref[0, out_rows, j * LANES:(j + 1) * LANES]).astype(BF16)


def _attn_prompt(qt4, gate4, szb3, kcc, vcct, ovlt, ksa3, vsb4, kwb3, vwb4):
    b, nq = qt4.shape[0], qt4.shape[1]
    t = nq * Q_BLOCK
    cols = N_HEADS * Q_BLOCK
    assert nq % Q_SUB == 0
    qtile = lambda a: pl.BlockSpec((1, Q_SUB) + a.shape[2:], lambda bi, i: (bi, i, 0, 0))
    rows = pl.BlockSpec((1, Q_SUB * Q_BLOCK, D_B), lambda bi, i: (bi, i, 0))
    seq = lambda a: pl.BlockSpec((1,) + a.shape[1:], lambda bi, i, nd=a.ndim: (bi,) + (0,) * (nd - 1))
    return pl.pallas_call(
        _attn_prompt_kernel,
        grid=(b, nq // Q_SUB),
        in_specs=[qtile(qt4), qtile(gate4), rows,
                  seq(kcc), seq(vcct), pl.BlockSpec(ovlt.shape, lambda bi, i: (0, 0)),
                  seq(ksa3), seq(vsb4), seq(kwb3), seq(vwb4)],
        out_specs=rows,
        out_shape=jax.ShapeDtypeStruct((b, t, D_B), BF16),
        scratch_shapes=[pltpu.VMEM((2 * LANES, cols), BF16),
                        pltpu.VMEM((HEAD_DIM, cols), F32),
                        pltpu.VMEM((SEL_TILE, cols), F32),
                        pltpu.VMEM((SEL_TILE, cols), F32)],
        compiler_params=pltpu.CompilerParams(
            dimension_semantics=("arbitrary", "arbitrary"), vmem_limit_bytes=VMEM_LIMIT_BYTES),
        name="attn_prompt",
    )(qt4, gate4, szb3, kcc, vcct, ovlt, ksa3, vsb4, kwb3, vwb4)


def _outproj_kernel(x_ref, a_ref, m_ref, wa_ref, wb_ref, fg_ref, y_ref):
    y = x_ref[...] + _dot(a_ref[...], wa_ref[...]) + _dot(m_ref[...].astype(BF16), wb_ref[...])
    y_ref[...] = _rms_rows(y, fg_ref[...])


def _outproj(x2, a_out, mix_b, wa, wb, fg, tm):
    n = x2.shape[0]
    row = lambda i: (i, 0)
    const2 = lambda i: (0, 0)
    return pl.pallas_call(
        _outproj_kernel,
        grid=(n // tm,),
        in_specs=[pl.BlockSpec((tm, D_MODEL), row),
                  pl.BlockSpec((tm, a_out.shape[1]), row),
                  pl.BlockSpec((tm, mix_b.shape[1]), row),
                  pl.BlockSpec(wa.shape, const2), pl.BlockSpec(wb.shape, const2),
                  pl.BlockSpec((1, D_MODEL), const2)],
        out_specs=pl.BlockSpec((tm, D_MODEL), row),
        out_shape=jax.ShapeDtypeStruct((n, D_MODEL), F32),
        compiler_params=pltpu.CompilerParams(
            dimension_semantics=("arbitrary",), vmem_limit_bytes=VMEM_LIMIT_BYTES),
        name="outproj",
    )(x2, a_out, mix_b, wa, wb, fg)


def _page_copies(pt_ref, pool_ref, buf_ref, sem_ref, step, slot):
    group, n_pages = buf_ref.shape[1], buf_ref.shape[2]
    return [pltpu.make_async_copy(pool_ref.at[pt_ref[step * group + g, p]], buf_ref.at[slot, g, p],
                                  sem_ref.at[slot])
            for g in range(group) for p in range(n_pages)]


def _gather_pages(pt_ref, pools, bufs, sems):
    b = pl.program_id(0)
    slot = lax.rem(b, 2)

    def start(step, sl):
        for pool, buf, sem in zip(pools, bufs, sems):
            for cp in _page_copies(pt_ref, pool, buf, sem, step, sl):
                cp.start()

    @pl.when(b == 0)
    def _():
        start(0, 0)

    @pl.when(b + 1 < pl.num_programs(0))
    def _():
        start(b + 1, 1 - slot)

    for pool, buf, sem in zip(pools, bufs, sems):
        for cp in _page_copies(pt_ref, pool, buf, sem, b, slot):
            cp.wait()
    return slot


def _decode_cmp_kernel(pt_ref, ck_hbm, cv_hbm, q_ref, wk_ref, wv_ref, pek_ref, pev_ref, rope_ref, ovl_ref,
                       oc_ref, psum_ref, kbuf, vbuf, ksem, vsem, ktok_scr, vtok_scr, kpos_scr, vpos_scr):
    slot = _gather_pages(pt_ref, (ck_hbm, cv_hbm), (kbuf, vbuf), (ksem, vsem))
    group, n_pages = kbuf.shape[1], kbuf.shape[2]

    @pl.when(pl.program_id(0) == 0)
    def _():
        kpos_scr[...] = _position_term(pek_ref, wk_ref)
        vpos_scr[...] = _position_term(pev_ref, wv_ref)

    def chunk_rows(buf, tok_scr, g):
        for p in range(n_pages):
            tok_scr[g, p * PAGE_SIZE:(p + 1) * PAGE_SIZE, :] = buf[slot, g, p].T
        n = tok_scr.shape[1] // CMP_STRIDE
        return jnp.concatenate(
            [tok_scr[g, pl.ds(r, n, stride=CMP_STRIDE), :] for r in range(CMP_STRIDE)], axis=1)

    for g in range(group):
        kch = chunk_rows(kbuf, ktok_scr, g)
        vch = chunk_rows(vbuf, vtok_scr, g)
        kcc = _rope(_compress_chunks(kch, kpos_scr[...], wk_ref), rope_ref[...]).astype(BF16)
        vcc = _compress_chunks(vch, vpos_scr[...], wv_ref).astype(BF16)
        n_chunks = kch.shape[0]
        q = q_ref[g]
        rows = q.shape[0]
        s = _dot_nt(q, kcc)
        seen = _lane_iota((rows, n_chunks)) < n_chunks - 1
        p_c = jnp.where(seen, _softmax_rows(jnp.where(seen, s, NEG)), 0.0)
        oc_ref[g] = _dot(p_c.astype(BF16), vcc)[0:N_HEADS]
        ps0 = jnp.sum(p_c[0:GQA], axis=0, keepdims=True)
        ps1 = jnp.sum(p_c[GQA:2 * GQA], axis=0, keepdims=True)
        psum = jnp.where(_row_iota((rows, n_chunks)) < N_HEADS // 2, ps0, ps1)
        psum_ref[g] = _dot(psum.astype(BF16), ovl_ref[...])[0:N_HEADS]


def _decode_cmp(page_table, ck_pool, cv_pool, q16, wk, wv, pek, pev, rope_tab, ovl):
    nb, n_pages = page_table.shape
    group = DECODE_GROUP
    assert nb % group == 0
    hbm = pl.BlockSpec(memory_space=pl.ANY)
    page_buf = pltpu.VMEM((2, group, n_pages, KV_W, PAGE_SIZE), F32)
    tok_buf = pltpu.VMEM((group, n_pages * PAGE_SIZE, KV_W), F32)
    const2 = lambda b, pt: (0, 0)
    per_b = lambda a: pl.BlockSpec((group,) + a.shape[1:], lambda b, pt: (b, 0, 0))
    in_specs = [hbm, hbm, per_b(q16),
                pl.BlockSpec(wk.shape, const2), pl.BlockSpec(wv.shape, const2),
                pl.BlockSpec(pek.shape, const2), pl.BlockSpec(pev.shape, const2),
                pl.BlockSpec(rope_tab.shape, const2), pl.BlockSpec(ovl.shape, const2)]
    out_blk = pl.BlockSpec((group, N_HEADS, LANES), lambda b, pt: (b, 0, 0))
    grid_spec = pltpu.PrefetchScalarGridSpec(
        num_scalar_prefetch=1, grid=(nb // group,), in_specs=in_specs, out_specs=[out_blk, out_blk],
        scratch_shapes=[page_buf, page_buf, pltpu.SemaphoreType.DMA((2,)), pltpu.SemaphoreType.DMA((2,)),
                        tok_buf, tok_buf, pltpu.VMEM((1, KV_W), F32), pltpu.VMEM((1, KV_W), F32)])
    return pl.pallas_call(
        _decode_cmp_kernel,
        grid_spec=grid_spec,
        out_shape=[jax.ShapeDtypeStruct((nb, N_HEADS, LANES), F32)] * 2,
        compiler_params=pltpu.CompilerParams(
            dimension_semantics=("arbitrary",), vmem_limit_bytes=VMEM_LIMIT_BYTES),
        name="decode_cmp",
    )(page_table, ck_pool, cv_pool, q16, wk, wv, pek, pev, rope_tab, ovl)


def _decode_topk_kernel(imp_ref, sel_ref, *, n_blk, cur):
    n = imp_ref.shape[0]
    tiles = [imp_ref[i * LANES:(i + 1) * LANES, :].T for i in range(n // LANES)]
    imp_t = jnp.concatenate(tiles, axis=1)
    kk = _row_iota(imp_t.shape)
    forced = (kk == 0) | (kk == cur) | (kk == cur - 1)
    selneg = jnp.where(_select_blocks(imp_t, kk, kk < n_blk, forced), 0.0, NEG)
    for i in range(n // LANES):
        sel_ref[i * LANES:(i + 1) * LANES, :] = selneg[:, i * LANES:(i + 1) * LANES].T


def _decode_topk(imp2, n_blk, cur):
    return pl.pallas_call(
        functools.partial(_decode_topk_kernel, n_blk=n_blk, cur=cur),
        out_shape=jax.ShapeDtypeStruct(imp2.shape, F32),
        compiler_params=pltpu.CompilerParams(vmem_limit_bytes=VMEM_LIMIT_BYTES),
        name="decode_topk",
    )(imp2)


def _decode_attend(s, q32, vals_t_bf, k_new, v_new, bias_new):
    kn = k_new.astype(BF16).astype(F32)
    s_new = jnp.sum(q32 * kn, axis=1, keepdims=True) + bias_new
    mx = jnp.maximum(jnp.max(s, axis=1, keepdims=True), s_new)
    ex = jnp.exp(s - mx)
    ex_new = jnp.exp(s_new - mx)
    inv = 1.0 / (jnp.sum(ex, axis=1, keepdims=True) + ex_new)
    p = (ex * inv).astype(BF16)
    p_new = (ex_new * inv).astype(BF16).astype(F32)
    return _dot_nt(p, vals_t_bf) + p_new * v_new.astype(BF16).astype(F32)


P_Q, P_SEL, P_OC, P_SZB, P_NEW, P_GATE, P_ROWS = 0, 16, 32, 40, 48, 56, 64


def _row_to_col(row):
    n = row.shape[1]
    diag = _row_iota((n, n)) == _lane_iota((n, n))
    return jnp.sum(jnp.where(diag, row, 0.0), axis=1, keepdims=True)


def _decode_sel_kernel(pt_ref, ks_hbm, vs_hbm, ckw_ref, cvw_ref, pack_ref, onehot_ref,
                       mix_ref, okw_ref, ovw_ref, kbuf, vbuf, ksem, vsem, *, cur):
    slot = _gather_pages(pt_ref, (ks_hbm, vs_hbm), (kbuf, vbuf), (ksem, vsem))
    group, n_pages = kbuf.shape[1], kbuf.shape[2]
    for g in range(group):
        pk = pack_ref[g]
        q32 = pk[P_Q:P_Q + 2 * N_HEADS]
        q = q32.astype(BF16)
        sel = pk[P_SEL:P_SEL + 2 * N_HEADS].astype(BF16)
        rows = q.shape[0]
        ks_new, vs_new = pk[P_NEW:P_NEW + 1], pk[P_NEW + 1:P_NEW + 2]
        kw_new, vw_new = pk[P_NEW + 2:P_NEW + 3], pk[P_NEW + 3:P_NEW + 4]

        kt = jnp.concatenate([kbuf[slot, g, p] for p in range(n_pages)], axis=1).astype(BF16)
        vt = jnp.concatenate([vbuf[slot, g, p] for p in range(n_pages)], axis=1).astype(BF16)
        kaug = jnp.concatenate([kt, onehot_ref[...]], axis=0)
        qaug = jnp.concatenate([q, sel], axis=1)
        s_sel = _dot(qaug, kaug)
        o_s = _decode_attend(s_sel, q32, vt, ks_new, vs_new, sel.astype(F32)[:, cur:cur + 1])

        kwin = ckw_ref[g]
        vwin = cvw_ref[g]
        keep = kwin.shape[1]
        visible = _lane_iota((rows, keep)) > keep - WINDOW
        s_win = jnp.where(visible, _dot(q, kwin.astype(BF16)), NEG)
        o_w = _decode_attend(s_win, q32, vwin.astype(BF16), kw_new, vw_new, 0.0)

        gates = pk[P_GATE:P_GATE + N_HEADS]
        o = (gates[:, 0:1] * pk[P_OC:P_OC + N_HEADS] + gates[:, 1:2] * o_s[0:N_HEADS]
             + gates[:, 2:3] * o_w[0:N_HEADS])
        mix_ref[g] = o * pk[P_SZB:P_SZB + N_HEADS]

        last = _lane_iota((KV_W, keep)) == keep - 1
        okw_ref[g] = jnp.where(last, _row_to_col(kw_new), pltpu.roll(kwin, keep - 1, 1))
        ovw_ref[g] = jnp.where(last, _row_to_col(vw_new), pltpu.roll(vwin, keep - 1, 1))


def _decode_sel(page_table, ks_pool, vs_pool, ckw, cvw, pack, onehot, cur):
    nb, n_pages = page_table.shape
    group = DECODE_GROUP
    assert nb % group == 0
    keep = ckw.shape[2]
    hbm = pl.BlockSpec(memory_space=pl.ANY)
    page_buf = pltpu.VMEM((2, group, n_pages, KV_W, PAGE_SIZE), F32)
    per_b = lambda a: pl.BlockSpec((group,) + a.shape[1:], lambda b, pt: (b, 0, 0))
    in_specs = [hbm, hbm, per_b(ckw), per_b(cvw), per_b(pack),
                pl.BlockSpec(onehot.shape, lambda b, pt: (0, 0))]
    out_specs = [pl.BlockSpec((group, N_HEADS, LANES), lambda b, pt: (b, 0, 0)),
                 pl.BlockSpec((group, KV_W, keep), lambda b, pt: (b, 0, 0)),
                 pl.BlockSpec((group, KV_W, keep), lambda b, pt: (b, 0, 0))]
    grid_spec = pltpu.PrefetchScalarGridSpec(
        num_scalar_prefetch=1, grid=(nb // group,), in_specs=in_specs, out_specs=out_specs,
        scratch_shapes=[page_buf, page_buf, pltpu.SemaphoreType.DMA((2,)), pltpu.SemaphoreType.DMA((2,))])
    return pl.pallas_call(
        functools.partial(_decode_sel_kernel, cur=cur),
        grid_spec=grid_spec,
        out_shape=[jax.ShapeDtypeStruct((nb, N_HEADS, LANES), F32),
                   jax.ShapeDtypeStruct((nb, KV_W, keep), F32),
                   jax.ShapeDtypeStruct((nb, KV_W, keep), F32)],
        compiler_params=pltpu.CompilerParams(
            dimension_semantics=("arbitrary",), vmem_limit_bytes=VMEM_LIMIT_BYTES),
        name="decode_sel",
    )(page_table, ks_pool, vs_pool, ckw, cvw, pack, onehot)


def _rope_table(pos):
    half = ROT_DIM // 2
    inv = jnp.power(jnp.float32(ROPE_THETA), -jnp.arange(half, dtype=F32) / half)
    ang = pos.astype(F32)[:, None] * inv[None, :]
    cs = jnp.concatenate([jnp.cos(ang), jnp.sin(ang)], axis=1)
    top16 = lambda a: lax.bitcast_convert_type(
        lax.bitcast_convert_type(a, jnp.uint32) & jnp.uint32(0xFFFF0000), F32)
    hi = top16(cs)
    mid = top16(cs - hi)
    pieces = jnp.concatenate([hi, mid, cs - hi - mid], axis=1)
    lane = jnp.arange(2 * LANES)
    seg, l64 = lane // LANES, lane % HEAD_DIM
    src = jnp.where(seg == 0, l64 % half, half + l64 % half)
    sign = jnp.where(seg == 0, l64 < ROT_DIM, l64 < half).astype(F32)
    sign = jnp.where(seg == 1, -sign, sign)
    place = (jnp.arange(ROT_DIM)[:, None] == src[None, :]).astype(F32) * sign[None, :]
    ones = ((seg == 0) & (l64 >= ROT_DIM)).astype(F32)
    return jnp.dot(pieces, jnp.concatenate([place] * 3, axis=0)) + ones[None, :]


def _compress_weights(w_c, pe_c):
    w4 = w_c.reshape(2, CMP_STRIDE, HEAD_DIM, HEAD_DIM).astype(BF16)
    zero = jnp.zeros_like(w4[0])
    wb = jnp.stack([jnp.concatenate([w4[a] if g == h else zero for a in range(2) for g in range(N_KV)], axis=-1)
                    for h in range(N_KV)], axis=1)
    wb = wb.reshape(CMP_STRIDE * KV_W, 2 * KV_W)
    pe = jnp.broadcast_to(pe_c.reshape(2, CMP_STRIDE, 1, HEAD_DIM), (2, CMP_STRIDE, N_KV, HEAD_DIM))
    return wb, jnp.pad(pe.reshape(2, CMP_STRIDE * KV_W), ((0, 14), (0, 0)))


def _overlap_matrix(n_rows, n_blk):
    cs = jnp.arange(n_rows, dtype=jnp.int32)[:, None] * CMP_STRIDE
    bs = jnp.arange(LANES, dtype=jnp.int32)[None, :] * SEL_BLOCK
    hit = (cs < bs + SEL_BLOCK) & (cs + CMP_BLOCK > bs) & (jnp.arange(LANES)[None, :] < n_blk)
    return hit.astype(BF16)


def kernel(x_prompt, x_sample, cache_k_cmp, cache_v_cmp, cache_k_sel, cache_v_sel, cache_k_win,
           cache_v_win, page_table, norm_g, w_in, ln_v_g, ln_v_b, w_s, b_s, w_ck, pe_ck, w_cv,
           pe_cv, w_out, final_g):
    depth = norm_g.shape[0]
    assert depth == 1
    bsz, seq, _ = x_prompt.shape
    nb, dec_seq, _ = x_sample.shape
    assert dec_seq == 1 and seq % SEL_TILE == 0 and seq >= WIN_SPAN and (seq // CMP_STRIDE) % LANES == 0
    n_pages = page_table.shape[1]
    past = n_pages * PAGE_SIZE
    keep = cache_k_win.shape[2]
    assert keep == WINDOW and past % SEL_BLOCK == 0

    w = w_in[0]
    w_pad = (w[:, :C_G].astype(BF16),
             jnp.pad(w[:, C_G:C_G + N_GATE_COLS], ((0, 0), (0, LANES - N_GATE_COLS))).astype(BF16),
             w[:, C_G + N_GATE_COLS:].astype(BF16))
    g_row = norm_g[0][None, :]
    lng = ln_v_g[0][None, :]
    lnb = ln_v_b[0][None, :]
    fg = final_g[None, :]
    wk_c, pek = _compress_weights(w_ck[0], pe_ck[0])
    wv_c, pev = _compress_weights(w_cv[0], pe_cv[0])
    w_o = w_out[0].astype(BF16)
    wo_a, wo_b = w_o[:D_A], w_o[D_A:]

    tril = jnp.tril(jnp.ones((CHUNK, CHUNK), w_s.dtype))
    ws_bf = (w_s[0] * tril).astype(BF16)
    bias_full = jnp.repeat(b_s[0].T, HEAD_DIM, axis=1)
    xp2 = x_prompt.reshape(bsz * seq, D_MODEL)
    (a_out, qt, kc, vc, kc_t, vc_t, ks_t, vs_t, kw_t, vw_t, ksa, kwb, vsb, vwb, gates_t, szb) = _inproj_prompt(
        xp2, g_row, w_pad, lng, lnb, ws_bf, bias_full,
        _rope_table(jnp.arange(seq, dtype=jnp.int32)), seq)
    n_chunks = seq // CMP_STRIDE
    kcc, vcc_t = _compress_prompt(
        kc.reshape(bsz, n_chunks, CMP_STRIDE * KV_W), vc.reshape(bsz, n_chunks, CMP_STRIDE * KV_W),
        wk_c, wv_c, pek, pev,
        _rope_table(jnp.arange(n_chunks, dtype=jnp.int32) * CMP_STRIDE))
    b3 = lambda a: a.reshape(bsz, seq, a.shape[-1])
    mix_p = _attn_prompt(qt, gates_t, b3(szb), kcc, vcc_t,
                         _overlap_matrix(n_chunks, seq // SEL_BLOCK).T,
                         b3(ksa), vsb, b3(kwb), vwb)
    y_prompt = _outproj(xp2, a_out, mix_p.reshape(bsz * seq, D_B), wo_a, wo_b, fg, OUT_ROW_TILE)
    y_prompt = y_prompt.reshape(bsz, seq, D_MODEL)
    st = lambda a: jnp.transpose(a.reshape(a.shape[0], N_KV, HEAD_DIM, a.shape[2]), (0, 3, 1, 2))[None]
    keep_p = min(WINDOW, seq)

    xs2 = x_sample.reshape(nb, D_MODEL)
    wsv = jnp.repeat(w_s[0][:, 0, 0], HEAD_DIM)[None, :]
    bsv = jnp.repeat(b_s[0][:, 0], HEAD_DIM)[None, :]
    (a_s, qa_s, kc_s, vc_s, ks_s, vs_s, kw_s, vw_s, gates_s, szb_s, vn_s) = _inproj_decode(
        xs2, g_row, w_pad, lng, lnb, wsv, bsv,
        _rope_table(jnp.full((nb,), past, dtype=jnp.int32)))
    q16 = jnp.pad(qa_s.reshape(nb, N_HEADS, LANES), ((0, 0), (0, N_HEADS), (0, 0)))
    d_chunks = past // CMP_STRIDE
    n_blk = -(-(past + 1) // SEL_BLOCK)
    cur = past // SEL_BLOCK
    kv_t = lambda c: jnp.transpose(c[0], (0, 2, 3, 1)).reshape(c.shape[1], KV_W, c.shape[2])
    oc_s, imp_s = _decode_cmp(
        page_table, kv_t(cache_k_cmp), kv_t(cache_v_cmp),
        q16, wk_c, wv_c, pek, pev,
        _rope_table(jnp.arange(d_chunks, dtype=jnp.int32) * CMP_STRIDE),
        _overlap_matrix(d_chunks, n_blk))
    selneg = _decode_topk(imp_s[:, ::GQA, :].reshape(nb * N_KV, LANES), n_blk, cur)
    sel16 = jnp.pad(jnp.repeat(selneg.reshape(nb, N_KV, LANES), GQA, axis=1), ((0, 0), (0, N_HEADS), (0, 0)))
    tok_blk = jnp.arange(past, dtype=jnp.int32)[None, :] // SEL_BLOCK
    onehot = (tok_blk == jnp.arange(LANES, dtype=jnp.int32)[:, None]).astype(BF16)
    gate_rows = jnp.pad(gates_s[:, :N_GATE_COLS].reshape(nb, N_HEADS, 3), ((0, 0), (0, 0), (0, LANES - 3)))
    new_rows = jnp.pad(jnp.stack([ks_s, vs_s, kw_s, vw_s], axis=1), ((0, 0), (0, P_GATE - P_NEW - 4), (0, 0)))
    pack = jnp.concatenate(
        [q16.astype(F32), sel16, oc_s, szb_s.reshape(nb, N_HEADS, LANES), new_rows, gate_rows], axis=1)
    assert pack.shape[1] == P_ROWS
    mix_s, okw, ovw = _decode_sel(
        page_table, kv_t(cache_k_sel), kv_t(cache_v_sel), kv_t(cache_k_win), kv_t(cache_v_win),
        pack, onehot, cur)
    wo_slab = jnp.zeros((N_HEADS, N_KV, HEAD_DIM, D_MODEL), BF16)
    wo_heads = wo_b.reshape(N_HEADS, HEAD_DIM, D_MODEL)
    for e in range(N_HEADS):
        wo_slab = wo_slab.at[e, e // GQA].set(wo_heads[e])
    y_sample = _outproj(xs2, a_s, mix_s.reshape(nb, N_HEADS * LANES), wo_a,
                        wo_slab.reshape(N_HEADS * LANES, D_MODEL), fg, nb)
    y_sample = y_sample.reshape(nb, 1, D_MODEL)
    ss = lambda a: a.reshape(1, nb, 1, N_KV, HEAD_DIM)

    return (y_prompt, y_sample,
            st(kc_t), st(vc_t), st(ks_t), st(vs_t),
            st(kw_t[:, :, seq - keep_p:]), st(vw_t[:, :, seq - keep_p:]),
            ss(kc_s), ss(vc_s), ss(ks_s), ss(vs_s),
            st(okw), st(ovw),
            vn_s.reshape(1, nb, 1, D_A))
```

```python
import functools

import jax
import jax.numpy as jnp
from jax import lax
from jax.experimental import pallas as pl
from jax.experimental.pallas import tpu as pltpu

F32 = jnp.float32
BF16 = jnp.bfloat16

D_MODEL = 1024
HEAD_DIM = 64
D_A = 512
D_B = 512
A_GROUPS = 8
CHUNK = 128
N_HEADS = 8
N_KV = 2
GQA = 4
KV_W = 128
ROT_DIM = 16
ROPE_THETA = 500000.0
CMP_BLOCK = 32
CMP_STRIDE = 16
SEL_BLOCK = 64
SEL_SHIFT = 6
N_SELECT = 16
WINDOW = 512
Q_BLOCK = 128
PAGE_SIZE = 128
NORM_EPS = 1e-6
FORCE_SCORE = 1e4
N_FORCED = 3
NEG = -1e30
LOG2_E = 1.4426950408889634

LANES = 128
VMEM_LIMIT_BYTES = 56 * 1024 * 1024

C_U, C_V, C_ZA, C_Q = 0, 512, 1024, 1536
C_KC, C_VC, C_KS, C_VS, C_KW, C_VW = 2048, 2176, 2304, 2432, 2560, 2688
C_G, C_ZB, C_END = 2816, 2944, 3456
N_GATE_COLS = 3 * N_HEADS

ROW_TILE = 512
OUT_ROW_TILE = 1024
Q_SUB = 4
SEL_TILE = 512
DECODE_GROUP = 4
WIN_SPAN = WINDOW + Q_BLOCK

_NT = (((1,), (1,)), ((), ()))


def _dot(a, b):
    return jnp.dot(a, b, preferred_element_type=F32)


def _dot_nt(a, b):
    return lax.dot_general(a, b, _NT, preferred_element_type=F32)


def _lane_iota(shape):
    return lax.broadcasted_iota(jnp.int32, shape, len(shape) - 1)


def _row_iota(shape):
    return lax.broadcasted_iota(jnp.int32, shape, len(shape) - 2)


def _rope(x, tab):
    c = tab[:, 0:LANES]
    s1 = tab[:, LANES:2 * LANES]
    return x * c + pltpu.roll(x, LANES - ROT_DIM // 2, 1) * s1 - pltpu.roll(x * s1, ROT_DIM // 2, 1)


def _rms_rows(x, g):
    ms = jnp.mean(x * x, axis=-1, keepdims=True)
    return x * lax.rsqrt(ms + NORM_EPS) * g


def _layer_norm_rows(v, g, b):
    mu = jnp.mean(v, axis=-1, keepdims=True)
    vc = v - mu
    var = jnp.mean(vc * vc, axis=-1, keepdims=True)
    return vc * lax.rsqrt(var + NORM_EPS) * g + b


def _head_slabs(x512):
    out = []
    rows = x512.shape[0]
    lane = _lane_iota((rows, LANES))
    lo = lane < HEAD_DIM
    for j in range(4):
        slab = x512[:, j * LANES:(j + 1) * LANES]
        swapped = pltpu.roll(slab, HEAD_DIM, 1)
        if j < 2:
            out.append(jnp.where(lo, slab, 0.0))
            out.append(jnp.where(lo, swapped, 0.0))
        else:
            out.append(jnp.where(lo, 0.0, swapped))
            out.append(jnp.where(lo, 0.0, slab))
    return out


def _project(x_ref, g_ref, w_refs, z_scr):
    hb = _rms_rows(x_ref[...], g_ref[...]).astype(BF16)
    wm_ref, wg_ref, wz_ref = w_refs
    z_scr[:, 0:C_G] = _dot(hb, wm_ref[...])
    z_scr[:, C_G:C_ZB] = _dot(hb, wg_ref[...])
    z_scr[:, C_ZB:C_END] = _dot(hb, wz_ref[...])


def _inproj_prompt_kernel(x_ref, g_ref, wm_ref, wg_ref, wz_ref, lng_ref, lnb_ref, ws_ref, bias_ref, rope_ref,
                          aout_ref, qt_ref, kc_ref, vc_ref,
                          kct_ref, vct_ref, kst_ref, vst_ref, kwt_ref, vwt_ref,
                          ksa_ref, kwb_ref, vsb_ref, vwb_ref, gate_ref, szb_ref,
                          z_scr, kstage_scr, vstage_scr, *, tm, tiles_per_batch):
    _project(x_ref, g_ref, (wm_ref, wg_ref, wz_ref), z_scr)
    tab = rope_ref[...]

    lane = _lane_iota((CHUNK, LANES))
    lo = lane < HEAD_DIM
    for c in range(tm // CHUNK):
        r0 = c * CHUNK
        u = z_scr[r0:r0 + CHUNK, C_U:C_U + D_A]
        v = z_scr[r0:r0 + CHUNK, C_V:C_V + D_A]
        za = z_scr[r0:r0 + CHUNK, C_ZA:C_ZA + D_A]
        vn = _layer_norm_rows(v, lng_ref[...], lnb_ref[...])
        parts = []
        for p in range(A_GROUPS // 2):
            vp = vn[:, p * LANES:(p + 1) * LANES]
            v_lo = jnp.where(lo, vp, 0.0).astype(BF16)
            v_hi = jnp.where(lo, 0.0, vp).astype(BF16)
            parts.append(_dot(ws_ref[2 * p], v_lo) + _dot(ws_ref[2 * p + 1], v_hi))
        s = jnp.concatenate(parts, axis=1) + bias_ref[...]
        aout_ref[r0:r0 + CHUNK, :] = (u * s * jax.nn.silu(za)).astype(BF16)

    q = z_scr[:, C_Q:C_Q + D_B]
    qr = jnp.concatenate(
        [_rope(q[:, j * LANES:(j + 1) * LANES], tab) for j in range(4)], axis=1) * (HEAD_DIM ** -0.5 * LOG2_E)
    slabs = _head_slabs(qr)
    kc = z_scr[:, C_KC:C_KC + KV_W]
    vc = z_scr[:, C_VC:C_VC + KV_W]
    for src, stage, dst in ((kc, kstage_scr, kc_ref), (vc, vstage_scr, vc_ref)):
        stage[...] = src
        for r in range(CMP_STRIDE):
            dst[:, r * KV_W:(r + 1) * KV_W] = stage[pl.ds(r, tm // CMP_STRIDE, stride=CMP_STRIDE), :]
    ks = _rope(z_scr[:, C_KS:C_KS + KV_W], tab)
    vs = z_scr[:, C_VS:C_VS + KV_W]
    kw = _rope(z_scr[:, C_KW:C_KW + KV_W], tab)
    vw = z_scr[:, C_VW:C_VW + KV_W]
    gates = jax.nn.sigmoid(z_scr[:, C_G:C_G + LANES])
    pos = (pl.program_id(0) % tiles_per_batch) * tm + _row_iota((tm, LANES))
    onehot = jnp.where((pos >> SEL_SHIFT) == _lane_iota((tm, LANES)), 1.0, 0.0)
    ksa_ref[:, 0:LANES] = ks.astype(BF16)
    ksa_ref[:, LANES:2 * LANES] = onehot.astype(BF16)
    kwb_ref[...] = kw.astype(BF16)
    for j in range(tm // LANES):
        r0, r1 = j * LANES, (j + 1) * LANES
        kct_ref[0, :, r0:r1] = kc[r0:r1].T
        vct_ref[0, :, r0:r1] = vc[r0:r1].T
        kst_ref[0, :, r0:r1] = ks[r0:r1].T
        kwt_ref[0, :, r0:r1] = kw[r0:r1].T
        vs_t, vw_t = vs[r0:r1].T, vw[r0:r1].T
        vst_ref[0, :, r0:r1] = vs_t
        vwt_ref[0, :, r0:r1] = vw_t
        vsb_ref[0, j] = vs_t.astype(BF16)
        vwb_ref[0, j] = vw_t.astype(BF16)
        for e in range(N_HEADS):
            qt_ref[0, j, :, e * Q_BLOCK:(e + 1) * Q_BLOCK] = slabs[e][r0:r1].T.astype(BF16)
        gate_ref[0, j] = gates[r0:r1].T
    szb_ref[...] = jax.nn.silu(z_scr[:, C_ZB:C_ZB + D_B])


def _inproj_prompt(x2, g, w, lng, lnb, ws_bf, bias_full, rope_tab, seq, tm=ROW_TILE):
    n = x2.shape[0]
    tiles_per_batch = seq // tm
    bsz = n // seq
    n_tok_tiles = seq // LANES
    row = lambda i: (i, 0)
    const2 = lambda i: (0, 0)
    rows2d = lambda width, dtype: (jax.ShapeDtypeStruct((n, width), dtype), pl.BlockSpec((tm, width), row))
    kv_t = (jax.ShapeDtypeStruct((bsz, KV_W, seq), F32),
            pl.BlockSpec((1, KV_W, tm), lambda i: (i // tiles_per_batch, 0, i % tiles_per_batch)))
    chunk_rows = (jax.ShapeDtypeStruct((n // CMP_STRIDE, CMP_STRIDE * KV_W), F32),
                  pl.BlockSpec((tm // CMP_STRIDE, CMP_STRIDE * KV_W), row))
    tiles = lambda r, c, dtype: (
        jax.ShapeDtypeStruct((bsz, n_tok_tiles, r, c), dtype),
        pl.BlockSpec((1, tm // LANES, r, c), lambda i: (i // tiles_per_batch, i % tiles_per_batch, 0, 0)))
    outs = [
        rows2d(D_A, BF16),
        tiles(KV_W, N_HEADS * Q_BLOCK, BF16),
        chunk_rows, chunk_rows,
        kv_t, kv_t, kv_t, kv_t, kv_t, kv_t,
        rows2d(2 * KV_W, BF16),
        rows2d(KV_W, BF16),
        tiles(KV_W, LANES, BF16),
        tiles(KV_W, LANES, BF16),
        tiles(LANES, LANES, F32),
        rows2d(D_B, F32),
    ]
    out_shapes = [o[0] for o in outs]
    out_specs = [o[1] for o in outs]
    in_specs = [
        pl.BlockSpec((tm, D_MODEL), row),
        pl.BlockSpec((1, D_MODEL), const2),
        pl.BlockSpec((D_MODEL, C_G), const2),
        pl.BlockSpec((D_MODEL, C_ZB - C_G), const2),
        pl.BlockSpec((D_MODEL, C_END - C_ZB), const2),
        pl.BlockSpec((1, D_A), const2),
        pl.BlockSpec((1, D_A), const2),
        pl.BlockSpec((A_GROUPS, CHUNK, CHUNK), lambda i: (0, 0, 0)),
        pl.BlockSpec((CHUNK, D_A), const2),
        pl.BlockSpec((tm, 2 * LANES), lambda i: (i % tiles_per_batch, 0)),
    ]
    return pl.pallas_call(
        functools.partial(_inproj_prompt_kernel, tm=tm, tiles_per_batch=tiles_per_batch),
        grid=(n // tm,),
        in_specs=in_specs,
        out_specs=out_specs,
        out_shape=out_shapes,
        scratch_shapes=[pltpu.VMEM((tm, C_END), F32), pltpu.VMEM((tm, KV_W), F32), pltpu.VMEM((tm, KV_W), F32)],
        compiler_params=pltpu.CompilerParams(
            dimension_semantics=("arbitrary",), vmem_limit_bytes=VMEM_LIMIT_BYTES),
        name="inproj_prompt",
    )(x2, g, *w, lng, lnb, ws_bf, bias_full, rope_tab)


def _inproj_decode_kernel(x_ref, g_ref, wm_ref, wg_ref, wz_ref, lng_ref, lnb_ref, wsv_ref, bsv_ref, rope_ref,
                          aout_ref, qa_ref, kc_ref, vc_ref, ks_ref, vs_ref, kw_ref, vw_ref,
                          gate_ref, szb_ref, vn_ref, z_scr):
    _project(x_ref, g_ref, (wm_ref, wg_ref, wz_ref), z_scr)
    tab = rope_ref[...]
    u = z_scr[:, C_U:C_U + D_A]
    v = z_scr[:, C_V:C_V + D_A]
    za = z_scr[:, C_ZA:C_ZA + D_A]
    vn = _layer_norm_rows(v, lng_ref[...], lnb_ref[...])
    vn_ref[...] = vn
    s = vn * wsv_ref[...] + bsv_ref[...]
    aout_ref[...] = (u * s * jax.nn.silu(za)).astype(BF16)

    q = z_scr[:, C_Q:C_Q + D_B]
    qr = jnp.concatenate(
        [_rope(q[:, j * LANES:(j + 1) * LANES], tab) for j in range(4)], axis=1) * (HEAD_DIM ** -0.5)
    for e, slab in enumerate(_head_slabs(qr)):
        qa_ref[:, e * LANES:(e + 1) * LANES] = slab.astype(BF16)
    kc_ref[...] = z_scr[:, C_KC:C_KC + KV_W]
    vc_ref[...] = z_scr[:, C_VC:C_VC + KV_W]
    ks_ref[...] = _rope(z_scr[:, C_KS:C_KS + KV_W], tab)
    vs_ref[...] = z_scr[:, C_VS:C_VS + KV_W]
    kw_ref[...] = _rope(z_scr[:, C_KW:C_KW + KV_W], tab)
    vw_ref[...] = z_scr[:, C_VW:C_VW + KV_W]
    gate_ref[...] = jax.nn.sigmoid(z_scr[:, C_G:C_G + LANES])
    szb = jax.nn.silu(z_scr[:, C_ZB:C_ZB + D_B])
    for e, slab in enumerate(_head_slabs(szb)):
        szb_ref[:, e * LANES:(e + 1) * LANES] = slab


def _inproj_decode(x2, g, w, lng, lnb, wsv, bsv, rope_tab):
    n = x2.shape[0]
    out_shapes = [
        jax.ShapeDtypeStruct((n, D_A), BF16),
        jax.ShapeDtypeStruct((n, N_HEADS * LANES), BF16),
        jax.ShapeDtypeStruct((n, KV_W), F32),
        jax.ShapeDtypeStruct((n, KV_W), F32),
        jax.ShapeDtypeStruct((n, KV_W), F32),
        jax.ShapeDtypeStruct((n, KV_W), F32),
        jax.ShapeDtypeStruct((n, KV_W), F32),
        jax.ShapeDtypeStruct((n, KV_W), F32),
        jax.ShapeDtypeStruct((n, LANES), F32),
        jax.ShapeDtypeStruct((n, N_HEADS * LANES), F32),
        jax.ShapeDtypeStruct((n, D_A), F32),
    ]
    return pl.pallas_call(
        _inproj_decode_kernel,
        out_shape=out_shapes,
        scratch_shapes=[pltpu.VMEM((n, C_END), F32)],
        compiler_params=pltpu.CompilerParams(vmem_limit_bytes=VMEM_LIMIT_BYTES),
        name="inproj_decode",
    )(x2, g, *w, lng, lnb, wsv, bsv, rope_tab)


def _compress_chunks(ch, pos_term, w_ref):
    ab = _dot(ch.astype(BF16), w_ref[...])
    return ab[:, 0:KV_W] + pltpu.roll(ab[:, KV_W:2 * KV_W], ch.shape[0] - 1, 0) + pos_term


def _position_term(pe_ref, w_ref):
    pe = pe_ref[...]
    hi = pe.astype(BF16)
    r1 = pe - hi.astype(F32)
    mid = r1.astype(BF16)
    lo = (r1 - mid.astype(F32)).astype(BF16)
    pw = _dot(hi, w_ref[...]) + _dot(mid, w_ref[...]) + _dot(lo, w_ref[...])
    return pw[0:1, 0:KV_W] + pw[1:2, KV_W:2 * KV_W]


def _compress_prompt_kernel(kc_ref, vc_ref, wk_ref, wv_ref, pek_ref, pev_ref, rope_ref, kcc_ref, vcct_ref):
    kcc = _compress_chunks(kc_ref[0], _position_term(pek_ref, wk_ref), wk_ref)
    kcc_ref[0] = _rope(kcc, rope_ref[...]).astype(BF16)
    vcc = _compress_chunks(vc_ref[0], _position_term(pev_ref, wv_ref), wv_ref)
    for j in range(vcc.shape[0] // LANES):
        vcct_ref[0, :, j * LANES:(j + 1) * LANES] = vcc[j * LANES:(j + 1) * LANES].T.astype(BF16)


def _compress_prompt(kc3, vc3, wk, wv, pek, pev, rope_tab):
    b, c, width = kc3.shape
    blk = pl.BlockSpec((1, c, width), lambda i: (i, 0, 0))
    const2 = lambda i: (0, 0)
    return pl.pallas_call(
        _compress_prompt_kernel,
        grid=(b,),
        in_specs=[blk, blk,
                  pl.BlockSpec(wk.shape, const2), pl.BlockSpec(wv.shape, const2),
                  pl.BlockSpec(pek.shape, const2), pl.BlockSpec(pev.shape, const2),
                  pl.BlockSpec(rope_tab.shape, const2)],
        out_specs=[pl.BlockSpec((1, c, KV_W), lambda i: (i, 0, 0)),
                   pl.BlockSpec((1, KV_W, c), lambda i: (i, 0, 0))],
        out_shape=[jax.ShapeDtypeStruct((b, c, KV_W), BF16), jax.ShapeDtypeStruct((b, KV_W, c), BF16)],
        compiler_params=pltpu.CompilerParams(
            dimension_semantics=("arbitrary",), vmem_limit_bytes=VMEM_LIMIT_BYTES),
        name="compress_prompt",
    )(kc3, vc3, wk, wv, pek, pev, rope_tab)


def _select_blocks(imp, kk, candidate, forced):
    assert FORCE_SCORE > 2 * GQA
    start = jnp.where(candidate, jnp.where(forced, -jnp.inf, imp), -jnp.inf)
    score = start
    for _ in range(N_SELECT - N_FORCED):
        cm = jnp.max(score, axis=0, keepdims=True)
        first = jnp.min(jnp.where(score == cm, kk, LANES), axis=0, keepdims=True)
        score = jnp.where(kk == first, -jnp.inf, score)
    taken = (score == -jnp.inf) & (start > -jnp.inf)
    return candidate & (forced | taken)


def _softmax_rows(s):
    mx = jnp.max(s, axis=1, keepdims=True)
    ex = jnp.exp(s - mx)
    return ex * (1.0 / jnp.sum(ex, axis=1, keepdims=True))


def _tile_heads(x):
    return jnp.concatenate([x] * N_HEADS, axis=1)


def _values_by_kv_head(vt, p):
    half = GQA * Q_BLOCK
    return jnp.concatenate(
        [_dot(vt[h * HEAD_DIM:(h + 1) * HEAD_DIM], p[:, h * half:(h + 1) * half]) for h in range(N_KV)], axis=1)


def _attn_prompt_kernel(*refs):
    def body(sub, carry):
        _attn_query_block(sub, *refs)
        return carry

    lax.fori_loop(0, Q_SUB, body, 0)


def _attn_query_block(sub, qt_ref, gate_ref, szb_ref, kcc_ref, vcct_ref, ovlt_ref,
                      ksa_ref, vsb_ref, kwb_ref, vwb_ref, out_ref,
                      qaug_scr, acc_scr, sa_scr, sb_scr):
    i_blk = pl.program_id(1) * Q_SUB + sub
    q0 = i_blk * Q_BLOCK
    cols = N_HEADS * Q_BLOCK
    qt = qt_ref[0, sub]
    out_rows = pl.ds(pl.multiple_of(sub * Q_BLOCK, Q_BLOCK), Q_BLOCK)

    n_cmp = kcc_ref.shape[1]

    def compressed(extent):
        t_c = q0 + _lane_iota((extent, Q_BLOCK))
        seen = (_row_iota((extent, Q_BLOCK)) * CMP_STRIDE + (CMP_BLOCK - 1)) <= t_c
        s = _dot(kcc_ref[0, 0:extent, :], qt) + _tile_heads(jnp.where(seen, 0.0, NEG))
        mx = jnp.max(s, axis=0, keepdims=True)
        ex = jnp.exp2(s - mx)
        inv = jnp.where(mx > 0.5 * NEG, 1.0 / jnp.sum(ex, axis=0, keepdims=True), 0.0)
        p_c = ex * inv
        imps = []
        for h in range(N_KV):
            c0 = h * GQA * Q_BLOCK
            p_sum = (p_c[:, c0:c0 + Q_BLOCK] + p_c[:, c0 + Q_BLOCK:c0 + 2 * Q_BLOCK]
                     + p_c[:, c0 + 2 * Q_BLOCK:c0 + 3 * Q_BLOCK] + p_c[:, c0 + 3 * Q_BLOCK:c0 + 4 * Q_BLOCK])
            imps.append(_dot(ovlt_ref[:, 0:extent], p_sum.astype(BF16)))
        return (_values_by_kv_head(vcct_ref[0, :, 0:extent], p_c.astype(BF16)), *imps)

    n_need = (q0 + Q_BLOCK) // CMP_STRIDE
    extents = list(range(LANES, n_cmp + 1, LANES))
    o_c, *imp_t = lax.switch(
        jnp.minimum((n_need - 1) // LANES, len(extents) - 1),
        [functools.partial(compressed, ext) for ext in extents])

    n_wt = WIN_SPAN // LANES
    r_w, l_w = _row_iota((LANES, Q_BLOCK)), _lane_iota((LANES, Q_BLOCK))
    k_tiles, v_tiles = [], []
    for t in range(n_wt):
        j = i_blk - (n_wt - 1) + t
        jc = jnp.maximum(j, 0)
        off = jnp.where((l_w == 0) & (j < 0), 1.0, 0.0).astype(BF16)
        k_tiles.append(jnp.concatenate(
            [kwb_ref[0, pl.ds(pl.multiple_of(jc * LANES, LANES), LANES), :], off], axis=1))
        v_tiles.append(vwb_ref[0, jc])
    vwt = jnp.concatenate(v_tiles, axis=1)
    q_off = jnp.where(_row_iota((LANES, cols)) == 0, NEG, 0.0).astype(BF16)
    sw = _dot(jnp.concatenate(k_tiles, axis=0), jnp.concatenate([qt, q_off], axis=0))
    sw = jnp.concatenate([
        sw[0:LANES] + _tile_heads(jnp.where(r_w > l_w, 0.0, NEG)),
        sw[LANES:WIN_SPAN - LANES],
        sw[WIN_SPAN - LANES:] + _tile_heads(jnp.where(r_w <= l_w, 0.0, NEG))], axis=0)
    exw = jnp.exp2(sw - jnp.max(sw, axis=0, keepdims=True))
    o_w = _values_by_kv_head(vwt, exw.astype(BF16)) * (1.0 / jnp.sum(exw, axis=0, keepdims=True))

    kk = _row_iota((LANES, Q_BLOCK))
    tq = q0 + _lane_iota((LANES, Q_BLOCK))
    valid = kk * SEL_BLOCK <= tq
    cur = tq >> SEL_SHIFT
    forced = (kk == 0) | (kk == cur) | (kk == cur - 1)
    both = lambda a: jnp.concatenate([a] * N_KV, axis=1)
    picked = _select_blocks(jnp.concatenate(imp_t, axis=1), both(kk), both(valid), both(forced))
    qaug_scr[0:LANES, :] = qt
    for h in range(N_KV):
        selneg = jnp.where(picked[:, h * Q_BLOCK:(h + 1) * Q_BLOCK], 0.0, NEG).astype(BF16)
        for g in range(GQA):
            c0 = (h * GQA + g) * Q_BLOCK
            qaug_scr[LANES:2 * LANES, c0:c0 + Q_BLOCK] = selneg

    acc_scr[...] = jnp.zeros(acc_scr.shape, F32)
    tiles_per_step = SEL_TILE // LANES

    def score(step, dst):
        t0 = pl.multiple_of(step * SEL_TILE, SEL_TILE)
        dst[...] = _dot(ksa_ref[0, pl.ds(t0, SEL_TILE), :], qaug_scr[...])

    def attend(src, step, m_prev, l_prev, causal_rows=None):
        rows = SEL_TILE if causal_rows is None else causal_rows
        vt = jnp.concatenate([vsb_ref[0, step * tiles_per_step + i] for i in range(rows // LANES)], axis=1)
        if causal_rows is None:
            sc = src[...]
        else:
            own = src[rows - Q_BLOCK:rows, :] + _tile_heads(jnp.where(r_w <= l_w, 0.0, NEG))
            sc = own if rows == Q_BLOCK else jnp.concatenate([src[0:rows - Q_BLOCK, :], own], axis=0)
        m_next = jnp.maximum(m_prev, jnp.max(sc, axis=0, keepdims=True))
        p = jnp.exp2(sc - m_next)
        alpha = jnp.exp2(m_prev - m_next)
        l_next = alpha * l_prev + jnp.sum(p, axis=0, keepdims=True)
        acc_scr[...] = alpha * acc_scr[...] + _values_by_kv_head(vt, p.astype(BF16))
        return m_next, l_next

    n_full = q0 // SEL_TILE
    score(0, sa_scr)

    def pair(jj, carry):
        m_c, l_c = carry
        score(2 * jj + 1, sb_scr)
        m_c, l_c = attend(sa_scr, 2 * jj, m_c, l_c)
        score(2 * jj + 2, sa_scr)
        return attend(sb_scr, 2 * jj + 1, m_c, l_c)

    m_run, l_run = lax.fori_loop(
        0, n_full // 2, pair,
        (jnp.full((1, cols), -jnp.inf, F32), jnp.zeros((1, cols), F32)))

    def last_step(src, m_c, l_c):
        extents = list(range(Q_BLOCK, SEL_TILE + 1, Q_BLOCK))
        return lax.switch(
            (q0 % SEL_TILE) // Q_BLOCK,
            [lambda m, l, ext=ext: attend(src, n_full, m, l, causal_rows=ext)[1] for ext in extents],
            m_c, l_c)

    def odd_tail(m_c, l_c):
        score(n_full, sb_scr)
        m_c, l_c = attend(sa_scr, n_full - 1, m_c, l_c)
        return last_step(sb_scr, m_c, l_c)

    def even_tail(m_c, l_c):
        return last_step(sa_scr, m_c, l_c)

    l_run = lax.cond(n_full % 2 == 1, odd_tail, even_tail, m_run, l_run)

    gt = gate_ref[0, sub]
    o_s = acc_scr[...] * (1.0 / l_run)
    heads = []
    for e in range(N_HEADS):
        c0 = e * Q_BLOCK
        heads.append(gt[3 * e:3 * e + 1, :] * o_c[:, c0:c0 + Q_BLOCK]
                     + gt[3 * e + 1:3 * e + 2, :] * o_s[:, c0:c0 + Q_BLOCK]
                     + gt[3 * e + 2:3 * e + 3, :] * o_w[:, c0:c0 + Q_BLOCK])
    for j in range(N_HEADS // 2):
        slab = jnp.concatenate([heads[2 * j], heads[2 * j + 1]], axis=0).T
        out_ref[0, out_rows, j * LANES:(j + 1) * LANES] = (
            slab * szb_ref[0, out_rows, j * LANES:(j + 1) * LANES]).astype(BF16)


def _attn_prompt(qt4, gate4, szb3, kcc, vcct, ovlt, ksa3, vsb4, kwb3, vwb4):
    b, nq = qt4.shape[0], qt4.shape[1]
    t = nq * Q_BLOCK
    cols = N_HEADS * Q_BLOCK
    assert nq % Q_SUB == 0
    qtile = lambda a: pl.BlockSpec((1, Q_SUB) + a.shape[2:], lambda bi, i: (bi, i, 0, 0))
    rows = pl.BlockSpec((1, Q_SUB * Q_BLOCK, D_B), lambda bi, i: (bi, i, 0))
    seq = lambda a: pl.BlockSpec((1,) + a.shape[1:], lambda bi, i, nd=a.ndim: (bi,) + (0,) * (nd - 1))
    return pl.pallas_call(
        _attn_prompt_kernel,
        grid=(b, nq // Q_SUB),
        in_specs=[qtile(qt4), qtile(gate4), rows,
                  seq(kcc), seq(vcct), pl.BlockSpec(ovlt.shape, lambda bi, i: (0, 0)),
                  seq(ksa3), seq(vsb4), seq(kwb3), seq(vwb4)],
        out_specs=rows,
        out_shape=jax.ShapeDtypeStruct((b, t, D_B), BF16),
        scratch_shapes=[pltpu.VMEM((2 * LANES, cols), BF16),
                        pltpu.VMEM((HEAD_DIM, cols), F32),
                        pltpu.VMEM((SEL_TILE, cols), F32),
                        pltpu.VMEM((SEL_TILE, cols), F32)],
        compiler_params=pltpu.CompilerParams(
            dimension_semantics=("arbitrary", "arbitrary"), vmem_limit_bytes=VMEM_LIMIT_BYTES),
        name="attn_prompt",
    )(qt4, gate4, szb3, kcc, vcct, ovlt, ksa3, vsb4, kwb3, vwb4)


def _outproj_kernel(x_ref, a_ref, m_ref, wa_ref, wb_ref, fg_ref, y_ref):
    y = x_ref[...] + _dot(a_ref[...], wa_ref[...]) + _dot(m_ref[...].astype(BF16), wb_ref[...])
    y_ref[...] = _rms_rows(y, fg_ref[...])


def _outproj(x2, a_out, mix_b, wa, wb, fg, tm):
    n = x2.shape[0]
    row = lambda i: (i, 0)
    const2 = lambda i: (0, 0)
    return pl.pallas_call(
        _outproj_kernel,
        grid=(n // tm,),
        in_specs=[pl.BlockSpec((tm, D_MODEL), row),
                  pl.BlockSpec((tm, a_out.shape[1]), row),
                  pl.BlockSpec((tm, mix_b.shape[1]), row),
                  pl.BlockSpec(wa.shape, const2), pl.BlockSpec(wb.shape, const2),
                  pl.BlockSpec((1, D_MODEL), const2)],
        out_specs=pl.BlockSpec((tm, D_MODEL), row),
        out_shape=jax.ShapeDtypeStruct((n, D_MODEL), F32),
        compiler_params=pltpu.CompilerParams(
            dimension_semantics=("arbitrary",), vmem_limit_bytes=VMEM_LIMIT_BYTES),
        name="outproj",
    )(x2, a_out, mix_b, wa, wb, fg)


def _page_copies(pt_ref, pool_ref, buf_ref, sem_ref, step, slot):
    group, n_pages = buf_ref.shape[1], buf_ref.shape[2]
    return [pltpu.make_async_copy(pool_ref.at[pt_ref[step * group + g, p]], buf_ref.at[slot, g, p],
                                  sem_ref.at[slot])
            for g in range(group) for p in range(n_pages)]


def _gather_pages(pt_ref, pools, bufs, sems):
    b = pl.program_id(0)
    slot = lax.rem(b, 2)

    def start(step, sl):
        for pool, buf, sem in zip(pools, bufs, sems):
            for cp in _page_copies(pt_ref, pool, buf, sem, step, sl):
                cp.start()

    @pl.when(b == 0)
    def _():
        start(0, 0)

    @pl.when(b + 1 < pl.num_programs(0))
    def _():
        start(b + 1, 1 - slot)

    for pool, buf, sem in zip(pools, bufs, sems):
        for cp in _page_copies(pt_ref, pool, buf, sem, b, slot):
            cp.wait()
    return slot


def _decode_cmp_kernel(pt_ref, ck_hbm, cv_hbm, q_ref, wk_ref, wv_ref, pek_ref, pev_ref, rope_ref, ovl_ref,
                       oc_ref, psum_ref, kbuf, vbuf, ksem, vsem, ktok_scr, vtok_scr, kpos_scr, vpos_scr):
    slot = _gather_pages(pt_ref, (ck_hbm, cv_hbm), (kbuf, vbuf), (ksem, vsem))
    group, n_pages = kbuf.shape[1], kbuf.shape[2]

    @pl.when(pl.program_id(0) == 0)
    def _():
        kpos_scr[...] = _position_term(pek_ref, wk_ref)
        vpos_scr[...] = _position_term(pev_ref, wv_ref)

    def chunk_rows(buf, tok_scr, g):
        for p in range(n_pages):
            tok_scr[g, p * PAGE_SIZE:(p + 1) * PAGE_SIZE, :] = buf[slot, g, p].T
        n = tok_scr.shape[1] // CMP_STRIDE
        return jnp.concatenate(
            [tok_scr[g, pl.ds(r, n, stride=CMP_STRIDE), :] for r in range(CMP_STRIDE)], axis=1)

    for g in range(group):
        kch = chunk_rows(kbuf, ktok_scr, g)
        vch = chunk_rows(vbuf, vtok_scr, g)
        kcc = _rope(_compress_chunks(kch, kpos_scr[...], wk_ref), rope_ref[...]).astype(BF16)
        vcc = _compress_chunks(vch, vpos_scr[...], wv_ref).astype(BF16)
        n_chunks = kch.shape[0]
        q = q_ref[g]
        rows = q.shape[0]
        s = _dot_nt(q, kcc)
        seen = _lane_iota((rows, n_chunks)) < n_chunks - 1
        p_c = jnp.where(seen, _softmax_rows(jnp.where(seen, s, NEG)), 0.0)
        oc_ref[g] = _dot(p_c.astype(BF16), vcc)[0:N_HEADS]
        ps0 = jnp.sum(p_c[0:GQA], axis=0, keepdims=True)
        ps1 = jnp.sum(p_c[GQA:2 * GQA], axis=0, keepdims=True)
        psum = jnp.where(_row_iota((rows, n_chunks)) < N_HEADS // 2, ps0, ps1)
        psum_ref[g] = _dot(psum.astype(BF16), ovl_ref[...])[0:N_HEADS]


def _decode_cmp(page_table, ck_pool, cv_pool, q16, wk, wv, pek, pev, rope_tab, ovl):
    nb, n_pages = page_table.shape
    group = DECODE_GROUP
    assert nb % group == 0
    hbm = pl.BlockSpec(memory_space=pl.ANY)
    page_buf = pltpu.VMEM((2, group, n_pages, KV_W, PAGE_SIZE), F32)
    tok_buf = pltpu.VMEM((group, n_pages * PAGE_SIZE, KV_W), F32)
    const2 = lambda b, pt: (0, 0)
    per_b = lambda a: pl.BlockSpec((group,) + a.shape[1:], lambda b, pt: (b, 0, 0))
    in_specs = [hbm, hbm, per_b(q16),
                pl.BlockSpec(wk.shape, const2), pl.BlockSpec(wv.shape, const2),
                pl.BlockSpec(pek.shape, const2), pl.BlockSpec(pev.shape, const2),
                pl.BlockSpec(rope_tab.shape, const2), pl.BlockSpec(ovl.shape, const2)]
    out_blk = pl.BlockSpec((group, N_HEADS, LANES), lambda b, pt: (b, 0, 0))
    grid_spec = pltpu.PrefetchScalarGridSpec(
        num_scalar_prefetch=1, grid=(nb // group,), in_specs=in_specs, out_specs=[out_blk, out_blk],
        scratch_shapes=[page_buf, page_buf, pltpu.SemaphoreType.DMA((2,)), pltpu.SemaphoreType.DMA((2,)),
                        tok_buf, tok_buf, pltpu.VMEM((1, KV_W), F32), pltpu.VMEM((1, KV_W), F32)])
    return pl.pallas_call(
        _decode_cmp_kernel,
        grid_spec=grid_spec,
        out_shape=[jax.ShapeDtypeStruct((nb, N_HEADS, LANES), F32)] * 2,
        compiler_params=pltpu.CompilerParams(
            dimension_semantics=("arbitrary",), vmem_limit_bytes=VMEM_LIMIT_BYTES),
        name="decode_cmp",
    )(page_table, ck_pool, cv_pool, q16, wk, wv, pek, pev, rope_tab, ovl)


def _decode_topk_kernel(imp_ref, sel_ref, *, n_blk, cur):
    n = imp_ref.shape[0]
    tiles = [imp_ref[i * LANES:(i + 1) * LANES, :].T for i in range(n // LANES)]
    imp_t = jnp.concatenate(tiles, axis=1)
    kk = _row_iota(imp_t.shape)
    forced = (kk == 0) | (kk == cur) | (kk == cur - 1)
    selneg = jnp.where(_select_blocks(imp_t, kk, kk < n_blk, forced), 0.0, NEG)
    for i in range(n // LANES):
        sel_ref[i * LANES:(i + 1) * LANES, :] = selneg[:, i * LANES:(i + 1) * LANES].T


def _decode_topk(imp2, n_blk, cur):
    return pl.pallas_call(
        functools.partial(_decode_topk_kernel, n_blk=n_blk, cur=cur),
        out_shape=jax.ShapeDtypeStruct(imp2.shape, F32),
        compiler_params=pltpu.CompilerParams(vmem_limit_bytes=VMEM_LIMIT_BYTES),
        name="decode_topk",
    )(imp2)


def _decode_attend(s, q32, vals_t_bf, k_new, v_new, bias_new):
    kn = k_new.astype(BF16).astype(F32)
    s_new = jnp.sum(q32 * kn, axis=1, keepdims=True) + bias_new
    mx = jnp.maximum(jnp.max(s, axis=1, keepdims=True), s_new)
    ex = jnp.exp(s - mx)
    ex_new = jnp.exp(s_new - mx)
    inv = 1.0 / (jnp.sum(ex, axis=1, keepdims=True) + ex_new)
    p = (ex * inv).astype(BF16)
    p_new = (ex_new * inv).astype(BF16).astype(F32)
    return _dot_nt(p, vals_t_bf) + p_new * v_new.astype(BF16).astype(F32)


P_Q, P_SEL, P_OC, P_SZB, P_NEW, P_GATE, P_ROWS = 0, 16, 32, 40, 48, 56, 64


def _row_to_col(row):
    n = row.shape[1]
    diag = _row_iota((n, n)) == _lane_iota((n, n))
    return jnp.sum(jnp.where(diag, row, 0.0), axis=1, keepdims=True)


def _decode_sel_kernel(pt_ref, ks_hbm, vs_hbm, ckw_ref, cvw_ref, pack_ref, onehot_ref,
                       mix_ref, okw_ref, ovw_ref, kbuf, vbuf, ksem, vsem, *, cur):
    slot = _gather_pages(pt_ref, (ks_hbm, vs_hbm), (kbuf, vbuf), (ksem, vsem))
    group, n_pages = kbuf.shape[1], kbuf.shape[2]
    for g in range(group):
        pk = pack_ref[g]
        q32 = pk[P_Q:P_Q + 2 * N_HEADS]
        q = q32.astype(BF16)
        sel = pk[P_SEL:P_SEL + 2 * N_HEADS].astype(BF16)
        rows = q.shape[0]
        ks_new, vs_new = pk[P_NEW:P_NEW + 1], pk[P_NEW + 1:P_NEW + 2]
        kw_new, vw_new = pk[P_NEW + 2:P_NEW + 3], pk[P_NEW + 3:P_NEW + 4]

        kt = jnp.concatenate([kbuf[slot, g, p] for p in range(n_pages)], axis=1).astype(BF16)
        vt = jnp.concatenate([vbuf[slot, g, p] for p in range(n_pages)], axis=1).astype(BF16)
        kaug = jnp.concatenate([kt, onehot_ref[...]], axis=0)
        qaug = jnp.concatenate([q, sel], axis=1)
        s_sel = _dot(qaug, kaug)
        o_s = _decode_attend(s_sel, q32, vt, ks_new, vs_new, sel.astype(F32)[:, cur:cur + 1])

        kwin = ckw_ref[g]
        vwin = cvw_ref[g]
        keep = kwin.shape[1]
        visible = _lane_iota((rows, keep)) > keep - WINDOW
        s_win = jnp.where(visible, _dot(q, kwin.astype(BF16)), NEG)
        o_w = _decode_attend(s_win, q32, vwin.astype(BF16), kw_new, vw_new, 0.0)

        gates = pk[P_GATE:P_GATE + N_HEADS]
        o = (gates[:, 0:1] * pk[P_OC:P_OC + N_HEADS] + gates[:, 1:2] * o_s[0:N_HEADS]
             + gates[:, 2:3] * o_w[0:N_HEADS])
        mix_ref[g] = o * pk[P_SZB:P_SZB + N_HEADS]

        last = _lane_iota((KV_W, keep)) == keep - 1
        okw_ref[g] = jnp.where(last, _row_to_col(kw_new), pltpu.roll(kwin, keep - 1, 1))
        ovw_ref[g] = jnp.where(last, _row_to_col(vw_new), pltpu.roll(vwin, keep - 1, 1))


def _decode_sel(page_table, ks_pool, vs_pool, ckw, cvw, pack, onehot, cur):
    nb, n_pages = page_table.shape
    group = DECODE_GROUP
    assert nb % group == 0
    keep = ckw.shape[2]
    hbm = pl.BlockSpec(memory_space=pl.ANY)
    page_buf = pltpu.VMEM((2, group, n_pages, KV_W, PAGE_SIZE), F32)
    per_b = lambda a: pl.BlockSpec((group,) + a.shape[1:], lambda b, pt: (b, 0, 0))
    in_specs = [hbm, hbm, per_b(ckw), per_b(cvw), per_b(pack),
                pl.BlockSpec(onehot.shape, lambda b, pt: (0, 0))]
    out_specs = [pl.BlockSpec((group, N_HEADS, LANES), lambda b, pt: (b, 0, 0)),
                 pl.BlockSpec((group, KV_W, keep), lambda b, pt: (b, 0, 0)),
                 pl.BlockSpec((group, KV_W, keep), lambda b, pt: (b, 0, 0))]
    grid_spec = pltpu.PrefetchScalarGridSpec(
        num_scalar_prefetch=1, grid=(nb // group,), in_specs=in_specs, out_specs=out_specs,
        scratch_shapes=[page_buf, page_buf, pltpu.SemaphoreType.DMA((2,)), pltpu.SemaphoreType.DMA((2,))])
    return pl.pallas_call(
        functools.partial(_decode_sel_kernel, cur=cur),
        grid_spec=grid_spec,
        out_shape=[jax.ShapeDtypeStruct((nb, N_HEADS, LANES), F32),
                   jax.ShapeDtypeStruct((nb, KV_W, keep), F32),
                   jax.ShapeDtypeStruct((nb, KV_W, keep), F32)],
        compiler_params=pltpu.CompilerParams(
            dimension_semantics=("arbitrary",), vmem_limit_bytes=VMEM_LIMIT_BYTES),
        name="decode_sel",
    )(page_table, ks_pool, vs_pool, ckw, cvw, pack, onehot)


def _rope_table(pos):
    half = ROT_DIM // 2
    inv = jnp.power(jnp.float32(ROPE_THETA), -jnp.arange(half, dtype=F32) / half)
    ang = pos.astype(F32)[:, None] * inv[None, :]
    cs = jnp.concatenate([jnp.cos(ang), jnp.sin(ang)], axis=1)
    top16 = lambda a: lax.bitcast_convert_type(
        lax.bitcast_convert_type(a, jnp.uint32) & jnp.uint32(0xFFFF0000), F32)
    hi = top16(cs)
    mid = top16(cs - hi)
    pieces = jnp.concatenate([hi, mid, cs - hi - mid], axis=1)
    lane = jnp.arange(2 * LANES)
    seg, l64 = lane // LANES, lane % HEAD_DIM
    src = jnp.where(seg == 0, l64 % half, half + l64 % half)
    sign = jnp.where(seg == 0, l64 < ROT_DIM, l64 < half).astype(F32)
    sign = jnp.where(seg == 1, -sign, sign)
    place = (jnp.arange(ROT_DIM)[:, None] == src[None, :]).astype(F32) * sign[None, :]
    ones = ((seg == 0) & (l64 >= ROT_DIM)).astype(F32)
    return jnp.dot(pieces, jnp.concatenate([place] * 3, axis=0)) + ones[None, :]


def _compress_weights(w_c, pe_c):
    w4 = w_c.reshape(2, CMP_STRIDE, HEAD_DIM, HEAD_DIM).astype(BF16)
    zero = jnp.zeros_like(w4[0])
    wb = jnp.stack([jnp.concatenate([w4[a] if g == h else zero for a in range(2) for g in range(N_KV)], axis=-1)
                    for h in range(N_KV)], axis=1)
    wb = wb.reshape(CMP_STRIDE * KV_W, 2 * KV_W)
    pe = jnp.broadcast_to(pe_c.reshape(2, CMP_STRIDE, 1, HEAD_DIM), (2, CMP_STRIDE, N_KV, HEAD_DIM))
    return wb, jnp.pad(pe.reshape(2, CMP_STRIDE * KV_W), ((0, 14), (0, 0)))


def _overlap_matrix(n_rows, n_blk):
    cs = jnp.arange(n_rows, dtype=jnp.int32)[:, None] * CMP_STRIDE
    bs = jnp.arange(LANES, dtype=jnp.int32)[None, :] * SEL_BLOCK
    hit = (cs < bs + SEL_BLOCK) & (cs + CMP_BLOCK > bs) & (jnp.arange(LANES)[None, :] < n_blk)
    return hit.astype(BF16)


def kernel(x_prompt, x_sample, cache_k_cmp, cache_v_cmp, cache_k_sel, cache_v_sel, cache_k_win,
           cache_v_win, page_table, norm_g, w_in, ln_v_g, ln_v_b, w_s, b_s, w_ck, pe_ck, w_cv,
           pe_cv, w_out, final_g):
    depth = norm_g.shape[0]
    assert depth == 1
    bsz, seq, _ = x_prompt.shape
    nb, dec_seq, _ = x_sample.shape
    assert dec_seq == 1 and seq % SEL_TILE == 0 and seq >= WIN_SPAN and (seq // CMP_STRIDE) % LANES == 0
    n_pages = page_table.shape[1]
    past = n_pages * PAGE_SIZE
    keep = cache_k_win.shape[2]
    assert keep == WINDOW and past % SEL_BLOCK == 0

    w = w_in[0]
    w_pad = (w[:, :C_G].astype(BF16),
             jnp.pad(w[:, C_G:C_G + N_GATE_COLS], ((0, 0), (0, LANES - N_GATE_COLS))).astype(BF16),
             w[:, C_G + N_GATE_COLS:].astype(BF16))
    g_row = norm_g[0][None, :]
    lng = ln_v_g[0][None, :]
    lnb = ln_v_b[0][None, :]
    fg = final_g[None, :]
    wk_c, pek = _compress_weights(w_ck[0], pe_ck[0])
    wv_c, pev = _compress_weights(w_cv[0], pe_cv[0])
    w_o = w_out[0].astype(BF16)
    wo_a, wo_b = w_o[:D_A], w_o[D_A:]

    tril = jnp.tril(jnp.ones((CHUNK, CHUNK), w_s.dtype))
    ws_bf = (w_s[0] * tril).astype(BF16)
    bias_full = jnp.repeat(b_s[0].T, HEAD_DIM, axis=1)
    xp2 = x_prompt.reshape(bsz * seq, D_MODEL)
    (a_out, qt, kc, vc, kc_t, vc_t, ks_t, vs_t, kw_t, vw_t, ksa, kwb, vsb, vwb, gates_t, szb) = _inproj_prompt(
        xp2, g_row, w_pad, lng, lnb, ws_bf, bias_full,
        _rope_table(jnp.arange(seq, dtype=jnp.int32)), seq)
    n_chunks = seq // CMP_STRIDE
    kcc, vcc_t = _compress_prompt(
        kc.reshape(bsz, n_chunks, CMP_STRIDE * KV_W), vc.reshape(bsz, n_chunks, CMP_STRIDE * KV_W),
        wk_c, wv_c, pek, pev,
        _rope_table(jnp.arange(n_chunks, dtype=jnp.int32) * CMP_STRIDE))
    b3 = lambda a: a.reshape(bsz, seq, a.shape[-1])
    mix_p = _attn_prompt(qt, gates_t, b3(szb), kcc, vcc_t,
                         _overlap_matrix(n_chunks, seq // SEL_BLOCK).T,
                         b3(ksa), vsb, b3(kwb), vwb)
    y_prompt = _outproj(xp2, a_out, mix_p.reshape(bsz * seq, D_B), wo_a, wo_b, fg, OUT_ROW_TILE)
    y_prompt = y_prompt.reshape(bsz, seq, D_MODEL)
    st = lambda a: jnp.transpose(a.reshape(a.shape[0], N_KV, HEAD_DIM, a.shape[2]), (0, 3, 1, 2))[None]
    keep_p = min(WINDOW, seq)

    xs2 = x_sample.reshape(nb, D_MODEL)
    wsv = jnp.repeat(w_s[0][:, 0, 0], HEAD_DIM)[None, :]
    bsv = jnp.repeat(b_s[0][:, 0], HEAD_DIM)[None, :]
    (a_s, qa_s, kc_s, vc_s, ks_s, vs_s, kw_s, vw_s, gates_s, szb_s, vn_s) = _inproj_decode(
        xs2, g_row, w_pad, lng, lnb, wsv, bsv,
        _rope_table(jnp.full((nb,), past, dtype=jnp.int32)))
    q16 = jnp.pad(qa_s.reshape(nb, N_HEADS, LANES), ((0, 0), (0, N_HEADS), (0, 0)))
    d_chunks = past // CMP_STRIDE
    n_blk = -(-(past + 1) // SEL_BLOCK)
    cur = past // SEL_BLOCK
    kv_t = lambda c: jnp.transpose(c[0], (0, 2, 3, 1)).reshape(c.shape[1], KV_W, c.shape[2])
    oc_s, imp_s = _decode_cmp(
        page_table, kv_t(cache_k_cmp), kv_t(cache_v_cmp),
        q16, wk_c, wv_c, pek, pev,
        _rope_table(jnp.arange(d_chunks, dtype=jnp.int32) * CMP_STRIDE),
        _overlap_matrix(d_chunks, n_blk))
    selneg = _decode_topk(imp_s[:, ::GQA, :].reshape(nb * N_KV, LANES), n_blk, cur)
    sel16 = jnp.pad(jnp.repeat(selneg.reshape(nb, N_KV, LANES), GQA, axis=1), ((0, 0), (0, N_HEADS), (0, 0)))
    tok_blk = jnp.arange(past, dtype=jnp.int32)[None, :] // SEL_BLOCK
    onehot = (tok_blk == jnp.arange(LANES, dtype=jnp.int32)[:, None]).astype(BF16)
    gate_rows = jnp.pad(gates_s[:, :N_GATE_COLS].reshape(nb, N_HEADS, 3), ((0, 0), (0, 0), (0, LANES - 3)))
    new_rows = jnp.pad(jnp.stack([ks_s, vs_s, kw_s, vw_s], axis=1), ((0, 0), (0, P_GATE - P_NEW - 4), (0, 0)))
    pack = jnp.concatenate(
        [q16.astype(F32), sel16, oc_s, szb_s.reshape(nb, N_HEADS, LANES), new_rows, gate_rows], axis=1)
    assert pack.shape[1] == P_ROWS
    mix_s, okw, ovw = _decode_sel(
        page_table, kv_t(cache_k_sel), kv_t(cache_v_sel), kv_t(cache_k_win), kv_t(cache_v_win),
        pack, onehot, cur)
    wo_slab = jnp.zeros((N_HEADS, N_KV, HEAD_DIM, D_MODEL), BF16)
    wo_heads = wo_b.reshape(N_HEADS, HEAD_DIM, D_MODEL)
    for e in range(N_HEADS):
        wo_slab = wo_slab.at[e, e // GQA].set(wo_heads[e])
    y_sample = _outproj(xs2, a_s, mix_s.reshape(nb, N_HEADS * LANES), wo_a,
                        wo_slab.reshape(N_HEADS * LANES, D_MODEL), fg, nb)
    y_sample = y_sample.reshape(nb, 1, D_MODEL)
    ss = lambda a: a.reshape(1, nb, 1, N_KV, HEAD_DIM)

    return (y_prompt, y_sample,
            st(kc_t), st(vc_t), st(ks_t), st(vs_t),
            st(kw_t[:, :, seq - keep_p:]), st(vw_t[:, :, seq - keep_p:]),
            ss(kc_s), ss(vc_s), ss(ks_s), ss(vs_s),
            st(okw), st(ovw),
            vn_s.reshape(1, nb, 1, D_A))
```

```python
import functools

import jax
import jax.numpy as jnp
from jax import lax
from jax.experimental import pallas as pl
from jax.experimental.pallas import tpu as pltpu

F32 = jnp.float32
BF16 = jnp.bfloat16

D_MODEL = 1024
HEAD_DIM = 64
D_A = 512
D_B = 512
A_GROUPS = 8
CHUNK = 128
N_HEADS = 8
N_KV = 2
GQA = 4
KV_W = 128
ROT_DIM = 16
ROPE_THETA = 500000.0
CMP_BLOCK = 32
CMP_STRIDE = 16
SEL_BLOCK = 64
SEL_SHIFT = 6
N_SELECT = 16
WINDOW = 512
Q_BLOCK = 128
PAGE_SIZE = 128
NORM_EPS = 1e-6
FORCE_SCORE = 1e4
N_FORCED = 3
NEG = -1e30
LOG2_E = 1.4426950408889634

LANES = 128
VMEM_LIMIT_BYTES = 56 * 1024 * 1024

C_U, C_V, C_ZA, C_Q = 0, 512, 1024, 1536
C_KC, C_VC, C_KS, C_VS, C_KW, C_VW = 2048, 2176, 2304, 2432, 2560, 2688
C_G, C_ZB, C_END = 2816, 2944, 3456
N_GATE_COLS = 3 * N_HEADS

ROW_TILE = 512
OUT_ROW_TILE = 1024
Q_SUB = 4
SEL_TILE = 512
DECODE_GROUP = 4
WIN_SPAN = WINDOW + Q_BLOCK

_NT = (((1,), (1,)), ((), ()))


def _dot(a, b):
    return jnp.dot(a, b, preferred_element_type=F32)


def _dot_nt(a, b):
    return lax.dot_general(a, b, _NT, preferred_element_type=F32)


def _lane_iota(shape):
    return lax.broadcasted_iota(jnp.int32, shape, len(shape) - 1)


def _row_iota(shape):
    return lax.broadcasted_iota(jnp.int32, shape, len(shape) - 2)


def _rope(x, tab):
    c = tab[:, 0:LANES]
    s1 = tab[:, LANES:2 * LANES]
    return x * c + pltpu.roll(x, LANES - ROT_DIM // 2, 1) * s1 - pltpu.roll(x * s1, ROT_DIM // 2, 1)


def _rms_rows(x, g):
    ms = jnp.mean(x * x, axis=-1, keepdims=True)
    return x * lax.rsqrt(ms + NORM_EPS) * g


def _layer_norm_rows(v, g, b):
    mu = jnp.mean(v, axis=-1, keepdims=True)
    vc = v - mu
    var = jnp.mean(vc * vc, axis=-1, keepdims=True)
    return vc * lax.rsqrt(var + NORM_EPS) * g + b


def _head_slabs(x512):
    out = []
    rows = x512.shape[0]
    lane = _lane_iota((rows, LANES))
    lo = lane < HEAD_DIM
    for j in range(4):
        slab = x512[:, j * LANES:(j + 1) * LANES]
        swapped = pltpu.roll(slab, HEAD_DIM, 1)
        if j < 2:
            out.append(jnp.where(lo, slab, 0.0))
            out.append(jnp.where(lo, swapped, 0.0))
        else:
            out.append(jnp.where(lo, 0.0, swapped))
            out.append(jnp.where(lo, 0.0, slab))
    return out


def _project(x_ref, g_ref, w_refs, z_scr):
    hb = _rms_rows(x_ref[...], g_ref[...]).astype(BF16)
    wm_ref, wg_ref, wz_ref = w_refs
    z_scr[:, 0:C_G] = _dot(hb, wm_ref[...])
    z_scr[:, C_G:C_ZB] = _dot(hb, wg_ref[...])
    z_scr[:, C_ZB:C_END] = _dot(hb, wz_ref[...])


def _inproj_prompt_kernel(x_ref, g_ref, wm_ref, wg_ref, wz_ref, lng_ref, lnb_ref, ws_ref, bias_ref, rope_ref,
                          aout_ref, qt_ref, kc_ref, vc_ref,
                          kct_ref, vct_ref, kst_ref, vst_ref, kwt_ref, vwt_ref,
                          ksa_ref, kwb_ref, vsb_ref, vwb_ref, gate_ref, szb_ref,
                          z_scr, kstage_scr, vstage_scr, *, tm, tiles_per_batch):
    _project(x_ref, g_ref, (wm_ref, wg_ref, wz_ref), z_scr)
    tab = rope_ref[...]

    lane = _lane_iota((CHUNK, LANES))
    lo = lane < HEAD_DIM
    for c in range(tm // CHUNK):
        r0 = c * CHUNK
        u = z_scr[r0:r0 + CHUNK, C_U:C_U + D_A]
        v = z_scr[r0:r0 + CHUNK, C_V:C_V + D_A]
        za = z_scr[r0:r0 + CHUNK, C_ZA:C_ZA + D_A]
        vn = _layer_norm_rows(v, lng_ref[...], lnb_ref[...])
        parts = []
        for p in range(A_GROUPS // 2):
            vp = vn[:, p * LANES:(p + 1) * LANES]
            v_lo = jnp.where(lo, vp, 0.0).astype(BF16)
            v_hi = jnp.where(lo, 0.0, vp).astype(BF16)
            parts.append(_dot(ws_ref[2 * p], v_lo) + _dot(ws_ref[2 * p + 1], v_hi))
        s = jnp.concatenate(parts, axis=1) + bias_ref[...]
        aout_ref[r0:r0 + CHUNK, :] = (u * s * jax.nn.silu(za)).astype(BF16)

    q = z_scr[:, C_Q:C_Q + D_B]
    qr = jnp.concatenate(
        [_rope(q[:, j * LANES:(j + 1) * LANES], tab) for j in range(4)], axis=1) * (HEAD_DIM ** -0.5 * LOG2_E)
    slabs = _head_slabs(qr)
    kc = z_scr[:, C_KC:C_KC + KV_W]
    vc = z_scr[:, C_VC:C_VC + KV_W]
    for src, stage, dst in ((kc, kstage_scr, kc_ref), (vc, vstage_scr, vc_ref)):
        stage[...] = src
        for r in range(CMP_STRIDE):
            dst[:, r * KV_W:(r + 1) * KV_W] = stage[pl.ds(r, tm // CMP_STRIDE, stride=CMP_STRIDE), :]
    ks = _rope(z_scr[:, C_KS:C_KS + KV_W], tab)
    vs = z_scr[:, C_VS:C_VS + KV_W]
    kw = _rope(z_scr[:, C_KW:C_KW + KV_W], tab)
    vw = z_scr[:, C_VW:C_VW + KV_W]
    gates = jax.nn.sigmoid(z_scr[:, C_G:C_G + LANES])
    pos = (pl.program_id(0) % tiles_per_batch) * tm + _row_iota((tm, LANES))
    onehot = jnp.where((pos >> SEL_SHIFT) == _lane_iota((tm, LANES)), 1.0, 0.0)
    ksa_ref[:, 0:LANES] = ks.astype(BF16)
    ksa_ref[:, LANES:2 * LANES] = onehot.astype(BF16)
    kwb_ref[...] = kw.astype(BF16)
    for j in range(tm // LANES):
        r0, r1 = j * LANES, (j + 1) * LANES
        kct_ref[0, :, r0:r1] = kc[r0:r1].T
        vct_ref[0, :, r0:r1] = vc[r0:r1].T
        kst_ref[0, :, r0:r1] = ks[r0:r1].T
        kwt_ref[0, :, r0:r1] = kw[r0:r1].T
        vs_t, vw_t = vs[r0:r1].T, vw[r0:r1].T
        vst_ref[0, :, r0:r1] = vs_t
        vwt_ref[0, :, r0:r1] = vw_t
        vsb_ref[0, j] = vs_t.astype(BF16)
        vwb_ref[0, j] = vw_t.astype(BF16)
        for e in range(N_HEADS):
            qt_ref[0, j, :, e * Q_BLOCK:(e + 1) * Q_BLOCK] = slabs[e][r0:r1].T.astype(BF16)
        gate_ref[0, j] = gates[r0:r1].T
    szb_ref[...] = jax.nn.silu(z_scr[:, C_ZB:C_ZB + D_B])


def _inproj_prompt(x2, g, w, lng, lnb, ws_bf, bias_full, rope_tab, seq, tm=ROW_TILE):
    n = x2.shape[0]
    tiles_per_batch = seq // tm
    bsz = n // seq
    n_tok_tiles = seq // LANES
    row = lambda i: (i, 0)
    const2 = lambda i: (0, 0)
    rows2d = lambda width, dtype: (jax.ShapeDtypeStruct((n, width), dtype), pl.BlockSpec((tm, width), row))
    kv_t = (jax.ShapeDtypeStruct((bsz, KV_W, seq), F32),
            pl.BlockSpec((1, KV_W, tm), lambda i: (i // tiles_per_batch, 0, i % tiles_per_batch)))
    chunk_rows = (jax.ShapeDtypeStruct((n // CMP_STRIDE, CMP_STRIDE * KV_W), F32),
                  pl.BlockSpec((tm // CMP_STRIDE, CMP_STRIDE * KV_W), row))
    tiles = lambda r, c, dtype: (
        jax.ShapeDtypeStruct((bsz, n_tok_tiles, r, c), dtype),
        pl.BlockSpec((1, tm // LANES, r, c), lambda i: (i // tiles_per_batch, i % tiles_per_batch, 0, 0)))
    outs = [
        rows2d(D_A, BF16),
        tiles(KV_W, N_HEADS * Q_BLOCK, BF16),
        chunk_rows, chunk_rows,
        kv_t, kv_t, kv_t, kv_t, kv_t, kv_t,
        rows2d(2 * KV_W, BF16),
        rows2d(KV_W, BF16),
        tiles(KV_W, LANES, BF16),
        tiles(KV_W, LANES, BF16),
        tiles(LANES, LANES, F32),
        rows2d(D_B, F32),
    ]
    out_shapes = [o[0] for o in outs]
    out_specs = [o[1] for o in outs]
    in_specs = [
        pl.BlockSpec((tm, D_MODEL), row),
        pl.BlockSpec((1, D_MODEL), const2),
        pl.BlockSpec((D_MODEL, C_G), const2),
        pl.BlockSpec((D_MODEL, C_ZB - C_G), const2),
        pl.BlockSpec((D_MODEL, C_END - C_ZB), const2),
        pl.BlockSpec((1, D_A), const2),
        pl.BlockSpec((1, D_A), const2),
        pl.BlockSpec((A_GROUPS, CHUNK, CHUNK), lambda i: (0, 0, 0)),
        pl.BlockSpec((CHUNK, D_A), const2),
        pl.BlockSpec((tm, 2 * LANES), lambda i: (i % tiles_per_batch, 0)),
    ]
    return pl.pallas_call(
        functools.partial(_inproj_prompt_kernel, tm=tm, tiles_per_batch=tiles_per_batch),
        grid=(n // tm,),
        in_specs=in_specs,
        out_specs=out_specs,
        out_shape=out_shapes,
        scratch_shapes=[pltpu.VMEM((tm, C_END), F32), pltpu.VMEM((tm, KV_W), F32), pltpu.VMEM((tm, KV_W), F32)],
        compiler_params=pltpu.CompilerParams(
            dimension_semantics=("arbitrary",), vmem_limit_bytes=VMEM_LIMIT_BYTES),
        name="inproj_prompt",
    )(x2, g, *w, lng, lnb, ws_bf, bias_full, rope_tab)


def _inproj_decode_kernel(x_ref, g_ref, wm_ref, wg_ref, wz_ref, lng_ref, lnb_ref, wsv_ref, bsv_ref, rope_ref,
                          aout_ref, qa_ref, kc_ref, vc_ref, ks_ref, vs_ref, kw_ref, vw_ref,
                          gate_ref, szb_ref, vn_ref, z_scr):
    _project(x_ref, g_ref, (wm_ref, wg_ref, wz_ref), z_scr)
    tab = rope_ref[...]
    u = z_scr[:, C_U:C_U + D_A]
    v = z_scr[:, C_V:C_V + D_A]
    za = z_scr[:, C_ZA:C_ZA + D_A]
    vn = _layer_norm_rows(v, lng_ref[...], lnb_ref[...])
    vn_ref[...] = vn
    s = vn * wsv_ref[...] + bsv_ref[...]
    aout_ref[...] = (u * s * jax.nn.silu(za)).astype(BF16)

    q = z_scr[:, C_Q:C_Q + D_B]
    qr = jnp.concatenate(
        [_rope(q[:, j * LANES:(j + 1) * LANES], tab) for j in range(4)], axis=1) * (HEAD_DIM ** -0.5)
    for e, slab in enumerate(_head_slabs(qr)):
        qa_ref[:, e * LANES:(e + 1) * LANES] = slab.astype(BF16)
    kc_ref[...] = z_scr[:, C_KC:C_KC + KV_W]
    vc_ref[...] = z_scr[:, C_VC:C_VC + KV_W]
    ks_ref[...] = _rope(z_scr[:, C_KS:C_KS + KV_W], tab)
    vs_ref[...] = z_scr[:, C_VS:C_VS + KV_W]
    kw_ref[...] = _rope(z_scr[:, C_KW:C_KW + KV_W], tab)
    vw_ref[...] = z_scr[:, C_VW:C_VW + KV_W]
    gate_ref[...] = jax.nn.sigmoid(z_scr[:, C_G:C_G + LANES])
    szb = jax.nn.silu(z_scr[:, C_ZB:C_ZB + D_B])
    for e, slab in enumerate(_head_slabs(szb)):
        szb_ref[:, e * LANES:(e + 1) * LANES] = slab


def _inproj_decode(x2, g, w, lng, lnb, wsv, bsv, rope_tab):
    n = x2.shape[0]
    out_shapes = [
        jax.ShapeDtypeStruct((n, D_A), BF16),
        jax.ShapeDtypeStruct((n, N_HEADS * LANES), BF16),
        jax.ShapeDtypeStruct((n, KV_W), F32),
        jax.ShapeDtypeStruct((n, KV_W), F32),
        jax.ShapeDtypeStruct((n, KV_W), F32),
        jax.ShapeDtypeStruct((n, KV_W), F32),
        jax.ShapeDtypeStruct((n, KV_W), F32),
        jax.ShapeDtypeStruct((n, KV_W), F32),
        jax.ShapeDtypeStruct((n, LANES), F32),
        jax.ShapeDtypeStruct((n, N_HEADS * LANES), F32),
        jax.ShapeDtypeStruct((n, D_A), F32),
    ]
    return pl.pallas_call(
        _inproj_decode_kernel,
        out_shape=out_shapes,
        scratch_shapes=[pltpu.VMEM((n, C_END), F32)],
        compiler_params=pltpu.CompilerParams(vmem_limit_bytes=VMEM_LIMIT_BYTES),
        name="inproj_decode",
    )(x2, g, *w, lng, lnb, wsv, bsv, rope_tab)


def _compress_chunks(ch, pos_term, w_ref):
    ab = _dot(ch.astype(BF16), w_ref[...])
    return ab[:, 0:KV_W] + pltpu.roll(ab[:, KV_W:2 * KV_W], ch.shape[0] - 1, 0) + pos_term


def _position_term(pe_ref, w_ref):
    pe = pe_ref[...]
    hi = pe.astype(BF16)
    r1 = pe - hi.astype(F32)
    mid = r1.astype(BF16)
    lo = (r1 - mid.astype(F32)).astype(BF16)
    pw = _dot(hi, w_ref[...]) + _dot(mid, w_ref[...]) + _dot(lo, w_ref[...])
    return pw[0:1, 0:KV_W] + pw[1:2, KV_W:2 * KV_W]


def _compress_prompt_kernel(kc_ref, vc_ref, wk_ref, wv_ref, pek_ref, pev_ref, rope_ref, kcc_ref, vcct_ref):
    kcc = _compress_chunks(kc_ref[0], _position_term(pek_ref, wk_ref), wk_ref)
    kcc_ref[0] = _rope(kcc, rope_ref[...]).astype(BF16)
    vcc = _compress_chunks(vc_ref[0], _position_term(pev_ref, wv_ref), wv_ref)
    for j in range(vcc.shape[0] // LANES):
        vcct_ref[0, :, j * LANES:(j + 1) * LANES] = vcc[j * LANES:(j + 1) * LANES].T.astype(BF16)


def _compress_prompt(kc3, vc3, wk, wv, pek, pev, rope_tab):
    b, c, width = kc3.shape
    blk = pl.BlockSpec((1, c, width), lambda i: (i, 0, 0))
    const2 = lambda i: (0, 0)
    return pl.pallas_call(
        _compress_prompt_kernel,
        grid=(b,),
        in_specs=[blk, blk,
                  pl.BlockSpec(wk.shape, const2), pl.BlockSpec(wv.shape, const2),
                  pl.BlockSpec(pek.shape, const2), pl.BlockSpec(pev.shape, const2),
                  pl.BlockSpec(rope_tab.shape, const2)],
        out_specs=[pl.BlockSpec((1, c, KV_W), lambda i: (i, 0, 0)),
                   pl.BlockSpec((1, KV_W, c), lambda i: (i, 0, 0))],
        out_shape=[jax.ShapeDtypeStruct((b, c, KV_W), BF16), jax.ShapeDtypeStruct((b, KV_W, c), BF16)],
        compiler_params=pltpu.CompilerParams(
            dimension_semantics=("arbitrary",), vmem_limit_bytes=VMEM_LIMIT_BYTES),
        name="compress_prompt",
    )(kc3, vc3, wk, wv, pek, pev, rope_tab)


def _select_blocks(imp, kk, candidate, forced):
    assert FORCE_SCORE > 2 * GQA
    start = jnp.where(candidate, jnp.where(forced, -jnp.inf, imp), -jnp.inf)
    score = start
    for _ in range(N_SELECT - N_FORCED):
        cm = jnp.max(score, axis=0, keepdims=True)
        first = jnp.min(jnp.where(score == cm, kk, LANES), axis=0, keepdims=True)
        score = jnp.where(kk == first, -jnp.inf, score)
    taken = (score == -jnp.inf) & (start > -jnp.inf)
    return candidate & (forced | taken)


def _softmax_rows(s):
    mx = jnp.max(s, axis=1, keepdims=True)
    ex = jnp.exp(s - mx)
    return ex * (1.0 / jnp.sum(ex, axis=1, keepdims=True))


def _tile_heads(x):
    return jnp.concatenate([x] * N_HEADS, axis=1)


def _values_by_kv_head(vt, p):
    half = GQA * Q_BLOCK
    return jnp.concatenate(
        [_dot(vt[h * HEAD_DIM:(h + 1) * HEAD_DIM], p[:, h * half:(h + 1) * half]) for h in range(N_KV)], axis=1)


def _attn_prompt_kernel(*refs):
    def body(sub, carry):
        _attn_query_block(sub, *refs)
        return carry

    lax.fori_loop(0, Q_SUB, body, 0)


def _attn_query_block(sub, qt_ref, gate_ref, szb_ref, kcc_ref, vcct_ref, ovlt_ref,
                      ksa_ref, vsb_ref, kwb_ref, vwb_ref, out_ref,
                      qaug_scr, acc_scr, sa_scr, sb_scr):
    i_blk = pl.program_id(1) * Q_SUB + sub
    q0 = i_blk * Q_BLOCK
    cols = N_HEADS * Q_BLOCK
    qt = qt_ref[0, sub]
    out_rows = pl.ds(pl.multiple_of(sub * Q_BLOCK, Q_BLOCK), Q_BLOCK)

    n_cmp = kcc_ref.shape[1]

    def compressed(extent):
        t_c = q0 + _lane_iota((extent, Q_BLOCK))
        seen = (_row_iota((extent, Q_BLOCK)) * CMP_STRIDE + (CMP_BLOCK - 1)) <= t_c
        s = _dot(kcc_ref[0, 0:extent, :], qt) + _tile_heads(jnp.where(seen, 0.0, NEG))
        mx = jnp.max(s, axis=0, keepdims=True)
        ex = jnp.exp2(s - mx)
        inv = jnp.where(mx > 0.5 * NEG, 1.0 / jnp.sum(ex, axis=0, keepdims=True), 0.0)
        p_c = ex * inv
        imps = []
        for h in range(N_KV):
            c0 = h * GQA * Q_BLOCK
            p_sum = (p_c[:, c0:c0 + Q_BLOCK] + p_c[:, c0 + Q_BLOCK:c0 + 2 * Q_BLOCK]
                     + p_c[:, c0 + 2 * Q_BLOCK:c0 + 3 * Q_BLOCK] + p_c[:, c0 + 3 * Q_BLOCK:c0 + 4 * Q_BLOCK])
            imps.append(_dot(ovlt_ref[:, 0:extent], p_sum.astype(BF16)))
        return (_values_by_kv_head(vcct_ref[0, :, 0:extent], p_c.astype(BF16)), *imps)

    n_need = (q0 + Q_BLOCK) // CMP_STRIDE
    extents = list(range(LANES, n_cmp + 1, LANES))
    o_c, *imp_t = lax.switch(
        jnp.minimum((n_need - 1) // LANES, len(extents) - 1),
        [functools.partial(compressed, ext) for ext in extents])

    sa_scr[...] = _dot(ksa_ref[0, 0:SEL_TILE, 0:LANES], qt)

    n_wt = WIN_SPAN // LANES
    r_w, l_w = _row_iota((LANES, Q_BLOCK)), _lane_iota((LANES, Q_BLOCK))
    k_tiles, v_tiles = [], []
    for t in range(n_wt):
        j = i_blk - (n_wt - 1) + t
        jc = jnp.maximum(j, 0)
        off = jnp.where((l_w == 0) & (j < 0), 1.0, 0.0).astype(BF16)
        k_tiles.append(jnp.concatenate(
            [kwb_ref[0, pl.ds(pl.multiple_of(jc * LANES, LANES), LANES), :], off], axis=1))
        v_tiles.append(vwb_ref[0, jc])
    vwt = jnp.concatenate(v_tiles, axis=1)
    q_off = jnp.where(_row_iota((LANES, cols)) == 0, NEG, 0.0).astype(BF16)
    sw = _dot(jnp.concatenate(k_tiles, axis=0), jnp.concatenate([qt, q_off], axis=0))
    sw = jnp.concatenate([
        sw[0:LANES] + _tile_heads(jnp.where(r_w > l_w, 0.0, NEG)),
        sw[LANES:WIN_SPAN - LANES],
        sw[WIN_SPAN - LANES:] + _tile_heads(jnp.where(r_w <= l_w, 0.0, NEG))], axis=0)
    exw = jnp.exp2(sw - jnp.max(sw, axis=0, keepdims=True))
    o_w = _values_by_kv_head(vwt, exw.astype(BF16)) * (1.0 / jnp.sum(exw, axis=0, keepdims=True))

    kk = _row_iota((LANES, Q_BLOCK))
    tq = q0 + _lane_iota((LANES, Q_BLOCK))
    valid = kk * SEL_BLOCK <= tq
    cur = tq >> SEL_SHIFT
    forced = (kk == 0) | (kk == cur) | (kk == cur - 1)
    both = lambda a: jnp.concatenate([a] * N_KV, axis=1)
    picked = _select_blocks(jnp.concatenate(imp_t, axis=1), both(kk), both(valid), both(forced))
    qaug_scr[0:LANES, :] = qt
    for h in range(N_KV):
        sel_f = jnp.where(picked[:, h * Q_BLOCK:(h + 1) * Q_BLOCK], 0.0, NEG)
        selneg = sel_f.astype(BF16)
        for g in range(GQA):
            c0 = (h * GQA + g) * Q_BLOCK
            qaug_scr[LANES:2 * LANES, c0:c0 + Q_BLOCK] = selneg
            for b in range(SEL_TILE // SEL_BLOCK):
                sa_scr[b * SEL_BLOCK:(b + 1) * SEL_BLOCK, c0:c0 + Q_BLOCK] += jnp.broadcast_to(
                    sel_f[b:b + 1, :], (SEL_BLOCK, Q_BLOCK))

    acc_scr[...] = jnp.zeros(acc_scr.shape, F32)
    tiles_per_step = SEL_TILE // LANES

    def score(step, dst):
        t0 = pl.multiple_of(step * SEL_TILE, SEL_TILE)
        dst[...] = _dot(ksa_ref[0, pl.ds(t0, SEL_TILE), :], qaug_scr[...])

    def attend(src, step, m_prev, l_prev, causal_rows=None):
        rows = SEL_TILE if causal_rows is None else causal_rows
        vt = jnp.concatenate([vsb_ref[0, step * tiles_per_step + i] for i in range(rows // LANES)], axis=1)
        if causal_rows is None:
            sc = src[...]
        else:
            own = src[rows - Q_BLOCK:rows, :] + _tile_heads(jnp.where(r_w <= l_w, 0.0, NEG))
            sc = own if rows == Q_BLOCK else jnp.concatenate([src[0:rows - Q_BLOCK, :], own], axis=0)
        m_next = jnp.maximum(m_prev, jnp.max(sc, axis=0, keepdims=True))
        p = jnp.exp2(sc - m_next)
        alpha = jnp.exp2(m_prev - m_next)
        l_next = alpha * l_prev + jnp.sum(p, axis=0, keepdims=True)
        acc_scr[...] = alpha * acc_scr[...] + _values_by_kv_head(vt, p.astype(BF16))
        return m_next, l_next

    n_full = q0 // SEL_TILE

    def pair(jj, carry):
        m_c, l_c = carry
        score(2 * jj + 1, sb_scr)
        m_c, l_c = attend(sa_scr, 2 * jj, m_c, l_c)
        score(2 * jj + 2, sa_scr)
        return attend(sb_scr, 2 * jj + 1, m_c, l_c)

    m_run, l_run = lax.fori_loop(
        0, n_full // 2, pair,
        (jnp.full((1, cols), -jnp.inf, F32), jnp.zeros((1, cols), F32)))

    def last_step(src, m_c, l_c):
        extents = list(range(Q_BLOCK, SEL_TILE + 1, Q_BLOCK))
        return lax.switch(
            (q0 % SEL_TILE) // Q_BLOCK,
            [lambda m, l, ext=ext: attend(src, n_full, m, l, causal_rows=ext)[1] for ext in extents],
            m_c, l_c)

    def odd_tail(m_c, l_c):
        score(n_full, sb_scr)
        m_c, l_c = attend(sa_scr, n_full - 1, m_c, l_c)
        return last_step(sb_scr, m_c, l_c)

    def even_tail(m_c, l_c):
        return last_step(sa_scr, m_c, l_c)

    l_run = lax.cond(n_full % 2 == 1, odd_tail, even_tail, m_run, l_run)

    gt = gate_ref[0, sub]
    o_s = acc_scr[...] * (1.0 / l_run)
    heads = []
    for e in range(N_HEADS):
        c0 = e * Q_BLOCK
        heads.append(gt[3 * e:3 * e + 1, :] * o_c[:, c0:c0 + Q_BLOCK]
                     + gt[3 * e + 1:3 * e + 2, :] * o_s[:, c0:c0 + Q_BLOCK]
                     + gt[3 * e + 2:3 * e + 3, :] * o_w[:, c0:c0 + Q_BLOCK])
    for j in range(N_HEADS // 2):
        slab = jnp.concatenate([heads[2 * j], heads[2 * j + 1]], axis=0).T
        out_ref[0, out_rows, j * LANES:(j + 1) * LANES] = (
            slab * szb_ref[0, out_rows, j * LANES:(j + 1) * LANES]).astype(BF16)


def _attn_prompt(qt4, gate4, szb3, kcc, vcct, ovlt, ksa3, vsb4, kwb3, vwb4):
    b, nq = qt4.shape[0], qt4.shape[1]
    t = nq * Q_BLOCK
    cols = N_HEADS * Q_BLOCK
    assert nq % Q_SUB == 0
    qtile = lambda a: pl.BlockSpec((1, Q_SUB) + a.shape[2:], lambda bi, i: (bi, i, 0, 0))
    rows = pl.BlockSpec((1, Q_SUB * Q_BLOCK, D_B), lambda bi, i: (bi, i, 0))
    seq = lambda a: pl.BlockSpec((1,) + a.shape[1:], lambda bi, i, nd=a.ndim: (bi,) + (0,) * (nd - 1))
    return pl.pallas_call(
        _attn_prompt_kernel,
        grid=(b, nq // Q_SUB),
        in_specs=[qtile(qt4), qtile(gate4), rows,
                  seq(kcc), seq(vcct), pl.BlockSpec(ovlt.shape, lambda bi, i: (0, 0)),
                  seq(ksa3), seq(vsb4), seq(kwb3), seq(vwb4)],
        out_specs=rows,
        out_shape=jax.ShapeDtypeStruct((b, t, D_B), BF16),
        scratch_shapes=[pltpu.VMEM((2 * LANES, cols), BF16),
                        pltpu.VMEM((HEAD_DIM, cols), F32),
                        pltpu.VMEM((SEL_TILE, cols), F32),
                        pltpu.VMEM((SEL_TILE, cols), F32)],
        compiler_params=pltpu.CompilerParams(
            dimension_semantics=("arbitrary", "arbitrary"), vmem_limit_bytes=VMEM_LIMIT_BYTES),
        name="attn_prompt",
    )(qt4, gate4, szb3, kcc, vcct, ovlt, ksa3, vsb4, kwb3, vwb4)


def _outproj_kernel(x_ref, a_ref, m_ref, wa_ref, wb_ref, fg_ref, y_ref):
    y = x_ref[...] + _dot(a_ref[...], wa_ref[...]) + _dot(m_ref[...].astype(BF16), wb_ref[...])
    y_ref[...] = _rms_rows(y, fg_ref[...])


def _outproj(x2, a_out, mix_b, wa, wb, fg, tm):
    n = x2.shape[0]
    row = lambda i: (i, 0)
    const2 = lambda i: (0, 0)
    return pl.pallas_call(
        _outproj_kernel,
        grid=(n // tm,),
        in_specs=[pl.BlockSpec((tm, D_MODEL), row),
                  pl.BlockSpec((tm, a_out.shape[1]), row),
                  pl.BlockSpec((tm, mix_b.shape[1]), row),
                  pl.BlockSpec(wa.shape, const2), pl.BlockSpec(wb.shape, const2),
                  pl.BlockSpec((1, D_MODEL), const2)],
        out_specs=pl.BlockSpec((tm, D_MODEL), row),
        out_shape=jax.ShapeDtypeStruct((n, D_MODEL), F32),
        compiler_params=pltpu.CompilerParams(
            dimension_semantics=("arbitrary",), vmem_limit_bytes=VMEM_LIMIT_BYTES),
        name="outproj",
    )(x2, a_out, mix_b, wa, wb, fg)


def _page_copies(pt_ref, pool_ref, buf_ref, sem_ref, step, slot):
    group, n_pages = buf_ref.shape[1], buf_ref.shape[2]
    return [pltpu.make_async_copy(pool_ref.at[pt_ref[step * group + g, p]], buf_ref.at[slot, g, p],
                                  sem_ref.at[slot])
            for g in range(group) for p in range(n_pages)]


def _gather_pages(pt_ref, pools, bufs, sems):
    b = pl.program_id(0)
    slot = lax.rem(b, 2)

    def start(step, sl):
        for pool, buf, sem in zip(pools, bufs, sems):
            for cp in _page_copies(pt_ref, pool, buf, sem, step, sl):
                cp.start()

    @pl.when(b == 0)
    def _():
        start(0, 0)

    @pl.when(b + 1 < pl.num_programs(0))
    def _():
        start(b + 1, 1 - slot)

    for pool, buf, sem in zip(pools, bufs, sems):
        for cp in _page_copies(pt_ref, pool, buf, sem, b, slot):
            cp.wait()
    return slot


def _decode_cmp_kernel(pt_ref, ck_hbm, cv_hbm, q_ref, wk_ref, wv_ref, pek_ref, pev_ref, rope_ref, ovl_ref,
                       oc_ref, psum_ref, kbuf, vbuf, ksem, vsem, ktok_scr, vtok_scr, kpos_scr, vpos_scr):
    slot = _gather_pages(pt_ref, (ck_hbm, cv_hbm), (kbuf, vbuf), (ksem, vsem))
    group, n_pages = kbuf.shape[1], kbuf.shape[2]

    @pl.when(pl.program_id(0) == 0)
    def _():
        kpos_scr[...] = _position_term(pek_ref, wk_ref)
        vpos_scr[...] = _position_term(pev_ref, wv_ref)

    def chunk_rows(buf, tok_scr, g):
        for p in range(n_pages):
            tok_scr[g, p * PAGE_SIZE:(p + 1) * PAGE_SIZE, :] = buf[slot, g, p].T
        n = tok_scr.shape[1] // CMP_STRIDE
        return jnp.concatenate(
            [tok_scr[g, pl.ds(r, n, stride=CMP_STRIDE), :] for r in range(CMP_STRIDE)], axis=1)

    for g in range(group):
        kch = chunk_rows(kbuf, ktok_scr, g)
        vch = chunk_rows(vbuf, vtok_scr, g)
        kcc = _rope(_compress_chunks(kch, kpos_scr[...], wk_ref), rope_ref[...]).astype(BF16)
        vcc = _compress_chunks(vch, vpos_scr[...], wv_ref).astype(BF16)
        n_chunks = kch.shape[0]
        q = q_ref[g]
        rows = q.shape[0]
        s = _dot_nt(q, kcc)
        seen = _lane_iota((rows, n_chunks)) < n_chunks - 1
        p_c = jnp.where(seen, _softmax_rows(jnp.where(seen, s, NEG)), 0.0)
        oc_ref[g] = _dot(p_c.astype(BF16), vcc)[0:N_HEADS]
        ps0 = jnp.sum(p_c[0:GQA], axis=0, keepdims=True)
        ps1 = jnp.sum(p_c[GQA:2 * GQA], axis=0, keepdims=True)
        psum = jnp.where(_row_iota((rows, n_chunks)) < N_HEADS // 2, ps0, ps1)
        psum_ref[g] = _dot(psum.astype(BF16), ovl_ref[...])[0:N_HEADS]


def _decode_cmp(page_table, ck_pool, cv_pool, q16, wk, wv, pek, pev, rope_tab, ovl):
    nb, n_pages = page_table.shape
    group = DECODE_GROUP
    assert nb % group == 0
    hbm = pl.BlockSpec(memory_space=pl.ANY)
    page_buf = pltpu.VMEM((2, group, n_pages, KV_W, PAGE_SIZE), F32)
    tok_buf = pltpu.VMEM((group, n_pages * PAGE_SIZE, KV_W), F32)
    const2 = lambda b, pt: (0, 0)
    per_b = lambda a: pl.BlockSpec((group,) + a.shape[1:], lambda b, pt: (b, 0, 0))
    in_specs = [hbm, hbm, per_b(q16),
                pl.BlockSpec(wk.shape, const2), pl.BlockSpec(wv.shape, const2),
                pl.BlockSpec(pek.shape, const2), pl.BlockSpec(pev.shape, const2),
                pl.BlockSpec(rope_tab.shape, const2), pl.BlockSpec(ovl.shape, const2)]
    out_blk = pl.BlockSpec((group, N_HEADS, LANES), lambda b, pt: (b, 0, 0))
    grid_spec = pltpu.PrefetchScalarGridSpec(
        num_scalar_prefetch=1, grid=(nb // group,), in_specs=in_specs, out_specs=[out_blk, out_blk],
        scratch_shapes=[page_buf, page_buf, pltpu.SemaphoreType.DMA((2,)), pltpu.SemaphoreType.DMA((2,)),
                        tok_buf, tok_buf, pltpu.VMEM((1, KV_W), F32), pltpu.VMEM((1, KV_W), F32)])
    return pl.pallas_call(
        _decode_cmp_kernel,
        grid_spec=grid_spec,
        out_shape=[jax.ShapeDtypeStruct((nb, N_HEADS, LANES), F32)] * 2,
        compiler_params=pltpu.CompilerParams(
            dimension_semantics=("arbitrary",), vmem_limit_bytes=VMEM_LIMIT_BYTES),
        name="decode_cmp",
    )(page_table, ck_pool, cv_pool, q16, wk, wv, pek, pev, rope_tab, ovl)


def _decode_topk_kernel(imp_ref, sel_ref, *, n_blk, cur):
    n = imp_ref.shape[0]
    tiles = [imp_ref[i * LANES:(i + 1) * LANES, :].T for i in range(n // LANES)]
    imp_t = jnp.concatenate(tiles, axis=1)
    kk = _row_iota(imp_t.shape)
    forced = (kk == 0) | (kk == cur) | (kk == cur - 1)
    selneg = jnp.where(_select_blocks(imp_t, kk, kk < n_blk, forced), 0.0, NEG)
    for i in range(n // LANES):
        sel_ref[i * LANES:(i + 1) * LANES, :] = selneg[:, i * LANES:(i + 1) * LANES].T


def _decode_topk(imp2, n_blk, cur):
    return pl.pallas_call(
        functools.partial(_decode_topk_kernel, n_blk=n_blk, cur=cur),
        out_shape=jax.ShapeDtypeStruct(imp2.shape, F32),
        compiler_params=pltpu.CompilerParams(vmem_limit_bytes=VMEM_LIMIT_BYTES),
        name="decode_topk",
    )(imp2)


def _decode_attend(s, q32, vals_t_bf, k_new, v_new, bias_new):
    kn = k_new.astype(BF16).astype(F32)
    s_new = jnp.sum(q32 * kn, axis=1, keepdims=True) + bias_new
    mx = jnp.maximum(jnp.max(s, axis=1, keepdims=True), s_new)
    ex = jnp.exp(s - mx)
    ex_new = jnp.exp(s_new - mx)
    inv = 1.0 / (jnp.sum(ex, axis=1, keepdims=True) + ex_new)
    p = (ex * inv).astype(BF16)
    p_new = (ex_new * inv).astype(BF16).astype(F32)
    return _dot_nt(p, vals_t_bf) + p_new * v_new.astype(BF16).astype(F32)


P_Q, P_SEL, P_OC, P_SZB, P_NEW, P_GATE, P_ROWS = 0, 16, 32, 40, 48, 56, 64


def _row_to_col(row):
    n = row.shape[1]
    diag = _row_iota((n, n)) == _lane_iota((n, n))
    return jnp.sum(jnp.where(diag, row, 0.0), axis=1, keepdims=True)


def _decode_sel_kernel(pt_ref, ks_hbm, vs_hbm, ckw_ref, cvw_ref, pack_ref, onehot_ref,
                       mix_ref, okw_ref, ovw_ref, kbuf, vbuf, ksem, vsem, *, cur):
    slot = _gather_pages(pt_ref, (ks_hbm, vs_hbm), (kbuf, vbuf), (ksem, vsem))
    group, n_pages = kbuf.shape[1], kbuf.shape[2]
    for g in range(group):
        pk = pack_ref[g]
        q32 = pk[P_Q:P_Q + 2 * N_HEADS]
        q = q32.astype(BF16)
        sel = pk[P_SEL:P_SEL + 2 * N_HEADS].astype(BF16)
        rows = q.shape[0]
        ks_new, vs_new = pk[P_NEW:P_NEW + 1], pk[P_NEW + 1:P_NEW + 2]
        kw_new, vw_new = pk[P_NEW + 2:P_NEW + 3], pk[P_NEW + 3:P_NEW + 4]

        kt = jnp.concatenate([kbuf[slot, g, p] for p in range(n_pages)], axis=1).astype(BF16)
        vt = jnp.concatenate([vbuf[slot, g, p] for p in range(n_pages)], axis=1).astype(BF16)
        kaug = jnp.concatenate([kt, onehot_ref[...]], axis=0)
        qaug = jnp.concatenate([q, sel], axis=1)
        s_sel = _dot(qaug, kaug)
        o_s = _decode_attend(s_sel, q32, vt, ks_new, vs_new, sel.astype(F32)[:, cur:cur + 1])

        kwin = ckw_ref[g]
        vwin = cvw_ref[g]
        keep = kwin.shape[1]
        visible = _lane_iota((rows, keep)) > keep - WINDOW
        s_win = jnp.where(visible, _dot(q, kwin.astype(BF16)), NEG)
        o_w = _decode_attend(s_win, q32, vwin.astype(BF16), kw_new, vw_new, 0.0)

        gates = pk[P_GATE:P_GATE + N_HEADS]
        o = (gates[:, 0:1] * pk[P_OC:P_OC + N_HEADS] + gates[:, 1:2] * o_s[0:N_HEADS]
             + gates[:, 2:3] * o_w[0:N_HEADS])
        mix_ref[g] = o * pk[P_SZB:P_SZB + N_HEADS]

        last = _lane_iota((KV_W, keep)) == keep - 1
        okw_ref[g] = jnp.where(last, _row_to_col(kw_new), pltpu.roll(kwin, keep - 1, 1))
        ovw_ref[g] = jnp.where(last, _row_to_col(vw_new), pltpu.roll(vwin, keep - 1, 1))


def _decode_sel(page_table, ks_pool, vs_pool, ckw, cvw, pack, onehot, cur):
    nb, n_pages = page_table.shape
    group = DECODE_GROUP
    assert nb % group == 0
    keep = ckw.shape[2]
    hbm = pl.BlockSpec(memory_space=pl.ANY)
    page_buf = pltpu.VMEM((2, group, n_pages, KV_W, PAGE_SIZE), F32)
    per_b = lambda a: pl.BlockSpec((group,) + a.shape[1:], lambda b, pt: (b, 0, 0))
    in_specs = [hbm, hbm, per_b(ckw), per_b(cvw), per_b(pack),
                pl.BlockSpec(onehot.shape, lambda b, pt: (0, 0))]
    out_specs = [pl.BlockSpec((group, N_HEADS, LANES), lambda b, pt: (b, 0, 0)),
                 pl.BlockSpec((group, KV_W, keep), lambda b, pt: (b, 0, 0)),
                 pl.BlockSpec((group, KV_W, keep), lambda b, pt: (b, 0, 0))]
    grid_spec = pltpu.PrefetchScalarGridSpec(
        num_scalar_prefetch=1, grid=(nb // group,), in_specs=in_specs, out_specs=out_specs,
        scratch_shapes=[page_buf, page_buf, pltpu.SemaphoreType.DMA((2,)), pltpu.SemaphoreType.DMA((2,))])
    return pl.pallas_call(
        functools.partial(_decode_sel_kernel, cur=cur),
        grid_spec=grid_spec,
        out_shape=[jax.ShapeDtypeStruct((nb, N_HEADS, LANES), F32),
                   jax.ShapeDtypeStruct((nb, KV_W, keep), F32),
                   jax.ShapeDtypeStruct((nb, KV_W, keep), F32)],
        compiler_params=pltpu.CompilerParams(
            dimension_semantics=("arbitrary",), vmem_limit_bytes=VMEM_LIMIT_BYTES),
        name="decode_sel",
    )(page_table, ks_pool, vs_pool, ckw, cvw, pack, onehot)


def _rope_table(pos):
    half = ROT_DIM // 2
    inv = jnp.power(jnp.float32(ROPE_THETA), -jnp.arange(half, dtype=F32) / half)
    ang = pos.astype(F32)[:, None] * inv[None, :]
    cs = jnp.concatenate([jnp.cos(ang), jnp.sin(ang)], axis=1)
    top16 = lambda a: lax.bitcast_convert_type(
        lax.bitcast_convert_type(a, jnp.uint32) & jnp.uint32(0xFFFF0000), F32)
    hi = top16(cs)
    mid = top16(cs - hi)
    pieces = jnp.concatenate([hi, mid, cs - hi - mid], axis=1)
    lane = jnp.arange(2 * LANES)
    seg, l64 = lane // LANES, lane % HEAD_DIM
    src = jnp.where(seg == 0, l64 % half, half + l64 % half)
    sign = jnp.where(seg == 0, l64 < ROT_DIM, l64 < half).astype(F32)
    sign = jnp.where(seg == 1, -sign, sign)
    place = (jnp.arange(ROT_DIM)[:, None] == src[None, :]).astype(F32) * sign[None, :]
    ones = ((seg == 0) & (l64 >= ROT_DIM)).astype(F32)
    return jnp.dot(pieces, jnp.concatenate([place] * 3, axis=0)) + ones[None, :]


def _compress_weights(w_c, pe_c):
    w4 = w_c.reshape(2, CMP_STRIDE, HEAD_DIM, HEAD_DIM).astype(BF16)
    zero = jnp.zeros_like(w4[0])
    wb = jnp.stack([jnp.concatenate([w4[a] if g == h else zero for a in range(2) for g in range(N_KV)], axis=-1)
                    for h in range(N_KV)], axis=1)
    wb = wb.reshape(CMP_STRIDE * KV_W, 2 * KV_W)
    pe = jnp.broadcast_to(pe_c.reshape(2, CMP_STRIDE, 1, HEAD_DIM), (2, CMP_STRIDE, N_KV, HEAD_DIM))
    return wb, jnp.pad(pe.reshape(2, CMP_STRIDE * KV_W), ((0, 14), (0, 0)))


def _overlap_matrix(n_rows, n_blk):
    cs = jnp.arange(n_rows, dtype=jnp.int32)[:, None] * CMP_STRIDE
    bs = jnp.arange(LANES, dtype=jnp.int32)[None, :] * SEL_BLOCK
    hit = (cs < bs + SEL_BLOCK) & (cs + CMP_BLOCK > bs) & (jnp.arange(LANES)[None, :] < n_blk)
    return hit.astype(BF16)


def kernel(x_prompt, x_sample, cache_k_cmp, cache_v_cmp, cache_k_sel, cache_v_sel, cache_k_win,
           cache_v_win, page_table, norm_g, w_in, ln_v_g, ln_v_b, w_s, b_s, w_ck, pe_ck, w_cv,
           pe_cv, w_out, final_g):
    depth = norm_g.shape[0]
    assert depth == 1
    bsz, seq, _ = x_prompt.shape
    nb, dec_seq, _ = x_sample.shape
    assert dec_seq == 1 and seq % SEL_TILE == 0 and seq >= WIN_SPAN and (seq // CMP_STRIDE) % LANES == 0
    n_pages = page_table.shape[1]
    past = n_pages * PAGE_SIZE
    keep = cache_k_win.shape[2]
    assert keep == WINDOW and past % SEL_BLOCK == 0

    w = w_in[0]
    w_pad = (w[:, :C_G].astype(BF16),
             jnp.pad(w[:, C_G:C_G + N_GATE_COLS], ((0, 0), (0, LANES - N_GATE_COLS))).astype(BF16),
             w[:, C_G + N_GATE_COLS:].astype(BF16))
    g_row = norm_g[0][None, :]
    lng = ln_v_g[0][None, :]
    lnb = ln_v_b[0][None, :]
    fg = final_g[None, :]
    wk_c, pek = _compress_weights(w_ck[0], pe_ck[0])
    wv_c, pev = _compress_weights(w_cv[0], pe_cv[0])
    w_o = w_out[0].astype(BF16)
    wo_a, wo_b = w_o[:D_A], w_o[D_A:]

    tril = jnp.tril(jnp.ones((CHUNK, CHUNK), w_s.dtype))
    ws_bf = (w_s[0] * tril).astype(BF16)
    bias_full = jnp.repeat(b_s[0].T, HEAD_DIM, axis=1)
    xp2 = x_prompt.reshape(bsz * seq, D_MODEL)
    (a_out, qt, kc, vc, kc_t, vc_t, ks_t, vs_t, kw_t, vw_t, ksa, kwb, vsb, vwb, gates_t, szb) = _inproj_prompt(
        xp2, g_row, w_pad, lng, lnb, ws_bf, bias_full,
        _rope_table(jnp.arange(seq, dtype=jnp.int32)), seq)
    n_chunks = seq // CMP_STRIDE
    kcc, vcc_t = _compress_prompt(
        kc.reshape(bsz, n_chunks, CMP_STRIDE * KV_W), vc.reshape(bsz, n_chunks, CMP_STRIDE * KV_W),
        wk_c, wv_c, pek, pev,
        _rope_table(jnp.arange(n_chunks, dtype=jnp.int32) * CMP_STRIDE))
    b3 = lambda a: a.reshape(bsz, seq, a.shape[-1])
    mix_p = _attn_prompt(qt, gates_t, b3(szb), kcc, vcc_t,
                         _overlap_matrix(n_chunks, seq // SEL_BLOCK).T,
                         b3(ksa), vsb, b3(kwb), vwb)
    y_prompt = _outproj(xp2, a_out, mix_p.reshape(bsz * seq, D_B), wo_a, wo_b, fg, OUT_ROW_TILE)
    y_prompt = y_prompt.reshape(bsz, seq, D_MODEL)
    st = lambda a: jnp.transpose(a.reshape(a.shape[0], N_KV, HEAD_DIM, a.shape[2]), (0, 3, 1, 2))[None]
    keep_p = min(WINDOW, seq)

    xs2 = x_sample.reshape(nb, D_MODEL)
    wsv = jnp.repeat(w_s[0][:, 0, 0], HEAD_DIM)[None, :]
    bsv = jnp.repeat(b_s[0][:, 0], HEAD_DIM)[None, :]
    (a_s, qa_s, kc_s, vc_s, ks_s, vs_s, kw_s, vw_s, gates_s, szb_s, vn_s) = _inproj_decode(
        xs2, g_row, w_pad, lng, lnb, wsv, bsv,
        _rope_table(jnp.full((nb,), past, dtype=jnp.int32)))
    q16 = jnp.pad(qa_s.reshape(nb, N_HEADS, LANES), ((0, 0), (0, N_HEADS), (0, 0)))
    d_chunks = past // CMP_STRIDE
    n_blk = -(-(past + 1) // SEL_BLOCK)
    cur = past // SEL_BLOCK
    kv_t = lambda c: jnp.transpose(c[0], (0, 2, 3, 1)).reshape(c.shape[1], KV_W, c.shape[2])
    oc_s, imp_s = _decode_cmp(
        page_table, kv_t(cache_k_cmp), kv_t(cache_v_cmp),
        q16, wk_c, wv_c, pek, pev,
        _rope_table(jnp.arange(d_chunks, dtype=jnp.int32) * CMP_STRIDE),
        _overlap_matrix(d_chunks, n_blk))
    selneg = _decode_topk(imp_s[:, ::GQA, :].reshape(nb * N_KV, LANES), n_blk, cur)
    sel16 = jnp.pad(jnp.repeat(selneg.reshape(nb, N_KV, LANES), GQA, axis=1), ((0, 0), (0, N_HEADS), (0, 0)))
    tok_blk = jnp.arange(past, dtype=jnp.int32)[None, :] // SEL_BLOCK
    onehot = (tok_blk == jnp.arange(LANES, dtype=jnp.int32)[:, None]).astype(BF16)
    gate_rows = jnp.pad(gates_s[:, :N_GATE_COLS].reshape(nb, N_HEADS, 3), ((0, 0), (0, 0), (0, LANES - 3)))
    new_rows = jnp.pad(jnp.stack([ks_s, vs_s, kw_s, vw_s], axis=1), ((0, 0), (0, P_GATE - P_NEW - 4), (0, 0)))
    pack = jnp.concatenate(
        [q16.astype(F32), sel16, oc_s, szb_s.reshape(nb, N_HEADS, LANES), new_rows, gate_rows], axis=1)
    assert pack.shape[1] == P_ROWS
    mix_s, okw, ovw = _decode_sel(
        page_table, kv_t(cache_k_sel), kv_t(cache_v_sel), kv_t(cache_k_win), kv_t(cache_v_win),
        pack, onehot, cur)
    wo_slab = jnp.zeros((N_HEADS, N_KV, HEAD_DIM, D_MODEL), BF16)
    wo_heads = wo_b.reshape(N_HEADS, HEAD_DIM, D_MODEL)
    for e in range(N_HEADS):
        wo_slab = wo_slab.at[e, e // GQA].set(wo_heads[e])
    y_sample = _outproj(xs2, a_s, mix_s.reshape(nb, N_HEADS * LANES), wo_a,
                        wo_slab.reshape(N_HEADS * LANES, D_MODEL), fg, nb)
    y_sample = y_sample.reshape(nb, 1, D_MODEL)
    ss = lambda a: a.reshape(1, nb, 1, N_KV, HEAD_DIM)

    return (y_prompt, y_sample,
            st(kc_t), st(vc_t), st(ks_t), st(vs_t),
            st(kw_t[:, :, seq - keep_p:]), st(vw_t[:, :, seq - keep_p:]),
            ss(kc_s), ss(vc_s), ss(ks_s), ss(vs_s),
            st(okw), st(ovw),
            vn_s.reshape(1, nb, 1, D_A))
```
